```python
import math
import jax
import jax.numpy as jnp
from jax import lax
import numpy as np

D_MODEL = 1024
BATCH = 8
SEQ = 2048
DEPTH = 4

N_META = 16
LEAD = 128
HG_WIDTH = 512
HG_HEADS = 4
HG_DK = 128
HG_DV = 128
HG_CHUNK = 64
DA_WIDTH = 512
DA_HEADS = 4
DA_DQK = 64
DA_DV = 128
Q_BLOCK = 128
REL_BUCKETS = 32
REL_MAX_DIST = 128
D_FF_DENSE = 2816
N_EXPERTS = 8
TOP_K = 2
D_FF_EXPERT = 3584
MOE_BLOCK = 128
EPS = 1e-6
NEG = -1e30
LB_MAX = 0.999
MIX_WIDTH = HG_WIDTH + DA_WIDTH
W_IN_COLS = 4 * HG_WIDTH + 2 * (DA_HEADS * 2 * DA_DQK) + DA_HEADS * DA_DV
N_DENSE = (DEPTH + 1) // 2
N_MOE = DEPTH // 2

kernel_name = 'hybrid_hgrn2_diffattn_moe'


def rmsnorm(x, w):
    xf = x.astype(jnp.float32)
    y = xf * lax.rsqrt(jnp.mean(xf * xf, axis=-1, keepdims=True) + EPS)
    return (y * w.astype(jnp.float32)).astype(x.dtype)


def t5_causal_bucket(dist):
    n = jnp.maximum(dist, 0)
    max_exact = REL_BUCKETS // 2
    nf = jnp.maximum(n, max_exact).astype(jnp.float32)
    large = max_exact + (jnp.log(nf / max_exact) / math.log(REL_MAX_DIST / max_exact)
                         * (REL_BUCKETS - max_exact)).astype(jnp.int32)
    large = jnp.minimum(large, REL_BUCKETS - 1)
    return jnp.where(n < max_exact, n, large)


def attention_block_biases(rel_bias, pos, valid):
    tab = rel_bias.astype(jnp.float32)
    out = []
    for i in range(pos.shape[0] // Q_BLOCK):
        k_end = (i + 1) * Q_BLOCK
        qp = pos[i * Q_BLOCK:k_end]
        kp = pos[:k_end]
        bias = jnp.transpose(tab[t5_causal_bucket(qp[:, None] - kp[None, :])], (2, 0, 1))
        allowed = (kp[None, :] <= qp[:, None]) & valid[None, :k_end]
        out.append(jnp.where(allowed[None], bias, NEG))
    return out


def hgrn2_mixer(q, f_pre, v, g, lb, norm_w, valid):
    B, L, _ = q.shape
    n_chunks = L // HG_CHUNK

    def to_chunks(t):
        return t.reshape(B, n_chunks, HG_CHUNK, HG_HEADS, -1).transpose(1, 0, 3, 2, 4)

    qf = jax.nn.silu(q.astype(jnp.float32))
    log_f = jnp.logaddexp(jnp.log(lb), jnp.log1p(-lb) + jax.nn.log_sigmoid(f_pre.astype(jnp.float32)))
    k = -jnp.expm1(log_f)
    vmask = valid[None, :, None]
    log_f = jnp.where(vmask, log_f, 0.0)
    k = jnp.where(vmask, k, 0.0)
    qc, kc, vc, lc = (to_chunks(t) for t in (qf, k, v.astype(jnp.float32), log_f))
    bc = jnp.cumsum(lc, axis=-2)
    causal = jnp.tril(jnp.ones((HG_CHUNK, HG_CHUNK), dtype=bool))

    def step(S, inp):
        qi, ki, vi, bi = inp
        inter = jnp.einsum('bhtk,bhkv->bhtv', qi * jnp.exp(bi), S)
        rel = bi[..., :, None, :] - bi[..., None, :, :]
        decay = jnp.exp(jnp.where(causal[:, :, None], rel, -jnp.inf))
        scores = jnp.einsum('bhtk,bhsk,bhtsk->bhts', qi, ki, decay)
        intra = jnp.einsum('bhts,bhsv->bhtv', scores, vi)
        b_end = bi[..., -1:, :]
        S_new = jnp.exp(b_end[..., 0, :])[..., None] * S + jnp.einsum(
            'bhsk,bhsv->bhkv', ki * jnp.exp(b_end - bi), vi)
        return S_new, inter + intra

    S0 = jnp.zeros((B, HG_HEADS, HG_DK, HG_DV), jnp.float32)
    _, o = lax.scan(step, S0, (qc, kc, vc, bc))
    o = o.transpose(1, 0, 3, 2, 4).reshape(B, L, HG_HEADS, HG_DV)
    o = rmsnorm(o, norm_w).reshape(B, L, HG_WIDTH) * jax.nn.silu(g.astype(jnp.float32))
    return o.astype(q.dtype)


def diff_attention(q, k, v, lam, lam_init, subln_w, block_biases):
    B, L, _ = q.shape
    q = (q.reshape(B, L, DA_HEADS, 2, DA_DQK) * (DA_DQK ** -0.5)).transpose(0, 2, 3, 1, 4)
    k = k.reshape(B, L, DA_HEADS, 2, DA_DQK).transpose(0, 2, 3, 1, 4)
    vf = v.reshape(B, L, DA_HEADS, DA_DV).transpose(0, 2, 1, 3).astype(jnp.float32)
    outs = []
    for i, bias in enumerate(block_biases):
        k_end = (i + 1) * Q_BLOCK
        s = jnp.einsum('bhcqd,bhckd->bhcqk', q[:, :, :, i * Q_BLOCK:k_end], k[:, :, :, :k_end]).astype(jnp.float32)
        p = jax.nn.softmax(s + bias[None, :, None], axis=-1)
        a = p[:, :, 0] - lam * p[:, :, 1]
        outs.append(jnp.einsum('bhqk,bhkv->bhqv', a, vf[:, :, :k_end]))
    o = jnp.concatenate(outs, axis=2)
    o = rmsnorm(o, subln_w) * (1.0 - lam_init)
    return o.transpose(0, 2, 1, 3).reshape(B, L, DA_HEADS * DA_DV).astype(v.dtype)


def swiglu(h, w1, w3, w2):
    return (jax.nn.silu(h @ w1) * (h @ w3)) @ w2


def moe_swiglu(h, router, w1, w3, w2):
    N, D = h.shape
    logits = (h @ router).astype(jnp.float32)
    top_logit, top_idx = lax.top_k(logits, TOP_K)
    gate = jax.nn.softmax(top_logit, axis=-1)
    M = N * TOP_K
    flat_e = top_idx.reshape(M)
    flat_tok = jnp.repeat(jnp.arange(N, dtype=jnp.int32), TOP_K)
    flat_gate = gate.reshape(M)
    order = jnp.argsort(flat_e)
    e_sorted = flat_e[order]
    counts = jnp.bincount(flat_e, length=N_EXPERTS)
    padded = (counts + MOE_BLOCK - 1) // MOE_BLOCK * MOE_BLOCK
    pad_end = jnp.cumsum(padded)
    pad_start = pad_end - padded
    start = jnp.cumsum(counts) - counts
    dest = pad_start[e_sorted] + jnp.arange(M, dtype=jnp.int32) - start[e_sorted]
    n_blocks = -(-M // MOE_BLOCK) + N_EXPERTS
    P = n_blocks * MOE_BLOCK
    slot_tok = jnp.full((P,), N, jnp.int32).at[dest].set(flat_tok[order])
    slot_gate = jnp.zeros((P,), jnp.float32).at[dest].set(flat_gate[order])
    block_expert = jnp.minimum(
        jnp.searchsorted(pad_end, jnp.arange(n_blocks, dtype=jnp.int32) * MOE_BLOCK, side='right'),
        N_EXPERTS - 1)
    h_ext = jnp.concatenate([h, jnp.zeros((1, D), h.dtype)], axis=0)
    xb = h_ext[slot_tok].reshape(n_blocks, MOE_BLOCK, D)

    def expert_block(args):
        xblk, e = args
        return swiglu(xblk, w1[e], w3[e], w2[e])

    yb = lax.map(expert_block, (xb, block_expert)).reshape(P, D)
    y = yb.astype(jnp.float32) * slot_gate[:, None]
    out = jnp.zeros((N + 1, D), jnp.float32).at[slot_tok].add(y)[:N]
    return out.astype(h.dtype)


def setup_inputs(seed: int = 0) -> dict:
    key = jax.random.key(seed)
    ks = jax.random.split(key, 20)
    f32 = jnp.float32
    nrm = lambda k, shape, scale: jax.random.normal(k, shape, f32) * scale
    gain = lambda k, shape: 1.0 + 0.02 * jax.random.normal(k, shape, f32)
    return {
        'x': nrm(ks[0], (BATCH, SEQ, D_MODEL), 1.0),
        'meta': nrm(ks[1], (N_META, D_MODEL), 1.0),
        'rel_bias': nrm(ks[2], (REL_BUCKETS, DA_HEADS), 0.5),
        'norm_mix': gain(ks[3], (DEPTH, D_MODEL)),
        'w_in': nrm(ks[4], (DEPTH, D_MODEL, W_IN_COLS), D_MODEL ** -0.5),
        'hg_lb_logits': nrm(ks[5], (DEPTH, HG_WIDTH), 0.5),
        'hg_norm_w': gain(ks[6], (DEPTH, HG_DV)),
        'da_lambda': nrm(ks[7], (DEPTH, 4, DA_DQK), 0.1),
        'da_subln_w': gain(ks[8], (DEPTH, DA_DV)),
        'w_out': nrm(ks[9], (DEPTH, MIX_WIDTH, D_MODEL), MIX_WIDTH ** -0.5),
        'norm_ffn': gain(ks[10], (DEPTH, D_MODEL)),
        'dense_w1': nrm(ks[11], (N_DENSE, D_MODEL, D_FF_DENSE), D_MODEL ** -0.5),
        'dense_w3': nrm(ks[12], (N_DENSE, D_MODEL, D_FF_DENSE), D_MODEL ** -0.5),
        'dense_w2': nrm(ks[13], (N_DENSE, D_FF_DENSE, D_MODEL), D_FF_DENSE ** -0.5),
        'moe_router': nrm(ks[14], (N_MOE, D_MODEL, N_EXPERTS), D_MODEL ** -0.5),
        'moe_w1': nrm(ks[15], (N_MOE, N_EXPERTS, D_MODEL, D_FF_EXPERT), D_MODEL ** -0.5),
        'moe_w3': nrm(ks[16], (N_MOE, N_EXPERTS, D_MODEL, D_FF_EXPERT), D_MODEL ** -0.5),
        'moe_w2': nrm(ks[17], (N_MOE, N_EXPERTS, D_FF_EXPERT, D_MODEL), D_FF_EXPERT ** -0.5),
        'final_norm': gain(ks[18], (D_MODEL,)),
    }


def reference(x, meta, rel_bias, norm_mix, w_in, hg_lb_logits, hg_norm_w, da_lambda, da_subln_w,
              w_out, norm_ffn, dense_w1, dense_w3, dense_w2, moe_router, moe_w1, moe_w3, moe_w2,
              final_norm):
    B, S, D = x.shape
    L = LEAD + S
    h0 = jnp.concatenate([
        jnp.zeros((B, LEAD - N_META, D), x.dtype),
        jnp.broadcast_to(meta[None].astype(x.dtype), (B, N_META, D)),
        x], axis=1)
    pos = jnp.arange(L, dtype=jnp.int32) - (LEAD - N_META)
    valid = pos >= 0
    block_biases = attention_block_biases(rel_bias, pos, valid)

    lb_cum = jnp.cumsum(jax.nn.softmax(hg_lb_logits.astype(jnp.float32), axis=0), axis=0)
    lb_all = jnp.clip(lb_cum - lb_cum[0:1], 0.0, LB_MAX)

    c = np.cumsum([0, HG_WIDTH, HG_WIDTH, HG_WIDTH, HG_WIDTH,
                   DA_HEADS * 2 * DA_DQK, DA_HEADS * 2 * DA_DQK, DA_HEADS * DA_DV]).tolist()
    h = h0
    for l in range(DEPTH):
        u = rmsnorm(h, norm_mix[l])
        proj = u @ w_in[l]
        hq, hf, hv, hg, dq, dk, dv = (proj[..., c[j]:c[j + 1]] for j in range(7))
        o_hg = hgrn2_mixer(hq, hf, hv, hg, lb_all[l], hg_norm_w[l], valid)
        lam_init = 0.8 - 0.6 * math.exp(-0.3 * l)
        lv = da_lambda[l].astype(jnp.float32)
        lam = jnp.exp(jnp.sum(lv[0] * lv[1])) - jnp.exp(jnp.sum(lv[2] * lv[3])) + lam_init
        o_da = diff_attention(dq, dk, dv, lam, lam_init, da_subln_w[l], block_biases)
        h = h + (jnp.concatenate([o_hg, o_da], axis=-1) @ w_out[l]).astype(h.dtype)
        u = rmsnorm(h, norm_ffn[l])
        if l % 2 == 0:
            i = l // 2
            ff = swiglu(u, dense_w1[i], dense_w3[i], dense_w2[i])
        else:
            i = l // 2
            ff = moe_swiglu(u.reshape(B * L, D), moe_router[i], moe_w1[i], moe_w3[i], moe_w2[i]).reshape(B, L, D)
        h = h + ff.astype(h.dtype)
    out = rmsnorm(h, final_norm)
    return out[:, LEAD:]
```

```python
import functools
import math

import jax
import jax.numpy as jnp
import numpy as np
from jax import lax
from jax.experimental import pallas as pl
from jax.experimental.pallas import tpu as pltpu

D_MODEL = 1024
DEPTH = 4
N_META = 16
LEAD = 128
HG_WIDTH = 512
HG_HEADS = 4
HG_D = 128
HG_CHUNK = 64
DA_HEADS = 4
DA_DQK = 64
DA_DV = 128
Q_BLOCK = 128
REL_BUCKETS = 32
REL_MAX_DIST = 128
N_EXPERTS = 8
TOP_K = 2
D_FF_EXPERT = 3584
EPS = 1e-6
NEG = -1e30
LB_MAX = 0.999
HG_COLS = 4 * HG_WIDTH
DA_COLS = 3 * DA_HEADS * DA_DV
W_IN_COLS = HG_COLS + DA_COLS

LANES = 128
VMEM_LIMIT = 56 * 1024 * 1024

ROW_TILE = 256
HG_TILE = 128
MOE_TILE = 256
FF_CHUNK = 512

F32 = jnp.float32
BF16 = jnp.bfloat16


def _params(*sem):
    return pltpu.CompilerParams(dimension_semantics=sem, vmem_limit_bytes=VMEM_LIMIT)


def _dot(a, b):
    return jnp.dot(a, b, preferred_element_type=F32)


def _dot_nt(a, b):
    return lax.dot_general(a, b, (((1,), (1,)), ((), ())), preferred_element_type=F32)


def _dot_tn(a, b):
    return lax.dot_general(a, b, (((0,), (0,)), ((), ())), preferred_element_type=F32)


def _rms(x, gain):
    return x * lax.rsqrt(jnp.mean(x * x, axis=-1, keepdims=True) + EPS) * gain


def _silu(x):
    return x * (1.0 / (1.0 + jnp.exp(-x)))


def _mix_in_kernel(x_ref, g_ref, w_ref, hg_ref, da_ref):
    u = _rms(x_ref[...], g_ref[...]).astype(BF16)
    hg_ref[...] = _dot(u, w_ref[:, :HG_COLS])
    da_ref[...] = _dot(u, w_ref[:, HG_COLS:]).astype(BF16)


def _mix_in(h, gain, w):
    n = h.shape[0]
    return pl.pallas_call(
        _mix_in_kernel,
        grid=(n // ROW_TILE,),
        in_specs=[
            pl.BlockSpec((ROW_TILE, D_MODEL), lambda i: (i, 0)),
            pl.BlockSpec((1, D_MODEL), lambda i: (0, 0)),
            pl.BlockSpec((D_MODEL, W_IN_COLS), lambda i: (0, 0)),
        ],
        out_specs=[
            pl.BlockSpec((ROW_TILE, HG_COLS), lambda i: (i, 0)),
            pl.BlockSpec((ROW_TILE, DA_COLS), lambda i: (i, 0)),
        ],
        out_shape=[
            jax.ShapeDtypeStruct((n, HG_COLS), F32),
            jax.ShapeDtypeStruct((n, DA_COLS), BF16),
        ],
        compiler_params=_params("parallel"),
        name="mix_in",
    )(h, gain, w)


HG_LEVELS = (32, 16, 8, 4, 2, 1)
N_SUMS = len(HG_LEVELS) + 2


def _hgrn_consts():
    c = HG_CHUNK
    t = np.arange(c)[:, None]
    j = np.arange(c)[None, :]
    sums = np.zeros((N_SUMS, c, c), np.float32)
    masks = np.zeros((len(HG_LEVELS) + 1, c, c), np.float32)
    sums[0] = j <= t
    masks[0] = np.eye(c)
    for li, w in enumerate(HG_LEVELS, start=1):
        ref = (t // (2 * w)) * (2 * w) + w
        sums[li] = np.where(t >= ref, (j > ref) & (j <= t), (j > t) & (j <= ref))
        masks[li] = (t // (2 * w) == j // (2 * w)) & (t % (2 * w) >= w) & (j % (2 * w) < w)
    sums[N_SUMS - 1] = j > t
    return sums.reshape(N_SUMS * c, c), masks


_HG_SUMS, _HG_MASKS = _hgrn_consts()


def _hgrn_kernel(hg_ref, loga_ref, log1m_ref, nw_ref, sums_ref, masks_ref, o_ref, state_ref):
    c_idx = pl.program_id(1)

    @pl.when(c_idx == 0)
    def _():
        state_ref[...] = jnp.zeros_like(state_ref)

    C = HG_CHUNK
    sums = sums_ref[...]
    nw = nw_ref[...]
    for ch in range(HG_TILE // C):
        rows = slice(ch * C, (ch + 1) * C)
        row_idx = c_idx * HG_TILE + ch * C + lax.broadcasted_iota(jnp.int32, (C, 1), 0)
        valid = row_idx >= (LEAD - N_META)
        for hd in range(HG_HEADS):
            cols = slice(hd * HG_D, (hd + 1) * HG_D)
            q = hg_ref[rows, hd * HG_D:(hd + 1) * HG_D]
            f = hg_ref[rows, HG_WIDTH + hd * HG_D:HG_WIDTH + (hd + 1) * HG_D]
            v = hg_ref[rows, 2 * HG_WIDTH + hd * HG_D:2 * HG_WIDTH + (hd + 1) * HG_D]
            g = hg_ref[rows, 3 * HG_WIDTH + hd * HG_D:3 * HG_WIDTH + (hd + 1) * HG_D]
            loga = loga_ref[:, cols]
            log1m = log1m_ref[:, cols]

            qf = _silu(q)
            ls = jnp.minimum(f, 0.0) - jnp.log1p(jnp.exp(-jnp.abs(f)))
            cc = log1m + ls
            lf = jnp.maximum(loga, cc) + jnp.log1p(jnp.exp(-jnp.abs(loga - cc)))
            kk = jnp.exp(cc - f)
            lf = jnp.where(valid, lf, 0.0)
            kk = jnp.where(valid, kk, 0.0)

            lf_hi = lf.astype(BF16)
            lf_lo = (lf - lf_hi.astype(F32)).astype(BF16)
            e = jnp.exp(_dot(sums, lf_hi) + _dot(sums, lf_lo))

            e_b = e[0:C]
            st = state_ref[hd]
            inter = _dot_nt((qf * e_b).astype(BF16), st.astype(BF16))
            scores = masks_ref[0] * _dot_nt(qf.astype(BF16), kk.astype(BF16))
            for li in range(1, len(HG_LEVELS) + 1):
                e_l = e[li * C:(li + 1) * C]
                scores += masks_ref[li] * _dot_nt((qf * e_l).astype(BF16), (kk * e_l).astype(BF16))
            vb = v.astype(BF16)
            o = inter + _dot(scores.astype(BF16), vb)

            e_end = e[(N_SUMS - 1) * C:N_SUMS * C]
            state_ref[hd] = st * e_b[C - 1:C, :] + _dot_tn(vb, (kk * e_end).astype(BF16))

            o = _rms(o, nw) * _silu(g)
            o_ref[rows, cols] = o.astype(o_ref.dtype)


def _hgrn(hg, loga, log1m, norm_w, batch, length):
    hg3 = hg.reshape(batch, length, HG_COLS)
    out = pl.pallas_call(
        _hgrn_kernel,
        grid=(batch, length // HG_TILE),
        in_specs=[
            pl.BlockSpec((None, HG_TILE, HG_COLS), lambda b, c: (b, c, 0)),
            pl.BlockSpec((1, HG_WIDTH), lambda b, c: (0, 0)),
            pl.BlockSpec((1, HG_WIDTH), lambda b, c: (0, 0)),
            pl.BlockSpec((1, HG_D), lambda b, c: (0, 0)),
            pl.BlockSpec(_HG_SUMS.shape, lambda b, c: (0, 0)),
            pl.BlockSpec(_HG_MASKS.shape, lambda b, c: (0, 0, 0)),
        ],
        out_specs=pl.BlockSpec((None, HG_TILE, HG_WIDTH), lambda b, c: (b, c, 0)),
        out_shape=jax.ShapeDtypeStruct((batch, length, HG_WIDTH), BF16),
        scratch_shapes=[pltpu.VMEM((HG_HEADS, HG_D, HG_D), F32)],
        compiler_params=_params("parallel", "arbitrary"),
        name="hgrn2",
    )(hg3, loga, log1m, norm_w, jnp.asarray(_HG_SUMS, BF16), jnp.asarray(_HG_MASKS, F32))
    return out.reshape(batch * length, HG_WIDTH)


def _attn_kernel(q_ref, k_ref, v_ref, toe_ref, far_ref, cst_ref, w_ref, o_ref, *, n_blocks):
    lam = cst_ref[0:1, :]
    post = cst_ref[1:2, :]
    far_bias = far_ref[0:1, 0:1]
    lane = lax.broadcasted_iota(jnp.int32, (Q_BLOCK, Q_BLOCK), 1)
    first_half = lane < DA_DQK
    key_ok0 = lane >= (LEAD - N_META)
    scale = DA_DQK ** -0.5

    def block_bias(kind, kb):
        bias = toe_ref[kind]
        if kb == 0:
            bias = jnp.where(key_ok0, bias, NEG)
        return bias

    for i in range(n_blocks):
        qi = q_ref[i * Q_BLOCK:(i + 1) * Q_BLOCK, :]
        zero = jnp.zeros_like(qi)
        q_maps = (jnp.where(first_half, qi, zero), jnp.where(first_half, zero, qi))
        n_far = (i - 1) * Q_BLOCK if i >= 2 else 0
        if n_far:
            far_lane = lax.broadcasted_iota(jnp.int32, (1, n_far), 1)
            far_row = jnp.where(far_lane >= (LEAD - N_META), far_bias, NEG)
        probs = []
        for qm in q_maps:
            pieces = []
            if n_far:
                pieces.append(_dot_nt(qm, k_ref[0:n_far, :]) * scale + far_row)
            if i >= 1:
                kb = i - 1
                pieces.append(_dot_nt(qm, k_ref[kb * Q_BLOCK:(kb + 1) * Q_BLOCK, :]) * scale
                              + block_bias(1, kb))
            pieces.append(_dot_nt(qm, k_ref[i * Q_BLOCK:(i + 1) * Q_BLOCK, :]) * scale
                          + block_bias(0, i))
            m = pieces[0].max(axis=-1, keepdims=True)
            for s in pieces[1:]:
                m = jnp.maximum(m, s.max(axis=-1, keepdims=True))
            pieces = [jnp.exp(s - m) for s in pieces]
            den = pieces[0].sum(axis=-1, keepdims=True)
            for p in pieces[1:]:
                den = den + p.sum(axis=-1, keepdims=True)
            probs.append((pieces, 1.0 / den))
        (p0, r0), (p1, r1) = probs
        r1 = r1 * lam[:, 0:1]
        starts = ([0] if n_far else []) + ([(i - 1) * Q_BLOCK] if i >= 1 else []) + [i * Q_BLOCK]
        o = None
        for a0, a1, ks in zip(p0, p1, starts):
            a = (a0 * r0 - a1 * r1).astype(BF16)
            part = _dot(a, v_ref[ks:ks + a.shape[1], :])
            o = part if o is None else o + part
        o = _rms(o, w_ref[...]) * post
        o_ref[i * Q_BLOCK:(i + 1) * Q_BLOCK, :] = o.astype(o_ref.dtype)


def _attn(da, toe, far, cst, subln_w, batch, length):
    da3 = da.reshape(batch, length, DA_COLS)
    hw = DA_HEADS
    out = pl.pallas_call(
        functools.partial(_attn_kernel, n_blocks=length // Q_BLOCK),
        grid=(batch, DA_HEADS),
        in_specs=[
            pl.BlockSpec((None, length, DA_DV), lambda b, h: (b, 0, h)),
            pl.BlockSpec((None, length, DA_DV), lambda b, h: (b, 0, hw + h)),
            pl.BlockSpec((None, length, DA_DV), lambda b, h: (b, 0, 2 * hw + h)),
            pl.BlockSpec((None, 2, Q_BLOCK, Q_BLOCK), lambda b, h: (h, 0, 0, 0)),
            pl.BlockSpec((None, 1, LANES), lambda b, h: (h, 0, 0)),
            pl.BlockSpec((8, LANES), lambda b, h: (0, 0)),
            pl.BlockSpec((1, DA_DV), lambda b, h: (0, 0)),
        ],
        out_specs=pl.BlockSpec((None, length, DA_DV), lambda b, h: (b, 0, h)),
        out_shape=jax.ShapeDtypeStruct((batch, length, DA_HEADS * DA_DV), BF16),
        compiler_params=_params("parallel", "parallel"),
        name="diff_attn",
    )(da3, da3, da3, toe, far, cst, subln_w)
    return out.reshape(batch * length, DA_HEADS * DA_DV)


def _t5_bucket(dist):
    n = jnp.maximum(dist, 0)
    max_exact = REL_BUCKETS // 2
    nf = jnp.maximum(n, max_exact).astype(F32)
    large = max_exact + (jnp.log(nf / max_exact) / math.log(REL_MAX_DIST / max_exact)
                         * (REL_BUCKETS - max_exact)).astype(jnp.int32)
    large = jnp.minimum(large, REL_BUCKETS - 1)
    return jnp.where(n < max_exact, n, large)


def _attn_bias_tables(rel_bias):
    tab = rel_bias.astype(F32)
    qi = jnp.arange(Q_BLOCK, dtype=jnp.int32)[:, None]
    ki = jnp.arange(Q_BLOCK, dtype=jnp.int32)[None, :]
    diag = jnp.transpose(tab[_t5_bucket(qi - ki)], (2, 0, 1))
    diag = jnp.where((ki <= qi)[None], diag, NEG)
    prev = jnp.transpose(tab[_t5_bucket(qi - ki + Q_BLOCK)], (2, 0, 1))
    toe = jnp.stack([diag, prev], axis=1)
    far = jnp.broadcast_to(tab[REL_BUCKETS - 1][:, None, None], (DA_HEADS, 1, LANES))
    return toe, far


def _out_proj_kernel(h_ref, ohg_ref, oda_ref, wo_ref, g_ref, *rest, with_router):
    if with_router:
        router_ref, hn_ref, u_ref, lg_ref = rest
    else:
        hn_ref, u_ref = rest
    hn = (h_ref[...] + _dot(ohg_ref[...], wo_ref[:HG_WIDTH, :])
          + _dot(oda_ref[...], wo_ref[HG_WIDTH:, :]))
    hn_ref[...] = hn
    u = _rms(hn, g_ref[...])
    u_ref[...] = u.astype(u_ref.dtype)
    if with_router:
        lg_ref[...] = jnp.dot(u, router_ref[...], precision=lax.Precision.HIGHEST,
                              preferred_element_type=F32)


def _out_proj(h, o_hg, o_da, wo, gain, router=None):
    n = h.shape[0]
    with_router = router is not None
    row = lambda i: (i, 0)
    full = lambda i: (0, 0)
    in_specs = [
        pl.BlockSpec((ROW_TILE, D_MODEL), row),
        pl.BlockSpec((ROW_TILE, HG_WIDTH), row),
        pl.BlockSpec((ROW_TILE, DA_HEADS * DA_DV), row),
        pl.BlockSpec(wo.shape, full),
        pl.BlockSpec((1, D_MODEL), full),
    ]
    out_specs = [pl.BlockSpec((ROW_TILE, D_MODEL), row), pl.BlockSpec((ROW_TILE, D_MODEL), row)]
    out_shape = [jax.ShapeDtypeStruct((n, D_MODEL), F32),
                 jax.ShapeDtypeStruct((n, D_MODEL), F32 if with_router else BF16)]
    args = [h, o_hg, o_da, wo, gain]
    if with_router:
        in_specs.append(pl.BlockSpec(router.shape, full))
        out_specs.append(pl.BlockSpec((ROW_TILE, LANES), row))
        out_shape.append(jax.ShapeDtypeStruct((n, LANES), F32))
        args.append(router)
    return pl.pallas_call(
        functools.partial(_out_proj_kernel, with_router=with_router),
        grid=(n // ROW_TILE,),
        in_specs=in_specs,
        out_specs=out_specs,
        out_shape=out_shape,
        compiler_params=_params("parallel"),
        name="out_proj_router" if with_router else "out_proj",
    )(*args)


def _dense_ffn_kernel(h_ref, u_ref, w1_ref, w3_ref, w2_ref, o_ref):
    u = u_ref[...]
    a = _dot(u, w1_ref[...])
    act = (_silu(a) * _dot(u, w3_ref[...])).astype(BF16)
    o_ref[...] = h_ref[...] + _dot(act, w2_ref[...])


def _dense_ffn(h, u, w1, w3, w2):
    n = h.shape[0]
    row = lambda i: (i, 0)
    full = lambda i: (0, 0)
    return pl.pallas_call(
        _dense_ffn_kernel,
        grid=(n // ROW_TILE,),
        in_specs=[
            pl.BlockSpec((ROW_TILE, D_MODEL), row),
            pl.BlockSpec((ROW_TILE, D_MODEL), row),
            pl.BlockSpec(w1.shape, full),
            pl.BlockSpec(w3.shape, full),
            pl.BlockSpec(w2.shape, full),
        ],
        out_specs=pl.BlockSpec((ROW_TILE, D_MODEL), row),
        out_shape=jax.ShapeDtypeStruct((n, D_MODEL), F32),
        compiler_params=_params("parallel"),
        name="dense_ffn",
    )(h, u, w1, w3, w2)


def _row_copy(src_ref, src_row, dst_ref, dst_row, sem):
    return pltpu.make_async_copy(src_ref.at[pl.ds(src_row, 1), :], dst_ref.at[pl.ds(dst_row, 1), :], sem)


def _dispatch_kernel(dest_ref, u_ref, xs_in_ref, xs_ref, sem):
    del xs_in_ref
    base = pl.program_id(0) * ROW_TILE

    def issue(r, carry):
        for k in range(TOP_K):
            _row_copy(u_ref, r, xs_ref, dest_ref[(base + r) * TOP_K + k], sem).start()
        return carry

    lax.fori_loop(0, ROW_TILE, issue, 0)

    def drain(r, carry):
        for k in range(TOP_K):
            _row_copy(u_ref, r, xs_ref, dest_ref[(base + r) * TOP_K + k], sem).wait()
        return carry

    lax.fori_loop(0, ROW_TILE, drain, 0)


def _dispatch(dest, u, n_slots):
    n = u.shape[0]
    zeros = jnp.zeros((n_slots, D_MODEL), u.dtype)
    return pl.pallas_call(
        _dispatch_kernel,
        grid_spec=pltpu.PrefetchScalarGridSpec(
            num_scalar_prefetch=1,
            grid=(n // ROW_TILE,),
            in_specs=[
                pl.BlockSpec((ROW_TILE, D_MODEL), lambda i, d: (i, 0)),
                pl.BlockSpec(memory_space=pl.ANY),
            ],
            out_specs=pl.BlockSpec(memory_space=pl.ANY),
            scratch_shapes=[pltpu.SemaphoreType.DMA(())],
        ),
        out_shape=jax.ShapeDtypeStruct((n_slots, D_MODEL), u.dtype),
        input_output_aliases={2: 0},
        compiler_params=_params("arbitrary"),
        name="moe_dispatch",
    )(dest, u, zeros)


def _expert_kernel(be_ref, nused_ref, xs_ref, w1_ref, w3_ref, w2_ref, y_ref):
    del be_ref
    blk = pl.program_id(0)

    @pl.when(blk < nused_ref[0])
    def _():
        x = xs_ref[...].astype(BF16)
        acc = jnp.zeros((MOE_TILE, D_MODEL), F32)
        for c0 in range(0, D_FF_EXPERT, FF_CHUNK):
            a = _dot(x, w1_ref[:, c0:c0 + FF_CHUNK])
            act = (_silu(a) * _dot(x, w3_ref[:, c0:c0 + FF_CHUNK])).astype(BF16)
            acc = acc + _dot(act, w2_ref[c0:c0 + FF_CHUNK, :])
        y_ref[...] = acc

    @pl.when(blk >= nused_ref[0])
    def _():
        y_ref[...] = jnp.zeros_like(y_ref)


def _experts(block_expert, n_used, xs, w1, w3, w2):
    n_slots = xs.shape[0]
    wmap = lambda i, be, nu: (be[i], 0, 0)
    return pl.pallas_call(
        _expert_kernel,
        grid_spec=pltpu.PrefetchScalarGridSpec(
            num_scalar_prefetch=2,
            grid=(n_slots // MOE_TILE,),
            in_specs=[
                pl.BlockSpec((MOE_TILE, D_MODEL), lambda i, be, nu: (i, 0)),
                pl.BlockSpec((None, D_MODEL, D_FF_EXPERT), wmap),
                pl.BlockSpec((None, D_MODEL, D_FF_EXPERT), wmap),
                pl.BlockSpec((None, D_FF_EXPERT, D_MODEL), wmap),
            ],
            out_specs=pl.BlockSpec((MOE_TILE, D_MODEL), lambda i, be, nu: (i, 0)),
        ),
        out_shape=jax.ShapeDtypeStruct((n_slots, D_MODEL), F32),
        compiler_params=_params("arbitrary"),
        name="moe_experts",
    )(block_expert, n_used, xs, w1, w3, w2)


def _combine_kernel(dest_ref, h_ref, gate_ref, y_ref, o_ref, buf_ref, sem):
    base = pl.program_id(0) * ROW_TILE

    def issue(r, carry):
        for k in range(TOP_K):
            _row_copy(y_ref, dest_ref[(base + r) * TOP_K + k], buf_ref.at[k], r, sem).start()
        return carry

    lax.fori_loop(0, ROW_TILE, issue, 0)

    def drain(r, carry):
        for k in range(TOP_K):
            _row_copy(y_ref, dest_ref[(base + r) * TOP_K + k], buf_ref.at[k], r, sem).wait()
        return carry

    lax.fori_loop(0, ROW_TILE, drain, 0)
    gate = gate_ref[...]
    o_ref[...] = h_ref[...] + gate[:, 0:1] * buf_ref[0] + gate[:, 1:2] * buf_ref[1]


def _combine(dest, h, gate, y):
    n = h.shape[0]
    return pl.pallas_call(
        _combine_kernel,
        grid_spec=pltpu.PrefetchScalarGridSpec(
            num_scalar_prefetch=1,
            grid=(n // ROW_TILE,),
            in_specs=[
                pl.BlockSpec((ROW_TILE, D_MODEL), lambda i, d: (i, 0)),
                pl.BlockSpec((ROW_TILE, TOP_K), lambda i, d: (i, 0)),
                pl.BlockSpec(memory_space=pl.ANY),
            ],
            out_specs=pl.BlockSpec((ROW_TILE, D_MODEL), lambda i, d: (i, 0)),
            scratch_shapes=[pltpu.VMEM((TOP_K, ROW_TILE, D_MODEL), F32),
                            pltpu.SemaphoreType.DMA(())],
        ),
        out_shape=jax.ShapeDtypeStruct((n, D_MODEL), F32),
        compiler_params=_params("arbitrary"),
        name="moe_combine",
    )(dest, h, gate, y)


def _route(logits, n_slots):
    n = logits.shape[0]
    eid = lax.broadcasted_iota(jnp.int32, logits.shape, 1)
    i1 = jnp.argmax(logits, axis=-1).astype(jnp.int32)
    l1 = jnp.max(logits, axis=-1)
    rest = jnp.where(eid == i1[:, None], -jnp.inf, logits)
    i2 = jnp.argmax(rest, axis=-1).astype(jnp.int32)
    l2 = jnp.max(rest, axis=-1)
    e2 = jnp.exp(l2 - l1)
    gate = jnp.stack([1.0 / (1.0 + e2), e2 / (1.0 + e2)], axis=-1)

    flat_e = jnp.stack([i1, i2], axis=-1).reshape(n * TOP_K)
    onehot = (flat_e[:, None] == jnp.arange(N_EXPERTS, dtype=jnp.int32)[None, :]).astype(jnp.int32)
    csum = jnp.cumsum(onehot, axis=0)
    rank = jnp.sum(csum * onehot, axis=-1) - 1
    counts = csum[-1]
    padded = (counts + MOE_TILE - 1) // MOE_TILE * MOE_TILE
    pad_end = jnp.cumsum(padded)
    pad_start = pad_end - padded
    dest = (pad_start[flat_e] + rank).astype(jnp.int32)
    n_blocks = n_slots // MOE_TILE
    block_expert = jnp.minimum(
        jnp.searchsorted(pad_end, jnp.arange(n_blocks, dtype=jnp.int32) * MOE_TILE, side="right"),
        N_EXPERTS - 1).astype(jnp.int32)
    n_used = (pad_end[-1:] // MOE_TILE).astype(jnp.int32)
    return dest, gate, block_expert, n_used


def _moe_ffn(h, u, logits, w1, w3, w2):
    n = h.shape[0]
    n_slots = (n * TOP_K // MOE_TILE + N_EXPERTS) * MOE_TILE
    dest, gate, block_expert, n_used = _route(logits, n_slots)
    xs = _dispatch(dest, u, n_slots)
    y = _experts(block_expert, n_used, xs, w1, w3, w2)
    return _combine(dest, h, gate, y)


def _final_kernel(h_ref, g_ref, o_ref):
    o_ref[...] = _rms(h_ref[...], g_ref[...])


def _final_norm(h3, gain, seq):
    batch = h3.shape[0]
    lead_blocks = LEAD // Q_BLOCK
    return pl.pallas_call(
        _final_kernel,
        grid=(batch, seq // Q_BLOCK),
        in_specs=[
            pl.BlockSpec((None, Q_BLOCK, D_MODEL), lambda b, j: (b, j + lead_blocks, 0)),
            pl.BlockSpec((1, D_MODEL), lambda b, j: (0, 0)),
        ],
        out_specs=pl.BlockSpec((None, Q_BLOCK, D_MODEL), lambda b, j: (b, j, 0)),
        out_shape=jax.ShapeDtypeStruct((batch, seq, D_MODEL), F32),
        compiler_params=_params("parallel", "parallel"),
        name="final_norm",
    )(h3, gain)


def kernel(x, meta, rel_bias, norm_mix, w_in, hg_lb_logits, hg_norm_w, da_lambda, da_subln_w, w_out, norm_ffn, dense_w1, dense_w3, dense_w2, moe_router, moe_w1, moe_w3, moe_w2, final_norm):
    batch, seq, d = x.shape
    length = LEAD + seq
    h = jnp.concatenate([
        jnp.zeros((batch, LEAD - N_META, d), x.dtype),
        jnp.broadcast_to(meta[None].astype(x.dtype), (batch, N_META, d)),
        x], axis=1).reshape(batch * length, d)

    toe, far = _attn_bias_tables(rel_bias)
    lb_cum = jnp.cumsum(jax.nn.softmax(hg_lb_logits.astype(F32), axis=0), axis=0)
    lb_all = jnp.clip(lb_cum - lb_cum[0:1], 0.0, LB_MAX)
    log_lb = jnp.log(lb_all)
    log_1m_lb = jnp.log1p(-lb_all)

    for l in range(DEPTH):
        hg, da = _mix_in(h, norm_mix[l][None], w_in[l].astype(BF16))
        o_hg = _hgrn(hg, log_lb[l][None], log_1m_lb[l][None], hg_norm_w[l][None], batch, length)
        lam_init = 0.8 - 0.6 * math.exp(-0.3 * l)
        lv = da_lambda[l].astype(F32)
        lam = jnp.exp(jnp.sum(lv[0] * lv[1])) - jnp.exp(jnp.sum(lv[2] * lv[3])) + lam_init
        cst = jnp.zeros((8, LANES), F32).at[0].set(lam).at[1].set(1.0 - lam_init)
        o_da = _attn(da, toe, far, cst, da_subln_w[l][None], batch, length)
        wo = w_out[l].astype(BF16)
        i = l // 2
        if l % 2 == 0:
            hn, u = _out_proj(h, o_hg, o_da, wo, norm_ffn[l][None])
            h = _dense_ffn(hn, u, dense_w1[i].astype(BF16), dense_w3[i].astype(BF16),
                           dense_w2[i].astype(BF16))
        else:
            router = jnp.zeros((d, LANES), F32).at[:, :N_EXPERTS].set(moe_router[i].astype(F32))
            hn, u, lg = _out_proj(h, o_hg, o_da, wo, norm_ffn[l][None], router)
            h = _moe_ffn(hn, u, lg[:, :N_EXPERTS], moe_w1[i].astype(BF16),
                         moe_w3[i].astype(BF16), moe_w2[i].astype(BF16))
    return _final_norm(h.reshape(batch, length, d), final_norm[None], seq)
```

```python
import functools
import math

import jax
import jax.numpy as jnp
import numpy as np
from jax import lax
from jax.experimental import pallas as pl
from jax.experimental.pallas import tpu as pltpu

D_MODEL = 1024
DEPTH = 4
N_META = 16
LEAD = 128
HG_WIDTH = 512
HG_HEADS = 4
HG_D = 128
HG_CHUNK = 64
DA_HEADS = 4
DA_DQK = 64
DA_DV = 128
Q_BLOCK = 128
REL_BUCKETS = 32
REL_MAX_DIST = 128
N_EXPERTS = 8
TOP_K = 2
D_FF_EXPERT = 3584
EPS = 1e-6
NEG = -1e30
LB_MAX = 0.999
HG_COLS = 4 * HG_WIDTH
DA_COLS = 3 * DA_HEADS * DA_DV
W_IN_COLS = HG_COLS + DA_COLS

LANES = 128
VMEM_LIMIT = 56 * 1024 * 1024

ROW_TILE = 256
HG_TILE = 128
MOE_TILE = 256
FF_CHUNK = 512

F32 = jnp.float32
BF16 = jnp.bfloat16


def _params(*sem):
    return pltpu.CompilerParams(dimension_semantics=sem, vmem_limit_bytes=VMEM_LIMIT)


def _dot(a, b):
    return jnp.dot(a, b, preferred_element_type=F32)


def _dot_nt(a, b):
    return lax.dot_general(a, b, (((1,), (1,)), ((), ())), preferred_element_type=F32)


def _dot_tn(a, b):
    return lax.dot_general(a, b, (((0,), (0,)), ((), ())), preferred_element_type=F32)


def _rms(x, gain):
    return x * lax.rsqrt(jnp.mean(x * x, axis=-1, keepdims=True) + EPS) * gain


def _silu(x):
    return x * (0.5 * jnp.tanh(0.5 * x) + 0.5)


def _mix_in_kernel(x_ref, g_ref, w_ref, hg_ref, da_ref):
    u = _rms(x_ref[...], g_ref[...]).astype(BF16)
    hg_ref[...] = _dot(u, w_ref[:, :HG_COLS])
    da_ref[...] = _dot(u, w_ref[:, HG_COLS:]).astype(BF16)


def _mix_in(h, gain, w, layer):
    n = h.shape[0]
    return pl.pallas_call(
        _mix_in_kernel,
        grid=(n // ROW_TILE,),
        in_specs=[
            pl.BlockSpec((ROW_TILE, D_MODEL), lambda i: (i, 0)),
            pl.BlockSpec((1, D_MODEL), lambda i: (0, 0)),
            pl.BlockSpec((None, D_MODEL, W_IN_COLS), lambda i: (layer, 0, 0)),
        ],
        out_specs=[
            pl.BlockSpec((ROW_TILE, HG_COLS), lambda i: (i, 0)),
            pl.BlockSpec((ROW_TILE, DA_COLS), lambda i: (i, 0)),
        ],
        out_shape=[
            jax.ShapeDtypeStruct((n, HG_COLS), F32),
            jax.ShapeDtypeStruct((n, DA_COLS), BF16),
        ],
        compiler_params=_params("parallel"),
        name="mix_in",
    )(h, gain, w)


HG_LEVELS = (32, 16, 8, 4, 2, 1)
N_SUMS = len(HG_LEVELS) + 2


def _hgrn_consts():
    c = HG_CHUNK
    t = np.arange(c)[:, None]
    j = np.arange(c)[None, :]
    sums = np.zeros((N_SUMS, c, c), np.float32)
    masks = np.zeros((len(HG_LEVELS) + 1, c, c), np.float32)
    sums[0] = j <= t
    masks[0] = np.eye(c)
    for li, w in enumerate(HG_LEVELS, start=1):
        ref = (t // (2 * w)) * (2 * w) + w
        sums[li] = np.where(t >= ref, (j > ref) & (j <= t), (j > t) & (j <= ref))
        masks[li] = (t // (2 * w) == j // (2 * w)) & (t % (2 * w) >= w) & (j % (2 * w) < w)
    sums[N_SUMS - 1] = j > t
    return sums.reshape(N_SUMS * c, c), masks


_HG_SUMS, _HG_MASKS = _hgrn_consts()


def _hgrn_kernel(hg_ref, loga_ref, log1m_ref, nw_ref, sums_ref, masks_ref, o_ref, state_ref):
    c_idx = pl.program_id(1)

    @pl.when(c_idx == 0)
    def _():
        state_ref[...] = jnp.zeros_like(state_ref)

    C = HG_CHUNK
    sums = sums_ref[...]
    nw = nw_ref[...]
    for ch in range(HG_TILE // C):
        rows = slice(ch * C, (ch + 1) * C)
        row_idx = c_idx * HG_TILE + ch * C + lax.broadcasted_iota(jnp.int32, (C, 1), 0)
        valid = row_idx >= (LEAD - N_META)
        for hd in range(HG_HEADS):
            cols = slice(hd * HG_D, (hd + 1) * HG_D)
            q = hg_ref[rows, hd * HG_D:(hd + 1) * HG_D]
            f = hg_ref[rows, HG_WIDTH + hd * HG_D:HG_WIDTH + (hd + 1) * HG_D]
            v = hg_ref[rows, 2 * HG_WIDTH + hd * HG_D:2 * HG_WIDTH + (hd + 1) * HG_D]
            g = hg_ref[rows, 3 * HG_WIDTH + hd * HG_D:3 * HG_WIDTH + (hd + 1) * HG_D]
            loga = loga_ref[:, cols]
            log1m = log1m_ref[:, cols]

            qf = _silu(q)
            ls = jnp.minimum(f, 0.0) - jnp.log1p(jnp.exp(-jnp.abs(f)))
            cc = log1m + ls
            lf = jnp.maximum(loga, cc) + jnp.log1p(jnp.exp(-jnp.abs(loga - cc)))
            kk = jnp.exp(cc - f)
            lf = jnp.where(valid, lf, 0.0)
            kk = jnp.where(valid, kk, 0.0)

            lf_hi = lf.astype(BF16)
            lf_lo = (lf - lf_hi.astype(F32)).astype(BF16)
            e = jnp.exp(_dot(sums, lf_hi) + _dot(sums, lf_lo))

            e_b = e[0:C]
            st = state_ref[hd]
            inter = _dot_nt((qf * e_b).astype(BF16), st.astype(BF16))
            scores = masks_ref[0] * _dot_nt(qf.astype(BF16), kk.astype(BF16))
            for li in range(1, len(HG_LEVELS) + 1):
                e_l = e[li * C:(li + 1) * C]
                scores += masks_ref[li] * _dot_nt((qf * e_l).astype(BF16), (kk * e_l).astype(BF16))
            vb = v.astype(BF16)
            o = inter + _dot(scores.astype(BF16), vb)

            e_end = e[(N_SUMS - 1) * C:N_SUMS * C]
            state_ref[hd] = st * e_b[C - 1:C, :] + _dot_tn(vb, (kk * e_end).astype(BF16))

            o = _rms(o, nw) * _silu(g)
            o_ref[rows, cols] = o.astype(o_ref.dtype)


def _hgrn(hg, loga, log1m, norm_w, batch, length):
    hg3 = hg.reshape(batch, length, HG_COLS)
    out = pl.pallas_call(
        _hgrn_kernel,
        grid=(batch, length // HG_TILE),
        in_specs=[
            pl.BlockSpec((None, HG_TILE, HG_COLS), lambda b, c: (b, c, 0)),
            pl.BlockSpec((1, HG_WIDTH), lambda b, c: (0, 0)),
            pl.BlockSpec((1, HG_WIDTH), lambda b, c: (0, 0)),
            pl.BlockSpec((1, HG_D), lambda b, c: (0, 0)),
            pl.BlockSpec(_HG_SUMS.shape, lambda b, c: (0, 0)),
            pl.BlockSpec(_HG_MASKS.shape, lambda b, c: (0, 0, 0)),
        ],
        out_specs=pl.BlockSpec((None, HG_TILE, HG_WIDTH), lambda b, c: (b, c, 0)),
        out_shape=jax.ShapeDtypeStruct((batch, length, HG_WIDTH), BF16),
        scratch_shapes=[pltpu.VMEM((HG_HEADS, HG_D, HG_D), F32)],
        compiler_params=_params("parallel", "arbitrary"),
        name="hgrn2",
    )(hg3, loga, log1m, norm_w, jnp.asarray(_HG_SUMS, BF16), jnp.asarray(_HG_MASKS, F32))
    return out.reshape(batch * length, HG_WIDTH)


def _attn_kernel(q_ref, k_ref, v_ref, toe_ref, far_ref, cst_ref, w_ref, o_ref, *, n_blocks):
    lam = cst_ref[0:1, :]
    post = cst_ref[1:2, :]
    far_bias = far_ref[0:1, 0:1]
    lane = lax.broadcasted_iota(jnp.int32, (Q_BLOCK, Q_BLOCK), 1)
    first_half = lane < DA_DQK
    key_ok0 = lane >= (LEAD - N_META)
    scale = DA_DQK ** -0.5

    def block_bias(kind, kb):
        bias = toe_ref[kind]
        if kb == 0:
            bias = jnp.where(key_ok0, bias, NEG)
        return bias

    for i in range(n_blocks):
        qi = q_ref[i * Q_BLOCK:(i + 1) * Q_BLOCK, :] * scale
        zero = jnp.zeros_like(qi)
        q_maps = (jnp.where(first_half, qi, zero), jnp.where(first_half, zero, qi))
        n_far = (i - 1) * Q_BLOCK if i >= 2 else 0
        if n_far:
            far_lane = lax.broadcasted_iota(jnp.int32, (1, n_far), 1)
            far_row = jnp.where(far_lane >= (LEAD - N_META), far_bias, NEG)
        probs = []
        for qm in q_maps:
            pieces = []
            if n_far:
                pieces.append(_dot_nt(qm, k_ref[0:n_far, :]) + far_row)
            if i >= 1:
                kb = i - 1
                pieces.append(_dot_nt(qm, k_ref[kb * Q_BLOCK:(kb + 1) * Q_BLOCK, :])
                              + block_bias(1, kb))
            pieces.append(_dot_nt(qm, k_ref[i * Q_BLOCK:(i + 1) * Q_BLOCK, :])
                          + block_bias(0, i))
            m = pieces[0].max(axis=-1, keepdims=True)
            for s in pieces[1:]:
                m = jnp.maximum(m, s.max(axis=-1, keepdims=True))
            pieces = [jnp.exp(s - m) for s in pieces]
            den = pieces[0].sum(axis=-1, keepdims=True)
            for p in pieces[1:]:
                den = den + p.sum(axis=-1, keepdims=True)
            probs.append((pieces, 1.0 / den))
        (p0, r0), (p1, r1) = probs
        r1 = r1 * lam[:, 0:1]
        starts = ([0] if n_far else []) + ([(i - 1) * Q_BLOCK] if i >= 1 else []) + [i * Q_BLOCK]
        o = None
        for a0, a1, ks in zip(p0, p1, starts):
            a = (a0 * r0 - a1 * r1).astype(BF16)
            part = _dot(a, v_ref[ks:ks + a.shape[1], :])
            o = part if o is None else o + part
        o = _rms(o, w_ref[...]) * post
        o_ref[i * Q_BLOCK:(i + 1) * Q_BLOCK, :] = o.astype(o_ref.dtype)


def _attn(da, toe, far, cst, subln_w, batch, length):
    da3 = da.reshape(batch, length, DA_COLS)
    hw = DA_HEADS
    out = pl.pallas_call(
        functools.partial(_attn_kernel, n_blocks=length // Q_BLOCK),
        grid=(batch, DA_HEADS),
        in_specs=[
            pl.BlockSpec((None, length, DA_DV), lambda b, h: (b, 0, h)),
            pl.BlockSpec((None, length, DA_DV), lambda b, h: (b, 0, hw + h)),
            pl.BlockSpec((None, length, DA_DV), lambda b, h: (b, 0, 2 * hw + h)),
            pl.BlockSpec((None, 2, Q_BLOCK, Q_BLOCK), lambda b, h: (h, 0, 0, 0)),
            pl.BlockSpec((None, 1, LANES), lambda b, h: (h, 0, 0)),
            pl.BlockSpec((8, LANES), lambda b, h: (0, 0)),
            pl.BlockSpec((1, DA_DV), lambda b, h: (0, 0)),
        ],
        out_specs=pl.BlockSpec((None, length, DA_DV), lambda b, h: (b, 0, h)),
        out_shape=jax.ShapeDtypeStruct((batch, length, DA_HEADS * DA_DV), BF16),
        compiler_params=_params("parallel", "parallel"),
        name="diff_attn",
    )(da3, da3, da3, toe, far, cst, subln_w)
    return out.reshape(batch * length, DA_HEADS * DA_DV)


def _t5_bucket(dist):
    n = jnp.maximum(dist, 0)
    max_exact = REL_BUCKETS // 2
    nf = jnp.maximum(n, max_exact).astype(F32)
    large = max_exact + (jnp.log(nf / max_exact) / math.log(REL_MAX_DIST / max_exact)
                         * (REL_BUCKETS - max_exact)).astype(jnp.int32)
    large = jnp.minimum(large, REL_BUCKETS - 1)
    return jnp.where(n < max_exact, n, large)


def _attn_bias_tables(rel_bias):
    tab = rel_bias.astype(F32)
    qi = jnp.arange(Q_BLOCK, dtype=jnp.int32)[:, None]
    ki = jnp.arange(Q_BLOCK, dtype=jnp.int32)[None, :]

    def lookup(bucket):
        onehot = bucket[None, :, :, None] == jnp.arange(REL_BUCKETS, dtype=jnp.int32)
        return jnp.sum(jnp.where(onehot, tab.T[:, None, None, :], 0.0), axis=-1)

    diag = jnp.where((ki <= qi)[None], lookup(_t5_bucket(qi - ki)), NEG)
    prev = lookup(_t5_bucket(qi - ki + Q_BLOCK))
    toe = jnp.stack([diag, prev], axis=1)
    far = jnp.broadcast_to(tab[REL_BUCKETS - 1][:, None, None], (DA_HEADS, 1, LANES))
    return toe, far


def _out_proj_kernel(h_ref, ohg_ref, oda_ref, wo_ref, g_ref, *rest, with_router):
    if with_router:
        router_ref, hn_ref, u_ref, lg_ref = rest
    else:
        hn_ref, u_ref = rest
    hn = (h_ref[...] + _dot(ohg_ref[...], wo_ref[:HG_WIDTH, :])
          + _dot(oda_ref[...], wo_ref[HG_WIDTH:, :]))
    hn_ref[...] = hn
    u = _rms(hn, g_ref[...])
    u_ref[...] = u.astype(u_ref.dtype)
    if with_router:
        lg_ref[...] = jnp.dot(u, router_ref[...], precision=lax.Precision.HIGHEST,
                              preferred_element_type=F32)


def _out_proj(h, o_hg, o_da, wo, layer, gain, router=None):
    n = h.shape[0]
    with_router = router is not None
    row = lambda i: (i, 0)
    full = lambda i: (0, 0)
    in_specs = [
        pl.BlockSpec((ROW_TILE, D_MODEL), row),
        pl.BlockSpec((ROW_TILE, HG_WIDTH), row),
        pl.BlockSpec((ROW_TILE, DA_HEADS * DA_DV), row),
        pl.BlockSpec((None,) + wo.shape[1:], lambda i: (layer, 0, 0)),
        pl.BlockSpec((1, D_MODEL), full),
    ]
    out_specs = [pl.BlockSpec((ROW_TILE, D_MODEL), row), pl.BlockSpec((ROW_TILE, D_MODEL), row)]
    out_shape = [jax.ShapeDtypeStruct((n, D_MODEL), F32),
                 jax.ShapeDtypeStruct((n, D_MODEL), F32 if with_router else BF16)]
    args = [h, o_hg, o_da, wo, gain]
    if with_router:
        in_specs.append(pl.BlockSpec(router.shape, full))
        out_specs.append(pl.BlockSpec((ROW_TILE, LANES), row))
        out_shape.append(jax.ShapeDtypeStruct((n, LANES), F32))
        args.append(router)
    return pl.pallas_call(
        functools.partial(_out_proj_kernel, with_router=with_router),
        grid=(n // ROW_TILE,),
        in_specs=in_specs,
        out_specs=out_specs,
        out_shape=out_shape,
        compiler_params=_params("parallel"),
        name="out_proj_router" if with_router else "out_proj",
    )(*args)


def _dense_ffn_kernel(h_ref, u_ref, w1_ref, w3_ref, w2_ref, o_ref):
    u = u_ref[...]
    a = _dot(u, w1_ref[...])
    act = (_silu(a) * _dot(u, w3_ref[...])).astype(BF16)
    o_ref[...] = h_ref[...] + _dot(act, w2_ref[...])


def _dense_ffn(h, u, w1, w3, w2, idx):
    n = h.shape[0]
    row = lambda i: (i, 0)
    stacked = lambda i: (idx, 0, 0)
    return pl.pallas_call(
        _dense_ffn_kernel,
        grid=(n // ROW_TILE,),
        in_specs=[
            pl.BlockSpec((ROW_TILE, D_MODEL), row),
            pl.BlockSpec((ROW_TILE, D_MODEL), row),
            pl.BlockSpec((None,) + w1.shape[1:], stacked),
            pl.BlockSpec((None,) + w3.shape[1:], stacked),
            pl.BlockSpec((None,) + w2.shape[1:], stacked),
        ],
        out_specs=pl.BlockSpec((ROW_TILE, D_MODEL), row),
        out_shape=jax.ShapeDtypeStruct((n, D_MODEL), F32),
        compiler_params=_params("parallel"),
        name="dense_ffn",
    )(h, u, w1, w3, w2)


def _row_copy(src_ref, src_row, dst_ref, dst_row, sem):
    return pltpu.make_async_copy(src_ref.at[pl.ds(src_row, 1), :], dst_ref.at[pl.ds(dst_row, 1), :], sem)


def _dispatch_kernel(dest_ref, u_ref, xs_in_ref, xs_ref, sem):
    del xs_in_ref
    base = pl.program_id(0) * ROW_TILE

    def issue(r, carry):
        for k in range(TOP_K):
            _row_copy(u_ref, r, xs_ref, dest_ref[(base + r) * TOP_K + k], sem).start()
        return carry

    lax.fori_loop(0, ROW_TILE, issue, 0)

    def drain(r, carry):
        for k in range(TOP_K):
            _row_copy(u_ref, r, xs_ref, dest_ref[(base + r) * TOP_K + k], sem).wait()
        return carry

    lax.fori_loop(0, ROW_TILE, drain, 0)


def _dispatch(dest, u, n_slots):
    n = u.shape[0]
    zeros = jnp.zeros((n_slots, D_MODEL), u.dtype)
    return pl.pallas_call(
        _dispatch_kernel,
        grid_spec=pltpu.PrefetchScalarGridSpec(
            num_scalar_prefetch=1,
            grid=(n // ROW_TILE,),
            in_specs=[
                pl.BlockSpec((ROW_TILE, D_MODEL), lambda i, d: (i, 0)),
                pl.BlockSpec(memory_space=pl.ANY),
            ],
            out_specs=pl.BlockSpec(memory_space=pl.ANY),
            scratch_shapes=[pltpu.SemaphoreType.DMA(())],
        ),
        out_shape=jax.ShapeDtypeStruct((n_slots, D_MODEL), u.dtype),
        input_output_aliases={2: 0},
        compiler_params=_params("arbitrary"),
        name="moe_dispatch",
    )(dest, u, zeros)


def _expert_kernel(be_ref, nused_ref, xs_ref, w1_ref, w3_ref, w2_ref, y_ref):
    del be_ref
    blk = pl.program_id(0)

    @pl.when(blk < nused_ref[0])
    def _():
        x = xs_ref[...].astype(BF16)
        acc = jnp.zeros((MOE_TILE, D_MODEL), F32)
        for c0 in range(0, D_FF_EXPERT, FF_CHUNK):
            a = _dot(x, w1_ref[:, c0:c0 + FF_CHUNK])
            act = (_silu(a) * _dot(x, w3_ref[:, c0:c0 + FF_CHUNK])).astype(BF16)
            acc = acc + _dot(act, w2_ref[c0:c0 + FF_CHUNK, :])
        y_ref[...] = acc

    @pl.when(blk >= nused_ref[0])
    def _():
        y_ref[...] = jnp.zeros_like(y_ref)


def _experts(block_expert, n_used, xs, w1, w3, w2, idx):
    n_slots = xs.shape[0]
    wmap = lambda i, be, nu: (idx, be[i], 0, 0)
    return pl.pallas_call(
        _expert_kernel,
        grid_spec=pltpu.PrefetchScalarGridSpec(
            num_scalar_prefetch=2,
            grid=(n_slots // MOE_TILE,),
            in_specs=[
                pl.BlockSpec((MOE_TILE, D_MODEL), lambda i, be, nu: (i, 0)),
                pl.BlockSpec((None, None, D_MODEL, D_FF_EXPERT), wmap),
                pl.BlockSpec((None, None, D_MODEL, D_FF_EXPERT), wmap),
                pl.BlockSpec((None, None, D_FF_EXPERT, D_MODEL), wmap),
            ],
            out_specs=pl.BlockSpec((MOE_TILE, D_MODEL), lambda i, be, nu: (i, 0)),
        ),
        out_shape=jax.ShapeDtypeStruct((n_slots, D_MODEL), F32),
        compiler_params=_params("arbitrary"),
        name="moe_experts",
    )(block_expert, n_used, xs, w1, w3, w2)


def _combine_kernel(dest_ref, h_ref, gate_ref, y_ref, o_ref, buf_ref, sem):
    base = pl.program_id(0) * ROW_TILE

    def issue(r, carry):
        for k in range(TOP_K):
            _row_copy(y_ref, dest_ref[(base + r) * TOP_K + k], buf_ref.at[k], r, sem).start()
        return carry

    lax.fori_loop(0, ROW_TILE, issue, 0)

    def drain(r, carry):
        for k in range(TOP_K):
            _row_copy(y_ref, dest_ref[(base + r) * TOP_K + k], buf_ref.at[k], r, sem).wait()
        return carry

    lax.fori_loop(0, ROW_TILE, drain, 0)
    gate = gate_ref[...]
    o_ref[...] = h_ref[...] + gate[:, 0:1] * buf_ref[0] + gate[:, 1:2] * buf_ref[1]


def _combine(dest, h, gate, y):
    n = h.shape[0]
    return pl.pallas_call(
        _combine_kernel,
        grid_spec=pltpu.PrefetchScalarGridSpec(
            num_scalar_prefetch=1,
            grid=(n // ROW_TILE,),
            in_specs=[
                pl.BlockSpec((ROW_TILE, D_MODEL), lambda i, d: (i, 0)),
                pl.BlockSpec((ROW_TILE, TOP_K), lambda i, d: (i, 0)),
                pl.BlockSpec(memory_space=pl.ANY),
            ],
            out_specs=pl.BlockSpec((ROW_TILE, D_MODEL), lambda i, d: (i, 0)),
            scratch_shapes=[pltpu.VMEM((TOP_K, ROW_TILE, D_MODEL), F32),
                            pltpu.SemaphoreType.DMA(())],
        ),
        out_shape=jax.ShapeDtypeStruct((n, D_MODEL), F32),
        compiler_params=_params("arbitrary"),
        name="moe_combine",
    )(dest, h, gate, y)


def _route(logits, n_slots):
    n = logits.shape[0]
    eid = lax.broadcasted_iota(jnp.int32, logits.shape, 1)
    i1 = jnp.argmax(logits, axis=-1).astype(jnp.int32)
    l1 = jnp.max(logits, axis=-1)
    rest = jnp.where(eid == i1[:, None], -jnp.inf, logits)
    i2 = jnp.argmax(rest, axis=-1).astype(jnp.int32)
    l2 = jnp.max(rest, axis=-1)
    e2 = jnp.exp(l2 - l1)
    gate = jnp.stack([1.0 / (1.0 + e2), e2 / (1.0 + e2)], axis=-1)

    flat_e = jnp.stack([i1, i2], axis=-1).reshape(n * TOP_K)
    onehot = (flat_e[:, None] == jnp.arange(N_EXPERTS, dtype=jnp.int32)[None, :]).astype(jnp.int32)
    csum = jnp.cumsum(onehot, axis=0)
    rank = jnp.sum(csum * onehot, axis=-1) - 1
    counts = csum[-1]
    padded = (counts + MOE_TILE - 1) // MOE_TILE * MOE_TILE
    pad_end = jnp.cumsum(padded)
    pad_start = pad_end - padded
    dest = (pad_start[flat_e] + rank).astype(jnp.int32)
    n_blocks = n_slots // MOE_TILE
    block_expert = jnp.minimum(
        jnp.searchsorted(pad_end, jnp.arange(n_blocks, dtype=jnp.int32) * MOE_TILE, side="right"),
        N_EXPERTS - 1).astype(jnp.int32)
    n_used = (pad_end[-1:] // MOE_TILE).astype(jnp.int32)
    return dest, gate, block_expert, n_used


def _moe_ffn(h, u, logits, w1, w3, w2, idx):
    n = h.shape[0]
    n_slots = (n * TOP_K // MOE_TILE + N_EXPERTS) * MOE_TILE
    dest, gate, block_expert, n_used = _route(logits, n_slots)
    xs = _dispatch(dest, u, n_slots)
    y = _experts(block_expert, n_used, xs, w1, w3, w2, idx)
    return _combine(dest, h, gate, y)


def _final_kernel(h_ref, g_ref, o_ref):
    o_ref[...] = _rms(h_ref[...], g_ref[...])


def _final_norm(h3, gain, seq):
    batch = h3.shape[0]
    lead_blocks = LEAD // Q_BLOCK
    return pl.pallas_call(
        _final_kernel,
        grid=(seq // Q_BLOCK,),
        in_specs=[
            pl.BlockSpec((batch, Q_BLOCK, D_MODEL), lambda j: (0, j + lead_blocks, 0)),
            pl.BlockSpec((1, D_MODEL), lambda j: (0, 0)),
        ],
        out_specs=pl.BlockSpec((batch, Q_BLOCK, D_MODEL), lambda j: (0, j, 0)),
        out_shape=jax.ShapeDtypeStruct((batch, seq, D_MODEL), F32),
        compiler_params=_params("parallel"),
        name="final_norm",
    )(h3, gain)


def kernel(x, meta, rel_bias, norm_mix, w_in, hg_lb_logits, hg_norm_w, da_lambda, da_subln_w, w_out, norm_ffn, dense_w1, dense_w3, dense_w2, moe_router, moe_w1, moe_w3, moe_w2, final_norm):
    batch, seq, d = x.shape
    length = LEAD + seq
    h = jnp.concatenate([
        jnp.zeros((batch, LEAD - N_META, d), x.dtype),
        jnp.broadcast_to(meta[None].astype(x.dtype), (batch, N_META, d)),
        x], axis=1).reshape(batch * length, d)

    toe, far = _attn_bias_tables(rel_bias)
    lb_cum = jnp.cumsum(jax.nn.softmax(hg_lb_logits.astype(F32), axis=0), axis=0)
    lb_all = jnp.clip(lb_cum - lb_cum[0:1], 0.0, LB_MAX)
    log_lb = jnp.log(lb_all)
    log_1m_lb = jnp.log1p(-lb_all)

    w_in_b, w_out_b = w_in.astype(BF16), w_out.astype(BF16)
    dense_b = [w.astype(BF16) for w in (dense_w1, dense_w3, dense_w2)]
    moe_b = [w.astype(BF16) for w in (moe_w1, moe_w3, moe_w2)]

    for l in range(DEPTH):
        hg, da = _mix_in(h, norm_mix[l][None], w_in_b, l)
        o_hg = _hgrn(hg, log_lb[l][None], log_1m_lb[l][None], hg_norm_w[l][None], batch, length)
        lam_init = 0.8 - 0.6 * math.exp(-0.3 * l)
        lv = da_lambda[l].astype(F32)
        lam = jnp.exp(jnp.sum(lv[0] * lv[1])) - jnp.exp(jnp.sum(lv[2] * lv[3])) + lam_init
        cst = jnp.zeros((8, LANES), F32).at[0].set(lam).at[1].set(1.0 - lam_init)
        o_da = _attn(da, toe, far, cst, da_subln_w[l][None], batch, length)
        i = l // 2
        if l % 2 == 0:
            hn, u = _out_proj(h, o_hg, o_da, w_out_b, l, norm_ffn[l][None])
            h = _dense_ffn(hn, u, *dense_b, i)
        else:
            router = jnp.zeros((d, LANES), F32).at[:, :N_EXPERTS].set(moe_router[i].astype(F32))
            hn, u, lg = _out_proj(h, o_hg, o_da, w_out_b, l, norm_ffn[l][None], router)
            h = _moe_ffn(hn, u, lg[:, :N_EXPERTS], *moe_b, i)
    return _final_norm(h.reshape(batch, length, d), final_norm[None], seq)
```

```python
import functools
import math

import jax
import jax.numpy as jnp
import numpy as np
from jax import lax
from jax.experimental import pallas as pl
from jax.experimental.pallas import tpu as pltpu

D_MODEL = 1024
DEPTH = 4
N_META = 16
LEAD = 128
HG_WIDTH = 512
HG_HEADS = 4
HG_D = 128
HG_CHUNK = 64
DA_HEADS = 4
DA_DQK = 64
DA_DV = 128
Q_BLOCK = 128
REL_BUCKETS = 32
REL_MAX_DIST = 128
N_EXPERTS = 8
TOP_K = 2
D_FF_EXPERT = 3584
EPS = 1e-6
NEG = -1e30
LB_MAX = 0.999
HG_COLS = 4 * HG_WIDTH
DA_COLS = 3 * DA_HEADS * DA_DV
W_IN_COLS = HG_COLS + DA_COLS

LANES = 128
VMEM_LIMIT = 56 * 1024 * 1024

ROW_TILE = 256
HG_TILE = 128
MOE_TILE = 256
FF_CHUNK = 512

F32 = jnp.float32
BF16 = jnp.bfloat16


def _params(*sem):
    return pltpu.CompilerParams(dimension_semantics=sem, vmem_limit_bytes=VMEM_LIMIT)


def _dot(a, b):
    return jnp.dot(a, b, preferred_element_type=F32)


def _dot_nt(a, b):
    return lax.dot_general(a, b, (((1,), (1,)), ((), ())), preferred_element_type=F32)


def _dot_tn(a, b):
    return lax.dot_general(a, b, (((0,), (0,)), ((), ())), preferred_element_type=F32)


def _rms(x, gain):
    return x * lax.rsqrt(jnp.mean(x * x, axis=-1, keepdims=True) + EPS) * gain


TOKEN_ROWS = D_MODEL // LANES


def _store_token_major(ref, x):
    t = x.shape[0]
    for s in range(TOKEN_ROWS):
        ref[pl.ds(s, t, stride=TOKEN_ROWS), :] = x[:, s * LANES:(s + 1) * LANES]


def _load_token_major(ref, t):
    return jnp.concatenate(
        [ref[pl.ds(s, t, stride=TOKEN_ROWS), :] for s in range(TOKEN_ROWS)], axis=1)


def _silu(x):
    return x * (0.5 * jnp.tanh(0.5 * x) + 0.5)


def _mix_in_kernel(x_ref, g_ref, w_ref, hg_ref, da_ref):
    u = _rms(x_ref[...], g_ref[...]).astype(BF16)
    hg_ref[...] = _dot(u, w_ref[:, :HG_COLS])
    da_ref[...] = _dot(u, w_ref[:, HG_COLS:]).astype(BF16)


def _mix_in(h, gain, w, layer):
    n = h.shape[0]
    return pl.pallas_call(
        _mix_in_kernel,
        grid=(n // ROW_TILE,),
        in_specs=[
            pl.BlockSpec((ROW_TILE, D_MODEL), lambda i: (i, 0)),
            pl.BlockSpec((1, D_MODEL), lambda i: (0, 0)),
            pl.BlockSpec((None, D_MODEL, W_IN_COLS), lambda i: (layer, 0, 0)),
        ],
        out_specs=[
            pl.BlockSpec((ROW_TILE, HG_COLS), lambda i: (i, 0)),
            pl.BlockSpec((ROW_TILE, DA_COLS), lambda i: (i, 0)),
        ],
        out_shape=[
            jax.ShapeDtypeStruct((n, HG_COLS), F32),
            jax.ShapeDtypeStruct((n, DA_COLS), BF16),
        ],
        compiler_params=_params("parallel"),
        name="mix_in",
    )(h, gain, w)


HG_LEVELS = (32, 16, 8, 4, 2, 1)
N_SUMS = len(HG_LEVELS) + 2


def _hgrn_consts():
    c = HG_CHUNK
    t = np.arange(c)[:, None]
    j = np.arange(c)[None, :]
    sums = np.zeros((N_SUMS, c, c), np.float32)
    masks = np.zeros((len(HG_LEVELS) + 1, c, c), np.float32)
    sums[0] = j <= t
    masks[0] = np.eye(c)
    for li, w in enumerate(HG_LEVELS, start=1):
        ref = (t // (2 * w)) * (2 * w) + w
        sums[li] = np.where(t >= ref, (j > ref) & (j <= t), (j > t) & (j <= ref))
        masks[li] = (t // (2 * w) == j // (2 * w)) & (t % (2 * w) >= w) & (j % (2 * w) < w)
    sums[N_SUMS - 1] = j > t
    return sums.reshape(N_SUMS * c, c), masks


_HG_SUMS, _HG_MASKS = _hgrn_consts()


def _hgrn_kernel(hg_ref, loga_ref, log1m_ref, nw_ref, sums_ref, masks_ref, o_ref, state_ref):
    c_idx = pl.program_id(1)

    @pl.when(c_idx == 0)
    def _():
        state_ref[...] = jnp.zeros_like(state_ref)

    C = HG_CHUNK
    sums = sums_ref[...]
    nw = nw_ref[...]
    for ch in range(HG_TILE // C):
        rows = slice(ch * C, (ch + 1) * C)
        row_idx = c_idx * HG_TILE + ch * C + lax.broadcasted_iota(jnp.int32, (C, 1), 0)
        valid = row_idx >= (LEAD - N_META)
        for hd in range(HG_HEADS):
            cols = slice(hd * HG_D, (hd + 1) * HG_D)
            q = hg_ref[rows, hd * HG_D:(hd + 1) * HG_D]
            f = hg_ref[rows, HG_WIDTH + hd * HG_D:HG_WIDTH + (hd + 1) * HG_D]
            v = hg_ref[rows, 2 * HG_WIDTH + hd * HG_D:2 * HG_WIDTH + (hd + 1) * HG_D]
            g = hg_ref[rows, 3 * HG_WIDTH + hd * HG_D:3 * HG_WIDTH + (hd + 1) * HG_D]
            loga = loga_ref[:, cols]
            log1m = log1m_ref[:, cols]

            qf = _silu(q)
            ls = jnp.minimum(f, 0.0) - jnp.log1p(jnp.exp(-jnp.abs(f)))
            cc = log1m + ls
            lf = jnp.maximum(loga, cc) + jnp.log1p(jnp.exp(-jnp.abs(loga - cc)))
            kk = jnp.exp(cc - f)
            lf = jnp.where(valid, lf, 0.0)
            kk = jnp.where(valid, kk, 0.0)

            lf_hi = lf.astype(BF16)
            lf_lo = (lf - lf_hi.astype(F32)).astype(BF16)
            e = jnp.exp(_dot(sums, lf_hi) + _dot(sums, lf_lo))

            e_b = e[0:C]
            st = state_ref[hd]
            inter = _dot_nt((qf * e_b).astype(BF16), st.astype(BF16))
            scores = masks_ref[0] * _dot_nt(qf.astype(BF16), kk.astype(BF16))
            for li in range(1, len(HG_LEVELS) + 1):
                e_l = e[li * C:(li + 1) * C]
                scores += masks_ref[li] * _dot_nt((qf * e_l).astype(BF16), (kk * e_l).astype(BF16))
            vb = v.astype(BF16)
            o = inter + _dot(scores.astype(BF16), vb)

            e_end = e[(N_SUMS - 1) * C:N_SUMS * C]
            state_ref[hd] = st * e_b[C - 1:C, :] + _dot_tn(vb, (kk * e_end).astype(BF16))

            o = _rms(o, nw) * _silu(g)
            o_ref[rows, cols] = o.astype(o_ref.dtype)


def _hgrn(hg, loga, log1m, norm_w, batch, length):
    hg3 = hg.reshape(batch, length, HG_COLS)
    out = pl.pallas_call(
        _hgrn_kernel,
        grid=(batch, length // HG_TILE),
        in_specs=[
            pl.BlockSpec((None, HG_TILE, HG_COLS), lambda b, c: (b, c, 0)),
            pl.BlockSpec((1, HG_WIDTH), lambda b, c: (0, 0)),
            pl.BlockSpec((1, HG_WIDTH), lambda b, c: (0, 0)),
            pl.BlockSpec((1, HG_D), lambda b, c: (0, 0)),
            pl.BlockSpec(_HG_SUMS.shape, lambda b, c: (0, 0)),
            pl.BlockSpec(_HG_MASKS.shape, lambda b, c: (0, 0, 0)),
        ],
        out_specs=pl.BlockSpec((None, HG_TILE, HG_WIDTH), lambda b, c: (b, c, 0)),
        out_shape=jax.ShapeDtypeStruct((batch, length, HG_WIDTH), BF16),
        scratch_shapes=[pltpu.VMEM((HG_HEADS, HG_D, HG_D), F32)],
        compiler_params=_params("parallel", "arbitrary"),
        name="hgrn2",
    )(hg3, loga, log1m, norm_w, jnp.asarray(_HG_SUMS, BF16), jnp.asarray(_HG_MASKS, F32))
    return out.reshape(batch * length, HG_WIDTH)


def _attn_kernel(q_ref, k_ref, v_ref, toe_ref, far_ref, cst_ref, w_ref, o_ref, *, n_blocks):
    lam = cst_ref[0:1, :]
    post = cst_ref[1:2, :]
    far_bias = far_ref[0:1, 0:1]
    lane = lax.broadcasted_iota(jnp.int32, (Q_BLOCK, Q_BLOCK), 1)
    first_half = lane < DA_DQK
    key_ok0 = lane >= (LEAD - N_META)
    scale = DA_DQK ** -0.5

    def block_bias(kind, kb):
        bias = toe_ref[kind]
        if kb == 0:
            bias = jnp.where(key_ok0, bias, NEG)
        return bias

    for i in range(n_blocks):
        qi = q_ref[i * Q_BLOCK:(i + 1) * Q_BLOCK, :] * scale
        zero = jnp.zeros_like(qi)
        q_maps = (jnp.where(first_half, qi, zero), jnp.where(first_half, zero, qi))
        n_far = (i - 1) * Q_BLOCK if i >= 2 else 0
        if n_far:
            far_lane = lax.broadcasted_iota(jnp.int32, (1, n_far), 1)
            far_row = jnp.where(far_lane >= (LEAD - N_META), far_bias, NEG)
        probs = []
        for qm in q_maps:
            pieces = []
            if n_far:
                pieces.append(_dot_nt(qm, k_ref[0:n_far, :]) + far_row)
            if i >= 1:
                kb = i - 1
                pieces.append(_dot_nt(qm, k_ref[kb * Q_BLOCK:(kb + 1) * Q_BLOCK, :])
                              + block_bias(1, kb))
            pieces.append(_dot_nt(qm, k_ref[i * Q_BLOCK:(i + 1) * Q_BLOCK, :])
                          + block_bias(0, i))
            m = pieces[0].max(axis=-1, keepdims=True)
            for s in pieces[1:]:
                m = jnp.maximum(m, s.max(axis=-1, keepdims=True))
            pieces = [jnp.exp(s - m) for s in pieces]
            den = pieces[0].sum(axis=-1, keepdims=True)
            for p in pieces[1:]:
                den = den + p.sum(axis=-1, keepdims=True)
            probs.append((pieces, 1.0 / den))
        (p0, r0), (p1, r1) = probs
        r1 = r1 * lam[:, 0:1]
        starts = ([0] if n_far else []) + ([(i - 1) * Q_BLOCK] if i >= 1 else []) + [i * Q_BLOCK]
        o = None
        for a0, a1, ks in zip(p0, p1, starts):
            a = (a0 * r0 - a1 * r1).astype(BF16)
            part = _dot(a, v_ref[ks:ks + a.shape[1], :])
            o = part if o is None else o + part
        o = _rms(o, w_ref[...]) * post
        o_ref[i * Q_BLOCK:(i + 1) * Q_BLOCK, :] = o.astype(o_ref.dtype)


def _attn(da, toe, far, cst, subln_w, batch, length):
    da3 = da.reshape(batch, length, DA_COLS)
    hw = DA_HEADS
    out = pl.pallas_call(
        functools.partial(_attn_kernel, n_blocks=length // Q_BLOCK),
        grid=(batch, DA_HEADS),
        in_specs=[
            pl.BlockSpec((None, length, DA_DV), lambda b, h: (b, 0, h)),
            pl.BlockSpec((None, length, DA_DV), lambda b, h: (b, 0, hw + h)),
            pl.BlockSpec((None, length, DA_DV), lambda b, h: (b, 0, 2 * hw + h)),
            pl.BlockSpec((None, 2, Q_BLOCK, Q_BLOCK), lambda b, h: (h, 0, 0, 0)),
            pl.BlockSpec((None, 1, LANES), lambda b, h: (h, 0, 0)),
            pl.BlockSpec((8, LANES), lambda b, h: (0, 0)),
            pl.BlockSpec((1, DA_DV), lambda b, h: (0, 0)),
        ],
        out_specs=pl.BlockSpec((None, length, DA_DV), lambda b, h: (b, 0, h)),
        out_shape=jax.ShapeDtypeStruct((batch, length, DA_HEADS * DA_DV), BF16),
        compiler_params=_params("parallel", "parallel"),
        name="diff_attn",
    )(da3, da3, da3, toe, far, cst, subln_w)
    return out.reshape(batch * length, DA_HEADS * DA_DV)


def _t5_bucket(dist):
    n = jnp.maximum(dist, 0)
    max_exact = REL_BUCKETS // 2
    nf = jnp.maximum(n, max_exact).astype(F32)
    large = max_exact + (jnp.log(nf / max_exact) / math.log(REL_MAX_DIST / max_exact)
                         * (REL_BUCKETS - max_exact)).astype(jnp.int32)
    large = jnp.minimum(large, REL_BUCKETS - 1)
    return jnp.where(n < max_exact, n, large)


def _attn_bias_tables(rel_bias):
    tab = rel_bias.astype(F32)
    qi = jnp.arange(Q_BLOCK, dtype=jnp.int32)[:, None]
    ki = jnp.arange(Q_BLOCK, dtype=jnp.int32)[None, :]

    def lookup(bucket):
        onehot = bucket[None, :, :, None] == jnp.arange(REL_BUCKETS, dtype=jnp.int32)
        return jnp.sum(jnp.where(onehot, tab.T[:, None, None, :], 0.0), axis=-1)

    diag = jnp.where((ki <= qi)[None], lookup(_t5_bucket(qi - ki)), NEG)
    prev = lookup(_t5_bucket(qi - ki + Q_BLOCK))
    toe = jnp.stack([diag, prev], axis=1)
    far = jnp.broadcast_to(tab[REL_BUCKETS - 1][:, None, None], (DA_HEADS, 1, LANES))
    return toe, far


def _out_proj_kernel(h_ref, ohg_ref, oda_ref, wo_ref, g_ref, *rest, with_router):
    if with_router:
        router_ref, hn_ref, u_ref, lg_ref = rest
    else:
        hn_ref, u_ref = rest
    hn = (h_ref[...] + _dot(ohg_ref[...], wo_ref[:HG_WIDTH, :])
          + _dot(oda_ref[...], wo_ref[HG_WIDTH:, :]))
    hn_ref[...] = hn
    u = _rms(hn, g_ref[...])
    if with_router:
        _store_token_major(u_ref, u)
        lg_ref[...] = jnp.dot(u, router_ref[...], precision=lax.Precision.HIGHEST,
                              preferred_element_type=F32)
    else:
        u_ref[...] = u.astype(u_ref.dtype)


def _out_proj(h, o_hg, o_da, wo, layer, gain, router=None):
    n = h.shape[0]
    with_router = router is not None
    row = lambda i: (i, 0)
    full = lambda i: (0, 0)
    in_specs = [
        pl.BlockSpec((ROW_TILE, D_MODEL), row),
        pl.BlockSpec((ROW_TILE, HG_WIDTH), row),
        pl.BlockSpec((ROW_TILE, DA_HEADS * DA_DV), row),
        pl.BlockSpec((None,) + wo.shape[1:], lambda i: (layer, 0, 0)),
        pl.BlockSpec((1, D_MODEL), full),
    ]
    out_specs = [pl.BlockSpec((ROW_TILE, D_MODEL), row), pl.BlockSpec((ROW_TILE, D_MODEL), row)]
    out_shape = [jax.ShapeDtypeStruct((n, D_MODEL), F32), jax.ShapeDtypeStruct((n, D_MODEL), BF16)]
    args = [h, o_hg, o_da, wo, gain]
    if with_router:
        out_specs[1] = pl.BlockSpec((ROW_TILE * TOKEN_ROWS, LANES), row)
        out_shape[1] = jax.ShapeDtypeStruct((n * TOKEN_ROWS, LANES), F32)
        in_specs.append(pl.BlockSpec(router.shape, full))
        out_specs.append(pl.BlockSpec((ROW_TILE, LANES), row))
        out_shape.append(jax.ShapeDtypeStruct((n, LANES), F32))
        args.append(router)
    return pl.pallas_call(
        functools.partial(_out_proj_kernel, with_router=with_router),
        grid=(n // ROW_TILE,),
        in_specs=in_specs,
        out_specs=out_specs,
        out_shape=out_shape,
        compiler_params=_params("parallel"),
        name="out_proj_router" if with_router else "out_proj",
    )(*args)


def _dense_ffn_kernel(h_ref, u_ref, w1_ref, w3_ref, w2_ref, o_ref):
    u = u_ref[...]
    a = _dot(u, w1_ref[...])
    act = (_silu(a) * _dot(u, w3_ref[...])).astype(BF16)
    o_ref[...] = h_ref[...] + _dot(act, w2_ref[...])


def _dense_ffn(h, u, w1, w3, w2, idx):
    n = h.shape[0]
    row = lambda i: (i, 0)
    stacked = lambda i: (idx, 0, 0)
    return pl.pallas_call(
        _dense_ffn_kernel,
        grid=(n // ROW_TILE,),
        in_specs=[
            pl.BlockSpec((ROW_TILE, D_MODEL), row),
            pl.BlockSpec((ROW_TILE, D_MODEL), row),
            pl.BlockSpec((None,) + w1.shape[1:], stacked),
            pl.BlockSpec((None,) + w3.shape[1:], stacked),
            pl.BlockSpec((None,) + w2.shape[1:], stacked),
        ],
        out_specs=pl.BlockSpec((ROW_TILE, D_MODEL), row),
        out_shape=jax.ShapeDtypeStruct((n, D_MODEL), F32),
        compiler_params=_params("parallel"),
        name="dense_ffn",
    )(h, u, w1, w3, w2)


DMA_BATCH = 512
ISSUE_TOKENS = 8


def _token_copy(src_ref, src_tok, dst_ref, dst_tok, sem):
    src = pl.multiple_of(src_tok * TOKEN_ROWS, TOKEN_ROWS)
    dst = pl.multiple_of(dst_tok * TOKEN_ROWS, TOKEN_ROWS)
    return pltpu.make_async_copy(src_ref.at[pl.ds(src, TOKEN_ROWS), :],
                                 dst_ref.at[pl.ds(dst, TOKEN_ROWS), :], sem)


def _dispatch_kernel(dest_ref, u_ref, xs_in_ref, xs_ref, sem, *, n_pairs):
    del xs_in_ref

    def start_batch(b):
        def start(g, carry):
            tok0 = b * (DMA_BATCH // TOP_K) + g * ISSUE_TOKENS
            slots = [dest_ref[tok0 * TOP_K + j] for j in range(ISSUE_TOKENS * TOP_K)]
            for j, slot in enumerate(slots):
                _token_copy(u_ref, tok0 + j // TOP_K, xs_ref, slot, sem).start()
            return carry

        lax.fori_loop(0, DMA_BATCH // TOP_K // ISSUE_TOKENS, start, 0)

    def retire_batch():
        rows = DMA_BATCH * TOKEN_ROWS
        pltpu.make_async_copy(u_ref.at[pl.ds(0, rows), :], xs_ref.at[pl.ds(0, rows), :], sem).wait()

    def batch(b, carry):
        start_batch(b)
        retire_batch()
        return carry

    start_batch(0)
    lax.fori_loop(1, n_pairs // DMA_BATCH, batch, 0)
    retire_batch()


def _dispatch(dest, u_tm, n_slots):
    n_pairs = dest.shape[0]
    assert n_pairs % DMA_BATCH == 0
    zeros = jnp.zeros((n_slots * TOKEN_ROWS, LANES), u_tm.dtype)
    return pl.pallas_call(
        functools.partial(_dispatch_kernel, n_pairs=n_pairs),
        grid_spec=pltpu.PrefetchScalarGridSpec(
            num_scalar_prefetch=1,
            grid=(1,),
            in_specs=[pl.BlockSpec(memory_space=pl.ANY), pl.BlockSpec(memory_space=pl.ANY)],
            out_specs=pl.BlockSpec(memory_space=pl.ANY),
            scratch_shapes=[pltpu.SemaphoreType.DMA(())],
        ),
        out_shape=jax.ShapeDtypeStruct(zeros.shape, u_tm.dtype),
        input_output_aliases={2: 0},
        compiler_params=_params("arbitrary"),
        name="moe_dispatch",
    )(dest, u_tm, zeros)


def _expert_kernel(be_ref, nused_ref, xs_ref, w1_ref, w3_ref, w2_ref, y_ref):
    del be_ref
    blk = pl.program_id(0)

    @pl.when(blk < nused_ref[0])
    def _():
        x = _load_token_major(xs_ref, MOE_TILE).astype(BF16)
        acc = jnp.zeros((MOE_TILE, D_MODEL), F32)
        for c0 in range(0, D_FF_EXPERT, FF_CHUNK):
            a = _dot(x, w1_ref[:, c0:c0 + FF_CHUNK])
            act = (_silu(a) * _dot(x, w3_ref[:, c0:c0 + FF_CHUNK])).astype(BF16)
            acc = acc + _dot(act, w2_ref[c0:c0 + FF_CHUNK, :])
        _store_token_major(y_ref, acc)

    @pl.when(blk >= nused_ref[0])
    def _():
        y_ref[...] = jnp.zeros_like(y_ref)


def _experts(block_expert, n_used, xs, w1, w3, w2, idx):
    n_blocks = xs.shape[0] // (MOE_TILE * TOKEN_ROWS)
    wmap = lambda i, be, nu: (idx, be[i], 0, 0)
    slots = pl.BlockSpec((MOE_TILE * TOKEN_ROWS, LANES), lambda i, be, nu: (i, 0))
    return pl.pallas_call(
        _expert_kernel,
        grid_spec=pltpu.PrefetchScalarGridSpec(
            num_scalar_prefetch=2,
            grid=(n_blocks,),
            in_specs=[
                slots,
                pl.BlockSpec((None, None, D_MODEL, D_FF_EXPERT), wmap),
                pl.BlockSpec((None, None, D_MODEL, D_FF_EXPERT), wmap),
                pl.BlockSpec((None, None, D_FF_EXPERT, D_MODEL), wmap),
            ],
            out_specs=slots,
        ),
        out_shape=jax.ShapeDtypeStruct(xs.shape, F32),
        compiler_params=_params("arbitrary"),
        name="moe_experts",
    )(block_expert, n_used, xs, w1, w3, w2)


def _combine_kernel(dest_ref, h_ref, gate_ref, y_ref, o_ref, buf_ref, sem):
    step = pl.program_id(0)

    def issue(s, slot):
        base = s * ROW_TILE

        def body(g, carry):
            r0 = g * ISSUE_TOKENS
            slots = [dest_ref[(base + r0) * TOP_K + j] for j in range(ISSUE_TOKENS * TOP_K)]
            for j, src in enumerate(slots):
                _token_copy(y_ref, src, buf_ref.at[slot, j % TOP_K], r0 + j // TOP_K,
                            sem.at[slot]).start()
            return carry

        lax.fori_loop(0, ROW_TILE // ISSUE_TOKENS, body, 0)

    @pl.when(step == 0)
    def _():
        issue(0, 0)

    for slot in range(2):
        @pl.when(step % 2 == slot)
        def _():
            @pl.when(step + 1 < pl.num_programs(0))
            def _():
                issue(step + 1, 1 - slot)

            for k in range(TOP_K):
                pltpu.make_async_copy(y_ref.at[pl.ds(0, ROW_TILE * TOKEN_ROWS), :],
                                      buf_ref.at[slot, k], sem.at[slot]).wait()
            gate = gate_ref[...]
            o_ref[...] = (h_ref[...]
                          + gate[:, 0:1] * _load_token_major(buf_ref.at[slot, 0], ROW_TILE)
                          + gate[:, 1:2] * _load_token_major(buf_ref.at[slot, 1], ROW_TILE))


def _combine(dest, h, gate, y):
    n = h.shape[0]
    return pl.pallas_call(
        _combine_kernel,
        grid_spec=pltpu.PrefetchScalarGridSpec(
            num_scalar_prefetch=1,
            grid=(n // ROW_TILE,),
            in_specs=[
                pl.BlockSpec((ROW_TILE, D_MODEL), lambda i, d: (i, 0)),
                pl.BlockSpec((ROW_TILE, TOP_K), lambda i, d: (i, 0)),
                pl.BlockSpec(memory_space=pl.ANY),
            ],
            out_specs=pl.BlockSpec((ROW_TILE, D_MODEL), lambda i, d: (i, 0)),
            scratch_shapes=[pltpu.VMEM((2, TOP_K, ROW_TILE * TOKEN_ROWS, LANES), F32),
                            pltpu.SemaphoreType.DMA((2,))],
        ),
        out_shape=jax.ShapeDtypeStruct((n, D_MODEL), F32),
        compiler_params=_params("arbitrary"),
        name="moe_combine",
    )(dest, h, gate, y)


def _route(logits, n_slots):
    n = logits.shape[0]
    eid = lax.broadcasted_iota(jnp.int32, logits.shape, 1)
    i1 = jnp.argmax(logits, axis=-1).astype(jnp.int32)
    l1 = jnp.max(logits, axis=-1)
    rest = jnp.where(eid == i1[:, None], -jnp.inf, logits)
    i2 = jnp.argmax(rest, axis=-1).astype(jnp.int32)
    l2 = jnp.max(rest, axis=-1)
    e2 = jnp.exp(l2 - l1)
    gate = jnp.stack([1.0 / (1.0 + e2), e2 / (1.0 + e2)], axis=-1)

    flat_e = jnp.stack([i1, i2], axis=-1).reshape(n * TOP_K)
    onehot = (flat_e[:, None] == jnp.arange(N_EXPERTS, dtype=jnp.int32)[None, :]).astype(jnp.int32)
    csum = jnp.cumsum(onehot, axis=0)
    rank = jnp.sum(csum * onehot, axis=-1) - 1
    counts = csum[-1]
    padded = (counts + MOE_TILE - 1) // MOE_TILE * MOE_TILE
    pad_end = jnp.cumsum(padded)
    pad_start = pad_end - padded
    dest = (pad_start[flat_e] + rank).astype(jnp.int32)
    n_blocks = n_slots // MOE_TILE
    block_expert = jnp.minimum(
        jnp.searchsorted(pad_end, jnp.arange(n_blocks, dtype=jnp.int32) * MOE_TILE, side="right"),
        N_EXPERTS - 1).astype(jnp.int32)
    n_used = (pad_end[-1:] // MOE_TILE).astype(jnp.int32)
    return dest, gate, block_expert, n_used


def _moe_ffn(h, u, logits, w1, w3, w2, idx):
    n = h.shape[0]
    n_slots = (n * TOP_K // MOE_TILE + N_EXPERTS) * MOE_TILE
    dest, gate, block_expert, n_used = _route(logits, n_slots)
    xs = _dispatch(dest, u, n_slots)
    y = _experts(block_expert, n_used, xs, w1, w3, w2, idx)
    return _combine(dest, h, gate, y)


def _final_kernel(h_ref, g_ref, o_ref):
    o_ref[...] = _rms(h_ref[...], g_ref[...])


def _final_norm(h3, gain, seq):
    batch = h3.shape[0]
    lead_blocks = LEAD // Q_BLOCK
    return pl.pallas_call(
        _final_kernel,
        grid=(seq // Q_BLOCK,),
        in_specs=[
            pl.BlockSpec((batch, Q_BLOCK, D_MODEL), lambda j: (0, j + lead_blocks, 0)),
            pl.BlockSpec((1, D_MODEL), lambda j: (0, 0)),
        ],
        out_specs=pl.BlockSpec((batch, Q_BLOCK, D_MODEL), lambda j: (0, j, 0)),
        out_shape=jax.ShapeDtypeStruct((batch, seq, D_MODEL), F32),
        compiler_params=_params("parallel"),
        name="final_norm",
    )(h3, gain)


def kernel(x, meta, rel_bias, norm_mix, w_in, hg_lb_logits, hg_norm_w, da_lambda, da_subln_w, w_out, norm_ffn, dense_w1, dense_w3, dense_w2, moe_router, moe_w1, moe_w3, moe_w2, final_norm):
    batch, seq, d = x.shape
    length = LEAD + seq
    h = jnp.concatenate([
        jnp.zeros((batch, LEAD - N_META, d), x.dtype),
        jnp.broadcast_to(meta[None].astype(x.dtype), (batch, N_META, d)),
        x], axis=1).reshape(batch * length, d)

    toe, far = _attn_bias_tables(rel_bias)
    lb_cum = jnp.cumsum(jax.nn.softmax(hg_lb_logits.astype(F32), axis=0), axis=0)
    lb_all = jnp.clip(lb_cum - lb_cum[0:1], 0.0, LB_MAX)
    log_lb = jnp.log(lb_all)
    log_1m_lb = jnp.log1p(-lb_all)

    w_in_b, w_out_b = w_in.astype(BF16), w_out.astype(BF16)
    dense_b = [w.astype(BF16) for w in (dense_w1, dense_w3, dense_w2)]
    moe_b = [w.astype(BF16) for w in (moe_w1, moe_w3, moe_w2)]

    for l in range(DEPTH):
        hg, da = _mix_in(h, norm_mix[l][None], w_in_b, l)
        o_hg = _hgrn(hg, log_lb[l][None], log_1m_lb[l][None], hg_norm_w[l][None], batch, length)
        lam_init = 0.8 - 0.6 * math.exp(-0.3 * l)
        lv = da_lambda[l].astype(F32)
        lam = jnp.exp(jnp.sum(lv[0] * lv[1])) - jnp.exp(jnp.sum(lv[2] * lv[3])) + lam_init
        cst = jnp.zeros((8, LANES), F32).at[0].set(lam).at[1].set(1.0 - lam_init)
        o_da = _attn(da, toe, far, cst, da_subln_w[l][None], batch, length)
        i = l // 2
        if l % 2 == 0:
            hn, u = _out_proj(h, o_hg, o_da, w_out_b, l, norm_ffn[l][None])
            h = _dense_ffn(hn, u, *dense_b, i)
        else:
            router = jnp.zeros((d, LANES), F32).at[:, :N_EXPERTS].set(moe_router[i].astype(F32))
            hn, u, lg = _out_proj(h, o_hg, o_da, w_out_b, l, norm_ffn[l][None], router)
            h = _moe_ffn(hn, u, lg[:, :N_EXPERTS], *moe_b, i)
    return _final_norm(h.reshape(batch, length, d), final_norm[None], seq)
```

```python
import functools
import math

import jax
import jax.numpy as jnp
import numpy as np
from jax import lax
from jax.experimental import pallas as pl
from jax.experimental.pallas import tpu as pltpu

D_MODEL = 1024
DEPTH = 4
N_META = 16
LEAD = 128
HG_WIDTH = 512
HG_HEADS = 4
HG_D = 128
HG_CHUNK = 64
DA_HEADS = 4
DA_DQK = 64
DA_DV = 128
Q_BLOCK = 128
REL_BUCKETS = 32
REL_MAX_DIST = 128
N_EXPERTS = 8
TOP_K = 2
D_FF_EXPERT = 3584
EPS = 1e-6
NEG = -1e30
LB_MAX = 0.999
HG_COLS = 4 * HG_WIDTH
DA_COLS = 3 * DA_HEADS * DA_DV
W_IN_COLS = HG_COLS + DA_COLS

LANES = 128
VMEM_LIMIT = 56 * 1024 * 1024

ROW_TILE = 256
HG_TILE = 128
MOE_TILE = 256
FF_CHUNK = 512

F32 = jnp.float32
BF16 = jnp.bfloat16


def _params(*sem):
    return pltpu.CompilerParams(dimension_semantics=sem, vmem_limit_bytes=VMEM_LIMIT)


def _dot(a, b):
    return jnp.dot(a, b, preferred_element_type=F32)


def _dot_nt(a, b):
    return lax.dot_general(a, b, (((1,), (1,)), ((), ())), preferred_element_type=F32)


def _dot_tn(a, b):
    return lax.dot_general(a, b, (((0,), (0,)), ((), ())), preferred_element_type=F32)


def _rms(x, gain):
    return x * lax.rsqrt(jnp.mean(x * x, axis=-1, keepdims=True) + EPS) * gain


TOKEN_ROWS = D_MODEL // LANES


def _store_token_major(ref, x):
    t = x.shape[0]
    for s in range(TOKEN_ROWS):
        ref[pl.ds(s, t, stride=TOKEN_ROWS), :] = x[:, s * LANES:(s + 1) * LANES]


def _load_token_major(ref, t):
    return jnp.concatenate(
        [ref[pl.ds(s, t, stride=TOKEN_ROWS), :] for s in range(TOKEN_ROWS)], axis=1)


def _silu(x):
    return x * (0.5 * jnp.tanh(0.5 * x) + 0.5)


def _mix_in_kernel(x_ref, g_ref, w_ref, hg_ref, da_ref):
    u = _rms(x_ref[...], g_ref[...]).astype(BF16)
    hg_ref[...] = _dot(u, w_ref[:, :HG_COLS])
    da_ref[...] = _dot(u, w_ref[:, HG_COLS:]).astype(BF16)


def _mix_in(h, gain, w, layer):
    n = h.shape[0]
    return pl.pallas_call(
        _mix_in_kernel,
        grid=(n // ROW_TILE,),
        in_specs=[
            pl.BlockSpec((ROW_TILE, D_MODEL), lambda i: (i, 0)),
            pl.BlockSpec((1, D_MODEL), lambda i: (0, 0)),
            pl.BlockSpec((None, D_MODEL, W_IN_COLS), lambda i: (layer, 0, 0)),
        ],
        out_specs=[
            pl.BlockSpec((ROW_TILE, HG_COLS), lambda i: (i, 0)),
            pl.BlockSpec((ROW_TILE, DA_COLS), lambda i: (i, 0)),
        ],
        out_shape=[
            jax.ShapeDtypeStruct((n, HG_COLS), F32),
            jax.ShapeDtypeStruct((n, DA_COLS), BF16),
        ],
        compiler_params=_params("parallel"),
        name="mix_in",
    )(h, gain, w)


HG_LEVELS = (32, 16, 8, 4, 2, 1)
N_SUMS = len(HG_LEVELS) + 2


def _hgrn_consts():
    c = HG_CHUNK
    t = np.arange(c)[:, None]
    j = np.arange(c)[None, :]
    sums = np.zeros((N_SUMS, c, c), np.float32)
    masks = np.zeros((len(HG_LEVELS) + 1, c, c), np.float32)
    sums[0] = j <= t
    masks[0] = np.eye(c)
    for li, w in enumerate(HG_LEVELS, start=1):
        ref = (t // (2 * w)) * (2 * w) + w
        sums[li] = np.where(t >= ref, (j > ref) & (j <= t), (j > t) & (j <= ref))
        masks[li] = (t // (2 * w) == j // (2 * w)) & (t % (2 * w) >= w) & (j % (2 * w) < w)
    sums[N_SUMS - 1] = j > t
    return sums.reshape(N_SUMS * c, c), masks


_HG_SUMS, _HG_MASKS = _hgrn_consts()


def _hgrn_kernel(hg_ref, loga_ref, log1m_ref, nw_ref, sums_ref, masks_ref, o_ref, state_ref):
    c_idx = pl.program_id(1)

    @pl.when(c_idx == 0)
    def _():
        state_ref[...] = jnp.zeros_like(state_ref)

    C = HG_CHUNK
    sums = sums_ref[...]
    nw = nw_ref[...]
    for ch in range(HG_TILE // C):
        rows = slice(ch * C, (ch + 1) * C)
        row_idx = c_idx * HG_TILE + ch * C + lax.broadcasted_iota(jnp.int32, (C, 1), 0)
        valid = row_idx >= (LEAD - N_META)
        for hd in range(HG_HEADS):
            cols = slice(hd * HG_D, (hd + 1) * HG_D)
            q = hg_ref[rows, hd * HG_D:(hd + 1) * HG_D]
            f = hg_ref[rows, HG_WIDTH + hd * HG_D:HG_WIDTH + (hd + 1) * HG_D]
            v = hg_ref[rows, 2 * HG_WIDTH + hd * HG_D:2 * HG_WIDTH + (hd + 1) * HG_D]
            g = hg_ref[rows, 3 * HG_WIDTH + hd * HG_D:3 * HG_WIDTH + (hd + 1) * HG_D]
            loga = loga_ref[:, cols]
            log1m = log1m_ref[:, cols]

            qf = _silu(q)
            ls = jnp.minimum(f, 0.0) - jnp.log1p(jnp.exp(-jnp.abs(f)))
            cc = log1m + ls
            lf = jnp.maximum(loga, cc) + jnp.log1p(jnp.exp(-jnp.abs(loga - cc)))
            kk = jnp.exp(cc - f)
            lf = jnp.where(valid, lf, 0.0)
            kk = jnp.where(valid, kk, 0.0)

            lf_hi = lf.astype(BF16)
            lf_lo = (lf - lf_hi.astype(F32)).astype(BF16)
            e = jnp.exp(_dot(sums, lf_hi) + _dot(sums, lf_lo))

            e_b = e[0:C]
            st = state_ref[hd]
            inter = _dot_nt((qf * e_b).astype(BF16), st.astype(BF16))
            scores = masks_ref[0] * _dot_nt(qf.astype(BF16), kk.astype(BF16))
            for li in range(1, len(HG_LEVELS) + 1):
                e_l = e[li * C:(li + 1) * C]
                scores += masks_ref[li] * _dot_nt((qf * e_l).astype(BF16), (kk * e_l).astype(BF16))
            vb = v.astype(BF16)
            o = inter + _dot(scores.astype(BF16), vb)

            e_end = e[(N_SUMS - 1) * C:N_SUMS * C]
            state_ref[hd] = st * e_b[C - 1:C, :] + _dot_tn(vb, (kk * e_end).astype(BF16))

            o = _rms(o, nw) * _silu(g)
            o_ref[rows, cols] = o.astype(o_ref.dtype)


def _hgrn(hg, loga, log1m, norm_w, batch, length):
    hg3 = hg.reshape(batch, length, HG_COLS)
    out = pl.pallas_call(
        _hgrn_kernel,
        grid=(batch, length // HG_TILE),
        in_specs=[
            pl.BlockSpec((None, HG_TILE, HG_COLS), lambda b, c: (b, c, 0)),
            pl.BlockSpec((1, HG_WIDTH), lambda b, c: (0, 0)),
            pl.BlockSpec((1, HG_WIDTH), lambda b, c: (0, 0)),
            pl.BlockSpec((1, HG_D), lambda b, c: (0, 0)),
            pl.BlockSpec(_HG_SUMS.shape, lambda b, c: (0, 0)),
            pl.BlockSpec(_HG_MASKS.shape, lambda b, c: (0, 0, 0)),
        ],
        out_specs=pl.BlockSpec((None, HG_TILE, HG_WIDTH), lambda b, c: (b, c, 0)),
        out_shape=jax.ShapeDtypeStruct((batch, length, HG_WIDTH), BF16),
        scratch_shapes=[pltpu.VMEM((HG_HEADS, HG_D, HG_D), F32)],
        compiler_params=_params("parallel", "arbitrary"),
        name="hgrn2",
    )(hg3, loga, log1m, norm_w, jnp.asarray(_HG_SUMS, BF16), jnp.asarray(_HG_MASKS, F32))
    return out.reshape(batch * length, HG_WIDTH)


def _attn_kernel(q_ref, k_ref, v_ref, toe_ref, far_ref, cst_ref, w_ref, o_ref, *, n_blocks):
    lam = cst_ref[0:1, :]
    post = cst_ref[1:2, :]
    far_bias = far_ref[0:1, 0:1]
    lane = lax.broadcasted_iota(jnp.int32, (Q_BLOCK, Q_BLOCK), 1)
    first_half = lane < DA_DQK
    key_ok0 = lane >= (LEAD - N_META)
    scale = DA_DQK ** -0.5

    def block_bias(kind, kb):
        bias = toe_ref[kind]
        if kb == 0:
            bias = jnp.where(key_ok0, bias, NEG)
        return bias

    for i in range(n_blocks):
        qi = q_ref[i * Q_BLOCK:(i + 1) * Q_BLOCK, :] * scale
        zero = jnp.zeros_like(qi)
        q_maps = (jnp.where(first_half, qi, zero), jnp.where(first_half, zero, qi))
        n_far = (i - 1) * Q_BLOCK if i >= 2 else 0
        if n_far:
            far_lane = lax.broadcasted_iota(jnp.int32, (1, n_far), 1)
            far_row = jnp.where(far_lane >= (LEAD - N_META), far_bias, NEG)
        probs = []
        for qm in q_maps:
            pieces = []
            if n_far:
                pieces.append(_dot_nt(qm, k_ref[0:n_far, :]) + far_row)
            if i >= 1:
                kb = i - 1
                pieces.append(_dot_nt(qm, k_ref[kb * Q_BLOCK:(kb + 1) * Q_BLOCK, :])
                              + block_bias(1, kb))
            pieces.append(_dot_nt(qm, k_ref[i * Q_BLOCK:(i + 1) * Q_BLOCK, :])
                          + block_bias(0, i))
            m = pieces[0].max(axis=-1, keepdims=True)
            for s in pieces[1:]:
                m = jnp.maximum(m, s.max(axis=-1, keepdims=True))
            pieces = [jnp.exp(s - m) for s in pieces]
            den = pieces[0].sum(axis=-1, keepdims=True)
            for p in pieces[1:]:
                den = den + p.sum(axis=-1, keepdims=True)
            probs.append((pieces, 1.0 / den))
        (p0, r0), (p1, r1) = probs
        r1 = r1 * lam[:, 0:1]
        starts = ([0] if n_far else []) + ([(i - 1) * Q_BLOCK] if i >= 1 else []) + [i * Q_BLOCK]
        o = None
        for a0, a1, ks in zip(p0, p1, starts):
            a = (a0 * r0 - a1 * r1).astype(BF16)
            part = _dot(a, v_ref[ks:ks + a.shape[1], :])
            o = part if o is None else o + part
        o = _rms(o, w_ref[...]) * post
        o_ref[i * Q_BLOCK:(i + 1) * Q_BLOCK, :] = o.astype(o_ref.dtype)


def _attn(da, toe, far, cst, subln_w, batch, length):
    da3 = da.reshape(batch, length, DA_COLS)
    hw = DA_HEADS
    out = pl.pallas_call(
        functools.partial(_attn_kernel, n_blocks=length // Q_BLOCK),
        grid=(batch, DA_HEADS),
        in_specs=[
            pl.BlockSpec((None, length, DA_DV), lambda b, h: (b, 0, h)),
            pl.BlockSpec((None, length, DA_DV), lambda b, h: (b, 0, hw + h)),
            pl.BlockSpec((None, length, DA_DV), lambda b, h: (b, 0, 2 * hw + h)),
            pl.BlockSpec((None, 2, Q_BLOCK, Q_BLOCK), lambda b, h: (h, 0, 0, 0)),
            pl.BlockSpec((None, 1, LANES), lambda b, h: (h, 0, 0)),
            pl.BlockSpec((8, LANES), lambda b, h: (0, 0)),
            pl.BlockSpec((1, DA_DV), lambda b, h: (0, 0)),
        ],
        out_specs=pl.BlockSpec((None, length, DA_DV), lambda b, h: (b, 0, h)),
        out_shape=jax.ShapeDtypeStruct((batch, length, DA_HEADS * DA_DV), BF16),
        compiler_params=_params("parallel", "parallel"),
        name="diff_attn",
    )(da3, da3, da3, toe, far, cst, subln_w)
    return out.reshape(batch * length, DA_HEADS * DA_DV)


def _t5_bucket(dist):
    n = jnp.maximum(dist, 0)
    max_exact = REL_BUCKETS // 2
    nf = jnp.maximum(n, max_exact).astype(F32)
    large = max_exact + (jnp.log(nf / max_exact) / math.log(REL_MAX_DIST / max_exact)
                         * (REL_BUCKETS - max_exact)).astype(jnp.int32)
    large = jnp.minimum(large, REL_BUCKETS - 1)
    return jnp.where(n < max_exact, n, large)


def _attn_bias_tables(rel_bias):
    tab = rel_bias.astype(F32)
    qi = jnp.arange(Q_BLOCK, dtype=jnp.int32)[:, None]
    ki = jnp.arange(Q_BLOCK, dtype=jnp.int32)[None, :]

    def lookup(bucket):
        onehot = bucket[None, :, :, None] == jnp.arange(REL_BUCKETS, dtype=jnp.int32)
        return jnp.sum(jnp.where(onehot, tab.T[:, None, None, :], 0.0), axis=-1)

    diag = jnp.where((ki <= qi)[None], lookup(_t5_bucket(qi - ki)), NEG)
    prev = lookup(_t5_bucket(qi - ki + Q_BLOCK))
    toe = jnp.stack([diag, prev], axis=1)
    far = jnp.broadcast_to(tab[REL_BUCKETS - 1][:, None, None], (DA_HEADS, 1, LANES))
    return toe, far


def _out_proj_kernel(h_ref, ohg_ref, oda_ref, wo_ref, g_ref, *rest, with_router):
    if with_router:
        router_ref, hn_ref, u_ref, lg_ref = rest
    else:
        hn_ref, u_ref = rest
    hn = (h_ref[...] + _dot(ohg_ref[...], wo_ref[:HG_WIDTH, :])
          + _dot(oda_ref[...], wo_ref[HG_WIDTH:, :]))
    hn_ref[...] = hn
    u = _rms(hn, g_ref[...])
    if with_router:
        _store_token_major(u_ref, u)
        lg_ref[...] = jnp.dot(u, router_ref[...], precision=lax.Precision.HIGHEST,
                              preferred_element_type=F32)
    else:
        u_ref[...] = u.astype(u_ref.dtype)


def _out_proj(h, o_hg, o_da, wo, layer, gain, router=None):
    n = h.shape[0]
    with_router = router is not None
    row = lambda i: (i, 0)
    full = lambda i: (0, 0)
    in_specs = [
        pl.BlockSpec((ROW_TILE, D_MODEL), row),
        pl.BlockSpec((ROW_TILE, HG_WIDTH), row),
        pl.BlockSpec((ROW_TILE, DA_HEADS * DA_DV), row),
        pl.BlockSpec((None,) + wo.shape[1:], lambda i: (layer, 0, 0)),
        pl.BlockSpec((1, D_MODEL), full),
    ]
    out_specs = [pl.BlockSpec((ROW_TILE, D_MODEL), row), pl.BlockSpec((ROW_TILE, D_MODEL), row)]
    out_shape = [jax.ShapeDtypeStruct((n, D_MODEL), F32), jax.ShapeDtypeStruct((n, D_MODEL), BF16)]
    args = [h, o_hg, o_da, wo, gain]
    if with_router:
        out_specs[1] = pl.BlockSpec((ROW_TILE * TOKEN_ROWS, LANES), row)
        out_shape[1] = jax.ShapeDtypeStruct((n * TOKEN_ROWS, LANES), F32)
        in_specs.append(pl.BlockSpec(router.shape, full))
        out_specs.append(pl.BlockSpec((ROW_TILE, LANES), row))
        out_shape.append(jax.ShapeDtypeStruct((n, LANES), F32))
        args.append(router)
    return pl.pallas_call(
        functools.partial(_out_proj_kernel, with_router=with_router),
        grid=(n // ROW_TILE,),
        in_specs=in_specs,
        out_specs=out_specs,
        out_shape=out_shape,
        compiler_params=_params("parallel"),
        name="out_proj_router" if with_router else "out_proj",
    )(*args)


def _dense_ffn_kernel(h_ref, u_ref, w1_ref, w3_ref, w2_ref, o_ref):
    u = u_ref[...]
    a = _dot(u, w1_ref[...])
    act = (_silu(a) * _dot(u, w3_ref[...])).astype(BF16)
    o_ref[...] = h_ref[...] + _dot(act, w2_ref[...])


def _dense_ffn(h, u, w1, w3, w2, idx):
    n = h.shape[0]
    row = lambda i: (i, 0)
    stacked = lambda i: (idx, 0, 0)
    return pl.pallas_call(
        _dense_ffn_kernel,
        grid=(n // ROW_TILE,),
        in_specs=[
            pl.BlockSpec((ROW_TILE, D_MODEL), row),
            pl.BlockSpec((ROW_TILE, D_MODEL), row),
            pl.BlockSpec((None,) + w1.shape[1:], stacked),
            pl.BlockSpec((None,) + w3.shape[1:], stacked),
            pl.BlockSpec((None,) + w2.shape[1:], stacked),
        ],
        out_specs=pl.BlockSpec((ROW_TILE, D_MODEL), row),
        out_shape=jax.ShapeDtypeStruct((n, D_MODEL), F32),
        compiler_params=_params("parallel"),
        name="dense_ffn",
    )(h, u, w1, w3, w2)


DISPATCH_TILE = 1024
ISSUE_TOKENS = 8


def _token_copy(src_ref, src_tok, dst_ref, dst_tok, sem):
    src = pl.multiple_of(src_tok * TOKEN_ROWS, TOKEN_ROWS)
    dst = pl.multiple_of(dst_tok * TOKEN_ROWS, TOKEN_ROWS)
    return pltpu.make_async_copy(src_ref.at[pl.ds(src, TOKEN_ROWS), :],
                                 dst_ref.at[pl.ds(dst, TOKEN_ROWS), :], sem)


def _dispatch_kernel(dest_ref, u_ref, xs_in_ref, xs_ref, sem):
    del xs_in_ref
    base = pl.program_id(0) * DISPATCH_TILE

    def start(g, carry):
        r0 = g * ISSUE_TOKENS
        slots = [dest_ref[(base + r0) * TOP_K + j] for j in range(ISSUE_TOKENS * TOP_K)]
        for j, slot in enumerate(slots):
            _token_copy(u_ref, r0 + j // TOP_K, xs_ref, slot, sem).start()
        return carry

    lax.fori_loop(0, DISPATCH_TILE // ISSUE_TOKENS, start, 0)
    rows = DISPATCH_TILE * TOKEN_ROWS
    for _ in range(TOP_K):
        pltpu.make_async_copy(u_ref, xs_ref.at[pl.ds(0, rows), :], sem).wait()


def _dispatch(dest, u_tm, n_slots):
    n = dest.shape[0] // TOP_K
    zeros = jnp.zeros((n_slots * TOKEN_ROWS, LANES), u_tm.dtype)
    return pl.pallas_call(
        _dispatch_kernel,
        grid_spec=pltpu.PrefetchScalarGridSpec(
            num_scalar_prefetch=1,
            grid=(n // DISPATCH_TILE,),
            in_specs=[pl.BlockSpec((DISPATCH_TILE * TOKEN_ROWS, LANES), lambda i, d: (i, 0)),
                      pl.BlockSpec(memory_space=pl.ANY)],
            out_specs=pl.BlockSpec(memory_space=pl.ANY),
            scratch_shapes=[pltpu.SemaphoreType.DMA(())],
        ),
        out_shape=jax.ShapeDtypeStruct(zeros.shape, u_tm.dtype),
        input_output_aliases={2: 0},
        compiler_params=_params("arbitrary"),
        name="moe_dispatch",
    )(dest, u_tm, zeros)


def _expert_kernel(be_ref, nused_ref, xs_ref, w1_ref, w3_ref, w2_ref, y_ref):
    del be_ref
    blk = pl.program_id(0)

    @pl.when(blk < nused_ref[0])
    def _():
        x = _load_token_major(xs_ref, MOE_TILE).astype(BF16)
        acc = jnp.zeros((MOE_TILE, D_MODEL), F32)
        for c0 in range(0, D_FF_EXPERT, FF_CHUNK):
            a = _dot(x, w1_ref[:, c0:c0 + FF_CHUNK])
            act = (_silu(a) * _dot(x, w3_ref[:, c0:c0 + FF_CHUNK])).astype(BF16)
            acc = acc + _dot(act, w2_ref[c0:c0 + FF_CHUNK, :])
        _store_token_major(y_ref, acc)

    @pl.when(blk >= nused_ref[0])
    def _():
        y_ref[...] = jnp.zeros_like(y_ref)


def _experts(block_expert, n_used, xs, w1, w3, w2, idx):
    n_blocks = xs.shape[0] // (MOE_TILE * TOKEN_ROWS)
    wmap = lambda i, be, nu: (idx, be[i], 0, 0)
    slots = pl.BlockSpec((MOE_TILE * TOKEN_ROWS, LANES), lambda i, be, nu: (i, 0))
    return pl.pallas_call(
        _expert_kernel,
        grid_spec=pltpu.PrefetchScalarGridSpec(
            num_scalar_prefetch=2,
            grid=(n_blocks,),
            in_specs=[
                slots,
                pl.BlockSpec((None, None, D_MODEL, D_FF_EXPERT), wmap),
                pl.BlockSpec((None, None, D_MODEL, D_FF_EXPERT), wmap),
                pl.BlockSpec((None, None, D_FF_EXPERT, D_MODEL), wmap),
            ],
            out_specs=slots,
        ),
        out_shape=jax.ShapeDtypeStruct(xs.shape, F32),
        compiler_params=_params("arbitrary"),
        name="moe_experts",
    )(block_expert, n_used, xs, w1, w3, w2)


def _combine_kernel(dest_ref, h_ref, gate_ref, y_ref, o_ref, buf_ref, sem):
    step = pl.program_id(0)

    def issue(s, slot):
        base = s * ROW_TILE

        def body(g, carry):
            r0 = g * ISSUE_TOKENS
            slots = [dest_ref[(base + r0) * TOP_K + j] for j in range(ISSUE_TOKENS * TOP_K)]
            for j, src in enumerate(slots):
                _token_copy(y_ref, src, buf_ref.at[slot, j % TOP_K], r0 + j // TOP_K,
                            sem.at[slot]).start()
            return carry

        lax.fori_loop(0, ROW_TILE // ISSUE_TOKENS, body, 0)

    @pl.when(step == 0)
    def _():
        issue(0, 0)

    for slot in range(2):
        @pl.when(step % 2 == slot)
        def _():
            @pl.when(step + 1 < pl.num_programs(0))
            def _():
                issue(step + 1, 1 - slot)

            for k in range(TOP_K):
                pltpu.make_async_copy(y_ref.at[pl.ds(0, ROW_TILE * TOKEN_ROWS), :],
                                      buf_ref.at[slot, k], sem.at[slot]).wait()
            gate = gate_ref[...]
            o_ref[...] = (h_ref[...]
                          + gate[:, 0:1] * _load_token_major(buf_ref.at[slot, 0], ROW_TILE)
                          + gate[:, 1:2] * _load_token_major(buf_ref.at[slot, 1], ROW_TILE))


def _combine(dest, h, gate, y):
    n = h.shape[0]
    return pl.pallas_call(
        _combine_kernel,
        grid_spec=pltpu.PrefetchScalarGridSpec(
            num_scalar_prefetch=1,
            grid=(n // ROW_TILE,),
            in_specs=[
                pl.BlockSpec((ROW_TILE, D_MODEL), lambda i, d: (i, 0)),
                pl.BlockSpec((ROW_TILE, TOP_K), lambda i, d: (i, 0)),
                pl.BlockSpec(memory_space=pl.ANY),
            ],
            out_specs=pl.BlockSpec((ROW_TILE, D_MODEL), lambda i, d: (i, 0)),
            scratch_shapes=[pltpu.VMEM((2, TOP_K, ROW_TILE * TOKEN_ROWS, LANES), F32),
                            pltpu.SemaphoreType.DMA((2,))],
        ),
        out_shape=jax.ShapeDtypeStruct((n, D_MODEL), F32),
        compiler_params=_params("arbitrary"),
        name="moe_combine",
    )(dest, h, gate, y)


def _route(logits, n_slots):
    n = logits.shape[0]
    eid = lax.broadcasted_iota(jnp.int32, logits.shape, 1)
    i1 = jnp.argmax(logits, axis=-1).astype(jnp.int32)
    l1 = jnp.max(logits, axis=-1)
    rest = jnp.where(eid == i1[:, None], -jnp.inf, logits)
    i2 = jnp.argmax(rest, axis=-1).astype(jnp.int32)
    l2 = jnp.max(rest, axis=-1)
    e2 = jnp.exp(l2 - l1)
    gate = jnp.stack([1.0 / (1.0 + e2), e2 / (1.0 + e2)], axis=-1)

    flat_e = jnp.stack([i1, i2], axis=-1).reshape(n * TOP_K)
    onehot = (flat_e[:, None] == jnp.arange(N_EXPERTS, dtype=jnp.int32)[None, :]).astype(jnp.int32)
    csum = jnp.cumsum(onehot, axis=0)
    rank = jnp.sum(csum * onehot, axis=-1) - 1
    counts = csum[-1]
    padded = (counts + MOE_TILE - 1) // MOE_TILE * MOE_TILE
    pad_end = jnp.cumsum(padded)
    pad_start = pad_end - padded
    dest = (pad_start[flat_e] + rank).astype(jnp.int32)
    n_blocks = n_slots // MOE_TILE
    block_expert = jnp.minimum(
        jnp.searchsorted(pad_end, jnp.arange(n_blocks, dtype=jnp.int32) * MOE_TILE, side="right"),
        N_EXPERTS - 1).astype(jnp.int32)
    n_used = (pad_end[-1:] // MOE_TILE).astype(jnp.int32)
    return dest, gate, block_expert, n_used


def _moe_ffn(h, u, logits, w1, w3, w2, idx):
    n = h.shape[0]
    n_slots = (n * TOP_K // MOE_TILE + N_EXPERTS) * MOE_TILE
    dest, gate, block_expert, n_used = _route(logits, n_slots)
    xs = _dispatch(dest, u, n_slots)
    y = _experts(block_expert, n_used, xs, w1, w3, w2, idx)
    return _combine(dest, h, gate, y)


def _final_kernel(h_ref, g_ref, o_ref):
    o_ref[...] = _rms(h_ref[...], g_ref[...])


def _final_norm(h3, gain, seq):
    batch = h3.shape[0]
    lead_blocks = LEAD // Q_BLOCK
    return pl.pallas_call(
        _final_kernel,
        grid=(seq // Q_BLOCK,),
        in_specs=[
            pl.BlockSpec((batch, Q_BLOCK, D_MODEL), lambda j: (0, j + lead_blocks, 0)),
            pl.BlockSpec((1, D_MODEL), lambda j: (0, 0)),
        ],
        out_specs=pl.BlockSpec((batch, Q_BLOCK, D_MODEL), lambda j: (0, j, 0)),
        out_shape=jax.ShapeDtypeStruct((batch, seq, D_MODEL), F32),
        compiler_params=_params("parallel"),
        name="final_norm",
    )(h3, gain)


def kernel(x, meta, rel_bias, norm_mix, w_in, hg_lb_logits, hg_norm_w, da_lambda, da_subln_w, w_out, norm_ffn, dense_w1, dense_w3, dense_w2, moe_router, moe_w1, moe_w3, moe_w2, final_norm):
    batch, seq, d = x.shape
    length = LEAD + seq
    h = jnp.concatenate([
        jnp.zeros((batch, LEAD - N_META, d), x.dtype),
        jnp.broadcast_to(meta[None].astype(x.dtype), (batch, N_META, d)),
        x], axis=1).reshape(batch * length, d)

    toe, far = _attn_bias_tables(rel_bias)
    lb_cum = jnp.cumsum(jax.nn.softmax(hg_lb_logits.astype(F32), axis=0), axis=0)
    lb_all = jnp.clip(lb_cum - lb_cum[0:1], 0.0, LB_MAX)
    log_lb = jnp.log(lb_all)
    log_1m_lb = jnp.log1p(-lb_all)

    w_in_b, w_out_b = w_in.astype(BF16), w_out.astype(BF16)
    dense_b = [w.astype(BF16) for w in (dense_w1, dense_w3, dense_w2)]
    moe_b = [w.astype(BF16) for w in (moe_w1, moe_w3, moe_w2)]

    for l in range(DEPTH):
        hg, da = _mix_in(h, norm_mix[l][None], w_in_b, l)
        o_hg = _hgrn(hg, log_lb[l][None], log_1m_lb[l][None], hg_norm_w[l][None], batch, length)
        lam_init = 0.8 - 0.6 * math.exp(-0.3 * l)
        lv = da_lambda[l].astype(F32)
        lam = jnp.exp(jnp.sum(lv[0] * lv[1])) - jnp.exp(jnp.sum(lv[2] * lv[3])) + lam_init
        cst = jnp.zeros((8, LANES), F32).at[0].set(lam).at[1].set(1.0 - lam_init)
        o_da = _attn(da, toe, far, cst, da_subln_w[l][None], batch, length)
        i = l // 2
        if l % 2 == 0:
            hn, u = _out_proj(h, o_hg, o_da, w_out_b, l, norm_ffn[l][None])
            h = _dense_ffn(hn, u, *dense_b, i)
        else:
            router = jnp.zeros((d, LANES), F32).at[:, :N_EXPERTS].set(moe_router[i].astype(F32))
            hn, u, lg = _out_proj(h, o_hg, o_da, w_out_b, l, norm_ffn[l][None], router)
            h = _moe_ffn(hn, u, lg[:, :N_EXPERTS], *moe_b, i)
    return _final_norm(h.reshape(batch, length, d), final_norm[None], seq)
```

```python
import functools
import math

import jax
import jax.numpy as jnp
import numpy as np
from jax import lax
from jax.experimental import pallas as pl
from jax.experimental.pallas import tpu as pltpu

D_MODEL = 1024
DEPTH = 4
N_META = 16
LEAD = 128
HG_WIDTH = 512
HG_HEADS = 4
HG_D = 128
HG_CHUNK = 64
DA_HEADS = 4
DA_DQK = 64
DA_DV = 128
Q_BLOCK = 128
KEY_TILE = 2 * Q_BLOCK
REL_BUCKETS = 32
REL_MAX_DIST = 128
N_EXPERTS = 8
TOP_K = 2
D_FF_EXPERT = 3584
EPS = 1e-6
NEG = -1e30
LB_MAX = 0.999
HG_COLS = 4 * HG_WIDTH
DA_COLS = 3 * DA_HEADS * DA_DV
W_IN_COLS = HG_COLS + DA_COLS

LANES = 128
VMEM_LIMIT = 56 * 1024 * 1024

ROW_TILE = 256
HG_TILE = 128
MOE_TILE = 256
FF_CHUNK = 512

F32 = jnp.float32
BF16 = jnp.bfloat16


def _params(*sem):
    return pltpu.CompilerParams(dimension_semantics=sem, vmem_limit_bytes=VMEM_LIMIT)


def _dot(a, b):
    return jnp.dot(a, b, preferred_element_type=F32)


def _dot_nt(a, b):
    return lax.dot_general(a, b, (((1,), (1,)), ((), ())), preferred_element_type=F32)


def _dot_tn(a, b):
    return lax.dot_general(a, b, (((0,), (0,)), ((), ())), preferred_element_type=F32)


def _rms(x, gain):
    return x * lax.rsqrt(jnp.mean(x * x, axis=-1, keepdims=True) + EPS) * gain


TOKEN_ROWS = D_MODEL // LANES


def _store_token_major(ref, x):
    t = x.shape[0]
    for s in range(TOKEN_ROWS):
        ref[pl.ds(s, t, stride=TOKEN_ROWS), :] = x[:, s * LANES:(s + 1) * LANES]


def _load_token_major(ref, t):
    return jnp.concatenate(
        [ref[pl.ds(s, t, stride=TOKEN_ROWS), :] for s in range(TOKEN_ROWS)], axis=1)


def _silu(x):
    return x * (0.5 * jnp.tanh(0.5 * x) + 0.5)


def _mix_in_kernel(x_ref, g_ref, w_ref, hg_ref, da_ref):
    u = _rms(x_ref[...], g_ref[...]).astype(BF16)
    hg_ref[...] = _dot(u, w_ref[:, :HG_COLS])
    da_ref[...] = _dot(u, w_ref[:, HG_COLS:]).astype(BF16)


def _mix_in(h, gain, w, layer):
    n = h.shape[0]
    return pl.pallas_call(
        _mix_in_kernel,
        grid=(n // ROW_TILE,),
        in_specs=[
            pl.BlockSpec((ROW_TILE, D_MODEL), lambda i: (i, 0)),
            pl.BlockSpec((1, D_MODEL), lambda i: (0, 0)),
            pl.BlockSpec((None, D_MODEL, W_IN_COLS), lambda i: (layer, 0, 0)),
        ],
        out_specs=[
            pl.BlockSpec((ROW_TILE, HG_COLS), lambda i: (i, 0)),
            pl.BlockSpec((ROW_TILE, DA_COLS), lambda i: (i, 0)),
        ],
        out_shape=[
            jax.ShapeDtypeStruct((n, HG_COLS), F32),
            jax.ShapeDtypeStruct((n, DA_COLS), BF16),
        ],
        compiler_params=_params("parallel"),
        name="mix_in",
    )(h, gain, w)


HG_LEVELS = (32, 16, 8, 4, 2, 1)
N_SUMS = len(HG_LEVELS) + 2


def _hgrn_consts():
    c = HG_CHUNK
    t = np.arange(c)[:, None]
    j = np.arange(c)[None, :]
    sums = np.zeros((N_SUMS, c, c), np.float32)
    masks = np.zeros((len(HG_LEVELS) + 1, c, c), np.float32)
    sums[0] = j <= t
    masks[0] = np.eye(c)
    for li, w in enumerate(HG_LEVELS, start=1):
        ref = (t // (2 * w)) * (2 * w) + w
        sums[li] = np.where(t >= ref, (j > ref) & (j <= t), (j > t) & (j <= ref))
        masks[li] = (t // (2 * w) == j // (2 * w)) & (t % (2 * w) >= w) & (j % (2 * w) < w)
    sums[N_SUMS - 1] = j > t
    return sums.reshape(N_SUMS * c, c), masks


_HG_SUMS, _HG_MASKS = _hgrn_consts()


def _hgrn_kernel(hg_ref, loga_ref, log1m_ref, nw_ref, sums_ref, masks_ref, o_ref, state_ref):
    c_idx = pl.program_id(1)

    @pl.when(c_idx == 0)
    def _():
        state_ref[...] = jnp.zeros_like(state_ref)

    C = HG_CHUNK
    sums = sums_ref[...]
    nw = nw_ref[...]
    for ch in range(HG_TILE // C):
        rows = slice(ch * C, (ch + 1) * C)
        row_idx = c_idx * HG_TILE + ch * C + lax.broadcasted_iota(jnp.int32, (C, 1), 0)
        valid = row_idx >= (LEAD - N_META)
        for hd in range(HG_HEADS):
            cols = slice(hd * HG_D, (hd + 1) * HG_D)
            q = hg_ref[rows, hd * HG_D:(hd + 1) * HG_D]
            f = hg_ref[rows, HG_WIDTH + hd * HG_D:HG_WIDTH + (hd + 1) * HG_D]
            v = hg_ref[rows, 2 * HG_WIDTH + hd * HG_D:2 * HG_WIDTH + (hd + 1) * HG_D]
            g = hg_ref[rows, 3 * HG_WIDTH + hd * HG_D:3 * HG_WIDTH + (hd + 1) * HG_D]
            loga = loga_ref[:, cols]
            log1m = log1m_ref[:, cols]

            qf = _silu(q)
            ls = jnp.minimum(f, 0.0) - jnp.log1p(jnp.exp(-jnp.abs(f)))
            cc = log1m + ls
            lf = jnp.maximum(loga, cc) + jnp.log1p(jnp.exp(-jnp.abs(loga - cc)))
            kk = jnp.exp(cc - f)
            lf = jnp.where(valid, lf, 0.0)
            kk = jnp.where(valid, kk, 0.0)

            lf_hi = lf.astype(BF16)
            lf_lo = (lf - lf_hi.astype(F32)).astype(BF16)
            e = jnp.exp(_dot(sums, lf_hi) + _dot(sums, lf_lo))

            e_b = e[0:C]
            st = state_ref[hd]
            inter = _dot_nt((qf * e_b).astype(BF16), st.astype(BF16))
            scores = masks_ref[0] * _dot_nt(qf.astype(BF16), kk.astype(BF16))
            for li in range(1, len(HG_LEVELS) + 1):
                e_l = e[li * C:(li + 1) * C]
                scores += masks_ref[li] * _dot_nt((qf * e_l).astype(BF16), (kk * e_l).astype(BF16))
            vb = v.astype(BF16)
            o = inter + _dot(scores.astype(BF16), vb)

            e_end = e[(N_SUMS - 1) * C:N_SUMS * C]
            state_ref[hd] = st * e_b[C - 1:C, :] + _dot_tn(vb, (kk * e_end).astype(BF16))

            o = _rms(o, nw) * _silu(g)
            o_ref[rows, cols] = o.astype(o_ref.dtype)


def _hgrn(hg, loga, log1m, norm_w, batch, length):
    hg3 = hg.reshape(batch, length, HG_COLS)
    out = pl.pallas_call(
        _hgrn_kernel,
        grid=(batch, length // HG_TILE),
        in_specs=[
            pl.BlockSpec((None, HG_TILE, HG_COLS), lambda b, c: (b, c, 0)),
            pl.BlockSpec((1, HG_WIDTH), lambda b, c: (0, 0)),
            pl.BlockSpec((1, HG_WIDTH), lambda b, c: (0, 0)),
            pl.BlockSpec((1, HG_D), lambda b, c: (0, 0)),
            pl.BlockSpec(_HG_SUMS.shape, lambda b, c: (0, 0)),
            pl.BlockSpec(_HG_MASKS.shape, lambda b, c: (0, 0, 0)),
        ],
        out_specs=pl.BlockSpec((None, HG_TILE, HG_WIDTH), lambda b, c: (b, c, 0)),
        out_shape=jax.ShapeDtypeStruct((batch, length, HG_WIDTH), BF16),
        scratch_shapes=[pltpu.VMEM((HG_HEADS, HG_D, HG_D), F32)],
        compiler_params=_params("parallel", "arbitrary"),
        name="hgrn2",
    )(hg3, loga, log1m, norm_w, jnp.asarray(_HG_SUMS, BF16), jnp.asarray(_HG_MASKS, F32))
    return out.reshape(batch * length, HG_WIDTH)


def _attn_kernel(q_ref, k_ref, v_ref, toe_ref, cst_ref, w_ref, o_ref, s_ref, *, n_blocks):
    lam = cst_ref[0:1, 0:1]
    post = cst_ref[1:2, :]
    lane = lax.broadcasted_iota(jnp.int32, (Q_BLOCK, Q_BLOCK), 1)
    first_half = lane < DA_DQK
    key_ok0 = lane >= (LEAD - N_META)
    inert_bias = jnp.where(key_ok0, 0.0, NEG)
    scale = DA_DQK ** -0.5

    def near_bias(kind, kb):
        bias = toe_ref[kind]
        if kb == 0:
            bias = jnp.where(key_ok0, bias, NEG)
        return bias

    def slabs(x):
        return [x[:, c:c + Q_BLOCK] for c in range(0, x.shape[1], Q_BLOCK)]

    def both_maps(bias):
        return jnp.concatenate([bias, bias], axis=0)

    def stacked_q(i):
        qi = q_ref[i * Q_BLOCK:(i + 1) * Q_BLOCK, :] * scale
        zero = jnp.zeros_like(qi)
        return jnp.concatenate([jnp.where(first_half, qi, zero), jnp.where(first_half, zero, qi)],
                               axis=0)

    def key_tiles(i):
        tiles = []
        far_end = max(i - 1, 0)
        if far_end >= 1:
            tiles.append((0, Q_BLOCK, both_maps(inert_bias)))
        kb = 1
        while kb < far_end:
            width = KEY_TILE if kb + KEY_TILE // Q_BLOCK <= far_end else Q_BLOCK
            tiles.append((kb * Q_BLOCK, width, None))
            kb += width // Q_BLOCK
        if i >= 1:
            tiles.append(((i - 1) * Q_BLOCK, Q_BLOCK, both_maps(near_bias(1, i - 1))))
        tiles.append((i * Q_BLOCK, Q_BLOCK, both_maps(near_bias(0, i))))
        return tiles

    def scores(q2, tile):
        start, width, bias = tile
        s = _dot_nt(q2, k_ref[start:start + width, :])
        return s if bias is None else s + bias

    def sweep_scores(i):
        q2 = stacked_q(i)
        m_acc = None
        for tile in key_tiles(i):
            s = scores(q2, tile)
            s_ref[i % 2, :, tile[0]:tile[0] + tile[1]] = s
            for slab in slabs(s):
                m_acc = slab if m_acc is None else jnp.maximum(m_acc, slab)
        return m_acc.max(axis=-1, keepdims=True)

    row_max = sweep_scores(0)
    for i in range(n_blocks):
        m = row_max
        if i + 1 < n_blocks:
            row_max = sweep_scores(i + 1)
        l_acc = o_acc = None
        for start, width, _ in key_tiles(i):
            p = jnp.exp(s_ref[i % 2, :, start:start + width] - m)
            for slab in slabs(p):
                l_acc = slab if l_acc is None else l_acc + slab
            part = _dot(p.astype(BF16), v_ref[start:start + width, :])
            o_acc = part if o_acc is None else o_acc + part
        o2 = o_acc * (1.0 / l_acc.sum(axis=-1, keepdims=True))
        o = o2[:Q_BLOCK] - lam * o2[Q_BLOCK:]
        o = _rms(o, w_ref[...]) * post
        o_ref[i * Q_BLOCK:(i + 1) * Q_BLOCK, :] = o.astype(o_ref.dtype)


def _attn(da, toe, cst, subln_w, batch, length):
    da3 = da.reshape(batch, length, DA_COLS)
    hw = DA_HEADS
    out = pl.pallas_call(
        functools.partial(_attn_kernel, n_blocks=length // Q_BLOCK),
        grid=(batch, DA_HEADS),
        in_specs=[
            pl.BlockSpec((None, length, DA_DV), lambda b, h: (b, 0, h)),
            pl.BlockSpec((None, length, DA_DV), lambda b, h: (b, 0, hw + h)),
            pl.BlockSpec((None, length, DA_DV), lambda b, h: (b, 0, 2 * hw + h)),
            pl.BlockSpec((None, 2, Q_BLOCK, Q_BLOCK), lambda b, h: (h, 0, 0, 0)),
            pl.BlockSpec((8, LANES), lambda b, h: (0, 0)),
            pl.BlockSpec((1, DA_DV), lambda b, h: (0, 0)),
        ],
        out_specs=pl.BlockSpec((None, length, DA_DV), lambda b, h: (b, 0, h)),
        out_shape=jax.ShapeDtypeStruct((batch, length, DA_HEADS * DA_DV), BF16),
        scratch_shapes=[pltpu.VMEM((2, 2 * Q_BLOCK, length), F32)],
        compiler_params=_params("parallel", "parallel"),
        name="diff_attn",
    )(da3, da3, da3, toe, cst, subln_w)
    return out.reshape(batch * length, DA_HEADS * DA_DV)


def _t5_bucket(dist):
    n = jnp.maximum(dist, 0)
    max_exact = REL_BUCKETS // 2
    nf = jnp.maximum(n, max_exact).astype(F32)
    large = max_exact + (jnp.log(nf / max_exact) / math.log(REL_MAX_DIST / max_exact)
                         * (REL_BUCKETS - max_exact)).astype(jnp.int32)
    large = jnp.minimum(large, REL_BUCKETS - 1)
    return jnp.where(n < max_exact, n, large)


def _attn_bias_tables(rel_bias):
    tab = rel_bias.astype(F32)
    qi = jnp.arange(Q_BLOCK, dtype=jnp.int32)[:, None]
    ki = jnp.arange(Q_BLOCK, dtype=jnp.int32)[None, :]

    def lookup(bucket):
        onehot = bucket[None, :, :, None] == jnp.arange(REL_BUCKETS, dtype=jnp.int32)
        return jnp.sum(jnp.where(onehot, tab.T[:, None, None, :], 0.0), axis=-1)

    far = tab[REL_BUCKETS - 1][:, None, None]
    diag = jnp.where((ki <= qi)[None], lookup(_t5_bucket(qi - ki)) - far, NEG)
    prev = lookup(_t5_bucket(qi - ki + Q_BLOCK)) - far
    return jnp.stack([diag, prev], axis=1)


def _out_proj_kernel(h_ref, ohg_ref, oda_ref, wo_ref, g_ref, *rest, with_router):
    if with_router:
        router_ref, hn_ref, u_ref, lg_ref = rest
    else:
        hn_ref, u_ref = rest
    hn = (h_ref[...] + _dot(ohg_ref[...], wo_ref[:HG_WIDTH, :])
          + _dot(oda_ref[...], wo_ref[HG_WIDTH:, :]))
    hn_ref[...] = hn
    u = _rms(hn, g_ref[...])
    if with_router:
        _store_token_major(u_ref, u)
        lg_ref[...] = jnp.dot(u, router_ref[...], precision=lax.Precision.HIGHEST,
                              preferred_element_type=F32)
    else:
        u_ref[...] = u.astype(u_ref.dtype)


def _out_proj(h, o_hg, o_da, wo, layer, gain, router=None):
    n = h.shape[0]
    with_router = router is not None
    row = lambda i: (i, 0)
    full = lambda i: (0, 0)
    in_specs = [
        pl.BlockSpec((ROW_TILE, D_MODEL), row),
        pl.BlockSpec((ROW_TILE, HG_WIDTH), row),
        pl.BlockSpec((ROW_TILE, DA_HEADS * DA_DV), row),
        pl.BlockSpec((None,) + wo.shape[1:], lambda i: (layer, 0, 0)),
        pl.BlockSpec((1, D_MODEL), full),
    ]
    out_specs = [pl.BlockSpec((ROW_TILE, D_MODEL), row), pl.BlockSpec((ROW_TILE, D_MODEL), row)]
    out_shape = [jax.ShapeDtypeStruct((n, D_MODEL), F32), jax.ShapeDtypeStruct((n, D_MODEL), BF16)]
    args = [h, o_hg, o_da, wo, gain]
    if with_router:
        out_specs[1] = pl.BlockSpec((ROW_TILE * TOKEN_ROWS, LANES), row)
        out_shape[1] = jax.ShapeDtypeStruct((n * TOKEN_ROWS, LANES), F32)
        in_specs.append(pl.BlockSpec(router.shape, full))
        out_specs.append(pl.BlockSpec((ROW_TILE, LANES), row))
        out_shape.append(jax.ShapeDtypeStruct((n, LANES), F32))
        args.append(router)
    return pl.pallas_call(
        functools.partial(_out_proj_kernel, with_router=with_router),
        grid=(n // ROW_TILE,),
        in_specs=in_specs,
        out_specs=out_specs,
        out_shape=out_shape,
        compiler_params=_params("parallel"),
        name="out_proj_router" if with_router else "out_proj",
    )(*args)


def _dense_ffn_kernel(h_ref, u_ref, w1_ref, w3_ref, w2_ref, o_ref):
    u = u_ref[...]
    a = _dot(u, w1_ref[...])
    act = (_silu(a) * _dot(u, w3_ref[...])).astype(BF16)
    o_ref[...] = h_ref[...] + _dot(act, w2_ref[...])


def _dense_ffn(h, u, w1, w3, w2, idx):
    n = h.shape[0]
    row = lambda i: (i, 0)
    stacked = lambda i: (idx, 0, 0)
    return pl.pallas_call(
        _dense_ffn_kernel,
        grid=(n // ROW_TILE,),
        in_specs=[
            pl.BlockSpec((ROW_TILE, D_MODEL), row),
            pl.BlockSpec((ROW_TILE, D_MODEL), row),
            pl.BlockSpec((None,) + w1.shape[1:], stacked),
            pl.BlockSpec((None,) + w3.shape[1:], stacked),
            pl.BlockSpec((None,) + w2.shape[1:], stacked),
        ],
        out_specs=pl.BlockSpec((ROW_TILE, D_MODEL), row),
        out_shape=jax.ShapeDtypeStruct((n, D_MODEL), F32),
        compiler_params=_params("parallel"),
        name="dense_ffn",
    )(h, u, w1, w3, w2)


DISPATCH_TILE = 1024
ISSUE_TOKENS = 8


def _token_copy(src_ref, src_tok, dst_ref, dst_tok, sem):
    src = pl.multiple_of(src_tok * TOKEN_ROWS, TOKEN_ROWS)
    dst = pl.multiple_of(dst_tok * TOKEN_ROWS, TOKEN_ROWS)
    return pltpu.make_async_copy(src_ref.at[pl.ds(src, TOKEN_ROWS), :],
                                 dst_ref.at[pl.ds(dst, TOKEN_ROWS), :], sem)


def _dispatch_kernel(dest_ref, u_ref, xs_in_ref, xs_ref, sem):
    del xs_in_ref
    base = pl.program_id(0) * DISPATCH_TILE

    def start(g, carry):
        r0 = g * ISSUE_TOKENS
        slots = [dest_ref[(base + r0) * TOP_K + j] for j in range(ISSUE_TOKENS * TOP_K)]
        for j, slot in enumerate(slots):
            _token_copy(u_ref, r0 + j // TOP_K, xs_ref, slot, sem).start()
        return carry

    lax.fori_loop(0, DISPATCH_TILE // ISSUE_TOKENS, start, 0)
    rows = DISPATCH_TILE * TOKEN_ROWS
    for _ in range(TOP_K):
        pltpu.make_async_copy(u_ref, xs_ref.at[pl.ds(0, rows), :], sem).wait()


def _dispatch(dest, u_tm, n_slots):
    n = dest.shape[0] // TOP_K
    zeros = jnp.zeros((n_slots * TOKEN_ROWS, LANES), u_tm.dtype)
    return pl.pallas_call(
        _dispatch_kernel,
        grid_spec=pltpu.PrefetchScalarGridSpec(
            num_scalar_prefetch=1,
            grid=(n // DISPATCH_TILE,),
            in_specs=[pl.BlockSpec((DISPATCH_TILE * TOKEN_ROWS, LANES), lambda i, d: (i, 0)),
                      pl.BlockSpec(memory_space=pl.ANY)],
            out_specs=pl.BlockSpec(memory_space=pl.ANY),
            scratch_shapes=[pltpu.SemaphoreType.DMA(())],
        ),
        out_shape=jax.ShapeDtypeStruct(zeros.shape, u_tm.dtype),
        input_output_aliases={2: 0},
        compiler_params=_params("arbitrary"),
        name="moe_dispatch",
    )(dest, u_tm, zeros)


def _expert_kernel(be_ref, nused_ref, xs_ref, w1_ref, w3_ref, w2_ref, y_ref):
    del be_ref
    blk = pl.program_id(0)

    @pl.when(blk < nused_ref[0])
    def _():
        x = _load_token_major(xs_ref, MOE_TILE).astype(BF16)
        acc = jnp.zeros((MOE_TILE, D_MODEL), F32)
        for c0 in range(0, D_FF_EXPERT, FF_CHUNK):
            a = _dot(x, w1_ref[:, c0:c0 + FF_CHUNK])
            act = (_silu(a) * _dot(x, w3_ref[:, c0:c0 + FF_CHUNK])).astype(BF16)
            acc = acc + _dot(act, w2_ref[c0:c0 + FF_CHUNK, :])
        _store_token_major(y_ref, acc)

    @pl.when(blk >= nused_ref[0])
    def _():
        y_ref[...] = jnp.zeros_like(y_ref)


def _experts(block_expert, n_used, xs, w1, w3, w2, idx):
    n_blocks = xs.shape[0] // (MOE_TILE * TOKEN_ROWS)
    wmap = lambda i, be, nu: (idx, be[i], 0, 0)
    slots = pl.BlockSpec((MOE_TILE * TOKEN_ROWS, LANES), lambda i, be, nu: (i, 0))
    return pl.pallas_call(
        _expert_kernel,
        grid_spec=pltpu.PrefetchScalarGridSpec(
            num_scalar_prefetch=2,
            grid=(n_blocks,),
            in_specs=[
                slots,
                pl.BlockSpec((None, None, D_MODEL, D_FF_EXPERT), wmap),
                pl.BlockSpec((None, None, D_MODEL, D_FF_EXPERT), wmap),
                pl.BlockSpec((None, None, D_FF_EXPERT, D_MODEL), wmap),
            ],
            out_specs=slots,
        ),
        out_shape=jax.ShapeDtypeStruct(xs.shape, F32),
        compiler_params=_params("arbitrary"),
        name="moe_experts",
    )(block_expert, n_used, xs, w1, w3, w2)


def _combine_kernel(dest_ref, h_ref, gate_ref, y_ref, o_ref, buf_ref, sem):
    step = pl.program_id(0)

    def issue(s, slot):
        base = s * ROW_TILE

        def body(g, carry):
            r0 = g * ISSUE_TOKENS
            slots = [dest_ref[(base + r0) * TOP_K + j] for j in range(ISSUE_TOKENS * TOP_K)]
            for j, src in enumerate(slots):
                _token_copy(y_ref, src, buf_ref.at[slot, j % TOP_K], r0 + j // TOP_K,
                            sem.at[slot]).start()
            return carry

        lax.fori_loop(0, ROW_TILE // ISSUE_TOKENS, body, 0)

    @pl.when(step == 0)
    def _():
        issue(0, 0)

    for slot in range(2):
        @pl.when(step % 2 == slot)
        def _():
            @pl.when(step + 1 < pl.num_programs(0))
            def _():
                issue(step + 1, 1 - slot)

            for k in range(TOP_K):
                pltpu.make_async_copy(y_ref.at[pl.ds(0, ROW_TILE * TOKEN_ROWS), :],
                                      buf_ref.at[slot, k], sem.at[slot]).wait()
            gate = gate_ref[...]
            o_ref[...] = (h_ref[...]
                          + gate[:, 0:1] * _load_token_major(buf_ref.at[slot, 0], ROW_TILE)
                          + gate[:, 1:2] * _load_token_major(buf_ref.at[slot, 1], ROW_TILE))


def _combine(dest, h, gate, y):
    n = h.shape[0]
    return pl.pallas_call(
        _combine_kernel,
        grid_spec=pltpu.PrefetchScalarGridSpec(
            num_scalar_prefetch=1,
            grid=(n // ROW_TILE,),
            in_specs=[
                pl.BlockSpec((ROW_TILE, D_MODEL), lambda i, d: (i, 0)),
                pl.BlockSpec((ROW_TILE, TOP_K), lambda i, d: (i, 0)),
                pl.BlockSpec(memory_space=pl.ANY),
            ],
            out_specs=pl.BlockSpec((ROW_TILE, D_MODEL), lambda i, d: (i, 0)),
            scratch_shapes=[pltpu.VMEM((2, TOP_K, ROW_TILE * TOKEN_ROWS, LANES), F32),
                            pltpu.SemaphoreType.DMA((2,))],
        ),
        out_shape=jax.ShapeDtypeStruct((n, D_MODEL), F32),
        compiler_params=_params("arbitrary"),
        name="moe_combine",
    )(dest, h, gate, y)


def _route(logits, n_slots):
    n = logits.shape[0]
    eid = lax.broadcasted_iota(jnp.int32, logits.shape, 1)
    i1 = jnp.argmax(logits, axis=-1).astype(jnp.int32)
    l1 = jnp.max(logits, axis=-1)
    rest = jnp.where(eid == i1[:, None], -jnp.inf, logits)
    i2 = jnp.argmax(rest, axis=-1).astype(jnp.int32)
    l2 = jnp.max(rest, axis=-1)
    e2 = jnp.exp(l2 - l1)
    gate = jnp.stack([1.0 / (1.0 + e2), e2 / (1.0 + e2)], axis=-1)

    flat_e = jnp.stack([i1, i2], axis=-1).reshape(n * TOP_K)
    onehot = (flat_e[:, None] == jnp.arange(N_EXPERTS, dtype=jnp.int32)[None, :]).astype(jnp.int32)
    csum = jnp.cumsum(onehot, axis=0)
    rank = jnp.sum(csum * onehot, axis=-1) - 1
    counts = csum[-1]
    padded = (counts + MOE_TILE - 1) // MOE_TILE * MOE_TILE
    pad_end = jnp.cumsum(padded)
    pad_start = pad_end - padded
    dest = (pad_start[flat_e] + rank).astype(jnp.int32)
    n_blocks = n_slots // MOE_TILE
    block_expert = jnp.minimum(
        jnp.searchsorted(pad_end, jnp.arange(n_blocks, dtype=jnp.int32) * MOE_TILE, side="right"),
        N_EXPERTS - 1).astype(jnp.int32)
    n_used = (pad_end[-1:] // MOE_TILE).astype(jnp.int32)
    return dest, gate, block_expert, n_used


def _moe_ffn(h, u, logits, w1, w3, w2, idx):
    n = h.shape[0]
    n_slots = (n * TOP_K // MOE_TILE + N_EXPERTS) * MOE_TILE
    dest, gate, block_expert, n_used = _route(logits, n_slots)
    xs = _dispatch(dest, u, n_slots)
    y = _experts(block_expert, n_used, xs, w1, w3, w2, idx)
    return _combine(dest, h, gate, y)


def _final_kernel(h_ref, g_ref, o_ref):
    o_ref[...] = _rms(h_ref[...], g_ref[...])


def _final_norm(h3, gain, seq):
    batch = h3.shape[0]
    lead_blocks = LEAD // Q_BLOCK
    return pl.pallas_call(
        _final_kernel,
        grid=(seq // Q_BLOCK,),
        in_specs=[
            pl.BlockSpec((batch, Q_BLOCK, D_MODEL), lambda j: (0, j + lead_blocks, 0)),
            pl.BlockSpec((1, D_MODEL), lambda j: (0, 0)),
        ],
        out_specs=pl.BlockSpec((batch, Q_BLOCK, D_MODEL), lambda j: (0, j, 0)),
        out_shape=jax.ShapeDtypeStruct((batch, seq, D_MODEL), F32),
        compiler_params=_params("parallel"),
        name="final_norm",
    )(h3, gain)


def kernel(x, meta, rel_bias, norm_mix, w_in, hg_lb_logits, hg_norm_w, da_lambda, da_subln_w, w_out, norm_ffn, dense_w1, dense_w3, dense_w2, moe_router, moe_w1, moe_w3, moe_w2, final_norm):
    batch, seq, d = x.shape
    length = LEAD + seq
    h = jnp.concatenate([
        jnp.zeros((batch, LEAD - N_META, d), x.dtype),
        jnp.broadcast_to(meta[None].astype(x.dtype), (batch, N_META, d)),
        x], axis=1).reshape(batch * length, d)

    toe = _attn_bias_tables(rel_bias)
    lb_cum = jnp.cumsum(jax.nn.softmax(hg_lb_logits.astype(F32), axis=0), axis=0)
    lb_all = jnp.clip(lb_cum - lb_cum[0:1], 0.0, LB_MAX)
    log_lb = jnp.log(lb_all)
    log_1m_lb = jnp.log1p(-lb_all)

    w_in_b, w_out_b = w_in.astype(BF16), w_out.astype(BF16)
    dense_b = [w.astype(BF16) for w in (dense_w1, dense_w3, dense_w2)]
    moe_b = [w.astype(BF16) for w in (moe_w1, moe_w3, moe_w2)]

    for l in range(DEPTH):
        hg, da = _mix_in(h, norm_mix[l][None], w_in_b, l)
        o_hg = _hgrn(hg, log_lb[l][None], log_1m_lb[l][None], hg_norm_w[l][None], batch, length)
        lam_init = 0.8 - 0.6 * math.exp(-0.3 * l)
        lv = da_lambda[l].astype(F32)
        lam = jnp.exp(jnp.sum(lv[0] * lv[1])) - jnp.exp(jnp.sum(lv[2] * lv[3])) + lam_init
        cst = jnp.zeros((8, LANES), F32).at[0].set(lam).at[1].set(1.0 - lam_init)
        o_da = _attn(da, toe, cst, da_subln_w[l][None], batch, length)
        i = l // 2
        if l % 2 == 0:
            hn, u = _out_proj(h, o_hg, o_da, w_out_b, l, norm_ffn[l][None])
            h = _dense_ffn(hn, u, *dense_b, i)
        else:
            router = jnp.zeros((d, LANES), F32).at[:, :N_EXPERTS].set(moe_router[i].astype(F32))
            hn, u, lg = _out_proj(h, o_hg, o_da, w_out_b, l, norm_ffn[l][None], router)
            h = _moe_ffn(hn, u, lg[:, :N_EXPERTS], *moe_b, i)
    return _final_norm(h.reshape(batch, length, d), final_norm[None], seq)
```

```python
import functools
import math

import jax
import jax.numpy as jnp
import numpy as np
from jax import lax
from jax.experimental import pallas as pl
from jax.experimental.pallas import tpu as pltpu

D_MODEL = 1024
DEPTH = 4
N_META = 16
LEAD = 128
HG_WIDTH = 512
HG_HEADS = 4
HG_D = 128
HG_CHUNK = 64
DA_HEADS = 4
DA_DQK = 64
DA_DV = 128
Q_BLOCK = 128
KEY_TILE = 2 * Q_BLOCK
REL_BUCKETS = 32
REL_MAX_DIST = 128
N_EXPERTS = 8
TOP_K = 2
D_FF_EXPERT = 3584
EPS = 1e-6
NEG = -1e30
LB_MAX = 0.999
HG_COLS = 4 * HG_WIDTH
DA_COLS = 3 * DA_HEADS * DA_DV
W_IN_COLS = HG_COLS + DA_COLS

LANES = 128
VMEM_LIMIT = 56 * 1024 * 1024

ROW_TILE = 256
HG_TILE = 128
MOE_TILE = 256
FF_CHUNK = 512

F32 = jnp.float32
BF16 = jnp.bfloat16


def _params(*sem):
    return pltpu.CompilerParams(dimension_semantics=sem, vmem_limit_bytes=VMEM_LIMIT)


def _dot(a, b):
    return jnp.dot(a, b, preferred_element_type=F32)


def _dot_nt(a, b):
    return lax.dot_general(a, b, (((1,), (1,)), ((), ())), preferred_element_type=F32)


def _dot_tn(a, b):
    return lax.dot_general(a, b, (((0,), (0,)), ((), ())), preferred_element_type=F32)


def _rms(x, gain):
    return x * lax.rsqrt(jnp.mean(x * x, axis=-1, keepdims=True) + EPS) * gain


TOKEN_ROWS = D_MODEL // LANES


def _store_token_major(ref, x):
    t = x.shape[0]
    for s in range(TOKEN_ROWS):
        ref[pl.ds(s, t, stride=TOKEN_ROWS), :] = x[:, s * LANES:(s + 1) * LANES]


def _load_token_major(ref, t):
    return jnp.concatenate(
        [ref[pl.ds(s, t, stride=TOKEN_ROWS), :] for s in range(TOKEN_ROWS)], axis=1)


def _silu(x):
    return x * (0.5 * jnp.tanh(0.5 * x) + 0.5)


def _mix_in_kernel(x_ref, g_ref, w_ref, hg_ref, da_ref):
    u = _rms(x_ref[...], g_ref[...]).astype(BF16)
    hg_ref[...] = _dot(u, w_ref[:, :HG_COLS])
    da_ref[...] = _dot(u, w_ref[:, HG_COLS:]).astype(BF16)


def _mix_in(h, gain, w, layer):
    n = h.shape[0]
    return pl.pallas_call(
        _mix_in_kernel,
        grid=(n // ROW_TILE,),
        in_specs=[
            pl.BlockSpec((ROW_TILE, D_MODEL), lambda i: (i, 0)),
            pl.BlockSpec((1, D_MODEL), lambda i: (0, 0)),
            pl.BlockSpec((None, D_MODEL, W_IN_COLS), lambda i: (layer, 0, 0)),
        ],
        out_specs=[
            pl.BlockSpec((ROW_TILE, HG_COLS), lambda i: (i, 0)),
            pl.BlockSpec((ROW_TILE, DA_COLS), lambda i: (i, 0)),
        ],
        out_shape=[
            jax.ShapeDtypeStruct((n, HG_COLS), F32),
            jax.ShapeDtypeStruct((n, DA_COLS), BF16),
        ],
        compiler_params=_params("parallel"),
        name="mix_in",
    )(h, gain, w)


HG_LEVELS = (32, 16, 8, 4, 2, 1)
N_SUMS = len(HG_LEVELS) + 2


def _hgrn_consts():
    c = HG_CHUNK
    t = np.arange(c)[:, None]
    j = np.arange(c)[None, :]
    sums = np.zeros((N_SUMS, c, c), np.float32)
    masks = np.zeros((len(HG_LEVELS) + 1, c, c), np.float32)
    sums[0] = j <= t
    masks[0] = np.eye(c)
    for li, w in enumerate(HG_LEVELS, start=1):
        ref = (t // (2 * w)) * (2 * w) + w
        sums[li] = np.where(t >= ref, (j > ref) & (j <= t), (j > t) & (j <= ref))
        masks[li] = (t // (2 * w) == j // (2 * w)) & (t % (2 * w) >= w) & (j % (2 * w) < w)
    sums[N_SUMS - 1] = j > t
    return sums.reshape(N_SUMS * c, c), masks


_HG_SUMS, _HG_MASKS = _hgrn_consts()


def _hgrn_kernel(hg_ref, loga_ref, log1m_ref, nw_ref, sums_ref, masks_ref, o_ref, state_ref):
    c_idx = pl.program_id(1)

    @pl.when(c_idx == 0)
    def _():
        state_ref[...] = jnp.zeros_like(state_ref)

    C = HG_CHUNK
    W = HG_WIDTH
    n_chunks = HG_TILE // C
    sums = sums_ref[...]
    nw = nw_ref[...]
    row_idx = c_idx * HG_TILE + lax.broadcasted_iota(jnp.int32, (HG_TILE, 1), 0)
    valid = row_idx >= (LEAD - N_META)
    step = lax.broadcasted_iota(jnp.int32, (C, 1), 0)
    head_cols = [slice(hd * HG_D, (hd + 1) * HG_D) for hd in range(HG_HEADS)]

    f = hg_ref[:, W:2 * W]
    qf = _silu(hg_ref[:, 0:W])
    ls = jnp.minimum(f, 0.0) - jnp.log(1.0 + jnp.exp(-jnp.abs(f)))
    cc = log1m_ref[...] + ls
    loga = loga_ref[...]
    lf = jnp.maximum(loga, cc) + jnp.log(1.0 + jnp.exp(-jnp.abs(loga - cc)))
    kk = jnp.exp(cc - f)
    lf = jnp.where(valid, lf, 0.0)
    kk = jnp.where(valid, kk, 0.0)
    lf_hi = lf.astype(BF16)
    lf_lo = (lf - lf_hi.astype(F32)).astype(BF16)
    vb = hg_ref[:, 2 * W:3 * W].astype(BF16)
    gate = _silu(hg_ref[:, 3 * W:4 * W])
    qb = qf.astype(BF16)
    kb = kk.astype(BF16)

    chunks = []
    for ch in range(n_chunks):
        rows = slice(ch * C, (ch + 1) * C)
        e = jnp.exp(_dot(sums, lf_hi[rows]) + _dot(sums, lf_lo[rows]))
        e_b = e[0:C]
        q_in = (qf[rows] * e_b).astype(BF16)
        k_out = (kk[rows] * e[(N_SUMS - 1) * C:N_SUMS * C]).astype(BF16)
        z = [(jnp.where((step & w) != 0, qf[rows], kk[rows]) * e[li * C:(li + 1) * C]).astype(BF16)
             for li, w in enumerate(HG_LEVELS, start=1)]
        scores = []
        for cols in head_cols:
            s = masks_ref[0] * _dot_nt(qb[rows, cols], kb[rows, cols])
            for li in range(1, len(HG_LEVELS) + 1):
                zl = z[li - 1][:, cols]
                s += masks_ref[li] * _dot_nt(zl, zl)
            scores.append(s.astype(BF16))
        chunks.append((rows, e_b[C - 1:C, :], q_in, k_out, scores))

    for rows, decay_end, q_in, k_out, scores in chunks:
        for hd, cols in enumerate(head_cols):
            st = state_ref[hd]
            v_h = vb[rows, cols]
            o = _dot_nt(q_in[:, cols], st.astype(BF16)) + _dot(scores[hd], v_h)
            state_ref[hd] = st * decay_end[:, cols] + _dot_tn(v_h, k_out[:, cols])
            o = _rms(o, nw) * gate[rows, cols]
            o_ref[rows, cols] = o.astype(o_ref.dtype)


def _hgrn(hg, loga, log1m, norm_w, batch, length):
    hg3 = hg.reshape(batch, length, HG_COLS)
    out = pl.pallas_call(
        _hgrn_kernel,
        grid=(batch, length // HG_TILE),
        in_specs=[
            pl.BlockSpec((None, HG_TILE, HG_COLS), lambda b, c: (b, c, 0)),
            pl.BlockSpec((1, HG_WIDTH), lambda b, c: (0, 0)),
            pl.BlockSpec((1, HG_WIDTH), lambda b, c: (0, 0)),
            pl.BlockSpec((1, HG_D), lambda b, c: (0, 0)),
            pl.BlockSpec(_HG_SUMS.shape, lambda b, c: (0, 0)),
            pl.BlockSpec(_HG_MASKS.shape, lambda b, c: (0, 0, 0)),
        ],
        out_specs=pl.BlockSpec((None, HG_TILE, HG_WIDTH), lambda b, c: (b, c, 0)),
        out_shape=jax.ShapeDtypeStruct((batch, length, HG_WIDTH), BF16),
        scratch_shapes=[pltpu.VMEM((HG_HEADS, HG_D, HG_D), F32)],
        compiler_params=_params("parallel", "arbitrary"),
        name="hgrn2",
    )(hg3, loga, log1m, norm_w, jnp.asarray(_HG_SUMS, BF16), jnp.asarray(_HG_MASKS, F32))
    return out.reshape(batch * length, HG_WIDTH)


def _attn_kernel(q_ref, k_ref, v_ref, toe_ref, cst_ref, w_ref, o_ref, s_ref, *, n_blocks):
    lam = cst_ref[0:1, 0:1]
    post = cst_ref[1:2, :]
    lane = lax.broadcasted_iota(jnp.int32, (Q_BLOCK, Q_BLOCK), 1)
    first_half = lane < DA_DQK
    key_ok0 = lane >= (LEAD - N_META)
    inert_bias = jnp.where(key_ok0, 0.0, NEG)
    scale = DA_DQK ** -0.5

    def near_bias(kind, kb):
        bias = toe_ref[kind]
        if kb == 0:
            bias = jnp.where(key_ok0, bias, NEG)
        return bias

    def slabs(x):
        return [x[:, c:c + Q_BLOCK] for c in range(0, x.shape[1], Q_BLOCK)]

    def both_maps(bias):
        return jnp.concatenate([bias, bias], axis=0)

    def stacked_q(i):
        qi = q_ref[i * Q_BLOCK:(i + 1) * Q_BLOCK, :] * scale
        zero = jnp.zeros_like(qi)
        return jnp.concatenate([jnp.where(first_half, qi, zero), jnp.where(first_half, zero, qi)],
                               axis=0)

    def key_tiles(i):
        tiles = []
        far_end = max(i - 1, 0)
        if far_end >= 1:
            tiles.append((0, Q_BLOCK, both_maps(inert_bias)))
        kb = 1
        while kb < far_end:
            width = KEY_TILE if kb + KEY_TILE // Q_BLOCK <= far_end else Q_BLOCK
            tiles.append((kb * Q_BLOCK, width, None))
            kb += width // Q_BLOCK
        if i >= 1:
            tiles.append(((i - 1) * Q_BLOCK, Q_BLOCK, both_maps(near_bias(1, i - 1))))
        tiles.append((i * Q_BLOCK, Q_BLOCK, both_maps(near_bias(0, i))))
        return tiles

    def scores(q2, tile):
        start, width, bias = tile
        s = _dot_nt(q2, k_ref[start:start + width, :])
        return s if bias is None else s + bias

    def sweep_scores(i):
        q2 = stacked_q(i)
        m_acc = None
        for tile in key_tiles(i):
            s = scores(q2, tile)
            s_ref[i % 2, :, tile[0]:tile[0] + tile[1]] = s
            for slab in slabs(s):
                m_acc = slab if m_acc is None else jnp.maximum(m_acc, slab)
        return m_acc.max(axis=-1, keepdims=True)

    row_max = sweep_scores(0)
    for i in range(n_blocks):
        m = row_max
        if i + 1 < n_blocks:
            row_max = sweep_scores(i + 1)
        l_acc = o_acc = None
        for start, width, _ in key_tiles(i):
            p = jnp.exp(s_ref[i % 2, :, start:start + width] - m)
            for slab in slabs(p):
                l_acc = slab if l_acc is None else l_acc + slab
            part = _dot(p.astype(BF16), v_ref[start:start + width, :])
            o_acc = part if o_acc is None else o_acc + part
        o2 = o_acc * (1.0 / l_acc.sum(axis=-1, keepdims=True))
        o = o2[:Q_BLOCK] - lam * o2[Q_BLOCK:]
        o = _rms(o, w_ref[...]) * post
        o_ref[i * Q_BLOCK:(i + 1) * Q_BLOCK, :] = o.astype(o_ref.dtype)


def _attn(da, toe, cst, subln_w, batch, length):
    da3 = da.reshape(batch, length, DA_COLS)
    hw = DA_HEADS
    out = pl.pallas_call(
        functools.partial(_attn_kernel, n_blocks=length // Q_BLOCK),
        grid=(batch, DA_HEADS),
        in_specs=[
            pl.BlockSpec((None, length, DA_DV), lambda b, h: (b, 0, h)),
            pl.BlockSpec((None, length, DA_DV), lambda b, h: (b, 0, hw + h)),
            pl.BlockSpec((None, length, DA_DV), lambda b, h: (b, 0, 2 * hw + h)),
            pl.BlockSpec((None, 2, Q_BLOCK, Q_BLOCK), lambda b, h: (h, 0, 0, 0)),
            pl.BlockSpec((8, LANES), lambda b, h: (0, 0)),
            pl.BlockSpec((1, DA_DV), lambda b, h: (0, 0)),
        ],
        out_specs=pl.BlockSpec((None, length, DA_DV), lambda b, h: (b, 0, h)),
        out_shape=jax.ShapeDtypeStruct((batch, length, DA_HEADS * DA_DV), BF16),
        scratch_shapes=[pltpu.VMEM((2, 2 * Q_BLOCK, length), F32)],
        compiler_params=_params("parallel", "parallel"),
        name="diff_attn",
    )(da3, da3, da3, toe, cst, subln_w)
    return out.reshape(batch * length, DA_HEADS * DA_DV)


def _t5_bucket(dist):
    n = jnp.maximum(dist, 0)
    max_exact = REL_BUCKETS // 2
    nf = jnp.maximum(n, max_exact).astype(F32)
    large = max_exact + (jnp.log(nf / max_exact) / math.log(REL_MAX_DIST / max_exact)
                         * (REL_BUCKETS - max_exact)).astype(jnp.int32)
    large = jnp.minimum(large, REL_BUCKETS - 1)
    return jnp.where(n < max_exact, n, large)


def _attn_bias_tables(rel_bias):
    tab = rel_bias.astype(F32)
    qi = jnp.arange(Q_BLOCK, dtype=jnp.int32)[:, None]
    ki = jnp.arange(Q_BLOCK, dtype=jnp.int32)[None, :]

    def lookup(bucket):
        onehot = bucket[None, :, :, None] == jnp.arange(REL_BUCKETS, dtype=jnp.int32)
        return jnp.sum(jnp.where(onehot, tab.T[:, None, None, :], 0.0), axis=-1)

    far = tab[REL_BUCKETS - 1][:, None, None]
    diag = jnp.where((ki <= qi)[None], lookup(_t5_bucket(qi - ki)) - far, NEG)
    prev = lookup(_t5_bucket(qi - ki + Q_BLOCK)) - far
    return jnp.stack([diag, prev], axis=1)


def _out_proj_kernel(h_ref, ohg_ref, oda_ref, wo_ref, g_ref, *rest, with_router):
    if with_router:
        router_ref, hn_ref, u_ref, lg_ref = rest
    else:
        hn_ref, u_ref = rest
    hn = (h_ref[...] + _dot(ohg_ref[...], wo_ref[:HG_WIDTH, :])
          + _dot(oda_ref[...], wo_ref[HG_WIDTH:, :]))
    hn_ref[...] = hn
    u = _rms(hn, g_ref[...])
    if with_router:
        _store_token_major(u_ref, u)
        lg_ref[...] = jnp.dot(u, router_ref[...], precision=lax.Precision.HIGHEST,
                              preferred_element_type=F32)
    else:
        u_ref[...] = u.astype(u_ref.dtype)


def _out_proj(h, o_hg, o_da, wo, layer, gain, router=None):
    n = h.shape[0]
    with_router = router is not None
    row = lambda i: (i, 0)
    full = lambda i: (0, 0)
    in_specs = [
        pl.BlockSpec((ROW_TILE, D_MODEL), row),
        pl.BlockSpec((ROW_TILE, HG_WIDTH), row),
        pl.BlockSpec((ROW_TILE, DA_HEADS * DA_DV), row),
        pl.BlockSpec((None,) + wo.shape[1:], lambda i: (layer, 0, 0)),
        pl.BlockSpec((1, D_MODEL), full),
    ]
    out_specs = [pl.BlockSpec((ROW_TILE, D_MODEL), row), pl.BlockSpec((ROW_TILE, D_MODEL), row)]
    out_shape = [jax.ShapeDtypeStruct((n, D_MODEL), F32), jax.ShapeDtypeStruct((n, D_MODEL), BF16)]
    args = [h, o_hg, o_da, wo, gain]
    if with_router:
        out_specs[1] = pl.BlockSpec((ROW_TILE * TOKEN_ROWS, LANES), row)
        out_shape[1] = jax.ShapeDtypeStruct((n * TOKEN_ROWS, LANES), F32)
        in_specs.append(pl.BlockSpec(router.shape, full))
        out_specs.append(pl.BlockSpec((ROW_TILE, LANES), row))
        out_shape.append(jax.ShapeDtypeStruct((n, LANES), F32))
        args.append(router)
    return pl.pallas_call(
        functools.partial(_out_proj_kernel, with_router=with_router),
        grid=(n // ROW_TILE,),
        in_specs=in_specs,
        out_specs=out_specs,
        out_shape=out_shape,
        compiler_params=_params("parallel"),
        name="out_proj_router" if with_router else "out_proj",
    )(*args)


def _dense_ffn_kernel(h_ref, u_ref, w1_ref, w3_ref, w2_ref, o_ref):
    u = u_ref[...]
    a = _dot(u, w1_ref[...])
    act = (_silu(a) * _dot(u, w3_ref[...])).astype(BF16)
    o_ref[...] = h_ref[...] + _dot(act, w2_ref[...])


def _dense_ffn(h, u, w1, w3, w2, idx):
    n = h.shape[0]
    row = lambda i: (i, 0)
    stacked = lambda i: (idx, 0, 0)
    return pl.pallas_call(
        _dense_ffn_kernel,
        grid=(n // ROW_TILE,),
        in_specs=[
            pl.BlockSpec((ROW_TILE, D_MODEL), row),
            pl.BlockSpec((ROW_TILE, D_MODEL), row),
            pl.BlockSpec((None,) + w1.shape[1:], stacked),
            pl.BlockSpec((None,) + w3.shape[1:], stacked),
            pl.BlockSpec((None,) + w2.shape[1:], stacked),
        ],
        out_specs=pl.BlockSpec((ROW_TILE, D_MODEL), row),
        out_shape=jax.ShapeDtypeStruct((n, D_MODEL), F32),
        compiler_params=_params("parallel"),
        name="dense_ffn",
    )(h, u, w1, w3, w2)


DISPATCH_TILE = 1024
ISSUE_TOKENS = 8


def _token_copy(src_ref, src_tok, dst_ref, dst_tok, sem):
    src = pl.multiple_of(src_tok * TOKEN_ROWS, TOKEN_ROWS)
    dst = pl.multiple_of(dst_tok * TOKEN_ROWS, TOKEN_ROWS)
    return pltpu.make_async_copy(src_ref.at[pl.ds(src, TOKEN_ROWS), :],
                                 dst_ref.at[pl.ds(dst, TOKEN_ROWS), :], sem)


def _dispatch_kernel(dest_ref, u_ref, xs_in_ref, xs_ref, sem):
    del xs_in_ref
    base = pl.program_id(0) * DISPATCH_TILE

    def start(g, carry):
        r0 = g * ISSUE_TOKENS
        slots = [dest_ref[(base + r0) * TOP_K + j] for j in range(ISSUE_TOKENS * TOP_K)]
        for j, slot in enumerate(slots):
            _token_copy(u_ref, r0 + j // TOP_K, xs_ref, slot, sem).start()
        return carry

    lax.fori_loop(0, DISPATCH_TILE // ISSUE_TOKENS, start, 0)
    rows = DISPATCH_TILE * TOKEN_ROWS
    for _ in range(TOP_K):
        pltpu.make_async_copy(u_ref, xs_ref.at[pl.ds(0, rows), :], sem).wait()


def _dispatch(dest, u_tm, n_slots):
    n = dest.shape[0] // TOP_K
    zeros = jnp.zeros((n_slots * TOKEN_ROWS, LANES), u_tm.dtype)
    return pl.pallas_call(
        _dispatch_kernel,
        grid_spec=pltpu.PrefetchScalarGridSpec(
            num_scalar_prefetch=1,
            grid=(n // DISPATCH_TILE,),
            in_specs=[pl.BlockSpec((DISPATCH_TILE * TOKEN_ROWS, LANES), lambda i, d: (i, 0)),
                      pl.BlockSpec(memory_space=pl.ANY)],
            out_specs=pl.BlockSpec(memory_space=pl.ANY),
            scratch_shapes=[pltpu.SemaphoreType.DMA(())],
        ),
        out_shape=jax.ShapeDtypeStruct(zeros.shape, u_tm.dtype),
        input_output_aliases={2: 0},
        compiler_params=_params("arbitrary"),
        name="moe_dispatch",
    )(dest, u_tm, zeros)


def _expert_kernel(be_ref, nused_ref, xs_ref, w1_ref, w3_ref, w2_ref, y_ref):
    del be_ref
    blk = pl.program_id(0)

    @pl.when(blk < nused_ref[0])
    def _():
        x = _load_token_major(xs_ref, MOE_TILE).astype(BF16)
        acc = jnp.zeros((MOE_TILE, D_MODEL), F32)
        for c0 in range(0, D_FF_EXPERT, FF_CHUNK):
            a = _dot(x, w1_ref[:, c0:c0 + FF_CHUNK])
            act = (_silu(a) * _dot(x, w3_ref[:, c0:c0 + FF_CHUNK])).astype(BF16)
            acc = acc + _dot(act, w2_ref[c0:c0 + FF_CHUNK, :])
        _store_token_major(y_ref, acc)

    @pl.when(blk >= nused_ref[0])
    def _():
        y_ref[...] = jnp.zeros_like(y_ref)


def _experts(block_expert, n_used, xs, w1, w3, w2, idx):
    n_blocks = xs.shape[0] // (MOE_TILE * TOKEN_ROWS)
    wmap = lambda i, be, nu: (idx, be[i], 0, 0)
    slots = pl.BlockSpec((MOE_TILE * TOKEN_ROWS, LANES), lambda i, be, nu: (i, 0))
    return pl.pallas_call(
        _expert_kernel,
        grid_spec=pltpu.PrefetchScalarGridSpec(
            num_scalar_prefetch=2,
            grid=(n_blocks,),
            in_specs=[
                slots,
                pl.BlockSpec((None, None, D_MODEL, D_FF_EXPERT), wmap),
                pl.BlockSpec((None, None, D_MODEL, D_FF_EXPERT), wmap),
                pl.BlockSpec((None, None, D_FF_EXPERT, D_MODEL), wmap),
            ],
            out_specs=slots,
        ),
        out_shape=jax.ShapeDtypeStruct(xs.shape, F32),
        compiler_params=_params("arbitrary"),
        name="moe_experts",
    )(block_expert, n_used, xs, w1, w3, w2)


def _combine_kernel(dest_ref, h_ref, gate_ref, y_ref, o_ref, buf_ref, sem):
    step = pl.program_id(0)

    def issue(s, slot):
        base = s * ROW_TILE

        def body(g, carry):
            r0 = g * ISSUE_TOKENS
            slots = [dest_ref[(base + r0) * TOP_K + j] for j in range(ISSUE_TOKENS * TOP_K)]
            for j, src in enumerate(slots):
                _token_copy(y_ref, src, buf_ref.at[slot, j % TOP_K], r0 + j // TOP_K,
                            sem.at[slot]).start()
            return carry

        lax.fori_loop(0, ROW_TILE // ISSUE_TOKENS, body, 0)

    @pl.when(step == 0)
    def _():
        issue(0, 0)

    for slot in range(2):
        @pl.when(step % 2 == slot)
        def _():
            @pl.when(step + 1 < pl.num_programs(0))
            def _():
                issue(step + 1, 1 - slot)

            for k in range(TOP_K):
                pltpu.make_async_copy(y_ref.at[pl.ds(0, ROW_TILE * TOKEN_ROWS), :],
                                      buf_ref.at[slot, k], sem.at[slot]).wait()
            gate = gate_ref[...]
            o_ref[...] = (h_ref[...]
                          + gate[:, 0:1] * _load_token_major(buf_ref.at[slot, 0], ROW_TILE)
                          + gate[:, 1:2] * _load_token_major(buf_ref.at[slot, 1], ROW_TILE))


def _combine(dest, h, gate, y):
    n = h.shape[0]
    return pl.pallas_call(
        _combine_kernel,
        grid_spec=pltpu.PrefetchScalarGridSpec(
            num_scalar_prefetch=1,
            grid=(n // ROW_TILE,),
            in_specs=[
                pl.BlockSpec((ROW_TILE, D_MODEL), lambda i, d: (i, 0)),
                pl.BlockSpec((ROW_TILE, TOP_K), lambda i, d: (i, 0)),
                pl.BlockSpec(memory_space=pl.ANY),
            ],
            out_specs=pl.BlockSpec((ROW_TILE, D_MODEL), lambda i, d: (i, 0)),
            scratch_shapes=[pltpu.VMEM((2, TOP_K, ROW_TILE * TOKEN_ROWS, LANES), F32),
                            pltpu.SemaphoreType.DMA((2,))],
        ),
        out_shape=jax.ShapeDtypeStruct((n, D_MODEL), F32),
        compiler_params=_params("arbitrary"),
        name="moe_combine",
    )(dest, h, gate, y)


def _route(logits, n_slots):
    n = logits.shape[0]
    eid = lax.broadcasted_iota(jnp.int32, logits.shape, 1)
    i1 = jnp.argmax(logits, axis=-1).astype(jnp.int32)
    l1 = jnp.max(logits, axis=-1)
    rest = jnp.where(eid == i1[:, None], -jnp.inf, logits)
    i2 = jnp.argmax(rest, axis=-1).astype(jnp.int32)
    l2 = jnp.max(rest, axis=-1)
    e2 = jnp.exp(l2 - l1)
    gate = jnp.stack([1.0 / (1.0 + e2), e2 / (1.0 + e2)], axis=-1)

    flat_e = jnp.stack([i1, i2], axis=-1).reshape(n * TOP_K)
    onehot = (flat_e[:, None] == jnp.arange(N_EXPERTS, dtype=jnp.int32)[None, :]).astype(jnp.int32)
    csum = jnp.cumsum(onehot, axis=0)
    rank = jnp.sum(csum * onehot, axis=-1) - 1
    counts = csum[-1]
    padded = (counts + MOE_TILE - 1) // MOE_TILE * MOE_TILE
    pad_end = jnp.cumsum(padded)
    pad_start = pad_end - padded
    dest = (pad_start[flat_e] + rank).astype(jnp.int32)
    n_blocks = n_slots // MOE_TILE
    block_expert = jnp.minimum(
        jnp.searchsorted(pad_end, jnp.arange(n_blocks, dtype=jnp.int32) * MOE_TILE, side="right"),
        N_EXPERTS - 1).astype(jnp.int32)
    n_used = (pad_end[-1:] // MOE_TILE).astype(jnp.int32)
    return dest, gate, block_expert, n_used


def _moe_ffn(h, u, logits, w1, w3, w2, idx):
    n = h.shape[0]
    n_slots = (n * TOP_K // MOE_TILE + N_EXPERTS) * MOE_TILE
    dest, gate, block_expert, n_used = _route(logits, n_slots)
    xs = _dispatch(dest, u, n_slots)
    y = _experts(block_expert, n_used, xs, w1, w3, w2, idx)
    return _combine(dest, h, gate, y)


def _final_kernel(h_ref, g_ref, o_ref):
    o_ref[...] = _rms(h_ref[...], g_ref[...])


def _final_norm(h3, gain, seq):
    batch = h3.shape[0]
    lead_blocks = LEAD // Q_BLOCK
    return pl.pallas_call(
        _final_kernel,
        grid=(seq // Q_BLOCK,),
        in_specs=[
            pl.BlockSpec((batch, Q_BLOCK, D_MODEL), lambda j: (0, j + lead_blocks, 0)),
            pl.BlockSpec((1, D_MODEL), lambda j: (0, 0)),
        ],
        out_specs=pl.BlockSpec((batch, Q_BLOCK, D_MODEL), lambda j: (0, j, 0)),
        out_shape=jax.ShapeDtypeStruct((batch, seq, D_MODEL), F32),
        compiler_params=_params("parallel"),
        name="final_norm",
    )(h3, gain)


def kernel(x, meta, rel_bias, norm_mix, w_in, hg_lb_logits, hg_norm_w, da_lambda, da_subln_w, w_out, norm_ffn, dense_w1, dense_w3, dense_w2, moe_router, moe_w1, moe_w3, moe_w2, final_norm):
    batch, seq, d = x.shape
    length = LEAD + seq
    h = jnp.concatenate([
        jnp.zeros((batch, LEAD - N_META, d), x.dtype),
        jnp.broadcast_to(meta[None].astype(x.dtype), (batch, N_META, d)),
        x], axis=1).reshape(batch * length, d)

    toe = _attn_bias_tables(rel_bias)
    lb_cum = jnp.cumsum(jax.nn.softmax(hg_lb_logits.astype(F32), axis=0), axis=0)
    lb_all = jnp.clip(lb_cum - lb_cum[0:1], 0.0, LB_MAX)
    log_lb = jnp.log(lb_all)
    log_1m_lb = jnp.log1p(-lb_all)

    w_in_b, w_out_b = w_in.astype(BF16), w_out.astype(BF16)
    dense_b = [w.astype(BF16) for w in (dense_w1, dense_w3, dense_w2)]
    moe_b = [w.astype(BF16) for w in (moe_w1, moe_w3, moe_w2)]

    for l in range(DEPTH):
        hg, da = _mix_in(h, norm_mix[l][None], w_in_b, l)
        o_hg = _hgrn(hg, log_lb[l][None], log_1m_lb[l][None], hg_norm_w[l][None], batch, length)
        lam_init = 0.8 - 0.6 * math.exp(-0.3 * l)
        lv = da_lambda[l].astype(F32)
        lam = jnp.exp(jnp.sum(lv[0] * lv[1])) - jnp.exp(jnp.sum(lv[2] * lv[3])) + lam_init
        cst = jnp.zeros((8, LANES), F32).at[0].set(lam).at[1].set(1.0 - lam_init)
        o_da = _attn(da, toe, cst, da_subln_w[l][None], batch, length)
        i = l // 2
        if l % 2 == 0:
            hn, u = _out_proj(h, o_hg, o_da, w_out_b, l, norm_ffn[l][None])
            h = _dense_ffn(hn, u, *dense_b, i)
        else:
            router = jnp.zeros((d, LANES), F32).at[:, :N_EXPERTS].set(moe_router[i].astype(F32))
            hn, u, lg = _out_proj(h, o_hg, o_da, w_out_b, l, norm_ffn[l][None], router)
            h = _moe_ffn(hn, u, lg[:, :N_EXPERTS], *moe_b, i)
    return _final_norm(h.reshape(batch, length, d), final_norm[None], seq)
```

```python
import functools
import math

import jax
import jax.numpy as jnp
import numpy as np
from jax import lax
from jax.experimental import pallas as pl
from jax.experimental.pallas import tpu as pltpu

D_MODEL = 1024
DEPTH = 4
N_META = 16
LEAD = 128
HG_WIDTH = 512
HG_HEADS = 4
HG_D = 128
HG_CHUNK = 64
DA_HEADS = 4
DA_DQK = 64
DA_DV = 128
Q_BLOCK = 128
KEY_TILE = 2 * Q_BLOCK
LOG2E = math.log2(math.e)
Q_SCALE = DA_DQK ** -0.5 * LOG2E
REL_BUCKETS = 32
REL_MAX_DIST = 128
N_EXPERTS = 8
TOP_K = 2
D_FF_EXPERT = 3584
EPS = 1e-6
NEG = -1e30
LB_MAX = 0.999
HG_COLS = 4 * HG_WIDTH
DA_COLS = 3 * DA_HEADS * DA_DV
W_IN_COLS = HG_COLS + DA_COLS

LANES = 128
VMEM_LIMIT = 56 * 1024 * 1024

ROW_TILE = 256
HG_TILE = 128
MOE_TILE = 256
FF_CHUNK = 512

F32 = jnp.float32
BF16 = jnp.bfloat16


def _params(*sem):
    return pltpu.CompilerParams(dimension_semantics=sem, vmem_limit_bytes=VMEM_LIMIT)


def _dot(a, b):
    return jnp.dot(a, b, preferred_element_type=F32)


def _dot_nt(a, b):
    return lax.dot_general(a, b, (((1,), (1,)), ((), ())), preferred_element_type=F32)


def _dot_tn(a, b):
    return lax.dot_general(a, b, (((0,), (0,)), ((), ())), preferred_element_type=F32)


def _rms(x, gain):
    return x * lax.rsqrt(jnp.mean(x * x, axis=-1, keepdims=True) + EPS) * gain


TOKEN_ROWS = D_MODEL // LANES


def _store_token_major(ref, x):
    t = x.shape[0]
    for s in range(TOKEN_ROWS):
        ref[pl.ds(s, t, stride=TOKEN_ROWS), :] = x[:, s * LANES:(s + 1) * LANES]


def _load_token_major(ref, t):
    return jnp.concatenate(
        [ref[pl.ds(s, t, stride=TOKEN_ROWS), :] for s in range(TOKEN_ROWS)], axis=1)


def _silu(x):
    return x * (0.5 * jnp.tanh(0.5 * x) + 0.5)


def _mix_in_kernel(x_ref, g_ref, w_ref, hg_ref, da_ref):
    u = _rms(x_ref[...], g_ref[...]).astype(BF16)
    hg_ref[...] = _dot(u, w_ref[:, :HG_COLS])
    n_q = DA_HEADS * 2 * DA_DQK
    da_ref[:, :n_q] = (_dot(u, w_ref[:, HG_COLS:HG_COLS + n_q]) * Q_SCALE).astype(BF16)
    da_ref[:, n_q:] = _dot(u, w_ref[:, HG_COLS + n_q:]).astype(BF16)


def _mix_in(h, gain, w, layer):
    n = h.shape[0]
    return pl.pallas_call(
        _mix_in_kernel,
        grid=(n // ROW_TILE,),
        in_specs=[
            pl.BlockSpec((ROW_TILE, D_MODEL), lambda i: (i, 0)),
            pl.BlockSpec((1, D_MODEL), lambda i: (0, 0)),
            pl.BlockSpec((None, D_MODEL, W_IN_COLS), lambda i: (layer, 0, 0)),
        ],
        out_specs=[
            pl.BlockSpec((ROW_TILE, HG_COLS), lambda i: (i, 0)),
            pl.BlockSpec((ROW_TILE, DA_COLS), lambda i: (i, 0)),
        ],
        out_shape=[
            jax.ShapeDtypeStruct((n, HG_COLS), F32),
            jax.ShapeDtypeStruct((n, DA_COLS), BF16),
        ],
        compiler_params=_params("parallel"),
        name="mix_in",
    )(h, gain, w)


HG_LEVELS = (32, 16, 8, 4, 2, 1)
N_SUMS = len(HG_LEVELS) + 2


def _hgrn_consts():
    c = HG_CHUNK
    t = np.arange(c)[:, None]
    j = np.arange(c)[None, :]
    sums = np.zeros((N_SUMS, c, c), np.float32)
    masks = np.zeros((len(HG_LEVELS) + 1, c, c), np.float32)
    sums[0] = j <= t
    masks[0] = np.eye(c)
    for li, w in enumerate(HG_LEVELS, start=1):
        ref = (t // (2 * w)) * (2 * w) + w
        sums[li] = np.where(t >= ref, (j > ref) & (j <= t), (j > t) & (j <= ref))
        masks[li] = (t // (2 * w) == j // (2 * w)) & (t % (2 * w) >= w) & (j % (2 * w) < w)
    sums[N_SUMS - 1] = j > t
    sums = sums.reshape(N_SUMS * c, c)
    return np.concatenate([sums, sums], axis=1), masks


_HG_SUMS, _HG_MASKS = _hgrn_consts()


def _hgrn_kernel(hg_ref, loga_ref, log1m_ref, nw_ref, sums_ref, masks_ref, o_ref, state_ref):
    c_idx = pl.program_id(1)

    @pl.when(c_idx == 0)
    def _():
        state_ref[...] = jnp.zeros_like(state_ref)

    C = HG_CHUNK
    W = HG_WIDTH
    n_chunks = HG_TILE // C
    sums = sums_ref[...]
    nw = nw_ref[...]
    row_idx = c_idx * HG_TILE + lax.broadcasted_iota(jnp.int32, (HG_TILE, 1), 0)
    valid = row_idx >= (LEAD - N_META)
    step = lax.broadcasted_iota(jnp.int32, (C, 1), 0)
    head_cols = [slice(hd * HG_D, (hd + 1) * HG_D) for hd in range(HG_HEADS)]

    f = hg_ref[:, W:2 * W]
    qf = _silu(hg_ref[:, 0:W])
    ls = jnp.minimum(f, 0.0) - jnp.log(1.0 + jnp.exp(-jnp.abs(f)))
    cc = log1m_ref[...] + ls
    loga = loga_ref[...]
    lf = jnp.maximum(loga, cc) + jnp.log(1.0 + jnp.exp(-jnp.abs(loga - cc)))
    kk = jnp.exp(cc - f)
    lf = jnp.where(valid, lf, 0.0)
    kk = jnp.where(valid, kk, 0.0)
    lf2 = lf * LOG2E
    lf_hi = lf2.astype(BF16)
    lf_lo = (lf2 - lf_hi.astype(F32)).astype(BF16)
    vb = hg_ref[:, 2 * W:3 * W].astype(BF16)
    gate = _silu(hg_ref[:, 3 * W:4 * W])
    qb = qf.astype(BF16)
    kb = kk.astype(BF16)

    chunks = []
    for ch in range(n_chunks):
        rows = slice(ch * C, (ch + 1) * C)
        e = jnp.exp2(_dot(sums, jnp.concatenate([lf_hi[rows], lf_lo[rows]], axis=0)))
        e_b = e[0:C]
        q_in = (qf[rows] * e_b).astype(BF16)
        k_out = (kk[rows] * e[(N_SUMS - 1) * C:N_SUMS * C]).astype(BF16)
        z = [(jnp.where((step & w) != 0, qf[rows], kk[rows]) * e[li * C:(li + 1) * C]).astype(BF16)
             for li, w in enumerate(HG_LEVELS, start=1)]
        scores = []
        for cols in head_cols:
            s = masks_ref[0] * _dot_nt(qb[rows, cols], kb[rows, cols])
            for li in range(1, len(HG_LEVELS) + 1):
                zl = z[li - 1][:, cols]
                s += masks_ref[li] * _dot_nt(zl, zl)
            scores.append(s.astype(BF16))
        chunks.append((rows, e_b[C - 1:C, :], q_in, k_out, scores))

    for rows, decay_end, q_in, k_out, scores in chunks:
        for hd, cols in enumerate(head_cols):
            st = state_ref[hd]
            v_h = vb[rows, cols]
            o = _dot_nt(q_in[:, cols], st.astype(BF16)) + _dot(scores[hd], v_h)
            state_ref[hd] = st * decay_end[:, cols] + _dot_tn(v_h, k_out[:, cols])
            o = _rms(o, nw) * gate[rows, cols]
            o_ref[rows, cols] = o.astype(o_ref.dtype)


def _hgrn(hg, loga, log1m, norm_w, batch, length):
    hg3 = hg.reshape(batch, length, HG_COLS)
    out = pl.pallas_call(
        _hgrn_kernel,
        grid=(batch, length // HG_TILE),
        in_specs=[
            pl.BlockSpec((None, HG_TILE, HG_COLS), lambda b, c: (b, c, 0)),
            pl.BlockSpec((1, HG_WIDTH), lambda b, c: (0, 0)),
            pl.BlockSpec((1, HG_WIDTH), lambda b, c: (0, 0)),
            pl.BlockSpec((1, HG_D), lambda b, c: (0, 0)),
            pl.BlockSpec(_HG_SUMS.shape, lambda b, c: (0, 0)),
            pl.BlockSpec(_HG_MASKS.shape, lambda b, c: (0, 0, 0)),
        ],
        out_specs=pl.BlockSpec((None, HG_TILE, HG_WIDTH), lambda b, c: (b, c, 0)),
        out_shape=jax.ShapeDtypeStruct((batch, length, HG_WIDTH), BF16),
        scratch_shapes=[pltpu.VMEM((HG_HEADS, HG_D, HG_D), F32)],
        compiler_params=_params("parallel", "arbitrary"),
        name="hgrn2",
    )(hg3, loga, log1m, norm_w, jnp.asarray(_HG_SUMS, BF16), jnp.asarray(_HG_MASKS, F32))
    return out.reshape(batch * length, HG_WIDTH)


def _attn_kernel(q_ref, k_ref, v_ref, toe_ref, cst_ref, w_ref, o_ref, s_ref, *, n_blocks):
    lam = cst_ref[0:1, 0:1]
    post = cst_ref[1:2, :]
    lane = lax.broadcasted_iota(jnp.int32, (Q_BLOCK, Q_BLOCK), 1)
    first_half = lane < DA_DQK
    key_ok0 = lane >= (LEAD - N_META)
    inert_bias = jnp.where(key_ok0, 0.0, NEG)

    def near_bias(kind, kb):
        bias = toe_ref[kind]
        if kb == 0:
            bias = jnp.where(key_ok0, bias, NEG)
        return bias

    def slabs(x):
        return [x[:, c:c + Q_BLOCK] for c in range(0, x.shape[1], Q_BLOCK)]

    def both_maps(bias):
        return jnp.concatenate([bias, bias], axis=0)

    def stacked_q(i):
        qi = q_ref[i * Q_BLOCK:(i + 1) * Q_BLOCK, :]
        zero = jnp.zeros_like(qi)
        return jnp.concatenate([jnp.where(first_half, qi, zero), jnp.where(first_half, zero, qi)],
                               axis=0)

    def key_tiles(i):
        tiles = []
        far_end = max(i - 1, 0)
        if far_end >= 1:
            tiles.append((0, Q_BLOCK, both_maps(inert_bias)))
        kb = 1
        while kb < far_end:
            width = KEY_TILE if kb + KEY_TILE // Q_BLOCK <= far_end else Q_BLOCK
            tiles.append((kb * Q_BLOCK, width, None))
            kb += width // Q_BLOCK
        if i >= 1:
            tiles.append(((i - 1) * Q_BLOCK, Q_BLOCK, both_maps(near_bias(1, i - 1))))
        tiles.append((i * Q_BLOCK, Q_BLOCK, both_maps(near_bias(0, i))))
        return tiles

    def scores(q2, tile):
        start, width, bias = tile
        s = _dot_nt(q2, k_ref[start:start + width, :])
        return s if bias is None else s + bias

    def sweep_scores(i):
        q2 = stacked_q(i)
        m_acc = None
        for tile in key_tiles(i):
            s = scores(q2, tile)
            s_ref[i % 2, :, tile[0]:tile[0] + tile[1]] = s
            for slab in slabs(s):
                m_acc = slab if m_acc is None else jnp.maximum(m_acc, slab)
        return m_acc.max(axis=-1, keepdims=True)

    row_max = sweep_scores(0)
    for i in range(n_blocks):
        m = row_max
        if i + 1 < n_blocks:
            row_max = sweep_scores(i + 1)
        l_acc = o_acc = None
        for start, width, _ in key_tiles(i):
            p = jnp.exp2(s_ref[i % 2, :, start:start + width] - m)
            for slab in slabs(p):
                l_acc = slab if l_acc is None else l_acc + slab
            part = _dot(p.astype(BF16), v_ref[start:start + width, :])
            o_acc = part if o_acc is None else o_acc + part
        o2 = o_acc * (1.0 / l_acc.sum(axis=-1, keepdims=True))
        o = o2[:Q_BLOCK] - lam * o2[Q_BLOCK:]
        o = _rms(o, w_ref[...]) * post
        o_ref[i * Q_BLOCK:(i + 1) * Q_BLOCK, :] = o.astype(o_ref.dtype)


def _attn(da, toe, cst, subln_w, batch, length):
    da3 = da.reshape(batch, length, DA_COLS)
    hw = DA_HEADS
    out = pl.pallas_call(
        functools.partial(_attn_kernel, n_blocks=length // Q_BLOCK),
        grid=(batch, DA_HEADS),
        in_specs=[
            pl.BlockSpec((None, length, DA_DV), lambda b, h: (b, 0, h)),
            pl.BlockSpec((None, length, DA_DV), lambda b, h: (b, 0, hw + h)),
            pl.BlockSpec((None, length, DA_DV), lambda b, h: (b, 0, 2 * hw + h)),
            pl.BlockSpec((None, 2, Q_BLOCK, Q_BLOCK), lambda b, h: (h, 0, 0, 0)),
            pl.BlockSpec((8, LANES), lambda b, h: (0, 0)),
            pl.BlockSpec((1, DA_DV), lambda b, h: (0, 0)),
        ],
        out_specs=pl.BlockSpec((None, length, DA_DV), lambda b, h: (b, 0, h)),
        out_shape=jax.ShapeDtypeStruct((batch, length, DA_HEADS * DA_DV), BF16),
        scratch_shapes=[pltpu.VMEM((2, 2 * Q_BLOCK, length), F32)],
        compiler_params=_params("parallel", "parallel"),
        name="diff_attn",
    )(da3, da3, da3, toe, cst, subln_w)
    return out.reshape(batch * length, DA_HEADS * DA_DV)


def _t5_bucket(dist):
    n = jnp.maximum(dist, 0)
    max_exact = REL_BUCKETS // 2
    nf = jnp.maximum(n, max_exact).astype(F32)
    large = max_exact + (jnp.log(nf / max_exact) / math.log(REL_MAX_DIST / max_exact)
                         * (REL_BUCKETS - max_exact)).astype(jnp.int32)
    large = jnp.minimum(large, REL_BUCKETS - 1)
    return jnp.where(n < max_exact, n, large)


def _attn_bias_tables(rel_bias):
    tab = rel_bias.astype(F32)
    qi = jnp.arange(Q_BLOCK, dtype=jnp.int32)[:, None]
    ki = jnp.arange(Q_BLOCK, dtype=jnp.int32)[None, :]

    def lookup(bucket):
        onehot = bucket[None, :, :, None] == jnp.arange(REL_BUCKETS, dtype=jnp.int32)
        return jnp.sum(jnp.where(onehot, tab.T[:, None, None, :], 0.0), axis=-1)

    far = tab[REL_BUCKETS - 1][:, None, None]
    diag = jnp.where((ki <= qi)[None], (lookup(_t5_bucket(qi - ki)) - far) * LOG2E, NEG)
    prev = (lookup(_t5_bucket(qi - ki + Q_BLOCK)) - far) * LOG2E
    return jnp.stack([diag, prev], axis=1)


def _out_proj_kernel(h_ref, ohg_ref, oda_ref, wo_ref, g_ref, *rest, with_router):
    if with_router:
        router_ref, hn_ref, u_ref, lg_ref = rest
    else:
        hn_ref, u_ref = rest
    hn = (h_ref[...] + _dot(ohg_ref[...], wo_ref[:HG_WIDTH, :])
          + _dot(oda_ref[...], wo_ref[HG_WIDTH:, :]))
    hn_ref[...] = hn
    u = _rms(hn, g_ref[...])
    if with_router:
        _store_token_major(u_ref, u)
        u_hi = u.astype(BF16)
        u_lo = (u - u_hi.astype(F32)).astype(BF16)
        lg_ref[...] = _dot(jnp.concatenate([u_hi, u_lo, u_hi], axis=1), router_ref[...])
    else:
        u_ref[...] = u.astype(u_ref.dtype)


def _out_proj(h, o_hg, o_da, wo, layer, gain, router=None):
    n = h.shape[0]
    with_router = router is not None
    row = lambda i: (i, 0)
    full = lambda i: (0, 0)
    in_specs = [
        pl.BlockSpec((ROW_TILE, D_MODEL), row),
        pl.BlockSpec((ROW_TILE, HG_WIDTH), row),
        pl.BlockSpec((ROW_TILE, DA_HEADS * DA_DV), row),
        pl.BlockSpec((None,) + wo.shape[1:], lambda i: (layer, 0, 0)),
        pl.BlockSpec((1, D_MODEL), full),
    ]
    out_specs = [pl.BlockSpec((ROW_TILE, D_MODEL), row), pl.BlockSpec((ROW_TILE, D_MODEL), row)]
    out_shape = [jax.ShapeDtypeStruct((n, D_MODEL), F32), jax.ShapeDtypeStruct((n, D_MODEL), BF16)]
    args = [h, o_hg, o_da, wo, gain]
    if with_router:
        out_specs[1] = pl.BlockSpec((ROW_TILE * TOKEN_ROWS, LANES), row)
        out_shape[1] = jax.ShapeDtypeStruct((n * TOKEN_ROWS, LANES), F32)
        in_specs.append(pl.BlockSpec(router.shape, full))
        out_specs.append(pl.BlockSpec((ROW_TILE, LANES), row))
        out_shape.append(jax.ShapeDtypeStruct((n, LANES), F32))
        args.append(router)
    return pl.pallas_call(
        functools.partial(_out_proj_kernel, with_router=with_router),
        grid=(n // ROW_TILE,),
        in_specs=in_specs,
        out_specs=out_specs,
        out_shape=out_shape,
        compiler_params=_params("parallel"),
        name="out_proj_router" if with_router else "out_proj",
    )(*args)


def _dense_ffn_kernel(h_ref, u_ref, w1_ref, w3_ref, w2_ref, o_ref):
    u = u_ref[...]
    a = _dot(u, w1_ref[...])
    act = (_silu(a) * _dot(u, w3_ref[...])).astype(BF16)
    o_ref[...] = h_ref[...] + _dot(act, w2_ref[...])


def _dense_ffn(h, u, w1, w3, w2, idx):
    n = h.shape[0]
    row = lambda i: (i, 0)
    stacked = lambda i: (idx, 0, 0)
    return pl.pallas_call(
        _dense_ffn_kernel,
        grid=(n // ROW_TILE,),
        in_specs=[
            pl.BlockSpec((ROW_TILE, D_MODEL), row),
            pl.BlockSpec((ROW_TILE, D_MODEL), row),
            pl.BlockSpec((None,) + w1.shape[1:], stacked),
            pl.BlockSpec((None,) + w3.shape[1:], stacked),
            pl.BlockSpec((None,) + w2.shape[1:], stacked),
        ],
        out_specs=pl.BlockSpec((ROW_TILE, D_MODEL), row),
        out_shape=jax.ShapeDtypeStruct((n, D_MODEL), F32),
        compiler_params=_params("parallel"),
        name="dense_ffn",
    )(h, u, w1, w3, w2)


DISPATCH_TILE = 1024
ISSUE_TOKENS = 8


def _token_copy(src_ref, src_tok, dst_ref, dst_tok, sem):
    src = pl.multiple_of(src_tok * TOKEN_ROWS, TOKEN_ROWS)
    dst = pl.multiple_of(dst_tok * TOKEN_ROWS, TOKEN_ROWS)
    return pltpu.make_async_copy(src_ref.at[pl.ds(src, TOKEN_ROWS), :],
                                 dst_ref.at[pl.ds(dst, TOKEN_ROWS), :], sem)


def _dispatch_kernel(dest_ref, u_ref, xs_in_ref, xs_ref, sem):
    del xs_in_ref
    base = pl.program_id(0) * DISPATCH_TILE

    def start(g, carry):
        r0 = g * ISSUE_TOKENS
        slots = [dest_ref[(base + r0) * TOP_K + j] for j in range(ISSUE_TOKENS * TOP_K)]
        for j, slot in enumerate(slots):
            _token_copy(u_ref, r0 + j // TOP_K, xs_ref, slot, sem).start(priority=j % 2)
        return carry

    lax.fori_loop(0, DISPATCH_TILE // ISSUE_TOKENS, start, 0)
    rows = DISPATCH_TILE * TOKEN_ROWS
    for _ in range(TOP_K):
        pltpu.make_async_copy(u_ref, xs_ref.at[pl.ds(0, rows), :], sem).wait()


def _dispatch(dest, u_tm, n_slots):
    n = dest.shape[0] // TOP_K
    zeros = jnp.zeros((n_slots * TOKEN_ROWS, LANES), u_tm.dtype)
    return pl.pallas_call(
        _dispatch_kernel,
        grid_spec=pltpu.PrefetchScalarGridSpec(
            num_scalar_prefetch=1,
            grid=(n // DISPATCH_TILE,),
            in_specs=[pl.BlockSpec((DISPATCH_TILE * TOKEN_ROWS, LANES), lambda i, d: (i, 0)),
                      pl.BlockSpec(memory_space=pl.ANY)],
            out_specs=pl.BlockSpec(memory_space=pl.ANY),
            scratch_shapes=[pltpu.SemaphoreType.DMA(())],
        ),
        out_shape=jax.ShapeDtypeStruct(zeros.shape, u_tm.dtype),
        input_output_aliases={2: 0},
        compiler_params=_params("arbitrary"),
        name="moe_dispatch",
    )(dest, u_tm, zeros)


def _expert_kernel(be_ref, nused_ref, xs_ref, w1_ref, w3_ref, w2_ref, y_ref):
    del be_ref
    blk = pl.program_id(0)

    @pl.when(blk < nused_ref[0])
    def _():
        x = _load_token_major(xs_ref, MOE_TILE).astype(BF16)
        acc = jnp.zeros((MOE_TILE, D_MODEL), F32)
        for c0 in range(0, D_FF_EXPERT, FF_CHUNK):
            a = _dot(x, w1_ref[:, c0:c0 + FF_CHUNK])
            act = (_silu(a) * _dot(x, w3_ref[:, c0:c0 + FF_CHUNK])).astype(BF16)
            acc = acc + _dot(act, w2_ref[c0:c0 + FF_CHUNK, :])
        _store_token_major(y_ref, acc)

    @pl.when(blk >= nused_ref[0])
    def _():
        y_ref[...] = jnp.zeros_like(y_ref)


def _experts(block_expert, n_used, xs, w1, w3, w2, idx):
    n_blocks = xs.shape[0] // (MOE_TILE * TOKEN_ROWS)
    wmap = lambda i, be, nu: (idx, be[i], 0, 0)
    slots = pl.BlockSpec((MOE_TILE * TOKEN_ROWS, LANES), lambda i, be, nu: (i, 0))
    return pl.pallas_call(
        _expert_kernel,
        grid_spec=pltpu.PrefetchScalarGridSpec(
            num_scalar_prefetch=2,
            grid=(n_blocks,),
            in_specs=[
                slots,
                pl.BlockSpec((None, None, D_MODEL, D_FF_EXPERT), wmap),
                pl.BlockSpec((None, None, D_MODEL, D_FF_EXPERT), wmap),
                pl.BlockSpec((None, None, D_FF_EXPERT, D_MODEL), wmap),
            ],
            out_specs=slots,
        ),
        out_shape=jax.ShapeDtypeStruct(xs.shape, F32),
        compiler_params=_params("arbitrary"),
        name="moe_experts",
    )(block_expert, n_used, xs, w1, w3, w2)


def _combine_kernel(dest_ref, h_ref, gate_ref, y_ref, o_ref, buf_ref, sem):
    step = pl.program_id(0)

    def issue(s, slot):
        base = s * ROW_TILE

        def body(g, carry):
            r0 = g * ISSUE_TOKENS
            slots = [dest_ref[(base + r0) * TOP_K + j] for j in range(ISSUE_TOKENS * TOP_K)]
            for j, src in enumerate(slots):
                _token_copy(y_ref, src, buf_ref.at[slot, j % TOP_K], r0 + j // TOP_K,
                            sem.at[slot]).start(priority=j % 2)
            return carry

        lax.fori_loop(0, ROW_TILE // ISSUE_TOKENS, body, 0)

    @pl.when(step == 0)
    def _():
        issue(0, 0)

    for slot in range(2):
        @pl.when(step % 2 == slot)
        def _():
            @pl.when(step + 1 < pl.num_programs(0))
            def _():
                issue(step + 1, 1 - slot)

            for k in range(TOP_K):
                pltpu.make_async_copy(y_ref.at[pl.ds(0, ROW_TILE * TOKEN_ROWS), :],
                                      buf_ref.at[slot, k], sem.at[slot]).wait()
            gate = gate_ref[...]
            o_ref[...] = (h_ref[...]
                          + gate[:, 0:1] * _load_token_major(buf_ref.at[slot, 0], ROW_TILE)
                          + gate[:, 1:2] * _load_token_major(buf_ref.at[slot, 1], ROW_TILE))


def _combine(dest, h, gate, y):
    n = h.shape[0]
    return pl.pallas_call(
        _combine_kernel,
        grid_spec=pltpu.PrefetchScalarGridSpec(
            num_scalar_prefetch=1,
            grid=(n // ROW_TILE,),
            in_specs=[
                pl.BlockSpec((ROW_TILE, D_MODEL), lambda i, d: (i, 0)),
                pl.BlockSpec((ROW_TILE, TOP_K), lambda i, d: (i, 0)),
                pl.BlockSpec(memory_space=pl.ANY),
            ],
            out_specs=pl.BlockSpec((ROW_TILE, D_MODEL), lambda i, d: (i, 0)),
            scratch_shapes=[pltpu.VMEM((2, TOP_K, ROW_TILE * TOKEN_ROWS, LANES), F32),
                            pltpu.SemaphoreType.DMA((2,))],
        ),
        out_shape=jax.ShapeDtypeStruct((n, D_MODEL), F32),
        compiler_params=_params("arbitrary"),
        name="moe_combine",
    )(dest, h, gate, y)


def _route(logits, n_slots):
    n = logits.shape[0]
    eid = lax.broadcasted_iota(jnp.int32, logits.shape, 1)
    i1 = jnp.argmax(logits, axis=-1).astype(jnp.int32)
    l1 = jnp.max(logits, axis=-1)
    rest = jnp.where(eid == i1[:, None], -jnp.inf, logits)
    i2 = jnp.argmax(rest, axis=-1).astype(jnp.int32)
    l2 = jnp.max(rest, axis=-1)
    e2 = jnp.exp(l2 - l1)
    gate = jnp.stack([1.0 / (1.0 + e2), e2 / (1.0 + e2)], axis=-1)

    flat_e = jnp.stack([i1, i2], axis=-1).reshape(n * TOP_K)
    onehot = (flat_e[:, None] == jnp.arange(N_EXPERTS, dtype=jnp.int32)[None, :]).astype(jnp.int32)
    csum = jnp.cumsum(onehot, axis=0)
    rank = jnp.sum(csum * onehot, axis=-1) - 1
    counts = csum[-1]
    padded = (counts + MOE_TILE - 1) // MOE_TILE * MOE_TILE
    pad_end = jnp.cumsum(padded)
    pad_start = pad_end - padded
    dest = (pad_start[flat_e] + rank).astype(jnp.int32)
    n_blocks = n_slots // MOE_TILE
    block_expert = jnp.minimum(
        jnp.searchsorted(pad_end, jnp.arange(n_blocks, dtype=jnp.int32) * MOE_TILE, side="right"),
        N_EXPERTS - 1).astype(jnp.int32)
    n_used = (pad_end[-1:] // MOE_TILE).astype(jnp.int32)
    return dest, gate, block_expert, n_used


def _moe_ffn(h, u, logits, w1, w3, w2, idx):
    n = h.shape[0]
    n_slots = (n * TOP_K // MOE_TILE + N_EXPERTS) * MOE_TILE
    dest, gate, block_expert, n_used = _route(logits, n_slots)
    xs = _dispatch(dest, u, n_slots)
    y = _experts(block_expert, n_used, xs, w1, w3, w2, idx)
    return _combine(dest, h, gate, y)


def _final_kernel(h_ref, g_ref, o_ref):
    o_ref[...] = _rms(h_ref[...], g_ref[...])


def _final_norm(h3, gain, seq):
    batch = h3.shape[0]
    lead_blocks = LEAD // Q_BLOCK
    return pl.pallas_call(
        _final_kernel,
        grid=(seq // Q_BLOCK,),
        in_specs=[
            pl.BlockSpec((batch, Q_BLOCK, D_MODEL), lambda j: (0, j + lead_blocks, 0)),
            pl.BlockSpec((1, D_MODEL), lambda j: (0, 0)),
        ],
        out_specs=pl.BlockSpec((batch, Q_BLOCK, D_MODEL), lambda j: (0, j, 0)),
        out_shape=jax.ShapeDtypeStruct((batch, seq, D_MODEL), F32),
        compiler_params=_params("parallel"),
        name="final_norm",
    )(h3, gain)


def kernel(x, meta, rel_bias, norm_mix, w_in, hg_lb_logits, hg_norm_w, da_lambda, da_subln_w, w_out, norm_ffn, dense_w1, dense_w3, dense_w2, moe_router, moe_w1, moe_w3, moe_w2, final_norm):
    batch, seq, d = x.shape
    length = LEAD + seq
    h = jnp.concatenate([
        jnp.zeros((batch, LEAD - N_META, d), x.dtype),
        jnp.broadcast_to(meta[None].astype(x.dtype), (batch, N_META, d)),
        x], axis=1).reshape(batch * length, d)

    toe = _attn_bias_tables(rel_bias)
    lb_cum = jnp.cumsum(jax.nn.softmax(hg_lb_logits.astype(F32), axis=0), axis=0)
    lb_all = jnp.clip(lb_cum - lb_cum[0:1], 0.0, LB_MAX)
    log_lb = jnp.log(lb_all)
    log_1m_lb = jnp.log1p(-lb_all)

    w_in_b, w_out_b = w_in.astype(BF16), w_out.astype(BF16)
    dense_b = [w.astype(BF16) for w in (dense_w1, dense_w3, dense_w2)]
    moe_b = [w.astype(BF16) for w in (moe_w1, moe_w3, moe_w2)]

    for l in range(DEPTH):
        hg, da = _mix_in(h, norm_mix[l][None], w_in_b, l)
        o_hg = _hgrn(hg, log_lb[l][None], log_1m_lb[l][None], hg_norm_w[l][None], batch, length)
        lam_init = 0.8 - 0.6 * math.exp(-0.3 * l)
        lv = da_lambda[l].astype(F32)
        lam = jnp.exp(jnp.sum(lv[0] * lv[1])) - jnp.exp(jnp.sum(lv[2] * lv[3])) + lam_init
        cst = jnp.zeros((8, LANES), F32).at[0].set(lam).at[1].set(1.0 - lam_init)
        o_da = _attn(da, toe, cst, da_subln_w[l][None], batch, length)
        i = l // 2
        if l % 2 == 0:
            hn, u = _out_proj(h, o_hg, o_da, w_out_b, l, norm_ffn[l][None])
            h = _dense_ffn(hn, u, *dense_b, i)
        else:
            router = jnp.zeros((d, LANES), F32).at[:, :N_EXPERTS].set(moe_router[i].astype(F32))
            r_hi = router.astype(BF16)
            r_lo = (router - r_hi.astype(F32)).astype(BF16)
            router = jnp.concatenate([r_hi, r_hi, r_lo], axis=0)
            hn, u, lg = _out_proj(h, o_hg, o_da, w_out_b, l, norm_ffn[l][None], router)
            h = _moe_ffn(hn, u, lg[:, :N_EXPERTS], *moe_b, i)
    return _final_norm(h.reshape(batch, length, d), final_norm[None], seq)
```

```python
import functools
import math

import jax
import jax.numpy as jnp
import numpy as np
from jax import lax
from jax.experimental import pallas as pl
from jax.experimental.pallas import tpu as pltpu

D_MODEL = 1024
DEPTH = 4
N_META = 16
LEAD = 128
HG_WIDTH = 512
HG_HEADS = 4
HG_D = 128
HG_CHUNK = 64
DA_HEADS = 4
DA_DQK = 64
DA_DV = 128
Q_BLOCK = 128
KEY_TILE = 2 * Q_BLOCK
LOG2E = math.log2(math.e)
Q_SCALE = DA_DQK ** -0.5 * LOG2E
REL_BUCKETS = 32
REL_MAX_DIST = 128
N_EXPERTS = 8
TOP_K = 2
D_FF_EXPERT = 3584
EPS = 1e-6
NEG = -1e30
LB_MAX = 0.999
HG_COLS = 4 * HG_WIDTH
DA_COLS = 3 * DA_HEADS * DA_DV
W_IN_COLS = HG_COLS + DA_COLS

LANES = 128
VMEM_LIMIT = 56 * 1024 * 1024

ROW_TILE = 256
HG_TILE = 128
MOE_TILE = 256
FF_CHUNK = 512

F32 = jnp.float32
BF16 = jnp.bfloat16


def _params(*sem):
    return pltpu.CompilerParams(dimension_semantics=sem, vmem_limit_bytes=VMEM_LIMIT)


def _dot(a, b):
    return jnp.dot(a, b, preferred_element_type=F32)


def _dot_nt(a, b):
    return lax.dot_general(a, b, (((1,), (1,)), ((), ())), preferred_element_type=F32)


def _dot_tn(a, b):
    return lax.dot_general(a, b, (((0,), (0,)), ((), ())), preferred_element_type=F32)


def _rms(x, gain):
    return x * lax.rsqrt(jnp.mean(x * x, axis=-1, keepdims=True) + EPS) * gain


TOKEN_ROWS = D_MODEL // LANES


def _store_token_major(ref, x):
    t = x.shape[0]
    for s in range(TOKEN_ROWS):
        ref[pl.ds(s, t, stride=TOKEN_ROWS), :] = x[:, s * LANES:(s + 1) * LANES]


def _load_token_major(ref, t):
    return jnp.concatenate(
        [ref[pl.ds(s, t, stride=TOKEN_ROWS), :] for s in range(TOKEN_ROWS)], axis=1)


def _silu(x):
    return x * (0.5 * jnp.tanh(0.5 * x) + 0.5)


def _mix_in_kernel(x_ref, g_ref, w_ref, hg_ref, da_ref):
    u = _rms(x_ref[...], g_ref[...]).astype(BF16)
    hg_ref[...] = _dot(u, w_ref[:, :HG_COLS])
    n_q = DA_HEADS * 2 * DA_DQK
    da_ref[:, :n_q] = (_dot(u, w_ref[:, HG_COLS:HG_COLS + n_q]) * Q_SCALE).astype(BF16)
    da_ref[:, n_q:] = _dot(u, w_ref[:, HG_COLS + n_q:]).astype(BF16)


def _mix_in(h, gain, w, layer):
    n = h.shape[0]
    return pl.pallas_call(
        _mix_in_kernel,
        grid=(n // ROW_TILE,),
        in_specs=[
            pl.BlockSpec((ROW_TILE, D_MODEL), lambda i: (i, 0)),
            pl.BlockSpec((1, D_MODEL), lambda i: (0, 0)),
            pl.BlockSpec((None, D_MODEL, W_IN_COLS), lambda i: (layer, 0, 0)),
        ],
        out_specs=[
            pl.BlockSpec((ROW_TILE, HG_COLS), lambda i: (i, 0)),
            pl.BlockSpec((ROW_TILE, DA_COLS), lambda i: (i, 0)),
        ],
        out_shape=[
            jax.ShapeDtypeStruct((n, HG_COLS), F32),
            jax.ShapeDtypeStruct((n, DA_COLS), BF16),
        ],
        compiler_params=_params("parallel"),
        name="mix_in",
    )(h, gain, w)


HG_LEVELS = (32, 16, 8, 4, 2, 1)
N_SUMS = len(HG_LEVELS) + 2


def _hgrn_consts():
    c = HG_CHUNK
    t = np.arange(c)[:, None]
    j = np.arange(c)[None, :]
    sums = np.zeros((N_SUMS, c, c), np.float32)
    masks = np.zeros((len(HG_LEVELS) + 1, c, c), np.float32)
    sums[0] = j <= t
    masks[0] = np.eye(c)
    for li, w in enumerate(HG_LEVELS, start=1):
        ref = (t // (2 * w)) * (2 * w) + w
        sums[li] = np.where(t >= ref, (j > ref) & (j <= t), (j > t) & (j <= ref))
        masks[li] = (t // (2 * w) == j // (2 * w)) & (t % (2 * w) >= w) & (j % (2 * w) < w)
    sums[N_SUMS - 1] = j > t
    sums = sums.reshape(N_SUMS * c, c)
    return np.concatenate([sums, sums], axis=1), masks


_HG_SUMS, _HG_MASKS = _hgrn_consts()


def _hgrn_kernel(hg_ref, loga_ref, log1m_ref, nw_ref, sums_ref, masks_ref, o_ref, state_ref):
    c_idx = pl.program_id(1)

    @pl.when(c_idx == 0)
    def _():
        state_ref[...] = jnp.zeros_like(state_ref)

    C = HG_CHUNK
    W = HG_WIDTH
    n_chunks = HG_TILE // C
    sums = sums_ref[...]
    nw = nw_ref[...]
    row_idx = c_idx * HG_TILE + lax.broadcasted_iota(jnp.int32, (HG_TILE, 1), 0)
    valid = row_idx >= (LEAD - N_META)
    step = lax.broadcasted_iota(jnp.int32, (C, 1), 0)
    head_cols = [slice(hd * HG_D, (hd + 1) * HG_D) for hd in range(HG_HEADS)]

    f = hg_ref[:, W:2 * W]
    qf = _silu(hg_ref[:, 0:W])
    ls = jnp.minimum(f, 0.0) - jnp.log(1.0 + jnp.exp(-jnp.abs(f)))
    cc = log1m_ref[...] + ls
    loga = loga_ref[...]
    lf = jnp.maximum(loga, cc) + jnp.log(1.0 + jnp.exp(-jnp.abs(loga - cc)))
    kk = jnp.exp(cc - f)
    lf = jnp.where(valid, lf, 0.0)
    kk = jnp.where(valid, kk, 0.0)
    lf2 = lf * LOG2E
    lf_hi = lf2.astype(BF16)
    lf_lo = (lf2 - lf_hi.astype(F32)).astype(BF16)
    vb = hg_ref[:, 2 * W:3 * W].astype(BF16)
    gate = _silu(hg_ref[:, 3 * W:4 * W])
    qb = qf.astype(BF16)
    kb = kk.astype(BF16)

    chunks = []
    for ch in range(n_chunks):
        rows = slice(ch * C, (ch + 1) * C)
        e = jnp.exp2(_dot(sums, jnp.concatenate([lf_hi[rows], lf_lo[rows]], axis=0)))
        e_b = e[0:C]
        q_in = (qf[rows] * e_b).astype(BF16)
        k_out = (kk[rows] * e[(N_SUMS - 1) * C:N_SUMS * C]).astype(BF16)
        z = [(jnp.where((step & w) != 0, qf[rows], kk[rows]) * e[li * C:(li + 1) * C]).astype(BF16)
             for li, w in enumerate(HG_LEVELS, start=1)]
        scores = []
        for cols in head_cols:
            s = masks_ref[0] * _dot_nt(qb[rows, cols], kb[rows, cols])
            for li in range(1, len(HG_LEVELS) + 1):
                zl = z[li - 1][:, cols]
                s += masks_ref[li] * _dot_nt(zl, zl)
            scores.append(s.astype(BF16))
        chunks.append((rows, e_b[C - 1:C, :], q_in, k_out, scores))

    for rows, decay_end, q_in, k_out, scores in chunks:
        for hd, cols in enumerate(head_cols):
            st = state_ref[hd]
            v_h = vb[rows, cols]
            o = _dot_nt(q_in[:, cols], st.astype(BF16)) + _dot(scores[hd], v_h)
            state_ref[hd] = st * decay_end[:, cols] + _dot_tn(v_h, k_out[:, cols])
            o = _rms(o, nw) * gate[rows, cols]
            o_ref[rows, cols] = o.astype(o_ref.dtype)


def _hgrn(hg, loga, log1m, norm_w, batch, length):
    hg3 = hg.reshape(batch, length, HG_COLS)
    out = pl.pallas_call(
        _hgrn_kernel,
        grid=(batch, length // HG_TILE),
        in_specs=[
            pl.BlockSpec((None, HG_TILE, HG_COLS), lambda b, c: (b, c, 0)),
            pl.BlockSpec((1, HG_WIDTH), lambda b, c: (0, 0)),
            pl.BlockSpec((1, HG_WIDTH), lambda b, c: (0, 0)),
            pl.BlockSpec((1, HG_D), lambda b, c: (0, 0)),
            pl.BlockSpec(_HG_SUMS.shape, lambda b, c: (0, 0)),
            pl.BlockSpec(_HG_MASKS.shape, lambda b, c: (0, 0, 0)),
        ],
        out_specs=pl.BlockSpec((None, HG_TILE, HG_WIDTH), lambda b, c: (b, c, 0)),
        out_shape=jax.ShapeDtypeStruct((batch, length, HG_WIDTH), BF16),
        scratch_shapes=[pltpu.VMEM((HG_HEADS, HG_D, HG_D), F32)],
        compiler_params=_params("parallel", "arbitrary"),
        name="hgrn2",
    )(hg3, loga, log1m, norm_w, jnp.asarray(_HG_SUMS, BF16), jnp.asarray(_HG_MASKS, F32))
    return out.reshape(batch * length, HG_WIDTH)


def _attn_kernel(q_ref, k_ref, v_ref, toe_ref, cst_ref, w_ref, o_ref, s_ref, *, n_blocks):
    lam = cst_ref[0:1, 0:1]
    post = cst_ref[1:2, :]
    lane = lax.broadcasted_iota(jnp.int32, (Q_BLOCK, Q_BLOCK), 1)
    first_half = lane < DA_DQK
    key_ok0 = lane >= (LEAD - N_META)
    inert_bias = jnp.where(key_ok0, 0.0, NEG)

    def near_bias(kind, kb):
        bias = toe_ref[kind]
        if kb == 0:
            bias = jnp.where(key_ok0, bias, NEG)
        return bias

    def slabs(x):
        return [x[:, c:c + Q_BLOCK] for c in range(0, x.shape[1], Q_BLOCK)]

    def both_maps(bias):
        return jnp.concatenate([bias, bias], axis=0)

    def stacked_q(i):
        qi = q_ref[i * Q_BLOCK:(i + 1) * Q_BLOCK, :]
        zero = jnp.zeros_like(qi)
        return jnp.concatenate([jnp.where(first_half, qi, zero), jnp.where(first_half, zero, qi)],
                               axis=0)

    def key_tiles(i):
        tiles = []
        far_end = max(i - 1, 0)
        if far_end >= 1:
            tiles.append((0, Q_BLOCK, both_maps(inert_bias)))
        kb = 1
        while kb < far_end:
            width = KEY_TILE if kb + KEY_TILE // Q_BLOCK <= far_end else Q_BLOCK
            tiles.append((kb * Q_BLOCK, width, None))
            kb += width // Q_BLOCK
        if i >= 1:
            tiles.append(((i - 1) * Q_BLOCK, Q_BLOCK, both_maps(near_bias(1, i - 1))))
        tiles.append((i * Q_BLOCK, Q_BLOCK, both_maps(near_bias(0, i))))
        return tiles

    def scores(q2, tile):
        start, width, bias = tile
        s = _dot_nt(q2, k_ref[start:start + width, :])
        return s if bias is None else s + bias

    def sweep_scores(i):
        q2 = stacked_q(i)
        m_acc = None
        for tile in key_tiles(i):
            s = scores(q2, tile)
            s_ref[i % 2, :, tile[0]:tile[0] + tile[1]] = s
            for slab in slabs(s):
                m_acc = slab if m_acc is None else jnp.maximum(m_acc, slab)
        return m_acc.max(axis=-1, keepdims=True)

    row_max = sweep_scores(0)
    for i in range(n_blocks):
        m = row_max
        if i + 1 < n_blocks:
            row_max = sweep_scores(i + 1)
        l_acc = o_acc = None
        for start, width, _ in key_tiles(i):
            p = jnp.exp2(s_ref[i % 2, :, start:start + width] - m)
            for slab in slabs(p):
                l_acc = slab if l_acc is None else l_acc + slab
            part = _dot(p.astype(BF16), v_ref[start:start + width, :])
            o_acc = part if o_acc is None else o_acc + part
        o2 = o_acc * (1.0 / l_acc.sum(axis=-1, keepdims=True))
        o = o2[:Q_BLOCK] - lam * o2[Q_BLOCK:]
        o = _rms(o, w_ref[...]) * post
        o_ref[i * Q_BLOCK:(i + 1) * Q_BLOCK, :] = o.astype(o_ref.dtype)


def _attn(da, toe, cst, subln_w, batch, length):
    da3 = da.reshape(batch, length, DA_COLS)
    hw = DA_HEADS
    out = pl.pallas_call(
        functools.partial(_attn_kernel, n_blocks=length // Q_BLOCK),
        grid=(batch, DA_HEADS),
        in_specs=[
            pl.BlockSpec((None, length, DA_DV), lambda b, h: (b, 0, h)),
            pl.BlockSpec((None, length, DA_DV), lambda b, h: (b, 0, hw + h)),
            pl.BlockSpec((None, length, DA_DV), lambda b, h: (b, 0, 2 * hw + h)),
            pl.BlockSpec((None, 2, Q_BLOCK, Q_BLOCK), lambda b, h: (h, 0, 0, 0)),
            pl.BlockSpec((8, LANES), lambda b, h: (0, 0)),
            pl.BlockSpec((1, DA_DV), lambda b, h: (0, 0)),
        ],
        out_specs=pl.BlockSpec((None, length, DA_DV), lambda b, h: (b, 0, h)),
        out_shape=jax.ShapeDtypeStruct((batch, length, DA_HEADS * DA_DV), BF16),
        scratch_shapes=[pltpu.VMEM((2, 2 * Q_BLOCK, length), F32)],
        compiler_params=_params("parallel", "parallel"),
        name="diff_attn",
    )(da3, da3, da3, toe, cst, subln_w)
    return out.reshape(batch * length, DA_HEADS * DA_DV)


def _t5_bucket(dist):
    n = jnp.maximum(dist, 0)
    max_exact = REL_BUCKETS // 2
    nf = jnp.maximum(n, max_exact).astype(F32)
    large = max_exact + (jnp.log(nf / max_exact) / math.log(REL_MAX_DIST / max_exact)
                         * (REL_BUCKETS - max_exact)).astype(jnp.int32)
    large = jnp.minimum(large, REL_BUCKETS - 1)
    return jnp.where(n < max_exact, n, large)


def _attn_bias_tables(rel_bias):
    tab = rel_bias.astype(F32)
    qi = jnp.arange(Q_BLOCK, dtype=jnp.int32)[:, None]
    ki = jnp.arange(Q_BLOCK, dtype=jnp.int32)[None, :]

    def lookup(bucket):
        onehot = bucket[None, :, :, None] == jnp.arange(REL_BUCKETS, dtype=jnp.int32)
        return jnp.sum(jnp.where(onehot, tab.T[:, None, None, :], 0.0), axis=-1)

    far = tab[REL_BUCKETS - 1][:, None, None]
    diag = jnp.where((ki <= qi)[None], (lookup(_t5_bucket(qi - ki)) - far) * LOG2E, NEG)
    prev = (lookup(_t5_bucket(qi - ki + Q_BLOCK)) - far) * LOG2E
    return jnp.stack([diag, prev], axis=1)


def _out_proj_router_kernel(h_ref, ohg_ref, oda_ref, wo_ref, g_ref, router_ref, hn_ref, u_ref, lg_ref):
    hn = (h_ref[...] + _dot(ohg_ref[...], wo_ref[:HG_WIDTH, :])
          + _dot(oda_ref[...], wo_ref[HG_WIDTH:, :]))
    hn_ref[...] = hn
    u = _rms(hn, g_ref[...])
    _store_token_major(u_ref, u)
    u_hi = u.astype(BF16)
    u_lo = (u - u_hi.astype(F32)).astype(BF16)
    lg_ref[...] = _dot(jnp.concatenate([u_hi, u_lo, u_hi], axis=1), router_ref[...])


def _out_proj_router(h, o_hg, o_da, wo, layer, gain, router):
    n = h.shape[0]
    row = lambda i: (i, 0)
    full = lambda i: (0, 0)
    return pl.pallas_call(
        _out_proj_router_kernel,
        grid=(n // ROW_TILE,),
        in_specs=[
            pl.BlockSpec((ROW_TILE, D_MODEL), row),
            pl.BlockSpec((ROW_TILE, HG_WIDTH), row),
            pl.BlockSpec((ROW_TILE, DA_HEADS * DA_DV), row),
            pl.BlockSpec((None,) + wo.shape[1:], lambda i: (layer, 0, 0)),
            pl.BlockSpec((1, D_MODEL), full),
            pl.BlockSpec(router.shape, full),
        ],
        out_specs=[
            pl.BlockSpec((ROW_TILE, D_MODEL), row),
            pl.BlockSpec((ROW_TILE * TOKEN_ROWS, LANES), row),
            pl.BlockSpec((ROW_TILE, LANES), row),
        ],
        out_shape=[
            jax.ShapeDtypeStruct((n, D_MODEL), F32),
            jax.ShapeDtypeStruct((n * TOKEN_ROWS, LANES), F32),
            jax.ShapeDtypeStruct((n, LANES), F32),
        ],
        compiler_params=_params("parallel"),
        name="out_proj_router",
    )(h, o_hg, o_da, wo, gain, router)


def _dense_layer_kernel(h_ref, ohg_ref, oda_ref, wo_ref, g_ref, w1_ref, w3_ref, w2_ref, o_ref):
    hn = (h_ref[...] + _dot(ohg_ref[...], wo_ref[:HG_WIDTH, :])
          + _dot(oda_ref[...], wo_ref[HG_WIDTH:, :]))
    u = _rms(hn, g_ref[...]).astype(BF16)
    a = _dot(u, w1_ref[...])
    act = (_silu(a) * _dot(u, w3_ref[...])).astype(BF16)
    o_ref[...] = hn + _dot(act, w2_ref[...])


def _dense_layer(h, o_hg, o_da, wo, layer, gain, w1, w3, w2, idx):
    n = h.shape[0]
    row = lambda i: (i, 0)

    def resident(w, index):
        return pl.BlockSpec((None,) + w.shape[1:], lambda i: (index, 0, 0),
                            pipeline_mode=pl.Buffered(1))

    return pl.pallas_call(
        _dense_layer_kernel,
        grid=(n // ROW_TILE,),
        in_specs=[
            pl.BlockSpec((ROW_TILE, D_MODEL), row),
            pl.BlockSpec((ROW_TILE, HG_WIDTH), row),
            pl.BlockSpec((ROW_TILE, DA_HEADS * DA_DV), row),
            resident(wo, layer),
            pl.BlockSpec((1, D_MODEL), lambda i: (0, 0)),
            resident(w1, idx),
            resident(w3, idx),
            resident(w2, idx),
        ],
        out_specs=pl.BlockSpec((ROW_TILE, D_MODEL), row),
        out_shape=jax.ShapeDtypeStruct((n, D_MODEL), F32),
        compiler_params=_params("parallel"),
        name="dense_layer",
    )(h, o_hg, o_da, wo, gain, w1, w3, w2)


DISPATCH_TILE = 1024
ISSUE_TOKENS = 8


def _token_copy(src_ref, src_tok, dst_ref, dst_tok, sem):
    src = pl.multiple_of(src_tok * TOKEN_ROWS, TOKEN_ROWS)
    dst = pl.multiple_of(dst_tok * TOKEN_ROWS, TOKEN_ROWS)
    return pltpu.make_async_copy(src_ref.at[pl.ds(src, TOKEN_ROWS), :],
                                 dst_ref.at[pl.ds(dst, TOKEN_ROWS), :], sem)


def _dispatch_kernel(dest_ref, empty_ref, u_ref, xs_ref, zero_ref, sem):
    step = pl.program_id(0)
    base = step * DISPATCH_TILE
    tile_rows = DISPATCH_TILE * TOKEN_ROWS

    def retire_tile():
        pltpu.make_async_copy(u_ref, xs_ref.at[pl.ds(0, tile_rows), :], sem).wait()

    @pl.when(step == 0)
    def _():
        zero_ref[...] = jnp.zeros_like(zero_ref)

        def clear(g, carry):
            slots = [empty_ref[g * ISSUE_TOKENS * TOP_K + j] for j in range(ISSUE_TOKENS * TOP_K)]
            for j, slot in enumerate(slots):
                _token_copy(zero_ref, 0, xs_ref, slot, sem).start(priority=j % 2)
            return carry

        n_empty = empty_ref.shape[0]
        lax.fori_loop(0, n_empty // (ISSUE_TOKENS * TOP_K), clear, 0)
        for _ in range(n_empty // DISPATCH_TILE):
            retire_tile()

    def start(g, carry):
        r0 = g * ISSUE_TOKENS
        slots = [dest_ref[(base + r0) * TOP_K + j] for j in range(ISSUE_TOKENS * TOP_K)]
        for j, slot in enumerate(slots):
            _token_copy(u_ref, r0 + j // TOP_K, xs_ref, slot, sem).start(priority=j % 2)
        return carry

    lax.fori_loop(0, DISPATCH_TILE // ISSUE_TOKENS, start, 0)
    for _ in range(TOP_K):
        retire_tile()


def _dispatch(dest, empty_slots, u_tm, n_slots):
    n = dest.shape[0] // TOP_K
    assert empty_slots.shape[0] % DISPATCH_TILE == 0
    return pl.pallas_call(
        _dispatch_kernel,
        grid_spec=pltpu.PrefetchScalarGridSpec(
            num_scalar_prefetch=2,
            grid=(n // DISPATCH_TILE,),
            in_specs=[pl.BlockSpec((DISPATCH_TILE * TOKEN_ROWS, LANES), lambda i, d, e: (i, 0))],
            out_specs=pl.BlockSpec(memory_space=pl.ANY),
            scratch_shapes=[pltpu.VMEM((TOKEN_ROWS, LANES), u_tm.dtype),
                            pltpu.SemaphoreType.DMA(())],
        ),
        out_shape=jax.ShapeDtypeStruct((n_slots * TOKEN_ROWS, LANES), u_tm.dtype),
        compiler_params=_params("arbitrary"),
        name="moe_dispatch",
    )(dest, empty_slots, u_tm)


def _expert_kernel(be_ref, nused_ref, xs_ref, w1_ref, w3_ref, w2_ref, y_ref):
    del be_ref
    blk = pl.program_id(0)

    @pl.when(blk < nused_ref[0])
    def _():
        x = _load_token_major(xs_ref, MOE_TILE).astype(BF16)
        acc = jnp.zeros((MOE_TILE, D_MODEL), F32)
        for c0 in range(0, D_FF_EXPERT, FF_CHUNK):
            a = _dot(x, w1_ref[:, c0:c0 + FF_CHUNK])
            act = (_silu(a) * _dot(x, w3_ref[:, c0:c0 + FF_CHUNK])).astype(BF16)
            acc = acc + _dot(act, w2_ref[c0:c0 + FF_CHUNK, :])
        _store_token_major(y_ref, acc)

    @pl.when(blk >= nused_ref[0])
    def _():
        y_ref[...] = jnp.zeros_like(y_ref)


def _experts(block_expert, n_used, xs, w1, w3, w2, idx):
    n_blocks = xs.shape[0] // (MOE_TILE * TOKEN_ROWS)
    wmap = lambda i, be, nu: (idx, be[i], 0, 0)
    slots = pl.BlockSpec((MOE_TILE * TOKEN_ROWS, LANES), lambda i, be, nu: (i, 0))
    return pl.pallas_call(
        _expert_kernel,
        grid_spec=pltpu.PrefetchScalarGridSpec(
            num_scalar_prefetch=2,
            grid=(n_blocks,),
            in_specs=[
                slots,
                pl.BlockSpec((None, None, D_MODEL, D_FF_EXPERT), wmap),
                pl.BlockSpec((None, None, D_MODEL, D_FF_EXPERT), wmap),
                pl.BlockSpec((None, None, D_FF_EXPERT, D_MODEL), wmap),
            ],
            out_specs=slots,
        ),
        out_shape=jax.ShapeDtypeStruct(xs.shape, F32),
        compiler_params=_params("arbitrary"),
        name="moe_experts",
    )(block_expert, n_used, xs, w1, w3, w2)


def _combine_kernel(dest_ref, h_ref, gate_ref, y_ref, o_ref, buf_ref, sem):
    step = pl.program_id(0)

    def issue(s, slot):
        base = s * ROW_TILE

        def body(g, carry):
            r0 = g * ISSUE_TOKENS
            slots = [dest_ref[(base + r0) * TOP_K + j] for j in range(ISSUE_TOKENS * TOP_K)]
            for j, src in enumerate(slots):
                _token_copy(y_ref, src, buf_ref.at[slot, j % TOP_K], r0 + j // TOP_K,
                            sem.at[slot]).start(priority=j % 2)
            return carry

        lax.fori_loop(0, ROW_TILE // ISSUE_TOKENS, body, 0)

    @pl.when(step == 0)
    def _():
        issue(0, 0)

    for slot in range(2):
        @pl.when(step % 2 == slot)
        def _():
            @pl.when(step + 1 < pl.num_programs(0))
            def _():
                issue(step + 1, 1 - slot)

            for k in range(TOP_K):
                pltpu.make_async_copy(y_ref.at[pl.ds(0, ROW_TILE * TOKEN_ROWS), :],
                                      buf_ref.at[slot, k], sem.at[slot]).wait()
            gate = gate_ref[...]
            o_ref[...] = (h_ref[...]
                          + gate[:, 0:1] * _load_token_major(buf_ref.at[slot, 0], ROW_TILE)
                          + gate[:, 1:2] * _load_token_major(buf_ref.at[slot, 1], ROW_TILE))


def _combine(dest, h, gate, y):
    n = h.shape[0]
    return pl.pallas_call(
        _combine_kernel,
        grid_spec=pltpu.PrefetchScalarGridSpec(
            num_scalar_prefetch=1,
            grid=(n // ROW_TILE,),
            in_specs=[
                pl.BlockSpec((ROW_TILE, D_MODEL), lambda i, d: (i, 0)),
                pl.BlockSpec((ROW_TILE, TOP_K), lambda i, d: (i, 0)),
                pl.BlockSpec(memory_space=pl.ANY),
            ],
            out_specs=pl.BlockSpec((ROW_TILE, D_MODEL), lambda i, d: (i, 0)),
            scratch_shapes=[pltpu.VMEM((2, TOP_K, ROW_TILE * TOKEN_ROWS, LANES), F32),
                            pltpu.SemaphoreType.DMA((2,))],
        ),
        out_shape=jax.ShapeDtypeStruct((n, D_MODEL), F32),
        compiler_params=_params("arbitrary"),
        name="moe_combine",
    )(dest, h, gate, y)


def _route(logits, n_slots):
    n = logits.shape[0]
    eid = lax.broadcasted_iota(jnp.int32, logits.shape, 1)
    i1 = jnp.argmax(logits, axis=-1).astype(jnp.int32)
    l1 = jnp.max(logits, axis=-1)
    rest = jnp.where(eid == i1[:, None], -jnp.inf, logits)
    i2 = jnp.argmax(rest, axis=-1).astype(jnp.int32)
    l2 = jnp.max(rest, axis=-1)
    e2 = jnp.exp(l2 - l1)
    gate = jnp.stack([1.0 / (1.0 + e2), e2 / (1.0 + e2)], axis=-1)

    flat_e = jnp.stack([i1, i2], axis=-1).reshape(n * TOP_K)
    onehot = (flat_e[:, None] == jnp.arange(N_EXPERTS, dtype=jnp.int32)[None, :]).astype(jnp.int32)
    csum = jnp.cumsum(onehot, axis=0)
    rank = jnp.sum(csum * onehot, axis=-1) - 1
    counts = csum[-1]
    padded = (counts + MOE_TILE - 1) // MOE_TILE * MOE_TILE
    pad_end = jnp.cumsum(padded)
    pad_start = pad_end - padded
    dest = (pad_start[flat_e] + rank).astype(jnp.int32)
    n_blocks = n_slots // MOE_TILE
    block_expert = jnp.minimum(
        jnp.searchsorted(pad_end, jnp.arange(n_blocks, dtype=jnp.int32) * MOE_TILE, side="right"),
        N_EXPERTS - 1).astype(jnp.int32)
    n_used = (pad_end[-1:] // MOE_TILE).astype(jnp.int32)
    n_empty = n_slots - n * TOP_K
    gap_end = jnp.cumsum(padded - counts)
    j = jnp.arange(n_empty, dtype=jnp.int32)
    owner = jnp.sum((j[:, None] >= gap_end[None, :]).astype(jnp.int32), axis=-1)
    first_empty = jnp.concatenate([pad_start + counts, pad_end[-1:]])
    gap_start = jnp.concatenate([jnp.zeros((1,), gap_end.dtype), gap_end])
    pick = owner[:, None] == jnp.arange(N_EXPERTS + 1, dtype=jnp.int32)[None, :]
    empty_slots = (j + jnp.sum(jnp.where(pick, (first_empty - gap_start)[None, :], 0), axis=-1))
    return dest, gate, block_expert, n_used, empty_slots.astype(jnp.int32)


def _moe_ffn(h, u, logits, w1, w3, w2, idx):
    n = h.shape[0]
    n_slots = (n * TOP_K // MOE_TILE + N_EXPERTS) * MOE_TILE
    dest, gate, block_expert, n_used, empty_slots = _route(logits, n_slots)
    xs = _dispatch(dest, empty_slots, u, n_slots)
    y = _experts(block_expert, n_used, xs, w1, w3, w2, idx)
    return _combine(dest, h, gate, y)


def _final_kernel(h_ref, g_ref, o_ref):
    o_ref[...] = _rms(h_ref[...], g_ref[...])


def _final_norm(h3, gain, seq):
    batch = h3.shape[0]
    lead_blocks = LEAD // Q_BLOCK
    return pl.pallas_call(
        _final_kernel,
        grid=(seq // Q_BLOCK,),
        in_specs=[
            pl.BlockSpec((batch, Q_BLOCK, D_MODEL), lambda j: (0, j + lead_blocks, 0)),
            pl.BlockSpec((1, D_MODEL), lambda j: (0, 0)),
        ],
        out_specs=pl.BlockSpec((batch, Q_BLOCK, D_MODEL), lambda j: (0, j, 0)),
        out_shape=jax.ShapeDtypeStruct((batch, seq, D_MODEL), F32),
        compiler_params=_params("parallel"),
        name="final_norm",
    )(h3, gain)


def kernel(x, meta, rel_bias, norm_mix, w_in, hg_lb_logits, hg_norm_w, da_lambda, da_subln_w, w_out, norm_ffn, dense_w1, dense_w3, dense_w2, moe_router, moe_w1, moe_w3, moe_w2, final_norm):
    batch, seq, d = x.shape
    length = LEAD + seq
    h = jnp.concatenate([
        jnp.zeros((batch, LEAD - N_META, d), x.dtype),
        jnp.broadcast_to(meta[None].astype(x.dtype), (batch, N_META, d)),
        x], axis=1).reshape(batch * length, d)

    toe = _attn_bias_tables(rel_bias)
    lb_cum = jnp.cumsum(jax.nn.softmax(hg_lb_logits.astype(F32), axis=0), axis=0)
    lb_all = jnp.clip(lb_cum - lb_cum[0:1], 0.0, LB_MAX)
    log_lb = jnp.log(lb_all)
    log_1m_lb = jnp.log1p(-lb_all)

    w_in_b, w_out_b = w_in.astype(BF16), w_out.astype(BF16)
    dense_b = [w.astype(BF16) for w in (dense_w1, dense_w3, dense_w2)]
    moe_b = [w.astype(BF16) for w in (moe_w1, moe_w3, moe_w2)]

    for l in range(DEPTH):
        hg, da = _mix_in(h, norm_mix[l][None], w_in_b, l)
        o_hg = _hgrn(hg, log_lb[l][None], log_1m_lb[l][None], hg_norm_w[l][None], batch, length)
        lam_init = 0.8 - 0.6 * math.exp(-0.3 * l)
        lv = da_lambda[l].astype(F32)
        lam = jnp.exp(jnp.sum(lv[0] * lv[1])) - jnp.exp(jnp.sum(lv[2] * lv[3])) + lam_init
        cst = jnp.zeros((8, LANES), F32).at[0].set(lam).at[1].set(1.0 - lam_init)
        o_da = _attn(da, toe, cst, da_subln_w[l][None], batch, length)
        i = l // 2
        if l % 2 == 0:
            h = _dense_layer(h, o_hg, o_da, w_out_b, l, norm_ffn[l][None], *dense_b, i)
        else:
            router = jnp.zeros((d, LANES), F32).at[:, :N_EXPERTS].set(moe_router[i].astype(F32))
            r_hi = router.astype(BF16)
            r_lo = (router - r_hi.astype(F32)).astype(BF16)
            router = jnp.concatenate([r_hi, r_hi, r_lo], axis=0)
            hn, u, lg = _out_proj_router(h, o_hg, o_da, w_out_b, l, norm_ffn[l][None], router)
            h = _moe_ffn(hn, u, lg[:, :N_EXPERTS], *moe_b, i)
    return _final_norm(h.reshape(batch, length, d), final_norm[None], seq)
```

```python
import functools
import math

import jax
import jax.numpy as jnp
import numpy as np
from jax import lax
from jax.experimental import pallas as pl
from jax.experimental.pallas import tpu as pltpu

D_MODEL = 1024
DEPTH = 4
N_META = 16
LEAD = 128
HG_WIDTH = 512
HG_HEADS = 4
HG_D = 128
HG_CHUNK = 64
DA_HEADS = 4
DA_DQK = 64
DA_DV = 128
Q_BLOCK = 128
KEY_TILE = 2 * Q_BLOCK
LOG2E = math.log2(math.e)
Q_SCALE = DA_DQK ** -0.5 * LOG2E
REL_BUCKETS = 32
REL_MAX_DIST = 128
N_EXPERTS = 8
TOP_K = 2
D_FF_EXPERT = 3584
EPS = 1e-6
NEG = -1e30
LB_MAX = 0.999
HG_COLS = 4 * HG_WIDTH
DA_COLS = 3 * DA_HEADS * DA_DV
W_IN_COLS = HG_COLS + DA_COLS

LANES = 128
VMEM_LIMIT = 56 * 1024 * 1024

ROW_TILE = 256
HG_TILE = 128
MOE_TILE = 256
FF_CHUNK = 512

F32 = jnp.float32
BF16 = jnp.bfloat16


def _params(*sem):
    return pltpu.CompilerParams(dimension_semantics=sem, vmem_limit_bytes=VMEM_LIMIT)


def _dot(a, b):
    return jnp.dot(a, b, preferred_element_type=F32)


def _dot_nt(a, b):
    return lax.dot_general(a, b, (((1,), (1,)), ((), ())), preferred_element_type=F32)


def _dot_tn(a, b):
    return lax.dot_general(a, b, (((0,), (0,)), ((), ())), preferred_element_type=F32)


def _rms(x, gain):
    return x * lax.rsqrt(jnp.mean(x * x, axis=-1, keepdims=True) + EPS) * gain


TOKEN_ROWS = D_MODEL // LANES


def _store_token_major(ref, x):
    t = x.shape[0]
    for s in range(TOKEN_ROWS):
        ref[pl.ds(s, t, stride=TOKEN_ROWS), :] = x[:, s * LANES:(s + 1) * LANES]


def _load_token_major(ref, t):
    return jnp.concatenate(
        [ref[pl.ds(s, t, stride=TOKEN_ROWS), :] for s in range(TOKEN_ROWS)], axis=1)


def _silu(x):
    return x * (0.5 * jnp.tanh(0.5 * x) + 0.5)


def _mix_in_kernel(x_ref, g_ref, w_ref, hg_ref, da_ref):
    u = _rms(x_ref[...], g_ref[...]).astype(BF16)
    hg_ref[...] = _dot(u, w_ref[:, :HG_COLS])
    n_q = DA_HEADS * 2 * DA_DQK
    da_ref[:, :n_q] = (_dot(u, w_ref[:, HG_COLS:HG_COLS + n_q]) * Q_SCALE).astype(BF16)
    da_ref[:, n_q:] = _dot(u, w_ref[:, HG_COLS + n_q:]).astype(BF16)


def _mix_in(h, gain, w, layer):
    n = h.shape[0]
    return pl.pallas_call(
        _mix_in_kernel,
        grid=(n // ROW_TILE,),
        in_specs=[
            pl.BlockSpec((ROW_TILE, D_MODEL), lambda i: (i, 0)),
            pl.BlockSpec((1, D_MODEL), lambda i: (0, 0)),
            pl.BlockSpec((None, D_MODEL, W_IN_COLS), lambda i: (layer, 0, 0)),
        ],
        out_specs=[
            pl.BlockSpec((ROW_TILE, HG_COLS), lambda i: (i, 0)),
            pl.BlockSpec((ROW_TILE, DA_COLS), lambda i: (i, 0)),
        ],
        out_shape=[
            jax.ShapeDtypeStruct((n, HG_COLS), F32),
            jax.ShapeDtypeStruct((n, DA_COLS), BF16),
        ],
        compiler_params=_params("parallel"),
        name="mix_in",
    )(h, gain, w)


HG_LEVELS = (32, 16, 8, 4, 2, 1)
N_SUMS = len(HG_LEVELS) + 2


def _hgrn_consts():
    c = HG_CHUNK
    t = np.arange(c)[:, None]
    j = np.arange(c)[None, :]
    sums = np.zeros((N_SUMS, c, c), np.float32)
    masks = np.zeros((len(HG_LEVELS) + 1, c, c), np.float32)
    sums[0] = j <= t
    masks[0] = np.eye(c)
    for li, w in enumerate(HG_LEVELS, start=1):
        ref = (t // (2 * w)) * (2 * w) + w
        sums[li] = np.where(t >= ref, (j > ref) & (j <= t), (j > t) & (j <= ref))
        masks[li] = (t // (2 * w) == j // (2 * w)) & (t % (2 * w) >= w) & (j % (2 * w) < w)
    sums[N_SUMS - 1] = j > t
    sums = sums.reshape(N_SUMS * c, c)
    return np.concatenate([sums, sums], axis=1), masks


_HG_SUMS, _HG_MASKS = _hgrn_consts()


def _hgrn_kernel(hg_ref, loga_ref, log1m_ref, nw_ref, sums_ref, masks_ref, o_ref, state_ref):
    c_idx = pl.program_id(1)

    @pl.when(c_idx == 0)
    def _():
        state_ref[...] = jnp.zeros_like(state_ref)

    C = HG_CHUNK
    W = HG_WIDTH
    n_chunks = HG_TILE // C
    sums = sums_ref[...]
    nw = nw_ref[...]
    row_idx = c_idx * HG_TILE + lax.broadcasted_iota(jnp.int32, (HG_TILE, 1), 0)
    valid = row_idx >= (LEAD - N_META)
    step = lax.broadcasted_iota(jnp.int32, (C, 1), 0)
    head_cols = [slice(hd * HG_D, (hd + 1) * HG_D) for hd in range(HG_HEADS)]

    f = hg_ref[:, W:2 * W]
    qf = _silu(hg_ref[:, 0:W])
    ls = jnp.minimum(f, 0.0) - jnp.log(1.0 + jnp.exp(-jnp.abs(f)))
    cc = log1m_ref[...] + ls
    loga = loga_ref[...]
    lf = jnp.maximum(loga, cc) + jnp.log(1.0 + jnp.exp(-jnp.abs(loga - cc)))
    kk = jnp.exp(cc - f)
    lf = jnp.where(valid, lf, 0.0)
    kk = jnp.where(valid, kk, 0.0)
    lf2 = lf * LOG2E
    lf_hi = lf2.astype(BF16)
    lf_lo = (lf2 - lf_hi.astype(F32)).astype(BF16)
    vb = hg_ref[:, 2 * W:3 * W].astype(BF16)
    gate = _silu(hg_ref[:, 3 * W:4 * W])
    qb = qf.astype(BF16)
    kb = kk.astype(BF16)

    chunks = []
    for ch in range(n_chunks):
        rows = slice(ch * C, (ch + 1) * C)
        e = jnp.exp2(_dot(sums, jnp.concatenate([lf_hi[rows], lf_lo[rows]], axis=0)))
        e_b = e[0:C]
        q_in = (qf[rows] * e_b).astype(BF16)
        k_out = (kk[rows] * e[(N_SUMS - 1) * C:N_SUMS * C]).astype(BF16)
        z = [(jnp.where((step & w) != 0, qf[rows], kk[rows]) * e[li * C:(li + 1) * C]).astype(BF16)
             for li, w in enumerate(HG_LEVELS, start=1)]
        scores = []
        for cols in head_cols:
            s = masks_ref[0] * _dot_nt(qb[rows, cols], kb[rows, cols])
            for li in range(1, len(HG_LEVELS) + 1):
                zl = z[li - 1][:, cols]
                s += masks_ref[li] * _dot_nt(zl, zl)
            scores.append(s.astype(BF16))
        chunks.append((rows, e_b[C - 1:C, :], q_in, k_out, scores))

    for rows, decay_end, q_in, k_out, scores in chunks:
        for hd, cols in enumerate(head_cols):
            st = state_ref[hd]
            v_h = vb[rows, cols]
            o = _dot_nt(q_in[:, cols], st.astype(BF16)) + _dot(scores[hd], v_h)
            state_ref[hd] = st * decay_end[:, cols] + _dot_tn(v_h, k_out[:, cols])
            o = _rms(o, nw) * gate[rows, cols]
            o_ref[rows, cols] = o.astype(o_ref.dtype)


def _hgrn(hg, loga, log1m, norm_w, batch, length):
    hg3 = hg.reshape(batch, length, HG_COLS)
    out = pl.pallas_call(
        _hgrn_kernel,
        grid=(batch, length // HG_TILE),
        in_specs=[
            pl.BlockSpec((None, HG_TILE, HG_COLS), lambda b, c: (b, c, 0)),
            pl.BlockSpec((1, HG_WIDTH), lambda b, c: (0, 0)),
            pl.BlockSpec((1, HG_WIDTH), lambda b, c: (0, 0)),
            pl.BlockSpec((1, HG_D), lambda b, c: (0, 0)),
            pl.BlockSpec(_HG_SUMS.shape, lambda b, c: (0, 0)),
            pl.BlockSpec(_HG_MASKS.shape, lambda b, c: (0, 0, 0)),
        ],
        out_specs=pl.BlockSpec((None, HG_TILE, HG_WIDTH), lambda b, c: (b, c, 0)),
        out_shape=jax.ShapeDtypeStruct((batch, length, HG_WIDTH), BF16),
        scratch_shapes=[pltpu.VMEM((HG_HEADS, HG_D, HG_D), F32)],
        compiler_params=_params("parallel", "arbitrary"),
        name="hgrn2",
    )(hg3, loga, log1m, norm_w, jnp.asarray(_HG_SUMS, BF16), jnp.asarray(_HG_MASKS, F32))
    return out.reshape(batch * length, HG_WIDTH)


def _attn_kernel(q_ref, k_ref, v_ref, toe_ref, cst_ref, w_ref, o_ref, s_ref, *, n_blocks):
    lam = cst_ref[0:1, 0:1]
    post = cst_ref[1:2, :]
    lane = lax.broadcasted_iota(jnp.int32, (Q_BLOCK, Q_BLOCK), 1)
    first_half = lane < DA_DQK
    key_ok0 = lane >= (LEAD - N_META)
    inert_bias = jnp.where(key_ok0, 0.0, NEG)

    def near_bias(kind, kb):
        bias = toe_ref[kind]
        if kb == 0:
            bias = jnp.where(key_ok0, bias, NEG)
        return bias

    def slabs(x):
        return [x[:, c:c + Q_BLOCK] for c in range(0, x.shape[1], Q_BLOCK)]

    def both_maps(bias):
        return jnp.concatenate([bias, bias], axis=0)

    def stacked_q(i):
        qi = q_ref[i * Q_BLOCK:(i + 1) * Q_BLOCK, :]
        zero = jnp.zeros_like(qi)
        return jnp.concatenate([jnp.where(first_half, qi, zero), jnp.where(first_half, zero, qi)],
                               axis=0)

    def key_tiles(i):
        tiles = []
        far_end = max(i - 1, 0)
        if far_end >= 1:
            tiles.append((0, Q_BLOCK, both_maps(inert_bias)))
        kb = 1
        while kb < far_end:
            width = KEY_TILE if kb + KEY_TILE // Q_BLOCK <= far_end else Q_BLOCK
            tiles.append((kb * Q_BLOCK, width, None))
            kb += width // Q_BLOCK
        if i >= 1:
            tiles.append(((i - 1) * Q_BLOCK, Q_BLOCK, both_maps(near_bias(1, i - 1))))
        tiles.append((i * Q_BLOCK, Q_BLOCK, both_maps(near_bias(0, i))))
        return tiles

    def scores(q2, tile):
        start, width, bias = tile
        s = _dot_nt(q2, k_ref[start:start + width, :])
        return s if bias is None else s + bias

    def sweep_scores(i):
        q2 = stacked_q(i)
        m_acc = None
        for tile in key_tiles(i):
            s = scores(q2, tile)
            s_ref[i % 2, :, tile[0]:tile[0] + tile[1]] = s
            for slab in slabs(s):
                m_acc = slab if m_acc is None else jnp.maximum(m_acc, slab)
        return m_acc.max(axis=-1, keepdims=True)

    row_max = sweep_scores(0)
    for i in range(n_blocks):
        m = row_max
        if i + 1 < n_blocks:
            row_max = sweep_scores(i + 1)
        l_acc = o_acc = None
        for start, width, _ in key_tiles(i):
            p = jnp.exp2(s_ref[i % 2, :, start:start + width] - m)
            for slab in slabs(p):
                l_acc = slab if l_acc is None else l_acc + slab
            part = _dot(p.astype(BF16), v_ref[start:start + width, :])
            o_acc = part if o_acc is None else o_acc + part
        o2 = o_acc * (1.0 / l_acc.sum(axis=-1, keepdims=True))
        o = o2[:Q_BLOCK] - lam * o2[Q_BLOCK:]
        o = _rms(o, w_ref[...]) * post
        o_ref[i * Q_BLOCK:(i + 1) * Q_BLOCK, :] = o.astype(o_ref.dtype)


def _attn(da, toe, cst, subln_w, batch, length):
    da3 = da.reshape(batch, length, DA_COLS)
    hw = DA_HEADS
    out = pl.pallas_call(
        functools.partial(_attn_kernel, n_blocks=length // Q_BLOCK),
        grid=(batch, DA_HEADS),
        in_specs=[
            pl.BlockSpec((None, length, DA_DV), lambda b, h: (b, 0, h)),
            pl.BlockSpec((None, length, DA_DV), lambda b, h: (b, 0, hw + h)),
            pl.BlockSpec((None, length, DA_DV), lambda b, h: (b, 0, 2 * hw + h)),
            pl.BlockSpec((None, 2, Q_BLOCK, Q_BLOCK), lambda b, h: (h, 0, 0, 0)),
            pl.BlockSpec((8, LANES), lambda b, h: (0, 0)),
            pl.BlockSpec((1, DA_DV), lambda b, h: (0, 0)),
        ],
        out_specs=pl.BlockSpec((None, length, DA_DV), lambda b, h: (b, 0, h)),
        out_shape=jax.ShapeDtypeStruct((batch, length, DA_HEADS * DA_DV), BF16),
        scratch_shapes=[pltpu.VMEM((2, 2 * Q_BLOCK, length), F32)],
        compiler_params=_params("parallel", "parallel"),
        name="diff_attn",
    )(da3, da3, da3, toe, cst, subln_w)
    return out.reshape(batch * length, DA_HEADS * DA_DV)


def _t5_bucket(dist):
    n = jnp.maximum(dist, 0)
    max_exact = REL_BUCKETS // 2
    nf = jnp.maximum(n, max_exact).astype(F32)
    large = max_exact + (jnp.log(nf / max_exact) / math.log(REL_MAX_DIST / max_exact)
                         * (REL_BUCKETS - max_exact)).astype(jnp.int32)
    large = jnp.minimum(large, REL_BUCKETS - 1)
    return jnp.where(n < max_exact, n, large)


def _attn_bias_tables(rel_bias):
    tab = rel_bias.astype(F32)
    qi = jnp.arange(Q_BLOCK, dtype=jnp.int32)[:, None]
    ki = jnp.arange(Q_BLOCK, dtype=jnp.int32)[None, :]

    def lookup(bucket):
        onehot = bucket[None, :, :, None] == jnp.arange(REL_BUCKETS, dtype=jnp.int32)
        return jnp.sum(jnp.where(onehot, tab.T[:, None, None, :], 0.0), axis=-1)

    far = tab[REL_BUCKETS - 1][:, None, None]
    diag = jnp.where((ki <= qi)[None], (lookup(_t5_bucket(qi - ki)) - far) * LOG2E, NEG)
    prev = (lookup(_t5_bucket(qi - ki + Q_BLOCK)) - far) * LOG2E
    return jnp.stack([diag, prev], axis=1)


def _out_proj_router_kernel(h_ref, ohg_ref, oda_ref, wo_ref, g_ref, router_ref, hn_ref, u_ref, lg_ref):
    hn = (h_ref[...] + _dot(ohg_ref[...], wo_ref[:HG_WIDTH, :])
          + _dot(oda_ref[...], wo_ref[HG_WIDTH:, :]))
    hn_ref[...] = hn
    u = _rms(hn, g_ref[...])
    _store_token_major(u_ref, u)
    u_hi = u.astype(BF16)
    u_lo = (u - u_hi.astype(F32)).astype(BF16)
    lg_ref[...] = _dot(jnp.concatenate([u_hi, u_lo, u_hi], axis=1), router_ref[...])


def _out_proj_router(h, o_hg, o_da, wo, layer, gain, router):
    n = h.shape[0]
    row = lambda i: (i, 0)
    full = lambda i: (0, 0)
    return pl.pallas_call(
        _out_proj_router_kernel,
        grid=(n // ROW_TILE,),
        in_specs=[
            pl.BlockSpec((ROW_TILE, D_MODEL), row),
            pl.BlockSpec((ROW_TILE, HG_WIDTH), row),
            pl.BlockSpec((ROW_TILE, DA_HEADS * DA_DV), row),
            pl.BlockSpec((None,) + wo.shape[1:], lambda i: (layer, 0, 0)),
            pl.BlockSpec((1, D_MODEL), full),
            pl.BlockSpec(router.shape, full),
        ],
        out_specs=[
            pl.BlockSpec((ROW_TILE, D_MODEL), row),
            pl.BlockSpec((ROW_TILE * TOKEN_ROWS, LANES), row),
            pl.BlockSpec((ROW_TILE, LANES), row),
        ],
        out_shape=[
            jax.ShapeDtypeStruct((n, D_MODEL), F32),
            jax.ShapeDtypeStruct((n * TOKEN_ROWS, LANES), F32),
            jax.ShapeDtypeStruct((n, LANES), F32),
        ],
        compiler_params=_params("parallel"),
        name="out_proj_router",
    )(h, o_hg, o_da, wo, gain, router)


CAST_STEPS = 64


def _dense_layer_kernel(h_ref, ohg_ref, oda_ref, wo_ref, g_ref, w1_ref, w3_ref, w2_ref, *rest):
    n_cast = (len(rest) - 1) // 2
    o_ref = rest[n_cast]
    hn = (h_ref[...] + _dot(ohg_ref[...], wo_ref[:HG_WIDTH, :])
          + _dot(oda_ref[...], wo_ref[HG_WIDTH:, :]))
    u = _rms(hn, g_ref[...]).astype(BF16)
    a = _dot(u, w1_ref[...])
    act = (_silu(a) * _dot(u, w3_ref[...])).astype(BF16)
    o_ref[...] = hn + _dot(act, w2_ref[...])
    for src_ref, dst_ref in zip(rest[:n_cast], rest[n_cast + 1:]):
        dst_ref[...] = src_ref[...].astype(BF16)


def _dense_layer(h, o_hg, o_da, wo, layer, gain, w1, w3, w2, idx, to_cast, cast_idx):
    n = h.shape[0]
    steps = n // ROW_TILE
    assert steps >= CAST_STEPS
    row = lambda i: (i, 0)
    slab = lambda i: (jnp.minimum(i, CAST_STEPS - 1), 0)

    def resident(w, index):
        return pl.BlockSpec((None,) + w.shape[1:], lambda i: (index, 0, 0),
                            pipeline_mode=pl.Buffered(1))

    cast_in, cast_in_specs, cast_out_specs, cast_out_shape = [], [], [], []
    for w in to_cast:
        n_moe, n_exp, rows, cols = w.shape
        assert (n_exp * rows) % CAST_STEPS == 0
        slab_rows = n_exp * rows // CAST_STEPS
        cast_in.append(w.reshape(n_moe, n_exp * rows, cols))
        cast_in_specs.append(pl.BlockSpec((None, slab_rows, cols),
                                          lambda i: (cast_idx, jnp.minimum(i, CAST_STEPS - 1), 0)))
        cast_out_specs.append(pl.BlockSpec((slab_rows, cols), slab))
        cast_out_shape.append(jax.ShapeDtypeStruct((n_exp * rows, cols), BF16))

    out = pl.pallas_call(
        _dense_layer_kernel,
        grid=(steps,),
        in_specs=[
            pl.BlockSpec((ROW_TILE, D_MODEL), row),
            pl.BlockSpec((ROW_TILE, HG_WIDTH), row),
            pl.BlockSpec((ROW_TILE, DA_HEADS * DA_DV), row),
            resident(wo, layer),
            pl.BlockSpec((1, D_MODEL), lambda i: (0, 0)),
            resident(w1, idx),
            resident(w3, idx),
            resident(w2, idx),
        ] + cast_in_specs,
        out_specs=[pl.BlockSpec((ROW_TILE, D_MODEL), row)] + cast_out_specs,
        out_shape=[jax.ShapeDtypeStruct((n, D_MODEL), F32)] + cast_out_shape,
        compiler_params=_params("arbitrary"),
        name="dense_layer",
    )(h, o_hg, o_da, wo, gain, w1, w3, w2, *cast_in)
    return out[0], [b.reshape(w.shape[1:]) for b, w in zip(out[1:], to_cast)]


DISPATCH_TILE = 1024
ISSUE_TOKENS = 8


def _token_copy(src_ref, src_tok, dst_ref, dst_tok, sem):
    src = pl.multiple_of(src_tok * TOKEN_ROWS, TOKEN_ROWS)
    dst = pl.multiple_of(dst_tok * TOKEN_ROWS, TOKEN_ROWS)
    return pltpu.make_async_copy(src_ref.at[pl.ds(src, TOKEN_ROWS), :],
                                 dst_ref.at[pl.ds(dst, TOKEN_ROWS), :], sem)


def _dispatch_kernel(dest_ref, empty_ref, u_ref, xs_ref, zero_ref, sem):
    step = pl.program_id(0)
    base = step * DISPATCH_TILE
    tile_rows = DISPATCH_TILE * TOKEN_ROWS

    def retire_tile():
        pltpu.make_async_copy(u_ref, xs_ref.at[pl.ds(0, tile_rows), :], sem).wait()

    @pl.when(step == 0)
    def _():
        zero_ref[...] = jnp.zeros_like(zero_ref)

        def clear(g, carry):
            slots = [empty_ref[g * ISSUE_TOKENS * TOP_K + j] for j in range(ISSUE_TOKENS * TOP_K)]
            for j, slot in enumerate(slots):
                _token_copy(zero_ref, 0, xs_ref, slot, sem).start(priority=j % 2)
            return carry

        n_empty = empty_ref.shape[0]
        lax.fori_loop(0, n_empty // (ISSUE_TOKENS * TOP_K), clear, 0)
        for _ in range(n_empty // DISPATCH_TILE):
            retire_tile()

    def start(g, carry):
        r0 = g * ISSUE_TOKENS
        slots = [dest_ref[(base + r0) * TOP_K + j] for j in range(ISSUE_TOKENS * TOP_K)]
        for j, slot in enumerate(slots):
            _token_copy(u_ref, r0 + j // TOP_K, xs_ref, slot, sem).start(priority=j % 2)
        return carry

    lax.fori_loop(0, DISPATCH_TILE // ISSUE_TOKENS, start, 0)
    for _ in range(TOP_K):
        retire_tile()


def _dispatch(dest, empty_slots, u_tm, n_slots):
    n = dest.shape[0] // TOP_K
    assert empty_slots.shape[0] % DISPATCH_TILE == 0
    return pl.pallas_call(
        _dispatch_kernel,
        grid_spec=pltpu.PrefetchScalarGridSpec(
            num_scalar_prefetch=2,
            grid=(n // DISPATCH_TILE,),
            in_specs=[pl.BlockSpec((DISPATCH_TILE * TOKEN_ROWS, LANES), lambda i, d, e: (i, 0))],
            out_specs=pl.BlockSpec(memory_space=pl.ANY),
            scratch_shapes=[pltpu.VMEM((TOKEN_ROWS, LANES), u_tm.dtype),
                            pltpu.SemaphoreType.DMA(())],
        ),
        out_shape=jax.ShapeDtypeStruct((n_slots * TOKEN_ROWS, LANES), u_tm.dtype),
        compiler_params=_params("arbitrary"),
        name="moe_dispatch",
    )(dest, empty_slots, u_tm)


def _expert_kernel(be_ref, nused_ref, xs_ref, w1_ref, w3_ref, w2_ref, y_ref):
    del be_ref
    blk = pl.program_id(0)

    @pl.when(blk < nused_ref[0])
    def _():
        x = _load_token_major(xs_ref, MOE_TILE).astype(BF16)
        acc = jnp.zeros((MOE_TILE, D_MODEL), F32)
        for c0 in range(0, D_FF_EXPERT, FF_CHUNK):
            a = _dot(x, w1_ref[:, c0:c0 + FF_CHUNK])
            act = (_silu(a) * _dot(x, w3_ref[:, c0:c0 + FF_CHUNK])).astype(BF16)
            acc = acc + _dot(act, w2_ref[c0:c0 + FF_CHUNK, :])
        _store_token_major(y_ref, acc)

    @pl.when(blk >= nused_ref[0])
    def _():
        y_ref[...] = jnp.zeros_like(y_ref)


def _experts(block_expert, n_used, xs, w1, w3, w2):
    n_blocks = xs.shape[0] // (MOE_TILE * TOKEN_ROWS)
    wmap = lambda i, be, nu: (be[i], 0, 0)
    slots = pl.BlockSpec((MOE_TILE * TOKEN_ROWS, LANES), lambda i, be, nu: (i, 0))
    return pl.pallas_call(
        _expert_kernel,
        grid_spec=pltpu.PrefetchScalarGridSpec(
            num_scalar_prefetch=2,
            grid=(n_blocks,),
            in_specs=[
                slots,
                pl.BlockSpec((None, D_MODEL, D_FF_EXPERT), wmap),
                pl.BlockSpec((None, D_MODEL, D_FF_EXPERT), wmap),
                pl.BlockSpec((None, D_FF_EXPERT, D_MODEL), wmap),
            ],
            out_specs=slots,
        ),
        out_shape=jax.ShapeDtypeStruct(xs.shape, F32),
        compiler_params=_params("arbitrary"),
        name="moe_experts",
    )(block_expert, n_used, xs, w1, w3, w2)


def _combine_kernel(dest_ref, h_ref, gate_ref, y_ref, o_ref, buf_ref, sem):
    step = pl.program_id(0)

    def issue(s, slot):
        base = s * ROW_TILE

        def body(g, carry):
            r0 = g * ISSUE_TOKENS
            slots = [dest_ref[(base + r0) * TOP_K + j] for j in range(ISSUE_TOKENS * TOP_K)]
            for j, src in enumerate(slots):
                _token_copy(y_ref, src, buf_ref.at[slot, j % TOP_K], r0 + j // TOP_K,
                            sem.at[slot]).start(priority=j % 2)
            return carry

        lax.fori_loop(0, ROW_TILE // ISSUE_TOKENS, body, 0)

    @pl.when(step == 0)
    def _():
        issue(0, 0)

    for slot in range(2):
        @pl.when(step % 2 == slot)
        def _():
            @pl.when(step + 1 < pl.num_programs(0))
            def _():
                issue(step + 1, 1 - slot)

            for k in range(TOP_K):
                pltpu.make_async_copy(y_ref.at[pl.ds(0, ROW_TILE * TOKEN_ROWS), :],
                                      buf_ref.at[slot, k], sem.at[slot]).wait()
            gate = gate_ref[...]
            o_ref[...] = (h_ref[...]
                          + gate[:, 0:1] * _load_token_major(buf_ref.at[slot, 0], ROW_TILE)
                          + gate[:, 1:2] * _load_token_major(buf_ref.at[slot, 1], ROW_TILE))


def _combine(dest, h, gate, y):
    n = h.shape[0]
    return pl.pallas_call(
        _combine_kernel,
        grid_spec=pltpu.PrefetchScalarGridSpec(
            num_scalar_prefetch=1,
            grid=(n // ROW_TILE,),
            in_specs=[
                pl.BlockSpec((ROW_TILE, D_MODEL), lambda i, d: (i, 0)),
                pl.BlockSpec((ROW_TILE, TOP_K), lambda i, d: (i, 0)),
                pl.BlockSpec(memory_space=pl.ANY),
            ],
            out_specs=pl.BlockSpec((ROW_TILE, D_MODEL), lambda i, d: (i, 0)),
            scratch_shapes=[pltpu.VMEM((2, TOP_K, ROW_TILE * TOKEN_ROWS, LANES), F32),
                            pltpu.SemaphoreType.DMA((2,))],
        ),
        out_shape=jax.ShapeDtypeStruct((n, D_MODEL), F32),
        compiler_params=_params("arbitrary"),
        name="moe_combine",
    )(dest, h, gate, y)


def _route(logits, n_slots):
    n = logits.shape[0]
    eid = lax.broadcasted_iota(jnp.int32, logits.shape, 1)
    i1 = jnp.argmax(logits, axis=-1).astype(jnp.int32)
    l1 = jnp.max(logits, axis=-1)
    rest = jnp.where(eid == i1[:, None], -jnp.inf, logits)
    i2 = jnp.argmax(rest, axis=-1).astype(jnp.int32)
    l2 = jnp.max(rest, axis=-1)
    e2 = jnp.exp(l2 - l1)
    gate = jnp.stack([1.0 / (1.0 + e2), e2 / (1.0 + e2)], axis=-1)

    flat_e = jnp.stack([i1, i2], axis=-1).reshape(n * TOP_K)
    onehot = (flat_e[:, None] == jnp.arange(N_EXPERTS, dtype=jnp.int32)[None, :]).astype(jnp.int32)
    csum = jnp.cumsum(onehot, axis=0)
    rank = jnp.sum(csum * onehot, axis=-1) - 1
    counts = csum[-1]
    padded = (counts + MOE_TILE - 1) // MOE_TILE * MOE_TILE
    pad_end = jnp.cumsum(padded)
    pad_start = pad_end - padded
    dest = (pad_start[flat_e] + rank).astype(jnp.int32)
    n_blocks = n_slots // MOE_TILE
    block_expert = jnp.minimum(
        jnp.searchsorted(pad_end, jnp.arange(n_blocks, dtype=jnp.int32) * MOE_TILE, side="right"),
        N_EXPERTS - 1).astype(jnp.int32)
    n_used = (pad_end[-1:] // MOE_TILE).astype(jnp.int32)
    n_empty = n_slots - n * TOP_K
    gap_end = jnp.cumsum(padded - counts)
    j = jnp.arange(n_empty, dtype=jnp.int32)
    owner = jnp.sum((j[:, None] >= gap_end[None, :]).astype(jnp.int32), axis=-1)
    first_empty = jnp.concatenate([pad_start + counts, pad_end[-1:]])
    gap_start = jnp.concatenate([jnp.zeros((1,), gap_end.dtype), gap_end])
    pick = owner[:, None] == jnp.arange(N_EXPERTS + 1, dtype=jnp.int32)[None, :]
    empty_slots = (j + jnp.sum(jnp.where(pick, (first_empty - gap_start)[None, :], 0), axis=-1))
    return dest, gate, block_expert, n_used, empty_slots.astype(jnp.int32)


def _moe_ffn(h, u, logits, w1, w3, w2):
    n = h.shape[0]
    n_slots = (n * TOP_K // MOE_TILE + N_EXPERTS) * MOE_TILE
    dest, gate, block_expert, n_used, empty_slots = _route(logits, n_slots)
    xs = _dispatch(dest, empty_slots, u, n_slots)
    y = _experts(block_expert, n_used, xs, w1, w3, w2)
    return _combine(dest, h, gate, y)


def _final_kernel(h_ref, g_ref, o_ref):
    o_ref[...] = _rms(h_ref[...], g_ref[...])


def _final_norm(h3, gain, seq):
    batch = h3.shape[0]
    lead_blocks = LEAD // Q_BLOCK
    return pl.pallas_call(
        _final_kernel,
        grid=(seq // Q_BLOCK,),
        in_specs=[
            pl.BlockSpec((batch, Q_BLOCK, D_MODEL), lambda j: (0, j + lead_blocks, 0)),
            pl.BlockSpec((1, D_MODEL), lambda j: (0, 0)),
        ],
        out_specs=pl.BlockSpec((batch, Q_BLOCK, D_MODEL), lambda j: (0, j, 0)),
        out_shape=jax.ShapeDtypeStruct((batch, seq, D_MODEL), F32),
        compiler_params=_params("parallel"),
        name="final_norm",
    )(h3, gain)


def kernel(x, meta, rel_bias, norm_mix, w_in, hg_lb_logits, hg_norm_w, da_lambda, da_subln_w, w_out, norm_ffn, dense_w1, dense_w3, dense_w2, moe_router, moe_w1, moe_w3, moe_w2, final_norm):
    batch, seq, d = x.shape
    length = LEAD + seq
    h = jnp.concatenate([
        jnp.zeros((batch, LEAD - N_META, d), x.dtype),
        jnp.broadcast_to(meta[None].astype(x.dtype), (batch, N_META, d)),
        x], axis=1).reshape(batch * length, d)

    toe = _attn_bias_tables(rel_bias)
    lb_cum = jnp.cumsum(jax.nn.softmax(hg_lb_logits.astype(F32), axis=0), axis=0)
    lb_all = jnp.clip(lb_cum - lb_cum[0:1], 0.0, LB_MAX)
    log_lb = jnp.log(lb_all)
    log_1m_lb = jnp.log1p(-lb_all)

    w_in_b, w_out_b = w_in.astype(BF16), w_out.astype(BF16)
    dense_b = [w.astype(BF16) for w in (dense_w1, dense_w3, dense_w2)]
    assert DEPTH % 2 == 0

    for l in range(DEPTH):
        hg, da = _mix_in(h, norm_mix[l][None], w_in_b, l)
        o_hg = _hgrn(hg, log_lb[l][None], log_1m_lb[l][None], hg_norm_w[l][None], batch, length)
        lam_init = 0.8 - 0.6 * math.exp(-0.3 * l)
        lv = da_lambda[l].astype(F32)
        lam = jnp.exp(jnp.sum(lv[0] * lv[1])) - jnp.exp(jnp.sum(lv[2] * lv[3])) + lam_init
        cst = jnp.zeros((8, LANES), F32).at[0].set(lam).at[1].set(1.0 - lam_init)
        o_da = _attn(da, toe, cst, da_subln_w[l][None], batch, length)
        i = l // 2
        if l % 2 == 0:
            h, moe_b = _dense_layer(h, o_hg, o_da, w_out_b, l, norm_ffn[l][None], *dense_b, i,
                                    (moe_w1, moe_w3, moe_w2), i)
        else:
            router = jnp.zeros((d, LANES), F32).at[:, :N_EXPERTS].set(moe_router[i].astype(F32))
            r_hi = router.astype(BF16)
            r_lo = (router - r_hi.astype(F32)).astype(BF16)
            router = jnp.concatenate([r_hi, r_hi, r_lo], axis=0)
            hn, u, lg = _out_proj_router(h, o_hg, o_da, w_out_b, l, norm_ffn[l][None], router)
            h = _moe_ffn(hn, u, lg[:, :N_EXPERTS], *moe_b)
    return _final_norm(h.reshape(batch, length, d), final_norm[None], seq)
```

```python
import functools
import math

import jax
import jax.numpy as jnp
import numpy as np
from jax import lax
from jax.experimental import pallas as pl
from jax.experimental.pallas import tpu as pltpu

D_MODEL = 1024
DEPTH = 4
N_META = 16
LEAD = 128
HG_WIDTH = 512
HG_HEADS = 4
HG_D = 128
HG_CHUNK = 64
DA_HEADS = 4
DA_DQK = 64
DA_DV = 128
Q_BLOCK = 128
KEY_TILE = 2 * Q_BLOCK
Q_GROUP = 2
LOG2E = math.log2(math.e)
Q_SCALE = DA_DQK ** -0.5 * LOG2E
REL_BUCKETS = 32
REL_MAX_DIST = 128
N_EXPERTS = 8
TOP_K = 2
D_FF_EXPERT = 3584
EPS = 1e-6
NEG = -1e30
LB_MAX = 0.999
HG_COLS = 4 * HG_WIDTH
DA_COLS = 3 * DA_HEADS * DA_DV
W_IN_COLS = HG_COLS + DA_COLS

LANES = 128
VMEM_LIMIT = 56 * 1024 * 1024

ROW_TILE = 256
HG_TILE = 128
MOE_TILE = 256
FF_CHUNK = 512

F32 = jnp.float32
BF16 = jnp.bfloat16


def _params(*sem):
    return pltpu.CompilerParams(dimension_semantics=sem, vmem_limit_bytes=VMEM_LIMIT)


def _dot(a, b):
    return jnp.dot(a, b, preferred_element_type=F32)


def _dot_nt(a, b):
    return lax.dot_general(a, b, (((1,), (1,)), ((), ())), preferred_element_type=F32)


def _dot_tn(a, b):
    return lax.dot_general(a, b, (((0,), (0,)), ((), ())), preferred_element_type=F32)


def _rms(x, gain):
    return x * lax.rsqrt(jnp.mean(x * x, axis=-1, keepdims=True) + EPS) * gain


TOKEN_ROWS = D_MODEL // LANES


def _store_token_major(ref, x):
    t = x.shape[0]
    for s in range(TOKEN_ROWS):
        ref[pl.ds(s, t, stride=TOKEN_ROWS), :] = x[:, s * LANES:(s + 1) * LANES]


def _load_token_major(ref, t):
    return jnp.concatenate(
        [ref[pl.ds(s, t, stride=TOKEN_ROWS), :] for s in range(TOKEN_ROWS)], axis=1)


def _silu(x):
    return x * (0.5 * jnp.tanh(0.5 * x) + 0.5)


def _mix_in_kernel(x_ref, g_ref, w_ref, hg_ref, da_ref):
    u = _rms(x_ref[...], g_ref[...]).astype(BF16)
    hg_ref[...] = _dot(u, w_ref[:, :HG_COLS])
    n_q = DA_HEADS * 2 * DA_DQK
    da_ref[:, :n_q] = (_dot(u, w_ref[:, HG_COLS:HG_COLS + n_q]) * Q_SCALE).astype(BF16)
    da_ref[:, n_q:] = _dot(u, w_ref[:, HG_COLS + n_q:]).astype(BF16)


def _mix_in(h, gain, w, layer):
    n = h.shape[0]
    return pl.pallas_call(
        _mix_in_kernel,
        grid=(n // ROW_TILE,),
        in_specs=[
            pl.BlockSpec((ROW_TILE, D_MODEL), lambda i: (i, 0)),
            pl.BlockSpec((1, D_MODEL), lambda i: (0, 0)),
            pl.BlockSpec((None, D_MODEL, W_IN_COLS), lambda i: (layer, 0, 0)),
        ],
        out_specs=[
            pl.BlockSpec((ROW_TILE, HG_COLS), lambda i: (i, 0)),
            pl.BlockSpec((ROW_TILE, DA_COLS), lambda i: (i, 0)),
        ],
        out_shape=[
            jax.ShapeDtypeStruct((n, HG_COLS), F32),
            jax.ShapeDtypeStruct((n, DA_COLS), BF16),
        ],
        compiler_params=_params("parallel"),
        name="mix_in",
    )(h, gain, w)


HG_LEVELS = (32, 16, 8, 4, 2, 1)
N_SUMS = len(HG_LEVELS) + 2


def _hgrn_consts():
    c = HG_CHUNK
    t = np.arange(c)[:, None]
    j = np.arange(c)[None, :]
    sums = np.zeros((N_SUMS, c, c), np.float32)
    masks = np.zeros((len(HG_LEVELS) + 1, c, c), np.float32)
    sums[0] = j <= t
    masks[0] = np.eye(c)
    for li, w in enumerate(HG_LEVELS, start=1):
        ref = (t // (2 * w)) * (2 * w) + w
        sums[li] = np.where(t >= ref, (j > ref) & (j <= t), (j > t) & (j <= ref))
        masks[li] = (t // (2 * w) == j // (2 * w)) & (t % (2 * w) >= w) & (j % (2 * w) < w)
    sums[N_SUMS - 1] = j > t
    sums = sums.reshape(N_SUMS * c, c)
    return np.concatenate([sums, sums], axis=1), masks


_HG_SUMS, _HG_MASKS = _hgrn_consts()


def _hgrn_kernel(hg_ref, loga_ref, log1m_ref, nw_ref, sums_ref, masks_ref, o_ref, state_ref):
    c_idx = pl.program_id(1)

    @pl.when(c_idx == 0)
    def _():
        state_ref[...] = jnp.zeros_like(state_ref)

    C = HG_CHUNK
    W = HG_WIDTH
    n_chunks = HG_TILE // C
    sums = sums_ref[...]
    nw = nw_ref[...]
    row_idx = c_idx * HG_TILE + lax.broadcasted_iota(jnp.int32, (HG_TILE, 1), 0)
    valid = row_idx >= (LEAD - N_META)
    step = lax.broadcasted_iota(jnp.int32, (C, 1), 0)
    head_cols = [slice(hd * HG_D, (hd + 1) * HG_D) for hd in range(HG_HEADS)]

    f = hg_ref[:, W:2 * W]
    qf = _silu(hg_ref[:, 0:W])
    ls = jnp.minimum(f, 0.0) - jnp.log(1.0 + jnp.exp(-jnp.abs(f)))
    cc = log1m_ref[...] + ls
    loga = loga_ref[...]
    lf = jnp.maximum(loga, cc) + jnp.log(1.0 + jnp.exp(-jnp.abs(loga - cc)))
    kk = jnp.exp(cc - f)
    lf = jnp.where(valid, lf, 0.0)
    kk = jnp.where(valid, kk, 0.0)
    lf2 = lf * LOG2E
    lf_hi = lf2.astype(BF16)
    lf_lo = (lf2 - lf_hi.astype(F32)).astype(BF16)
    vb = hg_ref[:, 2 * W:3 * W].astype(BF16)
    gate = _silu(hg_ref[:, 3 * W:4 * W])

    def side_by_side(x):
        return jnp.concatenate([x[ch * C:(ch + 1) * C] for ch in range(n_chunks)], axis=1)

    qf_w, kk_w = side_by_side(qf), side_by_side(kk)
    qb_w, kb_w = qf_w.astype(BF16), kk_w.astype(BF16)
    e = jnp.exp2(_dot(sums, jnp.concatenate([side_by_side(lf_hi), side_by_side(lf_lo)], axis=0)))
    e_b = e[0:C]
    decay_end = e_b[C - 1:C, :]
    q_in = (qf_w * e_b).astype(BF16)
    k_out = (kk_w * e[(N_SUMS - 1) * C:N_SUMS * C]).astype(BF16)
    z = [(jnp.where((step & w) != 0, qf_w, kk_w) * e[li * C:(li + 1) * C]).astype(BF16)
         for li, w in enumerate(HG_LEVELS, start=1)]
    unit_cols = [[slice(ch * W + hd * HG_D, ch * W + (hd + 1) * HG_D) for hd in range(HG_HEADS)]
                 for ch in range(n_chunks)]
    scores = []
    for ch in range(n_chunks):
        scores.append([])
        for cols in unit_cols[ch]:
            s = masks_ref[0] * _dot_nt(qb_w[:, cols], kb_w[:, cols])
            for li in range(1, len(HG_LEVELS) + 1):
                zl = z[li - 1][:, cols]
                s += masks_ref[li] * _dot_nt(zl, zl)
            scores[ch].append(s.astype(BF16))

    for ch in range(n_chunks):
        rows = slice(ch * C, (ch + 1) * C)
        for hd, cols in enumerate(head_cols):
            wide = unit_cols[ch][hd]
            st = state_ref[hd]
            v_h = vb[rows, cols]
            o = _dot_nt(q_in[:, wide], st.astype(BF16)) + _dot(scores[ch][hd], v_h)
            state_ref[hd] = st * decay_end[:, wide] + _dot_tn(v_h, k_out[:, wide])
            o = _rms(o, nw) * gate[rows, cols]
            o_ref[rows, cols] = o.astype(o_ref.dtype)


def _hgrn(hg, loga, log1m, norm_w, batch, length):
    hg3 = hg.reshape(batch, length, HG_COLS)
    out = pl.pallas_call(
        _hgrn_kernel,
        grid=(batch, length // HG_TILE),
        in_specs=[
            pl.BlockSpec((None, HG_TILE, HG_COLS), lambda b, c: (b, c, 0)),
            pl.BlockSpec((1, HG_WIDTH), lambda b, c: (0, 0)),
            pl.BlockSpec((1, HG_WIDTH), lambda b, c: (0, 0)),
            pl.BlockSpec((1, HG_D), lambda b, c: (0, 0)),
            pl.BlockSpec(_HG_SUMS.shape, lambda b, c: (0, 0)),
            pl.BlockSpec(_HG_MASKS.shape, lambda b, c: (0, 0, 0)),
        ],
        out_specs=pl.BlockSpec((None, HG_TILE, HG_WIDTH), lambda b, c: (b, c, 0)),
        out_shape=jax.ShapeDtypeStruct((batch, length, HG_WIDTH), BF16),
        scratch_shapes=[pltpu.VMEM((HG_HEADS, HG_D, HG_D), F32)],
        compiler_params=_params("parallel", "arbitrary"),
        name="hgrn2",
    )(hg3, loga, log1m, norm_w, jnp.asarray(_HG_SUMS, BF16), jnp.asarray(_HG_MASKS, F32))
    return out.reshape(batch * length, HG_WIDTH)


def _attn_kernel(q_ref, k_ref, v_ref, toe_ref, cst_ref, w_ref, o_ref, s_ref, *, n_blocks):
    lam = cst_ref[0:1, 0:1]
    post = cst_ref[1:2, :]
    lane = lax.broadcasted_iota(jnp.int32, (Q_BLOCK, Q_BLOCK), 1)
    first_half = lane < DA_DQK
    key_ok0 = lane >= (LEAD - N_META)
    inert_bias = jnp.where(key_ok0, 0.0, NEG)

    def near_bias(kind, kb):
        bias = toe_ref[kind]
        if kb == 0:
            bias = jnp.where(key_ok0, bias, NEG)
        return bias

    def slabs(x):
        return [x[:, c:c + Q_BLOCK] for c in range(0, x.shape[1], Q_BLOCK)]

    map_rows = 2 * Q_BLOCK
    groups = [tuple(range(i, min(i + Q_GROUP, n_blocks))) for i in range(0, n_blocks, Q_GROUP)]

    def both_maps(bias):
        return jnp.concatenate([bias, bias], axis=0)

    def stacked_q(blocks):
        parts = []
        for i in blocks:
            qi = q_ref[i * Q_BLOCK:(i + 1) * Q_BLOCK, :]
            zero = jnp.zeros_like(qi)
            parts += [jnp.where(first_half, qi, zero), jnp.where(first_half, zero, qi)]
        return jnp.concatenate(parts, axis=0)

    def group_tiles(blocks):
        first, last = blocks[0], blocks[-1]
        tiles = []
        far_end = max(first - 1, 0)
        if far_end >= 1:
            tiles.append((0, Q_BLOCK, jnp.concatenate([inert_bias] * (2 * len(blocks)), axis=0), 0))
        kb = 1
        while kb < far_end:
            width = KEY_TILE if kb + KEY_TILE // Q_BLOCK <= far_end else Q_BLOCK
            tiles.append((kb * Q_BLOCK, width, None, 0))
            kb += width // Q_BLOCK
        for kb in range(far_end, last + 1):
            biases, first_row = [], None
            for r, qb in enumerate(blocks):
                if kb > qb:
                    continue
                if first_row is None:
                    first_row = r * map_rows
                if kb == qb:
                    bias = near_bias(0, kb)
                elif kb == qb - 1:
                    bias = near_bias(1, kb)
                else:
                    bias = inert_bias if kb == 0 else jnp.zeros_like(inert_bias)
                biases.append(both_maps(bias))
            tiles.append((kb * Q_BLOCK, Q_BLOCK, jnp.concatenate(biases, axis=0), first_row))
        return tiles

    def merge(acc, x, first_row, op):
        if acc is None:
            return x
        if first_row == 0:
            return op(acc, x)
        return jnp.concatenate([acc[:first_row], op(acc[first_row:], x)], axis=0)

    def sweep_scores(g):
        blocks = groups[g]
        rows = len(blocks) * map_rows
        q2 = stacked_q(blocks)
        m_acc = None
        for start, width, bias, first_row in group_tiles(blocks):
            s = _dot_nt(q2[first_row:], k_ref[start:start + width, :])
            if bias is not None:
                s = s + bias
            s_ref[g % 2, first_row:rows, start:start + width] = s
            for slab in slabs(s):
                m_acc = merge(m_acc, slab, first_row, jnp.maximum)
        return m_acc.max(axis=-1, keepdims=True)

    row_max = sweep_scores(0)
    for g, blocks in enumerate(groups):
        rows = len(blocks) * map_rows
        m = row_max
        if g + 1 < len(groups):
            row_max = sweep_scores(g + 1)
        l_acc = o_acc = None
        for start, width, _, first_row in group_tiles(blocks):
            p = jnp.exp2(s_ref[g % 2, first_row:rows, start:start + width] - m[first_row:])
            for slab in slabs(p):
                l_acc = merge(l_acc, slab, first_row, jnp.add)
            o_acc = merge(o_acc, _dot(p.astype(BF16), v_ref[start:start + width, :]), first_row, jnp.add)
        o2 = o_acc * (1.0 / l_acc.sum(axis=-1, keepdims=True))
        for r, i in enumerate(blocks):
            o = o2[r * map_rows:r * map_rows + Q_BLOCK] - lam * o2[r * map_rows + Q_BLOCK:(r + 1) * map_rows]
            o = _rms(o, w_ref[...]) * post
            o_ref[i * Q_BLOCK:(i + 1) * Q_BLOCK, :] = o.astype(o_ref.dtype)


def _attn(da, toe, cst, subln_w, batch, length):
    da3 = da.reshape(batch, length, DA_COLS)
    hw = DA_HEADS
    out = pl.pallas_call(
        functools.partial(_attn_kernel, n_blocks=length // Q_BLOCK),
        grid=(batch, DA_HEADS),
        in_specs=[
            pl.BlockSpec((None, length, DA_DV), lambda b, h: (b, 0, h)),
            pl.BlockSpec((None, length, DA_DV), lambda b, h: (b, 0, hw + h)),
            pl.BlockSpec((None, length, DA_DV), lambda b, h: (b, 0, 2 * hw + h)),
            pl.BlockSpec((None, 2, Q_BLOCK, Q_BLOCK), lambda b, h: (h, 0, 0, 0)),
            pl.BlockSpec((8, LANES), lambda b, h: (0, 0)),
            pl.BlockSpec((1, DA_DV), lambda b, h: (0, 0)),
        ],
        out_specs=pl.BlockSpec((None, length, DA_DV), lambda b, h: (b, 0, h)),
        out_shape=jax.ShapeDtypeStruct((batch, length, DA_HEADS * DA_DV), BF16),
        scratch_shapes=[pltpu.VMEM((2, Q_GROUP * 2 * Q_BLOCK, length), F32)],
        compiler_params=_params("parallel", "parallel"),
        name="diff_attn",
    )(da3, da3, da3, toe, cst, subln_w)
    return out.reshape(batch * length, DA_HEADS * DA_DV)


def _t5_bucket(dist):
    n = jnp.maximum(dist, 0)
    max_exact = REL_BUCKETS // 2
    nf = jnp.maximum(n, max_exact).astype(F32)
    large = max_exact + (jnp.log(nf / max_exact) / math.log(REL_MAX_DIST / max_exact)
                         * (REL_BUCKETS - max_exact)).astype(jnp.int32)
    large = jnp.minimum(large, REL_BUCKETS - 1)
    return jnp.where(n < max_exact, n, large)


def _attn_bias_tables(rel_bias):
    tab = rel_bias.astype(F32)
    qi = jnp.arange(Q_BLOCK, dtype=jnp.int32)[:, None]
    ki = jnp.arange(Q_BLOCK, dtype=jnp.int32)[None, :]

    def lookup(bucket):
        onehot = bucket[None, :, :, None] == jnp.arange(REL_BUCKETS, dtype=jnp.int32)
        return jnp.sum(jnp.where(onehot, tab.T[:, None, None, :], 0.0), axis=-1)

    far = tab[REL_BUCKETS - 1][:, None, None]
    diag = jnp.where((ki <= qi)[None], (lookup(_t5_bucket(qi - ki)) - far) * LOG2E, NEG)
    prev = (lookup(_t5_bucket(qi - ki + Q_BLOCK)) - far) * LOG2E
    return jnp.stack([diag, prev], axis=1)


def _out_proj_router_kernel(h_ref, ohg_ref, oda_ref, wo_ref, g_ref, router_ref, before_ref,
                            hn_ref, u_ref, route_ref, gate_ref, count_ref, seen_ref):
    @pl.when(pl.program_id(0) == 0)
    def _():
        seen_ref[...] = jnp.zeros_like(seen_ref)

    hn = (h_ref[...] + _dot(ohg_ref[...], wo_ref[:HG_WIDTH, :])
          + _dot(oda_ref[...], wo_ref[HG_WIDTH:, :]))
    hn_ref[...] = hn
    u = _rms(hn, g_ref[...])
    _store_token_major(u_ref, u)
    u_hi = u.astype(BF16)
    u_lo = (u - u_hi.astype(F32)).astype(BF16)
    logits = _dot(jnp.concatenate([u_hi, u_lo, u_hi], axis=1), router_ref[...])
    logits = logits.T[:N_EXPERTS]
    expert = lax.broadcasted_iota(jnp.int32, logits.shape, 0)

    def top(x):
        best = x.max(axis=0, keepdims=True)
        return best, jnp.where(x == best, expert, N_EXPERTS).min(axis=0, keepdims=True)

    l1, e1 = top(logits)
    l2, e2 = top(jnp.where(expert == e1, -jnp.inf, logits))
    w2 = jnp.exp(l2 - l1)
    g1 = 1.0 / (1.0 + w2)
    g2 = w2 / (1.0 + w2)

    pick1 = (expert == e1).astype(F32)
    pick2 = (expert == e2).astype(F32)
    picked = pick1 + pick2
    ahead = seen_ref[:, 0:1] + _dot(picked.astype(BF16), before_ref[...])
    rank1 = jnp.sum(pick1 * ahead, axis=0, keepdims=True).astype(jnp.int32)
    rank2 = jnp.sum(pick2 * ahead, axis=0, keepdims=True).astype(jnp.int32)
    seen = seen_ref[...] + jnp.sum(picked, axis=1, keepdims=True)
    seen_ref[...] = seen

    route_ref[...] = jnp.where(expert == 0, e1, jnp.where(expert == 1, e2, jnp.where(
        expert == 2, rank1, jnp.where(expert == 3, rank2, 0))))
    gate_ref[...] = jnp.where(expert == 0, g1, jnp.where(expert == 1, g2, 0.0))
    count_ref[...] = seen.astype(jnp.int32)


def _out_proj_router(h, o_hg, o_da, wo, layer, gain, router):
    n = h.shape[0]
    row = lambda i: (i, 0)
    full = lambda i: (0, 0)
    assert N_EXPERTS >= 2 * TOP_K
    lanes_of = lambda i: (0, i)
    before = jnp.asarray(np.triu(np.ones((ROW_TILE, ROW_TILE), np.float32), 1), BF16)
    return pl.pallas_call(
        _out_proj_router_kernel,
        grid=(n // ROW_TILE,),
        in_specs=[
            pl.BlockSpec((ROW_TILE, D_MODEL), row),
            pl.BlockSpec((ROW_TILE, HG_WIDTH), row),
            pl.BlockSpec((ROW_TILE, DA_HEADS * DA_DV), row),
            pl.BlockSpec((None,) + wo.shape[1:], lambda i: (layer, 0, 0)),
            pl.BlockSpec((1, D_MODEL), full),
            pl.BlockSpec(router.shape, full),
            pl.BlockSpec(before.shape, full),
        ],
        out_specs=[
            pl.BlockSpec((ROW_TILE, D_MODEL), row),
            pl.BlockSpec((ROW_TILE * TOKEN_ROWS, LANES), row),
            pl.BlockSpec((N_EXPERTS, ROW_TILE), lanes_of),
            pl.BlockSpec((N_EXPERTS, ROW_TILE), lanes_of),
            pl.BlockSpec((N_EXPERTS, LANES), full),
        ],
        out_shape=[
            jax.ShapeDtypeStruct((n, D_MODEL), F32),
            jax.ShapeDtypeStruct((n * TOKEN_ROWS, LANES), F32),
            jax.ShapeDtypeStruct((N_EXPERTS, n), jnp.int32),
            jax.ShapeDtypeStruct((N_EXPERTS, n), F32),
            jax.ShapeDtypeStruct((N_EXPERTS, LANES), jnp.int32),
        ],
        scratch_shapes=[pltpu.VMEM((N_EXPERTS, LANES), F32)],
        compiler_params=_params("arbitrary"),
        name="out_proj_router",
    )(h, o_hg, o_da, wo, gain, router, before)


CAST_STEPS = 64


def _dense_layer_kernel(h_ref, ohg_ref, oda_ref, wo_ref, g_ref, w1_ref, w3_ref, w2_ref, *rest):
    n_cast = (len(rest) - 1) // 2
    o_ref = rest[n_cast]
    hn = (h_ref[...] + _dot(ohg_ref[...], wo_ref[:HG_WIDTH, :])
          + _dot(oda_ref[...], wo_ref[HG_WIDTH:, :]))
    u = _rms(hn, g_ref[...]).astype(BF16)
    a = _dot(u, w1_ref[...])
    act = (_silu(a) * _dot(u, w3_ref[...])).astype(BF16)
    o_ref[...] = hn + _dot(act, w2_ref[...])
    for src_ref, dst_ref in zip(rest[:n_cast], rest[n_cast + 1:]):
        dst_ref[...] = src_ref[...].astype(BF16)


def _dense_layer(h, o_hg, o_da, wo, layer, gain, w1, w3, w2, idx, to_cast, cast_idx):
    n = h.shape[0]
    steps = n // ROW_TILE
    assert steps >= CAST_STEPS
    row = lambda i: (i, 0)
    slab = lambda i: (jnp.minimum(i, CAST_STEPS - 1), 0)

    def resident(w, index):
        return pl.BlockSpec((None,) + w.shape[1:], lambda i: (index, 0, 0),
                            pipeline_mode=pl.Buffered(1))

    cast_in, cast_in_specs, cast_out_specs, cast_out_shape = [], [], [], []
    for w in to_cast:
        n_moe, n_exp, rows, cols = w.shape
        assert (n_exp * rows) % CAST_STEPS == 0
        slab_rows = n_exp * rows // CAST_STEPS
        cast_in.append(w.reshape(n_moe, n_exp * rows, cols))
        cast_in_specs.append(pl.BlockSpec((None, slab_rows, cols),
                                          lambda i: (cast_idx, jnp.minimum(i, CAST_STEPS - 1), 0)))
        cast_out_specs.append(pl.BlockSpec((slab_rows, cols), slab))
        cast_out_shape.append(jax.ShapeDtypeStruct((n_exp * rows, cols), BF16))

    out = pl.pallas_call(
        _dense_layer_kernel,
        grid=(steps,),
        in_specs=[
            pl.BlockSpec((ROW_TILE, D_MODEL), row),
            pl.BlockSpec((ROW_TILE, HG_WIDTH), row),
            pl.BlockSpec((ROW_TILE, DA_HEADS * DA_DV), row),
            resident(wo, layer),
            pl.BlockSpec((1, D_MODEL), lambda i: (0, 0)),
            resident(w1, idx),
            resident(w3, idx),
            resident(w2, idx),
        ] + cast_in_specs,
        out_specs=[pl.BlockSpec((ROW_TILE, D_MODEL), row)] + cast_out_specs,
        out_shape=[jax.ShapeDtypeStruct((n, D_MODEL), F32)] + cast_out_shape,
        compiler_params=_params("arbitrary"),
        name="dense_layer",
    )(h, o_hg, o_da, wo, gain, w1, w3, w2, *cast_in)
    return out[0], [b.reshape(w.shape[1:]) for b, w in zip(out[1:], to_cast)]


DISPATCH_TILE = 1024
ISSUE_TOKENS = 8


def _token_copy(src_ref, src_tok, dst_ref, dst_tok, sem):
    src = pl.multiple_of(src_tok * TOKEN_ROWS, TOKEN_ROWS)
    dst = pl.multiple_of(dst_tok * TOKEN_ROWS, TOKEN_ROWS)
    return pltpu.make_async_copy(src_ref.at[pl.ds(src, TOKEN_ROWS), :],
                                 dst_ref.at[pl.ds(dst, TOKEN_ROWS), :], sem)


def _dispatch_kernel(dest_ref, empty_ref, u_ref, xs_ref, zero_ref, sem):
    step = pl.program_id(0)
    base = step * DISPATCH_TILE
    tile_rows = DISPATCH_TILE * TOKEN_ROWS

    def retire_tile():
        pltpu.make_async_copy(u_ref, xs_ref.at[pl.ds(0, tile_rows), :], sem).wait()

    @pl.when(step == 0)
    def _():
        zero_ref[...] = jnp.zeros_like(zero_ref)

        def clear(g, carry):
            slots = [empty_ref[g * ISSUE_TOKENS * TOP_K + j] for j in range(ISSUE_TOKENS * TOP_K)]
            for j, slot in enumerate(slots):
                _token_copy(zero_ref, 0, xs_ref, slot, sem).start(priority=j % 2)
            return carry

        n_empty = empty_ref.shape[0]
        lax.fori_loop(0, n_empty // (ISSUE_TOKENS * TOP_K), clear, 0)
        for _ in range(n_empty // DISPATCH_TILE):
            retire_tile()

    def start(g, carry):
        r0 = g * ISSUE_TOKENS
        slots = [dest_ref[(base + r0) * TOP_K + j] for j in range(ISSUE_TOKENS * TOP_K)]
        for j, slot in enumerate(slots):
            _token_copy(u_ref, r0 + j // TOP_K, xs_ref, slot, sem).start(priority=j % 2)
        return carry

    lax.fori_loop(0, DISPATCH_TILE // ISSUE_TOKENS, start, 0)
    for _ in range(TOP_K):
        retire_tile()


def _dispatch(dest, empty_slots, u_tm, n_slots):
    n = dest.shape[0] // TOP_K
    assert empty_slots.shape[0] % DISPATCH_TILE == 0
    return pl.pallas_call(
        _dispatch_kernel,
        grid_spec=pltpu.PrefetchScalarGridSpec(
            num_scalar_prefetch=2,
            grid=(n // DISPATCH_TILE,),
            in_specs=[pl.BlockSpec((DISPATCH_TILE * TOKEN_ROWS, LANES), lambda i, d, e: (i, 0))],
            out_specs=pl.BlockSpec(memory_space=pl.ANY),
            scratch_shapes=[pltpu.VMEM((TOKEN_ROWS, LANES), u_tm.dtype),
                            pltpu.SemaphoreType.DMA(())],
        ),
        out_shape=jax.ShapeDtypeStruct((n_slots * TOKEN_ROWS, LANES), u_tm.dtype),
        compiler_params=_params("arbitrary"),
        name="moe_dispatch",
    )(dest, empty_slots, u_tm)


def _expert_kernel(be_ref, nused_ref, xs_ref, w1_ref, w3_ref, w2_ref, y_ref):
    del be_ref
    blk = pl.program_id(0)

    @pl.when(blk < nused_ref[0])
    def _():
        x = _load_token_major(xs_ref, MOE_TILE).astype(BF16)
        acc = jnp.zeros((MOE_TILE, D_MODEL), F32)
        for c0 in range(0, D_FF_EXPERT, FF_CHUNK):
            a = _dot(x, w1_ref[:, c0:c0 + FF_CHUNK])
            act = (_silu(a) * _dot(x, w3_ref[:, c0:c0 + FF_CHUNK])).astype(BF16)
            acc = acc + _dot(act, w2_ref[c0:c0 + FF_CHUNK, :])
        _store_token_major(y_ref, acc)

    @pl.when(blk >= nused_ref[0])
    def _():
        y_ref[...] = jnp.zeros_like(y_ref)


def _experts(block_expert, n_used, xs, w1, w3, w2):
    n_blocks = xs.shape[0] // (MOE_TILE * TOKEN_ROWS)
    wmap = lambda i, be, nu: (be[i], 0, 0)
    slots = pl.BlockSpec((MOE_TILE * TOKEN_ROWS, LANES), lambda i, be, nu: (i, 0))
    return pl.pallas_call(
        _expert_kernel,
        grid_spec=pltpu.PrefetchScalarGridSpec(
            num_scalar_prefetch=2,
            grid=(n_blocks,),
            in_specs=[
                slots,
                pl.BlockSpec((None, D_MODEL, D_FF_EXPERT), wmap),
                pl.BlockSpec((None, D_MODEL, D_FF_EXPERT), wmap),
                pl.BlockSpec((None, D_FF_EXPERT, D_MODEL), wmap),
            ],
            out_specs=slots,
        ),
        out_shape=jax.ShapeDtypeStruct(xs.shape, F32),
        compiler_params=_params("arbitrary"),
        name="moe_experts",
    )(block_expert, n_used, xs, w1, w3, w2)


def _combine_kernel(dest_ref, h_ref, gate_ref, y_ref, o_ref, buf_ref, sem):
    step = pl.program_id(0)

    def issue(s, slot):
        base = s * ROW_TILE

        def body(g, carry):
            r0 = g * ISSUE_TOKENS
            slots = [dest_ref[(base + r0) * TOP_K + j] for j in range(ISSUE_TOKENS * TOP_K)]
            for j, src in enumerate(slots):
                _token_copy(y_ref, src, buf_ref.at[slot, j % TOP_K], r0 + j // TOP_K,
                            sem.at[slot]).start(priority=j % 2)
            return carry

        lax.fori_loop(0, ROW_TILE // ISSUE_TOKENS, body, 0)

    @pl.when(step == 0)
    def _():
        issue(0, 0)

    for slot in range(2):
        @pl.when(step % 2 == slot)
        def _():
            @pl.when(step + 1 < pl.num_programs(0))
            def _():
                issue(step + 1, 1 - slot)

            for k in range(TOP_K):
                pltpu.make_async_copy(y_ref.at[pl.ds(0, ROW_TILE * TOKEN_ROWS), :],
                                      buf_ref.at[slot, k], sem.at[slot]).wait()
            gate = gate_ref[...]
            o_ref[...] = (h_ref[...]
                          + gate[:, 0:1] * _load_token_major(buf_ref.at[slot, 0], ROW_TILE)
                          + gate[:, 1:2] * _load_token_major(buf_ref.at[slot, 1], ROW_TILE))


def _combine(dest, h, gate, y):
    n = h.shape[0]
    return pl.pallas_call(
        _combine_kernel,
        grid_spec=pltpu.PrefetchScalarGridSpec(
            num_scalar_prefetch=1,
            grid=(n // ROW_TILE,),
            in_specs=[
                pl.BlockSpec((ROW_TILE, D_MODEL), lambda i, d: (i, 0)),
                pl.BlockSpec((ROW_TILE, TOP_K), lambda i, d: (i, 0)),
                pl.BlockSpec(memory_space=pl.ANY),
            ],
            out_specs=pl.BlockSpec((ROW_TILE, D_MODEL), lambda i, d: (i, 0)),
            scratch_shapes=[pltpu.VMEM((2, TOP_K, ROW_TILE * TOKEN_ROWS, LANES), F32),
                            pltpu.SemaphoreType.DMA((2,))],
        ),
        out_shape=jax.ShapeDtypeStruct((n, D_MODEL), F32),
        compiler_params=_params("arbitrary"),
        name="moe_combine",
    )(dest, h, gate, y)


def _slot_tables(route, counts, n_slots):
    n = route.shape[1]
    flat_e = route[0:TOP_K].T.reshape(n * TOP_K)
    rank = route[TOP_K:2 * TOP_K].T.reshape(n * TOP_K)
    counts = counts[:, 0]
    padded = (counts + MOE_TILE - 1) // MOE_TILE * MOE_TILE
    pad_end = jnp.cumsum(padded)
    pad_start = pad_end - padded
    onehot = flat_e[:, None] == jnp.arange(N_EXPERTS, dtype=jnp.int32)[None, :]
    dest = (rank + jnp.sum(jnp.where(onehot, pad_start[None, :], 0), axis=-1)).astype(jnp.int32)
    n_blocks = n_slots // MOE_TILE
    block_expert = jnp.minimum(
        jnp.searchsorted(pad_end, jnp.arange(n_blocks, dtype=jnp.int32) * MOE_TILE, side="right"),
        N_EXPERTS - 1).astype(jnp.int32)
    n_used = (pad_end[-1:] // MOE_TILE).astype(jnp.int32)
    n_empty = n_slots - n * TOP_K
    gap_end = jnp.cumsum(padded - counts)
    j = jnp.arange(n_empty, dtype=jnp.int32)
    owner = jnp.sum((j[:, None] >= gap_end[None, :]).astype(jnp.int32), axis=-1)
    first_empty = jnp.concatenate([pad_start + counts, pad_end[-1:]])
    gap_start = jnp.concatenate([jnp.zeros((1,), gap_end.dtype), gap_end])
    pick = owner[:, None] == jnp.arange(N_EXPERTS + 1, dtype=jnp.int32)[None, :]
    empty_slots = (j + jnp.sum(jnp.where(pick, (first_empty - gap_start)[None, :], 0), axis=-1))
    return dest, block_expert, n_used, empty_slots.astype(jnp.int32)


def _moe_ffn(h, u, route, gate, counts, w1, w3, w2):
    n = h.shape[0]
    n_slots = (n * TOP_K // MOE_TILE + N_EXPERTS) * MOE_TILE
    dest, block_expert, n_used, empty_slots = _slot_tables(route, counts, n_slots)
    xs = _dispatch(dest, empty_slots, u, n_slots)
    y = _experts(block_expert, n_used, xs, w1, w3, w2)
    return _combine(dest, h, gate[0:TOP_K].T, y)


def _final_kernel(h_ref, g_ref, o_ref):
    o_ref[...] = _rms(h_ref[...], g_ref[...])


def _final_norm(h3, gain, seq):
    batch = h3.shape[0]
    lead_blocks = LEAD // Q_BLOCK
    return pl.pallas_call(
        _final_kernel,
        grid=(seq // Q_BLOCK,),
        in_specs=[
            pl.BlockSpec((batch, Q_BLOCK, D_MODEL), lambda j: (0, j + lead_blocks, 0)),
            pl.BlockSpec((1, D_MODEL), lambda j: (0, 0)),
        ],
        out_specs=pl.BlockSpec((batch, Q_BLOCK, D_MODEL), lambda j: (0, j, 0)),
        out_shape=jax.ShapeDtypeStruct((batch, seq, D_MODEL), F32),
        compiler_params=_params("parallel"),
        name="final_norm",
    )(h3, gain)


def kernel(x, meta, rel_bias, norm_mix, w_in, hg_lb_logits, hg_norm_w, da_lambda, da_subln_w, w_out, norm_ffn, dense_w1, dense_w3, dense_w2, moe_router, moe_w1, moe_w3, moe_w2, final_norm):
    batch, seq, d = x.shape
    length = LEAD + seq
    h = jnp.concatenate([
        jnp.zeros((batch, LEAD - N_META, d), x.dtype),
        jnp.broadcast_to(meta[None].astype(x.dtype), (batch, N_META, d)),
        x], axis=1).reshape(batch * length, d)

    toe = _attn_bias_tables(rel_bias)
    lb_cum = jnp.cumsum(jax.nn.softmax(hg_lb_logits.astype(F32), axis=0), axis=0)
    lb_all = jnp.clip(lb_cum - lb_cum[0:1], 0.0, LB_MAX)
    log_lb = jnp.log(lb_all)
    log_1m_lb = jnp.log1p(-lb_all)

    w_in_b, w_out_b = w_in.astype(BF16), w_out.astype(BF16)
    dense_b = [w.astype(BF16) for w in (dense_w1, dense_w3, dense_w2)]
    assert DEPTH % 2 == 0

    for l in range(DEPTH):
        hg, da = _mix_in(h, norm_mix[l][None], w_in_b, l)
        o_hg = _hgrn(hg, log_lb[l][None], log_1m_lb[l][None], hg_norm_w[l][None], batch, length)
        lam_init = 0.8 - 0.6 * math.exp(-0.3 * l)
        lv = da_lambda[l].astype(F32)
        lam = jnp.exp(jnp.sum(lv[0] * lv[1])) - jnp.exp(jnp.sum(lv[2] * lv[3])) + lam_init
        cst = jnp.zeros((8, LANES), F32).at[0].set(lam).at[1].set(1.0 - lam_init)
        o_da = _attn(da, toe, cst, da_subln_w[l][None], batch, length)
        i = l // 2
        if l % 2 == 0:
            h, moe_b = _dense_layer(h, o_hg, o_da, w_out_b, l, norm_ffn[l][None], *dense_b, i,
                                    (moe_w1, moe_w3, moe_w2), i)
        else:
            router = jnp.zeros((d, LANES), F32).at[:, :N_EXPERTS].set(moe_router[i].astype(F32))
            r_hi = router.astype(BF16)
            r_lo = (router - r_hi.astype(F32)).astype(BF16)
            router = jnp.concatenate([r_hi, r_hi, r_lo], axis=0)
            hn, u, route, gate, counts = _out_proj_router(h, o_hg, o_da, w_out_b, l,
                                                          norm_ffn[l][None], router)
            h = _moe_ffn(hn, u, route, gate, counts, *moe_b)
    return _final_norm(h.reshape(batch, length, d), final_norm[None], seq)
```

```python
import functools
import math

import jax
import jax.numpy as jnp
import numpy as np
from jax import lax
from jax.experimental import pallas as pl
from jax.experimental.pallas import tpu as pltpu

D_MODEL = 1024
DEPTH = 4
N_META = 16
LEAD = 128
HG_WIDTH = 512
HG_HEADS = 4
HG_D = 128
HG_CHUNK = 64
DA_HEADS = 4
DA_DQK = 64
DA_DV = 128
Q_BLOCK = 128
KEY_TILE = 2 * Q_BLOCK
Q_GROUP = 2
LOG2E = math.log2(math.e)
Q_SCALE = DA_DQK ** -0.5 * LOG2E
REL_BUCKETS = 32
REL_MAX_DIST = 128
N_EXPERTS = 8
TOP_K = 2
D_FF_EXPERT = 3584
EPS = 1e-6
NEG = -1e30
LB_MAX = 0.999
HG_COLS = 4 * HG_WIDTH
DA_COLS = 3 * DA_HEADS * DA_DV
W_IN_COLS = HG_COLS + DA_COLS

LANES = 128
VMEM_LIMIT = 56 * 1024 * 1024

ROW_TILE = 256
MIX_TILE = 512
HG_TILE = 128
MOE_TILE = 256
FF_CHUNK = 512

F32 = jnp.float32
BF16 = jnp.bfloat16


def _params(*sem):
    return pltpu.CompilerParams(dimension_semantics=sem, vmem_limit_bytes=VMEM_LIMIT)


def _dot(a, b):
    return jnp.dot(a, b, preferred_element_type=F32)


def _dot_nt(a, b):
    return lax.dot_general(a, b, (((1,), (1,)), ((), ())), preferred_element_type=F32)


def _dot_tn(a, b):
    return lax.dot_general(a, b, (((0,), (0,)), ((), ())), preferred_element_type=F32)


def _rms(x, gain):
    return x * lax.rsqrt(jnp.mean(x * x, axis=-1, keepdims=True) + EPS) * gain


TOKEN_ROWS = D_MODEL // LANES


def _store_token_major(ref, x):
    t = x.shape[0]
    for s in range(TOKEN_ROWS):
        ref[pl.ds(s, t, stride=TOKEN_ROWS), :] = x[:, s * LANES:(s + 1) * LANES]


def _load_token_major(ref, t):
    return jnp.concatenate(
        [ref[pl.ds(s, t, stride=TOKEN_ROWS), :] for s in range(TOKEN_ROWS)], axis=1)


def _silu(x):
    return x * (0.5 * jnp.tanh(0.5 * x) + 0.5)


def _mix_in_kernel(x_ref, g_ref, w_ref, hg_ref, da_ref):
    u = _rms(x_ref[...], g_ref[...]).astype(BF16)
    hg_ref[...] = _dot(u, w_ref[:, :HG_COLS])
    n_q = DA_HEADS * 2 * DA_DQK
    da_ref[:, :n_q] = (_dot(u, w_ref[:, HG_COLS:HG_COLS + n_q]) * Q_SCALE).astype(BF16)
    da_ref[:, n_q:] = _dot(u, w_ref[:, HG_COLS + n_q:]).astype(BF16)


def _mix_in(h, gain, w, layer):
    n = h.shape[0]
    return pl.pallas_call(
        _mix_in_kernel,
        grid=(n // MIX_TILE,),
        in_specs=[
            pl.BlockSpec((MIX_TILE, D_MODEL), lambda i: (i, 0)),
            pl.BlockSpec((1, D_MODEL), lambda i: (0, 0)),
            pl.BlockSpec((None, D_MODEL, W_IN_COLS), lambda i: (layer, 0, 0),
                         pipeline_mode=pl.Buffered(1)),
        ],
        out_specs=[
            pl.BlockSpec((MIX_TILE, HG_COLS), lambda i: (i, 0)),
            pl.BlockSpec((MIX_TILE, DA_COLS), lambda i: (i, 0)),
        ],
        out_shape=[
            jax.ShapeDtypeStruct((n, HG_COLS), F32),
            jax.ShapeDtypeStruct((n, DA_COLS), BF16),
        ],
        compiler_params=_params("parallel"),
        name="mix_in",
    )(h, gain, w)


HG_LEVELS = (32, 16, 8, 4, 2, 1)
N_SUMS = len(HG_LEVELS) + 2


def _hgrn_consts():
    c = HG_CHUNK
    t = np.arange(c)[:, None]
    j = np.arange(c)[None, :]
    sums = np.zeros((N_SUMS, c, c), np.float32)
    masks = np.zeros((len(HG_LEVELS) + 1, c, c), np.float32)
    sums[0] = j <= t
    masks[0] = np.eye(c)
    for li, w in enumerate(HG_LEVELS, start=1):
        ref = (t // (2 * w)) * (2 * w) + w
        sums[li] = np.where(t >= ref, (j > ref) & (j <= t), (j > t) & (j <= ref))
        masks[li] = (t // (2 * w) == j // (2 * w)) & (t % (2 * w) >= w) & (j % (2 * w) < w)
    sums[N_SUMS - 1] = j > t
    sums = sums.reshape(N_SUMS * c, c)
    return np.concatenate([sums, sums], axis=1), masks


_HG_SUMS, _HG_MASKS = _hgrn_consts()


def _hgrn_kernel(hg_ref, loga_ref, log1m_ref, nw_ref, sums_ref, masks_ref, o_ref, state_ref):
    c_idx = pl.program_id(1)

    @pl.when(c_idx == 0)
    def _():
        state_ref[...] = jnp.zeros_like(state_ref)

    C = HG_CHUNK
    W = HG_WIDTH
    n_chunks = HG_TILE // C
    sums = sums_ref[...]
    nw = nw_ref[...]
    row_idx = c_idx * HG_TILE + lax.broadcasted_iota(jnp.int32, (HG_TILE, 1), 0)
    valid = row_idx >= (LEAD - N_META)
    step = lax.broadcasted_iota(jnp.int32, (C, 1), 0)
    head_cols = [slice(hd * HG_D, (hd + 1) * HG_D) for hd in range(HG_HEADS)]

    f = hg_ref[:, W:2 * W]
    qf = _silu(hg_ref[:, 0:W])
    ls = jnp.minimum(f, 0.0) - jnp.log(1.0 + jnp.exp(-jnp.abs(f)))
    cc = log1m_ref[...] + ls
    loga = loga_ref[...]
    lf = jnp.maximum(loga, cc) + jnp.log(1.0 + jnp.exp(-jnp.abs(loga - cc)))
    kk = jnp.exp(cc - f)
    lf = jnp.where(valid, lf, 0.0)
    kk = jnp.where(valid, kk, 0.0)
    lf2 = lf * LOG2E
    lf_hi = lf2.astype(BF16)
    lf_lo = (lf2 - lf_hi.astype(F32)).astype(BF16)
    vb = hg_ref[:, 2 * W:3 * W].astype(BF16)
    gate = _silu(hg_ref[:, 3 * W:4 * W])

    def side_by_side(x):
        return jnp.concatenate([x[ch * C:(ch + 1) * C] for ch in range(n_chunks)], axis=1)

    qf_w, kk_w = side_by_side(qf), side_by_side(kk)
    qb_w, kb_w = qf_w.astype(BF16), kk_w.astype(BF16)
    e = jnp.exp2(_dot(sums, jnp.concatenate([side_by_side(lf_hi), side_by_side(lf_lo)], axis=0)))
    e_b = e[0:C]
    decay_end = e_b[C - 1:C, :]
    q_in = (qf_w * e_b).astype(BF16)
    k_out = (kk_w * e[(N_SUMS - 1) * C:N_SUMS * C]).astype(BF16)
    z = [(jnp.where((step & w) != 0, qf_w, kk_w) * e[li * C:(li + 1) * C]).astype(BF16)
         for li, w in enumerate(HG_LEVELS, start=1)]
    unit_cols = [[slice(ch * W + hd * HG_D, ch * W + (hd + 1) * HG_D) for hd in range(HG_HEADS)]
                 for ch in range(n_chunks)]
    scores = []
    for ch in range(n_chunks):
        scores.append([])
        for cols in unit_cols[ch]:
            s = masks_ref[0] * _dot_nt(qb_w[:, cols], kb_w[:, cols])
            for li in range(1, len(HG_LEVELS) + 1):
                zl = z[li - 1][:, cols]
                s += masks_ref[li] * _dot_nt(zl, zl)
            scores[ch].append(s.astype(BF16))

    for ch in range(n_chunks):
        rows = slice(ch * C, (ch + 1) * C)
        for hd, cols in enumerate(head_cols):
            wide = unit_cols[ch][hd]
            st = state_ref[hd]
            v_h = vb[rows, cols]
            o = _dot_nt(q_in[:, wide], st.astype(BF16)) + _dot(scores[ch][hd], v_h)
            state_ref[hd] = st * decay_end[:, wide] + _dot_tn(v_h, k_out[:, wide])
            o = _rms(o, nw) * gate[rows, cols]
            o_ref[rows, cols] = o.astype(o_ref.dtype)


def _hgrn(hg, loga, log1m, norm_w, batch, length):
    hg3 = hg.reshape(batch, length, HG_COLS)
    out = pl.pallas_call(
        _hgrn_kernel,
        grid=(batch, length // HG_TILE),
        in_specs=[
            pl.BlockSpec((None, HG_TILE, HG_COLS), lambda b, c: (b, c, 0)),
            pl.BlockSpec((1, HG_WIDTH), lambda b, c: (0, 0)),
            pl.BlockSpec((1, HG_WIDTH), lambda b, c: (0, 0)),
            pl.BlockSpec((1, HG_D), lambda b, c: (0, 0)),
            pl.BlockSpec(_HG_SUMS.shape, lambda b, c: (0, 0)),
            pl.BlockSpec(_HG_MASKS.shape, lambda b, c: (0, 0, 0)),
        ],
        out_specs=pl.BlockSpec((None, HG_TILE, HG_WIDTH), lambda b, c: (b, c, 0)),
        out_shape=jax.ShapeDtypeStruct((batch, length, HG_WIDTH), BF16),
        scratch_shapes=[pltpu.VMEM((HG_HEADS, HG_D, HG_D), F32)],
        compiler_params=_params("parallel", "arbitrary"),
        name="hgrn2",
    )(hg3, loga, log1m, norm_w, jnp.asarray(_HG_SUMS, BF16), jnp.asarray(_HG_MASKS, F32))
    return out.reshape(batch * length, HG_WIDTH)


def _attn_kernel(q_ref, k_ref, v_ref, toe_ref, cst_ref, w_ref, o_ref, s_ref, *, n_blocks):
    lam = cst_ref[0:1, 0:1]
    post = cst_ref[1:2, :]
    lane = lax.broadcasted_iota(jnp.int32, (Q_BLOCK, Q_BLOCK), 1)
    first_half = lane < DA_DQK
    key_ok0 = lane >= (LEAD - N_META)
    inert_bias = jnp.where(key_ok0, 0.0, NEG)

    def near_bias(kind, kb):
        bias = toe_ref[kind]
        if kb == 0:
            bias = jnp.where(key_ok0, bias, NEG)
        return bias

    def slabs(x):
        return [x[:, c:c + Q_BLOCK] for c in range(0, x.shape[1], Q_BLOCK)]

    map_rows = 2 * Q_BLOCK
    groups = [tuple(range(i, min(i + Q_GROUP, n_blocks))) for i in range(0, n_blocks, Q_GROUP)]

    def both_maps(bias):
        return jnp.concatenate([bias, bias], axis=0)

    def stacked_q(blocks):
        parts = []
        for i in blocks:
            qi = q_ref[i * Q_BLOCK:(i + 1) * Q_BLOCK, :]
            zero = jnp.zeros_like(qi)
            parts += [jnp.where(first_half, qi, zero), jnp.where(first_half, zero, qi)]
        return jnp.concatenate(parts, axis=0)

    def group_tiles(blocks):
        first, last = blocks[0], blocks[-1]
        tiles = []
        far_end = max(first - 1, 0)
        if far_end >= 1:
            tiles.append((0, Q_BLOCK, jnp.concatenate([inert_bias] * (2 * len(blocks)), axis=0), 0))
        kb = 1
        while kb < far_end:
            width = KEY_TILE if kb + KEY_TILE // Q_BLOCK <= far_end else Q_BLOCK
            tiles.append((kb * Q_BLOCK, width, None, 0))
            kb += width // Q_BLOCK
        for kb in range(far_end, last + 1):
            biases, first_row = [], None
            for r, qb in enumerate(blocks):
                if kb > qb:
                    continue
                if first_row is None:
                    first_row = r * map_rows
                if kb == qb:
                    bias = near_bias(0, kb)
                elif kb == qb - 1:
                    bias = near_bias(1, kb)
                else:
                    bias = inert_bias if kb == 0 else jnp.zeros_like(inert_bias)
                biases.append(both_maps(bias))
            tiles.append((kb * Q_BLOCK, Q_BLOCK, jnp.concatenate(biases, axis=0), first_row))
        return tiles

    def merge(acc, x, first_row, op):
        if acc is None:
            return x
        if first_row == 0:
            return op(acc, x)
        return jnp.concatenate([acc[:first_row], op(acc[first_row:], x)], axis=0)

    def sweep_scores(g):
        blocks = groups[g]
        rows = len(blocks) * map_rows
        q2 = stacked_q(blocks)
        m_acc = None
        for start, width, bias, first_row in group_tiles(blocks):
            s = _dot_nt(q2[first_row:], k_ref[start:start + width, :])
            if bias is not None:
                s = s + bias
            s_ref[g % 2, first_row:rows, start:start + width] = s
            for slab in slabs(s):
                m_acc = merge(m_acc, slab, first_row, jnp.maximum)
        return m_acc.max(axis=-1, keepdims=True)

    row_max = sweep_scores(0)
    for g, blocks in enumerate(groups):
        rows = len(blocks) * map_rows
        m = row_max
        if g + 1 < len(groups):
            row_max = sweep_scores(g + 1)
        l_acc = o_acc = None
        for start, width, _, first_row in group_tiles(blocks):
            p = jnp.exp2(s_ref[g % 2, first_row:rows, start:start + width] - m[first_row:])
            for slab in slabs(p):
                l_acc = merge(l_acc, slab, first_row, jnp.add)
            o_acc = merge(o_acc, _dot(p.astype(BF16), v_ref[start:start + width, :]), first_row, jnp.add)
        o2 = o_acc * (1.0 / l_acc.sum(axis=-1, keepdims=True))
        for r, i in enumerate(blocks):
            o = o2[r * map_rows:r * map_rows + Q_BLOCK] - lam * o2[r * map_rows + Q_BLOCK:(r + 1) * map_rows]
            o = _rms(o, w_ref[...]) * post
            o_ref[i * Q_BLOCK:(i + 1) * Q_BLOCK, :] = o.astype(o_ref.dtype)


def _attn(da, toe, cst, subln_w, batch, length):
    da3 = da.reshape(batch, length, DA_COLS)
    hw = DA_HEADS
    out = pl.pallas_call(
        functools.partial(_attn_kernel, n_blocks=length // Q_BLOCK),
        grid=(batch, DA_HEADS),
        in_specs=[
            pl.BlockSpec((None, length, DA_DV), lambda b, h: (b, 0, h)),
            pl.BlockSpec((None, length, DA_DV), lambda b, h: (b, 0, hw + h)),
            pl.BlockSpec((None, length, DA_DV), lambda b, h: (b, 0, 2 * hw + h)),
            pl.BlockSpec((None, 2, Q_BLOCK, Q_BLOCK), lambda b, h: (h, 0, 0, 0)),
            pl.BlockSpec((8, LANES), lambda b, h: (0, 0)),
            pl.BlockSpec((1, DA_DV), lambda b, h: (0, 0)),
        ],
        out_specs=pl.BlockSpec((None, length, DA_DV), lambda b, h: (b, 0, h)),
        out_shape=jax.ShapeDtypeStruct((batch, length, DA_HEADS * DA_DV), BF16),
        scratch_shapes=[pltpu.VMEM((2, Q_GROUP * 2 * Q_BLOCK, length), F32)],
        compiler_params=_params("parallel", "parallel"),
        name="diff_attn",
    )(da3, da3, da3, toe, cst, subln_w)
    return out.reshape(batch * length, DA_HEADS * DA_DV)


def _t5_bucket(dist):
    n = jnp.maximum(dist, 0)
    max_exact = REL_BUCKETS // 2
    nf = jnp.maximum(n, max_exact).astype(F32)
    large = max_exact + (jnp.log(nf / max_exact) / math.log(REL_MAX_DIST / max_exact)
                         * (REL_BUCKETS - max_exact)).astype(jnp.int32)
    large = jnp.minimum(large, REL_BUCKETS - 1)
    return jnp.where(n < max_exact, n, large)


def _attn_bias_tables(rel_bias):
    tab = rel_bias.astype(F32)
    qi = jnp.arange(Q_BLOCK, dtype=jnp.int32)[:, None]
    ki = jnp.arange(Q_BLOCK, dtype=jnp.int32)[None, :]

    def lookup(bucket):
        onehot = bucket[None, :, :, None] == jnp.arange(REL_BUCKETS, dtype=jnp.int32)
        return jnp.sum(jnp.where(onehot, tab.T[:, None, None, :], 0.0), axis=-1)

    far = tab[REL_BUCKETS - 1][:, None, None]
    diag = jnp.where((ki <= qi)[None], (lookup(_t5_bucket(qi - ki)) - far) * LOG2E, NEG)
    prev = (lookup(_t5_bucket(qi - ki + Q_BLOCK)) - far) * LOG2E
    return jnp.stack([diag, prev], axis=1)


def _out_proj_router_kernel(h_ref, ohg_ref, oda_ref, wo_ref, g_ref, router_ref, before_ref,
                            hn_ref, u_ref, route_ref, gate_ref, count_ref, seen_ref):
    @pl.when(pl.program_id(0) == 0)
    def _():
        seen_ref[...] = jnp.zeros_like(seen_ref)

    hn = (h_ref[...] + _dot(ohg_ref[...], wo_ref[:HG_WIDTH, :])
          + _dot(oda_ref[...], wo_ref[HG_WIDTH:, :]))
    hn_ref[...] = hn
    u = _rms(hn, g_ref[...])
    _store_token_major(u_ref, u)
    u_hi = u.astype(BF16)
    u_lo = (u - u_hi.astype(F32)).astype(BF16)
    logits = _dot(jnp.concatenate([u_hi, u_lo, u_hi], axis=1), router_ref[...])
    logits = logits.T[:N_EXPERTS]
    expert = lax.broadcasted_iota(jnp.int32, logits.shape, 0)

    def top(x):
        best = x.max(axis=0, keepdims=True)
        return best, jnp.where(x == best, expert, N_EXPERTS).min(axis=0, keepdims=True)

    l1, e1 = top(logits)
    l2, e2 = top(jnp.where(expert == e1, -jnp.inf, logits))
    w2 = jnp.exp(l2 - l1)
    g1 = 1.0 / (1.0 + w2)
    g2 = w2 / (1.0 + w2)

    pick1 = (expert == e1).astype(F32)
    pick2 = (expert == e2).astype(F32)
    picked = pick1 + pick2
    ahead = seen_ref[:, 0:1] + _dot(picked.astype(BF16), before_ref[...])
    rank1 = jnp.sum(pick1 * ahead, axis=0, keepdims=True).astype(jnp.int32)
    rank2 = jnp.sum(pick2 * ahead, axis=0, keepdims=True).astype(jnp.int32)
    seen = seen_ref[...] + jnp.sum(picked, axis=1, keepdims=True)
    seen_ref[...] = seen

    route_ref[...] = jnp.where(expert == 0, e1, jnp.where(expert == 1, e2, jnp.where(
        expert == 2, rank1, jnp.where(expert == 3, rank2, 0))))
    gate_ref[...] = jnp.where(expert == 0, g1, jnp.where(expert == 1, g2, 0.0))
    count_ref[...] = seen.astype(jnp.int32)


def _out_proj_router(h, o_hg, o_da, wo, layer, gain, router):
    n = h.shape[0]
    row = lambda i: (i, 0)
    full = lambda i: (0, 0)
    assert N_EXPERTS >= 2 * TOP_K
    lanes_of = lambda i: (0, i)
    before = jnp.asarray(np.triu(np.ones((ROW_TILE, ROW_TILE), np.float32), 1), BF16)
    return pl.pallas_call(
        _out_proj_router_kernel,
        grid=(n // ROW_TILE,),
        in_specs=[
            pl.BlockSpec((ROW_TILE, D_MODEL), row),
            pl.BlockSpec((ROW_TILE, HG_WIDTH), row),
            pl.BlockSpec((ROW_TILE, DA_HEADS * DA_DV), row),
            pl.BlockSpec((None,) + wo.shape[1:], lambda i: (layer, 0, 0)),
            pl.BlockSpec((1, D_MODEL), full),
            pl.BlockSpec(router.shape, full),
            pl.BlockSpec(before.shape, full),
        ],
        out_specs=[
            pl.BlockSpec((ROW_TILE, D_MODEL), row),
            pl.BlockSpec((ROW_TILE * TOKEN_ROWS, LANES), row),
            pl.BlockSpec((N_EXPERTS, ROW_TILE), lanes_of),
            pl.BlockSpec((N_EXPERTS, ROW_TILE), lanes_of),
            pl.BlockSpec((N_EXPERTS, LANES), full),
        ],
        out_shape=[
            jax.ShapeDtypeStruct((n, D_MODEL), F32),
            jax.ShapeDtypeStruct((n * TOKEN_ROWS, LANES), F32),
            jax.ShapeDtypeStruct((N_EXPERTS, n), jnp.int32),
            jax.ShapeDtypeStruct((N_EXPERTS, n), F32),
            jax.ShapeDtypeStruct((N_EXPERTS, LANES), jnp.int32),
        ],
        scratch_shapes=[pltpu.VMEM((N_EXPERTS, LANES), F32)],
        compiler_params=_params("arbitrary"),
        name="out_proj_router",
    )(h, o_hg, o_da, wo, gain, router, before)


CAST_STEPS = 64


def _dense_layer_kernel(h_ref, ohg_ref, oda_ref, wo_ref, g_ref, w1_ref, w3_ref, w2_ref, *rest):
    n_cast = (len(rest) - 1) // 2
    o_ref = rest[n_cast]
    hn = (h_ref[...] + _dot(ohg_ref[...], wo_ref[:HG_WIDTH, :])
          + _dot(oda_ref[...], wo_ref[HG_WIDTH:, :]))
    u = _rms(hn, g_ref[...]).astype(BF16)
    a = _dot(u, w1_ref[...])
    act = (_silu(a) * _dot(u, w3_ref[...])).astype(BF16)
    o_ref[...] = hn + _dot(act, w2_ref[...])
    for src_ref, dst_ref in zip(rest[:n_cast], rest[n_cast + 1:]):
        dst_ref[...] = src_ref[...].astype(BF16)


def _dense_layer(h, o_hg, o_da, wo, layer, gain, w1, w3, w2, idx, to_cast, cast_idx):
    n = h.shape[0]
    steps = n // ROW_TILE
    assert steps >= CAST_STEPS
    row = lambda i: (i, 0)
    slab = lambda i: (jnp.minimum(i, CAST_STEPS - 1), 0)

    def resident(w, index):
        return pl.BlockSpec((None,) + w.shape[1:], lambda i: (index, 0, 0),
                            pipeline_mode=pl.Buffered(1))

    cast_in, cast_in_specs, cast_out_specs, cast_out_shape = [], [], [], []
    for w in to_cast:
        n_moe, n_exp, rows, cols = w.shape
        assert (n_exp * rows) % CAST_STEPS == 0
        slab_rows = n_exp * rows // CAST_STEPS
        cast_in.append(w.reshape(n_moe, n_exp * rows, cols))
        cast_in_specs.append(pl.BlockSpec((None, slab_rows, cols),
                                          lambda i: (cast_idx, jnp.minimum(i, CAST_STEPS - 1), 0)))
        cast_out_specs.append(pl.BlockSpec((slab_rows, cols), slab))
        cast_out_shape.append(jax.ShapeDtypeStruct((n_exp * rows, cols), BF16))

    out = pl.pallas_call(
        _dense_layer_kernel,
        grid=(steps,),
        in_specs=[
            pl.BlockSpec((ROW_TILE, D_MODEL), row),
            pl.BlockSpec((ROW_TILE, HG_WIDTH), row),
            pl.BlockSpec((ROW_TILE, DA_HEADS * DA_DV), row),
            resident(wo, layer),
            pl.BlockSpec((1, D_MODEL), lambda i: (0, 0)),
            resident(w1, idx),
            resident(w3, idx),
            resident(w2, idx),
        ] + cast_in_specs,
        out_specs=[pl.BlockSpec((ROW_TILE, D_MODEL), row)] + cast_out_specs,
        out_shape=[jax.ShapeDtypeStruct((n, D_MODEL), F32)] + cast_out_shape,
        compiler_params=_params("arbitrary"),
        name="dense_layer",
    )(h, o_hg, o_da, wo, gain, w1, w3, w2, *cast_in)
    return out[0], [b.reshape(w.shape[1:]) for b, w in zip(out[1:], to_cast)]


DISPATCH_TILE = 1024
ISSUE_TOKENS = 8


def _token_copy(src_ref, src_tok, dst_ref, dst_tok, sem):
    src = pl.multiple_of(src_tok * TOKEN_ROWS, TOKEN_ROWS)
    dst = pl.multiple_of(dst_tok * TOKEN_ROWS, TOKEN_ROWS)
    return pltpu.make_async_copy(src_ref.at[pl.ds(src, TOKEN_ROWS), :],
                                 dst_ref.at[pl.ds(dst, TOKEN_ROWS), :], sem)


def _dispatch_kernel(dest_ref, empty_ref, u_ref, xs_ref, zero_ref, sem):
    step = pl.program_id(0)
    base = step * DISPATCH_TILE
    tile_rows = DISPATCH_TILE * TOKEN_ROWS

    def retire_tile():
        pltpu.make_async_copy(u_ref, xs_ref.at[pl.ds(0, tile_rows), :], sem).wait()

    @pl.when(step == 0)
    def _():
        zero_ref[...] = jnp.zeros_like(zero_ref)

        def clear(g, carry):
            slots = [empty_ref[g * ISSUE_TOKENS * TOP_K + j] for j in range(ISSUE_TOKENS * TOP_K)]
            for j, slot in enumerate(slots):
                _token_copy(zero_ref, 0, xs_ref, slot, sem).start(priority=j % 2)
            return carry

        n_empty = empty_ref.shape[0]
        lax.fori_loop(0, n_empty // (ISSUE_TOKENS * TOP_K), clear, 0)
        for _ in range(n_empty // DISPATCH_TILE):
            retire_tile()

    def start(g, carry):
        r0 = g * ISSUE_TOKENS
        slots = [dest_ref[(base + r0) * TOP_K + j] for j in range(ISSUE_TOKENS * TOP_K)]
        for j, slot in enumerate(slots):
            _token_copy(u_ref, r0 + j // TOP_K, xs_ref, slot, sem).start(priority=j % 2)
        return carry

    lax.fori_loop(0, DISPATCH_TILE // ISSUE_TOKENS, start, 0)
    for _ in range(TOP_K):
        retire_tile()


def _dispatch(dest, empty_slots, u_tm, n_slots):
    n = dest.shape[0] // TOP_K
    assert empty_slots.shape[0] % DISPATCH_TILE == 0
    return pl.pallas_call(
        _dispatch_kernel,
        grid_spec=pltpu.PrefetchScalarGridSpec(
            num_scalar_prefetch=2,
            grid=(n // DISPATCH_TILE,),
            in_specs=[pl.BlockSpec((DISPATCH_TILE * TOKEN_ROWS, LANES), lambda i, d, e: (i, 0))],
            out_specs=pl.BlockSpec(memory_space=pl.ANY),
            scratch_shapes=[pltpu.VMEM((TOKEN_ROWS, LANES), u_tm.dtype),
                            pltpu.SemaphoreType.DMA(())],
        ),
        out_shape=jax.ShapeDtypeStruct((n_slots * TOKEN_ROWS, LANES), u_tm.dtype),
        compiler_params=_params("arbitrary"),
        name="moe_dispatch",
    )(dest, empty_slots, u_tm)


def _expert_kernel(be_ref, nused_ref, xs_ref, w1_ref, w3_ref, w2_ref, y_ref):
    del be_ref
    blk = pl.program_id(0)

    @pl.when(blk < nused_ref[0])
    def _():
        x = _load_token_major(xs_ref, MOE_TILE).astype(BF16)
        acc = jnp.zeros((MOE_TILE, D_MODEL), F32)
        for c0 in range(0, D_FF_EXPERT, FF_CHUNK):
            a = _dot(x, w1_ref[:, c0:c0 + FF_CHUNK])
            act = (_silu(a) * _dot(x, w3_ref[:, c0:c0 + FF_CHUNK])).astype(BF16)
            acc = acc + _dot(act, w2_ref[c0:c0 + FF_CHUNK, :])
        _store_token_major(y_ref, acc)

    @pl.when(blk >= nused_ref[0])
    def _():
        y_ref[...] = jnp.zeros_like(y_ref)


def _experts(block_expert, n_used, xs, w1, w3, w2):
    n_blocks = xs.shape[0] // (MOE_TILE * TOKEN_ROWS)
    wmap = lambda i, be, nu: (be[i], 0, 0)
    slots = pl.BlockSpec((MOE_TILE * TOKEN_ROWS, LANES), lambda i, be, nu: (i, 0))
    return pl.pallas_call(
        _expert_kernel,
        grid_spec=pltpu.PrefetchScalarGridSpec(
            num_scalar_prefetch=2,
            grid=(n_blocks,),
            in_specs=[
                slots,
                pl.BlockSpec((None, D_MODEL, D_FF_EXPERT), wmap),
                pl.BlockSpec((None, D_MODEL, D_FF_EXPERT), wmap),
                pl.BlockSpec((None, D_FF_EXPERT, D_MODEL), wmap),
            ],
            out_specs=slots,
        ),
        out_shape=jax.ShapeDtypeStruct(xs.shape, F32),
        compiler_params=_params("arbitrary"),
        name="moe_experts",
    )(block_expert, n_used, xs, w1, w3, w2)


def _combine_kernel(dest_ref, h_ref, gate_ref, y_ref, o_ref, buf_ref, sem):
    step = pl.program_id(0)

    def issue(s, slot):
        base = s * ROW_TILE

        def body(g, carry):
            r0 = g * ISSUE_TOKENS
            slots = [dest_ref[(base + r0) * TOP_K + j] for j in range(ISSUE_TOKENS * TOP_K)]
            for j, src in enumerate(slots):
                _token_copy(y_ref, src, buf_ref.at[slot, j % TOP_K], r0 + j // TOP_K,
                            sem.at[slot]).start(priority=j % 2)
            return carry

        lax.fori_loop(0, ROW_TILE // ISSUE_TOKENS, body, 0)

    @pl.when(step == 0)
    def _():
        issue(0, 0)

    for slot in range(2):
        @pl.when(step % 2 == slot)
        def _():
            @pl.when(step + 1 < pl.num_programs(0))
            def _():
                issue(step + 1, 1 - slot)

            for k in range(TOP_K):
                pltpu.make_async_copy(y_ref.at[pl.ds(0, ROW_TILE * TOKEN_ROWS), :],
                                      buf_ref.at[slot, k], sem.at[slot]).wait()
            gate = gate_ref[...]
            o_ref[...] = (h_ref[...]
                          + gate[:, 0:1] * _load_token_major(buf_ref.at[slot, 0], ROW_TILE)
                          + gate[:, 1:2] * _load_token_major(buf_ref.at[slot, 1], ROW_TILE))


def _combine(dest, h, gate, y):
    n = h.shape[0]
    return pl.pallas_call(
        _combine_kernel,
        grid_spec=pltpu.PrefetchScalarGridSpec(
            num_scalar_prefetch=1,
            grid=(n // ROW_TILE,),
            in_specs=[
                pl.BlockSpec((ROW_TILE, D_MODEL), lambda i, d: (i, 0)),
                pl.BlockSpec((ROW_TILE, TOP_K), lambda i, d: (i, 0)),
                pl.BlockSpec(memory_space=pl.ANY),
            ],
            out_specs=pl.BlockSpec((ROW_TILE, D_MODEL), lambda i, d: (i, 0)),
            scratch_shapes=[pltpu.VMEM((2, TOP_K, ROW_TILE * TOKEN_ROWS, LANES), F32),
                            pltpu.SemaphoreType.DMA((2,))],
        ),
        out_shape=jax.ShapeDtypeStruct((n, D_MODEL), F32),
        compiler_params=_params("arbitrary"),
        name="moe_combine",
    )(dest, h, gate, y)


def _prefix_sum(x):
    k = x.shape[0]
    keep = (np.arange(k)[None, :] <= np.arange(k)[:, None]).reshape((k, k) + (1,) * (x.ndim - 1))
    return jnp.sum(jnp.where(keep, x[None], jnp.zeros_like(x[None])), axis=1)


def _slot_tables(route, counts, n_slots):
    n = route.shape[1]
    flat_e = route[0:TOP_K].T.reshape(n * TOP_K)
    rank = route[TOP_K:2 * TOP_K].T.reshape(n * TOP_K)
    counts = counts[:, 0]
    padded = (counts + MOE_TILE - 1) // MOE_TILE * MOE_TILE
    pad_end = _prefix_sum(padded)
    pad_start = pad_end - padded
    onehot = flat_e[:, None] == jnp.arange(N_EXPERTS, dtype=jnp.int32)[None, :]
    dest = (rank + jnp.sum(jnp.where(onehot, pad_start[None, :], 0), axis=-1)).astype(jnp.int32)
    n_blocks = n_slots // MOE_TILE
    block_expert = jnp.minimum(
        jnp.searchsorted(pad_end, jnp.arange(n_blocks, dtype=jnp.int32) * MOE_TILE, side="right"),
        N_EXPERTS - 1).astype(jnp.int32)
    n_used = (pad_end[-1:] // MOE_TILE).astype(jnp.int32)
    n_empty = n_slots - n * TOP_K
    gap_end = _prefix_sum(padded - counts)
    j = jnp.arange(n_empty, dtype=jnp.int32)
    owner = jnp.sum((j[:, None] >= gap_end[None, :]).astype(jnp.int32), axis=-1)
    first_empty = jnp.concatenate([pad_start + counts, pad_end[-1:]])
    gap_start = jnp.concatenate([jnp.zeros((1,), gap_end.dtype), gap_end])
    pick = owner[:, None] == jnp.arange(N_EXPERTS + 1, dtype=jnp.int32)[None, :]
    empty_slots = (j + jnp.sum(jnp.where(pick, (first_empty - gap_start)[None, :], 0), axis=-1))
    return dest, block_expert, n_used, empty_slots.astype(jnp.int32)


def _moe_ffn(h, u, route, gate, counts, w1, w3, w2):
    n = h.shape[0]
    n_slots = (n * TOP_K // MOE_TILE + N_EXPERTS) * MOE_TILE
    dest, block_expert, n_used, empty_slots = _slot_tables(route, counts, n_slots)
    xs = _dispatch(dest, empty_slots, u, n_slots)
    y = _experts(block_expert, n_used, xs, w1, w3, w2)
    return _combine(dest, h, gate[0:TOP_K].T, y)


def _final_kernel(h_ref, g_ref, o_ref):
    o_ref[...] = _rms(h_ref[...], g_ref[...])


def _final_norm(h3, gain, seq):
    batch = h3.shape[0]
    lead_blocks = LEAD // Q_BLOCK
    return pl.pallas_call(
        _final_kernel,
        grid=(seq // Q_BLOCK,),
        in_specs=[
            pl.BlockSpec((batch, Q_BLOCK, D_MODEL), lambda j: (0, j + lead_blocks, 0)),
            pl.BlockSpec((1, D_MODEL), lambda j: (0, 0)),
        ],
        out_specs=pl.BlockSpec((batch, Q_BLOCK, D_MODEL), lambda j: (0, j, 0)),
        out_shape=jax.ShapeDtypeStruct((batch, seq, D_MODEL), F32),
        compiler_params=_params("parallel"),
        name="final_norm",
    )(h3, gain)


def kernel(x, meta, rel_bias, norm_mix, w_in, hg_lb_logits, hg_norm_w, da_lambda, da_subln_w, w_out, norm_ffn, dense_w1, dense_w3, dense_w2, moe_router, moe_w1, moe_w3, moe_w2, final_norm):
    batch, seq, d = x.shape
    length = LEAD + seq
    h = jnp.concatenate([
        jnp.zeros((batch, LEAD - N_META, d), x.dtype),
        jnp.broadcast_to(meta[None].astype(x.dtype), (batch, N_META, d)),
        x], axis=1).reshape(batch * length, d)

    toe = _attn_bias_tables(rel_bias)
    lb_cum = _prefix_sum(jax.nn.softmax(hg_lb_logits.astype(F32), axis=0))
    lb_all = jnp.clip(lb_cum - lb_cum[0:1], 0.0, LB_MAX)
    log_lb = jnp.log(lb_all)
    log_1m_lb = jnp.log1p(-lb_all)

    w_in_b, w_out_b = w_in.astype(BF16), w_out.astype(BF16)
    dense_b = [w.astype(BF16) for w in (dense_w1, dense_w3, dense_w2)]
    assert DEPTH % 2 == 0

    for l in range(DEPTH):
        hg, da = _mix_in(h, norm_mix[l][None], w_in_b, l)
        o_hg = _hgrn(hg, log_lb[l][None], log_1m_lb[l][None], hg_norm_w[l][None], batch, length)
        lam_init = 0.8 - 0.6 * math.exp(-0.3 * l)
        lv = da_lambda[l].astype(F32)
        lam = jnp.exp(jnp.sum(lv[0] * lv[1])) - jnp.exp(jnp.sum(lv[2] * lv[3])) + lam_init
        cst = jnp.zeros((8, LANES), F32).at[0].set(lam).at[1].set(1.0 - lam_init)
        o_da = _attn(da, toe, cst, da_subln_w[l][None], batch, length)
        i = l // 2
        if l % 2 == 0:
            h, moe_b = _dense_layer(h, o_hg, o_da, w_out_b, l, norm_ffn[l][None], *dense_b, i,
                                    (moe_w1, moe_w3, moe_w2), i)
        else:
            router = jnp.zeros((d, LANES), F32).at[:, :N_EXPERTS].set(moe_router[i].astype(F32))
            r_hi = router.astype(BF16)
            r_lo = (router - r_hi.astype(F32)).astype(BF16)
            router = jnp.concatenate([r_hi, r_hi, r_lo], axis=0)
            hn, u, route, gate, counts = _out_proj_router(h, o_hg, o_da, w_out_b, l,
                                                          norm_ffn[l][None], router)
            h = _moe_ffn(hn, u, route, gate, counts, *moe_b)
    return _final_norm(h.reshape(batch, length, d), final_norm[None], seq)
```

```python
import functools
import math

import jax
import jax.numpy as jnp
import numpy as np
from jax import lax
from jax.experimental import pallas as pl
from jax.experimental.pallas import tpu as pltpu

D_MODEL = 1024
DEPTH = 4
N_META = 16
LEAD = 128
HG_WIDTH = 512
HG_HEADS = 4
HG_D = 128
HG_CHUNK = 64
DA_HEADS = 4
DA_DQK = 64
DA_DV = 128
Q_BLOCK = 128
KEY_TILE = 2 * Q_BLOCK
Q_GROUP = 2
LOG2E = math.log2(math.e)
Q_SCALE = DA_DQK ** -0.5 * LOG2E
REL_BUCKETS = 32
REL_MAX_DIST = 128
N_EXPERTS = 8
TOP_K = 2
D_FF_EXPERT = 3584
EPS = 1e-6
NEG = -1e30
LB_MAX = 0.999
HG_COLS = 4 * HG_WIDTH
DA_COLS = 3 * DA_HEADS * DA_DV
W_IN_COLS = HG_COLS + DA_COLS

LANES = 128
VMEM_LIMIT = 56 * 1024 * 1024

ROW_TILE = 256
MIX_TILE = 512
HG_TILE = 128
MOE_TILE = 256
FF_CHUNK = 512

F32 = jnp.float32
BF16 = jnp.bfloat16


def _params(*sem):
    return pltpu.CompilerParams(dimension_semantics=sem, vmem_limit_bytes=VMEM_LIMIT)


def _dot(a, b):
    return jnp.dot(a, b, preferred_element_type=F32)


def _dot_nt(a, b):
    return lax.dot_general(a, b, (((1,), (1,)), ((), ())), preferred_element_type=F32)


def _dot_tn(a, b):
    return lax.dot_general(a, b, (((0,), (0,)), ((), ())), preferred_element_type=F32)


def _rms(x, gain):
    return x * lax.rsqrt(jnp.mean(x * x, axis=-1, keepdims=True) + EPS) * gain


TOKEN_ROWS = D_MODEL // LANES


def _store_token_major(ref, x):
    t = x.shape[0]
    for s in range(TOKEN_ROWS):
        ref[pl.ds(s, t, stride=TOKEN_ROWS), :] = x[:, s * LANES:(s + 1) * LANES]


def _load_token_major(ref, t):
    return jnp.concatenate(
        [ref[pl.ds(s, t, stride=TOKEN_ROWS), :] for s in range(TOKEN_ROWS)], axis=1)


def _silu(x):
    return x * (0.5 * jnp.tanh(0.5 * x) + 0.5)


def _mix_in_kernel(x_ref, g_ref, w_ref, hg_ref, da_ref):
    u = _rms(x_ref[...], g_ref[...]).astype(BF16)
    hg_ref[...] = _dot(u, w_ref[:, :HG_COLS])
    n_q = DA_HEADS * 2 * DA_DQK
    da_ref[:, :n_q] = (_dot(u, w_ref[:, HG_COLS:HG_COLS + n_q]) * Q_SCALE).astype(BF16)
    da_ref[:, n_q:] = _dot(u, w_ref[:, HG_COLS + n_q:]).astype(BF16)


def _mix_in(h, gain, w, layer):
    n = h.shape[0]
    return pl.pallas_call(
        _mix_in_kernel,
        grid=(n // MIX_TILE,),
        in_specs=[
            pl.BlockSpec((MIX_TILE, D_MODEL), lambda i: (i, 0)),
            pl.BlockSpec((1, D_MODEL), lambda i: (0, 0)),
            pl.BlockSpec((None, D_MODEL, W_IN_COLS), lambda i: (layer, 0, 0),
                         pipeline_mode=pl.Buffered(1)),
        ],
        out_specs=[
            pl.BlockSpec((MIX_TILE, HG_COLS), lambda i: (i, 0)),
            pl.BlockSpec((MIX_TILE, DA_COLS), lambda i: (i, 0)),
        ],
        out_shape=[
            jax.ShapeDtypeStruct((n, HG_COLS), F32),
            jax.ShapeDtypeStruct((n, DA_COLS), BF16),
        ],
        compiler_params=_params("parallel"),
        name="mix_in",
    )(h, gain, w)


def _mix_in_combine_kernel(dest_ref, hn_ref, gate_ref, y_ref, g_ref, w_ref,
                           h_ref, hg_ref, da_ref, buf_ref, sem):
    step = pl.program_id(0)
    last = pl.num_programs(0) - 1
    slot = step % 2
    tile_rows = MIX_TILE * TOKEN_ROWS

    def start_gathers(s, to_slot, rows):
        base = s * MIX_TILE
        slots = [dest_ref[(base + r) * TOP_K + k] for r in rows for k in range(TOP_K)]
        for j, src in enumerate(slots):
            r, k = rows[j // TOP_K], j % TOP_K
            _token_copy(y_ref, src, buf_ref.at[to_slot, k], r, sem.at[to_slot]).start(priority=j % 2)

    def wait_gathers(of_slot):
        for k in range(TOP_K):
            pltpu.make_async_copy(y_ref.at[pl.ds(0, tile_rows), :], buf_ref.at[of_slot, k],
                                  sem.at[of_slot]).wait()

    @pl.when(step == 0)
    def _():
        def first(g, carry):
            start_gathers(0, 0, [g * ISSUE_TOKENS + r for r in range(ISSUE_TOKENS)])
            return carry

        lax.fori_loop(0, MIX_TILE // ISSUE_TOKENS, first, 0)

    wait_gathers(slot)
    gate = gate_ref[...]
    h = (hn_ref[...] + gate[:, 0:1] * _load_token_major(buf_ref.at[slot, 0], MIX_TILE)
         + gate[:, 1:2] * _load_token_major(buf_ref.at[slot, 1], MIX_TILE))
    h_ref[...] = h
    u = _rms(h, g_ref[...]).astype(BF16)

    nxt = jnp.minimum(step + 1, last)
    n_q = DA_HEADS * 2 * DA_DQK
    groups = [(c, c + HG_WIDTH) for c in range(0, HG_COLS, HG_WIDTH)]
    groups += [(HG_COLS, HG_COLS + n_q), (HG_COLS + n_q, W_IN_COLS)]
    per_group = -(-MIX_TILE // len(groups))
    for gi, (c0, c1) in enumerate(groups):
        proj = _dot(u, w_ref[:, c0:c1])
        if c1 <= HG_COLS:
            hg_ref[:, c0:c1] = proj
        elif c0 == HG_COLS:
            da_ref[:, :n_q] = (proj * Q_SCALE).astype(BF16)
        else:
            da_ref[:, n_q:] = proj.astype(BF16)
        rows = list(range(gi * per_group, min((gi + 1) * per_group, MIX_TILE)))
        for r0 in range(0, len(rows), ISSUE_TOKENS):
            start_gathers(nxt, 1 - slot, rows[r0:r0 + ISSUE_TOKENS])

    @pl.when(step == last)
    def _():
        wait_gathers(1 - slot)


def _mix_in_combine(dest, hn, gate, y, gain, w, layer):
    n = hn.shape[0]
    row = lambda i, d: (i, 0)
    return pl.pallas_call(
        _mix_in_combine_kernel,
        grid_spec=pltpu.PrefetchScalarGridSpec(
            num_scalar_prefetch=1,
            grid=(n // MIX_TILE,),
            in_specs=[
                pl.BlockSpec((MIX_TILE, D_MODEL), row),
                pl.BlockSpec((MIX_TILE, TOP_K), row),
                pl.BlockSpec(memory_space=pl.ANY),
                pl.BlockSpec((1, D_MODEL), lambda i, d: (0, 0)),
                pl.BlockSpec((None, D_MODEL, W_IN_COLS), lambda i, d: (layer, 0, 0),
                             pipeline_mode=pl.Buffered(1)),
            ],
            out_specs=[
                pl.BlockSpec((MIX_TILE, D_MODEL), row),
                pl.BlockSpec((MIX_TILE, HG_COLS), row),
                pl.BlockSpec((MIX_TILE, DA_COLS), row),
            ],
            scratch_shapes=[pltpu.VMEM((2, TOP_K, MIX_TILE * TOKEN_ROWS, LANES), F32),
                            pltpu.SemaphoreType.DMA((2,))],
        ),
        out_shape=[
            jax.ShapeDtypeStruct((n, D_MODEL), F32),
            jax.ShapeDtypeStruct((n, HG_COLS), F32),
            jax.ShapeDtypeStruct((n, DA_COLS), BF16),
        ],
        compiler_params=_params("arbitrary"),
        name="mix_in_combine",
    )(dest, hn, gate, y, gain, w)


HG_LEVELS = (32, 16, 8, 4, 2, 1)
N_SUMS = len(HG_LEVELS) + 2


def _hgrn_consts():
    c = HG_CHUNK
    t = np.arange(c)[:, None]
    j = np.arange(c)[None, :]
    sums = np.zeros((N_SUMS, c, c), np.float32)
    masks = np.zeros((len(HG_LEVELS) + 1, c, c), np.float32)
    sums[0] = j <= t
    masks[0] = np.eye(c)
    for li, w in enumerate(HG_LEVELS, start=1):
        ref = (t // (2 * w)) * (2 * w) + w
        sums[li] = np.where(t >= ref, (j > ref) & (j <= t), (j > t) & (j <= ref))
        masks[li] = (t // (2 * w) == j // (2 * w)) & (t % (2 * w) >= w) & (j % (2 * w) < w)
    sums[N_SUMS - 1] = j > t
    sums = sums.reshape(N_SUMS * c, c)
    return np.concatenate([sums, sums], axis=1), masks


_HG_SUMS, _HG_MASKS = _hgrn_consts()


def _hgrn_kernel(hg_ref, loga_ref, log1m_ref, nw_ref, sums_ref, masks_ref, o_ref, state_ref):
    c_idx = pl.program_id(1)

    @pl.when(c_idx == 0)
    def _():
        state_ref[...] = jnp.zeros_like(state_ref)

    C = HG_CHUNK
    W = HG_WIDTH
    n_chunks = HG_TILE // C
    sums = sums_ref[...]
    nw = nw_ref[...]
    row_idx = c_idx * HG_TILE + lax.broadcasted_iota(jnp.int32, (HG_TILE, 1), 0)
    valid = row_idx >= (LEAD - N_META)
    step = lax.broadcasted_iota(jnp.int32, (C, 1), 0)
    head_cols = [slice(hd * HG_D, (hd + 1) * HG_D) for hd in range(HG_HEADS)]

    f = hg_ref[:, W:2 * W]
    qf = _silu(hg_ref[:, 0:W])
    ls = jnp.minimum(f, 0.0) - jnp.log(1.0 + jnp.exp(-jnp.abs(f)))
    cc = log1m_ref[...] + ls
    loga = loga_ref[...]
    lf = jnp.maximum(loga, cc) + jnp.log(1.0 + jnp.exp(-jnp.abs(loga - cc)))
    kk = jnp.exp(cc - f)
    lf = jnp.where(valid, lf, 0.0)
    kk = jnp.where(valid, kk, 0.0)
    lf2 = lf * LOG2E
    lf_hi = lf2.astype(BF16)
    lf_lo = (lf2 - lf_hi.astype(F32)).astype(BF16)
    vb = hg_ref[:, 2 * W:3 * W].astype(BF16)
    gate = _silu(hg_ref[:, 3 * W:4 * W])

    def side_by_side(x):
        return jnp.concatenate([x[ch * C:(ch + 1) * C] for ch in range(n_chunks)], axis=1)

    qf_w, kk_w = side_by_side(qf), side_by_side(kk)
    qb_w, kb_w = qf_w.astype(BF16), kk_w.astype(BF16)
    e = jnp.exp2(_dot(sums, jnp.concatenate([side_by_side(lf_hi), side_by_side(lf_lo)], axis=0)))
    e_b = e[0:C]
    decay_end = e_b[C - 1:C, :]
    q_in = (qf_w * e_b).astype(BF16)
    k_out = (kk_w * e[(N_SUMS - 1) * C:N_SUMS * C]).astype(BF16)
    z = [(jnp.where((step & w) != 0, qf_w, kk_w) * e[li * C:(li + 1) * C]).astype(BF16)
         for li, w in enumerate(HG_LEVELS, start=1)]
    unit_cols = [[slice(ch * W + hd * HG_D, ch * W + (hd + 1) * HG_D) for hd in range(HG_HEADS)]
                 for ch in range(n_chunks)]
    scores = []
    for ch in range(n_chunks):
        scores.append([])
        for cols in unit_cols[ch]:
            s = masks_ref[0] * _dot_nt(qb_w[:, cols], kb_w[:, cols])
            for li in range(1, len(HG_LEVELS) + 1):
                zl = z[li - 1][:, cols]
                s += masks_ref[li] * _dot_nt(zl, zl)
            scores[ch].append(s.astype(BF16))

    for ch in range(n_chunks):
        rows = slice(ch * C, (ch + 1) * C)
        for hd, cols in enumerate(head_cols):
            wide = unit_cols[ch][hd]
            st = state_ref[hd]
            v_h = vb[rows, cols]
            o = _dot_nt(q_in[:, wide], st.astype(BF16)) + _dot(scores[ch][hd], v_h)
            state_ref[hd] = st * decay_end[:, wide] + _dot_tn(v_h, k_out[:, wide])
            o = _rms(o, nw) * gate[rows, cols]
            o_ref[rows, cols] = o.astype(o_ref.dtype)


def _hgrn(hg, loga, log1m, norm_w, batch, length):
    hg3 = hg.reshape(batch, length, HG_COLS)
    out = pl.pallas_call(
        _hgrn_kernel,
        grid=(batch, length // HG_TILE),
        in_specs=[
            pl.BlockSpec((None, HG_TILE, HG_COLS), lambda b, c: (b, c, 0)),
            pl.BlockSpec((1, HG_WIDTH), lambda b, c: (0, 0)),
            pl.BlockSpec((1, HG_WIDTH), lambda b, c: (0, 0)),
            pl.BlockSpec((1, HG_D), lambda b, c: (0, 0)),
            pl.BlockSpec(_HG_SUMS.shape, lambda b, c: (0, 0)),
            pl.BlockSpec(_HG_MASKS.shape, lambda b, c: (0, 0, 0)),
        ],
        out_specs=pl.BlockSpec((None, HG_TILE, HG_WIDTH), lambda b, c: (b, c, 0)),
        out_shape=jax.ShapeDtypeStruct((batch, length, HG_WIDTH), BF16),
        scratch_shapes=[pltpu.VMEM((HG_HEADS, HG_D, HG_D), F32)],
        compiler_params=_params("parallel", "arbitrary"),
        name="hgrn2",
    )(hg3, loga, log1m, norm_w, jnp.asarray(_HG_SUMS, BF16), jnp.asarray(_HG_MASKS, F32))
    return out.reshape(batch * length, HG_WIDTH)


def _attn_kernel(q_ref, k_ref, v_ref, toe_ref, cst_ref, w_ref, o_ref, s_ref, *, n_blocks):
    lam = cst_ref[0:1, 0:1]
    post = cst_ref[1:2, :]
    lane = lax.broadcasted_iota(jnp.int32, (Q_BLOCK, Q_BLOCK), 1)
    first_half = lane < DA_DQK
    key_ok0 = lane >= (LEAD - N_META)
    inert_bias = jnp.where(key_ok0, 0.0, NEG)

    def near_bias(kind, kb):
        bias = toe_ref[kind]
        if kb == 0:
            bias = jnp.where(key_ok0, bias, NEG)
        return bias

    def slabs(x):
        return [x[:, c:c + Q_BLOCK] for c in range(0, x.shape[1], Q_BLOCK)]

    map_rows = 2 * Q_BLOCK
    groups = [tuple(range(i, min(i + Q_GROUP, n_blocks))) for i in range(0, n_blocks, Q_GROUP)]

    def both_maps(bias):
        return jnp.concatenate([bias, bias], axis=0)

    def stacked_q(blocks):
        parts = []
        for i in blocks:
            qi = q_ref[i * Q_BLOCK:(i + 1) * Q_BLOCK, :]
            zero = jnp.zeros_like(qi)
            parts += [jnp.where(first_half, qi, zero), jnp.where(first_half, zero, qi)]
        return jnp.concatenate(parts, axis=0)

    def group_tiles(blocks):
        first, last = blocks[0], blocks[-1]
        tiles = []
        far_end = max(first - 1, 0)
        if far_end >= 1:
            tiles.append((0, Q_BLOCK, jnp.concatenate([inert_bias] * (2 * len(blocks)), axis=0), 0))
        kb = 1
        while kb < far_end:
            width = KEY_TILE if kb + KEY_TILE // Q_BLOCK <= far_end else Q_BLOCK
            tiles.append((kb * Q_BLOCK, width, None, 0))
            kb += width // Q_BLOCK
        for kb in range(far_end, last + 1):
            biases, first_row = [], None
            for r, qb in enumerate(blocks):
                if kb > qb:
                    continue
                if first_row is None:
                    first_row = r * map_rows
                if kb == qb:
                    bias = near_bias(0, kb)
                elif kb == qb - 1:
                    bias = near_bias(1, kb)
                else:
                    bias = inert_bias if kb == 0 else jnp.zeros_like(inert_bias)
                biases.append(both_maps(bias))
            tiles.append((kb * Q_BLOCK, Q_BLOCK, jnp.concatenate(biases, axis=0), first_row))
        return tiles

    def merge(acc, x, first_row, op):
        if acc is None:
            return x
        if first_row == 0:
            return op(acc, x)
        return jnp.concatenate([acc[:first_row], op(acc[first_row:], x)], axis=0)

    def sweep_scores(g):
        blocks = groups[g]
        rows = len(blocks) * map_rows
        q2 = stacked_q(blocks)
        m_acc = None
        for start, width, bias, first_row in group_tiles(blocks):
            s = _dot_nt(q2[first_row:], k_ref[start:start + width, :])
            if bias is not None:
                s = s + bias
            s_ref[g % 2, first_row:rows, start:start + width] = s
            for slab in slabs(s):
                m_acc = merge(m_acc, slab, first_row, jnp.maximum)
        return m_acc.max(axis=-1, keepdims=True)

    row_max = sweep_scores(0)
    for g, blocks in enumerate(groups):
        rows = len(blocks) * map_rows
        m = row_max
        if g + 1 < len(groups):
            row_max = sweep_scores(g + 1)
        l_acc = o_acc = None
        for start, width, _, first_row in group_tiles(blocks):
            p = jnp.exp2(s_ref[g % 2, first_row:rows, start:start + width] - m[first_row:])
            for slab in slabs(p):
                l_acc = merge(l_acc, slab, first_row, jnp.add)
            o_acc = merge(o_acc, _dot(p.astype(BF16), v_ref[start:start + width, :]), first_row, jnp.add)
        o2 = o_acc * (1.0 / l_acc.sum(axis=-1, keepdims=True))
        for r, i in enumerate(blocks):
            o = o2[r * map_rows:r * map_rows + Q_BLOCK] - lam * o2[r * map_rows + Q_BLOCK:(r + 1) * map_rows]
            o = _rms(o, w_ref[...]) * post
            o_ref[i * Q_BLOCK:(i + 1) * Q_BLOCK, :] = o.astype(o_ref.dtype)


def _attn(da, toe, cst, subln_w, batch, length):
    da3 = da.reshape(batch, length, DA_COLS)
    hw = DA_HEADS
    out = pl.pallas_call(
        functools.partial(_attn_kernel, n_blocks=length // Q_BLOCK),
        grid=(batch, DA_HEADS),
        in_specs=[
            pl.BlockSpec((None, length, DA_DV), lambda b, h: (b, 0, h)),
            pl.BlockSpec((None, length, DA_DV), lambda b, h: (b, 0, hw + h)),
            pl.BlockSpec((None, length, DA_DV), lambda b, h: (b, 0, 2 * hw + h)),
            pl.BlockSpec((None, 2, Q_BLOCK, Q_BLOCK), lambda b, h: (h, 0, 0, 0)),
            pl.BlockSpec((8, LANES), lambda b, h: (0, 0)),
            pl.BlockSpec((1, DA_DV), lambda b, h: (0, 0)),
        ],
        out_specs=pl.BlockSpec((None, length, DA_DV), lambda b, h: (b, 0, h)),
        out_shape=jax.ShapeDtypeStruct((batch, length, DA_HEADS * DA_DV), BF16),
        scratch_shapes=[pltpu.VMEM((2, Q_GROUP * 2 * Q_BLOCK, length), F32)],
        compiler_params=_params("parallel", "parallel"),
        name="diff_attn",
    )(da3, da3, da3, toe, cst, subln_w)
    return out.reshape(batch * length, DA_HEADS * DA_DV)


def _t5_bucket(dist):
    n = jnp.maximum(dist, 0)
    max_exact = REL_BUCKETS // 2
    nf = jnp.maximum(n, max_exact).astype(F32)
    large = max_exact + (jnp.log(nf / max_exact) / math.log(REL_MAX_DIST / max_exact)
                         * (REL_BUCKETS - max_exact)).astype(jnp.int32)
    large = jnp.minimum(large, REL_BUCKETS - 1)
    return jnp.where(n < max_exact, n, large)


def _attn_bias_tables(rel_bias):
    tab = rel_bias.astype(F32)
    qi = jnp.arange(Q_BLOCK, dtype=jnp.int32)[:, None]
    ki = jnp.arange(Q_BLOCK, dtype=jnp.int32)[None, :]

    def lookup(bucket):
        onehot = bucket[None, :, :, None] == jnp.arange(REL_BUCKETS, dtype=jnp.int32)
        return jnp.sum(jnp.where(onehot, tab.T[:, None, None, :], 0.0), axis=-1)

    far = tab[REL_BUCKETS - 1][:, None, None]
    diag = jnp.where((ki <= qi)[None], (lookup(_t5_bucket(qi - ki)) - far) * LOG2E, NEG)
    prev = (lookup(_t5_bucket(qi - ki + Q_BLOCK)) - far) * LOG2E
    return jnp.stack([diag, prev], axis=1)


def _out_proj_router_kernel(h_ref, ohg_ref, oda_ref, wo_ref, g_ref, router_ref, before_ref,
                            hn_ref, u_ref, route_ref, gate_ref, count_ref, seen_ref):
    @pl.when(pl.program_id(0) == 0)
    def _():
        seen_ref[...] = jnp.zeros_like(seen_ref)

    hn = (h_ref[...] + _dot(ohg_ref[...], wo_ref[:HG_WIDTH, :])
          + _dot(oda_ref[...], wo_ref[HG_WIDTH:, :]))
    hn_ref[...] = hn
    u = _rms(hn, g_ref[...])
    _store_token_major(u_ref, u)
    u_hi = u.astype(BF16)
    u_lo = (u - u_hi.astype(F32)).astype(BF16)
    logits = _dot(jnp.concatenate([u_hi, u_lo, u_hi], axis=1), router_ref[...])
    logits = logits.T[:N_EXPERTS]
    expert = lax.broadcasted_iota(jnp.int32, logits.shape, 0)

    def top(x):
        best = x.max(axis=0, keepdims=True)
        return best, jnp.where(x == best, expert, N_EXPERTS).min(axis=0, keepdims=True)

    l1, e1 = top(logits)
    l2, e2 = top(jnp.where(expert == e1, -jnp.inf, logits))
    w2 = jnp.exp(l2 - l1)
    g1 = 1.0 / (1.0 + w2)
    g2 = w2 / (1.0 + w2)

    pick1 = (expert == e1).astype(F32)
    pick2 = (expert == e2).astype(F32)
    picked = pick1 + pick2
    ahead = seen_ref[:, 0:1] + _dot(picked.astype(BF16), before_ref[...])
    rank1 = jnp.sum(pick1 * ahead, axis=0, keepdims=True).astype(jnp.int32)
    rank2 = jnp.sum(pick2 * ahead, axis=0, keepdims=True).astype(jnp.int32)
    seen = seen_ref[...] + jnp.sum(picked, axis=1, keepdims=True)
    seen_ref[...] = seen

    route_ref[...] = jnp.where(expert == 0, e1, jnp.where(expert == 1, e2, jnp.where(
        expert == 2, rank1, jnp.where(expert == 3, rank2, 0))))
    gate_ref[...] = jnp.where(expert == 0, g1, jnp.where(expert == 1, g2, 0.0))
    count_ref[...] = seen.astype(jnp.int32)


def _out_proj_router(h, o_hg, o_da, wo, layer, gain, router):
    n = h.shape[0]
    row = lambda i: (i, 0)
    full = lambda i: (0, 0)
    assert N_EXPERTS >= 2 * TOP_K
    lanes_of = lambda i: (0, i)
    before = jnp.asarray(np.triu(np.ones((ROW_TILE, ROW_TILE), np.float32), 1), BF16)
    return pl.pallas_call(
        _out_proj_router_kernel,
        grid=(n // ROW_TILE,),
        in_specs=[
            pl.BlockSpec((ROW_TILE, D_MODEL), row),
            pl.BlockSpec((ROW_TILE, HG_WIDTH), row),
            pl.BlockSpec((ROW_TILE, DA_HEADS * DA_DV), row),
            pl.BlockSpec((None,) + wo.shape[1:], lambda i: (layer, 0, 0)),
            pl.BlockSpec((1, D_MODEL), full),
            pl.BlockSpec(router.shape, full),
            pl.BlockSpec(before.shape, full),
        ],
        out_specs=[
            pl.BlockSpec((ROW_TILE, D_MODEL), row),
            pl.BlockSpec((ROW_TILE * TOKEN_ROWS, LANES), row),
            pl.BlockSpec((N_EXPERTS, ROW_TILE), lanes_of),
            pl.BlockSpec((N_EXPERTS, ROW_TILE), lanes_of),
            pl.BlockSpec((N_EXPERTS, LANES), full),
        ],
        out_shape=[
            jax.ShapeDtypeStruct((n, D_MODEL), F32),
            jax.ShapeDtypeStruct((n * TOKEN_ROWS, LANES), F32),
            jax.ShapeDtypeStruct((N_EXPERTS, n), jnp.int32),
            jax.ShapeDtypeStruct((N_EXPERTS, n), F32),
            jax.ShapeDtypeStruct((N_EXPERTS, LANES), jnp.int32),
        ],
        scratch_shapes=[pltpu.VMEM((N_EXPERTS, LANES), F32)],
        compiler_params=_params("arbitrary"),
        name="out_proj_router",
    )(h, o_hg, o_da, wo, gain, router, before)


CAST_STEPS = 64


def _dense_layer_kernel(h_ref, ohg_ref, oda_ref, wo_ref, g_ref, w1_ref, w3_ref, w2_ref, *rest):
    n_cast = (len(rest) - 1) // 2
    o_ref = rest[n_cast]
    hn = (h_ref[...] + _dot(ohg_ref[...], wo_ref[:HG_WIDTH, :])
          + _dot(oda_ref[...], wo_ref[HG_WIDTH:, :]))
    u = _rms(hn, g_ref[...]).astype(BF16)
    a = _dot(u, w1_ref[...])
    act = (_silu(a) * _dot(u, w3_ref[...])).astype(BF16)
    o_ref[...] = hn + _dot(act, w2_ref[...])
    for src_ref, dst_ref in zip(rest[:n_cast], rest[n_cast + 1:]):
        dst_ref[...] = src_ref[...].astype(BF16)


def _dense_layer(h, o_hg, o_da, wo, layer, gain, w1, w3, w2, idx, to_cast, cast_idx):
    n = h.shape[0]
    steps = n // ROW_TILE
    assert steps >= CAST_STEPS
    row = lambda i: (i, 0)
    slab = lambda i: (jnp.minimum(i, CAST_STEPS - 1), 0)

    def resident(w, index):
        return pl.BlockSpec((None,) + w.shape[1:], lambda i: (index, 0, 0),
                            pipeline_mode=pl.Buffered(1))

    cast_in, cast_in_specs, cast_out_specs, cast_out_shape = [], [], [], []
    for w in to_cast:
        n_moe, n_exp, rows, cols = w.shape
        assert (n_exp * rows) % CAST_STEPS == 0
        slab_rows = n_exp * rows // CAST_STEPS
        cast_in.append(w.reshape(n_moe, n_exp * rows, cols))
        cast_in_specs.append(pl.BlockSpec((None, slab_rows, cols),
                                          lambda i: (cast_idx, jnp.minimum(i, CAST_STEPS - 1), 0)))
        cast_out_specs.append(pl.BlockSpec((slab_rows, cols), slab))
        cast_out_shape.append(jax.ShapeDtypeStruct((n_exp * rows, cols), BF16))

    out = pl.pallas_call(
        _dense_layer_kernel,
        grid=(steps,),
        in_specs=[
            pl.BlockSpec((ROW_TILE, D_MODEL), row),
            pl.BlockSpec((ROW_TILE, HG_WIDTH), row),
            pl.BlockSpec((ROW_TILE, DA_HEADS * DA_DV), row),
            resident(wo, layer),
            pl.BlockSpec((1, D_MODEL), lambda i: (0, 0)),
            resident(w1, idx),
            resident(w3, idx),
            resident(w2, idx),
        ] + cast_in_specs,
        out_specs=[pl.BlockSpec((ROW_TILE, D_MODEL), row)] + cast_out_specs,
        out_shape=[jax.ShapeDtypeStruct((n, D_MODEL), F32)] + cast_out_shape,
        compiler_params=_params("arbitrary"),
        name="dense_layer",
    )(h, o_hg, o_da, wo, gain, w1, w3, w2, *cast_in)
    return out[0], [b.reshape(w.shape[1:]) for b, w in zip(out[1:], to_cast)]


DISPATCH_TILE = 1024
ISSUE_TOKENS = 8


def _token_copy(src_ref, src_tok, dst_ref, dst_tok, sem):
    src = pl.multiple_of(src_tok * TOKEN_ROWS, TOKEN_ROWS)
    dst = pl.multiple_of(dst_tok * TOKEN_ROWS, TOKEN_ROWS)
    return pltpu.make_async_copy(src_ref.at[pl.ds(src, TOKEN_ROWS), :],
                                 dst_ref.at[pl.ds(dst, TOKEN_ROWS), :], sem)


def _dispatch_kernel(dest_ref, empty_ref, u_ref, xs_ref, zero_ref, sem):
    step = pl.program_id(0)
    base = step * DISPATCH_TILE
    tile_rows = DISPATCH_TILE * TOKEN_ROWS

    def retire_tile():
        pltpu.make_async_copy(u_ref, xs_ref.at[pl.ds(0, tile_rows), :], sem).wait()

    @pl.when(step == 0)
    def _():
        zero_ref[...] = jnp.zeros_like(zero_ref)

        def clear(g, carry):
            slots = [empty_ref[g * ISSUE_TOKENS * TOP_K + j] for j in range(ISSUE_TOKENS * TOP_K)]
            for j, slot in enumerate(slots):
                _token_copy(zero_ref, 0, xs_ref, slot, sem).start(priority=j % 2)
            return carry

        n_empty = empty_ref.shape[0]
        lax.fori_loop(0, n_empty // (ISSUE_TOKENS * TOP_K), clear, 0)
        for _ in range(n_empty // DISPATCH_TILE):
            retire_tile()

    def start(g, carry):
        r0 = g * ISSUE_TOKENS
        slots = [dest_ref[(base + r0) * TOP_K + j] for j in range(ISSUE_TOKENS * TOP_K)]
        for j, slot in enumerate(slots):
            _token_copy(u_ref, r0 + j // TOP_K, xs_ref, slot, sem).start(priority=j % 2)
        return carry

    lax.fori_loop(0, DISPATCH_TILE // ISSUE_TOKENS, start, 0)
    for _ in range(TOP_K):
        retire_tile()


def _dispatch(dest, empty_slots, u_tm, n_slots):
    n = dest.shape[0] // TOP_K
    assert empty_slots.shape[0] % DISPATCH_TILE == 0
    return pl.pallas_call(
        _dispatch_kernel,
        grid_spec=pltpu.PrefetchScalarGridSpec(
            num_scalar_prefetch=2,
            grid=(n // DISPATCH_TILE,),
            in_specs=[pl.BlockSpec((DISPATCH_TILE * TOKEN_ROWS, LANES), lambda i, d, e: (i, 0))],
            out_specs=pl.BlockSpec(memory_space=pl.ANY),
            scratch_shapes=[pltpu.VMEM((TOKEN_ROWS, LANES), u_tm.dtype),
                            pltpu.SemaphoreType.DMA(())],
        ),
        out_shape=jax.ShapeDtypeStruct((n_slots * TOKEN_ROWS, LANES), u_tm.dtype),
        compiler_params=_params("arbitrary"),
        name="moe_dispatch",
    )(dest, empty_slots, u_tm)


def _expert_kernel(be_ref, nused_ref, xs_ref, w1_ref, w3_ref, w2_ref, y_ref):
    del be_ref
    blk = pl.program_id(0)

    @pl.when(blk < nused_ref[0])
    def _():
        x = _load_token_major(xs_ref, MOE_TILE).astype(BF16)
        acc = jnp.zeros((MOE_TILE, D_MODEL), F32)
        for c0 in range(0, D_FF_EXPERT, FF_CHUNK):
            a = _dot(x, w1_ref[:, c0:c0 + FF_CHUNK])
            act = (_silu(a) * _dot(x, w3_ref[:, c0:c0 + FF_CHUNK])).astype(BF16)
            acc = acc + _dot(act, w2_ref[c0:c0 + FF_CHUNK, :])
        _store_token_major(y_ref, acc)

    @pl.when(blk >= nused_ref[0])
    def _():
        y_ref[...] = jnp.zeros_like(y_ref)


def _experts(block_expert, n_used, xs, w1, w3, w2):
    n_blocks = xs.shape[0] // (MOE_TILE * TOKEN_ROWS)
    wmap = lambda i, be, nu: (be[i], 0, 0)
    slots = pl.BlockSpec((MOE_TILE * TOKEN_ROWS, LANES), lambda i, be, nu: (i, 0))
    return pl.pallas_call(
        _expert_kernel,
        grid_spec=pltpu.PrefetchScalarGridSpec(
            num_scalar_prefetch=2,
            grid=(n_blocks,),
            in_specs=[
                slots,
                pl.BlockSpec((None, D_MODEL, D_FF_EXPERT), wmap),
                pl.BlockSpec((None, D_MODEL, D_FF_EXPERT), wmap),
                pl.BlockSpec((None, D_FF_EXPERT, D_MODEL), wmap),
            ],
            out_specs=slots,
        ),
        out_shape=jax.ShapeDtypeStruct(xs.shape, F32),
        compiler_params=_params("arbitrary"),
        name="moe_experts",
    )(block_expert, n_used, xs, w1, w3, w2)


def _combine_kernel(dest_ref, h_ref, gate_ref, y_ref, o_ref, buf_ref, sem):
    step = pl.program_id(0)

    def issue(s, slot):
        base = s * ROW_TILE

        def body(g, carry):
            r0 = g * ISSUE_TOKENS
            slots = [dest_ref[(base + r0) * TOP_K + j] for j in range(ISSUE_TOKENS * TOP_K)]
            for j, src in enumerate(slots):
                _token_copy(y_ref, src, buf_ref.at[slot, j % TOP_K], r0 + j // TOP_K,
                            sem.at[slot]).start(priority=j % 2)
            return carry

        lax.fori_loop(0, ROW_TILE // ISSUE_TOKENS, body, 0)

    @pl.when(step == 0)
    def _():
        issue(0, 0)

    for slot in range(2):
        @pl.when(step % 2 == slot)
        def _():
            @pl.when(step + 1 < pl.num_programs(0))
            def _():
                issue(step + 1, 1 - slot)

            for k in range(TOP_K):
                pltpu.make_async_copy(y_ref.at[pl.ds(0, ROW_TILE * TOKEN_ROWS), :],
                                      buf_ref.at[slot, k], sem.at[slot]).wait()
            gate = gate_ref[...]
            o_ref[...] = (h_ref[...]
                          + gate[:, 0:1] * _load_token_major(buf_ref.at[slot, 0], ROW_TILE)
                          + gate[:, 1:2] * _load_token_major(buf_ref.at[slot, 1], ROW_TILE))


def _combine(dest, h, gate, y):
    n = h.shape[0]
    return pl.pallas_call(
        _combine_kernel,
        grid_spec=pltpu.PrefetchScalarGridSpec(
            num_scalar_prefetch=1,
            grid=(n // ROW_TILE,),
            in_specs=[
                pl.BlockSpec((ROW_TILE, D_MODEL), lambda i, d: (i, 0)),
                pl.BlockSpec((ROW_TILE, TOP_K), lambda i, d: (i, 0)),
                pl.BlockSpec(memory_space=pl.ANY),
            ],
            out_specs=pl.BlockSpec((ROW_TILE, D_MODEL), lambda i, d: (i, 0)),
            scratch_shapes=[pltpu.VMEM((2, TOP_K, ROW_TILE * TOKEN_ROWS, LANES), F32),
                            pltpu.SemaphoreType.DMA((2,))],
        ),
        out_shape=jax.ShapeDtypeStruct((n, D_MODEL), F32),
        compiler_params=_params("arbitrary"),
        name="moe_combine",
    )(dest, h, gate, y)


def _prefix_sum(x):
    k = x.shape[0]
    keep = (np.arange(k)[None, :] <= np.arange(k)[:, None]).reshape((k, k) + (1,) * (x.ndim - 1))
    return jnp.sum(jnp.where(keep, x[None], jnp.zeros_like(x[None])), axis=1)


def _slot_tables(route, counts, n_slots):
    n = route.shape[1]
    flat_e = route[0:TOP_K].T.reshape(n * TOP_K)
    rank = route[TOP_K:2 * TOP_K].T.reshape(n * TOP_K)
    counts = counts[:, 0]
    padded = (counts + MOE_TILE - 1) // MOE_TILE * MOE_TILE
    pad_end = _prefix_sum(padded)
    pad_start = pad_end - padded
    onehot = flat_e[:, None] == jnp.arange(N_EXPERTS, dtype=jnp.int32)[None, :]
    dest = (rank + jnp.sum(jnp.where(onehot, pad_start[None, :], 0), axis=-1)).astype(jnp.int32)
    n_blocks = n_slots // MOE_TILE
    block_first = jnp.arange(n_blocks, dtype=jnp.int32) * MOE_TILE
    block_expert = jnp.minimum(
        jnp.sum((pad_end[None, :] <= block_first[:, None]).astype(jnp.int32), axis=-1),
        N_EXPERTS - 1).astype(jnp.int32)
    n_used = (pad_end[-1:] // MOE_TILE).astype(jnp.int32)
    n_empty = n_slots - n * TOP_K
    gap_end = _prefix_sum(padded - counts)
    j = jnp.arange(n_empty, dtype=jnp.int32)
    owner = jnp.sum((j[:, None] >= gap_end[None, :]).astype(jnp.int32), axis=-1)
    first_empty = jnp.concatenate([pad_start + counts, pad_end[-1:]])
    gap_start = jnp.concatenate([jnp.zeros((1,), gap_end.dtype), gap_end])
    pick = owner[:, None] == jnp.arange(N_EXPERTS + 1, dtype=jnp.int32)[None, :]
    empty_slots = (j + jnp.sum(jnp.where(pick, (first_empty - gap_start)[None, :], 0), axis=-1))
    return dest, block_expert, n_used, empty_slots.astype(jnp.int32)


def _moe_ffn(u, route, gate, counts, w1, w3, w2):
    n = route.shape[1]
    n_slots = (n * TOP_K // MOE_TILE + N_EXPERTS) * MOE_TILE
    dest, block_expert, n_used, empty_slots = _slot_tables(route, counts, n_slots)
    xs = _dispatch(dest, empty_slots, u, n_slots)
    y = _experts(block_expert, n_used, xs, w1, w3, w2)
    return dest, gate[0:TOP_K].T, y


def _final_kernel(h_ref, g_ref, o_ref):
    o_ref[...] = _rms(h_ref[...], g_ref[...])


def _final_norm(h3, gain, seq):
    batch = h3.shape[0]
    lead_blocks = LEAD // Q_BLOCK
    return pl.pallas_call(
        _final_kernel,
        grid=(seq // Q_BLOCK,),
        in_specs=[
            pl.BlockSpec((batch, Q_BLOCK, D_MODEL), lambda j: (0, j + lead_blocks, 0)),
            pl.BlockSpec((1, D_MODEL), lambda j: (0, 0)),
        ],
        out_specs=pl.BlockSpec((batch, Q_BLOCK, D_MODEL), lambda j: (0, j, 0)),
        out_shape=jax.ShapeDtypeStruct((batch, seq, D_MODEL), F32),
        compiler_params=_params("parallel"),
        name="final_norm",
    )(h3, gain)


def kernel(x, meta, rel_bias, norm_mix, w_in, hg_lb_logits, hg_norm_w, da_lambda, da_subln_w, w_out, norm_ffn, dense_w1, dense_w3, dense_w2, moe_router, moe_w1, moe_w3, moe_w2, final_norm):
    batch, seq, d = x.shape
    length = LEAD + seq
    h = jnp.concatenate([
        jnp.zeros((batch, LEAD - N_META, d), x.dtype),
        jnp.broadcast_to(meta[None].astype(x.dtype), (batch, N_META, d)),
        x], axis=1).reshape(batch * length, d)

    toe = _attn_bias_tables(rel_bias)
    lb_cum = _prefix_sum(jax.nn.softmax(hg_lb_logits.astype(F32), axis=0))
    lb_all = jnp.clip(lb_cum - lb_cum[0:1], 0.0, LB_MAX)
    log_lb = jnp.log(lb_all)
    log_1m_lb = jnp.log1p(-lb_all)

    w_in_b, w_out_b = w_in.astype(BF16), w_out.astype(BF16)
    dense_b = [w.astype(BF16) for w in (dense_w1, dense_w3, dense_w2)]
    assert DEPTH % 2 == 0

    unsummed = None
    for l in range(DEPTH):
        if unsummed is None:
            hg, da = _mix_in(h, norm_mix[l][None], w_in_b, l)
        else:
            h, hg, da = _mix_in_combine(*unsummed, norm_mix[l][None], w_in_b, l)
            unsummed = None
        lam_init = 0.8 - 0.6 * math.exp(-0.3 * l)
        lv = da_lambda[l].astype(F32)
        lam = jnp.exp(jnp.sum(lv[0] * lv[1])) - jnp.exp(jnp.sum(lv[2] * lv[3])) + lam_init
        cst = jnp.zeros((8, LANES), F32).at[0].set(lam).at[1].set(1.0 - lam_init)
        o_hg = _hgrn(hg, log_lb[l][None], log_1m_lb[l][None], hg_norm_w[l][None], batch, length)
        o_da = _attn(da, toe, cst, da_subln_w[l][None], batch, length)
        i = l // 2
        if l % 2 == 0:
            h, moe_b = _dense_layer(h, o_hg, o_da, w_out_b, l, norm_ffn[l][None], *dense_b, i,
                                    (moe_w1, moe_w3, moe_w2), i)
        else:
            router = jnp.zeros((d, LANES), F32).at[:, :N_EXPERTS].set(moe_router[i].astype(F32))
            r_hi = router.astype(BF16)
            r_lo = (router - r_hi.astype(F32)).astype(BF16)
            router = jnp.concatenate([r_hi, r_hi, r_lo], axis=0)
            hn, u, route, gate, counts = _out_proj_router(h, o_hg, o_da, w_out_b, l,
                                                          norm_ffn[l][None], router)
            dest, gate, y = _moe_ffn(u, route, gate, counts, *moe_b)
            if l + 1 < DEPTH:
                unsummed = (dest, hn, gate, y)
            else:
                h = _combine(dest, hn, gate, y)
    return _final_norm(h.reshape(batch, length, d), final_norm[None], seq)
```

```python
import functools
import math

import jax
import jax.numpy as jnp
import numpy as np
from jax import lax
from jax.experimental import pallas as pl
from jax.experimental.pallas import tpu as pltpu

D_MODEL = 1024
DEPTH = 4
N_META = 16
LEAD = 128
HG_WIDTH = 512
HG_HEADS = 4
HG_D = 128
HG_CHUNK = 64
DA_HEADS = 4
DA_DQK = 64
DA_DV = 128
Q_BLOCK = 128
KEY_TILE = 2 * Q_BLOCK
Q_GROUP = 2
LOG2E = math.log2(math.e)
Q_SCALE = DA_DQK ** -0.5 * LOG2E
REL_BUCKETS = 32
REL_MAX_DIST = 128
N_EXPERTS = 8
TOP_K = 2
D_FF_EXPERT = 3584
EPS = 1e-6
NEG = -1e30
LB_MAX = 0.999
HG_COLS = 4 * HG_WIDTH
DA_COLS = 3 * DA_HEADS * DA_DV
W_IN_COLS = HG_COLS + DA_COLS

LANES = 128
VMEM_LIMIT = 56 * 1024 * 1024

ROW_TILE = 256
MIX_TILE = 512
HG_TILE = 128
MOE_TILE = 256
FF_CHUNK = 512

F32 = jnp.float32
BF16 = jnp.bfloat16


def _params(*sem):
    return pltpu.CompilerParams(dimension_semantics=sem, vmem_limit_bytes=VMEM_LIMIT)


def _dot(a, b):
    return jnp.dot(a, b, preferred_element_type=F32)


def _dot_nt(a, b):
    return lax.dot_general(a, b, (((1,), (1,)), ((), ())), preferred_element_type=F32)


def _dot_tn(a, b):
    return lax.dot_general(a, b, (((0,), (0,)), ((), ())), preferred_element_type=F32)


def _rms(x, gain):
    return x * lax.rsqrt(jnp.mean(x * x, axis=-1, keepdims=True) + EPS) * gain


TOKEN_ROWS = D_MODEL // LANES


def _store_token_major(ref, x):
    t = x.shape[0]
    for s in range(TOKEN_ROWS):
        ref[pl.ds(s, t, stride=TOKEN_ROWS), :] = x[:, s * LANES:(s + 1) * LANES]


def _load_token_major(ref, t):
    return jnp.concatenate(
        [ref[pl.ds(s, t, stride=TOKEN_ROWS), :] for s in range(TOKEN_ROWS)], axis=1)


def _silu(x):
    return x * (0.5 * jnp.tanh(0.5 * x) + 0.5)


def _mix_in_kernel(x_ref, g_ref, w_ref, hg_ref, da_ref):
    u = _rms(x_ref[...], g_ref[...]).astype(BF16)
    hg_ref[...] = _dot(u, w_ref[:, :HG_COLS])
    n_q = DA_HEADS * 2 * DA_DQK
    da_ref[:, :n_q] = (_dot(u, w_ref[:, HG_COLS:HG_COLS + n_q]) * Q_SCALE).astype(BF16)
    da_ref[:, n_q:] = _dot(u, w_ref[:, HG_COLS + n_q:]).astype(BF16)


def _mix_in(h, gain, w, layer):
    n = h.shape[0]
    return pl.pallas_call(
        _mix_in_kernel,
        grid=(n // MIX_TILE,),
        in_specs=[
            pl.BlockSpec((MIX_TILE, D_MODEL), lambda i: (i, 0)),
            pl.BlockSpec((1, D_MODEL), lambda i: (0, 0)),
            pl.BlockSpec((None, D_MODEL, W_IN_COLS), lambda i: (layer, 0, 0),
                         pipeline_mode=pl.Buffered(1)),
        ],
        out_specs=[
            pl.BlockSpec((MIX_TILE, HG_COLS), lambda i: (i, 0)),
            pl.BlockSpec((MIX_TILE, DA_COLS), lambda i: (i, 0)),
        ],
        out_shape=[
            jax.ShapeDtypeStruct((n, HG_COLS), F32),
            jax.ShapeDtypeStruct((n, DA_COLS), BF16),
        ],
        compiler_params=_params("parallel"),
        name="mix_in",
    )(h, gain, w)


def _mix_in_combine_kernel(dest_ref, hn_ref, gate_ref, y_ref, g_ref, w_ref,
                           h_ref, hg_ref, da_ref, buf_ref, sem):
    step = pl.program_id(0)
    last = pl.num_programs(0) - 1
    slot = step % 2
    tile_rows = MIX_TILE * TOKEN_ROWS

    def start_gathers(s, to_slot, rows):
        base = s * MIX_TILE
        slots = [dest_ref[(base + r) * TOP_K + k] for r in rows for k in range(TOP_K)]
        for j, src in enumerate(slots):
            r, k = rows[j // TOP_K], j % TOP_K
            _token_copy(y_ref, src, buf_ref.at[to_slot, k], r, sem.at[to_slot]).start(priority=j % 2)

    def wait_gathers(of_slot):
        for k in range(TOP_K):
            pltpu.make_async_copy(y_ref.at[pl.ds(0, tile_rows), :], buf_ref.at[of_slot, k],
                                  sem.at[of_slot]).wait()

    @pl.when(step == 0)
    def _():
        def first(g, carry):
            start_gathers(0, 0, [g * ISSUE_TOKENS + r for r in range(ISSUE_TOKENS)])
            return carry

        lax.fori_loop(0, MIX_TILE // ISSUE_TOKENS, first, 0)

    wait_gathers(slot)
    gate = gate_ref[...]
    h = (hn_ref[...] + gate[:, 0:1] * _load_token_major(buf_ref.at[slot, 0], MIX_TILE)
         + gate[:, 1:2] * _load_token_major(buf_ref.at[slot, 1], MIX_TILE))
    h_ref[...] = h
    u = _rms(h, g_ref[...]).astype(BF16)

    nxt = jnp.minimum(step + 1, last)
    n_q = DA_HEADS * 2 * DA_DQK
    groups = [(c, c + HG_WIDTH) for c in range(0, HG_COLS, HG_WIDTH)]
    groups += [(HG_COLS, HG_COLS + n_q), (HG_COLS + n_q, W_IN_COLS)]
    per_group = -(-MIX_TILE // len(groups))
    for gi, (c0, c1) in enumerate(groups):
        proj = _dot(u, w_ref[:, c0:c1])
        if c1 <= HG_COLS:
            hg_ref[:, c0:c1] = proj
        elif c0 == HG_COLS:
            da_ref[:, :n_q] = (proj * Q_SCALE).astype(BF16)
        else:
            da_ref[:, n_q:] = proj.astype(BF16)
        rows = list(range(gi * per_group, min((gi + 1) * per_group, MIX_TILE)))
        for r0 in range(0, len(rows), ISSUE_TOKENS):
            start_gathers(nxt, 1 - slot, rows[r0:r0 + ISSUE_TOKENS])

    @pl.when(step == last)
    def _():
        wait_gathers(1 - slot)


def _mix_in_combine(dest, hn, gate, y, gain, w, layer):
    n = hn.shape[0]
    row = lambda i, d: (i, 0)
    return pl.pallas_call(
        _mix_in_combine_kernel,
        grid_spec=pltpu.PrefetchScalarGridSpec(
            num_scalar_prefetch=1,
            grid=(n // MIX_TILE,),
            in_specs=[
                pl.BlockSpec((MIX_TILE, D_MODEL), row),
                pl.BlockSpec((MIX_TILE, TOP_K), row),
                pl.BlockSpec(memory_space=pl.ANY),
                pl.BlockSpec((1, D_MODEL), lambda i, d: (0, 0)),
                pl.BlockSpec((None, D_MODEL, W_IN_COLS), lambda i, d: (layer, 0, 0),
                             pipeline_mode=pl.Buffered(1)),
            ],
            out_specs=[
                pl.BlockSpec((MIX_TILE, D_MODEL), row),
                pl.BlockSpec((MIX_TILE, HG_COLS), row),
                pl.BlockSpec((MIX_TILE, DA_COLS), row),
            ],
            scratch_shapes=[pltpu.VMEM((2, TOP_K, MIX_TILE * TOKEN_ROWS, LANES), F32),
                            pltpu.SemaphoreType.DMA((2,))],
        ),
        out_shape=[
            jax.ShapeDtypeStruct((n, D_MODEL), F32),
            jax.ShapeDtypeStruct((n, HG_COLS), F32),
            jax.ShapeDtypeStruct((n, DA_COLS), BF16),
        ],
        compiler_params=_params("arbitrary"),
        name="mix_in_combine",
    )(dest, hn, gate, y, gain, w)


HG_LEVELS = (32, 16, 8, 4, 2, 1)
N_SUMS = len(HG_LEVELS) + 2


def _hgrn_consts():
    c = HG_CHUNK
    t = np.arange(c)[:, None]
    j = np.arange(c)[None, :]
    sums = np.zeros((N_SUMS, c, c), np.float32)
    masks = np.zeros((len(HG_LEVELS) + 1, c, c), np.float32)
    sums[0] = j <= t
    masks[0] = np.eye(c)
    for li, w in enumerate(HG_LEVELS, start=1):
        ref = (t // (2 * w)) * (2 * w) + w
        sums[li] = np.where(t >= ref, (j > ref) & (j <= t), (j > t) & (j <= ref))
        masks[li] = (t // (2 * w) == j // (2 * w)) & (t % (2 * w) >= w) & (j % (2 * w) < w)
    sums[N_SUMS - 1] = j > t
    sums = sums.reshape(N_SUMS * c, c)
    return np.concatenate([sums, sums], axis=1), masks


_HG_SUMS, _HG_MASKS = _hgrn_consts()


def _hgrn_kernel(hg_ref, loga_ref, log1m_ref, nw_ref, sums_ref, masks_ref, o_ref, state_ref):
    c_idx = pl.program_id(1)

    @pl.when(c_idx == 0)
    def _():
        state_ref[...] = jnp.zeros_like(state_ref)

    C = HG_CHUNK
    W = HG_WIDTH
    n_chunks = HG_TILE // C
    sums = sums_ref[...]
    nw = nw_ref[...]
    row_idx = c_idx * HG_TILE + lax.broadcasted_iota(jnp.int32, (HG_TILE, 1), 0)
    valid = row_idx >= (LEAD - N_META)
    step = lax.broadcasted_iota(jnp.int32, (C, 1), 0)
    head_cols = [slice(hd * HG_D, (hd + 1) * HG_D) for hd in range(HG_HEADS)]

    f = hg_ref[:, W:2 * W]
    qf = _silu(hg_ref[:, 0:W])
    ls = jnp.minimum(f, 0.0) - jnp.log(1.0 + jnp.exp(-jnp.abs(f)))
    cc = log1m_ref[...] + ls
    loga = loga_ref[...]
    lf = jnp.maximum(loga, cc) + jnp.log(1.0 + jnp.exp(-jnp.abs(loga - cc)))
    kk = jnp.exp(cc - f)
    lf = jnp.where(valid, lf, 0.0)
    kk = jnp.where(valid, kk, 0.0)
    lf2 = lf * LOG2E
    lf_hi = lf2.astype(BF16)
    lf_lo = (lf2 - lf_hi.astype(F32)).astype(BF16)
    vb = hg_ref[:, 2 * W:3 * W].astype(BF16)
    gate = _silu(hg_ref[:, 3 * W:4 * W])

    def side_by_side(x):
        return jnp.concatenate([x[ch * C:(ch + 1) * C] for ch in range(n_chunks)], axis=1)

    qf_w, kk_w = side_by_side(qf), side_by_side(kk)
    qb_w, kb_w = qf_w.astype(BF16), kk_w.astype(BF16)
    e = jnp.exp2(_dot(sums, jnp.concatenate([side_by_side(lf_hi), side_by_side(lf_lo)], axis=0)))
    e_b = e[0:C]
    decay_end = e_b[C - 1:C, :]
    q_in = (qf_w * e_b).astype(BF16)
    k_out = (kk_w * e[(N_SUMS - 1) * C:N_SUMS * C]).astype(BF16)
    z = [(jnp.where((step & w) != 0, qf_w, kk_w) * e[li * C:(li + 1) * C]).astype(BF16)
         for li, w in enumerate(HG_LEVELS, start=1)]
    unit_cols = [[slice(ch * W + hd * HG_D, ch * W + (hd + 1) * HG_D) for hd in range(HG_HEADS)]
                 for ch in range(n_chunks)]
    scores = []
    for ch in range(n_chunks):
        scores.append([])
        for cols in unit_cols[ch]:
            s = masks_ref[0] * _dot_nt(qb_w[:, cols], kb_w[:, cols])
            for li in range(1, len(HG_LEVELS) + 1):
                zl = z[li - 1][:, cols]
                s += masks_ref[li] * _dot_nt(zl, zl)
            scores[ch].append(s.astype(BF16))

    for ch in range(n_chunks):
        rows = slice(ch * C, (ch + 1) * C)
        for hd, cols in enumerate(head_cols):
            wide = unit_cols[ch][hd]
            st = state_ref[hd]
            v_h = vb[rows, cols]
            o = _dot_nt(q_in[:, wide], st.astype(BF16)) + _dot(scores[ch][hd], v_h)
            state_ref[hd] = st * decay_end[:, wide] + _dot_tn(v_h, k_out[:, wide])
            o = _rms(o, nw) * gate[rows, cols]
            o_ref[rows, cols] = o.astype(o_ref.dtype)


def _hgrn(hg, loga, log1m, norm_w, batch, length):
    hg3 = hg.reshape(batch, length, HG_COLS)
    out = pl.pallas_call(
        _hgrn_kernel,
        grid=(batch, length // HG_TILE),
        in_specs=[
            pl.BlockSpec((None, HG_TILE, HG_COLS), lambda b, c: (b, c, 0)),
            pl.BlockSpec((1, HG_WIDTH), lambda b, c: (0, 0)),
            pl.BlockSpec((1, HG_WIDTH), lambda b, c: (0, 0)),
            pl.BlockSpec((1, HG_D), lambda b, c: (0, 0)),
            pl.BlockSpec(_HG_SUMS.shape, lambda b, c: (0, 0)),
            pl.BlockSpec(_HG_MASKS.shape, lambda b, c: (0, 0, 0)),
        ],
        out_specs=pl.BlockSpec((None, HG_TILE, HG_WIDTH), lambda b, c: (b, c, 0)),
        out_shape=jax.ShapeDtypeStruct((batch, length, HG_WIDTH), BF16),
        scratch_shapes=[pltpu.VMEM((HG_HEADS, HG_D, HG_D), F32)],
        compiler_params=_params("parallel", "arbitrary"),
        name="hgrn2",
    )(hg3, loga, log1m, norm_w, jnp.asarray(_HG_SUMS, BF16), jnp.asarray(_HG_MASKS, F32))
    return out.reshape(batch * length, HG_WIDTH)


def _attn_kernel(q_ref, k_ref, v_ref, toe_ref, cst_ref, w_ref, o_ref, s_ref, *, n_blocks):
    lam = cst_ref[0:1, 0:1]
    post = cst_ref[1:2, :]
    lane = lax.broadcasted_iota(jnp.int32, (Q_BLOCK, Q_BLOCK), 1)
    first_half = lane < DA_DQK
    key_ok0 = lane >= (LEAD - N_META)
    inert_bias = jnp.where(key_ok0, 0.0, NEG)

    def near_bias(kind, kb):
        bias = toe_ref[kind]
        if kb == 0:
            bias = jnp.where(key_ok0, bias, NEG)
        return bias

    def slabs(x):
        return [x[:, c:c + Q_BLOCK] for c in range(0, x.shape[1], Q_BLOCK)]

    map_rows = 2 * Q_BLOCK
    groups = [tuple(range(i, min(i + Q_GROUP, n_blocks))) for i in range(0, n_blocks, Q_GROUP)]

    def both_maps(bias):
        return jnp.concatenate([bias, bias], axis=0)

    def stacked_q(blocks):
        parts = []
        for i in blocks:
            qi = q_ref[i * Q_BLOCK:(i + 1) * Q_BLOCK, :]
            zero = jnp.zeros_like(qi)
            parts += [jnp.where(first_half, qi, zero), jnp.where(first_half, zero, qi)]
        return jnp.concatenate(parts, axis=0)

    def group_tiles(blocks):
        first, last = blocks[0], blocks[-1]
        tiles = []
        far_end = max(first - 1, 0)
        if far_end >= 1:
            tiles.append((0, Q_BLOCK, jnp.concatenate([inert_bias] * (2 * len(blocks)), axis=0), 0))
        kb = 1
        while kb < far_end:
            width = KEY_TILE if kb + KEY_TILE // Q_BLOCK <= far_end else Q_BLOCK
            tiles.append((kb * Q_BLOCK, width, None, 0))
            kb += width // Q_BLOCK
        for kb in range(far_end, last + 1):
            biases, first_row = [], None
            for r, qb in enumerate(blocks):
                if kb > qb:
                    continue
                if first_row is None:
                    first_row = r * map_rows
                if kb == qb:
                    bias = near_bias(0, kb)
                elif kb == qb - 1:
                    bias = near_bias(1, kb)
                else:
                    bias = inert_bias if kb == 0 else jnp.zeros_like(inert_bias)
                biases.append(both_maps(bias))
            tiles.append((kb * Q_BLOCK, Q_BLOCK, jnp.concatenate(biases, axis=0), first_row))
        return tiles

    def merge(acc, x, first_row, op):
        if acc is None:
            return x
        if first_row == 0:
            return op(acc, x)
        return jnp.concatenate([acc[:first_row], op(acc[first_row:], x)], axis=0)

    def sweep_scores(g):
        blocks = groups[g]
        rows = len(blocks) * map_rows
        q2 = stacked_q(blocks)
        m_acc = None
        for start, width, bias, first_row in group_tiles(blocks):
            s = _dot_nt(q2[first_row:], k_ref[start:start + width, :])
            if bias is not None:
                s = s + bias
            s_ref[g % 2, first_row:rows, start:start + width] = s
            for slab in slabs(s):
                m_acc = merge(m_acc, slab, first_row, jnp.maximum)
        return m_acc.max(axis=-1, keepdims=True)

    row_max = sweep_scores(0)
    for g, blocks in enumerate(groups):
        rows = len(blocks) * map_rows
        m = row_max
        if g + 1 < len(groups):
            row_max = sweep_scores(g + 1)
        l_acc = o_acc = None
        for start, width, _, first_row in group_tiles(blocks):
            p = jnp.exp2(s_ref[g % 2, first_row:rows, start:start + width] - m[first_row:])
            for slab in slabs(p):
                l_acc = merge(l_acc, slab, first_row, jnp.add)
            o_acc = merge(o_acc, _dot(p.astype(BF16), v_ref[start:start + width, :]), first_row, jnp.add)
        o2 = o_acc * (1.0 / l_acc.sum(axis=-1, keepdims=True))
        for r, i in enumerate(blocks):
            o = o2[r * map_rows:r * map_rows + Q_BLOCK] - lam * o2[r * map_rows + Q_BLOCK:(r + 1) * map_rows]
            o = _rms(o, w_ref[...]) * post
            o_ref[i * Q_BLOCK:(i + 1) * Q_BLOCK, :] = o.astype(o_ref.dtype)


def _attn(da, toe, cst, subln_w, batch, length):
    da3 = da.reshape(batch, length, DA_COLS)
    hw = DA_HEADS
    out = pl.pallas_call(
        functools.partial(_attn_kernel, n_blocks=length // Q_BLOCK),
        grid=(batch, DA_HEADS),
        in_specs=[
            pl.BlockSpec((None, length, DA_DV), lambda b, h: (b, 0, h)),
            pl.BlockSpec((None, length, DA_DV), lambda b, h: (b, 0, hw + h)),
            pl.BlockSpec((None, length, DA_DV), lambda b, h: (b, 0, 2 * hw + h)),
            pl.BlockSpec((None, 2, Q_BLOCK, Q_BLOCK), lambda b, h: (h, 0, 0, 0)),
            pl.BlockSpec((8, LANES), lambda b, h: (0, 0)),
            pl.BlockSpec((1, DA_DV), lambda b, h: (0, 0)),
        ],
        out_specs=pl.BlockSpec((None, length, DA_DV), lambda b, h: (b, 0, h)),
        out_shape=jax.ShapeDtypeStruct((batch, length, DA_HEADS * DA_DV), BF16),
        scratch_shapes=[pltpu.VMEM((2, Q_GROUP * 2 * Q_BLOCK, length), F32)],
        compiler_params=_params("parallel", "parallel"),
        name="diff_attn",
    )(da3, da3, da3, toe, cst, subln_w)
    return out.reshape(batch * length, DA_HEADS * DA_DV)


def _t5_bucket(dist):
    n = jnp.maximum(dist, 0)
    max_exact = REL_BUCKETS // 2
    nf = jnp.maximum(n, max_exact).astype(F32)
    large = max_exact + (jnp.log(nf / max_exact) / math.log(REL_MAX_DIST / max_exact)
                         * (REL_BUCKETS - max_exact)).astype(jnp.int32)
    large = jnp.minimum(large, REL_BUCKETS - 1)
    return jnp.where(n < max_exact, n, large)


def _attn_bias_tables(rel_bias):
    tab = rel_bias.astype(F32)
    qi = jnp.arange(Q_BLOCK, dtype=jnp.int32)[:, None]
    ki = jnp.arange(Q_BLOCK, dtype=jnp.int32)[None, :]

    def lookup(bucket):
        onehot = bucket[None, :, :, None] == jnp.arange(REL_BUCKETS, dtype=jnp.int32)
        return jnp.sum(jnp.where(onehot, tab.T[:, None, None, :], 0.0), axis=-1)

    far = tab[REL_BUCKETS - 1][:, None, None]
    diag = jnp.where((ki <= qi)[None], (lookup(_t5_bucket(qi - ki)) - far) * LOG2E, NEG)
    prev = (lookup(_t5_bucket(qi - ki + Q_BLOCK)) - far) * LOG2E
    return jnp.stack([diag, prev], axis=1)


def _out_proj_router_kernel(h_ref, ohg_ref, oda_ref, wo_ref, g_ref, router_ref, before_ref,
                            hn_ref, u_ref, route_ref, gate_ref, count_ref, seen_ref):
    @pl.when(pl.program_id(0) == 0)
    def _():
        seen_ref[...] = jnp.zeros_like(seen_ref)

    hn = (h_ref[...] + _dot(ohg_ref[...], wo_ref[:HG_WIDTH, :])
          + _dot(oda_ref[...], wo_ref[HG_WIDTH:, :]))
    hn_ref[...] = hn
    u = _rms(hn, g_ref[...])
    _store_token_major(u_ref, u)
    u_hi = u.astype(BF16)
    u_lo = (u - u_hi.astype(F32)).astype(BF16)
    logits = _dot(jnp.concatenate([u_hi, u_lo, u_hi], axis=1), router_ref[...])
    logits = logits.T[:N_EXPERTS]
    expert = lax.broadcasted_iota(jnp.int32, logits.shape, 0)

    def top(x):
        best = x.max(axis=0, keepdims=True)
        return best, jnp.where(x == best, expert, N_EXPERTS).min(axis=0, keepdims=True)

    l1, e1 = top(logits)
    l2, e2 = top(jnp.where(expert == e1, -jnp.inf, logits))
    w2 = jnp.exp(l2 - l1)
    g1 = 1.0 / (1.0 + w2)
    g2 = w2 / (1.0 + w2)

    pick1 = (expert == e1).astype(F32)
    pick2 = (expert == e2).astype(F32)
    picked = pick1 + pick2
    ahead = seen_ref[:, 0:1] + _dot(picked.astype(BF16), before_ref[...])
    rank1 = jnp.sum(pick1 * ahead, axis=0, keepdims=True).astype(jnp.int32)
    rank2 = jnp.sum(pick2 * ahead, axis=0, keepdims=True).astype(jnp.int32)
    seen = seen_ref[...] + jnp.sum(picked, axis=1, keepdims=True)
    seen_ref[...] = seen

    route_ref[...] = jnp.where(expert == 0, e1, jnp.where(expert == 1, e2, jnp.where(
        expert == 2, rank1, jnp.where(expert == 3, rank2, 0))))
    gate_ref[...] = jnp.where(expert == 0, g1, jnp.where(expert == 1, g2, 0.0))
    count_ref[...] = seen.astype(jnp.int32)


def _out_proj_router(h, o_hg, o_da, wo, layer, gain, router):
    n = h.shape[0]
    row = lambda i: (i, 0)
    full = lambda i: (0, 0)
    assert N_EXPERTS >= 2 * TOP_K
    lanes_of = lambda i: (0, i)
    before = jnp.asarray(np.triu(np.ones((ROW_TILE, ROW_TILE), np.float32), 1), BF16)
    return pl.pallas_call(
        _out_proj_router_kernel,
        grid=(n // ROW_TILE,),
        in_specs=[
            pl.BlockSpec((ROW_TILE, D_MODEL), row),
            pl.BlockSpec((ROW_TILE, HG_WIDTH), row),
            pl.BlockSpec((ROW_TILE, DA_HEADS * DA_DV), row),
            pl.BlockSpec((None,) + wo.shape[1:], lambda i: (layer, 0, 0)),
            pl.BlockSpec((1, D_MODEL), full),
            pl.BlockSpec(router.shape, full),
            pl.BlockSpec(before.shape, full),
        ],
        out_specs=[
            pl.BlockSpec((ROW_TILE, D_MODEL), row),
            pl.BlockSpec((ROW_TILE * TOKEN_ROWS, LANES), row),
            pl.BlockSpec((N_EXPERTS, ROW_TILE), lanes_of),
            pl.BlockSpec((N_EXPERTS, ROW_TILE), lanes_of),
            pl.BlockSpec((N_EXPERTS, LANES), full),
        ],
        out_shape=[
            jax.ShapeDtypeStruct((n, D_MODEL), F32),
            jax.ShapeDtypeStruct((n * TOKEN_ROWS, LANES), F32),
            jax.ShapeDtypeStruct((N_EXPERTS, n), jnp.int32),
            jax.ShapeDtypeStruct((N_EXPERTS, n), F32),
            jax.ShapeDtypeStruct((N_EXPERTS, LANES), jnp.int32),
        ],
        scratch_shapes=[pltpu.VMEM((N_EXPERTS, LANES), F32)],
        compiler_params=_params("arbitrary"),
        name="out_proj_router",
    )(h, o_hg, o_da, wo, gain, router, before)


CAST_STEPS = 64


def _dense_layer_kernel(h_ref, ohg_ref, oda_ref, wo_ref, g_ref, w1_ref, w3_ref, w2_ref, *rest):
    n_cast = (len(rest) - 1) // 2
    o_ref = rest[n_cast]
    hn = (h_ref[...] + _dot(ohg_ref[...], wo_ref[:HG_WIDTH, :])
          + _dot(oda_ref[...], wo_ref[HG_WIDTH:, :]))
    u = _rms(hn, g_ref[...]).astype(BF16)
    a = _dot(u, w1_ref[...])
    act = (_silu(a) * _dot(u, w3_ref[...])).astype(BF16)
    o_ref[...] = hn + _dot(act, w2_ref[...])
    for src_ref, dst_ref in zip(rest[:n_cast], rest[n_cast + 1:]):
        dst_ref[...] = src_ref[...].astype(BF16)


def _dense_layer(h, o_hg, o_da, wo, layer, gain, w1, w3, w2, idx, to_cast, cast_idx):
    n = h.shape[0]
    steps = n // ROW_TILE
    assert steps >= CAST_STEPS
    row = lambda i: (i, 0)
    slab = lambda i: (jnp.minimum(i, CAST_STEPS - 1), 0)

    def resident(w, index):
        return pl.BlockSpec((None,) + w.shape[1:], lambda i: (index, 0, 0),
                            pipeline_mode=pl.Buffered(1))

    cast_in, cast_in_specs, cast_out_specs, cast_out_shape = [], [], [], []
    for w in to_cast:
        n_moe, n_exp, rows, cols = w.shape
        assert (n_exp * rows) % CAST_STEPS == 0
        slab_rows = n_exp * rows // CAST_STEPS
        cast_in.append(w.reshape(n_moe, n_exp * rows, cols))
        cast_in_specs.append(pl.BlockSpec((None, slab_rows, cols),
                                          lambda i: (cast_idx, jnp.minimum(i, CAST_STEPS - 1), 0)))
        cast_out_specs.append(pl.BlockSpec((slab_rows, cols), slab))
        cast_out_shape.append(jax.ShapeDtypeStruct((n_exp * rows, cols), BF16))

    out = pl.pallas_call(
        _dense_layer_kernel,
        grid=(steps,),
        in_specs=[
            pl.BlockSpec((ROW_TILE, D_MODEL), row),
            pl.BlockSpec((ROW_TILE, HG_WIDTH), row),
            pl.BlockSpec((ROW_TILE, DA_HEADS * DA_DV), row),
            resident(wo, layer),
            pl.BlockSpec((1, D_MODEL), lambda i: (0, 0)),
            resident(w1, idx),
            resident(w3, idx),
            resident(w2, idx),
        ] + cast_in_specs,
        out_specs=[pl.BlockSpec((ROW_TILE, D_MODEL), row)] + cast_out_specs,
        out_shape=[jax.ShapeDtypeStruct((n, D_MODEL), F32)] + cast_out_shape,
        compiler_params=_params("arbitrary"),
        name="dense_layer",
    )(h, o_hg, o_da, wo, gain, w1, w3, w2, *cast_in)
    return out[0], [b.reshape(w.shape[1:]) for b, w in zip(out[1:], to_cast)]


DISPATCH_TILE = 1024
ISSUE_TOKENS = 8


def _token_copy(src_ref, src_tok, dst_ref, dst_tok, sem):
    src = pl.multiple_of(src_tok * TOKEN_ROWS, TOKEN_ROWS)
    dst = pl.multiple_of(dst_tok * TOKEN_ROWS, TOKEN_ROWS)
    return pltpu.make_async_copy(src_ref.at[pl.ds(src, TOKEN_ROWS), :],
                                 dst_ref.at[pl.ds(dst, TOKEN_ROWS), :], sem)


def _dispatch_kernel(dest_ref, empty_ref, u_ref, xs_ref, zero_ref, sem):
    step = pl.program_id(0)
    base = step * DISPATCH_TILE
    tile_rows = DISPATCH_TILE * TOKEN_ROWS

    def retire_tile():
        pltpu.make_async_copy(u_ref, xs_ref.at[pl.ds(0, tile_rows), :], sem).wait()

    @pl.when(step == 0)
    def _():
        zero_ref[...] = jnp.zeros_like(zero_ref)

        def clear(g, carry):
            slots = [empty_ref[g * ISSUE_TOKENS * TOP_K + j] for j in range(ISSUE_TOKENS * TOP_K)]
            for j, slot in enumerate(slots):
                _token_copy(zero_ref, 0, xs_ref, slot, sem).start(priority=j % 2)
            return carry

        n_empty = empty_ref.shape[0]
        lax.fori_loop(0, n_empty // (ISSUE_TOKENS * TOP_K), clear, 0)
        for _ in range(n_empty // DISPATCH_TILE):
            retire_tile()

    def start(g, carry):
        r0 = g * ISSUE_TOKENS
        slots = [dest_ref[(base + r0) * TOP_K + j] for j in range(ISSUE_TOKENS * TOP_K)]
        for j, slot in enumerate(slots):
            _token_copy(u_ref, r0 + j // TOP_K, xs_ref, slot, sem).start(priority=j % 2)
        return carry

    lax.fori_loop(0, DISPATCH_TILE // ISSUE_TOKENS, start, 0)
    for _ in range(TOP_K):
        retire_tile()


def _dispatch(dest, empty_slots, u_tm, n_slots):
    n = dest.shape[0] // TOP_K
    assert empty_slots.shape[0] % DISPATCH_TILE == 0
    return pl.pallas_call(
        _dispatch_kernel,
        grid_spec=pltpu.PrefetchScalarGridSpec(
            num_scalar_prefetch=2,
            grid=(n // DISPATCH_TILE,),
            in_specs=[pl.BlockSpec((DISPATCH_TILE * TOKEN_ROWS, LANES), lambda i, d, e: (i, 0))],
            out_specs=pl.BlockSpec(memory_space=pl.ANY),
            scratch_shapes=[pltpu.VMEM((TOKEN_ROWS, LANES), u_tm.dtype),
                            pltpu.SemaphoreType.DMA(())],
        ),
        out_shape=jax.ShapeDtypeStruct((n_slots * TOKEN_ROWS, LANES), u_tm.dtype),
        compiler_params=_params("arbitrary"),
        name="moe_dispatch",
    )(dest, empty_slots, u_tm)


def _expert_kernel(be_ref, nused_ref, xs_ref, w1_ref, w3_ref, w2_ref, y_ref):
    del be_ref
    blk = pl.program_id(0)

    @pl.when(blk < nused_ref[0])
    def _():
        x = _load_token_major(xs_ref, MOE_TILE).astype(BF16)
        acc = jnp.zeros((MOE_TILE, D_MODEL), F32)
        for c0 in range(0, D_FF_EXPERT, FF_CHUNK):
            a = _dot(x, w1_ref[:, c0:c0 + FF_CHUNK])
            act = (_silu(a) * _dot(x, w3_ref[:, c0:c0 + FF_CHUNK])).astype(BF16)
            acc = acc + _dot(act, w2_ref[c0:c0 + FF_CHUNK, :])
        _store_token_major(y_ref, acc)

    @pl.when(blk >= nused_ref[0])
    def _():
        y_ref[...] = jnp.zeros_like(y_ref)


def _experts(block_expert, n_used, xs, w1, w3, w2):
    n_blocks = xs.shape[0] // (MOE_TILE * TOKEN_ROWS)
    wmap = lambda i, be, nu: (be[i], 0, 0)
    slots = pl.BlockSpec((MOE_TILE * TOKEN_ROWS, LANES), lambda i, be, nu: (i, 0))
    return pl.pallas_call(
        _expert_kernel,
        grid_spec=pltpu.PrefetchScalarGridSpec(
            num_scalar_prefetch=2,
            grid=(n_blocks,),
            in_specs=[
                slots,
                pl.BlockSpec((None, D_MODEL, D_FF_EXPERT), wmap),
                pl.BlockSpec((None, D_MODEL, D_FF_EXPERT), wmap),
                pl.BlockSpec((None, D_FF_EXPERT, D_MODEL), wmap),
            ],
            out_specs=slots,
        ),
        out_shape=jax.ShapeDtypeStruct(xs.shape, F32),
        compiler_params=_params("arbitrary"),
        name="moe_experts",
    )(block_expert, n_used, xs, w1, w3, w2)


def _combine_final_kernel(dest_ref, h_ref, gate_ref, y_ref, g_ref, o_ref, buf_ref, sem):
    step = pl.program_id(0)

    def issue(s, slot):
        base = s * Q_BLOCK

        def body(g, carry):
            r0 = g * ISSUE_TOKENS
            slots = [dest_ref[(base + r0) * TOP_K + j] for j in range(ISSUE_TOKENS * TOP_K)]
            for j, src in enumerate(slots):
                _token_copy(y_ref, src, buf_ref.at[slot, j % TOP_K], r0 + j // TOP_K,
                            sem.at[slot]).start(priority=j % 2)
            return carry

        lax.fori_loop(0, Q_BLOCK // ISSUE_TOKENS, body, 0)

    @pl.when(step == 0)
    def _():
        issue(0, 0)

    for slot in range(2):
        @pl.when(step % 2 == slot)
        def _():
            @pl.when(step + 1 < pl.num_programs(0))
            def _():
                issue(step + 1, 1 - slot)

            for k in range(TOP_K):
                pltpu.make_async_copy(y_ref.at[pl.ds(0, Q_BLOCK * TOKEN_ROWS), :],
                                      buf_ref.at[slot, k], sem.at[slot]).wait()
            gate = gate_ref[...]
            h = (h_ref[...]
                 + gate[:, 0:1] * _load_token_major(buf_ref.at[slot, 0], Q_BLOCK)
                 + gate[:, 1:2] * _load_token_major(buf_ref.at[slot, 1], Q_BLOCK))
            o_ref[...] = _rms(h, g_ref[...])


def _combine_final(dest, h, gate, y, gain, batch, length, seq):
    n = h.shape[0]
    per_seq = length // Q_BLOCK
    lead_blocks = LEAD // Q_BLOCK
    return pl.pallas_call(
        _combine_final_kernel,
        grid_spec=pltpu.PrefetchScalarGridSpec(
            num_scalar_prefetch=1,
            grid=(n // Q_BLOCK,),
            in_specs=[
                pl.BlockSpec((Q_BLOCK, D_MODEL), lambda i, d: (i, 0)),
                pl.BlockSpec((Q_BLOCK, TOP_K), lambda i, d: (i, 0)),
                pl.BlockSpec(memory_space=pl.ANY),
                pl.BlockSpec((1, D_MODEL), lambda i, d: (0, 0)),
            ],
            out_specs=pl.BlockSpec(
                (None, Q_BLOCK, D_MODEL),
                lambda i, d: (i // per_seq, jnp.maximum(i % per_seq - lead_blocks, 0), 0)),
            scratch_shapes=[pltpu.VMEM((2, TOP_K, Q_BLOCK * TOKEN_ROWS, LANES), F32),
                            pltpu.SemaphoreType.DMA((2,))],
        ),
        out_shape=jax.ShapeDtypeStruct((batch, seq, D_MODEL), F32),
        compiler_params=_params("arbitrary"),
        name="moe_combine_final",
    )(dest, h, gate, y, gain)


def _prefix_sum(x):
    k = x.shape[0]
    keep = (np.arange(k)[None, :] <= np.arange(k)[:, None]).reshape((k, k) + (1,) * (x.ndim - 1))
    return jnp.sum(jnp.where(keep, x[None], jnp.zeros_like(x[None])), axis=1)


def _slot_tables(route, counts, n_slots):
    n = route.shape[1]
    flat_e = route[0:TOP_K].T.reshape(n * TOP_K)
    rank = route[TOP_K:2 * TOP_K].T.reshape(n * TOP_K)
    counts = counts[:, 0]
    padded = (counts + MOE_TILE - 1) // MOE_TILE * MOE_TILE
    pad_end = _prefix_sum(padded)
    pad_start = pad_end - padded
    onehot = flat_e[:, None] == jnp.arange(N_EXPERTS, dtype=jnp.int32)[None, :]
    dest = (rank + jnp.sum(jnp.where(onehot, pad_start[None, :], 0), axis=-1)).astype(jnp.int32)
    n_blocks = n_slots // MOE_TILE
    block_first = jnp.arange(n_blocks, dtype=jnp.int32) * MOE_TILE
    block_expert = jnp.minimum(
        jnp.sum((pad_end[None, :] <= block_first[:, None]).astype(jnp.int32), axis=-1),
        N_EXPERTS - 1).astype(jnp.int32)
    n_used = (pad_end[-1:] // MOE_TILE).astype(jnp.int32)
    n_empty = n_slots - n * TOP_K
    gap_end = _prefix_sum(padded - counts)
    j = jnp.arange(n_empty, dtype=jnp.int32)
    owner = jnp.sum((j[:, None] >= gap_end[None, :]).astype(jnp.int32), axis=-1)
    first_empty = jnp.concatenate([pad_start + counts, pad_end[-1:]])
    gap_start = jnp.concatenate([jnp.zeros((1,), gap_end.dtype), gap_end])
    pick = owner[:, None] == jnp.arange(N_EXPERTS + 1, dtype=jnp.int32)[None, :]
    empty_slots = (j + jnp.sum(jnp.where(pick, (first_empty - gap_start)[None, :], 0), axis=-1))
    return dest, block_expert, n_used, empty_slots.astype(jnp.int32)


def _moe_ffn(u, route, gate, counts, w1, w3, w2):
    n = route.shape[1]
    n_slots = (n * TOP_K // MOE_TILE + N_EXPERTS) * MOE_TILE
    dest, block_expert, n_used, empty_slots = _slot_tables(route, counts, n_slots)
    xs = _dispatch(dest, empty_slots, u, n_slots)
    y = _experts(block_expert, n_used, xs, w1, w3, w2)
    return dest, gate[0:TOP_K].T, y


def kernel(x, meta, rel_bias, norm_mix, w_in, hg_lb_logits, hg_norm_w, da_lambda, da_subln_w, w_out, norm_ffn, dense_w1, dense_w3, dense_w2, moe_router, moe_w1, moe_w3, moe_w2, final_norm):
    batch, seq, d = x.shape
    length = LEAD + seq
    h = jnp.concatenate([
        jnp.zeros((batch, LEAD - N_META, d), x.dtype),
        jnp.broadcast_to(meta[None].astype(x.dtype), (batch, N_META, d)),
        x], axis=1).reshape(batch * length, d)

    toe = _attn_bias_tables(rel_bias)
    lb_cum = _prefix_sum(jax.nn.softmax(hg_lb_logits.astype(F32), axis=0))
    lb_all = jnp.clip(lb_cum - lb_cum[0:1], 0.0, LB_MAX)
    log_lb = jnp.log(lb_all)
    log_1m_lb = jnp.log1p(-lb_all)

    w_in_b, w_out_b = w_in.astype(BF16), w_out.astype(BF16)
    dense_b = [w.astype(BF16) for w in (dense_w1, dense_w3, dense_w2)]
    assert DEPTH % 2 == 0

    unsummed = None
    for l in range(DEPTH):
        if unsummed is None:
            hg, da = _mix_in(h, norm_mix[l][None], w_in_b, l)
        else:
            h, hg, da = _mix_in_combine(*unsummed, norm_mix[l][None], w_in_b, l)
            unsummed = None
        lam_init = 0.8 - 0.6 * math.exp(-0.3 * l)
        lv = da_lambda[l].astype(F32)
        lam = jnp.exp(jnp.sum(lv[0] * lv[1])) - jnp.exp(jnp.sum(lv[2] * lv[3])) + lam_init
        cst = jnp.zeros((8, LANES), F32).at[0].set(lam).at[1].set(1.0 - lam_init)
        o_hg = _hgrn(hg, log_lb[l][None], log_1m_lb[l][None], hg_norm_w[l][None], batch, length)
        o_da = _attn(da, toe, cst, da_subln_w[l][None], batch, length)
        i = l // 2
        if l % 2 == 0:
            h, moe_b = _dense_layer(h, o_hg, o_da, w_out_b, l, norm_ffn[l][None], *dense_b, i,
                                    (moe_w1, moe_w3, moe_w2), i)
        else:
            router = jnp.zeros((d, LANES), F32).at[:, :N_EXPERTS].set(moe_router[i].astype(F32))
            r_hi = router.astype(BF16)
            r_lo = (router - r_hi.astype(F32)).astype(BF16)
            router = jnp.concatenate([r_hi, r_hi, r_lo], axis=0)
            hn, u, route, gate, counts = _out_proj_router(h, o_hg, o_da, w_out_b, l,
                                                          norm_ffn[l][None], router)
            dest, gate, y = _moe_ffn(u, route, gate, counts, *moe_b)
            if l + 1 < DEPTH:
                unsummed = (dest, hn, gate, y)
            else:
                return _combine_final(dest, hn, gate, y, final_norm[None], batch, length, seq)
```

```python
import functools
import math

import jax
import jax.numpy as jnp
import numpy as np
from jax import lax
from jax.experimental import pallas as pl
from jax.experimental.pallas import tpu as pltpu

D_MODEL = 1024
DEPTH = 4
N_META = 16
LEAD = 128
HG_WIDTH = 512
HG_HEADS = 4
HG_D = 128
HG_CHUNK = 64
DA_HEADS = 4
DA_DQK = 64
DA_DV = 128
Q_BLOCK = 128
KEY_TILE = 4 * Q_BLOCK
Q_GROUP = 2
LOG2E = math.log2(math.e)
Q_SCALE = DA_DQK ** -0.5 * LOG2E
REL_BUCKETS = 32
REL_MAX_DIST = 128
N_EXPERTS = 8
TOP_K = 2
D_FF_EXPERT = 3584
EPS = 1e-6
NEG = -1e30
LB_MAX = 0.999
HG_COLS = 4 * HG_WIDTH
DA_COLS = 3 * DA_HEADS * DA_DV
W_IN_COLS = HG_COLS + DA_COLS

LANES = 128
VMEM_LIMIT = 56 * 1024 * 1024

ROW_TILE = 256
MIX_TILE = 512
HG_TILE = 128
MOE_TILE = 256
FF_CHUNK = 512

F32 = jnp.float32
BF16 = jnp.bfloat16


def _params(*sem):
    return pltpu.CompilerParams(dimension_semantics=sem, vmem_limit_bytes=VMEM_LIMIT)


def _dot(a, b):
    return jnp.dot(a, b, preferred_element_type=F32)


def _dot_nt(a, b):
    return lax.dot_general(a, b, (((1,), (1,)), ((), ())), preferred_element_type=F32)


def _dot_tn(a, b):
    return lax.dot_general(a, b, (((0,), (0,)), ((), ())), preferred_element_type=F32)


def _rms(x, gain):
    return x * lax.rsqrt(jnp.mean(x * x, axis=-1, keepdims=True) + EPS) * gain


TOKEN_ROWS = D_MODEL // LANES


def _store_token_major(ref, x):
    t = x.shape[0]
    for s in range(TOKEN_ROWS):
        ref[pl.ds(s, t, stride=TOKEN_ROWS), :] = x[:, s * LANES:(s + 1) * LANES]


def _load_token_major(ref, t):
    return jnp.concatenate(
        [ref[pl.ds(s, t, stride=TOKEN_ROWS), :] for s in range(TOKEN_ROWS)], axis=1)


def _silu(x):
    return x * (0.5 * jnp.tanh(0.5 * x) + 0.5)


BF16_SUBLANES = 16


def _cast_jobs(jobs, grid_steps):
    inputs, in_specs, out_specs, out_shapes = [], [], [], []
    for w, index in jobs:
        _, rows, cols = w.shape
        steps = max(s for s in range(1, grid_steps + 1)
                    if rows % s == 0 and (rows // s) % BF16_SUBLANES == 0)
        slab_rows = rows // steps
        in_specs.append(pl.BlockSpec((None, slab_rows, cols),
                                     lambda i, index=index, steps=steps: (index, jnp.minimum(i, steps - 1), 0)))
        out_specs.append(pl.BlockSpec((slab_rows, cols),
                                      lambda i, steps=steps: (jnp.minimum(i, steps - 1), 0)))
        out_shapes.append(jax.ShapeDtypeStruct((rows, cols), BF16))
        inputs.append(w)
    return inputs, in_specs, out_specs, out_shapes


def _run_cast_jobs(src_refs, dst_refs):
    for src_ref, dst_ref in zip(src_refs, dst_refs):
        dst_ref[...] = src_ref[...].astype(BF16)


def _mix_in_kernel(x_ref, g_ref, w_ref, *rest):
    n_cast = (len(rest) - 2) // 2
    hg_ref, da_ref = rest[n_cast:n_cast + 2]
    u = _rms(x_ref[...], g_ref[...]).astype(BF16)
    hg_ref[...] = _dot(u, w_ref[:, :HG_COLS])
    n_q = DA_HEADS * 2 * DA_DQK
    da_ref[:, :n_q] = (_dot(u, w_ref[:, HG_COLS:HG_COLS + n_q]) * Q_SCALE).astype(BF16)
    da_ref[:, n_q:] = _dot(u, w_ref[:, HG_COLS + n_q:]).astype(BF16)
    _run_cast_jobs(rest[:n_cast], rest[n_cast + 2:])


def _mix_in(h, gain, w, layer, to_cast=()):
    n = h.shape[0]
    steps = n // MIX_TILE
    cast_in, cast_in_specs, cast_out_specs, cast_out_shapes = _cast_jobs(to_cast, steps)
    out = pl.pallas_call(
        _mix_in_kernel,
        grid=(steps,),
        in_specs=[
            pl.BlockSpec((MIX_TILE, D_MODEL), lambda i: (i, 0)),
            pl.BlockSpec((1, D_MODEL), lambda i: (0, 0)),
            pl.BlockSpec((None, D_MODEL, W_IN_COLS), lambda i: (layer, 0, 0),
                         pipeline_mode=pl.Buffered(1)),
        ] + cast_in_specs,
        out_specs=[
            pl.BlockSpec((MIX_TILE, HG_COLS), lambda i: (i, 0)),
            pl.BlockSpec((MIX_TILE, DA_COLS), lambda i: (i, 0)),
        ] + cast_out_specs,
        out_shape=[
            jax.ShapeDtypeStruct((n, HG_COLS), F32),
            jax.ShapeDtypeStruct((n, DA_COLS), BF16),
        ] + cast_out_shapes,
        compiler_params=_params("arbitrary"),
        name="mix_in",
    )(h, gain, w, *cast_in)
    return out[0], out[1], list(out[2:])


def _mix_in_combine_kernel(dest_ref, hn_ref, gate_ref, y_ref, g_ref, w_ref,
                           h_ref, hg_ref, da_ref, buf_ref, sem):
    step = pl.program_id(0)
    last = pl.num_programs(0) - 1
    slot = step % 2
    tile_rows = MIX_TILE * TOKEN_ROWS

    def start_gathers(s, to_slot, rows):
        base = s * MIX_TILE
        slots = [dest_ref[(base + r) * TOP_K + k] for r in rows for k in range(TOP_K)]
        for j, src in enumerate(slots):
            r, k = rows[j // TOP_K], j % TOP_K
            _token_copy(y_ref, src, buf_ref.at[to_slot, k], r, sem.at[to_slot]).start(priority=j % 2)

    def wait_gathers(of_slot):
        for k in range(TOP_K):
            pltpu.make_async_copy(y_ref.at[pl.ds(0, tile_rows), :], buf_ref.at[of_slot, k],
                                  sem.at[of_slot]).wait()

    @pl.when(step == 0)
    def _():
        def first(g, carry):
            start_gathers(0, 0, [g * ISSUE_TOKENS + r for r in range(ISSUE_TOKENS)])
            return carry

        lax.fori_loop(0, MIX_TILE // ISSUE_TOKENS, first, 0)

    wait_gathers(slot)
    gate = gate_ref[...]
    h = (hn_ref[...] + gate[:, 0:1] * _load_token_major(buf_ref.at[slot, 0], MIX_TILE)
         + gate[:, 1:2] * _load_token_major(buf_ref.at[slot, 1], MIX_TILE))
    h_ref[...] = h
    u = _rms(h, g_ref[...]).astype(BF16)

    nxt = jnp.minimum(step + 1, last)
    n_q = DA_HEADS * 2 * DA_DQK
    groups = [(c, c + HG_WIDTH) for c in range(0, HG_COLS, HG_WIDTH)]
    groups += [(HG_COLS, HG_COLS + n_q), (HG_COLS + n_q, W_IN_COLS)]
    per_group = -(-MIX_TILE // len(groups))
    for gi, (c0, c1) in enumerate(groups):
        proj = _dot(u, w_ref[:, c0:c1])
        if c1 <= HG_COLS:
            hg_ref[:, c0:c1] = proj
        elif c0 == HG_COLS:
            da_ref[:, :n_q] = (proj * Q_SCALE).astype(BF16)
        else:
            da_ref[:, n_q:] = proj.astype(BF16)
        rows = list(range(gi * per_group, min((gi + 1) * per_group, MIX_TILE)))
        for r0 in range(0, len(rows), ISSUE_TOKENS):
            start_gathers(nxt, 1 - slot, rows[r0:r0 + ISSUE_TOKENS])

    @pl.when(step == last)
    def _():
        wait_gathers(1 - slot)


def _mix_in_combine(dest, hn, gate, y, gain, w, layer):
    n = hn.shape[0]
    row = lambda i, d: (i, 0)
    return pl.pallas_call(
        _mix_in_combine_kernel,
        grid_spec=pltpu.PrefetchScalarGridSpec(
            num_scalar_prefetch=1,
            grid=(n // MIX_TILE,),
            in_specs=[
                pl.BlockSpec((MIX_TILE, D_MODEL), row),
                pl.BlockSpec((MIX_TILE, TOP_K), row),
                pl.BlockSpec(memory_space=pl.ANY),
                pl.BlockSpec((1, D_MODEL), lambda i, d: (0, 0)),
                pl.BlockSpec((None, D_MODEL, W_IN_COLS), lambda i, d: (layer, 0, 0),
                             pipeline_mode=pl.Buffered(1)),
            ],
            out_specs=[
                pl.BlockSpec((MIX_TILE, D_MODEL), row),
                pl.BlockSpec((MIX_TILE, HG_COLS), row),
                pl.BlockSpec((MIX_TILE, DA_COLS), row),
            ],
            scratch_shapes=[pltpu.VMEM((2, TOP_K, MIX_TILE * TOKEN_ROWS, LANES), F32),
                            pltpu.SemaphoreType.DMA((2,))],
        ),
        out_shape=[
            jax.ShapeDtypeStruct((n, D_MODEL), F32),
            jax.ShapeDtypeStruct((n, HG_COLS), F32),
            jax.ShapeDtypeStruct((n, DA_COLS), BF16),
        ],
        compiler_params=_params("arbitrary"),
        name="mix_in_combine",
    )(dest, hn, gate, y, gain, w)


HG_LEVELS = (32, 16, 8, 4, 2, 1)
N_SUMS = len(HG_LEVELS) + 2


def _hgrn_consts():
    c = HG_CHUNK
    t = np.arange(c)[:, None]
    j = np.arange(c)[None, :]
    sums = np.zeros((N_SUMS, c, c), np.float32)
    masks = np.zeros((len(HG_LEVELS) + 1, c, c), np.float32)
    sums[0] = j <= t
    masks[0] = np.eye(c)
    for li, w in enumerate(HG_LEVELS, start=1):
        ref = (t // (2 * w)) * (2 * w) + w
        sums[li] = np.where(t >= ref, (j > ref) & (j <= t), (j > t) & (j <= ref))
        masks[li] = (t // (2 * w) == j // (2 * w)) & (t % (2 * w) >= w) & (j % (2 * w) < w)
    sums[N_SUMS - 1] = j > t
    sums = sums.reshape(N_SUMS * c, c)
    return np.concatenate([sums, sums], axis=1), masks


_HG_SUMS, _HG_MASKS = _hgrn_consts()


def _hgrn_kernel(hg_ref, loga_ref, log1m_ref, nw_ref, sums_ref, masks_ref, o_ref, state_ref):
    c_idx = pl.program_id(1)

    @pl.when(c_idx == 0)
    def _():
        state_ref[...] = jnp.zeros_like(state_ref)

    C = HG_CHUNK
    W = HG_WIDTH
    n_chunks = HG_TILE // C
    sums = sums_ref[...]
    nw = nw_ref[...]
    row_idx = c_idx * HG_TILE + lax.broadcasted_iota(jnp.int32, (HG_TILE, 1), 0)
    valid = row_idx >= (LEAD - N_META)
    step = lax.broadcasted_iota(jnp.int32, (C, 1), 0)
    head_cols = [slice(hd * HG_D, (hd + 1) * HG_D) for hd in range(HG_HEADS)]

    f = hg_ref[:, W:2 * W]
    qf = _silu(hg_ref[:, 0:W])
    ls = jnp.minimum(f, 0.0) - jnp.log(1.0 + jnp.exp(-jnp.abs(f)))
    cc = log1m_ref[...] + ls
    loga = loga_ref[...]
    lf = jnp.maximum(loga, cc) + jnp.log(1.0 + jnp.exp(-jnp.abs(loga - cc)))
    kk = jnp.exp(cc - f)
    lf = jnp.where(valid, lf, 0.0)
    kk = jnp.where(valid, kk, 0.0)
    lf2 = lf * LOG2E
    lf_hi = lf2.astype(BF16)
    lf_lo = (lf2 - lf_hi.astype(F32)).astype(BF16)
    vb = hg_ref[:, 2 * W:3 * W].astype(BF16)
    gate = _silu(hg_ref[:, 3 * W:4 * W])

    def side_by_side(x):
        return jnp.concatenate([x[ch * C:(ch + 1) * C] for ch in range(n_chunks)], axis=1)

    qf_w, kk_w = side_by_side(qf), side_by_side(kk)
    qb_w, kb_w = qf_w.astype(BF16), kk_w.astype(BF16)
    e = jnp.exp2(_dot(sums, jnp.concatenate([side_by_side(lf_hi), side_by_side(lf_lo)], axis=0)))
    e_b = e[0:C]
    decay_end = e_b[C - 1:C, :]
    q_in = (qf_w * e_b).astype(BF16)
    k_out = (kk_w * e[(N_SUMS - 1) * C:N_SUMS * C]).astype(BF16)
    z = [(jnp.where((step & w) != 0, qf_w, kk_w) * e[li * C:(li + 1) * C]).astype(BF16)
         for li, w in enumerate(HG_LEVELS, start=1)]
    unit_cols = [[slice(ch * W + hd * HG_D, ch * W + (hd + 1) * HG_D) for hd in range(HG_HEADS)]
                 for ch in range(n_chunks)]
    scores = []
    for ch in range(n_chunks):
        scores.append([])
        for cols in unit_cols[ch]:
            s = masks_ref[0] * _dot_nt(qb_w[:, cols], kb_w[:, cols])
            for li in range(1, len(HG_LEVELS) + 1):
                zl = z[li - 1][:, cols]
                s += masks_ref[li] * _dot_nt(zl, zl)
            scores[ch].append(s.astype(BF16))

    for ch in range(n_chunks):
        rows = slice(ch * C, (ch + 1) * C)
        for hd, cols in enumerate(head_cols):
            wide = unit_cols[ch][hd]
            st = state_ref[hd]
            v_h = vb[rows, cols]
            o = _dot_nt(q_in[:, wide], st.astype(BF16)) + _dot(scores[ch][hd], v_h)
            state_ref[hd] = st * decay_end[:, wide] + _dot_tn(v_h, k_out[:, wide])
            o = _rms(o, nw) * gate[rows, cols]
            o_ref[rows, cols] = o.astype(o_ref.dtype)


def _hgrn(hg, loga, log1m, norm_w, batch, length):
    hg3 = hg.reshape(batch, length, HG_COLS)
    out = pl.pallas_call(
        _hgrn_kernel,
        grid=(batch, length // HG_TILE),
        in_specs=[
            pl.BlockSpec((None, HG_TILE, HG_COLS), lambda b, c: (b, c, 0)),
            pl.BlockSpec((1, HG_WIDTH), lambda b, c: (0, 0)),
            pl.BlockSpec((1, HG_WIDTH), lambda b, c: (0, 0)),
            pl.BlockSpec((1, HG_D), lambda b, c: (0, 0)),
            pl.BlockSpec(_HG_SUMS.shape, lambda b, c: (0, 0)),
            pl.BlockSpec(_HG_MASKS.shape, lambda b, c: (0, 0, 0)),
        ],
        out_specs=pl.BlockSpec((None, HG_TILE, HG_WIDTH), lambda b, c: (b, c, 0)),
        out_shape=jax.ShapeDtypeStruct((batch, length, HG_WIDTH), BF16),
        scratch_shapes=[pltpu.VMEM((HG_HEADS, HG_D, HG_D), F32)],
        compiler_params=_params("parallel", "arbitrary"),
        name="hgrn2",
    )(hg3, loga, log1m, norm_w, jnp.asarray(_HG_SUMS, BF16), jnp.asarray(_HG_MASKS, F32))
    return out.reshape(batch * length, HG_WIDTH)


def _attn_kernel(q_ref, k_ref, v_ref, toe_ref, cst_ref, w_ref, o_ref, s_ref, *, n_blocks):
    lam = cst_ref[0:1, 0:1]
    post = cst_ref[1:2, :]
    lane = lax.broadcasted_iota(jnp.int32, (Q_BLOCK, Q_BLOCK), 1)
    first_half = lane < DA_DQK
    key_ok0 = lane >= (LEAD - N_META)
    inert_bias = jnp.where(key_ok0, 0.0, NEG)

    def near_bias(kind, kb):
        bias = toe_ref[kind]
        if kb == 0:
            bias = jnp.where(key_ok0, bias, NEG)
        return bias

    def slabs(x):
        return [x[:, c:c + Q_BLOCK] for c in range(0, x.shape[1], Q_BLOCK)]

    map_rows = 2 * Q_BLOCK
    groups = [tuple(range(i, min(i + Q_GROUP, n_blocks))) for i in range(0, n_blocks, Q_GROUP)]

    def both_maps(bias):
        return jnp.concatenate([bias, bias], axis=0)

    def stacked_q(blocks):
        parts = []
        for i in blocks:
            qi = q_ref[i * Q_BLOCK:(i + 1) * Q_BLOCK, :]
            zero = jnp.zeros_like(qi)
            parts += [jnp.where(first_half, qi, zero), jnp.where(first_half, zero, qi)]
        return jnp.concatenate(parts, axis=0)

    def group_tiles(blocks):
        first, last = blocks[0], blocks[-1]
        tiles = []
        far_end = max(first - 1, 0)
        if far_end >= 1:
            tiles.append((0, Q_BLOCK, jnp.concatenate([inert_bias] * (2 * len(blocks)), axis=0), 0))
        kb = 1
        while kb < far_end:
            width = KEY_TILE
            while kb + width // Q_BLOCK > far_end:
                width //= 2
            tiles.append((kb * Q_BLOCK, width, None, 0))
            kb += width // Q_BLOCK
        for kb in range(far_end, last + 1):
            biases, first_row = [], None
            for r, qb in enumerate(blocks):
                if kb > qb:
                    continue
                if first_row is None:
                    first_row = r * map_rows
                if kb == qb:
                    bias = near_bias(0, kb)
                elif kb == qb - 1:
                    bias = near_bias(1, kb)
                else:
                    bias = inert_bias if kb == 0 else jnp.zeros_like(inert_bias)
                biases.append(both_maps(bias))
            tiles.append((kb * Q_BLOCK, Q_BLOCK, jnp.concatenate(biases, axis=0), first_row))
        return tiles

    def merge(acc, x, first_row, op):
        if acc is None:
            return x
        if first_row == 0:
            return op(acc, x)
        return jnp.concatenate([acc[:first_row], op(acc[first_row:], x)], axis=0)

    def sweep_scores(g):
        blocks = groups[g]
        rows = len(blocks) * map_rows
        q2 = stacked_q(blocks)
        m_acc = None
        for start, width, bias, first_row in group_tiles(blocks):
            s = _dot_nt(q2[first_row:], k_ref[start:start + width, :])
            if bias is not None:
                s = s + bias
            s_ref[g % 2, first_row:rows, start:start + width] = s
            for slab in slabs(s):
                m_acc = merge(m_acc, slab, first_row, jnp.maximum)
        return m_acc.max(axis=-1, keepdims=True)

    row_max = sweep_scores(0)
    for g, blocks in enumerate(groups):
        rows = len(blocks) * map_rows
        m = row_max
        if g + 1 < len(groups):
            row_max = sweep_scores(g + 1)
        l_acc = o_acc = None
        for start, width, _, first_row in group_tiles(blocks):
            p = jnp.exp2(s_ref[g % 2, first_row:rows, start:start + width] - m[first_row:])
            for slab in slabs(p):
                l_acc = merge(l_acc, slab, first_row, jnp.add)
            o_acc = merge(o_acc, _dot(p.astype(BF16), v_ref[start:start + width, :]), first_row, jnp.add)
        o2 = o_acc * (1.0 / l_acc.sum(axis=-1, keepdims=True))
        for r, i in enumerate(blocks):
            o = o2[r * map_rows:r * map_rows + Q_BLOCK] - lam * o2[r * map_rows + Q_BLOCK:(r + 1) * map_rows]
            o = _rms(o, w_ref[...]) * post
            o_ref[i * Q_BLOCK:(i + 1) * Q_BLOCK, :] = o.astype(o_ref.dtype)


def _attn(da, toe, cst, subln_w, batch, length):
    da3 = da.reshape(batch, length, DA_COLS)
    hw = DA_HEADS
    out = pl.pallas_call(
        functools.partial(_attn_kernel, n_blocks=length // Q_BLOCK),
        grid=(batch, DA_HEADS),
        in_specs=[
            pl.BlockSpec((None, length, DA_DV), lambda b, h: (b, 0, h)),
            pl.BlockSpec((None, length, DA_DV), lambda b, h: (b, 0, hw + h)),
            pl.BlockSpec((None, length, DA_DV), lambda b, h: (b, 0, 2 * hw + h)),
            pl.BlockSpec((None, 2, Q_BLOCK, Q_BLOCK), lambda b, h: (h, 0, 0, 0)),
            pl.BlockSpec((8, LANES), lambda b, h: (0, 0)),
            pl.BlockSpec((1, DA_DV), lambda b, h: (0, 0)),
        ],
        out_specs=pl.BlockSpec((None, length, DA_DV), lambda b, h: (b, 0, h)),
        out_shape=jax.ShapeDtypeStruct((batch, length, DA_HEADS * DA_DV), BF16),
        scratch_shapes=[pltpu.VMEM((2, Q_GROUP * 2 * Q_BLOCK, length), F32)],
        compiler_params=_params("parallel", "parallel"),
        name="diff_attn",
    )(da3, da3, da3, toe, cst, subln_w)
    return out.reshape(batch * length, DA_HEADS * DA_DV)


def _t5_bucket(dist):
    n = jnp.maximum(dist, 0)
    max_exact = REL_BUCKETS // 2
    nf = jnp.maximum(n, max_exact).astype(F32)
    large = max_exact + (jnp.log(nf / max_exact) / math.log(REL_MAX_DIST / max_exact)
                         * (REL_BUCKETS - max_exact)).astype(jnp.int32)
    large = jnp.minimum(large, REL_BUCKETS - 1)
    return jnp.where(n < max_exact, n, large)


def _attn_bias_tables(rel_bias):
    tab = rel_bias.astype(F32)
    qi = jnp.arange(Q_BLOCK, dtype=jnp.int32)[:, None]
    ki = jnp.arange(Q_BLOCK, dtype=jnp.int32)[None, :]

    def lookup(bucket):
        onehot = bucket[None, :, :, None] == jnp.arange(REL_BUCKETS, dtype=jnp.int32)
        return jnp.sum(jnp.where(onehot, tab.T[:, None, None, :], 0.0), axis=-1)

    far = tab[REL_BUCKETS - 1][:, None, None]
    diag = jnp.where((ki <= qi)[None], (lookup(_t5_bucket(qi - ki)) - far) * LOG2E, NEG)
    prev = (lookup(_t5_bucket(qi - ki + Q_BLOCK)) - far) * LOG2E
    return jnp.stack([diag, prev], axis=1)


def _out_proj_router_kernel(h_ref, ohg_ref, oda_ref, wo_ref, g_ref, router_ref, before_ref,
                            hn_ref, u_ref, route_ref, gate_ref, count_ref, seen_ref):
    @pl.when(pl.program_id(0) == 0)
    def _():
        seen_ref[...] = jnp.zeros_like(seen_ref)

    hn = (h_ref[...] + _dot(ohg_ref[...], wo_ref[:HG_WIDTH, :])
          + _dot(oda_ref[...], wo_ref[HG_WIDTH:, :]))
    hn_ref[...] = hn
    u = _rms(hn, g_ref[...])
    _store_token_major(u_ref, u)
    u_hi = u.astype(BF16)
    u_lo = (u - u_hi.astype(F32)).astype(BF16)
    logits = _dot(jnp.concatenate([u_hi, u_lo, u_hi], axis=1), router_ref[...])
    logits = logits.T[:N_EXPERTS]
    expert = lax.broadcasted_iota(jnp.int32, logits.shape, 0)

    def top(x):
        best = x.max(axis=0, keepdims=True)
        return best, jnp.where(x == best, expert, N_EXPERTS).min(axis=0, keepdims=True)

    l1, e1 = top(logits)
    l2, e2 = top(jnp.where(expert == e1, -jnp.inf, logits))
    w2 = jnp.exp(l2 - l1)
    g1 = 1.0 / (1.0 + w2)
    g2 = w2 / (1.0 + w2)

    pick1 = (expert == e1).astype(F32)
    pick2 = (expert == e2).astype(F32)
    picked = pick1 + pick2
    ahead = seen_ref[:, 0:1] + _dot(picked.astype(BF16), before_ref[...])
    rank1 = jnp.sum(pick1 * ahead, axis=0, keepdims=True).astype(jnp.int32)
    rank2 = jnp.sum(pick2 * ahead, axis=0, keepdims=True).astype(jnp.int32)
    seen = seen_ref[...] + jnp.sum(picked, axis=1, keepdims=True)
    seen_ref[...] = seen

    route_ref[...] = jnp.where(expert == 0, e1, jnp.where(expert == 1, e2, jnp.where(
        expert == 2, rank1, jnp.where(expert == 3, rank2, 0))))
    gate_ref[...] = jnp.where(expert == 0, g1, jnp.where(expert == 1, g2, 0.0))
    count_ref[...] = seen.astype(jnp.int32)


def _out_proj_router(h, o_hg, o_da, wo, layer, gain, router):
    n = h.shape[0]
    row = lambda i: (i, 0)
    full = lambda i: (0, 0)
    assert N_EXPERTS >= 2 * TOP_K
    lanes_of = lambda i: (0, i)
    before = jnp.asarray(np.triu(np.ones((ROW_TILE, ROW_TILE), np.float32), 1), BF16)
    return pl.pallas_call(
        _out_proj_router_kernel,
        grid=(n // ROW_TILE,),
        in_specs=[
            pl.BlockSpec((ROW_TILE, D_MODEL), row),
            pl.BlockSpec((ROW_TILE, HG_WIDTH), row),
            pl.BlockSpec((ROW_TILE, DA_HEADS * DA_DV), row),
            pl.BlockSpec((None,) + wo.shape[1:], lambda i: (layer, 0, 0)),
            pl.BlockSpec((1, D_MODEL), full),
            pl.BlockSpec(router.shape, full),
            pl.BlockSpec(before.shape, full),
        ],
        out_specs=[
            pl.BlockSpec((ROW_TILE, D_MODEL), row),
            pl.BlockSpec((ROW_TILE * TOKEN_ROWS, LANES), row),
            pl.BlockSpec((N_EXPERTS, ROW_TILE), lanes_of),
            pl.BlockSpec((N_EXPERTS, ROW_TILE), lanes_of),
            pl.BlockSpec((N_EXPERTS, LANES), full),
        ],
        out_shape=[
            jax.ShapeDtypeStruct((n, D_MODEL), F32),
            jax.ShapeDtypeStruct((n * TOKEN_ROWS, LANES), F32),
            jax.ShapeDtypeStruct((N_EXPERTS, n), jnp.int32),
            jax.ShapeDtypeStruct((N_EXPERTS, n), F32),
            jax.ShapeDtypeStruct((N_EXPERTS, LANES), jnp.int32),
        ],
        scratch_shapes=[pltpu.VMEM((N_EXPERTS, LANES), F32)],
        compiler_params=_params("arbitrary"),
        name="out_proj_router",
    )(h, o_hg, o_da, wo, gain, router, before)


def _dense_layer_kernel(h_ref, ohg_ref, oda_ref, wo_ref, g_ref, w1_ref, w3_ref, w2_ref, *rest):
    n_cast = (len(rest) - 1) // 2
    o_ref = rest[n_cast]
    hn = (h_ref[...] + _dot(ohg_ref[...], wo_ref[:HG_WIDTH, :])
          + _dot(oda_ref[...], wo_ref[HG_WIDTH:, :]))
    u = _rms(hn, g_ref[...]).astype(BF16)
    a = _dot(u, w1_ref[...])
    act = (_silu(a) * _dot(u, w3_ref[...])).astype(BF16)
    o_ref[...] = hn + _dot(act, w2_ref[...])
    _run_cast_jobs(rest[:n_cast], rest[n_cast + 1:])


def _dense_layer(h, o_hg, o_da, wo, layer, gain, w1, w3, w2, idx, to_cast, cast_idx):
    n = h.shape[0]
    steps = n // ROW_TILE
    row = lambda i: (i, 0)

    def resident(w, index):
        return pl.BlockSpec((None,) + w.shape[1:], lambda i: (index, 0, 0),
                            pipeline_mode=pl.Buffered(1))

    jobs = [(w.reshape(w.shape[0], w.shape[1] * w.shape[2], w.shape[3]), cast_idx) for w in to_cast]
    cast_in, cast_in_specs, cast_out_specs, cast_out_shape = _cast_jobs(jobs, steps)

    out = pl.pallas_call(
        _dense_layer_kernel,
        grid=(steps,),
        in_specs=[
            pl.BlockSpec((ROW_TILE, D_MODEL), row),
            pl.BlockSpec((ROW_TILE, HG_WIDTH), row),
            pl.BlockSpec((ROW_TILE, DA_HEADS * DA_DV), row),
            resident(wo, layer),
            pl.BlockSpec((1, D_MODEL), lambda i: (0, 0)),
            resident(w1, idx),
            resident(w3, idx),
            resident(w2, idx),
        ] + cast_in_specs,
        out_specs=[pl.BlockSpec((ROW_TILE, D_MODEL), row)] + cast_out_specs,
        out_shape=[jax.ShapeDtypeStruct((n, D_MODEL), F32)] + cast_out_shape,
        compiler_params=_params("arbitrary"),
        name="dense_layer",
    )(h, o_hg, o_da, wo, gain, w1, w3, w2, *cast_in)
    return out[0], [b.reshape(w.shape[1:]) for b, w in zip(out[1:], to_cast)]


DISPATCH_TILE = 1024
ISSUE_TOKENS = 8


def _token_copy(src_ref, src_tok, dst_ref, dst_tok, sem):
    src = pl.multiple_of(src_tok * TOKEN_ROWS, TOKEN_ROWS)
    dst = pl.multiple_of(dst_tok * TOKEN_ROWS, TOKEN_ROWS)
    return pltpu.make_async_copy(src_ref.at[pl.ds(src, TOKEN_ROWS), :],
                                 dst_ref.at[pl.ds(dst, TOKEN_ROWS), :], sem)


def _dispatch_kernel(dest_ref, empty_ref, u_ref, xs_ref, zero_ref, sem):
    step = pl.program_id(0)
    base = step * DISPATCH_TILE
    tile_rows = DISPATCH_TILE * TOKEN_ROWS

    def retire_tile():
        pltpu.make_async_copy(u_ref, xs_ref.at[pl.ds(0, tile_rows), :], sem).wait()

    @pl.when(step == 0)
    def _():
        zero_ref[...] = jnp.zeros_like(zero_ref)

        def clear(g, carry):
            slots = [empty_ref[g * ISSUE_TOKENS * TOP_K + j] for j in range(ISSUE_TOKENS * TOP_K)]
            for j, slot in enumerate(slots):
                _token_copy(zero_ref, 0, xs_ref, slot, sem).start(priority=j % 2)
            return carry

        n_empty = empty_ref.shape[0]
        lax.fori_loop(0, n_empty // (ISSUE_TOKENS * TOP_K), clear, 0)
        for _ in range(n_empty // DISPATCH_TILE):
            retire_tile()

    def start(g, carry):
        r0 = g * ISSUE_TOKENS
        slots = [dest_ref[(base + r0) * TOP_K + j] for j in range(ISSUE_TOKENS * TOP_K)]
        for j, slot in enumerate(slots):
            _token_copy(u_ref, r0 + j // TOP_K, xs_ref, slot, sem).start(priority=j % 2)
        return carry

    lax.fori_loop(0, DISPATCH_TILE // ISSUE_TOKENS, start, 0)
    for _ in range(TOP_K):
        retire_tile()


def _dispatch(dest, empty_slots, u_tm, n_slots):
    n = dest.shape[0] // TOP_K
    assert empty_slots.shape[0] % DISPATCH_TILE == 0
    return pl.pallas_call(
        _dispatch_kernel,
        grid_spec=pltpu.PrefetchScalarGridSpec(
            num_scalar_prefetch=2,
            grid=(n // DISPATCH_TILE,),
            in_specs=[pl.BlockSpec((DISPATCH_TILE * TOKEN_ROWS, LANES), lambda i, d, e: (i, 0))],
            out_specs=pl.BlockSpec(memory_space=pl.ANY),
            scratch_shapes=[pltpu.VMEM((TOKEN_ROWS, LANES), u_tm.dtype),
                            pltpu.SemaphoreType.DMA(())],
        ),
        out_shape=jax.ShapeDtypeStruct((n_slots * TOKEN_ROWS, LANES), u_tm.dtype),
        compiler_params=_params("arbitrary"),
        name="moe_dispatch",
    )(dest, empty_slots, u_tm)


def _expert_kernel(be_ref, nused_ref, xs_ref, w1_ref, w3_ref, w2_ref, y_ref):
    del be_ref
    blk = pl.program_id(0)

    @pl.when(blk < nused_ref[0])
    def _():
        x = _load_token_major(xs_ref, MOE_TILE).astype(BF16)
        acc = jnp.zeros((MOE_TILE, D_MODEL), F32)
        for c0 in range(0, D_FF_EXPERT, FF_CHUNK):
            a = _dot(x, w1_ref[:, c0:c0 + FF_CHUNK])
            act = (_silu(a) * _dot(x, w3_ref[:, c0:c0 + FF_CHUNK])).astype(BF16)
            acc = acc + _dot(act, w2_ref[c0:c0 + FF_CHUNK, :])
        _store_token_major(y_ref, acc)

    @pl.when(blk >= nused_ref[0])
    def _():
        y_ref[...] = jnp.zeros_like(y_ref)


def _experts(block_expert, n_used, xs, w1, w3, w2):
    n_blocks = xs.shape[0] // (MOE_TILE * TOKEN_ROWS)
    wmap = lambda i, be, nu: (be[i], 0, 0)
    slots = pl.BlockSpec((MOE_TILE * TOKEN_ROWS, LANES), lambda i, be, nu: (i, 0))
    return pl.pallas_call(
        _expert_kernel,
        grid_spec=pltpu.PrefetchScalarGridSpec(
            num_scalar_prefetch=2,
            grid=(n_blocks,),
            in_specs=[
                slots,
                pl.BlockSpec((None, D_MODEL, D_FF_EXPERT), wmap),
                pl.BlockSpec((None, D_MODEL, D_FF_EXPERT), wmap),
                pl.BlockSpec((None, D_FF_EXPERT, D_MODEL), wmap),
            ],
            out_specs=slots,
        ),
        out_shape=jax.ShapeDtypeStruct(xs.shape, F32),
        compiler_params=_params("arbitrary"),
        name="moe_experts",
    )(block_expert, n_used, xs, w1, w3, w2)


def _combine_final_kernel(dest_ref, h_ref, gate_ref, y_ref, g_ref, o_ref, buf_ref, sem):
    step = pl.program_id(0)

    def issue(s, slot):
        base = s * Q_BLOCK

        def body(g, carry):
            r0 = g * ISSUE_TOKENS
            slots = [dest_ref[(base + r0) * TOP_K + j] for j in range(ISSUE_TOKENS * TOP_K)]
            for j, src in enumerate(slots):
                _token_copy(y_ref, src, buf_ref.at[slot, j % TOP_K], r0 + j // TOP_K,
                            sem.at[slot]).start(priority=j % 2)
            return carry

        lax.fori_loop(0, Q_BLOCK // ISSUE_TOKENS, body, 0)

    @pl.when(step == 0)
    def _():
        issue(0, 0)

    for slot in range(2):
        @pl.when(step % 2 == slot)
        def _():
            @pl.when(step + 1 < pl.num_programs(0))
            def _():
                issue(step + 1, 1 - slot)

            for k in range(TOP_K):
                pltpu.make_async_copy(y_ref.at[pl.ds(0, Q_BLOCK * TOKEN_ROWS), :],
                                      buf_ref.at[slot, k], sem.at[slot]).wait()
            gate = gate_ref[...]
            h = (h_ref[...]
                 + gate[:, 0:1] * _load_token_major(buf_ref.at[slot, 0], Q_BLOCK)
                 + gate[:, 1:2] * _load_token_major(buf_ref.at[slot, 1], Q_BLOCK))
            o_ref[...] = _rms(h, g_ref[...])


def _combine_final(dest, h, gate, y, gain, batch, length, seq):
    n = h.shape[0]
    per_seq = length // Q_BLOCK
    lead_blocks = LEAD // Q_BLOCK
    return pl.pallas_call(
        _combine_final_kernel,
        grid_spec=pltpu.PrefetchScalarGridSpec(
            num_scalar_prefetch=1,
            grid=(n // Q_BLOCK,),
            in_specs=[
                pl.BlockSpec((Q_BLOCK, D_MODEL), lambda i, d: (i, 0)),
                pl.BlockSpec((Q_BLOCK, TOP_K), lambda i, d: (i, 0)),
                pl.BlockSpec(memory_space=pl.ANY),
                pl.BlockSpec((1, D_MODEL), lambda i, d: (0, 0)),
            ],
            out_specs=pl.BlockSpec(
                (None, Q_BLOCK, D_MODEL),
                lambda i, d: (i // per_seq, jnp.maximum(i % per_seq - lead_blocks, 0), 0)),
            scratch_shapes=[pltpu.VMEM((2, TOP_K, Q_BLOCK * TOKEN_ROWS, LANES), F32),
                            pltpu.SemaphoreType.DMA((2,))],
        ),
        out_shape=jax.ShapeDtypeStruct((batch, seq, D_MODEL), F32),
        compiler_params=_params("arbitrary"),
        name="moe_combine_final",
    )(dest, h, gate, y, gain)


def _prefix_sum(x):
    k = x.shape[0]
    keep = (np.arange(k)[None, :] <= np.arange(k)[:, None]).reshape((k, k) + (1,) * (x.ndim - 1))
    return jnp.sum(jnp.where(keep, x[None], jnp.zeros_like(x[None])), axis=1)


def _slot_tables(route, counts, n_slots):
    n = route.shape[1]
    flat_e = route[0:TOP_K].T.reshape(n * TOP_K)
    rank = route[TOP_K:2 * TOP_K].T.reshape(n * TOP_K)
    counts = counts[:, 0]
    padded = (counts + MOE_TILE - 1) // MOE_TILE * MOE_TILE
    pad_end = _prefix_sum(padded)
    pad_start = pad_end - padded
    onehot = flat_e[:, None] == jnp.arange(N_EXPERTS, dtype=jnp.int32)[None, :]
    dest = (rank + jnp.sum(jnp.where(onehot, pad_start[None, :], 0), axis=-1)).astype(jnp.int32)
    n_blocks = n_slots // MOE_TILE
    block_first = jnp.arange(n_blocks, dtype=jnp.int32) * MOE_TILE
    block_expert = jnp.minimum(
        jnp.sum((pad_end[None, :] <= block_first[:, None]).astype(jnp.int32), axis=-1),
        N_EXPERTS - 1).astype(jnp.int32)
    n_used = (pad_end[-1:] // MOE_TILE).astype(jnp.int32)
    n_empty = n_slots - n * TOP_K
    gap_end = _prefix_sum(padded - counts)
    j = jnp.arange(n_empty, dtype=jnp.int32)
    owner = jnp.sum((j[:, None] >= gap_end[None, :]).astype(jnp.int32), axis=-1)
    first_empty = jnp.concatenate([pad_start + counts, pad_end[-1:]])
    gap_start = jnp.concatenate([jnp.zeros((1,), gap_end.dtype), gap_end])
    pick = owner[:, None] == jnp.arange(N_EXPERTS + 1, dtype=jnp.int32)[None, :]
    empty_slots = (j + jnp.sum(jnp.where(pick, (first_empty - gap_start)[None, :], 0), axis=-1))
    return dest, block_expert, n_used, empty_slots.astype(jnp.int32)


def _moe_ffn(u, route, gate, counts, w1, w3, w2):
    n = route.shape[1]
    n_slots = (n * TOP_K // MOE_TILE + N_EXPERTS) * MOE_TILE
    dest, block_expert, n_used, empty_slots = _slot_tables(route, counts, n_slots)
    xs = _dispatch(dest, empty_slots, u, n_slots)
    y = _experts(block_expert, n_used, xs, w1, w3, w2)
    return dest, gate[0:TOP_K].T, y


def kernel(x, meta, rel_bias, norm_mix, w_in, hg_lb_logits, hg_norm_w, da_lambda, da_subln_w, w_out, norm_ffn, dense_w1, dense_w3, dense_w2, moe_router, moe_w1, moe_w3, moe_w2, final_norm):
    batch, seq, d = x.shape
    length = LEAD + seq
    h = jnp.concatenate([
        jnp.zeros((batch, LEAD - N_META, d), x.dtype),
        jnp.broadcast_to(meta[None].astype(x.dtype), (batch, N_META, d)),
        x], axis=1).reshape(batch * length, d)

    toe = _attn_bias_tables(rel_bias)
    lb_cum = _prefix_sum(jax.nn.softmax(hg_lb_logits.astype(F32), axis=0))
    lb_all = jnp.clip(lb_cum - lb_cum[0:1], 0.0, LB_MAX)
    log_lb = jnp.log(lb_all)
    log_1m_lb = jnp.log1p(-lb_all)

    assert DEPTH % 2 == 0
    dense_f32 = (dense_w1, dense_w3, dense_w2)
    stacked_rows = lambda w: w.reshape(1, w.shape[0] * w.shape[1], w.shape[2])

    unsummed = None
    for l in range(DEPTH):
        if l == 0:
            jobs = [(w, 0) for w in dense_f32] + [(stacked_rows(w_out), 0), (stacked_rows(w_in), 0)]
            hg, da, casts = _mix_in(h, norm_mix[l][None], w_in[0:1].astype(BF16), 0, jobs)
            dense_b = [w[None] for w in casts[:3]]
            w_out_b = casts[3].reshape(w_out.shape)
            w_in_b = casts[4].reshape(w_in.shape)
        elif unsummed is None:
            jobs = [(w, (l + 1) // 2) for w in dense_f32] if l + 1 < DEPTH else []
            hg, da, casts = _mix_in(h, norm_mix[l][None], w_in_b, l, jobs)
            if jobs:
                dense_b = [w[None] for w in casts]
        else:
            h, hg, da = _mix_in_combine(*unsummed, norm_mix[l][None], w_in_b, l)
            unsummed = None
        lam_init = 0.8 - 0.6 * math.exp(-0.3 * l)
        lv = da_lambda[l].astype(F32)
        lam = jnp.exp(jnp.sum(lv[0] * lv[1])) - jnp.exp(jnp.sum(lv[2] * lv[3])) + lam_init
        cst = jnp.zeros((8, LANES), F32).at[0].set(lam).at[1].set(1.0 - lam_init)
        o_hg = _hgrn(hg, log_lb[l][None], log_1m_lb[l][None], hg_norm_w[l][None], batch, length)
        o_da = _attn(da, toe, cst, da_subln_w[l][None], batch, length)
        i = l // 2
        if l % 2 == 0:
            h, moe_b = _dense_layer(h, o_hg, o_da, w_out_b, l, norm_ffn[l][None], *dense_b, 0,
                                    (moe_w1, moe_w3, moe_w2), i)
        else:
            router = jnp.zeros((d, LANES), F32).at[:, :N_EXPERTS].set(moe_router[i].astype(F32))
            r_hi = router.astype(BF16)
            r_lo = (router - r_hi.astype(F32)).astype(BF16)
            router = jnp.concatenate([r_hi, r_hi, r_lo], axis=0)
            hn, u, route, gate, counts = _out_proj_router(h, o_hg, o_da, w_out_b, l,
                                                          norm_ffn[l][None], router)
            dest, gate, y = _moe_ffn(u, route, gate, counts, *moe_b)
            if l + 1 < DEPTH:
                unsummed = (dest, hn, gate, y)
            else:
                return _combine_final(dest, hn, gate, y, final_norm[None], batch, length, seq)
```

```python
import functools
import math

import jax
import jax.numpy as jnp
import numpy as np
from jax import lax
from jax.experimental import pallas as pl
from jax.experimental.pallas import tpu as pltpu

D_MODEL = 1024
DEPTH = 4
N_META = 16
LEAD = 128
HG_WIDTH = 512
HG_HEADS = 4
HG_D = 128
HG_CHUNK = 64
DA_HEADS = 4
DA_DQK = 64
DA_DV = 128
Q_BLOCK = 128
KEY_TILE = 4 * Q_BLOCK
Q_GROUP = 2
LOG2E = math.log2(math.e)
Q_SCALE = DA_DQK ** -0.5 * LOG2E
REL_BUCKETS = 32
REL_MAX_DIST = 128
N_EXPERTS = 8
TOP_K = 2
D_FF_EXPERT = 3584
EPS = 1e-6
NEG = -1e30
LB_MAX = 0.999
HG_COLS = 4 * HG_WIDTH
DA_COLS = 3 * DA_HEADS * DA_DV
W_IN_COLS = HG_COLS + DA_COLS

LANES = 128
SUBLANES = 8
VMEM_LIMIT = 56 * 1024 * 1024

ROW_TILE = 256
MIX_TILE = 512
HG_TILE = 128
MOE_TILE = 256
FF_CHUNK = 512

F32 = jnp.float32
BF16 = jnp.bfloat16


def _params(*sem):
    return pltpu.CompilerParams(dimension_semantics=sem, vmem_limit_bytes=VMEM_LIMIT)


def _dot(a, b):
    return jnp.dot(a, b, preferred_element_type=F32)


def _dot_nt(a, b):
    return lax.dot_general(a, b, (((1,), (1,)), ((), ())), preferred_element_type=F32)


def _dot_tn(a, b):
    return lax.dot_general(a, b, (((0,), (0,)), ((), ())), preferred_element_type=F32)


def _rms(x, gain):
    return x * lax.rsqrt(jnp.mean(x * x, axis=-1, keepdims=True) + EPS) * gain


TOKEN_ROWS = D_MODEL // LANES


def _store_token_major(ref, x):
    t = x.shape[0]
    for s in range(TOKEN_ROWS):
        ref[pl.ds(s, t, stride=TOKEN_ROWS), :] = x[:, s * LANES:(s + 1) * LANES]


def _load_token_major(ref, t):
    return jnp.concatenate(
        [ref[pl.ds(s, t, stride=TOKEN_ROWS), :] for s in range(TOKEN_ROWS)], axis=1)


def _silu(x):
    return x * (0.5 * jnp.tanh(0.5 * x) + 0.5)


BF16_SUBLANES = 16


def _cast_jobs(jobs, grid_steps):
    inputs, in_specs, out_specs, out_shapes = [], [], [], []
    for w, index in jobs:
        _, rows, cols = w.shape
        steps = max(s for s in range(1, grid_steps + 1)
                    if rows % s == 0 and (rows // s) % BF16_SUBLANES == 0)
        slab_rows = rows // steps
        in_specs.append(pl.BlockSpec((None, slab_rows, cols),
                                     lambda i, index=index, steps=steps: (index, jnp.minimum(i, steps - 1), 0)))
        out_specs.append(pl.BlockSpec((slab_rows, cols),
                                      lambda i, steps=steps: (jnp.minimum(i, steps - 1), 0)))
        out_shapes.append(jax.ShapeDtypeStruct((rows, cols), BF16))
        inputs.append(w)
    return inputs, in_specs, out_specs, out_shapes


def _run_cast_jobs(src_refs, dst_refs):
    for src_ref, dst_ref in zip(src_refs, dst_refs):
        dst_ref[...] = src_ref[...].astype(BF16)


def _mix_in_kernel(x_ref, g_ref, w_ref, *rest):
    n_cast = (len(rest) - 2) // 2
    hg_ref, da_ref = rest[n_cast:n_cast + 2]
    u = _rms(x_ref[...], g_ref[...]).astype(BF16)
    hg_ref[...] = _dot(u, w_ref[:, :HG_COLS])
    n_q = DA_HEADS * 2 * DA_DQK
    da_ref[:, :n_q] = (_dot(u, w_ref[:, HG_COLS:HG_COLS + n_q]) * Q_SCALE).astype(BF16)
    da_ref[:, n_q:] = _dot(u, w_ref[:, HG_COLS + n_q:]).astype(BF16)
    _run_cast_jobs(rest[:n_cast], rest[n_cast + 2:])


def _mix_in(h, gain, w, layer, to_cast=()):
    n = h.shape[0]
    steps = n // MIX_TILE
    cast_in, cast_in_specs, cast_out_specs, cast_out_shapes = _cast_jobs(to_cast, steps)
    out = pl.pallas_call(
        _mix_in_kernel,
        grid=(steps,),
        in_specs=[
            pl.BlockSpec((MIX_TILE, D_MODEL), lambda i: (i, 0)),
            pl.BlockSpec((1, D_MODEL), lambda i: (0, 0)),
            pl.BlockSpec((None, D_MODEL, W_IN_COLS), lambda i: (layer, 0, 0),
                         pipeline_mode=pl.Buffered(1)),
        ] + cast_in_specs,
        out_specs=[
            pl.BlockSpec((MIX_TILE, HG_COLS), lambda i: (i, 0)),
            pl.BlockSpec((MIX_TILE, DA_COLS), lambda i: (i, 0)),
        ] + cast_out_specs,
        out_shape=[
            jax.ShapeDtypeStruct((n, HG_COLS), F32),
            jax.ShapeDtypeStruct((n, DA_COLS), BF16),
        ] + cast_out_shapes,
        compiler_params=_params("arbitrary"),
        name="mix_in",
    )(h, gain, w, *cast_in)
    return out[0], out[1], list(out[2:])


def _mix_in_combine_kernel(dest_ref, hn_ref, gate_ref, y_ref, g_ref, w_ref,
                           h_ref, hg_ref, da_ref, buf_ref, sem):
    step = pl.program_id(0)
    last = pl.num_programs(0) - 1
    slot = step % 2
    tile_rows = MIX_TILE * TOKEN_ROWS
    n_tokens = dest_ref.shape[0] // TOP_K

    def start_gathers(s, to_slot, rows):
        base = s * MIX_TILE
        slots = [dest_ref[k * n_tokens + base + r] for r in rows for k in range(TOP_K)]
        for j, src in enumerate(slots):
            r, k = rows[j // TOP_K], j % TOP_K
            _token_copy(y_ref, src, buf_ref.at[to_slot, k], r, sem.at[to_slot]).start(priority=j % 2)

    def wait_gathers(of_slot):
        for k in range(TOP_K):
            pltpu.make_async_copy(y_ref.at[pl.ds(0, tile_rows), :], buf_ref.at[of_slot, k],
                                  sem.at[of_slot]).wait()

    @pl.when(step == 0)
    def _():
        def first(g, carry):
            start_gathers(0, 0, [g * ISSUE_TOKENS + r for r in range(ISSUE_TOKENS)])
            return carry

        lax.fori_loop(0, MIX_TILE // ISSUE_TOKENS, first, 0)

    wait_gathers(slot)
    gate = gate_ref[...]
    h = (hn_ref[...] + gate[:, 0:1] * _load_token_major(buf_ref.at[slot, 0], MIX_TILE)
         + gate[:, 1:2] * _load_token_major(buf_ref.at[slot, 1], MIX_TILE))
    h_ref[...] = h
    u = _rms(h, g_ref[...]).astype(BF16)

    nxt = jnp.minimum(step + 1, last)
    n_q = DA_HEADS * 2 * DA_DQK
    groups = [(c, c + HG_WIDTH) for c in range(0, HG_COLS, HG_WIDTH)]
    groups += [(HG_COLS, HG_COLS + n_q), (HG_COLS + n_q, W_IN_COLS)]
    per_group = -(-MIX_TILE // len(groups))
    for gi, (c0, c1) in enumerate(groups):
        proj = _dot(u, w_ref[:, c0:c1])
        if c1 <= HG_COLS:
            hg_ref[:, c0:c1] = proj
        elif c0 == HG_COLS:
            da_ref[:, :n_q] = (proj * Q_SCALE).astype(BF16)
        else:
            da_ref[:, n_q:] = proj.astype(BF16)
        rows = list(range(gi * per_group, min((gi + 1) * per_group, MIX_TILE)))
        for r0 in range(0, len(rows), ISSUE_TOKENS):
            start_gathers(nxt, 1 - slot, rows[r0:r0 + ISSUE_TOKENS])

    @pl.when(step == last)
    def _():
        wait_gathers(1 - slot)


def _mix_in_combine(dest, hn, gate, y, gain, w, layer):
    n = hn.shape[0]
    row = lambda i, d: (i, 0)
    return pl.pallas_call(
        _mix_in_combine_kernel,
        grid_spec=pltpu.PrefetchScalarGridSpec(
            num_scalar_prefetch=1,
            grid=(n // MIX_TILE,),
            in_specs=[
                pl.BlockSpec((MIX_TILE, D_MODEL), row),
                pl.BlockSpec((MIX_TILE, N_EXPERTS), row),
                pl.BlockSpec(memory_space=pl.ANY),
                pl.BlockSpec((1, D_MODEL), lambda i, d: (0, 0)),
                pl.BlockSpec((None, D_MODEL, W_IN_COLS), lambda i, d: (layer, 0, 0),
                             pipeline_mode=pl.Buffered(1)),
            ],
            out_specs=[
                pl.BlockSpec((MIX_TILE, D_MODEL), row),
                pl.BlockSpec((MIX_TILE, HG_COLS), row),
                pl.BlockSpec((MIX_TILE, DA_COLS), row),
            ],
            scratch_shapes=[pltpu.VMEM((2, TOP_K, MIX_TILE * TOKEN_ROWS, LANES), F32),
                            pltpu.SemaphoreType.DMA((2,))],
        ),
        out_shape=[
            jax.ShapeDtypeStruct((n, D_MODEL), F32),
            jax.ShapeDtypeStruct((n, HG_COLS), F32),
            jax.ShapeDtypeStruct((n, DA_COLS), BF16),
        ],
        compiler_params=_params("arbitrary"),
        name="mix_in_combine",
    )(dest, hn, gate, y, gain, w)


HG_LEVELS = (32, 16, 8, 4, 2, 1)
N_SUMS = len(HG_LEVELS) + 2


def _hgrn_consts():
    c = HG_CHUNK
    t = np.arange(c)[:, None]
    j = np.arange(c)[None, :]
    sums = np.zeros((N_SUMS, c, c), np.float32)
    masks = np.zeros((len(HG_LEVELS) + 1, c, c), np.float32)
    sums[0] = j <= t
    masks[0] = np.eye(c)
    for li, w in enumerate(HG_LEVELS, start=1):
        ref = (t // (2 * w)) * (2 * w) + w
        sums[li] = np.where(t >= ref, (j > ref) & (j <= t), (j > t) & (j <= ref))
        masks[li] = (t // (2 * w) == j // (2 * w)) & (t % (2 * w) >= w) & (j % (2 * w) < w)
    sums[N_SUMS - 1] = j > t
    sums = sums.reshape(N_SUMS * c, c)
    return np.concatenate([sums, sums], axis=1), masks


_HG_SUMS, _HG_MASKS = _hgrn_consts()


def _hgrn_kernel(hg_ref, loga_ref, log1m_ref, nw_ref, sums_ref, masks_ref, o_ref, state_ref):
    c_idx = pl.program_id(1)

    @pl.when(c_idx == 0)
    def _():
        state_ref[...] = jnp.zeros_like(state_ref)

    C = HG_CHUNK
    W = HG_WIDTH
    n_chunks = HG_TILE // C
    sums = sums_ref[...]
    nw = nw_ref[...]
    row_idx = c_idx * HG_TILE + lax.broadcasted_iota(jnp.int32, (HG_TILE, 1), 0)
    valid = row_idx >= (LEAD - N_META)
    step = lax.broadcasted_iota(jnp.int32, (C, 1), 0)
    head_cols = [slice(hd * HG_D, (hd + 1) * HG_D) for hd in range(HG_HEADS)]

    f = hg_ref[:, W:2 * W]
    qf = _silu(hg_ref[:, 0:W])
    ls = jnp.minimum(f, 0.0) - jnp.log(1.0 + jnp.exp(-jnp.abs(f)))
    cc = log1m_ref[...] + ls
    loga = loga_ref[...]
    lf = jnp.maximum(loga, cc) + jnp.log(1.0 + jnp.exp(-jnp.abs(loga - cc)))
    kk = jnp.exp(cc - f)
    lf = jnp.where(valid, lf, 0.0)
    kk = jnp.where(valid, kk, 0.0)
    lf2 = lf * LOG2E
    lf_hi = lf2.astype(BF16)
    lf_lo = (lf2 - lf_hi.astype(F32)).astype(BF16)
    vb = hg_ref[:, 2 * W:3 * W].astype(BF16)
    gate = _silu(hg_ref[:, 3 * W:4 * W])

    def side_by_side(x):
        return jnp.concatenate([x[ch * C:(ch + 1) * C] for ch in range(n_chunks)], axis=1)

    qf_w, kk_w = side_by_side(qf), side_by_side(kk)
    qb_w, kb_w = qf_w.astype(BF16), kk_w.astype(BF16)
    e = jnp.exp2(_dot(sums, jnp.concatenate([side_by_side(lf_hi), side_by_side(lf_lo)], axis=0)))
    e_b = e[0:C]
    decay_end = e_b[C - 1:C, :]
    q_in = (qf_w * e_b).astype(BF16)
    k_out = (kk_w * e[(N_SUMS - 1) * C:N_SUMS * C]).astype(BF16)
    z = [(jnp.where((step & w) != 0, qf_w, kk_w) * e[li * C:(li + 1) * C]).astype(BF16)
         for li, w in enumerate(HG_LEVELS, start=1)]
    unit_cols = [[slice(ch * W + hd * HG_D, ch * W + (hd + 1) * HG_D) for hd in range(HG_HEADS)]
                 for ch in range(n_chunks)]
    scores = []
    for ch in range(n_chunks):
        scores.append([])
        for cols in unit_cols[ch]:
            s = masks_ref[0] * _dot_nt(qb_w[:, cols], kb_w[:, cols])
            for li in range(1, len(HG_LEVELS) + 1):
                zl = z[li - 1][:, cols]
                s += masks_ref[li] * _dot_nt(zl, zl)
            scores[ch].append(s.astype(BF16))

    for ch in range(n_chunks):
        rows = slice(ch * C, (ch + 1) * C)
        for hd, cols in enumerate(head_cols):
            wide = unit_cols[ch][hd]
            st = state_ref[hd]
            v_h = vb[rows, cols]
            o = _dot_nt(q_in[:, wide], st.astype(BF16)) + _dot(scores[ch][hd], v_h)
            state_ref[hd] = st * decay_end[:, wide] + _dot_tn(v_h, k_out[:, wide])
            o = _rms(o, nw) * gate[rows, cols]
            o_ref[rows, cols] = o.astype(o_ref.dtype)


def _hgrn(hg, loga, log1m, norm_w, batch, length):
    hg3 = hg.reshape(batch, length, HG_COLS)
    out = pl.pallas_call(
        _hgrn_kernel,
        grid=(batch, length // HG_TILE),
        in_specs=[
            pl.BlockSpec((None, HG_TILE, HG_COLS), lambda b, c: (b, c, 0)),
            pl.BlockSpec((1, HG_WIDTH), lambda b, c: (0, 0)),
            pl.BlockSpec((1, HG_WIDTH), lambda b, c: (0, 0)),
            pl.BlockSpec((1, HG_D), lambda b, c: (0, 0)),
            pl.BlockSpec(_HG_SUMS.shape, lambda b, c: (0, 0)),
            pl.BlockSpec(_HG_MASKS.shape, lambda b, c: (0, 0, 0)),
        ],
        out_specs=pl.BlockSpec((None, HG_TILE, HG_WIDTH), lambda b, c: (b, c, 0)),
        out_shape=jax.ShapeDtypeStruct((batch, length, HG_WIDTH), BF16),
        scratch_shapes=[pltpu.VMEM((HG_HEADS, HG_D, HG_D), F32)],
        compiler_params=_params("parallel", "arbitrary"),
        name="hgrn2",
    )(hg3, loga, log1m, norm_w, jnp.asarray(_HG_SUMS, BF16), jnp.asarray(_HG_MASKS, F32))
    return out.reshape(batch * length, HG_WIDTH)


def _attn_kernel(q_ref, k_ref, v_ref, toe_ref, cst_ref, w_ref, o_ref, s_ref, *, n_blocks):
    lam = cst_ref[0:1, 0:1]
    post = cst_ref[1:2, :]
    lane = lax.broadcasted_iota(jnp.int32, (Q_BLOCK, Q_BLOCK), 1)
    first_half = lane < DA_DQK
    key_ok0 = lane >= (LEAD - N_META)
    inert_bias = jnp.where(key_ok0, 0.0, NEG)

    def near_bias(kind, kb):
        bias = toe_ref[kind]
        if kb == 0:
            bias = jnp.where(key_ok0, bias, NEG)
        return bias

    def slabs(x):
        return [x[:, c:c + Q_BLOCK] for c in range(0, x.shape[1], Q_BLOCK)]

    map_rows = 2 * Q_BLOCK
    groups = [tuple(range(i, min(i + Q_GROUP, n_blocks))) for i in range(0, n_blocks, Q_GROUP)]

    def both_maps(bias):
        return jnp.concatenate([bias, bias], axis=0)

    def stacked_q(blocks):
        parts = []
        for i in blocks:
            qi = q_ref[i * Q_BLOCK:(i + 1) * Q_BLOCK, :]
            zero = jnp.zeros_like(qi)
            parts += [jnp.where(first_half, qi, zero), jnp.where(first_half, zero, qi)]
        return jnp.concatenate(parts, axis=0)

    def group_tiles(blocks):
        first, last = blocks[0], blocks[-1]
        tiles = []
        far_end = max(first - 1, 0)
        if far_end >= 1:
            tiles.append((0, Q_BLOCK, jnp.concatenate([inert_bias] * (2 * len(blocks)), axis=0), 0))
        kb = 1
        while kb < far_end:
            width = KEY_TILE
            while kb + width // Q_BLOCK > far_end:
                width //= 2
            tiles.append((kb * Q_BLOCK, width, None, 0))
            kb += width // Q_BLOCK
        for kb in range(far_end, last + 1):
            biases, first_row = [], None
            for r, qb in enumerate(blocks):
                if kb > qb:
                    continue
                if first_row is None:
                    first_row = r * map_rows
                if kb == qb:
                    bias = near_bias(0, kb)
                elif kb == qb - 1:
                    bias = near_bias(1, kb)
                else:
                    bias = inert_bias if kb == 0 else jnp.zeros_like(inert_bias)
                biases.append(both_maps(bias))
            tiles.append((kb * Q_BLOCK, Q_BLOCK, jnp.concatenate(biases, axis=0), first_row))
        return tiles

    def merge(acc, x, first_row, op):
        if acc is None:
            return x
        if first_row == 0:
            return op(acc, x)
        return jnp.concatenate([acc[:first_row], op(acc[first_row:], x)], axis=0)

    def sweep_scores(g):
        blocks = groups[g]
        rows = len(blocks) * map_rows
        q2 = stacked_q(blocks)
        m_acc = None
        for start, width, bias, first_row in group_tiles(blocks):
            s = _dot_nt(q2[first_row:], k_ref[start:start + width, :])
            if bias is not None:
                s = s + bias
            s_ref[g % 2, first_row:rows, start:start + width] = s
            for slab in slabs(s):
                m_acc = merge(m_acc, slab, first_row, jnp.maximum)
        return m_acc.max(axis=-1, keepdims=True)

    row_max = sweep_scores(0)
    for g, blocks in enumerate(groups):
        rows = len(blocks) * map_rows
        m = row_max
        if g + 1 < len(groups):
            row_max = sweep_scores(g + 1)
        l_acc = o_acc = None
        for start, width, _, first_row in group_tiles(blocks):
            p = jnp.exp2(s_ref[g % 2, first_row:rows, start:start + width] - m[first_row:])
            for slab in slabs(p):
                l_acc = merge(l_acc, slab, first_row, jnp.add)
            o_acc = merge(o_acc, _dot(p.astype(BF16), v_ref[start:start + width, :]), first_row, jnp.add)
        o2 = o_acc * (1.0 / l_acc.sum(axis=-1, keepdims=True))
        for r, i in enumerate(blocks):
            o = o2[r * map_rows:r * map_rows + Q_BLOCK] - lam * o2[r * map_rows + Q_BLOCK:(r + 1) * map_rows]
            o = _rms(o, w_ref[...]) * post
            o_ref[i * Q_BLOCK:(i + 1) * Q_BLOCK, :] = o.astype(o_ref.dtype)


def _attn(da, toe, cst, subln_w, batch, length):
    da3 = da.reshape(batch, length, DA_COLS)
    hw = DA_HEADS
    out = pl.pallas_call(
        functools.partial(_attn_kernel, n_blocks=length // Q_BLOCK),
        grid=(batch, DA_HEADS),
        in_specs=[
            pl.BlockSpec((None, length, DA_DV), lambda b, h: (b, 0, h)),
            pl.BlockSpec((None, length, DA_DV), lambda b, h: (b, 0, hw + h)),
            pl.BlockSpec((None, length, DA_DV), lambda b, h: (b, 0, 2 * hw + h)),
            pl.BlockSpec((None, 2, Q_BLOCK, Q_BLOCK), lambda b, h: (h, 0, 0, 0)),
            pl.BlockSpec((SUBLANES, LANES), lambda b, h: (0, 0)),
            pl.BlockSpec((1, DA_DV), lambda b, h: (0, 0)),
        ],
        out_specs=pl.BlockSpec((None, length, DA_DV), lambda b, h: (b, 0, h)),
        out_shape=jax.ShapeDtypeStruct((batch, length, DA_HEADS * DA_DV), BF16),
        scratch_shapes=[pltpu.VMEM((2, Q_GROUP * 2 * Q_BLOCK, length), F32)],
        compiler_params=_params("parallel", "parallel"),
        name="diff_attn",
    )(da3, da3, da3, toe, cst, subln_w)
    return out.reshape(batch * length, DA_HEADS * DA_DV)


def _t5_bucket(dist):
    n = jnp.maximum(dist, 0)
    max_exact = REL_BUCKETS // 2
    nf = jnp.maximum(n, max_exact).astype(F32)
    large = max_exact + (jnp.log(nf / max_exact) / math.log(REL_MAX_DIST / max_exact)
                         * (REL_BUCKETS - max_exact)).astype(jnp.int32)
    large = jnp.minimum(large, REL_BUCKETS - 1)
    return jnp.where(n < max_exact, n, large)


def _attn_bias_tables(rel_bias):
    tab = rel_bias.astype(F32)
    qi = jnp.arange(Q_BLOCK, dtype=jnp.int32)[:, None]
    ki = jnp.arange(Q_BLOCK, dtype=jnp.int32)[None, :]

    def lookup(bucket):
        onehot = bucket[None, :, :, None] == jnp.arange(REL_BUCKETS, dtype=jnp.int32)
        return jnp.sum(jnp.where(onehot, tab.T[:, None, None, :], 0.0), axis=-1)

    far = tab[REL_BUCKETS - 1][:, None, None]
    diag = jnp.where((ki <= qi)[None], (lookup(_t5_bucket(qi - ki)) - far) * LOG2E, NEG)
    prev = (lookup(_t5_bucket(qi - ki + Q_BLOCK)) - far) * LOG2E
    return jnp.stack([diag, prev], axis=1)


def _out_proj_router_kernel(h_ref, ohg_ref, oda_ref, wo_ref, g_ref, router_ref, before_ref,
                            hn_ref, u_ref, route_ref, gate_ref, count_ref, seen_ref):
    @pl.when(pl.program_id(0) == 0)
    def _():
        seen_ref[...] = jnp.zeros_like(seen_ref)

    hn = (h_ref[...] + _dot(ohg_ref[...], wo_ref[:HG_WIDTH, :])
          + _dot(oda_ref[...], wo_ref[HG_WIDTH:, :]))
    hn_ref[...] = hn
    u = _rms(hn, g_ref[...])
    _store_token_major(u_ref, u)
    u_hi = u.astype(BF16)
    u_lo = (u - u_hi.astype(F32)).astype(BF16)
    logits = _dot(jnp.concatenate([u_hi, u_lo, u_hi], axis=1), router_ref[...])
    logits = logits.T[:N_EXPERTS]
    expert = lax.broadcasted_iota(jnp.int32, logits.shape, 0)

    def top(x):
        best = x.max(axis=0, keepdims=True)
        return best, jnp.where(x == best, expert, N_EXPERTS).min(axis=0, keepdims=True)

    l1, e1 = top(logits)
    l2, e2 = top(jnp.where(expert == e1, -jnp.inf, logits))
    w2 = jnp.exp(l2 - l1)
    g1 = 1.0 / (1.0 + w2)
    g2 = w2 / (1.0 + w2)

    pick1 = (expert == e1).astype(F32)
    pick2 = (expert == e2).astype(F32)
    picked = pick1 + pick2
    ahead = seen_ref[:, 0:1] + _dot(picked.astype(BF16), before_ref[...])
    rank1 = jnp.sum(pick1 * ahead, axis=0, keepdims=True).astype(jnp.int32)
    rank2 = jnp.sum(pick2 * ahead, axis=0, keepdims=True).astype(jnp.int32)
    seen = seen_ref[...] + jnp.sum(picked, axis=1, keepdims=True)
    seen_ref[...] = seen

    route_ref[...] = jnp.where(expert == 0, e1, jnp.where(expert == 1, e2, jnp.where(
        expert == 2, rank1, jnp.where(expert == 3, rank2, 0))))
    gate_ref[...] = jnp.where(expert == 0, g1, jnp.where(expert == 1, g2, 0.0)).T
    count_ref[...] = seen.astype(jnp.int32)


def _out_proj_router(h, o_hg, o_da, wo, layer, gain, router):
    n = h.shape[0]
    row = lambda i: (i, 0)
    full = lambda i: (0, 0)
    assert N_EXPERTS >= 2 * TOP_K
    lanes_of = lambda i: (0, i)
    before = jnp.asarray(np.triu(np.ones((ROW_TILE, ROW_TILE), np.float32), 1), BF16)
    return pl.pallas_call(
        _out_proj_router_kernel,
        grid=(n // ROW_TILE,),
        in_specs=[
            pl.BlockSpec((ROW_TILE, D_MODEL), row),
            pl.BlockSpec((ROW_TILE, HG_WIDTH), row),
            pl.BlockSpec((ROW_TILE, DA_HEADS * DA_DV), row),
            pl.BlockSpec((None,) + wo.shape[1:], lambda i: (layer, 0, 0)),
            pl.BlockSpec((1, D_MODEL), full),
            pl.BlockSpec(router.shape, full),
            pl.BlockSpec(before.shape, full),
        ],
        out_specs=[
            pl.BlockSpec((ROW_TILE, D_MODEL), row),
            pl.BlockSpec((ROW_TILE * TOKEN_ROWS, LANES), row),
            pl.BlockSpec((N_EXPERTS, ROW_TILE), lanes_of),
            pl.BlockSpec((ROW_TILE, N_EXPERTS), row),
            pl.BlockSpec((N_EXPERTS, LANES), full),
        ],
        out_shape=[
            jax.ShapeDtypeStruct((n, D_MODEL), F32),
            jax.ShapeDtypeStruct((n * TOKEN_ROWS, LANES), F32),
            jax.ShapeDtypeStruct((N_EXPERTS, n), jnp.int32),
            jax.ShapeDtypeStruct((n, N_EXPERTS), F32),
            jax.ShapeDtypeStruct((N_EXPERTS, LANES), jnp.int32),
        ],
        scratch_shapes=[pltpu.VMEM((N_EXPERTS, LANES), F32)],
        compiler_params=_params("arbitrary"),
        name="out_proj_router",
    )(h, o_hg, o_da, wo, gain, router, before)


def _dense_layer_kernel(h_ref, ohg_ref, oda_ref, wo_ref, g_ref, w1_ref, w3_ref, w2_ref, *rest):
    n_cast = (len(rest) - 1) // 2
    o_ref = rest[n_cast]
    hn = (h_ref[...] + _dot(ohg_ref[...], wo_ref[:HG_WIDTH, :])
          + _dot(oda_ref[...], wo_ref[HG_WIDTH:, :]))
    u = _rms(hn, g_ref[...]).astype(BF16)
    a = _dot(u, w1_ref[...])
    act = (_silu(a) * _dot(u, w3_ref[...])).astype(BF16)
    o_ref[...] = hn + _dot(act, w2_ref[...])
    _run_cast_jobs(rest[:n_cast], rest[n_cast + 1:])


def _dense_layer(h, o_hg, o_da, wo, layer, gain, w1, w3, w2, idx, to_cast, cast_idx):
    n = h.shape[0]
    steps = n // ROW_TILE
    row = lambda i: (i, 0)

    def resident(w, index):
        return pl.BlockSpec((None,) + w.shape[1:], lambda i: (index, 0, 0),
                            pipeline_mode=pl.Buffered(1))

    jobs = [(w.reshape(w.shape[0], w.shape[1] * w.shape[2], w.shape[3]), cast_idx) for w in to_cast]
    cast_in, cast_in_specs, cast_out_specs, cast_out_shape = _cast_jobs(jobs, steps)

    out = pl.pallas_call(
        _dense_layer_kernel,
        grid=(steps,),
        in_specs=[
            pl.BlockSpec((ROW_TILE, D_MODEL), row),
            pl.BlockSpec((ROW_TILE, HG_WIDTH), row),
            pl.BlockSpec((ROW_TILE, DA_HEADS * DA_DV), row),
            resident(wo, layer),
            pl.BlockSpec((1, D_MODEL), lambda i: (0, 0)),
            resident(w1, idx),
            resident(w3, idx),
            resident(w2, idx),
        ] + cast_in_specs,
        out_specs=[pl.BlockSpec((ROW_TILE, D_MODEL), row)] + cast_out_specs,
        out_shape=[jax.ShapeDtypeStruct((n, D_MODEL), F32)] + cast_out_shape,
        compiler_params=_params("arbitrary"),
        name="dense_layer",
    )(h, o_hg, o_da, wo, gain, w1, w3, w2, *cast_in)
    return out[0], [b.reshape(w.shape[1:]) for b, w in zip(out[1:], to_cast)]


DISPATCH_TILE = 1024
ISSUE_TOKENS = 8


def _token_copy(src_ref, src_tok, dst_ref, dst_tok, sem):
    src = pl.multiple_of(src_tok * TOKEN_ROWS, TOKEN_ROWS)
    dst = pl.multiple_of(dst_tok * TOKEN_ROWS, TOKEN_ROWS)
    return pltpu.make_async_copy(src_ref.at[pl.ds(src, TOKEN_ROWS), :],
                                 dst_ref.at[pl.ds(dst, TOKEN_ROWS), :], sem)


def _dispatch_kernel(dest_ref, empty_ref, u_ref, xs_ref, zero_ref, sem):
    step = pl.program_id(0)
    base = step * DISPATCH_TILE
    tile_rows = DISPATCH_TILE * TOKEN_ROWS

    def retire_tile():
        pltpu.make_async_copy(u_ref, xs_ref.at[pl.ds(0, tile_rows), :], sem).wait()

    @pl.when(step == 0)
    def _():
        zero_ref[...] = jnp.zeros_like(zero_ref)

        def clear(g, carry):
            slots = [empty_ref[g * ISSUE_TOKENS * TOP_K + j] for j in range(ISSUE_TOKENS * TOP_K)]
            for j, slot in enumerate(slots):
                _token_copy(zero_ref, 0, xs_ref, slot, sem).start(priority=j % 2)
            return carry

        n_empty = empty_ref.shape[0]
        lax.fori_loop(0, n_empty // (ISSUE_TOKENS * TOP_K), clear, 0)
        for _ in range(n_empty // DISPATCH_TILE):
            retire_tile()

    n_tokens = dest_ref.shape[0] // TOP_K

    def start(g, carry):
        r0 = g * ISSUE_TOKENS
        slots = [dest_ref[(j % TOP_K) * n_tokens + base + r0 + j // TOP_K]
                 for j in range(ISSUE_TOKENS * TOP_K)]
        for j, slot in enumerate(slots):
            _token_copy(u_ref, r0 + j // TOP_K, xs_ref, slot, sem).start(priority=j % 2)
        return carry

    lax.fori_loop(0, DISPATCH_TILE // ISSUE_TOKENS, start, 0)
    for _ in range(TOP_K):
        retire_tile()


def _dispatch(dest, empty_slots, u_tm, n_slots):
    n = dest.shape[0] // TOP_K
    assert empty_slots.shape[0] % DISPATCH_TILE == 0
    return pl.pallas_call(
        _dispatch_kernel,
        grid_spec=pltpu.PrefetchScalarGridSpec(
            num_scalar_prefetch=2,
            grid=(n // DISPATCH_TILE,),
            in_specs=[pl.BlockSpec((DISPATCH_TILE * TOKEN_ROWS, LANES), lambda i, d, e: (i, 0))],
            out_specs=pl.BlockSpec(memory_space=pl.ANY),
            scratch_shapes=[pltpu.VMEM((TOKEN_ROWS, LANES), u_tm.dtype),
                            pltpu.SemaphoreType.DMA(())],
        ),
        out_shape=jax.ShapeDtypeStruct((n_slots * TOKEN_ROWS, LANES), u_tm.dtype),
        compiler_params=_params("arbitrary"),
        name="moe_dispatch",
    )(dest, empty_slots, u_tm)


def _expert_kernel(be_ref, nused_ref, xs_ref, w1_ref, w3_ref, w2_ref, y_ref):
    del be_ref
    blk = pl.program_id(0)

    @pl.when(blk < nused_ref[0])
    def _():
        x = _load_token_major(xs_ref, MOE_TILE).astype(BF16)
        acc = jnp.zeros((MOE_TILE, D_MODEL), F32)
        for c0 in range(0, D_FF_EXPERT, FF_CHUNK):
            a = _dot(x, w1_ref[:, c0:c0 + FF_CHUNK])
            act = (_silu(a) * _dot(x, w3_ref[:, c0:c0 + FF_CHUNK])).astype(BF16)
            acc = acc + _dot(act, w2_ref[c0:c0 + FF_CHUNK, :])
        _store_token_major(y_ref, acc)

    @pl.when(blk >= nused_ref[0])
    def _():
        y_ref[...] = jnp.zeros_like(y_ref)


def _experts(block_expert, n_used, xs, w1, w3, w2):
    n_blocks = xs.shape[0] // (MOE_TILE * TOKEN_ROWS)
    wmap = lambda i, be, nu: (be[i], 0, 0)
    slots = pl.BlockSpec((MOE_TILE * TOKEN_ROWS, LANES), lambda i, be, nu: (i, 0))
    return pl.pallas_call(
        _expert_kernel,
        grid_spec=pltpu.PrefetchScalarGridSpec(
            num_scalar_prefetch=2,
            grid=(n_blocks,),
            in_specs=[
                slots,
                pl.BlockSpec((None, D_MODEL, D_FF_EXPERT), wmap),
                pl.BlockSpec((None, D_MODEL, D_FF_EXPERT), wmap),
                pl.BlockSpec((None, D_FF_EXPERT, D_MODEL), wmap),
            ],
            out_specs=slots,
        ),
        out_shape=jax.ShapeDtypeStruct(xs.shape, F32),
        compiler_params=_params("arbitrary"),
        name="moe_experts",
    )(block_expert, n_used, xs, w1, w3, w2)


def _combine_final_kernel(dest_ref, h_ref, gate_ref, y_ref, g_ref, o_ref, buf_ref, sem):
    step = pl.program_id(0)
    n_tokens = dest_ref.shape[0] // TOP_K

    def issue(s, slot):
        base = s * Q_BLOCK

        def body(g, carry):
            r0 = g * ISSUE_TOKENS
            slots = [dest_ref[(j % TOP_K) * n_tokens + base + r0 + j // TOP_K]
                     for j in range(ISSUE_TOKENS * TOP_K)]
            for j, src in enumerate(slots):
                _token_copy(y_ref, src, buf_ref.at[slot, j % TOP_K], r0 + j // TOP_K,
                            sem.at[slot]).start(priority=j % 2)
            return carry

        lax.fori_loop(0, Q_BLOCK // ISSUE_TOKENS, body, 0)

    @pl.when(step == 0)
    def _():
        issue(0, 0)

    for slot in range(2):
        @pl.when(step % 2 == slot)
        def _():
            @pl.when(step + 1 < pl.num_programs(0))
            def _():
                issue(step + 1, 1 - slot)

            for k in range(TOP_K):
                pltpu.make_async_copy(y_ref.at[pl.ds(0, Q_BLOCK * TOKEN_ROWS), :],
                                      buf_ref.at[slot, k], sem.at[slot]).wait()
            gate = gate_ref[...]
            h = (h_ref[...]
                 + gate[:, 0:1] * _load_token_major(buf_ref.at[slot, 0], Q_BLOCK)
                 + gate[:, 1:2] * _load_token_major(buf_ref.at[slot, 1], Q_BLOCK))
            o_ref[...] = _rms(h, g_ref[...])


def _combine_final(dest, h, gate, y, gain, batch, length, seq):
    n = h.shape[0]
    per_seq = length // Q_BLOCK
    lead_blocks = LEAD // Q_BLOCK
    return pl.pallas_call(
        _combine_final_kernel,
        grid_spec=pltpu.PrefetchScalarGridSpec(
            num_scalar_prefetch=1,
            grid=(n // Q_BLOCK,),
            in_specs=[
                pl.BlockSpec((Q_BLOCK, D_MODEL), lambda i, d: (i, 0)),
                pl.BlockSpec((Q_BLOCK, N_EXPERTS), lambda i, d: (i, 0)),
                pl.BlockSpec(memory_space=pl.ANY),
                pl.BlockSpec((1, D_MODEL), lambda i, d: (0, 0)),
            ],
            out_specs=pl.BlockSpec(
                (None, Q_BLOCK, D_MODEL),
                lambda i, d: (i // per_seq, jnp.maximum(i % per_seq - lead_blocks, 0), 0)),
            scratch_shapes=[pltpu.VMEM((2, TOP_K, Q_BLOCK * TOKEN_ROWS, LANES), F32),
                            pltpu.SemaphoreType.DMA((2,))],
        ),
        out_shape=jax.ShapeDtypeStruct((batch, seq, D_MODEL), F32),
        compiler_params=_params("arbitrary"),
        name="moe_combine_final",
    )(dest, h, gate, y, gain)


def _prefix_sum(x):
    k = x.shape[0]
    keep = (np.arange(k)[None, :] <= np.arange(k)[:, None]).reshape((k, k) + (1,) * (x.ndim - 1))
    return jnp.sum(jnp.where(keep, x[None], jnp.zeros_like(x[None])), axis=1)


def _slot_tables(route, counts, n_slots):
    n = route.shape[1]
    flat_e = route[0:TOP_K].reshape(n * TOP_K)
    rank = route[TOP_K:2 * TOP_K].reshape(n * TOP_K)
    counts = counts[:, 0]
    padded = (counts + MOE_TILE - 1) // MOE_TILE * MOE_TILE
    pad_end = _prefix_sum(padded)
    pad_start = pad_end - padded
    onehot = flat_e[:, None] == jnp.arange(N_EXPERTS, dtype=jnp.int32)[None, :]
    dest = (rank + jnp.sum(jnp.where(onehot, pad_start[None, :], 0), axis=-1)).astype(jnp.int32)
    n_blocks = n_slots // MOE_TILE
    block_first = jnp.arange(n_blocks, dtype=jnp.int32) * MOE_TILE
    block_expert = jnp.minimum(
        jnp.sum((pad_end[None, :] <= block_first[:, None]).astype(jnp.int32), axis=-1),
        N_EXPERTS - 1).astype(jnp.int32)
    n_used = (pad_end[-1:] // MOE_TILE).astype(jnp.int32)
    n_empty = n_slots - n * TOP_K
    gap_end = _prefix_sum(padded - counts)
    j = jnp.arange(n_empty, dtype=jnp.int32)
    owner = jnp.sum((j[:, None] >= gap_end[None, :]).astype(jnp.int32), axis=-1)
    first_empty = jnp.concatenate([pad_start + counts, pad_end[-1:]])
    gap_start = jnp.concatenate([jnp.zeros((1,), gap_end.dtype), gap_end])
    pick = owner[:, None] == jnp.arange(N_EXPERTS + 1, dtype=jnp.int32)[None, :]
    empty_slots = (j + jnp.sum(jnp.where(pick, (first_empty - gap_start)[None, :], 0), axis=-1))
    return dest, block_expert, n_used, empty_slots.astype(jnp.int32)


def _moe_ffn(u, route, counts, w1, w3, w2):
    n = route.shape[1]
    n_slots = (n * TOP_K // MOE_TILE + N_EXPERTS) * MOE_TILE
    dest, block_expert, n_used, empty_slots = _slot_tables(route, counts, n_slots)
    xs = _dispatch(dest, empty_slots, u, n_slots)
    y = _experts(block_expert, n_used, xs, w1, w3, w2)
    return dest, y


def kernel(x, meta, rel_bias, norm_mix, w_in, hg_lb_logits, hg_norm_w, da_lambda, da_subln_w, w_out, norm_ffn, dense_w1, dense_w3, dense_w2, moe_router, moe_w1, moe_w3, moe_w2, final_norm):
    batch, seq, d = x.shape
    length = LEAD + seq
    h = jnp.concatenate([
        jnp.zeros((batch, LEAD - N_META, d), x.dtype),
        jnp.broadcast_to(meta[None].astype(x.dtype), (batch, N_META, d)),
        x], axis=1).reshape(batch * length, d)

    toe = _attn_bias_tables(rel_bias)
    lb_cum = _prefix_sum(jax.nn.softmax(hg_lb_logits.astype(F32), axis=0))
    lb_all = jnp.clip(lb_cum - lb_cum[0:1], 0.0, LB_MAX)
    log_lb = jnp.log(lb_all)
    log_1m_lb = jnp.log1p(-lb_all)

    assert DEPTH % 2 == 0
    dense_f32 = (dense_w1, dense_w3, dense_w2)
    stacked_rows = lambda w: w.reshape(1, w.shape[0] * w.shape[1], w.shape[2])

    unsummed = None
    for l in range(DEPTH):
        if l == 0:
            jobs = [(w, 0) for w in dense_f32] + [(stacked_rows(w_out), 0), (stacked_rows(w_in), 0)]
            hg, da, casts = _mix_in(h, norm_mix[l][None], w_in[0:1].astype(BF16), 0, jobs)
            dense_b = [w[None] for w in casts[:3]]
            w_out_b = casts[3].reshape(w_out.shape)
            w_in_b = casts[4].reshape(w_in.shape)
        elif unsummed is None:
            jobs = [(w, (l + 1) // 2) for w in dense_f32] if l + 1 < DEPTH else []
            hg, da, casts = _mix_in(h, norm_mix[l][None], w_in_b, l, jobs)
            if jobs:
                dense_b = [w[None] for w in casts]
        else:
            h, hg, da = _mix_in_combine(*unsummed, norm_mix[l][None], w_in_b, l)
            unsummed = None
        lam_init = 0.8 - 0.6 * math.exp(-0.3 * l)
        lv = da_lambda[l].astype(F32)
        lam = jnp.exp(jnp.sum(lv[0] * lv[1])) - jnp.exp(jnp.sum(lv[2] * lv[3])) + lam_init
        cst = jnp.zeros((SUBLANES, LANES), F32).at[0].set(lam).at[1].set(1.0 - lam_init)
        o_hg = _hgrn(hg, log_lb[l][None], log_1m_lb[l][None], hg_norm_w[l][None], batch, length)
        o_da = _attn(da, toe, cst, da_subln_w[l][None], batch, length)
        i = l // 2
        if l % 2 == 0:
            h, moe_b = _dense_layer(h, o_hg, o_da, w_out_b, l, norm_ffn[l][None], *dense_b, 0,
                                    (moe_w1, moe_w3, moe_w2), i)
        else:
            router = jnp.zeros((d, LANES), F32).at[:, :N_EXPERTS].set(moe_router[i].astype(F32))
            r_hi = router.astype(BF16)
            r_lo = (router - r_hi.astype(F32)).astype(BF16)
            router = jnp.concatenate([r_hi, r_hi, r_lo], axis=0)
            hn, u, route, gate, counts = _out_proj_router(h, o_hg, o_da, w_out_b, l,
                                                          norm_ffn[l][None], router)
            dest, y = _moe_ffn(u, route, counts, *moe_b)
            if l + 1 < DEPTH:
                unsummed = (dest, hn, gate, y)
            else:
                return _combine_final(dest, hn, gate, y, final_norm[None], batch, length, seq)
```

```python
import functools
import math

import jax
import jax.numpy as jnp
import numpy as np
from jax import lax
from jax.experimental import pallas as pl
from jax.experimental.pallas import tpu as pltpu

D_MODEL = 1024
DEPTH = 4
N_META = 16
LEAD = 128
HG_WIDTH = 512
HG_HEADS = 4
HG_D = 128
HG_CHUNK = 64
DA_HEADS = 4
DA_DQK = 64
DA_DV = 128
Q_BLOCK = 128
KEY_TILE = 4 * Q_BLOCK
Q_GROUP = 2
LOG2E = math.log2(math.e)
Q_SCALE = DA_DQK ** -0.5 * LOG2E
REL_BUCKETS = 32
REL_MAX_DIST = 128
N_EXPERTS = 8
TOP_K = 2
D_FF_EXPERT = 3584
EPS = 1e-6
NEG = -1e30
LB_MAX = 0.999
HG_COLS = 4 * HG_WIDTH
DA_COLS = 3 * DA_HEADS * DA_DV
W_IN_COLS = HG_COLS + DA_COLS

LANES = 128
SUBLANES = 8
VMEM_LIMIT = 56 * 1024 * 1024

ROW_TILE = 256
MIX_TILE = 512
HG_TILE = 128
MOE_TILE = 256
FF_CHUNK = 1792

F32 = jnp.float32
BF16 = jnp.bfloat16


def _params(*sem):
    return pltpu.CompilerParams(dimension_semantics=sem, vmem_limit_bytes=VMEM_LIMIT)


def _dot(a, b):
    return jnp.dot(a, b, preferred_element_type=F32)


def _dot_nt(a, b):
    return lax.dot_general(a, b, (((1,), (1,)), ((), ())), preferred_element_type=F32)


def _dot_tn(a, b):
    return lax.dot_general(a, b, (((0,), (0,)), ((), ())), preferred_element_type=F32)


def _rms(x, gain):
    return x * lax.rsqrt(jnp.mean(x * x, axis=-1, keepdims=True) + EPS) * gain


TOKEN_ROWS = D_MODEL // LANES


def _store_token_major(ref, x):
    t = x.shape[0]
    for s in range(TOKEN_ROWS):
        ref[pl.ds(s, t, stride=TOKEN_ROWS), :] = x[:, s * LANES:(s + 1) * LANES]


def _load_token_major(ref, t):
    return jnp.concatenate(
        [ref[pl.ds(s, t, stride=TOKEN_ROWS), :] for s in range(TOKEN_ROWS)], axis=1)


def _silu(x):
    return x * (0.5 * jnp.tanh(0.5 * x) + 0.5)


BF16_SUBLANES = 16


def _cast_jobs(jobs, grid_steps):
    inputs, in_specs, out_specs, out_shapes = [], [], [], []
    for w, index in jobs:
        _, rows, cols = w.shape
        steps = max(s for s in range(1, grid_steps + 1)
                    if rows % s == 0 and (rows // s) % BF16_SUBLANES == 0)
        slab_rows = rows // steps
        in_specs.append(pl.BlockSpec((None, slab_rows, cols),
                                     lambda i, index=index, steps=steps: (index, jnp.minimum(i, steps - 1), 0)))
        out_specs.append(pl.BlockSpec((slab_rows, cols),
                                      lambda i, steps=steps: (jnp.minimum(i, steps - 1), 0)))
        out_shapes.append(jax.ShapeDtypeStruct((rows, cols), BF16))
        inputs.append(w)
    return inputs, in_specs, out_specs, out_shapes


def _run_cast_jobs(src_refs, dst_refs):
    for src_ref, dst_ref in zip(src_refs, dst_refs):
        dst_ref[...] = src_ref[...].astype(BF16)


def _mix_in_kernel(x_ref, g_ref, w_ref, *rest):
    n_cast = (len(rest) - 2) // 2
    hg_ref, da_ref = rest[n_cast:n_cast + 2]
    u = _rms(x_ref[...], g_ref[...]).astype(BF16)
    hg_ref[...] = _dot(u, w_ref[:, :HG_COLS])
    n_q = DA_HEADS * 2 * DA_DQK
    da_ref[:, :n_q] = (_dot(u, w_ref[:, HG_COLS:HG_COLS + n_q]) * Q_SCALE).astype(BF16)
    da_ref[:, n_q:] = _dot(u, w_ref[:, HG_COLS + n_q:]).astype(BF16)
    _run_cast_jobs(rest[:n_cast], rest[n_cast + 2:])


def _mix_in(h, gain, w, layer, to_cast=()):
    n = h.shape[0]
    steps = n // MIX_TILE
    cast_in, cast_in_specs, cast_out_specs, cast_out_shapes = _cast_jobs(to_cast, steps)
    out = pl.pallas_call(
        _mix_in_kernel,
        grid=(steps,),
        in_specs=[
            pl.BlockSpec((MIX_TILE, D_MODEL), lambda i: (i, 0)),
            pl.BlockSpec((1, D_MODEL), lambda i: (0, 0)),
            pl.BlockSpec((None, D_MODEL, W_IN_COLS), lambda i: (layer, 0, 0),
                         pipeline_mode=pl.Buffered(1)),
        ] + cast_in_specs,
        out_specs=[
            pl.BlockSpec((MIX_TILE, HG_COLS), lambda i: (i, 0)),
            pl.BlockSpec((MIX_TILE, DA_COLS), lambda i: (i, 0)),
        ] + cast_out_specs,
        out_shape=[
            jax.ShapeDtypeStruct((n, HG_COLS), F32),
            jax.ShapeDtypeStruct((n, DA_COLS), BF16),
        ] + cast_out_shapes,
        compiler_params=_params("arbitrary"),
        name="mix_in",
    )(h, gain, w, *cast_in)
    return out[0], out[1], list(out[2:])


def _mix_in_combine_kernel(dest_ref, hn_ref, gate_ref, y_ref, g_ref, w_ref,
                           h_ref, hg_ref, da_ref, buf_ref, sem):
    step = pl.program_id(0)
    last = pl.num_programs(0) - 1
    slot = step % 2
    tile_rows = MIX_TILE * TOKEN_ROWS
    n_tokens = dest_ref.shape[0] // TOP_K

    def start_gathers(s, to_slot, rows):
        base = s * MIX_TILE
        slots = [dest_ref[k * n_tokens + base + r] for r in rows for k in range(TOP_K)]
        for j, src in enumerate(slots):
            r, k = rows[j // TOP_K], j % TOP_K
            _token_copy(y_ref, src, buf_ref.at[to_slot, k], r, sem.at[to_slot]).start(priority=j % 2)

    def wait_gathers(of_slot):
        for k in range(TOP_K):
            pltpu.make_async_copy(y_ref.at[pl.ds(0, tile_rows), :], buf_ref.at[of_slot, k],
                                  sem.at[of_slot]).wait()

    @pl.when(step == 0)
    def _():
        def first(g, carry):
            start_gathers(0, 0, [g * ISSUE_TOKENS + r for r in range(ISSUE_TOKENS)])
            return carry

        lax.fori_loop(0, MIX_TILE // ISSUE_TOKENS, first, 0)

    wait_gathers(slot)
    gate = gate_ref[...]
    h = (hn_ref[...] + gate[:, 0:1] * _load_token_major(buf_ref.at[slot, 0], MIX_TILE)
         + gate[:, 1:2] * _load_token_major(buf_ref.at[slot, 1], MIX_TILE))
    h_ref[...] = h
    u = _rms(h, g_ref[...]).astype(BF16)

    nxt = jnp.minimum(step + 1, last)
    n_q = DA_HEADS * 2 * DA_DQK
    groups = [(c, c + HG_WIDTH) for c in range(0, HG_COLS, HG_WIDTH)]
    groups += [(HG_COLS, HG_COLS + n_q), (HG_COLS + n_q, W_IN_COLS)]
    per_group = -(-MIX_TILE // len(groups))
    for gi, (c0, c1) in enumerate(groups):
        proj = _dot(u, w_ref[:, c0:c1])
        if c1 <= HG_COLS:
            hg_ref[:, c0:c1] = proj
        elif c0 == HG_COLS:
            da_ref[:, :n_q] = (proj * Q_SCALE).astype(BF16)
        else:
            da_ref[:, n_q:] = proj.astype(BF16)
        rows = list(range(gi * per_group, min((gi + 1) * per_group, MIX_TILE)))
        for r0 in range(0, len(rows), ISSUE_TOKENS):
            start_gathers(nxt, 1 - slot, rows[r0:r0 + ISSUE_TOKENS])

    @pl.when(step == last)
    def _():
        wait_gathers(1 - slot)


def _mix_in_combine(dest, hn, gate, y, gain, w, layer):
    n = hn.shape[0]
    row = lambda i, d: (i, 0)
    return pl.pallas_call(
        _mix_in_combine_kernel,
        grid_spec=pltpu.PrefetchScalarGridSpec(
            num_scalar_prefetch=1,
            grid=(n // MIX_TILE,),
            in_specs=[
                pl.BlockSpec((MIX_TILE, D_MODEL), row),
                pl.BlockSpec((MIX_TILE, N_EXPERTS), row),
                pl.BlockSpec(memory_space=pl.ANY),
                pl.BlockSpec((1, D_MODEL), lambda i, d: (0, 0)),
                pl.BlockSpec((None, D_MODEL, W_IN_COLS), lambda i, d: (layer, 0, 0),
                             pipeline_mode=pl.Buffered(1)),
            ],
            out_specs=[
                pl.BlockSpec((MIX_TILE, D_MODEL), row),
                pl.BlockSpec((MIX_TILE, HG_COLS), row),
                pl.BlockSpec((MIX_TILE, DA_COLS), row),
            ],
            scratch_shapes=[pltpu.VMEM((2, TOP_K, MIX_TILE * TOKEN_ROWS, LANES), F32),
                            pltpu.SemaphoreType.DMA((2,))],
        ),
        out_shape=[
            jax.ShapeDtypeStruct((n, D_MODEL), F32),
            jax.ShapeDtypeStruct((n, HG_COLS), F32),
            jax.ShapeDtypeStruct((n, DA_COLS), BF16),
        ],
        compiler_params=_params("arbitrary"),
        name="mix_in_combine",
    )(dest, hn, gate, y, gain, w)


HG_LEVELS = (32, 16, 8, 4, 2, 1)
N_SUMS = len(HG_LEVELS) + 2


def _hgrn_consts():
    c = HG_CHUNK
    t = np.arange(c)[:, None]
    j = np.arange(c)[None, :]
    sums = np.zeros((N_SUMS, c, c), np.float32)
    masks = np.zeros((len(HG_LEVELS) + 1, c, c), np.float32)
    sums[0] = j <= t
    masks[0] = np.eye(c)
    for li, w in enumerate(HG_LEVELS, start=1):
        ref = (t // (2 * w)) * (2 * w) + w
        sums[li] = np.where(t >= ref, (j > ref) & (j <= t), (j > t) & (j <= ref))
        masks[li] = (t // (2 * w) == j // (2 * w)) & (t % (2 * w) >= w) & (j % (2 * w) < w)
    sums[N_SUMS - 1] = j > t
    sums = sums.reshape(N_SUMS * c, c)
    return np.concatenate([sums, sums], axis=1), masks


_HG_SUMS, _HG_MASKS = _hgrn_consts()


def _hgrn_kernel(hg_ref, loga_ref, log1m_ref, nw_ref, sums_ref, masks_ref, o_ref, state_ref):
    c_idx = pl.program_id(1)

    @pl.when(c_idx == 0)
    def _():
        state_ref[...] = jnp.zeros_like(state_ref)

    C = HG_CHUNK
    W = HG_WIDTH
    n_chunks = HG_TILE // C
    sums = sums_ref[...]
    nw = nw_ref[...]
    row_idx = c_idx * HG_TILE + lax.broadcasted_iota(jnp.int32, (HG_TILE, 1), 0)
    valid = row_idx >= (LEAD - N_META)
    step = lax.broadcasted_iota(jnp.int32, (C, 1), 0)
    head_cols = [slice(hd * HG_D, (hd + 1) * HG_D) for hd in range(HG_HEADS)]

    f = hg_ref[:, W:2 * W]
    qf = _silu(hg_ref[:, 0:W])
    ls = jnp.minimum(f, 0.0) - jnp.log(1.0 + jnp.exp(-jnp.abs(f)))
    cc = log1m_ref[...] + ls
    loga = loga_ref[...]
    lf = jnp.maximum(loga, cc) + jnp.log(1.0 + jnp.exp(-jnp.abs(loga - cc)))
    kk = jnp.exp(cc - f)
    lf = jnp.where(valid, lf, 0.0)
    kk = jnp.where(valid, kk, 0.0)
    lf2 = lf * LOG2E
    lf_hi = lf2.astype(BF16)
    lf_lo = (lf2 - lf_hi.astype(F32)).astype(BF16)
    vb = hg_ref[:, 2 * W:3 * W].astype(BF16)
    gate = _silu(hg_ref[:, 3 * W:4 * W])

    def side_by_side(x):
        return jnp.concatenate([x[ch * C:(ch + 1) * C] for ch in range(n_chunks)], axis=1)

    qf_w, kk_w = side_by_side(qf), side_by_side(kk)
    qb_w, kb_w = qf_w.astype(BF16), kk_w.astype(BF16)
    e = jnp.exp2(_dot(sums, jnp.concatenate([side_by_side(lf_hi), side_by_side(lf_lo)], axis=0)))
    e_b = e[0:C]
    decay_end = e_b[C - 1:C, :]
    q_in = (qf_w * e_b).astype(BF16)
    k_out = (kk_w * e[(N_SUMS - 1) * C:N_SUMS * C]).astype(BF16)
    z = [(jnp.where((step & w) != 0, qf_w, kk_w) * e[li * C:(li + 1) * C]).astype(BF16)
         for li, w in enumerate(HG_LEVELS, start=1)]
    unit_cols = [[slice(ch * W + hd * HG_D, ch * W + (hd + 1) * HG_D) for hd in range(HG_HEADS)]
                 for ch in range(n_chunks)]
    scores = []
    for ch in range(n_chunks):
        scores.append([])
        for cols in unit_cols[ch]:
            s = masks_ref[0] * _dot_nt(qb_w[:, cols], kb_w[:, cols])
            for li in range(1, len(HG_LEVELS) + 1):
                zl = z[li - 1][:, cols]
                s += masks_ref[li] * _dot_nt(zl, zl)
            scores[ch].append(s.astype(BF16))

    for ch in range(n_chunks):
        rows = slice(ch * C, (ch + 1) * C)
        for hd, cols in enumerate(head_cols):
            wide = unit_cols[ch][hd]
            st = state_ref[hd]
            v_h = vb[rows, cols]
            o = _dot_nt(q_in[:, wide], st.astype(BF16)) + _dot(scores[ch][hd], v_h)
            state_ref[hd] = st * decay_end[:, wide] + _dot_tn(v_h, k_out[:, wide])
            o = _rms(o, nw) * gate[rows, cols]
            o_ref[rows, cols] = o.astype(o_ref.dtype)


def _hgrn(hg, loga, log1m, norm_w, batch, length):
    hg3 = hg.reshape(batch, length, HG_COLS)
    out = pl.pallas_call(
        _hgrn_kernel,
        grid=(batch, length // HG_TILE),
        in_specs=[
            pl.BlockSpec((None, HG_TILE, HG_COLS), lambda b, c: (b, c, 0)),
            pl.BlockSpec((1, HG_WIDTH), lambda b, c: (0, 0)),
            pl.BlockSpec((1, HG_WIDTH), lambda b, c: (0, 0)),
            pl.BlockSpec((1, HG_D), lambda b, c: (0, 0)),
            pl.BlockSpec(_HG_SUMS.shape, lambda b, c: (0, 0)),
            pl.BlockSpec(_HG_MASKS.shape, lambda b, c: (0, 0, 0)),
        ],
        out_specs=pl.BlockSpec((None, HG_TILE, HG_WIDTH), lambda b, c: (b, c, 0)),
        out_shape=jax.ShapeDtypeStruct((batch, length, HG_WIDTH), BF16),
        scratch_shapes=[pltpu.VMEM((HG_HEADS, HG_D, HG_D), F32)],
        compiler_params=_params("parallel", "arbitrary"),
        name="hgrn2",
    )(hg3, loga, log1m, norm_w, jnp.asarray(_HG_SUMS, BF16), jnp.asarray(_HG_MASKS, F32))
    return out.reshape(batch * length, HG_WIDTH)


def _attn_kernel(q_ref, k_ref, v_ref, toe_ref, cst_ref, w_ref, o_ref, s_ref, *, n_blocks):
    lam = cst_ref[0:1, 0:1]
    post = cst_ref[1:2, :]
    lane = lax.broadcasted_iota(jnp.int32, (Q_BLOCK, Q_BLOCK), 1)
    first_half = lane < DA_DQK
    key_ok0 = lane >= (LEAD - N_META)
    inert_bias = jnp.where(key_ok0, 0.0, NEG)

    def near_bias(kind, kb):
        bias = toe_ref[kind]
        if kb == 0:
            bias = jnp.where(key_ok0, bias, NEG)
        return bias

    def slabs(x):
        return [x[:, c:c + Q_BLOCK] for c in range(0, x.shape[1], Q_BLOCK)]

    map_rows = 2 * Q_BLOCK
    groups = [tuple(range(i, min(i + Q_GROUP, n_blocks))) for i in range(0, n_blocks, Q_GROUP)]

    def both_maps(bias):
        return jnp.concatenate([bias, bias], axis=0)

    def stacked_q(blocks):
        parts = []
        for i in blocks:
            qi = q_ref[i * Q_BLOCK:(i + 1) * Q_BLOCK, :]
            zero = jnp.zeros_like(qi)
            parts += [jnp.where(first_half, qi, zero), jnp.where(first_half, zero, qi)]
        return jnp.concatenate(parts, axis=0)

    def group_tiles(blocks):
        first, last = blocks[0], blocks[-1]
        tiles = []
        far_end = max(first - 1, 0)
        if far_end >= 1:
            tiles.append((0, Q_BLOCK, jnp.concatenate([inert_bias] * (2 * len(blocks)), axis=0), 0))
        kb = 1
        while kb < far_end:
            width = KEY_TILE
            while kb + width // Q_BLOCK > far_end:
                width //= 2
            tiles.append((kb * Q_BLOCK, width, None, 0))
            kb += width // Q_BLOCK
        for kb in range(far_end, last + 1):
            biases, first_row = [], None
            for r, qb in enumerate(blocks):
                if kb > qb:
                    continue
                if first_row is None:
                    first_row = r * map_rows
                if kb == qb:
                    bias = near_bias(0, kb)
                elif kb == qb - 1:
                    bias = near_bias(1, kb)
                else:
                    bias = inert_bias if kb == 0 else jnp.zeros_like(inert_bias)
                biases.append(both_maps(bias))
            tiles.append((kb * Q_BLOCK, Q_BLOCK, jnp.concatenate(biases, axis=0), first_row))
        return tiles

    def merge(acc, x, first_row, op):
        if acc is None:
            return x
        if first_row == 0:
            return op(acc, x)
        return jnp.concatenate([acc[:first_row], op(acc[first_row:], x)], axis=0)

    def sweep_scores(g):
        blocks = groups[g]
        rows = len(blocks) * map_rows
        q2 = stacked_q(blocks)
        m_acc = None
        for start, width, bias, first_row in group_tiles(blocks):
            s = _dot_nt(q2[first_row:], k_ref[start:start + width, :])
            if bias is not None:
                s = s + bias
            s_ref[g % 2, first_row:rows, start:start + width] = s
            for slab in slabs(s):
                m_acc = merge(m_acc, slab, first_row, jnp.maximum)
        return m_acc.max(axis=-1, keepdims=True)

    row_max = sweep_scores(0)
    for g, blocks in enumerate(groups):
        rows = len(blocks) * map_rows
        m = row_max
        if g + 1 < len(groups):
            row_max = sweep_scores(g + 1)
        l_acc = o_acc = None
        for start, width, _, first_row in group_tiles(blocks):
            p = jnp.exp2(s_ref[g % 2, first_row:rows, start:start + width] - m[first_row:])
            for slab in slabs(p):
                l_acc = merge(l_acc, slab, first_row, jnp.add)
            o_acc = merge(o_acc, _dot(p.astype(BF16), v_ref[start:start + width, :]), first_row, jnp.add)
        o2 = o_acc * (1.0 / l_acc.sum(axis=-1, keepdims=True))
        for r, i in enumerate(blocks):
            o = o2[r * map_rows:r * map_rows + Q_BLOCK] - lam * o2[r * map_rows + Q_BLOCK:(r + 1) * map_rows]
            o = _rms(o, w_ref[...]) * post
            o_ref[i * Q_BLOCK:(i + 1) * Q_BLOCK, :] = o.astype(o_ref.dtype)


def _attn(da, toe, cst, subln_w, batch, length):
    da3 = da.reshape(batch, length, DA_COLS)
    hw = DA_HEADS
    out = pl.pallas_call(
        functools.partial(_attn_kernel, n_blocks=length // Q_BLOCK),
        grid=(batch, DA_HEADS),
        in_specs=[
            pl.BlockSpec((None, length, DA_DV), lambda b, h: (b, 0, h)),
            pl.BlockSpec((None, length, DA_DV), lambda b, h: (b, 0, hw + h)),
            pl.BlockSpec((None, length, DA_DV), lambda b, h: (b, 0, 2 * hw + h)),
            pl.BlockSpec((None, 2, Q_BLOCK, Q_BLOCK), lambda b, h: (h, 0, 0, 0)),
            pl.BlockSpec((SUBLANES, LANES), lambda b, h: (0, 0)),
            pl.BlockSpec((1, DA_DV), lambda b, h: (0, 0)),
        ],
        out_specs=pl.BlockSpec((None, length, DA_DV), lambda b, h: (b, 0, h)),
        out_shape=jax.ShapeDtypeStruct((batch, length, DA_HEADS * DA_DV), BF16),
        scratch_shapes=[pltpu.VMEM((2, Q_GROUP * 2 * Q_BLOCK, length), F32)],
        compiler_params=_params("parallel", "parallel"),
        name="diff_attn",
    )(da3, da3, da3, toe, cst, subln_w)
    return out.reshape(batch * length, DA_HEADS * DA_DV)


def _t5_bucket(dist):
    n = jnp.maximum(dist, 0)
    max_exact = REL_BUCKETS // 2
    nf = jnp.maximum(n, max_exact).astype(F32)
    large = max_exact + (jnp.log(nf / max_exact) / math.log(REL_MAX_DIST / max_exact)
                         * (REL_BUCKETS - max_exact)).astype(jnp.int32)
    large = jnp.minimum(large, REL_BUCKETS - 1)
    return jnp.where(n < max_exact, n, large)


def _attn_bias_tables(rel_bias):
    tab = rel_bias.astype(F32)
    qi = jnp.arange(Q_BLOCK, dtype=jnp.int32)[:, None]
    ki = jnp.arange(Q_BLOCK, dtype=jnp.int32)[None, :]

    def lookup(bucket):
        onehot = bucket[None, :, :, None] == jnp.arange(REL_BUCKETS, dtype=jnp.int32)
        return jnp.sum(jnp.where(onehot, tab.T[:, None, None, :], 0.0), axis=-1)

    far = tab[REL_BUCKETS - 1][:, None, None]
    diag = jnp.where((ki <= qi)[None], (lookup(_t5_bucket(qi - ki)) - far) * LOG2E, NEG)
    prev = (lookup(_t5_bucket(qi - ki + Q_BLOCK)) - far) * LOG2E
    return jnp.stack([diag, prev], axis=1)


def _out_proj_router_kernel(h_ref, ohg_ref, oda_ref, wo_ref, g_ref, router_ref, before_ref,
                            hn_ref, u_ref, route_ref, gate_ref, count_ref, seen_ref):
    @pl.when(pl.program_id(0) == 0)
    def _():
        seen_ref[...] = jnp.zeros_like(seen_ref)

    hn = (h_ref[...] + _dot(ohg_ref[...], wo_ref[:HG_WIDTH, :])
          + _dot(oda_ref[...], wo_ref[HG_WIDTH:, :]))
    hn_ref[...] = hn
    u = _rms(hn, g_ref[...])
    _store_token_major(u_ref, u)
    u_hi = u.astype(BF16)
    u_lo = (u - u_hi.astype(F32)).astype(BF16)
    logits = _dot(jnp.concatenate([u_hi, u_lo, u_hi], axis=1), router_ref[...])
    logits = logits.T[:N_EXPERTS]
    expert = lax.broadcasted_iota(jnp.int32, logits.shape, 0)

    def top(x):
        best = x.max(axis=0, keepdims=True)
        return best, jnp.where(x == best, expert, N_EXPERTS).min(axis=0, keepdims=True)

    l1, e1 = top(logits)
    l2, e2 = top(jnp.where(expert == e1, -jnp.inf, logits))
    w2 = jnp.exp(l2 - l1)
    g1 = 1.0 / (1.0 + w2)
    g2 = w2 / (1.0 + w2)

    pick1 = (expert == e1).astype(F32)
    pick2 = (expert == e2).astype(F32)
    picked = pick1 + pick2
    ahead = seen_ref[:, 0:1] + _dot(picked.astype(BF16), before_ref[...])
    rank1 = jnp.sum(pick1 * ahead, axis=0, keepdims=True).astype(jnp.int32)
    rank2 = jnp.sum(pick2 * ahead, axis=0, keepdims=True).astype(jnp.int32)
    seen = seen_ref[...] + jnp.sum(picked, axis=1, keepdims=True)
    seen_ref[...] = seen

    route_ref[...] = jnp.where(expert == 0, e1, jnp.where(expert == 1, e2, jnp.where(
        expert == 2, rank1, jnp.where(expert == 3, rank2, 0))))
    gate_ref[...] = jnp.where(expert == 0, g1, jnp.where(expert == 1, g2, 0.0)).T
    count_ref[...] = seen.astype(jnp.int32)


def _out_proj_router(h, o_hg, o_da, wo, layer, gain, router):
    n = h.shape[0]
    row = lambda i: (i, 0)
    full = lambda i: (0, 0)
    assert N_EXPERTS >= 2 * TOP_K
    lanes_of = lambda i: (0, i)
    before = jnp.asarray(np.triu(np.ones((ROW_TILE, ROW_TILE), np.float32), 1), BF16)
    return pl.pallas_call(
        _out_proj_router_kernel,
        grid=(n // ROW_TILE,),
        in_specs=[
            pl.BlockSpec((ROW_TILE, D_MODEL), row),
            pl.BlockSpec((ROW_TILE, HG_WIDTH), row),
            pl.BlockSpec((ROW_TILE, DA_HEADS * DA_DV), row),
            pl.BlockSpec((None,) + wo.shape[1:], lambda i: (layer, 0, 0)),
            pl.BlockSpec((1, D_MODEL), full),
            pl.BlockSpec(router.shape, full),
            pl.BlockSpec(before.shape, full),
        ],
        out_specs=[
            pl.BlockSpec((ROW_TILE, D_MODEL), row),
            pl.BlockSpec((ROW_TILE * TOKEN_ROWS, LANES), row),
            pl.BlockSpec((N_EXPERTS, ROW_TILE), lanes_of),
            pl.BlockSpec((ROW_TILE, N_EXPERTS), row),
            pl.BlockSpec((N_EXPERTS, LANES), full),
        ],
        out_shape=[
            jax.ShapeDtypeStruct((n, D_MODEL), F32),
            jax.ShapeDtypeStruct((n * TOKEN_ROWS, LANES), F32),
            jax.ShapeDtypeStruct((N_EXPERTS, n), jnp.int32),
            jax.ShapeDtypeStruct((n, N_EXPERTS), F32),
            jax.ShapeDtypeStruct((N_EXPERTS, LANES), jnp.int32),
        ],
        scratch_shapes=[pltpu.VMEM((N_EXPERTS, LANES), F32)],
        compiler_params=_params("arbitrary"),
        name="out_proj_router",
    )(h, o_hg, o_da, wo, gain, router, before)


def _dense_layer_kernel(h_ref, ohg_ref, oda_ref, wo_ref, g_ref, w1_ref, w3_ref, w2_ref, *rest):
    n_cast = (len(rest) - 1) // 2
    o_ref = rest[n_cast]
    hn = (h_ref[...] + _dot(ohg_ref[...], wo_ref[:HG_WIDTH, :])
          + _dot(oda_ref[...], wo_ref[HG_WIDTH:, :]))
    u = _rms(hn, g_ref[...]).astype(BF16)
    a = _dot(u, w1_ref[...])
    act = (_silu(a) * _dot(u, w3_ref[...])).astype(BF16)
    o_ref[...] = hn + _dot(act, w2_ref[...])
    _run_cast_jobs(rest[:n_cast], rest[n_cast + 1:])


def _dense_layer(h, o_hg, o_da, wo, layer, gain, w1, w3, w2, idx, to_cast, cast_idx):
    n = h.shape[0]
    steps = n // ROW_TILE
    row = lambda i: (i, 0)

    def resident(w, index):
        return pl.BlockSpec((None,) + w.shape[1:], lambda i: (index, 0, 0),
                            pipeline_mode=pl.Buffered(1))

    jobs = [(w.reshape(w.shape[0], w.shape[1] * w.shape[2], w.shape[3]), cast_idx) for w in to_cast]
    cast_in, cast_in_specs, cast_out_specs, cast_out_shape = _cast_jobs(jobs, steps)

    out = pl.pallas_call(
        _dense_layer_kernel,
        grid=(steps,),
        in_specs=[
            pl.BlockSpec((ROW_TILE, D_MODEL), row),
            pl.BlockSpec((ROW_TILE, HG_WIDTH), row),
            pl.BlockSpec((ROW_TILE, DA_HEADS * DA_DV), row),
            resident(wo, layer),
            pl.BlockSpec((1, D_MODEL), lambda i: (0, 0)),
            resident(w1, idx),
            resident(w3, idx),
            resident(w2, idx),
        ] + cast_in_specs,
        out_specs=[pl.BlockSpec((ROW_TILE, D_MODEL), row)] + cast_out_specs,
        out_shape=[jax.ShapeDtypeStruct((n, D_MODEL), F32)] + cast_out_shape,
        compiler_params=_params("arbitrary"),
        name="dense_layer",
    )(h, o_hg, o_da, wo, gain, w1, w3, w2, *cast_in)
    return out[0], [b.reshape(w.shape[1:]) for b, w in zip(out[1:], to_cast)]


DISPATCH_TILE = 1024
ISSUE_TOKENS = 8


def _token_copy(src_ref, src_tok, dst_ref, dst_tok, sem):
    src = pl.multiple_of(src_tok * TOKEN_ROWS, TOKEN_ROWS)
    dst = pl.multiple_of(dst_tok * TOKEN_ROWS, TOKEN_ROWS)
    return pltpu.make_async_copy(src_ref.at[pl.ds(src, TOKEN_ROWS), :],
                                 dst_ref.at[pl.ds(dst, TOKEN_ROWS), :], sem)


def _dispatch_kernel(dest_ref, empty_ref, u_ref, xs_ref, zero_ref, sem):
    step = pl.program_id(0)
    base = step * DISPATCH_TILE
    tile_rows = DISPATCH_TILE * TOKEN_ROWS

    def retire_tile():
        pltpu.make_async_copy(u_ref, xs_ref.at[pl.ds(0, tile_rows), :], sem).wait()

    @pl.when(step == 0)
    def _():
        zero_ref[...] = jnp.zeros_like(zero_ref)

        def clear(g, carry):
            slots = [empty_ref[g * ISSUE_TOKENS * TOP_K + j] for j in range(ISSUE_TOKENS * TOP_K)]
            for j, slot in enumerate(slots):
                _token_copy(zero_ref, 0, xs_ref, slot, sem).start(priority=j % 2)
            return carry

        n_empty = empty_ref.shape[0]
        lax.fori_loop(0, n_empty // (ISSUE_TOKENS * TOP_K), clear, 0)
        for _ in range(n_empty // DISPATCH_TILE):
            retire_tile()

    n_tokens = dest_ref.shape[0] // TOP_K

    def start(g, carry):
        r0 = g * ISSUE_TOKENS
        slots = [dest_ref[(j % TOP_K) * n_tokens + base + r0 + j // TOP_K]
                 for j in range(ISSUE_TOKENS * TOP_K)]
        for j, slot in enumerate(slots):
            _token_copy(u_ref, r0 + j // TOP_K, xs_ref, slot, sem).start(priority=j % 2)
        return carry

    lax.fori_loop(0, DISPATCH_TILE // ISSUE_TOKENS, start, 0)
    for _ in range(TOP_K):
        retire_tile()


def _dispatch(dest, empty_slots, u_tm, n_slots):
    n = dest.shape[0] // TOP_K
    assert empty_slots.shape[0] % DISPATCH_TILE == 0
    return pl.pallas_call(
        _dispatch_kernel,
        grid_spec=pltpu.PrefetchScalarGridSpec(
            num_scalar_prefetch=2,
            grid=(n // DISPATCH_TILE,),
            in_specs=[pl.BlockSpec((DISPATCH_TILE * TOKEN_ROWS, LANES), lambda i, d, e: (i, 0))],
            out_specs=pl.BlockSpec(memory_space=pl.ANY),
            scratch_shapes=[pltpu.VMEM((TOKEN_ROWS, LANES), u_tm.dtype),
                            pltpu.SemaphoreType.DMA(())],
        ),
        out_shape=jax.ShapeDtypeStruct((n_slots * TOKEN_ROWS, LANES), u_tm.dtype),
        compiler_params=_params("arbitrary"),
        name="moe_dispatch",
    )(dest, empty_slots, u_tm)


def _expert_kernel(be_ref, nused_ref, xs_ref, w1_ref, w3_ref, w2_ref, y_ref):
    del be_ref
    blk = pl.program_id(0)

    @pl.when(blk < nused_ref[0])
    def _():
        x = _load_token_major(xs_ref, MOE_TILE).astype(BF16)
        acc = jnp.zeros((MOE_TILE, D_MODEL), F32)
        for c0 in range(0, D_FF_EXPERT, FF_CHUNK):
            a = _dot(x, w1_ref[:, c0:c0 + FF_CHUNK])
            act = (_silu(a) * _dot(x, w3_ref[:, c0:c0 + FF_CHUNK])).astype(BF16)
            acc = acc + _dot(act, w2_ref[c0:c0 + FF_CHUNK, :])
        _store_token_major(y_ref, acc)

    @pl.when(blk >= nused_ref[0])
    def _():
        y_ref[...] = jnp.zeros_like(y_ref)


def _experts(block_expert, n_used, xs, w1, w3, w2):
    n_blocks = xs.shape[0] // (MOE_TILE * TOKEN_ROWS)
    wmap = lambda i, be, nu: (be[i], 0, 0)
    slots = pl.BlockSpec((MOE_TILE * TOKEN_ROWS, LANES), lambda i, be, nu: (i, 0))
    return pl.pallas_call(
        _expert_kernel,
        grid_spec=pltpu.PrefetchScalarGridSpec(
            num_scalar_prefetch=2,
            grid=(n_blocks,),
            in_specs=[
                slots,
                pl.BlockSpec((None, D_MODEL, D_FF_EXPERT), wmap),
                pl.BlockSpec((None, D_MODEL, D_FF_EXPERT), wmap),
                pl.BlockSpec((None, D_FF_EXPERT, D_MODEL), wmap),
            ],
            out_specs=slots,
        ),
        out_shape=jax.ShapeDtypeStruct(xs.shape, F32),
        compiler_params=_params("arbitrary"),
        name="moe_experts",
    )(block_expert, n_used, xs, w1, w3, w2)


def _combine_final_kernel(dest_ref, h_ref, gate_ref, y_ref, g_ref, o_ref, buf_ref, sem):
    step = pl.program_id(0)
    n_tokens = dest_ref.shape[0] // TOP_K

    def issue(s, slot):
        base = s * Q_BLOCK

        def body(g, carry):
            r0 = g * ISSUE_TOKENS
            slots = [dest_ref[(j % TOP_K) * n_tokens + base + r0 + j // TOP_K]
                     for j in range(ISSUE_TOKENS * TOP_K)]
            for j, src in enumerate(slots):
                _token_copy(y_ref, src, buf_ref.at[slot, j % TOP_K], r0 + j // TOP_K,
                            sem.at[slot]).start(priority=j % 2)
            return carry

        lax.fori_loop(0, Q_BLOCK // ISSUE_TOKENS, body, 0)

    @pl.when(step == 0)
    def _():
        issue(0, 0)

    for slot in range(2):
        @pl.when(step % 2 == slot)
        def _():
            @pl.when(step + 1 < pl.num_programs(0))
            def _():
                issue(step + 1, 1 - slot)

            for k in range(TOP_K):
                pltpu.make_async_copy(y_ref.at[pl.ds(0, Q_BLOCK * TOKEN_ROWS), :],
                                      buf_ref.at[slot, k], sem.at[slot]).wait()
            gate = gate_ref[...]
            h = (h_ref[...]
                 + gate[:, 0:1] * _load_token_major(buf_ref.at[slot, 0], Q_BLOCK)
                 + gate[:, 1:2] * _load_token_major(buf_ref.at[slot, 1], Q_BLOCK))
            o_ref[...] = _rms(h, g_ref[...])


def _combine_final(dest, h, gate, y, gain, batch, length, seq):
    n = h.shape[0]
    per_seq = length // Q_BLOCK
    lead_blocks = LEAD // Q_BLOCK
    return pl.pallas_call(
        _combine_final_kernel,
        grid_spec=pltpu.PrefetchScalarGridSpec(
            num_scalar_prefetch=1,
            grid=(n // Q_BLOCK,),
            in_specs=[
                pl.BlockSpec((Q_BLOCK, D_MODEL), lambda i, d: (i, 0)),
                pl.BlockSpec((Q_BLOCK, N_EXPERTS), lambda i, d: (i, 0)),
                pl.BlockSpec(memory_space=pl.ANY),
                pl.BlockSpec((1, D_MODEL), lambda i, d: (0, 0)),
            ],
            out_specs=pl.BlockSpec(
                (None, Q_BLOCK, D_MODEL),
                lambda i, d: (i // per_seq, jnp.maximum(i % per_seq - lead_blocks, 0), 0)),
            scratch_shapes=[pltpu.VMEM((2, TOP_K, Q_BLOCK * TOKEN_ROWS, LANES), F32),
                            pltpu.SemaphoreType.DMA((2,))],
        ),
        out_shape=jax.ShapeDtypeStruct((batch, seq, D_MODEL), F32),
        compiler_params=_params("arbitrary"),
        name="moe_combine_final",
    )(dest, h, gate, y, gain)


def _prefix_sum(x):
    k = x.shape[0]
    keep = (np.arange(k)[None, :] <= np.arange(k)[:, None]).reshape((k, k) + (1,) * (x.ndim - 1))
    return jnp.sum(jnp.where(keep, x[None], jnp.zeros_like(x[None])), axis=1)


def _slot_tables(route, counts, n_slots):
    n = route.shape[1]
    flat_e = route[0:TOP_K].reshape(n * TOP_K)
    rank = route[TOP_K:2 * TOP_K].reshape(n * TOP_K)
    counts = counts[:, 0]
    padded = (counts + MOE_TILE - 1) // MOE_TILE * MOE_TILE
    pad_end = _prefix_sum(padded)
    pad_start = pad_end - padded
    onehot = flat_e[:, None] == jnp.arange(N_EXPERTS, dtype=jnp.int32)[None, :]
    dest = (rank + jnp.sum(jnp.where(onehot, pad_start[None, :], 0), axis=-1)).astype(jnp.int32)
    n_blocks = n_slots // MOE_TILE
    block_first = jnp.arange(n_blocks, dtype=jnp.int32) * MOE_TILE
    block_expert = jnp.minimum(
        jnp.sum((pad_end[None, :] <= block_first[:, None]).astype(jnp.int32), axis=-1),
        N_EXPERTS - 1).astype(jnp.int32)
    n_used = (pad_end[-1:] // MOE_TILE).astype(jnp.int32)
    n_empty = n_slots - n * TOP_K
    gap_end = _prefix_sum(padded - counts)
    j = jnp.arange(n_empty, dtype=jnp.int32)
    owner = jnp.sum((j[:, None] >= gap_end[None, :]).astype(jnp.int32), axis=-1)
    first_empty = jnp.concatenate([pad_start + counts, pad_end[-1:]])
    gap_start = jnp.concatenate([jnp.zeros((1,), gap_end.dtype), gap_end])
    pick = owner[:, None] == jnp.arange(N_EXPERTS + 1, dtype=jnp.int32)[None, :]
    empty_slots = (j + jnp.sum(jnp.where(pick, (first_empty - gap_start)[None, :], 0), axis=-1))
    return dest, block_expert, n_used, empty_slots.astype(jnp.int32)


def _moe_ffn(u, route, counts, w1, w3, w2):
    n = route.shape[1]
    n_slots = (n * TOP_K // MOE_TILE + N_EXPERTS) * MOE_TILE
    dest, block_expert, n_used, empty_slots = _slot_tables(route, counts, n_slots)
    xs = _dispatch(dest, empty_slots, u, n_slots)
    y = _experts(block_expert, n_used, xs, w1, w3, w2)
    return dest, y


def kernel(x, meta, rel_bias, norm_mix, w_in, hg_lb_logits, hg_norm_w, da_lambda, da_subln_w, w_out, norm_ffn, dense_w1, dense_w3, dense_w2, moe_router, moe_w1, moe_w3, moe_w2, final_norm):
    batch, seq, d = x.shape
    length = LEAD + seq
    h = jnp.concatenate([
        jnp.zeros((batch, LEAD - N_META, d), x.dtype),
        jnp.broadcast_to(meta[None].astype(x.dtype), (batch, N_META, d)),
        x], axis=1).reshape(batch * length, d)

    toe = _attn_bias_tables(rel_bias)
    lb_cum = _prefix_sum(jax.nn.softmax(hg_lb_logits.astype(F32), axis=0))
    lb_all = jnp.clip(lb_cum - lb_cum[0:1], 0.0, LB_MAX)
    log_lb = jnp.log(lb_all)
    log_1m_lb = jnp.log1p(-lb_all)

    assert DEPTH % 2 == 0
    dense_f32 = (dense_w1, dense_w3, dense_w2)
    stacked_rows = lambda w: w.reshape(1, w.shape[0] * w.shape[1], w.shape[2])

    unsummed = None
    for l in range(DEPTH):
        if l == 0:
            jobs = [(w, 0) for w in dense_f32] + [(stacked_rows(w_out), 0), (stacked_rows(w_in), 0)]
            hg, da, casts = _mix_in(h, norm_mix[l][None], w_in[0:1].astype(BF16), 0, jobs)
            dense_b = [w[None] for w in casts[:3]]
            w_out_b = casts[3].reshape(w_out.shape)
            w_in_b = casts[4].reshape(w_in.shape)
        elif unsummed is None:
            jobs = [(w, (l + 1) // 2) for w in dense_f32] if l + 1 < DEPTH else []
            hg, da, casts = _mix_in(h, norm_mix[l][None], w_in_b, l, jobs)
            if jobs:
                dense_b = [w[None] for w in casts]
        else:
            h, hg, da = _mix_in_combine(*unsummed, norm_mix[l][None], w_in_b, l)
            unsummed = None
        lam_init = 0.8 - 0.6 * math.exp(-0.3 * l)
        lv = da_lambda[l].astype(F32)
        lam = jnp.exp(jnp.sum(lv[0] * lv[1])) - jnp.exp(jnp.sum(lv[2] * lv[3])) + lam_init
        cst = jnp.zeros((SUBLANES, LANES), F32).at[0].set(lam).at[1].set(1.0 - lam_init)
        o_hg = _hgrn(hg, log_lb[l][None], log_1m_lb[l][None], hg_norm_w[l][None], batch, length)
        o_da = _attn(da, toe, cst, da_subln_w[l][None], batch, length)
        i = l // 2
        if l % 2 == 0:
            h, moe_b = _dense_layer(h, o_hg, o_da, w_out_b, l, norm_ffn[l][None], *dense_b, 0,
                                    (moe_w1, moe_w3, moe_w2), i)
        else:
            router = jnp.zeros((d, LANES), F32).at[:, :N_EXPERTS].set(moe_router[i].astype(F32))
            r_hi = router.astype(BF16)
            r_lo = (router - r_hi.astype(F32)).astype(BF16)
            router = jnp.concatenate([r_hi, r_hi, r_lo], axis=0)
            hn, u, route, gate, counts = _out_proj_router(h, o_hg, o_da, w_out_b, l,
                                                          norm_ffn[l][None], router)
            dest, y = _moe_ffn(u, route, counts, *moe_b)
            if l + 1 < DEPTH:
                unsummed = (dest, hn, gate, y)
            else:
                return _combine_final(dest, hn, gate, y, final_norm[None], batch, length, seq)
```

```python
import functools
import math

import jax
import jax.numpy as jnp
import numpy as np
from jax import lax
from jax.experimental import pallas as pl
from jax.experimental.pallas import tpu as pltpu

D_MODEL = 1024
DEPTH = 4
N_META = 16
LEAD = 128
HG_WIDTH = 512
HG_HEADS = 4
HG_D = 128
HG_CHUNK = 64
DA_HEADS = 4
DA_DQK = 64
DA_DV = 128
Q_BLOCK = 128
KEY_TILE = 4 * Q_BLOCK
Q_GROUP = 2
LOG2E = math.log2(math.e)
Q_SCALE = DA_DQK ** -0.5 * LOG2E
REL_BUCKETS = 32
REL_MAX_DIST = 128
N_EXPERTS = 8
TOP_K = 2
D_FF_EXPERT = 3584
EPS = 1e-6
NEG = -1e30
LB_MAX = 0.999
HG_COLS = 4 * HG_WIDTH
DA_COLS = 3 * DA_HEADS * DA_DV
W_IN_COLS = HG_COLS + DA_COLS

LANES = 128
SUBLANES = 8
VMEM_LIMIT = 56 * 1024 * 1024

ROW_TILE = 256
MIX_TILE = 512
HG_TILE = 128
MOE_TILE = 256
FF_CHUNK = 1792

F32 = jnp.float32
BF16 = jnp.bfloat16


def _params(*sem):
    return pltpu.CompilerParams(dimension_semantics=sem, vmem_limit_bytes=VMEM_LIMIT)


def _dot(a, b):
    return jnp.dot(a, b, preferred_element_type=F32)


def _dot_nt(a, b):
    return lax.dot_general(a, b, (((1,), (1,)), ((), ())), preferred_element_type=F32)


def _dot_tn(a, b):
    return lax.dot_general(a, b, (((0,), (0,)), ((), ())), preferred_element_type=F32)


def _rms(x, gain):
    return x * lax.rsqrt(jnp.mean(x * x, axis=-1, keepdims=True) + EPS) * gain


TOKEN_ROWS = D_MODEL // LANES


def _store_token_major(ref, x):
    t = x.shape[0]
    for s in range(TOKEN_ROWS):
        ref[pl.ds(s, t, stride=TOKEN_ROWS), :] = x[:, s * LANES:(s + 1) * LANES]


def _load_token_major(ref, t):
    return jnp.concatenate(
        [ref[pl.ds(s, t, stride=TOKEN_ROWS), :] for s in range(TOKEN_ROWS)], axis=1)


def _silu(x):
    return x * (0.5 * jnp.tanh(0.5 * x) + 0.5)


BF16_SUBLANES = 16


def _cast_jobs(jobs, grid_steps):
    inputs, in_specs, out_specs, out_shapes = [], [], [], []
    for w, index in jobs:
        _, rows, cols = w.shape
        steps = max(s for s in range(1, grid_steps + 1)
                    if rows % s == 0 and (rows // s) % BF16_SUBLANES == 0)
        slab_rows = rows // steps
        in_specs.append(pl.BlockSpec((None, slab_rows, cols),
                                     lambda i, index=index, steps=steps: (index, jnp.minimum(i, steps - 1), 0)))
        out_specs.append(pl.BlockSpec((slab_rows, cols),
                                      lambda i, steps=steps: (jnp.minimum(i, steps - 1), 0)))
        out_shapes.append(jax.ShapeDtypeStruct((rows, cols), BF16))
        inputs.append(w)
    return inputs, in_specs, out_specs, out_shapes


def _run_cast_jobs(src_refs, dst_refs):
    for src_ref, dst_ref in zip(src_refs, dst_refs):
        dst_ref[...] = src_ref[...].astype(BF16)


def _mix_in_kernel(x_ref, g_ref, w_ref, *rest):
    n_cast = (len(rest) - 2) // 2
    hg_ref, da_ref = rest[n_cast:n_cast + 2]
    u = _rms(x_ref[...], g_ref[...]).astype(BF16)
    hg_ref[...] = _dot(u, w_ref[:, :HG_COLS])
    n_q = DA_HEADS * 2 * DA_DQK
    da_ref[:, :n_q] = (_dot(u, w_ref[:, HG_COLS:HG_COLS + n_q]) * Q_SCALE).astype(BF16)
    da_ref[:, n_q:] = _dot(u, w_ref[:, HG_COLS + n_q:]).astype(BF16)
    _run_cast_jobs(rest[:n_cast], rest[n_cast + 2:])


def _mix_in(h, gain, w, layer, to_cast=()):
    n = h.shape[0]
    steps = n // MIX_TILE
    cast_in, cast_in_specs, cast_out_specs, cast_out_shapes = _cast_jobs(to_cast, steps)
    out = pl.pallas_call(
        _mix_in_kernel,
        grid=(steps,),
        in_specs=[
            pl.BlockSpec((MIX_TILE, D_MODEL), lambda i: (i, 0)),
            pl.BlockSpec((1, D_MODEL), lambda i: (0, 0)),
            pl.BlockSpec((None, D_MODEL, W_IN_COLS), lambda i: (layer, 0, 0),
                         pipeline_mode=pl.Buffered(1)),
        ] + cast_in_specs,
        out_specs=[
            pl.BlockSpec((MIX_TILE, HG_COLS), lambda i: (i, 0)),
            pl.BlockSpec((MIX_TILE, DA_COLS), lambda i: (i, 0)),
        ] + cast_out_specs,
        out_shape=[
            jax.ShapeDtypeStruct((n, HG_COLS), F32),
            jax.ShapeDtypeStruct((n, DA_COLS), BF16),
        ] + cast_out_shapes,
        compiler_params=_params("arbitrary"),
        name="mix_in",
    )(h, gain, w, *cast_in)
    return out[0], out[1], list(out[2:])


def _mix_in_combine_kernel(dest_ref, hn_ref, gate_ref, y_ref, g_ref, w_ref,
                           h_ref, hg_ref, da_ref, buf_ref, sem):
    step = pl.program_id(0)
    last = pl.num_programs(0) - 1
    slot = step % 2
    tile_rows = MIX_TILE * TOKEN_ROWS
    n_tokens = dest_ref.shape[0] // TOP_K

    def start_gathers(s, to_slot, rows):
        base = s * MIX_TILE
        slots = [dest_ref[k * n_tokens + base + r] for r in rows for k in range(TOP_K)]
        for j, src in enumerate(slots):
            r, k = rows[j // TOP_K], j % TOP_K
            _token_copy(y_ref, src, buf_ref.at[to_slot, k], r, sem.at[to_slot]).start(priority=j % 2)

    def wait_gathers(of_slot):
        for k in range(TOP_K):
            pltpu.make_async_copy(y_ref.at[pl.ds(0, tile_rows), :], buf_ref.at[of_slot, k],
                                  sem.at[of_slot]).wait()

    @pl.when(step == 0)
    def _():
        def first(g, carry):
            start_gathers(0, 0, [g * ISSUE_TOKENS + r for r in range(ISSUE_TOKENS)])
            return carry

        lax.fori_loop(0, MIX_TILE // ISSUE_TOKENS, first, 0)

    wait_gathers(slot)
    gate = gate_ref[...]
    h = (hn_ref[...] + gate[:, 0:1] * _load_token_major(buf_ref.at[slot, 0], MIX_TILE)
         + gate[:, 1:2] * _load_token_major(buf_ref.at[slot, 1], MIX_TILE))
    h_ref[...] = h
    u = _rms(h, g_ref[...]).astype(BF16)

    nxt = jnp.minimum(step + 1, last)
    n_q = DA_HEADS * 2 * DA_DQK
    groups = [(c, c + HG_WIDTH) for c in range(0, HG_COLS, HG_WIDTH)]
    groups += [(HG_COLS, HG_COLS + n_q), (HG_COLS + n_q, W_IN_COLS)]
    per_group = -(-MIX_TILE // len(groups))
    for gi, (c0, c1) in enumerate(groups):
        proj = _dot(u, w_ref[:, c0:c1])
        if c1 <= HG_COLS:
            hg_ref[:, c0:c1] = proj
        elif c0 == HG_COLS:
            da_ref[:, :n_q] = (proj * Q_SCALE).astype(BF16)
        else:
            da_ref[:, n_q:] = proj.astype(BF16)
        rows = list(range(gi * per_group, min((gi + 1) * per_group, MIX_TILE)))
        for r0 in range(0, len(rows), ISSUE_TOKENS):
            start_gathers(nxt, 1 - slot, rows[r0:r0 + ISSUE_TOKENS])

    @pl.when(step == last)
    def _():
        wait_gathers(1 - slot)


def _mix_in_combine(dest, hn, gate, y, gain, w, layer):
    n = hn.shape[0]
    row = lambda i, d: (i, 0)
    return pl.pallas_call(
        _mix_in_combine_kernel,
        grid_spec=pltpu.PrefetchScalarGridSpec(
            num_scalar_prefetch=1,
            grid=(n // MIX_TILE,),
            in_specs=[
                pl.BlockSpec((MIX_TILE, D_MODEL), row),
                pl.BlockSpec((MIX_TILE, N_EXPERTS), row),
                pl.BlockSpec(memory_space=pl.ANY),
                pl.BlockSpec((1, D_MODEL), lambda i, d: (0, 0)),
                pl.BlockSpec((None, D_MODEL, W_IN_COLS), lambda i, d: (layer, 0, 0),
                             pipeline_mode=pl.Buffered(1)),
            ],
            out_specs=[
                pl.BlockSpec((MIX_TILE, D_MODEL), row),
                pl.BlockSpec((MIX_TILE, HG_COLS), row),
                pl.BlockSpec((MIX_TILE, DA_COLS), row),
            ],
            scratch_shapes=[pltpu.VMEM((2, TOP_K, MIX_TILE * TOKEN_ROWS, LANES), F32),
                            pltpu.SemaphoreType.DMA((2,))],
        ),
        out_shape=[
            jax.ShapeDtypeStruct((n, D_MODEL), F32),
            jax.ShapeDtypeStruct((n, HG_COLS), F32),
            jax.ShapeDtypeStruct((n, DA_COLS), BF16),
        ],
        compiler_params=_params("arbitrary"),
        name="mix_in_combine",
    )(dest, hn, gate, y, gain, w)


HG_LEVELS = (32, 16, 8, 4, 2, 1)
N_SUMS = len(HG_LEVELS) + 2


def _hgrn_consts():
    c = HG_CHUNK
    t = np.arange(c)[:, None]
    j = np.arange(c)[None, :]
    sums = np.zeros((N_SUMS, c, c), np.float32)
    masks = np.zeros((len(HG_LEVELS) + 1, c, c), np.float32)
    sums[0] = j <= t
    masks[0] = np.eye(c)
    for li, w in enumerate(HG_LEVELS, start=1):
        ref = (t // (2 * w)) * (2 * w) + w
        sums[li] = np.where(t >= ref, (j > ref) & (j <= t), (j > t) & (j <= ref))
        masks[li] = (t // (2 * w) == j // (2 * w)) & (t % (2 * w) >= w) & (j % (2 * w) < w)
    sums[N_SUMS - 1] = j > t
    sums = sums.reshape(N_SUMS * c, c)
    return np.concatenate([sums, sums], axis=1), masks


_HG_SUMS, _HG_MASKS = _hgrn_consts()


def _hgrn_kernel(hg_ref, loga_ref, log1m_ref, nw_ref, sums_ref, masks_ref, o_ref, state_ref):
    c_idx = pl.program_id(1)

    @pl.when(c_idx == 0)
    def _():
        state_ref[...] = jnp.zeros_like(state_ref)

    C = HG_CHUNK
    W = HG_WIDTH
    n_chunks = HG_TILE // C
    sums = sums_ref[...]
    nw = nw_ref[...]
    row_idx = c_idx * HG_TILE + lax.broadcasted_iota(jnp.int32, (HG_TILE, 1), 0)
    valid = row_idx >= (LEAD - N_META)
    step = lax.broadcasted_iota(jnp.int32, (C, 1), 0)
    head_cols = [slice(hd * HG_D, (hd + 1) * HG_D) for hd in range(HG_HEADS)]

    f = hg_ref[:, W:2 * W]
    qf = _silu(hg_ref[:, 0:W])
    ls = jnp.minimum(f, 0.0) - jnp.log(1.0 + jnp.exp(-jnp.abs(f)))
    cc = log1m_ref[...] + ls
    loga = loga_ref[...]
    lf = jnp.maximum(loga, cc) + jnp.log(1.0 + jnp.exp(-jnp.abs(loga - cc)))
    kk = jnp.exp(cc - f)
    lf = jnp.where(valid, lf, 0.0)
    kk = jnp.where(valid, kk, 0.0)
    lf2 = lf * LOG2E
    lf_hi = lf2.astype(BF16)
    lf_lo = (lf2 - lf_hi.astype(F32)).astype(BF16)
    vb = hg_ref[:, 2 * W:3 * W].astype(BF16)
    gate = _silu(hg_ref[:, 3 * W:4 * W])

    def side_by_side(x):
        return jnp.concatenate([x[ch * C:(ch + 1) * C] for ch in range(n_chunks)], axis=1)

    qf_w, kk_w = side_by_side(qf), side_by_side(kk)
    qb_w, kb_w = qf_w.astype(BF16), kk_w.astype(BF16)
    e = jnp.exp2(_dot(sums, jnp.concatenate([side_by_side(lf_hi), side_by_side(lf_lo)], axis=0)))
    e_b = e[0:C]
    decay_end = e_b[C - 1:C, :]
    q_in = (qf_w * e_b).astype(BF16)
    k_out = (kk_w * e[(N_SUMS - 1) * C:N_SUMS * C]).astype(BF16)
    z = [(jnp.where((step & w) != 0, qf_w, kk_w) * e[li * C:(li + 1) * C]).astype(BF16)
         for li, w in enumerate(HG_LEVELS, start=1)]
    unit_cols = [[slice(ch * W + hd * HG_D, ch * W + (hd + 1) * HG_D) for hd in range(HG_HEADS)]
                 for ch in range(n_chunks)]
    scores = []
    for ch in range(n_chunks):
        scores.append([])
        for cols in unit_cols[ch]:
            s = masks_ref[0] * _dot_nt(qb_w[:, cols], kb_w[:, cols])
            for li in range(1, len(HG_LEVELS) + 1):
                zl = z[li - 1][:, cols]
                s += masks_ref[li] * _dot_nt(zl, zl)
            scores[ch].append(s.astype(BF16))

    for ch in range(n_chunks):
        rows = slice(ch * C, (ch + 1) * C)
        for hd, cols in enumerate(head_cols):
            wide = unit_cols[ch][hd]
            st = state_ref[hd]
            v_h = vb[rows, cols]
            o = _dot_nt(q_in[:, wide], st.astype(BF16)) + _dot(scores[ch][hd], v_h)
            state_ref[hd] = st * decay_end[:, wide] + _dot_tn(v_h, k_out[:, wide])
            o = _rms(o, nw) * gate[rows, cols]
            o_ref[rows, cols] = o.astype(o_ref.dtype)


def _hgrn(hg, loga, log1m, norm_w, batch, length):
    hg3 = hg.reshape(batch, length, HG_COLS)
    out = pl.pallas_call(
        _hgrn_kernel,
        grid=(batch, length // HG_TILE),
        in_specs=[
            pl.BlockSpec((None, HG_TILE, HG_COLS), lambda b, c: (b, c, 0)),
            pl.BlockSpec((1, HG_WIDTH), lambda b, c: (0, 0)),
            pl.BlockSpec((1, HG_WIDTH), lambda b, c: (0, 0)),
            pl.BlockSpec((1, HG_D), lambda b, c: (0, 0)),
            pl.BlockSpec(_HG_SUMS.shape, lambda b, c: (0, 0)),
            pl.BlockSpec(_HG_MASKS.shape, lambda b, c: (0, 0, 0)),
        ],
        out_specs=pl.BlockSpec((None, HG_TILE, HG_WIDTH), lambda b, c: (b, c, 0)),
        out_shape=jax.ShapeDtypeStruct((batch, length, HG_WIDTH), BF16),
        scratch_shapes=[pltpu.VMEM((HG_HEADS, HG_D, HG_D), F32)],
        compiler_params=_params("parallel", "arbitrary"),
        name="hgrn2",
    )(hg3, loga, log1m, norm_w, jnp.asarray(_HG_SUMS, BF16), jnp.asarray(_HG_MASKS, F32))
    return out.reshape(batch * length, HG_WIDTH)


def _attn_kernel(q_ref, k_ref, v_ref, toe_ref, cst_ref, w_ref, o_ref, s_ref, *, n_blocks):
    lam = cst_ref[0:1, 0:1]
    post = cst_ref[1:2, :]
    lane = lax.broadcasted_iota(jnp.int32, (Q_BLOCK, Q_BLOCK), 1)
    first_half = lane < DA_DQK
    key_ok0 = lane >= (LEAD - N_META)
    inert_bias = jnp.where(key_ok0, 0.0, NEG)

    def near_bias(kind, kb):
        bias = toe_ref[kind]
        if kb == 0:
            bias = jnp.where(key_ok0, bias, NEG)
        return bias

    def slabs(x):
        return [x[:, c:c + Q_BLOCK] for c in range(0, x.shape[1], Q_BLOCK)]

    map_rows = 2 * Q_BLOCK
    groups = [tuple(range(i, min(i + Q_GROUP, n_blocks))) for i in range(0, n_blocks, Q_GROUP)]

    def both_maps(bias):
        return jnp.concatenate([bias, bias], axis=0)

    def stacked_q(blocks):
        parts = []
        for i in blocks:
            qi = q_ref[i * Q_BLOCK:(i + 1) * Q_BLOCK, :]
            zero = jnp.zeros_like(qi)
            parts += [jnp.where(first_half, qi, zero), jnp.where(first_half, zero, qi)]
        return jnp.concatenate(parts, axis=0)

    def group_tiles(blocks):
        first, last = blocks[0], blocks[-1]
        tiles = []
        far_end = max(first - 1, 0)
        if far_end >= 1:
            tiles.append((0, Q_BLOCK, jnp.concatenate([inert_bias] * (2 * len(blocks)), axis=0), 0))
        kb = 1
        while kb < far_end:
            width = KEY_TILE
            while kb + width // Q_BLOCK > far_end:
                width //= 2
            tiles.append((kb * Q_BLOCK, width, None, 0))
            kb += width // Q_BLOCK
        for kb in range(far_end, last + 1):
            biases, first_row = [], None
            for r, qb in enumerate(blocks):
                if kb > qb:
                    continue
                if first_row is None:
                    first_row = r * map_rows
                if kb == qb:
                    bias = near_bias(0, kb)
                elif kb == qb - 1:
                    bias = near_bias(1, kb)
                else:
                    bias = inert_bias if kb == 0 else jnp.zeros_like(inert_bias)
                biases.append(both_maps(bias))
            tiles.append((kb * Q_BLOCK, Q_BLOCK, jnp.concatenate(biases, axis=0), first_row))
        return tiles

    def merge(acc, x, first_row, op):
        if acc is None:
            return x
        if first_row == 0:
            return op(acc, x)
        return jnp.concatenate([acc[:first_row], op(acc[first_row:], x)], axis=0)

    def sweep_scores(g):
        blocks = groups[g]
        rows = len(blocks) * map_rows
        q2 = stacked_q(blocks)
        m_acc = None
        for start, width, bias, first_row in group_tiles(blocks):
            s = _dot_nt(q2[first_row:], k_ref[start:start + width, :])
            if bias is not None:
                s = s + bias
            s_ref[g % 2, first_row:rows, start:start + width] = s
            for slab in slabs(s):
                m_acc = merge(m_acc, slab, first_row, jnp.maximum)
        return m_acc.max(axis=-1, keepdims=True)

    row_max = sweep_scores(0)
    for g, blocks in enumerate(groups):
        rows = len(blocks) * map_rows
        m = row_max
        if g + 1 < len(groups):
            row_max = sweep_scores(g + 1)
        l_acc = o_acc = None
        for start, width, _, first_row in group_tiles(blocks):
            p = jnp.exp2(s_ref[g % 2, first_row:rows, start:start + width] - m[first_row:])
            for slab in slabs(p):
                l_acc = merge(l_acc, slab, first_row, jnp.add)
            o_acc = merge(o_acc, _dot(p.astype(BF16), v_ref[start:start + width, :]), first_row, jnp.add)
        o2 = o_acc * (1.0 / l_acc.sum(axis=-1, keepdims=True))
        for r, i in enumerate(blocks):
            o = o2[r * map_rows:r * map_rows + Q_BLOCK] - lam * o2[r * map_rows + Q_BLOCK:(r + 1) * map_rows]
            o = _rms(o, w_ref[...]) * post
            o_ref[i * Q_BLOCK:(i + 1) * Q_BLOCK, :] = o.astype(o_ref.dtype)


def _attn(da, toe, cst, subln_w, batch, length):
    da3 = da.reshape(batch, length, DA_COLS)
    hw = DA_HEADS
    out = pl.pallas_call(
        functools.partial(_attn_kernel, n_blocks=length // Q_BLOCK),
        grid=(batch, DA_HEADS),
        in_specs=[
            pl.BlockSpec((None, length, DA_DV), lambda b, h: (b, 0, h)),
            pl.BlockSpec((None, length, DA_DV), lambda b, h: (b, 0, hw + h)),
            pl.BlockSpec((None, length, DA_DV), lambda b, h: (b, 0, 2 * hw + h)),
            pl.BlockSpec((None, 2, Q_BLOCK, Q_BLOCK), lambda b, h: (h, 0, 0, 0)),
            pl.BlockSpec((SUBLANES, LANES), lambda b, h: (0, 0)),
            pl.BlockSpec((1, DA_DV), lambda b, h: (0, 0)),
        ],
        out_specs=pl.BlockSpec((None, length, DA_DV), lambda b, h: (b, 0, h)),
        out_shape=jax.ShapeDtypeStruct((batch, length, DA_HEADS * DA_DV), BF16),
        scratch_shapes=[pltpu.VMEM((2, Q_GROUP * 2 * Q_BLOCK, length), F32)],
        compiler_params=_params("parallel", "parallel"),
        name="diff_attn",
    )(da3, da3, da3, toe, cst, subln_w)
    return out.reshape(batch * length, DA_HEADS * DA_DV)


def _t5_bucket(dist):
    n = jnp.maximum(dist, 0)
    max_exact = REL_BUCKETS // 2
    nf = jnp.maximum(n, max_exact).astype(F32)
    large = max_exact + (jnp.log(nf / max_exact) / math.log(REL_MAX_DIST / max_exact)
                         * (REL_BUCKETS - max_exact)).astype(jnp.int32)
    large = jnp.minimum(large, REL_BUCKETS - 1)
    return jnp.where(n < max_exact, n, large)


def _attn_bias_tables(rel_bias):
    tab = rel_bias.astype(F32)
    qi = jnp.arange(Q_BLOCK, dtype=jnp.int32)[:, None]
    ki = jnp.arange(Q_BLOCK, dtype=jnp.int32)[None, :]

    def lookup(bucket):
        onehot = bucket[None, :, :, None] == jnp.arange(REL_BUCKETS, dtype=jnp.int32)
        return jnp.sum(jnp.where(onehot, tab.T[:, None, None, :], 0.0), axis=-1)

    far = tab[REL_BUCKETS - 1][:, None, None]
    diag = jnp.where((ki <= qi)[None], (lookup(_t5_bucket(qi - ki)) - far) * LOG2E, NEG)
    prev = (lookup(_t5_bucket(qi - ki + Q_BLOCK)) - far) * LOG2E
    return jnp.stack([diag, prev], axis=1)


def _out_proj_router_kernel(h_ref, ohg_ref, oda_ref, wo_ref, g_ref, router_ref, before_ref,
                            hn_ref, u_ref, route_ref, gate_ref, count_ref, seen_ref):
    @pl.when(pl.program_id(0) == 0)
    def _():
        seen_ref[...] = jnp.zeros_like(seen_ref)

    hn = (h_ref[...] + _dot(ohg_ref[...], wo_ref[:HG_WIDTH, :])
          + _dot(oda_ref[...], wo_ref[HG_WIDTH:, :]))
    hn_ref[...] = hn
    u = _rms(hn, g_ref[...])
    _store_token_major(u_ref, u)
    u_hi = u.astype(BF16)
    u_lo = (u - u_hi.astype(F32)).astype(BF16)
    logits = _dot(jnp.concatenate([u_hi, u_lo, u_hi], axis=1), router_ref[...])
    logits = logits.T[:N_EXPERTS]
    expert = lax.broadcasted_iota(jnp.int32, logits.shape, 0)

    def top(x):
        best = x.max(axis=0, keepdims=True)
        return best, jnp.where(x == best, expert, N_EXPERTS).min(axis=0, keepdims=True)

    l1, e1 = top(logits)
    l2, e2 = top(jnp.where(expert == e1, -jnp.inf, logits))
    w2 = jnp.exp(l2 - l1)
    g1 = 1.0 / (1.0 + w2)
    g2 = w2 / (1.0 + w2)

    pick1 = (expert == e1).astype(F32)
    pick2 = (expert == e2).astype(F32)
    picked = pick1 + pick2
    ahead = seen_ref[:, 0:1] + _dot(picked.astype(BF16), before_ref[...])
    rank1 = jnp.sum(pick1 * ahead, axis=0, keepdims=True).astype(jnp.int32)
    rank2 = jnp.sum(pick2 * ahead, axis=0, keepdims=True).astype(jnp.int32)
    seen = seen_ref[...] + jnp.sum(picked, axis=1, keepdims=True)
    seen_ref[...] = seen

    route_ref[...] = jnp.where(expert == 0, e1, jnp.where(expert == 1, e2, jnp.where(
        expert == 2, rank1, jnp.where(expert == 3, rank2, 0))))
    gate_ref[...] = jnp.where(expert == 0, g1, jnp.where(expert == 1, g2, 0.0)).T
    count_ref[...] = seen.astype(jnp.int32)


def _out_proj_router(h, o_hg, o_da, wo, layer, gain, router):
    n = h.shape[0]
    row = lambda i: (i, 0)
    full = lambda i: (0, 0)
    assert N_EXPERTS >= 2 * TOP_K
    lanes_of = lambda i: (0, i)
    before = jnp.asarray(np.triu(np.ones((ROW_TILE, ROW_TILE), np.float32), 1), BF16)
    return pl.pallas_call(
        _out_proj_router_kernel,
        grid=(n // ROW_TILE,),
        in_specs=[
            pl.BlockSpec((ROW_TILE, D_MODEL), row),
            pl.BlockSpec((ROW_TILE, HG_WIDTH), row),
            pl.BlockSpec((ROW_TILE, DA_HEADS * DA_DV), row),
            pl.BlockSpec((None,) + wo.shape[1:], lambda i: (layer, 0, 0)),
            pl.BlockSpec((1, D_MODEL), full),
            pl.BlockSpec(router.shape, full),
            pl.BlockSpec(before.shape, full),
        ],
        out_specs=[
            pl.BlockSpec((ROW_TILE, D_MODEL), row),
            pl.BlockSpec((ROW_TILE * TOKEN_ROWS, LANES), row),
            pl.BlockSpec((N_EXPERTS, ROW_TILE), lanes_of),
            pl.BlockSpec((ROW_TILE, N_EXPERTS), row),
            pl.BlockSpec((N_EXPERTS, LANES), full),
        ],
        out_shape=[
            jax.ShapeDtypeStruct((n, D_MODEL), F32),
            jax.ShapeDtypeStruct((n * TOKEN_ROWS, LANES), F32),
            jax.ShapeDtypeStruct((N_EXPERTS, n), jnp.int32),
            jax.ShapeDtypeStruct((n, N_EXPERTS), F32),
            jax.ShapeDtypeStruct((N_EXPERTS, LANES), jnp.int32),
        ],
        scratch_shapes=[pltpu.VMEM((N_EXPERTS, LANES), F32)],
        compiler_params=_params("arbitrary"),
        name="out_proj_router",
    )(h, o_hg, o_da, wo, gain, router, before)


def _dense_layer_kernel(h_ref, ohg_ref, oda_ref, wo_ref, g_ref, w1_ref, w3_ref, w2_ref, *rest):
    n_cast = (len(rest) - 1) // 2
    o_ref = rest[n_cast]
    hn = (h_ref[...] + _dot(ohg_ref[...], wo_ref[:HG_WIDTH, :])
          + _dot(oda_ref[...], wo_ref[HG_WIDTH:, :]))
    u = _rms(hn, g_ref[...]).astype(BF16)
    a = _dot(u, w1_ref[...])
    act = (_silu(a) * _dot(u, w3_ref[...])).astype(BF16)
    o_ref[...] = hn + _dot(act, w2_ref[...])
    _run_cast_jobs(rest[:n_cast], rest[n_cast + 1:])


def _dense_layer(h, o_hg, o_da, wo, layer, gain, w1, w3, w2, idx, to_cast, cast_idx):
    n = h.shape[0]
    steps = n // ROW_TILE
    row = lambda i: (i, 0)

    def resident(w, index):
        return pl.BlockSpec((None,) + w.shape[1:], lambda i: (index, 0, 0),
                            pipeline_mode=pl.Buffered(1))

    jobs = [(w.reshape(w.shape[0], w.shape[1] * w.shape[2], w.shape[3]), cast_idx) for w in to_cast]
    cast_in, cast_in_specs, cast_out_specs, cast_out_shape = _cast_jobs(jobs, steps)

    out = pl.pallas_call(
        _dense_layer_kernel,
        grid=(steps,),
        in_specs=[
            pl.BlockSpec((ROW_TILE, D_MODEL), row),
            pl.BlockSpec((ROW_TILE, HG_WIDTH), row),
            pl.BlockSpec((ROW_TILE, DA_HEADS * DA_DV), row),
            resident(wo, layer),
            pl.BlockSpec((1, D_MODEL), lambda i: (0, 0)),
            resident(w1, idx),
            resident(w3, idx),
            resident(w2, idx),
        ] + cast_in_specs,
        out_specs=[pl.BlockSpec((ROW_TILE, D_MODEL), row)] + cast_out_specs,
        out_shape=[jax.ShapeDtypeStruct((n, D_MODEL), F32)] + cast_out_shape,
        compiler_params=_params("arbitrary"),
        name="dense_layer",
    )(h, o_hg, o_da, wo, gain, w1, w3, w2, *cast_in)
    return out[0], [b.reshape(w.shape[1:]) for b, w in zip(out[1:], to_cast)]


DISPATCH_TILE = 1024
ISSUE_TOKENS = 8


def _token_copy(src_ref, src_tok, dst_ref, dst_tok, sem):
    src = pl.multiple_of(src_tok * TOKEN_ROWS, TOKEN_ROWS)
    dst = pl.multiple_of(dst_tok * TOKEN_ROWS, TOKEN_ROWS)
    return pltpu.make_async_copy(src_ref.at[pl.ds(src, TOKEN_ROWS), :],
                                 dst_ref.at[pl.ds(dst, TOKEN_ROWS), :], sem)


def _dispatch_kernel(dest_ref, empty_ref, u_ref, xs_ref, zero_ref, sem):
    step = pl.program_id(0)
    base = step * DISPATCH_TILE
    tile_rows = DISPATCH_TILE * TOKEN_ROWS

    def retire_tile():
        pltpu.make_async_copy(u_ref, xs_ref.at[pl.ds(0, tile_rows), :], sem).wait()

    @pl.when(step == 0)
    def _():
        zero_ref[...] = jnp.zeros_like(zero_ref)

        def clear(g, carry):
            slots = [empty_ref[g * ISSUE_TOKENS * TOP_K + j] for j in range(ISSUE_TOKENS * TOP_K)]
            for j, slot in enumerate(slots):
                _token_copy(zero_ref, 0, xs_ref, slot, sem).start(priority=j % 2)
            return carry

        n_empty = empty_ref.shape[0]
        lax.fori_loop(0, n_empty // (ISSUE_TOKENS * TOP_K), clear, 0)
        for _ in range(n_empty // DISPATCH_TILE):
            retire_tile()

    n_tokens = dest_ref.shape[0] // TOP_K

    def start(g, carry):
        r0 = g * ISSUE_TOKENS
        slots = [dest_ref[(j % TOP_K) * n_tokens + base + r0 + j // TOP_K]
                 for j in range(ISSUE_TOKENS * TOP_K)]
        for j, slot in enumerate(slots):
            _token_copy(u_ref, r0 + j // TOP_K, xs_ref, slot, sem).start(priority=j % 2)
        return carry

    lax.fori_loop(0, DISPATCH_TILE // ISSUE_TOKENS, start, 0)
    for _ in range(TOP_K):
        retire_tile()


def _dispatch(dest, empty_slots, u_tm, n_slots):
    n = dest.shape[0] // TOP_K
    assert empty_slots.shape[0] % DISPATCH_TILE == 0
    return pl.pallas_call(
        _dispatch_kernel,
        grid_spec=pltpu.PrefetchScalarGridSpec(
            num_scalar_prefetch=2,
            grid=(n // DISPATCH_TILE,),
            in_specs=[pl.BlockSpec((DISPATCH_TILE * TOKEN_ROWS, LANES), lambda i, d, e: (i, 0))],
            out_specs=pl.BlockSpec(memory_space=pl.ANY),
            scratch_shapes=[pltpu.VMEM((TOKEN_ROWS, LANES), u_tm.dtype),
                            pltpu.SemaphoreType.DMA(())],
        ),
        out_shape=jax.ShapeDtypeStruct((n_slots * TOKEN_ROWS, LANES), u_tm.dtype),
        compiler_params=_params("arbitrary"),
        name="moe_dispatch",
    )(dest, empty_slots, u_tm)


def _expert_kernel(be_ref, nused_ref, xs_ref, w1_ref, w3_ref, w2_ref, y_ref):
    del be_ref
    blk = pl.program_id(0)

    @pl.when(blk < nused_ref[0])
    def _():
        x = _load_token_major(xs_ref, MOE_TILE).astype(BF16)
        acc = jnp.zeros((MOE_TILE, D_MODEL), F32)
        for c0 in range(0, D_FF_EXPERT, FF_CHUNK):
            a = _dot(x, w1_ref[:, c0:c0 + FF_CHUNK])
            act = (_silu(a) * _dot(x, w3_ref[:, c0:c0 + FF_CHUNK])).astype(BF16)
            acc = acc + _dot(act, w2_ref[c0:c0 + FF_CHUNK, :])
        _store_token_major(y_ref, acc)

    @pl.when(blk >= nused_ref[0])
    def _():
        y_ref[...] = jnp.zeros_like(y_ref)


def _experts(block_expert, n_used, xs, w1, w3, w2):
    n_blocks = xs.shape[0] // (MOE_TILE * TOKEN_ROWS)
    wmap = lambda i, be, nu: (be[i], 0, 0)
    slots = pl.BlockSpec((MOE_TILE * TOKEN_ROWS, LANES), lambda i, be, nu: (i, 0))
    return pl.pallas_call(
        _expert_kernel,
        grid_spec=pltpu.PrefetchScalarGridSpec(
            num_scalar_prefetch=2,
            grid=(n_blocks,),
            in_specs=[
                slots,
                pl.BlockSpec((None, D_MODEL, D_FF_EXPERT), wmap),
                pl.BlockSpec((None, D_MODEL, D_FF_EXPERT), wmap),
                pl.BlockSpec((None, D_FF_EXPERT, D_MODEL), wmap),
            ],
            out_specs=slots,
        ),
        out_shape=jax.ShapeDtypeStruct(xs.shape, F32),
        compiler_params=_params("arbitrary"),
        name="moe_experts",
    )(block_expert, n_used, xs, w1, w3, w2)


def _combine_final_kernel(dest_ref, h_ref, gate_ref, y_ref, g_ref, o_ref, buf0_ref, buf1_ref, sem):
    step = pl.program_id(0)
    last = pl.num_programs(0) - 1
    bufs = (buf0_ref, buf1_ref)
    n_tokens = dest_ref.shape[0] // TOP_K

    def start_gathers(s, to_slot, rows):
        base = s * Q_BLOCK
        slots = [dest_ref[k * n_tokens + base + r] for r in rows for k in range(TOP_K)]
        for j, src in enumerate(slots):
            r, k = rows[j // TOP_K], j % TOP_K
            _token_copy(y_ref, src, bufs[to_slot].at[k], r, sem.at[to_slot]).start(priority=j % 2)

    def wait_gathers(of_slot):
        for k in range(TOP_K):
            pltpu.make_async_copy(y_ref.at[pl.ds(0, Q_BLOCK * TOKEN_ROWS), :],
                                  bufs[of_slot].at[k], sem.at[of_slot]).wait()

    @pl.when(step == 0)
    def _():
        def first(g, carry):
            start_gathers(0, 0, [g * ISSUE_TOKENS + r for r in range(ISSUE_TOKENS)])
            return carry

        lax.fori_loop(0, Q_BLOCK // ISSUE_TOKENS, first, 0)

    per_slab = Q_BLOCK // TOKEN_ROWS
    for slot in range(2):
        @pl.when(step % 2 == slot)
        def _():
            wait_gathers(slot)
            gate = gate_ref[...]
            nxt = jnp.minimum(step + 1, last)
            slabs = []
            for s in range(TOKEN_ROWS):
                cols = slice(s * LANES, (s + 1) * LANES)
                slabs.append(
                    h_ref[:, cols]
                    + gate[:, 0:1] * bufs[slot][0, pl.ds(s, Q_BLOCK, stride=TOKEN_ROWS), :]
                    + gate[:, 1:2] * bufs[slot][1, pl.ds(s, Q_BLOCK, stride=TOKEN_ROWS), :])
                rows = list(range(s * per_slab, (s + 1) * per_slab))
                for r0 in range(0, per_slab, ISSUE_TOKENS):
                    start_gathers(nxt, 1 - slot, rows[r0:r0 + ISSUE_TOKENS])
            o_ref[...] = _rms(jnp.concatenate(slabs, axis=1), g_ref[...])

            @pl.when(step == last)
            def _():
                wait_gathers(1 - slot)


def _combine_final(dest, h, gate, y, gain, batch, length, seq):
    n = h.shape[0]
    per_seq = length // Q_BLOCK
    lead_blocks = LEAD // Q_BLOCK
    return pl.pallas_call(
        _combine_final_kernel,
        grid_spec=pltpu.PrefetchScalarGridSpec(
            num_scalar_prefetch=1,
            grid=(n // Q_BLOCK,),
            in_specs=[
                pl.BlockSpec((Q_BLOCK, D_MODEL), lambda i, d: (i, 0)),
                pl.BlockSpec((Q_BLOCK, N_EXPERTS), lambda i, d: (i, 0)),
                pl.BlockSpec(memory_space=pl.ANY),
                pl.BlockSpec((1, D_MODEL), lambda i, d: (0, 0)),
            ],
            out_specs=pl.BlockSpec(
                (None, Q_BLOCK, D_MODEL),
                lambda i, d: (i // per_seq, jnp.maximum(i % per_seq - lead_blocks, 0), 0)),
            scratch_shapes=[pltpu.VMEM((TOP_K, Q_BLOCK * TOKEN_ROWS, LANES), F32),
                            pltpu.VMEM((TOP_K, Q_BLOCK * TOKEN_ROWS, LANES), F32),
                            pltpu.SemaphoreType.DMA((2,))],
        ),
        out_shape=jax.ShapeDtypeStruct((batch, seq, D_MODEL), F32),
        compiler_params=_params("arbitrary"),
        name="moe_combine_final",
    )(dest, h, gate, y, gain)


def _prefix_sum(x):
    k = x.shape[0]
    keep = (np.arange(k)[None, :] <= np.arange(k)[:, None]).reshape((k, k) + (1,) * (x.ndim - 1))
    return jnp.sum(jnp.where(keep, x[None], jnp.zeros_like(x[None])), axis=1)


def _slot_tables(route, counts, n_slots):
    n = route.shape[1]
    flat_e = route[0:TOP_K].reshape(n * TOP_K)
    rank = route[TOP_K:2 * TOP_K].reshape(n * TOP_K)
    counts = counts[:, 0]
    padded = (counts + MOE_TILE - 1) // MOE_TILE * MOE_TILE
    pad_end = _prefix_sum(padded)
    pad_start = pad_end - padded
    onehot = flat_e[:, None] == jnp.arange(N_EXPERTS, dtype=jnp.int32)[None, :]
    dest = (rank + jnp.sum(jnp.where(onehot, pad_start[None, :], 0), axis=-1)).astype(jnp.int32)
    n_blocks = n_slots // MOE_TILE
    block_first = jnp.arange(n_blocks, dtype=jnp.int32) * MOE_TILE
    block_expert = jnp.minimum(
        jnp.sum((pad_end[None, :] <= block_first[:, None]).astype(jnp.int32), axis=-1),
        N_EXPERTS - 1).astype(jnp.int32)
    n_used = (pad_end[-1:] // MOE_TILE).astype(jnp.int32)
    n_empty = n_slots - n * TOP_K
    gap_end = _prefix_sum(padded - counts)
    j = jnp.arange(n_empty, dtype=jnp.int32)
    owner = jnp.sum((j[:, None] >= gap_end[None, :]).astype(jnp.int32), axis=-1)
    first_empty = jnp.concatenate([pad_start + counts, pad_end[-1:]])
    gap_start = jnp.concatenate([jnp.zeros((1,), gap_end.dtype), gap_end])
    pick = owner[:, None] == jnp.arange(N_EXPERTS + 1, dtype=jnp.int32)[None, :]
    empty_slots = (j + jnp.sum(jnp.where(pick, (first_empty - gap_start)[None, :], 0), axis=-1))
    return dest, block_expert, n_used, empty_slots.astype(jnp.int32)


def _moe_ffn(u, route, counts, w1, w3, w2):
    n = route.shape[1]
    n_slots = (n * TOP_K // MOE_TILE + N_EXPERTS) * MOE_TILE
    dest, block_expert, n_used, empty_slots = _slot_tables(route, counts, n_slots)
    xs = _dispatch(dest, empty_slots, u, n_slots)
    y = _experts(block_expert, n_used, xs, w1, w3, w2)
    return dest, y


def kernel(x, meta, rel_bias, norm_mix, w_in, hg_lb_logits, hg_norm_w, da_lambda, da_subln_w, w_out, norm_ffn, dense_w1, dense_w3, dense_w2, moe_router, moe_w1, moe_w3, moe_w2, final_norm):
    batch, seq, d = x.shape
    length = LEAD + seq
    h = jnp.concatenate([
        jnp.zeros((batch, LEAD - N_META, d), x.dtype),
        jnp.broadcast_to(meta[None].astype(x.dtype), (batch, N_META, d)),
        x], axis=1).reshape(batch * length, d)

    toe = _attn_bias_tables(rel_bias)
    lb_cum = _prefix_sum(jax.nn.softmax(hg_lb_logits.astype(F32), axis=0))
    lb_all = jnp.clip(lb_cum - lb_cum[0:1], 0.0, LB_MAX)
    log_lb = jnp.log(lb_all)
    log_1m_lb = jnp.log1p(-lb_all)

    assert DEPTH % 2 == 0
    dense_f32 = (dense_w1, dense_w3, dense_w2)
    stacked_rows = lambda w: w.reshape(1, w.shape[0] * w.shape[1], w.shape[2])

    unsummed = None
    for l in range(DEPTH):
        if l == 0:
            jobs = [(w, 0) for w in dense_f32] + [(stacked_rows(w_out), 0), (stacked_rows(w_in), 0)]
            hg, da, casts = _mix_in(h, norm_mix[l][None], w_in[0:1].astype(BF16), 0, jobs)
            dense_b = [w[None] for w in casts[:3]]
            w_out_b = casts[3].reshape(w_out.shape)
            w_in_b = casts[4].reshape(w_in.shape)
        elif unsummed is None:
            jobs = [(w, (l + 1) // 2) for w in dense_f32] if l + 1 < DEPTH else []
            hg, da, casts = _mix_in(h, norm_mix[l][None], w_in_b, l, jobs)
            if jobs:
                dense_b = [w[None] for w in casts]
        else:
            h, hg, da = _mix_in_combine(*unsummed, norm_mix[l][None], w_in_b, l)
            unsummed = None
        lam_init = 0.8 - 0.6 * math.exp(-0.3 * l)
        lv = da_lambda[l].astype(F32)
        lam = jnp.exp(jnp.sum(lv[0] * lv[1])) - jnp.exp(jnp.sum(lv[2] * lv[3])) + lam_init
        cst = jnp.zeros((SUBLANES, LANES), F32).at[0].set(lam).at[1].set(1.0 - lam_init)
        o_hg = _hgrn(hg, log_lb[l][None], log_1m_lb[l][None], hg_norm_w[l][None], batch, length)
        o_da = _attn(da, toe, cst, da_subln_w[l][None], batch, length)
        i = l // 2
        if l % 2 == 0:
            h, moe_b = _dense_layer(h, o_hg, o_da, w_out_b, l, norm_ffn[l][None], *dense_b, 0,
                                    (moe_w1, moe_w3, moe_w2), i)
        else:
            router = jnp.zeros((d, LANES), F32).at[:, :N_EXPERTS].set(moe_router[i].astype(F32))
            r_hi = router.astype(BF16)
            r_lo = (router - r_hi.astype(F32)).astype(BF16)
            router = jnp.concatenate([r_hi, r_hi, r_lo], axis=0)
            hn, u, route, gate, counts = _out_proj_router(h, o_hg, o_da, w_out_b, l,
                                                          norm_ffn[l][None], router)
            dest, y = _moe_ffn(u, route, counts, *moe_b)
            if l + 1 < DEPTH:
                unsummed = (dest, hn, gate, y)
            else:
                return _combine_final(dest, hn, gate, y, final_norm[None], batch, length, seq)
```

```python
import functools
import math

import jax
import jax.numpy as jnp
import numpy as np
from jax import lax
from jax.experimental import pallas as pl
from jax.experimental.pallas import tpu as pltpu

D_MODEL = 1024
DEPTH = 4
N_META = 16
LEAD = 128
HG_WIDTH = 512
HG_HEADS = 4
HG_D = 128
HG_CHUNK = 64
DA_HEADS = 4
DA_DQK = 64
DA_DV = 128
Q_BLOCK = 128
KEY_TILE = 8 * Q_BLOCK
Q_GROUP = 2
LOG2E = math.log2(math.e)
Q_SCALE = DA_DQK ** -0.5 * LOG2E
REL_BUCKETS = 32
REL_MAX_DIST = 128
N_EXPERTS = 8
TOP_K = 2
D_FF_EXPERT = 3584
EPS = 1e-6
NEG = -1e30
LB_MAX = 0.999
HG_COLS = 4 * HG_WIDTH
DA_COLS = 3 * DA_HEADS * DA_DV
W_IN_COLS = HG_COLS + DA_COLS

LANES = 128
SUBLANES = 8
VMEM_LIMIT = 56 * 1024 * 1024

ROW_TILE = 256
MIX_TILE = 512
HG_TILE = 128
MOE_TILE = 256
FF_CHUNK = 1792

F32 = jnp.float32
BF16 = jnp.bfloat16


def _params(*sem):
    return pltpu.CompilerParams(dimension_semantics=sem, vmem_limit_bytes=VMEM_LIMIT)


def _dot(a, b):
    return jnp.dot(a, b, preferred_element_type=F32)


def _dot_nt(a, b):
    return lax.dot_general(a, b, (((1,), (1,)), ((), ())), preferred_element_type=F32)


def _dot_tn(a, b):
    return lax.dot_general(a, b, (((0,), (0,)), ((), ())), preferred_element_type=F32)


def _rms(x, gain):
    return x * lax.rsqrt(jnp.mean(x * x, axis=-1, keepdims=True) + EPS) * gain


TOKEN_ROWS = D_MODEL // LANES


def _store_token_major(ref, x):
    t = x.shape[0]
    for s in range(TOKEN_ROWS):
        ref[pl.ds(s, t, stride=TOKEN_ROWS), :] = x[:, s * LANES:(s + 1) * LANES]


def _load_token_major(ref, t):
    return jnp.concatenate(
        [ref[pl.ds(s, t, stride=TOKEN_ROWS), :] for s in range(TOKEN_ROWS)], axis=1)


def _silu(x):
    return x * (0.5 * jnp.tanh(0.5 * x) + 0.5)


BF16_SUBLANES = 16


def _cast_jobs(jobs, grid_steps):
    inputs, in_specs, out_specs, out_shapes = [], [], [], []
    for w, index in jobs:
        _, rows, cols = w.shape
        steps = max(s for s in range(1, grid_steps + 1)
                    if rows % s == 0 and (rows // s) % BF16_SUBLANES == 0)
        slab_rows = rows // steps
        in_specs.append(pl.BlockSpec((None, slab_rows, cols),
                                     lambda i, index=index, steps=steps: (index, jnp.minimum(i, steps - 1), 0)))
        out_specs.append(pl.BlockSpec((slab_rows, cols),
                                      lambda i, steps=steps: (jnp.minimum(i, steps - 1), 0)))
        out_shapes.append(jax.ShapeDtypeStruct((rows, cols), BF16))
        inputs.append(w)
    return inputs, in_specs, out_specs, out_shapes


def _run_cast_jobs(src_refs, dst_refs):
    for src_ref, dst_ref in zip(src_refs, dst_refs):
        dst_ref[...] = src_ref[...].astype(BF16)


def _mix_in_kernel(x_ref, g_ref, w_ref, *rest):
    n_cast = (len(rest) - 2) // 2
    hg_ref, da_ref = rest[n_cast:n_cast + 2]
    u = _rms(x_ref[...], g_ref[...]).astype(BF16)
    hg_ref[...] = _dot(u, w_ref[:, :HG_COLS])
    n_q = DA_HEADS * 2 * DA_DQK
    da_ref[:, :n_q] = (_dot(u, w_ref[:, HG_COLS:HG_COLS + n_q]) * Q_SCALE).astype(BF16)
    da_ref[:, n_q:] = _dot(u, w_ref[:, HG_COLS + n_q:]).astype(BF16)
    _run_cast_jobs(rest[:n_cast], rest[n_cast + 2:])


def _mix_in(h, gain, w, layer, to_cast=()):
    n = h.shape[0]
    steps = n // MIX_TILE
    cast_in, cast_in_specs, cast_out_specs, cast_out_shapes = _cast_jobs(to_cast, steps)
    out = pl.pallas_call(
        _mix_in_kernel,
        grid=(steps,),
        in_specs=[
            pl.BlockSpec((MIX_TILE, D_MODEL), lambda i: (i, 0)),
            pl.BlockSpec((1, D_MODEL), lambda i: (0, 0)),
            pl.BlockSpec((None, D_MODEL, W_IN_COLS), lambda i: (layer, 0, 0),
                         pipeline_mode=pl.Buffered(1)),
        ] + cast_in_specs,
        out_specs=[
            pl.BlockSpec((MIX_TILE, HG_COLS), lambda i: (i, 0)),
            pl.BlockSpec((MIX_TILE, DA_COLS), lambda i: (i, 0)),
        ] + cast_out_specs,
        out_shape=[
            jax.ShapeDtypeStruct((n, HG_COLS), F32),
            jax.ShapeDtypeStruct((n, DA_COLS), BF16),
        ] + cast_out_shapes,
        compiler_params=_params("arbitrary"),
        name="mix_in",
    )(h, gain, w, *cast_in)
    return out[0], out[1], list(out[2:])


def _mix_in_combine_kernel(dest_ref, hn_ref, gate_ref, y_ref, g_ref, w_ref,
                           h_ref, hg_ref, da_ref, buf_ref, sem):
    step = pl.program_id(0)
    last = pl.num_programs(0) - 1
    slot = step % 2
    tile_rows = MIX_TILE * TOKEN_ROWS
    n_tokens = dest_ref.shape[0] // TOP_K

    def start_gathers(s, to_slot, rows):
        base = s * MIX_TILE
        slots = [dest_ref[k * n_tokens + base + r] for r in rows for k in range(TOP_K)]
        for j, src in enumerate(slots):
            r, k = rows[j // TOP_K], j % TOP_K
            _token_copy(y_ref, src, buf_ref.at[to_slot, k], r, sem.at[to_slot]).start(priority=j % 2)

    def wait_gathers(of_slot):
        for k in range(TOP_K):
            pltpu.make_async_copy(y_ref.at[pl.ds(0, tile_rows), :], buf_ref.at[of_slot, k],
                                  sem.at[of_slot]).wait()

    @pl.when(step == 0)
    def _():
        def first(g, carry):
            start_gathers(0, 0, [g * ISSUE_TOKENS + r for r in range(ISSUE_TOKENS)])
            return carry

        lax.fori_loop(0, MIX_TILE // ISSUE_TOKENS, first, 0)

    wait_gathers(slot)
    gate = gate_ref[...]
    h = (hn_ref[...] + gate[:, 0:1] * _load_token_major(buf_ref.at[slot, 0], MIX_TILE)
         + gate[:, 1:2] * _load_token_major(buf_ref.at[slot, 1], MIX_TILE))
    h_ref[...] = h
    u = _rms(h, g_ref[...]).astype(BF16)

    nxt = jnp.minimum(step + 1, last)
    n_q = DA_HEADS * 2 * DA_DQK
    groups = [(c, c + HG_WIDTH) for c in range(0, HG_COLS, HG_WIDTH)]
    groups += [(HG_COLS, HG_COLS + n_q), (HG_COLS + n_q, W_IN_COLS)]
    per_group = -(-MIX_TILE // len(groups))
    for gi, (c0, c1) in enumerate(groups):
        proj = _dot(u, w_ref[:, c0:c1])
        if c1 <= HG_COLS:
            hg_ref[:, c0:c1] = proj
        elif c0 == HG_COLS:
            da_ref[:, :n_q] = (proj * Q_SCALE).astype(BF16)
        else:
            da_ref[:, n_q:] = proj.astype(BF16)
        rows = list(range(gi * per_group, min((gi + 1) * per_group, MIX_TILE)))
        for r0 in range(0, len(rows), ISSUE_TOKENS):
            start_gathers(nxt, 1 - slot, rows[r0:r0 + ISSUE_TOKENS])

    @pl.when(step == last)
    def _():
        wait_gathers(1 - slot)


def _mix_in_combine(dest, hn, gate, y, gain, w, layer):
    n = hn.shape[0]
    row = lambda i, d: (i, 0)
    return pl.pallas_call(
        _mix_in_combine_kernel,
        grid_spec=pltpu.PrefetchScalarGridSpec(
            num_scalar_prefetch=1,
            grid=(n // MIX_TILE,),
            in_specs=[
                pl.BlockSpec((MIX_TILE, D_MODEL), row),
                pl.BlockSpec((MIX_TILE, N_EXPERTS), row),
                pl.BlockSpec(memory_space=pl.ANY),
                pl.BlockSpec((1, D_MODEL), lambda i, d: (0, 0)),
                pl.BlockSpec((None, D_MODEL, W_IN_COLS), lambda i, d: (layer, 0, 0),
                             pipeline_mode=pl.Buffered(1)),
            ],
            out_specs=[
                pl.BlockSpec((MIX_TILE, D_MODEL), row),
                pl.BlockSpec((MIX_TILE, HG_COLS), row),
                pl.BlockSpec((MIX_TILE, DA_COLS), row),
            ],
            scratch_shapes=[pltpu.VMEM((2, TOP_K, MIX_TILE * TOKEN_ROWS, LANES), F32),
                            pltpu.SemaphoreType.DMA((2,))],
        ),
        out_shape=[
            jax.ShapeDtypeStruct((n, D_MODEL), F32),
            jax.ShapeDtypeStruct((n, HG_COLS), F32),
            jax.ShapeDtypeStruct((n, DA_COLS), BF16),
        ],
        compiler_params=_params("arbitrary"),
        name="mix_in_combine",
    )(dest, hn, gate, y, gain, w)


HG_LEVELS = (32, 16, 8, 4, 2, 1)
N_SUMS = len(HG_LEVELS) + 2


def _hgrn_consts():
    c = HG_CHUNK
    t = np.arange(c)[:, None]
    j = np.arange(c)[None, :]
    sums = np.zeros((N_SUMS, c, c), np.float32)
    masks = np.zeros((len(HG_LEVELS) + 1, c, c), np.float32)
    sums[0] = j <= t
    masks[0] = np.eye(c)
    for li, w in enumerate(HG_LEVELS, start=1):
        ref = (t // (2 * w)) * (2 * w) + w
        sums[li] = np.where(t >= ref, (j > ref) & (j <= t), (j > t) & (j <= ref))
        masks[li] = (t // (2 * w) == j // (2 * w)) & (t % (2 * w) >= w) & (j % (2 * w) < w)
    sums[N_SUMS - 1] = j > t
    sums = sums.reshape(N_SUMS * c, c)
    return np.concatenate([sums, sums], axis=1), masks


_HG_SUMS, _HG_MASKS = _hgrn_consts()


def _hgrn_kernel(hg_ref, loga_ref, log1m_ref, nw_ref, sums_ref, masks_ref, o_ref, state_ref):
    c_idx = pl.program_id(1)

    @pl.when(c_idx == 0)
    def _():
        state_ref[...] = jnp.zeros_like(state_ref)

    C = HG_CHUNK
    W = HG_WIDTH
    n_chunks = HG_TILE // C
    sums = sums_ref[...]
    nw = nw_ref[...]
    row_idx = c_idx * HG_TILE + lax.broadcasted_iota(jnp.int32, (HG_TILE, 1), 0)
    valid = row_idx >= (LEAD - N_META)
    step = lax.broadcasted_iota(jnp.int32, (C, 1), 0)
    head_cols = [slice(hd * HG_D, (hd + 1) * HG_D) for hd in range(HG_HEADS)]

    f = hg_ref[:, W:2 * W]
    qf = _silu(hg_ref[:, 0:W])
    ls = jnp.minimum(f, 0.0) - jnp.log(1.0 + jnp.exp(-jnp.abs(f)))
    cc = log1m_ref[...] + ls
    loga = loga_ref[...]
    lf = jnp.maximum(loga, cc) + jnp.log(1.0 + jnp.exp(-jnp.abs(loga - cc)))
    kk = jnp.exp(cc - f)
    lf = jnp.where(valid, lf, 0.0)
    kk = jnp.where(valid, kk, 0.0)
    lf2 = lf * LOG2E
    lf_hi = lf2.astype(BF16)
    lf_lo = (lf2 - lf_hi.astype(F32)).astype(BF16)
    vb = hg_ref[:, 2 * W:3 * W].astype(BF16)
    gate = _silu(hg_ref[:, 3 * W:4 * W])

    def side_by_side(x):
        return jnp.concatenate([x[ch * C:(ch + 1) * C] for ch in range(n_chunks)], axis=1)

    qf_w, kk_w = side_by_side(qf), side_by_side(kk)
    qb_w, kb_w = qf_w.astype(BF16), kk_w.astype(BF16)
    e = jnp.exp2(_dot(sums, jnp.concatenate([side_by_side(lf_hi), side_by_side(lf_lo)], axis=0)))
    e_b = e[0:C]
    decay_end = e_b[C - 1:C, :]
    q_in = (qf_w * e_b).astype(BF16)
    k_out = (kk_w * e[(N_SUMS - 1) * C:N_SUMS * C]).astype(BF16)
    z = [(jnp.where((step & w) != 0, qf_w, kk_w) * e[li * C:(li + 1) * C]).astype(BF16)
         for li, w in enumerate(HG_LEVELS, start=1)]
    unit_cols = [[slice(ch * W + hd * HG_D, ch * W + (hd + 1) * HG_D) for hd in range(HG_HEADS)]
                 for ch in range(n_chunks)]
    scores = []
    for ch in range(n_chunks):
        scores.append([])
        for cols in unit_cols[ch]:
            s = masks_ref[0] * _dot_nt(qb_w[:, cols], kb_w[:, cols])
            for li in range(1, len(HG_LEVELS) + 1):
                zl = z[li - 1][:, cols]
                s += masks_ref[li] * _dot_nt(zl, zl)
            scores[ch].append(s.astype(BF16))

    for ch in range(n_chunks):
        rows = slice(ch * C, (ch + 1) * C)
        for hd, cols in enumerate(head_cols):
            wide = unit_cols[ch][hd]
            st = state_ref[hd]
            v_h = vb[rows, cols]
            o = _dot_nt(q_in[:, wide], st.astype(BF16)) + _dot(scores[ch][hd], v_h)
            state_ref[hd] = st * decay_end[:, wide] + _dot_tn(v_h, k_out[:, wide])
            o = _rms(o, nw) * gate[rows, cols]
            o_ref[rows, cols] = o.astype(o_ref.dtype)


def _hgrn(hg, loga, log1m, norm_w, batch, length):
    hg3 = hg.reshape(batch, length, HG_COLS)
    out = pl.pallas_call(
        _hgrn_kernel,
        grid=(batch, length // HG_TILE),
        in_specs=[
            pl.BlockSpec((None, HG_TILE, HG_COLS), lambda b, c: (b, c, 0)),
            pl.BlockSpec((1, HG_WIDTH), lambda b, c: (0, 0)),
            pl.BlockSpec((1, HG_WIDTH), lambda b, c: (0, 0)),
            pl.BlockSpec((1, HG_D), lambda b, c: (0, 0)),
            pl.BlockSpec(_HG_SUMS.shape, lambda b, c: (0, 0)),
            pl.BlockSpec(_HG_MASKS.shape, lambda b, c: (0, 0, 0)),
        ],
        out_specs=pl.BlockSpec((None, HG_TILE, HG_WIDTH), lambda b, c: (b, c, 0)),
        out_shape=jax.ShapeDtypeStruct((batch, length, HG_WIDTH), BF16),
        scratch_shapes=[pltpu.VMEM((HG_HEADS, HG_D, HG_D), F32)],
        compiler_params=_params("parallel", "arbitrary"),
        name="hgrn2",
    )(hg3, loga, log1m, norm_w, jnp.asarray(_HG_SUMS, BF16), jnp.asarray(_HG_MASKS, F32))
    return out.reshape(batch * length, HG_WIDTH)


def _attn_kernel(q_ref, k_ref, v_ref, toe_ref, cst_ref, w_ref, o_ref, s_ref, *, n_blocks):
    lam = cst_ref[0:1, 0:1]
    post = cst_ref[1:2, :]
    lane = lax.broadcasted_iota(jnp.int32, (Q_BLOCK, Q_BLOCK), 1)
    first_half = lane < DA_DQK
    key_ok0 = lane >= (LEAD - N_META)
    inert_bias = jnp.where(key_ok0, 0.0, NEG)

    def near_bias(kind, kb):
        bias = toe_ref[kind]
        if kb == 0:
            bias = jnp.where(key_ok0, bias, NEG)
        return bias

    def slabs(x):
        return [x[:, c:c + Q_BLOCK] for c in range(0, x.shape[1], Q_BLOCK)]

    map_rows = 2 * Q_BLOCK
    groups = [tuple(range(i, min(i + Q_GROUP, n_blocks))) for i in range(0, n_blocks, Q_GROUP)]

    def both_maps(bias):
        return jnp.concatenate([bias, bias], axis=0)

    def stacked_q(blocks):
        parts = []
        for i in blocks:
            qi = q_ref[i * Q_BLOCK:(i + 1) * Q_BLOCK, :]
            zero = jnp.zeros_like(qi)
            parts += [jnp.where(first_half, qi, zero), jnp.where(first_half, zero, qi)]
        return jnp.concatenate(parts, axis=0)

    def group_tiles(blocks):
        first, last = blocks[0], blocks[-1]
        tiles = []
        far_end = max(first - 1, 0)
        if far_end >= 1:
            tiles.append((0, Q_BLOCK, jnp.concatenate([inert_bias] * (2 * len(blocks)), axis=0), 0))
        kb = 1
        while kb < far_end:
            width = KEY_TILE
            while kb + width // Q_BLOCK > far_end:
                width //= 2
            tiles.append((kb * Q_BLOCK, width, None, 0))
            kb += width // Q_BLOCK
        for kb in range(far_end, last + 1):
            biases, first_row = [], None
            for r, qb in enumerate(blocks):
                if kb > qb:
                    continue
                if first_row is None:
                    first_row = r * map_rows
                if kb == qb:
                    bias = near_bias(0, kb)
                elif kb == qb - 1:
                    bias = near_bias(1, kb)
                else:
                    bias = inert_bias if kb == 0 else jnp.zeros_like(inert_bias)
                biases.append(both_maps(bias))
            tiles.append((kb * Q_BLOCK, Q_BLOCK, jnp.concatenate(biases, axis=0), first_row))
        return tiles

    def merge(acc, x, first_row, op):
        if acc is None:
            return x
        if first_row == 0:
            return op(acc, x)
        return jnp.concatenate([acc[:first_row], op(acc[first_row:], x)], axis=0)

    def sweep_scores(g):
        blocks = groups[g]
        rows = len(blocks) * map_rows
        q2 = stacked_q(blocks)
        m_acc = None
        for start, width, bias, first_row in group_tiles(blocks):
            s = _dot_nt(q2[first_row:], k_ref[start:start + width, :])
            if bias is not None:
                s = s + bias
            s_ref[g % 2, first_row:rows, start:start + width] = s
            for slab in slabs(s):
                m_acc = merge(m_acc, slab, first_row, jnp.maximum)
        return m_acc.max(axis=-1, keepdims=True)

    row_max = sweep_scores(0)
    for g, blocks in enumerate(groups):
        rows = len(blocks) * map_rows
        m = row_max
        if g + 1 < len(groups):
            row_max = sweep_scores(g + 1)
        l_acc = o_acc = None
        for start, width, _, first_row in group_tiles(blocks):
            p = jnp.exp2(s_ref[g % 2, first_row:rows, start:start + width] - m[first_row:])
            for slab in slabs(p):
                l_acc = merge(l_acc, slab, first_row, jnp.add)
            o_acc = merge(o_acc, _dot(p.astype(BF16), v_ref[start:start + width, :]), first_row, jnp.add)
        o2 = o_acc * (1.0 / l_acc.sum(axis=-1, keepdims=True))
        for r, i in enumerate(blocks):
            o = o2[r * map_rows:r * map_rows + Q_BLOCK] - lam * o2[r * map_rows + Q_BLOCK:(r + 1) * map_rows]
            o = _rms(o, w_ref[...]) * post
            o_ref[i * Q_BLOCK:(i + 1) * Q_BLOCK, :] = o.astype(o_ref.dtype)


def _attn(da, toe, cst, subln_w, batch, length):
    da3 = da.reshape(batch, length, DA_COLS)
    hw = DA_HEADS
    out = pl.pallas_call(
        functools.partial(_attn_kernel, n_blocks=length // Q_BLOCK),
        grid=(batch, DA_HEADS),
        in_specs=[
            pl.BlockSpec((None, length, DA_DV), lambda b, h: (b, 0, h)),
            pl.BlockSpec((None, length, DA_DV), lambda b, h: (b, 0, hw + h)),
            pl.BlockSpec((None, length, DA_DV), lambda b, h: (b, 0, 2 * hw + h)),
            pl.BlockSpec((None, 2, Q_BLOCK, Q_BLOCK), lambda b, h: (h, 0, 0, 0)),
            pl.BlockSpec((SUBLANES, LANES), lambda b, h: (0, 0)),
            pl.BlockSpec((1, DA_DV), lambda b, h: (0, 0)),
        ],
        out_specs=pl.BlockSpec((None, length, DA_DV), lambda b, h: (b, 0, h)),
        out_shape=jax.ShapeDtypeStruct((batch, length, DA_HEADS * DA_DV), BF16),
        scratch_shapes=[pltpu.VMEM((2, Q_GROUP * 2 * Q_BLOCK, length), F32)],
        compiler_params=_params("parallel", "parallel"),
        name="diff_attn",
    )(da3, da3, da3, toe, cst, subln_w)
    return out.reshape(batch * length, DA_HEADS * DA_DV)


def _t5_bucket(dist):
    n = jnp.maximum(dist, 0)
    max_exact = REL_BUCKETS // 2
    nf = jnp.maximum(n, max_exact).astype(F32)
    large = max_exact + (jnp.log(nf / max_exact) / math.log(REL_MAX_DIST / max_exact)
                         * (REL_BUCKETS - max_exact)).astype(jnp.int32)
    large = jnp.minimum(large, REL_BUCKETS - 1)
    return jnp.where(n < max_exact, n, large)


def _attn_bias_tables(rel_bias):
    tab = rel_bias.astype(F32)
    qi = jnp.arange(Q_BLOCK, dtype=jnp.int32)[:, None]
    ki = jnp.arange(Q_BLOCK, dtype=jnp.int32)[None, :]

    def lookup(bucket):
        onehot = bucket[None, :, :, None] == jnp.arange(REL_BUCKETS, dtype=jnp.int32)
        return jnp.sum(jnp.where(onehot, tab.T[:, None, None, :], 0.0), axis=-1)

    far = tab[REL_BUCKETS - 1][:, None, None]
    diag = jnp.where((ki <= qi)[None], (lookup(_t5_bucket(qi - ki)) - far) * LOG2E, NEG)
    prev = (lookup(_t5_bucket(qi - ki + Q_BLOCK)) - far) * LOG2E
    return jnp.stack([diag, prev], axis=1)


def _out_proj_router_kernel(h_ref, ohg_ref, oda_ref, wo_ref, g_ref, router_ref, before_ref,
                            hn_ref, u_ref, route_ref, gate_ref, count_ref, seen_ref):
    @pl.when(pl.program_id(0) == 0)
    def _():
        seen_ref[...] = jnp.zeros_like(seen_ref)

    hn = (h_ref[...] + _dot(ohg_ref[...], wo_ref[:HG_WIDTH, :])
          + _dot(oda_ref[...], wo_ref[HG_WIDTH:, :]))
    hn_ref[...] = hn
    u = _rms(hn, g_ref[...])
    _store_token_major(u_ref, u)
    u_hi = u.astype(BF16)
    u_lo = (u - u_hi.astype(F32)).astype(BF16)
    logits = _dot(jnp.concatenate([u_hi, u_lo, u_hi], axis=1), router_ref[...])
    logits = logits.T[:N_EXPERTS]
    expert = lax.broadcasted_iota(jnp.int32, logits.shape, 0)

    def top(x):
        best = x.max(axis=0, keepdims=True)
        return best, jnp.where(x == best, expert, N_EXPERTS).min(axis=0, keepdims=True)

    l1, e1 = top(logits)
    l2, e2 = top(jnp.where(expert == e1, -jnp.inf, logits))
    w2 = jnp.exp(l2 - l1)
    g1 = 1.0 / (1.0 + w2)
    g2 = w2 / (1.0 + w2)

    pick1 = (expert == e1).astype(F32)
    pick2 = (expert == e2).astype(F32)
    picked = pick1 + pick2
    ahead = seen_ref[:, 0:1] + _dot(picked.astype(BF16), before_ref[...])
    rank1 = jnp.sum(pick1 * ahead, axis=0, keepdims=True).astype(jnp.int32)
    rank2 = jnp.sum(pick2 * ahead, axis=0, keepdims=True).astype(jnp.int32)
    seen = seen_ref[...] + jnp.sum(picked, axis=1, keepdims=True)
    seen_ref[...] = seen

    route_ref[...] = jnp.where(expert == 0, e1, jnp.where(expert == 1, e2, jnp.where(
        expert == 2, rank1, jnp.where(expert == 3, rank2, 0))))
    gate_ref[...] = jnp.where(expert == 0, g1, jnp.where(expert == 1, g2, 0.0)).T
    count_ref[...] = seen.astype(jnp.int32)


def _out_proj_router(h, o_hg, o_da, wo, layer, gain, router):
    n = h.shape[0]
    row = lambda i: (i, 0)
    full = lambda i: (0, 0)
    assert N_EXPERTS >= 2 * TOP_K
    lanes_of = lambda i: (0, i)
    before = jnp.asarray(np.triu(np.ones((ROW_TILE, ROW_TILE), np.float32), 1), BF16)
    return pl.pallas_call(
        _out_proj_router_kernel,
        grid=(n // ROW_TILE,),
        in_specs=[
            pl.BlockSpec((ROW_TILE, D_MODEL), row),
            pl.BlockSpec((ROW_TILE, HG_WIDTH), row),
            pl.BlockSpec((ROW_TILE, DA_HEADS * DA_DV), row),
            pl.BlockSpec((None,) + wo.shape[1:], lambda i: (layer, 0, 0)),
            pl.BlockSpec((1, D_MODEL), full),
            pl.BlockSpec(router.shape, full),
            pl.BlockSpec(before.shape, full),
        ],
        out_specs=[
            pl.BlockSpec((ROW_TILE, D_MODEL), row),
            pl.BlockSpec((ROW_TILE * TOKEN_ROWS, LANES), row),
            pl.BlockSpec((N_EXPERTS, ROW_TILE), lanes_of),
            pl.BlockSpec((ROW_TILE, N_EXPERTS), row),
            pl.BlockSpec((N_EXPERTS, LANES), full),
        ],
        out_shape=[
            jax.ShapeDtypeStruct((n, D_MODEL), F32),
            jax.ShapeDtypeStruct((n * TOKEN_ROWS, LANES), F32),
            jax.ShapeDtypeStruct((N_EXPERTS, n), jnp.int32),
            jax.ShapeDtypeStruct((n, N_EXPERTS), F32),
            jax.ShapeDtypeStruct((N_EXPERTS, LANES), jnp.int32),
        ],
        scratch_shapes=[pltpu.VMEM((N_EXPERTS, LANES), F32)],
        compiler_params=_params("arbitrary"),
        name="out_proj_router",
    )(h, o_hg, o_da, wo, gain, router, before)


def _dense_layer_kernel(h_ref, ohg_ref, oda_ref, wo_ref, g_ref, w1_ref, w3_ref, w2_ref, *rest):
    n_cast = (len(rest) - 1) // 2
    o_ref = rest[n_cast]
    hn = (h_ref[...] + _dot(ohg_ref[...], wo_ref[:HG_WIDTH, :])
          + _dot(oda_ref[...], wo_ref[HG_WIDTH:, :]))
    u = _rms(hn, g_ref[...]).astype(BF16)
    a = _dot(u, w1_ref[...])
    act = (_silu(a) * _dot(u, w3_ref[...])).astype(BF16)
    o_ref[...] = hn + _dot(act, w2_ref[...])
    _run_cast_jobs(rest[:n_cast], rest[n_cast + 1:])


def _dense_layer(h, o_hg, o_da, wo, layer, gain, w1, w3, w2, idx, to_cast, cast_idx):
    n = h.shape[0]
    steps = n // ROW_TILE
    row = lambda i: (i, 0)

    def resident(w, index):
        return pl.BlockSpec((None,) + w.shape[1:], lambda i: (index, 0, 0),
                            pipeline_mode=pl.Buffered(1))

    jobs = [(w.reshape(w.shape[0], w.shape[1] * w.shape[2], w.shape[3]), cast_idx) for w in to_cast]
    cast_in, cast_in_specs, cast_out_specs, cast_out_shape = _cast_jobs(jobs, steps)

    out = pl.pallas_call(
        _dense_layer_kernel,
        grid=(steps,),
        in_specs=[
            pl.BlockSpec((ROW_TILE, D_MODEL), row),
            pl.BlockSpec((ROW_TILE, HG_WIDTH), row),
            pl.BlockSpec((ROW_TILE, DA_HEADS * DA_DV), row),
            resident(wo, layer),
            pl.BlockSpec((1, D_MODEL), lambda i: (0, 0)),
            resident(w1, idx),
            resident(w3, idx),
            resident(w2, idx),
        ] + cast_in_specs,
        out_specs=[pl.BlockSpec((ROW_TILE, D_MODEL), row)] + cast_out_specs,
        out_shape=[jax.ShapeDtypeStruct((n, D_MODEL), F32)] + cast_out_shape,
        compiler_params=_params("arbitrary"),
        name="dense_layer",
    )(h, o_hg, o_da, wo, gain, w1, w3, w2, *cast_in)
    return out[0], [b.reshape(w.shape[1:]) for b, w in zip(out[1:], to_cast)]


DISPATCH_TILE = 1024
ISSUE_TOKENS = 8


def _token_copy(src_ref, src_tok, dst_ref, dst_tok, sem):
    src = pl.multiple_of(src_tok * TOKEN_ROWS, TOKEN_ROWS)
    dst = pl.multiple_of(dst_tok * TOKEN_ROWS, TOKEN_ROWS)
    return pltpu.make_async_copy(src_ref.at[pl.ds(src, TOKEN_ROWS), :],
                                 dst_ref.at[pl.ds(dst, TOKEN_ROWS), :], sem)


def _dispatch_kernel(dest_ref, empty_ref, u_ref, xs_ref, zero_ref, sem):
    step = pl.program_id(0)
    base = step * DISPATCH_TILE
    tile_rows = DISPATCH_TILE * TOKEN_ROWS

    def retire_tile():
        pltpu.make_async_copy(u_ref, xs_ref.at[pl.ds(0, tile_rows), :], sem).wait()

    @pl.when(step == 0)
    def _():
        zero_ref[...] = jnp.zeros_like(zero_ref)

        def clear(g, carry):
            slots = [empty_ref[g * ISSUE_TOKENS * TOP_K + j] for j in range(ISSUE_TOKENS * TOP_K)]
            for j, slot in enumerate(slots):
                _token_copy(zero_ref, 0, xs_ref, slot, sem).start(priority=j % 2)
            return carry

        n_empty = empty_ref.shape[0]
        lax.fori_loop(0, n_empty // (ISSUE_TOKENS * TOP_K), clear, 0)
        for _ in range(n_empty // DISPATCH_TILE):
            retire_tile()

    n_tokens = dest_ref.shape[0] // TOP_K

    def start(g, carry):
        r0 = g * ISSUE_TOKENS
        slots = [dest_ref[(j % TOP_K) * n_tokens + base + r0 + j // TOP_K]
                 for j in range(ISSUE_TOKENS * TOP_K)]
        for j, slot in enumerate(slots):
            _token_copy(u_ref, r0 + j // TOP_K, xs_ref, slot, sem).start(priority=j % 2)
        return carry

    lax.fori_loop(0, DISPATCH_TILE // ISSUE_TOKENS, start, 0)
    for _ in range(TOP_K):
        retire_tile()


def _dispatch(dest, empty_slots, u_tm, n_slots):
    n = dest.shape[0] // TOP_K
    assert empty_slots.shape[0] % DISPATCH_TILE == 0
    return pl.pallas_call(
        _dispatch_kernel,
        grid_spec=pltpu.PrefetchScalarGridSpec(
            num_scalar_prefetch=2,
            grid=(n // DISPATCH_TILE,),
            in_specs=[pl.BlockSpec((DISPATCH_TILE * TOKEN_ROWS, LANES), lambda i, d, e: (i, 0))],
            out_specs=pl.BlockSpec(memory_space=pl.ANY),
            scratch_shapes=[pltpu.VMEM((TOKEN_ROWS, LANES), u_tm.dtype),
                            pltpu.SemaphoreType.DMA(())],
        ),
        out_shape=jax.ShapeDtypeStruct((n_slots * TOKEN_ROWS, LANES), u_tm.dtype),
        compiler_params=_params("arbitrary"),
        name="moe_dispatch",
    )(dest, empty_slots, u_tm)


def _expert_kernel(be_ref, nused_ref, xs_ref, w1_ref, w3_ref, w2_ref, y_ref):
    del be_ref
    blk = pl.program_id(0)

    @pl.when(blk < nused_ref[0])
    def _():
        x = _load_token_major(xs_ref, MOE_TILE).astype(BF16)
        acc = jnp.zeros((MOE_TILE, D_MODEL), F32)
        for c0 in range(0, D_FF_EXPERT, FF_CHUNK):
            a = _dot(x, w1_ref[:, c0:c0 + FF_CHUNK])
            act = (_silu(a) * _dot(x, w3_ref[:, c0:c0 + FF_CHUNK])).astype(BF16)
            acc = acc + _dot(act, w2_ref[c0:c0 + FF_CHUNK, :])
        _store_token_major(y_ref, acc)

    @pl.when(blk >= nused_ref[0])
    def _():
        y_ref[...] = jnp.zeros_like(y_ref)


def _experts(block_expert, n_used, xs, w1, w3, w2):
    n_blocks = xs.shape[0] // (MOE_TILE * TOKEN_ROWS)
    wmap = lambda i, be, nu: (be[i], 0, 0)
    slots = pl.BlockSpec((MOE_TILE * TOKEN_ROWS, LANES), lambda i, be, nu: (i, 0))
    return pl.pallas_call(
        _expert_kernel,
        grid_spec=pltpu.PrefetchScalarGridSpec(
            num_scalar_prefetch=2,
            grid=(n_blocks,),
            in_specs=[
                slots,
                pl.BlockSpec((None, D_MODEL, D_FF_EXPERT), wmap),
                pl.BlockSpec((None, D_MODEL, D_FF_EXPERT), wmap),
                pl.BlockSpec((None, D_FF_EXPERT, D_MODEL), wmap),
            ],
            out_specs=slots,
        ),
        out_shape=jax.ShapeDtypeStruct(xs.shape, F32),
        compiler_params=_params("arbitrary"),
        name="moe_experts",
    )(block_expert, n_used, xs, w1, w3, w2)


def _combine_final_kernel(dest_ref, h_ref, gate_ref, y_ref, g_ref, o_ref, buf_ref, sem):
    step = pl.program_id(0)
    n_tokens = dest_ref.shape[0] // TOP_K

    def issue(s, slot):
        base = s * Q_BLOCK

        def body(g, carry):
            r0 = g * ISSUE_TOKENS
            slots = [dest_ref[(j % TOP_K) * n_tokens + base + r0 + j // TOP_K]
                     for j in range(ISSUE_TOKENS * TOP_K)]
            for j, src in enumerate(slots):
                _token_copy(y_ref, src, buf_ref.at[slot, j % TOP_K], r0 + j // TOP_K,
                            sem.at[slot]).start(priority=j % 2)
            return carry

        lax.fori_loop(0, Q_BLOCK // ISSUE_TOKENS, body, 0)

    @pl.when(step == 0)
    def _():
        issue(0, 0)

    for slot in range(2):
        @pl.when(step % 2 == slot)
        def _():
            @pl.when(step + 1 < pl.num_programs(0))
            def _():
                issue(step + 1, 1 - slot)

            for k in range(TOP_K):
                pltpu.make_async_copy(y_ref.at[pl.ds(0, Q_BLOCK * TOKEN_ROWS), :],
                                      buf_ref.at[slot, k], sem.at[slot]).wait()
            gate = gate_ref[...]
            h = (h_ref[...]
                 + gate[:, 0:1] * _load_token_major(buf_ref.at[slot, 0], Q_BLOCK)
                 + gate[:, 1:2] * _load_token_major(buf_ref.at[slot, 1], Q_BLOCK))
            o_ref[...] = _rms(h, g_ref[...])


def _combine_final(dest, h, gate, y, gain, batch, length, seq):
    n = h.shape[0]
    per_seq = length // Q_BLOCK
    lead_blocks = LEAD // Q_BLOCK
    return pl.pallas_call(
        _combine_final_kernel,
        grid_spec=pltpu.PrefetchScalarGridSpec(
            num_scalar_prefetch=1,
            grid=(n // Q_BLOCK,),
            in_specs=[
                pl.BlockSpec((Q_BLOCK, D_MODEL), lambda i, d: (i, 0)),
                pl.BlockSpec((Q_BLOCK, N_EXPERTS), lambda i, d: (i, 0)),
                pl.BlockSpec(memory_space=pl.ANY),
                pl.BlockSpec((1, D_MODEL), lambda i, d: (0, 0)),
            ],
            out_specs=pl.BlockSpec(
                (None, Q_BLOCK, D_MODEL),
                lambda i, d: (i // per_seq, jnp.maximum(i % per_seq - lead_blocks, 0), 0)),
            scratch_shapes=[pltpu.VMEM((2, TOP_K, Q_BLOCK * TOKEN_ROWS, LANES), F32),
                            pltpu.SemaphoreType.DMA((2,))],
        ),
        out_shape=jax.ShapeDtypeStruct((batch, seq, D_MODEL), F32),
        compiler_params=_params("arbitrary"),
        name="moe_combine_final",
    )(dest, h, gate, y, gain)


def _prefix_sum(x):
    k = x.shape[0]
    keep = (np.arange(k)[None, :] <= np.arange(k)[:, None]).reshape((k, k) + (1,) * (x.ndim - 1))
    return jnp.sum(jnp.where(keep, x[None], jnp.zeros_like(x[None])), axis=1)


def _slot_tables(route, counts, n_slots):
    n = route.shape[1]
    flat_e = route[0:TOP_K].reshape(n * TOP_K)
    rank = route[TOP_K:2 * TOP_K].reshape(n * TOP_K)
    counts = counts[:, 0]
    padded = (counts + MOE_TILE - 1) // MOE_TILE * MOE_TILE
    pad_end = _prefix_sum(padded)
    pad_start = pad_end - padded
    onehot = flat_e[:, None] == jnp.arange(N_EXPERTS, dtype=jnp.int32)[None, :]
    dest = (rank + jnp.sum(jnp.where(onehot, pad_start[None, :], 0), axis=-1)).astype(jnp.int32)
    n_blocks = n_slots // MOE_TILE
    block_first = jnp.arange(n_blocks, dtype=jnp.int32) * MOE_TILE
    block_expert = jnp.minimum(
        jnp.sum((pad_end[None, :] <= block_first[:, None]).astype(jnp.int32), axis=-1),
        N_EXPERTS - 1).astype(jnp.int32)
    n_used = (pad_end[-1:] // MOE_TILE).astype(jnp.int32)
    n_empty = n_slots - n * TOP_K
    gap_end = _prefix_sum(padded - counts)
    j = jnp.arange(n_empty, dtype=jnp.int32)
    owner = jnp.sum((j[:, None] >= gap_end[None, :]).astype(jnp.int32), axis=-1)
    first_empty = jnp.concatenate([pad_start + counts, pad_end[-1:]])
    gap_start = jnp.concatenate([jnp.zeros((1,), gap_end.dtype), gap_end])
    pick = owner[:, None] == jnp.arange(N_EXPERTS + 1, dtype=jnp.int32)[None, :]
    empty_slots = (j + jnp.sum(jnp.where(pick, (first_empty - gap_start)[None, :], 0), axis=-1))
    return dest, block_expert, n_used, empty_slots.astype(jnp.int32)


def _moe_ffn(u, route, counts, w1, w3, w2):
    n = route.shape[1]
    n_slots = (n * TOP_K // MOE_TILE + N_EXPERTS) * MOE_TILE
    dest, block_expert, n_used, empty_slots = _slot_tables(route, counts, n_slots)
    xs = _dispatch(dest, empty_slots, u, n_slots)
    y = _experts(block_expert, n_used, xs, w1, w3, w2)
    return dest, y


def kernel(x, meta, rel_bias, norm_mix, w_in, hg_lb_logits, hg_norm_w, da_lambda, da_subln_w, w_out, norm_ffn, dense_w1, dense_w3, dense_w2, moe_router, moe_w1, moe_w3, moe_w2, final_norm):
    batch, seq, d = x.shape
    length = LEAD + seq
    h = jnp.concatenate([
        jnp.zeros((batch, LEAD - N_META, d), x.dtype),
        jnp.broadcast_to(meta[None].astype(x.dtype), (batch, N_META, d)),
        x], axis=1).reshape(batch * length, d)

    toe = _attn_bias_tables(rel_bias)
    lb_cum = _prefix_sum(jax.nn.softmax(hg_lb_logits.astype(F32), axis=0))
    lb_all = jnp.clip(lb_cum - lb_cum[0:1], 0.0, LB_MAX)
    log_lb = jnp.log(lb_all)
    log_1m_lb = jnp.log1p(-lb_all)

    assert DEPTH % 2 == 0
    dense_f32 = (dense_w1, dense_w3, dense_w2)
    stacked_rows = lambda w: w.reshape(1, w.shape[0] * w.shape[1], w.shape[2])

    unsummed = None
    for l in range(DEPTH):
        if l == 0:
            jobs = [(w, 0) for w in dense_f32] + [(stacked_rows(w_out), 0), (stacked_rows(w_in), 0)]
            hg, da, casts = _mix_in(h, norm_mix[l][None], w_in[0:1].astype(BF16), 0, jobs)
            dense_b = [w[None] for w in casts[:3]]
            w_out_b = casts[3].reshape(w_out.shape)
            w_in_b = casts[4].reshape(w_in.shape)
        elif unsummed is None:
            jobs = [(w, (l + 1) // 2) for w in dense_f32] if l + 1 < DEPTH else []
            hg, da, casts = _mix_in(h, norm_mix[l][None], w_in_b, l, jobs)
            if jobs:
                dense_b = [w[None] for w in casts]
        else:
            h, hg, da = _mix_in_combine(*unsummed, norm_mix[l][None], w_in_b, l)
            unsummed = None
        lam_init = 0.8 - 0.6 * math.exp(-0.3 * l)
        lv = da_lambda[l].astype(F32)
        lam = jnp.exp(jnp.sum(lv[0] * lv[1])) - jnp.exp(jnp.sum(lv[2] * lv[3])) + lam_init
        cst = jnp.zeros((SUBLANES, LANES), F32).at[0].set(lam).at[1].set(1.0 - lam_init)
        o_hg = _hgrn(hg, log_lb[l][None], log_1m_lb[l][None], hg_norm_w[l][None], batch, length)
        o_da = _attn(da, toe, cst, da_subln_w[l][None], batch, length)
        i = l // 2
        if l % 2 == 0:
            h, moe_b = _dense_layer(h, o_hg, o_da, w_out_b, l, norm_ffn[l][None], *dense_b, 0,
                                    (moe_w1, moe_w3, moe_w2), i)
        else:
            router = jnp.zeros((d, LANES), F32).at[:, :N_EXPERTS].set(moe_router[i].astype(F32))
            r_hi = router.astype(BF16)
            r_lo = (router - r_hi.astype(F32)).astype(BF16)
            router = jnp.concatenate([r_hi, r_hi, r_lo], axis=0)
            hn, u, route, gate, counts = _out_proj_router(h, o_hg, o_da, w_out_b, l,
                                                          norm_ffn[l][None], router)
            dest, y = _moe_ffn(u, route, counts, *moe_b)
            if l + 1 < DEPTH:
                unsummed = (dest, hn, gate, y)
            else:
                return _combine_final(dest, hn, gate, y, final_norm[None], batch, length, seq)
```

```python
import functools
import math

import jax
import jax.numpy as jnp
import numpy as np
from jax import lax
from jax.experimental import pallas as pl
from jax.experimental.pallas import tpu as pltpu

D_MODEL = 1024
DEPTH = 4
N_META = 16
LEAD = 128
HG_WIDTH = 512
HG_HEADS = 4
HG_D = 128
HG_CHUNK = 64
DA_HEADS = 4
DA_DQK = 64
DA_DV = 128
Q_BLOCK = 128
KEY_TILE = 4 * Q_BLOCK
Q_GROUP = 2
LOG2E = math.log2(math.e)
Q_SCALE = DA_DQK ** -0.5 * LOG2E
REL_BUCKETS = 32
REL_MAX_DIST = 128
N_EXPERTS = 8
TOP_K = 2
D_FF_EXPERT = 3584
EPS = 1e-6
NEG = -1e30
LB_MAX = 0.999
HG_COLS = 4 * HG_WIDTH
DA_COLS = 3 * DA_HEADS * DA_DV
W_IN_COLS = HG_COLS + DA_COLS

LANES = 128
SUBLANES = 8
VMEM_LIMIT = 56 * 1024 * 1024

ROW_TILE = 256
MIX_TILE = 512
HG_TILE = 128
MOE_TILE = 256
FF_CHUNK = 1792

F32 = jnp.float32
BF16 = jnp.bfloat16


def _params(*sem):
    return pltpu.CompilerParams(dimension_semantics=sem, vmem_limit_bytes=VMEM_LIMIT)


def _dot(a, b):
    return jnp.dot(a, b, preferred_element_type=F32)


def _dot_nt(a, b):
    return lax.dot_general(a, b, (((1,), (1,)), ((), ())), preferred_element_type=F32)


def _dot_tn(a, b):
    return lax.dot_general(a, b, (((0,), (0,)), ((), ())), preferred_element_type=F32)


def _rms(x, gain):
    return x * lax.rsqrt(jnp.mean(x * x, axis=-1, keepdims=True) + EPS) * gain


TOKEN_ROWS = D_MODEL // LANES


def _store_token_major(ref, x):
    t = x.shape[0]
    for s in range(TOKEN_ROWS):
        ref[pl.ds(s, t, stride=TOKEN_ROWS), :] = x[:, s * LANES:(s + 1) * LANES]


def _load_token_major(ref, t):
    return jnp.concatenate(
        [ref[pl.ds(s, t, stride=TOKEN_ROWS), :] for s in range(TOKEN_ROWS)], axis=1)


def _silu(x):
    return x * (0.5 * jnp.tanh(0.5 * x) + 0.5)


BF16_SUBLANES = 16


def _cast_jobs(jobs, grid_steps):
    inputs, in_specs, out_specs, out_shapes = [], [], [], []
    for w, index in jobs:
        _, rows, cols = w.shape
        steps = max(s for s in range(1, grid_steps + 1)
                    if rows % s == 0 and (rows // s) % BF16_SUBLANES == 0)
        slab_rows = rows // steps
        in_specs.append(pl.BlockSpec((None, slab_rows, cols),
                                     lambda i, index=index, steps=steps: (index, jnp.minimum(i, steps - 1), 0)))
        out_specs.append(pl.BlockSpec((slab_rows, cols),
                                      lambda i, steps=steps: (jnp.minimum(i, steps - 1), 0)))
        out_shapes.append(jax.ShapeDtypeStruct((rows, cols), BF16))
        inputs.append(w)
    return inputs, in_specs, out_specs, out_shapes


def _run_cast_jobs(src_refs, dst_refs):
    for src_ref, dst_ref in zip(src_refs, dst_refs):
        dst_ref[...] = src_ref[...].astype(BF16)


def _mix_in_kernel(x_ref, g_ref, w_ref, *rest):
    n_cast = (len(rest) - 2) // 2
    hg_ref, da_ref = rest[n_cast:n_cast + 2]
    u = _rms(x_ref[...], g_ref[...]).astype(BF16)
    hg_ref[...] = _dot(u, w_ref[:, :HG_COLS])
    n_q = DA_HEADS * 2 * DA_DQK
    da_ref[:, :n_q] = (_dot(u, w_ref[:, HG_COLS:HG_COLS + n_q]) * Q_SCALE).astype(BF16)
    da_ref[:, n_q:] = _dot(u, w_ref[:, HG_COLS + n_q:]).astype(BF16)
    _run_cast_jobs(rest[:n_cast], rest[n_cast + 2:])


def _mix_in(h, gain, w, layer, to_cast=()):
    n = h.shape[0]
    steps = n // MIX_TILE
    cast_in, cast_in_specs, cast_out_specs, cast_out_shapes = _cast_jobs(to_cast, steps)
    out = pl.pallas_call(
        _mix_in_kernel,
        grid=(steps,),
        in_specs=[
            pl.BlockSpec((MIX_TILE, D_MODEL), lambda i: (i, 0)),
            pl.BlockSpec((1, D_MODEL), lambda i: (0, 0)),
            pl.BlockSpec((None, D_MODEL, W_IN_COLS), lambda i: (layer, 0, 0),
                         pipeline_mode=pl.Buffered(1)),
        ] + cast_in_specs,
        out_specs=[
            pl.BlockSpec((MIX_TILE, HG_COLS), lambda i: (i, 0)),
            pl.BlockSpec((MIX_TILE, DA_COLS), lambda i: (i, 0)),
        ] + cast_out_specs,
        out_shape=[
            jax.ShapeDtypeStruct((n, HG_COLS), F32),
            jax.ShapeDtypeStruct((n, DA_COLS), BF16),
        ] + cast_out_shapes,
        compiler_params=_params("arbitrary"),
        name="mix_in",
    )(h, gain, w, *cast_in)
    return out[0], out[1], list(out[2:])


def _mix_in_combine_kernel(dest_ref, hn_ref, gate_ref, y_ref, g_ref, w_ref,
                           h_ref, hg_ref, da_ref, buf_ref, sem):
    step = pl.program_id(0)
    last = pl.num_programs(0) - 1
    slot = step % 2
    tile_rows = MIX_TILE * TOKEN_ROWS
    n_tokens = dest_ref.shape[0] // TOP_K

    def start_gathers(s, to_slot, rows):
        base = s * MIX_TILE
        slots = [dest_ref[k * n_tokens + base + r] for r in rows for k in range(TOP_K)]
        for j, src in enumerate(slots):
            r, k = rows[j // TOP_K], j % TOP_K
            _token_copy(y_ref, src, buf_ref.at[to_slot, k], r, sem.at[to_slot]).start(priority=j % 2)

    def wait_gathers(of_slot):
        for k in range(TOP_K):
            pltpu.make_async_copy(y_ref.at[pl.ds(0, tile_rows), :], buf_ref.at[of_slot, k],
                                  sem.at[of_slot]).wait()

    @pl.when(step == 0)
    def _():
        def first(g, carry):
            start_gathers(0, 0, [g * ISSUE_TOKENS + r for r in range(ISSUE_TOKENS)])
            return carry

        lax.fori_loop(0, MIX_TILE // ISSUE_TOKENS, first, 0)

    wait_gathers(slot)
    gate = gate_ref[...]
    h = (hn_ref[...] + gate[:, 0:1] * _load_token_major(buf_ref.at[slot, 0], MIX_TILE)
         + gate[:, 1:2] * _load_token_major(buf_ref.at[slot, 1], MIX_TILE))
    h_ref[...] = h
    u = _rms(h, g_ref[...]).astype(BF16)

    nxt = jnp.minimum(step + 1, last)
    n_q = DA_HEADS * 2 * DA_DQK
    groups = [(c, c + HG_WIDTH) for c in range(0, HG_COLS, HG_WIDTH)]
    groups += [(HG_COLS, HG_COLS + n_q), (HG_COLS + n_q, W_IN_COLS)]
    per_group = -(-MIX_TILE // len(groups))
    for gi, (c0, c1) in enumerate(groups):
        proj = _dot(u, w_ref[:, c0:c1])
        if c1 <= HG_COLS:
            hg_ref[:, c0:c1] = proj
        elif c0 == HG_COLS:
            da_ref[:, :n_q] = (proj * Q_SCALE).astype(BF16)
        else:
            da_ref[:, n_q:] = proj.astype(BF16)
        rows = list(range(gi * per_group, min((gi + 1) * per_group, MIX_TILE)))
        for r0 in range(0, len(rows), ISSUE_TOKENS):
            start_gathers(nxt, 1 - slot, rows[r0:r0 + ISSUE_TOKENS])

    @pl.when(step == last)
    def _():
        wait_gathers(1 - slot)


def _mix_in_combine(dest, hn, gate, y, gain, w, layer):
    n = hn.shape[0]
    row = lambda i, d: (i, 0)
    return pl.pallas_call(
        _mix_in_combine_kernel,
        grid_spec=pltpu.PrefetchScalarGridSpec(
            num_scalar_prefetch=1,
            grid=(n // MIX_TILE,),
            in_specs=[
                pl.BlockSpec((MIX_TILE, D_MODEL), row),
                pl.BlockSpec((MIX_TILE, N_EXPERTS), row),
                pl.BlockSpec(memory_space=pl.ANY),
                pl.BlockSpec((1, D_MODEL), lambda i, d: (0, 0)),
                pl.BlockSpec((None, D_MODEL, W_IN_COLS), lambda i, d: (layer, 0, 0),
                             pipeline_mode=pl.Buffered(1)),
            ],
            out_specs=[
                pl.BlockSpec((MIX_TILE, D_MODEL), row),
                pl.BlockSpec((MIX_TILE, HG_COLS), row),
                pl.BlockSpec((MIX_TILE, DA_COLS), row),
            ],
            scratch_shapes=[pltpu.VMEM((2, TOP_K, MIX_TILE * TOKEN_ROWS, LANES), F32),
                            pltpu.SemaphoreType.DMA((2,))],
        ),
        out_shape=[
            jax.ShapeDtypeStruct((n, D_MODEL), F32),
            jax.ShapeDtypeStruct((n, HG_COLS), F32),
            jax.ShapeDtypeStruct((n, DA_COLS), BF16),
        ],
        compiler_params=_params("arbitrary"),
        name="mix_in_combine",
    )(dest, hn, gate, y, gain, w)


HG_LEVELS = (32, 16, 8, 4, 2, 1)
N_SUMS = len(HG_LEVELS) + 2


def _hgrn_consts():
    c = HG_CHUNK
    t = np.arange(c)[:, None]
    j = np.arange(c)[None, :]
    sums = np.zeros((N_SUMS, c, c), np.float32)
    masks = np.zeros((len(HG_LEVELS) + 1, c, c), np.float32)
    sums[0] = j <= t
    masks[0] = np.eye(c)
    for li, w in enumerate(HG_LEVELS, start=1):
        ref = (t // (2 * w)) * (2 * w) + w
        sums[li] = np.where(t >= ref, (j > ref) & (j <= t), (j > t) & (j <= ref))
        masks[li] = (t // (2 * w) == j // (2 * w)) & (t % (2 * w) >= w) & (j % (2 * w) < w)
    sums[N_SUMS - 1] = j > t
    sums = sums.reshape(N_SUMS * c, c)
    return np.concatenate([sums, sums], axis=1), masks


_HG_SUMS, _HG_MASKS = _hgrn_consts()


def _hgrn_kernel(hg_ref, loga_ref, log1m_ref, nw_ref, sums_ref, masks_ref, o_ref, state_ref):
    c_idx = pl.program_id(1)

    @pl.when(c_idx == 0)
    def _():
        state_ref[...] = jnp.zeros_like(state_ref)

    C = HG_CHUNK
    W = HG_WIDTH
    n_chunks = HG_TILE // C
    sums = sums_ref[...]
    nw = nw_ref[...]
    row_idx = c_idx * HG_TILE + lax.broadcasted_iota(jnp.int32, (HG_TILE, 1), 0)
    valid = row_idx >= (LEAD - N_META)
    step = lax.broadcasted_iota(jnp.int32, (C, 1), 0)
    head_cols = [slice(hd * HG_D, (hd + 1) * HG_D) for hd in range(HG_HEADS)]

    f = hg_ref[:, W:2 * W]
    qf = _silu(hg_ref[:, 0:W])
    ls = jnp.minimum(f, 0.0) - jnp.log(1.0 + jnp.exp(-jnp.abs(f)))
    cc = log1m_ref[...] + ls
    loga = loga_ref[...]
    lf = jnp.maximum(loga, cc) + jnp.log(1.0 + jnp.exp(-jnp.abs(loga - cc)))
    kk = jnp.exp(cc - f)
    lf = jnp.where(valid, lf, 0.0)
    kk = jnp.where(valid, kk, 0.0)
    lf2 = lf * LOG2E
    lf_hi = lf2.astype(BF16)
    lf_lo = (lf2 - lf_hi.astype(F32)).astype(BF16)
    vb = hg_ref[:, 2 * W:3 * W].astype(BF16)
    gate = _silu(hg_ref[:, 3 * W:4 * W])

    def side_by_side(x):
        return jnp.concatenate([x[ch * C:(ch + 1) * C] for ch in range(n_chunks)], axis=1)

    qf_w, kk_w = side_by_side(qf), side_by_side(kk)
    qb_w, kb_w = qf_w.astype(BF16), kk_w.astype(BF16)
    e = jnp.exp2(_dot(sums, jnp.concatenate([side_by_side(lf_hi), side_by_side(lf_lo)], axis=0)))
    e_b = e[0:C]
    decay_end = e_b[C - 1:C, :]
    q_in = (qf_w * e_b).astype(BF16)
    k_out = (kk_w * e[(N_SUMS - 1) * C:N_SUMS * C]).astype(BF16)
    z = [(jnp.where((step & w) != 0, qf_w, kk_w) * e[li * C:(li + 1) * C]).astype(BF16)
         for li, w in enumerate(HG_LEVELS, start=1)]
    unit_cols = [[slice(ch * W + hd * HG_D, ch * W + (hd + 1) * HG_D) for hd in range(HG_HEADS)]
                 for ch in range(n_chunks)]
    scores = []
    for ch in range(n_chunks):
        scores.append([])
        for cols in unit_cols[ch]:
            s = masks_ref[0] * _dot_nt(qb_w[:, cols], kb_w[:, cols])
            for li in range(1, len(HG_LEVELS) + 1):
                zl = z[li - 1][:, cols]
                s += masks_ref[li] * _dot_nt(zl, zl)
            scores[ch].append(s.astype(BF16))

    for ch in range(n_chunks):
        rows = slice(ch * C, (ch + 1) * C)
        for hd, cols in enumerate(head_cols):
            wide = unit_cols[ch][hd]
            st = state_ref[hd]
            v_h = vb[rows, cols]
            o = _dot_nt(q_in[:, wide], st.astype(BF16)) + _dot(scores[ch][hd], v_h)
            state_ref[hd] = st * decay_end[:, wide] + _dot_tn(v_h, k_out[:, wide])
            o = _rms(o, nw) * gate[rows, cols]
            o_ref[rows, cols] = o.astype(o_ref.dtype)


def _hgrn(hg, loga, log1m, norm_w, batch, length):
    hg3 = hg.reshape(batch, length, HG_COLS)
    out = pl.pallas_call(
        _hgrn_kernel,
        grid=(batch, length // HG_TILE),
        in_specs=[
            pl.BlockSpec((None, HG_TILE, HG_COLS), lambda b, c: (b, c, 0)),
            pl.BlockSpec((1, HG_WIDTH), lambda b, c: (0, 0)),
            pl.BlockSpec((1, HG_WIDTH), lambda b, c: (0, 0)),
            pl.BlockSpec((1, HG_D), lambda b, c: (0, 0)),
            pl.BlockSpec(_HG_SUMS.shape, lambda b, c: (0, 0)),
            pl.BlockSpec(_HG_MASKS.shape, lambda b, c: (0, 0, 0)),
        ],
        out_specs=pl.BlockSpec((None, HG_TILE, HG_WIDTH), lambda b, c: (b, c, 0)),
        out_shape=jax.ShapeDtypeStruct((batch, length, HG_WIDTH), BF16),
        scratch_shapes=[pltpu.VMEM((HG_HEADS, HG_D, HG_D), F32)],
        compiler_params=_params("parallel", "arbitrary"),
        name="hgrn2",
    )(hg3, loga, log1m, norm_w, jnp.asarray(_HG_SUMS, BF16), jnp.asarray(_HG_MASKS, F32))
    return out.reshape(batch * length, HG_WIDTH)


def _attn_kernel(q_ref, k_ref, v_ref, toe_ref, cst_ref, w_ref, o_ref, s_ref, *, n_blocks):
    lam = cst_ref[0:1, 0:1]
    post = cst_ref[1:2, :]
    lane = lax.broadcasted_iota(jnp.int32, (Q_BLOCK, Q_BLOCK), 1)
    first_half = lane < DA_DQK
    key_ok0 = lane >= (LEAD - N_META)
    inert_bias = jnp.where(key_ok0, 0.0, NEG)

    def near_bias(kind, kb):
        bias = toe_ref[kind]
        if kb == 0:
            bias = jnp.where(key_ok0, bias, NEG)
        return bias

    def slabs(x):
        return [x[:, c:c + Q_BLOCK] for c in range(0, x.shape[1], Q_BLOCK)]

    map_rows = 2 * Q_BLOCK
    groups = [tuple(range(i, min(i + Q_GROUP, n_blocks))) for i in range(0, n_blocks, Q_GROUP)]

    def both_maps(bias):
        return jnp.concatenate([bias, bias], axis=0)

    def stacked_q(blocks):
        parts = []
        for i in blocks:
            qi = q_ref[i * Q_BLOCK:(i + 1) * Q_BLOCK, :]
            zero = jnp.zeros_like(qi)
            parts += [jnp.where(first_half, qi, zero), jnp.where(first_half, zero, qi)]
        return jnp.concatenate(parts, axis=0)

    def group_tiles(blocks):
        first, last = blocks[0], blocks[-1]
        tiles = []
        far_end = max(first - 1, 0)
        if far_end >= 1:
            tiles.append((0, Q_BLOCK, jnp.concatenate([inert_bias] * (2 * len(blocks)), axis=0), 0))
        kb = 1
        while kb < far_end:
            width = KEY_TILE
            while kb + width // Q_BLOCK > far_end:
                width //= 2
            tiles.append((kb * Q_BLOCK, width, None, 0))
            kb += width // Q_BLOCK
        for kb in range(far_end, last + 1):
            biases, first_row = [], None
            for r, qb in enumerate(blocks):
                if kb > qb:
                    continue
                if first_row is None:
                    first_row = r * map_rows
                if kb == qb:
                    bias = near_bias(0, kb)
                elif kb == qb - 1:
                    bias = near_bias(1, kb)
                else:
                    bias = inert_bias if kb == 0 else jnp.zeros_like(inert_bias)
                biases.append(both_maps(bias))
            tiles.append((kb * Q_BLOCK, Q_BLOCK, jnp.concatenate(biases, axis=0), first_row))
        return tiles

    def merge(acc, x, first_row, op):
        if acc is None:
            return x
        if first_row == 0:
            return op(acc, x)
        return jnp.concatenate([acc[:first_row], op(acc[first_row:], x)], axis=0)

    def sweep_scores(g):
        blocks = groups[g]
        rows = len(blocks) * map_rows
        q2 = stacked_q(blocks)
        m_acc = None
        for start, width, bias, first_row in group_tiles(blocks):
            s = _dot_nt(q2[first_row:], k_ref[start:start + width, :])
            if bias is not None:
                s = s + bias
            s_ref[g % 2, first_row:rows, start:start + width] = s
            for slab in slabs(s):
                m_acc = merge(m_acc, slab, first_row, jnp.maximum)
        return m_acc.max(axis=-1, keepdims=True)

    row_max = sweep_scores(0)
    for g, blocks in enumerate(groups):
        rows = len(blocks) * map_rows
        m = row_max
        if g + 1 < len(groups):
            row_max = sweep_scores(g + 1)
        l_acc = o_acc = None
        for start, width, _, first_row in group_tiles(blocks):
            p = jnp.exp2(s_ref[g % 2, first_row:rows, start:start + width] - m[first_row:])
            for slab in slabs(p):
                l_acc = merge(l_acc, slab, first_row, jnp.add)
            o_acc = merge(o_acc, _dot(p.astype(BF16), v_ref[start:start + width, :]), first_row, jnp.add)
        o2 = o_acc * (1.0 / l_acc.sum(axis=-1, keepdims=True))
        for r, i in enumerate(blocks):
            o = o2[r * map_rows:r * map_rows + Q_BLOCK] - lam * o2[r * map_rows + Q_BLOCK:(r + 1) * map_rows]
            o = _rms(o, w_ref[...]) * post
            o_ref[i * Q_BLOCK:(i + 1) * Q_BLOCK, :] = o.astype(o_ref.dtype)


def _attn(da, toe, cst, subln_w, batch, length):
    da3 = da.reshape(batch, length, DA_COLS)
    hw = DA_HEADS
    out = pl.pallas_call(
        functools.partial(_attn_kernel, n_blocks=length // Q_BLOCK),
        grid=(batch, DA_HEADS),
        in_specs=[
            pl.BlockSpec((None, length, DA_DV), lambda b, h: (b, 0, h)),
            pl.BlockSpec((None, length, DA_DV), lambda b, h: (b, 0, hw + h)),
            pl.BlockSpec((None, length, DA_DV), lambda b, h: (b, 0, 2 * hw + h)),
            pl.BlockSpec((None, 2, Q_BLOCK, Q_BLOCK), lambda b, h: (h, 0, 0, 0)),
            pl.BlockSpec((SUBLANES, LANES), lambda b, h: (0, 0)),
            pl.BlockSpec((1, DA_DV), lambda b, h: (0, 0)),
        ],
        out_specs=pl.BlockSpec((None, length, DA_DV), lambda b, h: (b, 0, h)),
        out_shape=jax.ShapeDtypeStruct((batch, length, DA_HEADS * DA_DV), BF16),
        scratch_shapes=[pltpu.VMEM((2, Q_GROUP * 2 * Q_BLOCK, length), F32)],
        compiler_params=_params("parallel", "parallel"),
        name="diff_attn",
    )(da3, da3, da3, toe, cst, subln_w)
    return out.reshape(batch * length, DA_HEADS * DA_DV)


def _t5_bucket(dist):
    n = jnp.maximum(dist, 0)
    max_exact = REL_BUCKETS // 2
    nf = jnp.maximum(n, max_exact).astype(F32)
    large = max_exact + (jnp.log(nf / max_exact) / math.log(REL_MAX_DIST / max_exact)
                         * (REL_BUCKETS - max_exact)).astype(jnp.int32)
    large = jnp.minimum(large, REL_BUCKETS - 1)
    return jnp.where(n < max_exact, n, large)


def _attn_bias_tables(rel_bias):
    tab = rel_bias.astype(F32)
    qi = jnp.arange(Q_BLOCK, dtype=jnp.int32)[:, None]
    ki = jnp.arange(Q_BLOCK, dtype=jnp.int32)[None, :]

    def lookup(bucket):
        onehot = bucket[None, :, :, None] == jnp.arange(REL_BUCKETS, dtype=jnp.int32)
        return jnp.sum(jnp.where(onehot, tab.T[:, None, None, :], 0.0), axis=-1)

    far = tab[REL_BUCKETS - 1][:, None, None]
    diag = jnp.where((ki <= qi)[None], (lookup(_t5_bucket(qi - ki)) - far) * LOG2E, NEG)
    prev = (lookup(_t5_bucket(qi - ki + Q_BLOCK)) - far) * LOG2E
    return jnp.stack([diag, prev], axis=1)


def _out_proj_router_kernel(h_ref, ohg_ref, oda_ref, wo_ref, g_ref, router_ref, before_ref,
                            hn_ref, u_ref, route_ref, gate_ref, count_ref, seen_ref):
    @pl.when(pl.program_id(0) == 0)
    def _():
        seen_ref[...] = jnp.zeros_like(seen_ref)

    hn = (h_ref[...] + _dot(ohg_ref[...], wo_ref[:HG_WIDTH, :])
          + _dot(oda_ref[...], wo_ref[HG_WIDTH:, :]))
    hn_ref[...] = hn
    u = _rms(hn, g_ref[...])
    _store_token_major(u_ref, u)
    u_hi = u.astype(BF16)
    u_lo = (u - u_hi.astype(F32)).astype(BF16)
    logits = _dot(jnp.concatenate([u_hi, u_lo, u_hi], axis=1), router_ref[...])
    logits = logits.T[:N_EXPERTS]
    expert = lax.broadcasted_iota(jnp.int32, logits.shape, 0)

    def top(x):
        best = x.max(axis=0, keepdims=True)
        return best, jnp.where(x == best, expert, N_EXPERTS).min(axis=0, keepdims=True)

    l1, e1 = top(logits)
    l2, e2 = top(jnp.where(expert == e1, -jnp.inf, logits))
    w2 = jnp.exp(l2 - l1)
    g1 = 1.0 / (1.0 + w2)
    g2 = w2 / (1.0 + w2)

    pick1 = (expert == e1).astype(F32)
    pick2 = (expert == e2).astype(F32)
    picked = pick1 + pick2
    ahead = seen_ref[:, 0:1] + _dot(picked.astype(BF16), before_ref[...])
    rank1 = jnp.sum(pick1 * ahead, axis=0, keepdims=True).astype(jnp.int32)
    rank2 = jnp.sum(pick2 * ahead, axis=0, keepdims=True).astype(jnp.int32)
    seen = seen_ref[...] + jnp.sum(picked, axis=1, keepdims=True)
    seen_ref[...] = seen

    route_ref[...] = jnp.where(expert == 0, e1, jnp.where(expert == 1, e2, jnp.where(
        expert == 2, rank1, jnp.where(expert == 3, rank2, 0))))
    gate_ref[...] = jnp.where(expert == 0, g1, jnp.where(expert == 1, g2, 0.0)).T
    count_ref[...] = seen.astype(jnp.int32)


def _out_proj_router(h, o_hg, o_da, wo, layer, gain, router):
    n = h.shape[0]
    row = lambda i: (i, 0)
    full = lambda i: (0, 0)
    assert N_EXPERTS >= 2 * TOP_K
    lanes_of = lambda i: (0, i)
    before = jnp.asarray(np.triu(np.ones((ROW_TILE, ROW_TILE), np.float32), 1), BF16)
    return pl.pallas_call(
        _out_proj_router_kernel,
        grid=(n // ROW_TILE,),
        in_specs=[
            pl.BlockSpec((ROW_TILE, D_MODEL), row),
            pl.BlockSpec((ROW_TILE, HG_WIDTH), row),
            pl.BlockSpec((ROW_TILE, DA_HEADS * DA_DV), row),
            pl.BlockSpec((None,) + wo.shape[1:], lambda i: (layer, 0, 0)),
            pl.BlockSpec((1, D_MODEL), full),
            pl.BlockSpec(router.shape, full),
            pl.BlockSpec(before.shape, full),
        ],
        out_specs=[
            pl.BlockSpec((ROW_TILE, D_MODEL), row),
            pl.BlockSpec((ROW_TILE * TOKEN_ROWS, LANES), row),
            pl.BlockSpec((N_EXPERTS, ROW_TILE), lanes_of),
            pl.BlockSpec((ROW_TILE, N_EXPERTS), row),
            pl.BlockSpec((N_EXPERTS, LANES), full),
        ],
        out_shape=[
            jax.ShapeDtypeStruct((n, D_MODEL), F32),
            jax.ShapeDtypeStruct((n * TOKEN_ROWS, LANES), F32),
            jax.ShapeDtypeStruct((N_EXPERTS, n), jnp.int32),
            jax.ShapeDtypeStruct((n, N_EXPERTS), F32),
            jax.ShapeDtypeStruct((N_EXPERTS, LANES), jnp.int32),
        ],
        scratch_shapes=[pltpu.VMEM((N_EXPERTS, LANES), F32)],
        compiler_params=_params("arbitrary"),
        name="out_proj_router",
    )(h, o_hg, o_da, wo, gain, router, before)


def _dense_layer_kernel(h_ref, ohg_ref, oda_ref, wo_ref, g_ref, w1_ref, w3_ref, w2_ref, *rest):
    n_cast = (len(rest) - 1) // 2
    o_ref = rest[n_cast]
    hn = (h_ref[...] + _dot(ohg_ref[...], wo_ref[:HG_WIDTH, :])
          + _dot(oda_ref[...], wo_ref[HG_WIDTH:, :]))
    u = _rms(hn, g_ref[...]).astype(BF16)
    a = _dot(u, w1_ref[...])
    act = (_silu(a) * _dot(u, w3_ref[...])).astype(BF16)
    o_ref[...] = hn + _dot(act, w2_ref[...])
    _run_cast_jobs(rest[:n_cast], rest[n_cast + 1:])


def _dense_layer(h, o_hg, o_da, wo, layer, gain, w1, w3, w2, idx, to_cast, cast_idx):
    n = h.shape[0]
    steps = n // ROW_TILE
    row = lambda i: (i, 0)

    def resident(w, index):
        return pl.BlockSpec((None,) + w.shape[1:], lambda i: (index, 0, 0),
                            pipeline_mode=pl.Buffered(1))

    jobs = [(w.reshape(w.shape[0], w.shape[1] * w.shape[2], w.shape[3]), cast_idx) for w in to_cast]
    cast_in, cast_in_specs, cast_out_specs, cast_out_shape = _cast_jobs(jobs, steps)

    out = pl.pallas_call(
        _dense_layer_kernel,
        grid=(steps,),
        in_specs=[
            pl.BlockSpec((ROW_TILE, D_MODEL), row),
            pl.BlockSpec((ROW_TILE, HG_WIDTH), row),
            pl.BlockSpec((ROW_TILE, DA_HEADS * DA_DV), row),
            resident(wo, layer),
            pl.BlockSpec((1, D_MODEL), lambda i: (0, 0)),
            resident(w1, idx),
            resident(w3, idx),
            resident(w2, idx),
        ] + cast_in_specs,
        out_specs=[pl.BlockSpec((ROW_TILE, D_MODEL), row)] + cast_out_specs,
        out_shape=[jax.ShapeDtypeStruct((n, D_MODEL), F32)] + cast_out_shape,
        compiler_params=_params("arbitrary"),
        name="dense_layer",
    )(h, o_hg, o_da, wo, gain, w1, w3, w2, *cast_in)
    return out[0], [b.reshape(w.shape[1:]) for b, w in zip(out[1:], to_cast)]


DISPATCH_TILE = 1024
ISSUE_TOKENS = 8


def _token_copy(src_ref, src_tok, dst_ref, dst_tok, sem):
    src = pl.multiple_of(src_tok * TOKEN_ROWS, TOKEN_ROWS)
    dst = pl.multiple_of(dst_tok * TOKEN_ROWS, TOKEN_ROWS)
    return pltpu.make_async_copy(src_ref.at[pl.ds(src, TOKEN_ROWS), :],
                                 dst_ref.at[pl.ds(dst, TOKEN_ROWS), :], sem)


def _dispatch_kernel(dest_ref, empty_ref, u_ref, xs_ref, zero_ref, sem):
    step = pl.program_id(0)
    base = step * DISPATCH_TILE
    tile_rows = DISPATCH_TILE * TOKEN_ROWS

    def retire_tile():
        pltpu.make_async_copy(u_ref, xs_ref.at[pl.ds(0, tile_rows), :], sem).wait()

    @pl.when(step == 0)
    def _():
        zero_ref[...] = jnp.zeros_like(zero_ref)

        def clear(g, carry):
            slots = [empty_ref[g * ISSUE_TOKENS * TOP_K + j] for j in range(ISSUE_TOKENS * TOP_K)]
            for j, slot in enumerate(slots):
                _token_copy(zero_ref, 0, xs_ref, slot, sem).start(priority=j % 2)
            return carry

        n_empty = empty_ref.shape[0]
        lax.fori_loop(0, n_empty // (ISSUE_TOKENS * TOP_K), clear, 0)
        for _ in range(n_empty // DISPATCH_TILE):
            retire_tile()

    n_tokens = dest_ref.shape[0] // TOP_K

    def start(g, carry):
        r0 = g * ISSUE_TOKENS
        slots = [dest_ref[(j % TOP_K) * n_tokens + base + r0 + j // TOP_K]
                 for j in range(ISSUE_TOKENS * TOP_K)]
        for j, slot in enumerate(slots):
            _token_copy(u_ref, r0 + j // TOP_K, xs_ref, slot, sem).start(priority=j % 2)
        return carry

    lax.fori_loop(0, DISPATCH_TILE // ISSUE_TOKENS, start, 0)
    for _ in range(TOP_K):
        retire_tile()


def _dispatch(dest, empty_slots, u_tm, n_slots):
    n = dest.shape[0] // TOP_K
    assert empty_slots.shape[0] % DISPATCH_TILE == 0
    return pl.pallas_call(
        _dispatch_kernel,
        grid_spec=pltpu.PrefetchScalarGridSpec(
            num_scalar_prefetch=2,
            grid=(n // DISPATCH_TILE,),
            in_specs=[pl.BlockSpec((DISPATCH_TILE * TOKEN_ROWS, LANES), lambda i, d, e: (i, 0))],
            out_specs=pl.BlockSpec(memory_space=pl.ANY),
            scratch_shapes=[pltpu.VMEM((TOKEN_ROWS, LANES), u_tm.dtype),
                            pltpu.SemaphoreType.DMA(())],
        ),
        out_shape=jax.ShapeDtypeStruct((n_slots * TOKEN_ROWS, LANES), u_tm.dtype),
        compiler_params=_params("arbitrary"),
        name="moe_dispatch",
    )(dest, empty_slots, u_tm)


def _expert_kernel(be_ref, nused_ref, tok_ref, u_ref, w1_ref, w3_ref, w2_ref, y_ref, xbuf_ref, sem):
    del be_ref
    blk = pl.program_id(0)
    last = pl.num_programs(0) - 1
    n_used = nused_ref[0]
    slot = blk % 2

    def start_gathers(b, to_slot, rows):
        toks = [tok_ref[b * MOE_TILE + r] for r in rows]
        for j, tok in enumerate(toks):
            _token_copy(u_ref, tok, xbuf_ref.at[to_slot], rows[j], sem.at[to_slot]).start(priority=j % 2)

    def wait_gathers(of_slot):
        pltpu.make_async_copy(u_ref.at[pl.ds(0, MOE_TILE * TOKEN_ROWS), :], xbuf_ref.at[of_slot],
                              sem.at[of_slot]).wait()

    @pl.when(blk == 0)
    def _():
        def first(g, carry):
            start_gathers(0, 0, [g * ISSUE_TOKENS + r for r in range(ISSUE_TOKENS)])
            return carry

        lax.fori_loop(0, MOE_TILE // ISSUE_TOKENS, first, 0)

    @pl.when(blk < n_used)
    def _():
        wait_gathers(slot)
        x = _load_token_major(xbuf_ref.at[slot], MOE_TILE).astype(BF16)
        nxt = jnp.minimum(blk + 1, last)
        chunks = list(range(0, D_FF_EXPERT, FF_CHUNK))
        per_chunk = -(-MOE_TILE // len(chunks))
        acc = jnp.zeros((MOE_TILE, D_MODEL), F32)
        for ci, c0 in enumerate(chunks):
            a = _dot(x, w1_ref[:, c0:c0 + FF_CHUNK])
            act = (_silu(a) * _dot(x, w3_ref[:, c0:c0 + FF_CHUNK])).astype(BF16)
            acc = acc + _dot(act, w2_ref[c0:c0 + FF_CHUNK, :])
            rows = list(range(ci * per_chunk, min((ci + 1) * per_chunk, MOE_TILE)))
            for r0 in range(0, len(rows), 2 * ISSUE_TOKENS):
                start_gathers(nxt, 1 - slot, rows[r0:r0 + 2 * ISSUE_TOKENS])
        _store_token_major(y_ref, acc)

        @pl.when(blk == last)
        def _():
            wait_gathers(1 - slot)

    @pl.when(blk >= n_used)
    def _():
        y_ref[...] = jnp.zeros_like(y_ref)

        @pl.when(blk == n_used)
        def _():
            wait_gathers(slot)


def _experts(block_expert, n_used, slot_tok, u_tm, w1, w3, w2):
    n_slots = slot_tok.shape[0]
    wmap = lambda i, be, nu, st: (be[i], 0, 0)
    return pl.pallas_call(
        _expert_kernel,
        grid_spec=pltpu.PrefetchScalarGridSpec(
            num_scalar_prefetch=3,
            grid=(n_slots // MOE_TILE,),
            in_specs=[
                pl.BlockSpec(memory_space=pl.ANY),
                pl.BlockSpec((None, D_MODEL, D_FF_EXPERT), wmap),
                pl.BlockSpec((None, D_MODEL, D_FF_EXPERT), wmap),
                pl.BlockSpec((None, D_FF_EXPERT, D_MODEL), wmap),
            ],
            out_specs=pl.BlockSpec((MOE_TILE * TOKEN_ROWS, LANES), lambda i, be, nu, st: (i, 0)),
            scratch_shapes=[pltpu.VMEM((2, MOE_TILE * TOKEN_ROWS, LANES), F32),
                            pltpu.SemaphoreType.DMA((2,))],
        ),
        out_shape=jax.ShapeDtypeStruct((n_slots * TOKEN_ROWS, LANES), F32),
        compiler_params=_params("arbitrary"),
        name="moe_experts",
    )(block_expert, n_used, slot_tok, u_tm, w1, w3, w2)


def _combine_final_kernel(dest_ref, h_ref, gate_ref, y_ref, g_ref, o_ref, buf_ref, sem):
    step = pl.program_id(0)
    n_tokens = dest_ref.shape[0] // TOP_K

    def issue(s, slot):
        base = s * Q_BLOCK

        def body(g, carry):
            r0 = g * ISSUE_TOKENS
            slots = [dest_ref[(j % TOP_K) * n_tokens + base + r0 + j // TOP_K]
                     for j in range(ISSUE_TOKENS * TOP_K)]
            for j, src in enumerate(slots):
                _token_copy(y_ref, src, buf_ref.at[slot, j % TOP_K], r0 + j // TOP_K,
                            sem.at[slot]).start(priority=j % 2)
            return carry

        lax.fori_loop(0, Q_BLOCK // ISSUE_TOKENS, body, 0)

    @pl.when(step == 0)
    def _():
        issue(0, 0)

    for slot in range(2):
        @pl.when(step % 2 == slot)
        def _():
            @pl.when(step + 1 < pl.num_programs(0))
            def _():
                issue(step + 1, 1 - slot)

            for k in range(TOP_K):
                pltpu.make_async_copy(y_ref.at[pl.ds(0, Q_BLOCK * TOKEN_ROWS), :],
                                      buf_ref.at[slot, k], sem.at[slot]).wait()
            gate = gate_ref[...]
            h = (h_ref[...]
                 + gate[:, 0:1] * _load_token_major(buf_ref.at[slot, 0], Q_BLOCK)
                 + gate[:, 1:2] * _load_token_major(buf_ref.at[slot, 1], Q_BLOCK))
            o_ref[...] = _rms(h, g_ref[...])


def _combine_final(dest, h, gate, y, gain, batch, length, seq):
    n = h.shape[0]
    per_seq = length // Q_BLOCK
    lead_blocks = LEAD // Q_BLOCK
    return pl.pallas_call(
        _combine_final_kernel,
        grid_spec=pltpu.PrefetchScalarGridSpec(
            num_scalar_prefetch=1,
            grid=(n // Q_BLOCK,),
            in_specs=[
                pl.BlockSpec((Q_BLOCK, D_MODEL), lambda i, d: (i, 0)),
                pl.BlockSpec((Q_BLOCK, N_EXPERTS), lambda i, d: (i, 0)),
                pl.BlockSpec(memory_space=pl.ANY),
                pl.BlockSpec((1, D_MODEL), lambda i, d: (0, 0)),
            ],
            out_specs=pl.BlockSpec(
                (None, Q_BLOCK, D_MODEL),
                lambda i, d: (i // per_seq, jnp.maximum(i % per_seq - lead_blocks, 0), 0)),
            scratch_shapes=[pltpu.VMEM((2, TOP_K, Q_BLOCK * TOKEN_ROWS, LANES), F32),
                            pltpu.SemaphoreType.DMA((2,))],
        ),
        out_shape=jax.ShapeDtypeStruct((batch, seq, D_MODEL), F32),
        compiler_params=_params("arbitrary"),
        name="moe_combine_final",
    )(dest, h, gate, y, gain)


def _prefix_sum(x):
    k = x.shape[0]
    keep = (np.arange(k)[None, :] <= np.arange(k)[:, None]).reshape((k, k) + (1,) * (x.ndim - 1))
    return jnp.sum(jnp.where(keep, x[None], jnp.zeros_like(x[None])), axis=1)


def _slot_tables(route, counts, n_slots):
    n = route.shape[1]
    flat_e = route[0:TOP_K].reshape(n * TOP_K)
    rank = route[TOP_K:2 * TOP_K].reshape(n * TOP_K)
    counts = counts[:, 0]
    padded = (counts + MOE_TILE - 1) // MOE_TILE * MOE_TILE
    pad_end = _prefix_sum(padded)
    pad_start = pad_end - padded
    onehot = flat_e[:, None] == jnp.arange(N_EXPERTS, dtype=jnp.int32)[None, :]
    dest = (rank + jnp.sum(jnp.where(onehot, pad_start[None, :], 0), axis=-1)).astype(jnp.int32)
    n_blocks = n_slots // MOE_TILE
    block_first = jnp.arange(n_blocks, dtype=jnp.int32) * MOE_TILE
    block_expert = jnp.minimum(
        jnp.sum((pad_end[None, :] <= block_first[:, None]).astype(jnp.int32), axis=-1),
        N_EXPERTS - 1).astype(jnp.int32)
    n_used = (pad_end[-1:] // MOE_TILE).astype(jnp.int32)
    n_empty = n_slots - n * TOP_K
    gap_end = _prefix_sum(padded - counts)
    j = jnp.arange(n_empty, dtype=jnp.int32)
    owner = jnp.sum((j[:, None] >= gap_end[None, :]).astype(jnp.int32), axis=-1)
    first_empty = jnp.concatenate([pad_start + counts, pad_end[-1:]])
    gap_start = jnp.concatenate([jnp.zeros((1,), gap_end.dtype), gap_end])
    pick = owner[:, None] == jnp.arange(N_EXPERTS + 1, dtype=jnp.int32)[None, :]
    empty_slots = (j + jnp.sum(jnp.where(pick, (first_empty - gap_start)[None, :], 0), axis=-1))
    tokens = jnp.arange(n, dtype=jnp.int32)
    _, slot_tok = lax.sort_key_val(
        jnp.concatenate([dest, empty_slots.astype(jnp.int32)]),
        jnp.concatenate([tokens] * TOP_K + [jnp.zeros((n_empty,), jnp.int32)]))
    return dest, block_expert, n_used, slot_tok


def _moe_ffn(u, route, counts, w1, w3, w2):
    n = route.shape[1]
    n_slots = (n * TOP_K // MOE_TILE + N_EXPERTS) * MOE_TILE
    dest, block_expert, n_used, slot_tok = _slot_tables(route, counts, n_slots)
    y = _experts(block_expert, n_used, slot_tok, u, w1, w3, w2)
    return dest, y


def kernel(x, meta, rel_bias, norm_mix, w_in, hg_lb_logits, hg_norm_w, da_lambda, da_subln_w, w_out, norm_ffn, dense_w1, dense_w3, dense_w2, moe_router, moe_w1, moe_w3, moe_w2, final_norm):
    batch, seq, d = x.shape
    length = LEAD + seq
    h = jnp.concatenate([
        jnp.zeros((batch, LEAD - N_META, d), x.dtype),
        jnp.broadcast_to(meta[None].astype(x.dtype), (batch, N_META, d)),
        x], axis=1).reshape(batch * length, d)

    toe = _attn_bias_tables(rel_bias)
    lb_cum = _prefix_sum(jax.nn.softmax(hg_lb_logits.astype(F32), axis=0))
    lb_all = jnp.clip(lb_cum - lb_cum[0:1], 0.0, LB_MAX)
    log_lb = jnp.log(lb_all)
    log_1m_lb = jnp.log1p(-lb_all)

    assert DEPTH % 2 == 0
    dense_f32 = (dense_w1, dense_w3, dense_w2)
    stacked_rows = lambda w: w.reshape(1, w.shape[0] * w.shape[1], w.shape[2])

    unsummed = None
    for l in range(DEPTH):
        if l == 0:
            jobs = [(w, 0) for w in dense_f32] + [(stacked_rows(w_out), 0), (stacked_rows(w_in), 0)]
            hg, da, casts = _mix_in(h, norm_mix[l][None], w_in[0:1].astype(BF16), 0, jobs)
            dense_b = [w[None] for w in casts[:3]]
            w_out_b = casts[3].reshape(w_out.shape)
            w_in_b = casts[4].reshape(w_in.shape)
        elif unsummed is None:
            jobs = [(w, (l + 1) // 2) for w in dense_f32] if l + 1 < DEPTH else []
            hg, da, casts = _mix_in(h, norm_mix[l][None], w_in_b, l, jobs)
            if jobs:
                dense_b = [w[None] for w in casts]
        else:
            h, hg, da = _mix_in_combine(*unsummed, norm_mix[l][None], w_in_b, l)
            unsummed = None
        lam_init = 0.8 - 0.6 * math.exp(-0.3 * l)
        lv = da_lambda[l].astype(F32)
        lam = jnp.exp(jnp.sum(lv[0] * lv[1])) - jnp.exp(jnp.sum(lv[2] * lv[3])) + lam_init
        cst = jnp.zeros((SUBLANES, LANES), F32).at[0].set(lam).at[1].set(1.0 - lam_init)
        o_hg = _hgrn(hg, log_lb[l][None], log_1m_lb[l][None], hg_norm_w[l][None], batch, length)
        o_da = _attn(da, toe, cst, da_subln_w[l][None], batch, length)
        i = l // 2
        if l % 2 == 0:
            h, moe_b = _dense_layer(h, o_hg, o_da, w_out_b, l, norm_ffn[l][None], *dense_b, 0,
                                    (moe_w1, moe_w3, moe_w2), i)
        else:
            router = jnp.zeros((d, LANES), F32).at[:, :N_EXPERTS].set(moe_router[i].astype(F32))
            r_hi = router.astype(BF16)
            r_lo = (router - r_hi.astype(F32)).astype(BF16)
            router = jnp.concatenate([r_hi, r_hi, r_lo], axis=0)
            hn, u, route, gate, counts = _out_proj_router(h, o_hg, o_da, w_out_b, l,
                                                          norm_ffn[l][None], router)
            dest, y = _moe_ffn(u, route, counts, *moe_b)
            if l + 1 < DEPTH:
                unsummed = (dest, hn, gate, y)
            else:
                return _combine_final(dest, hn, gate, y, final_norm[None], batch, length, seq)
```

```python
import functools
import math

import jax
import jax.numpy as jnp
import numpy as np
from jax import lax
from jax.experimental import pallas as pl
from jax.experimental.pallas import tpu as pltpu

D_MODEL = 1024
DEPTH = 4
N_META = 16
LEAD = 128
HG_WIDTH = 512
HG_HEADS = 4
HG_D = 128
HG_CHUNK = 64
DA_HEADS = 4
DA_DQK = 64
DA_DV = 128
Q_BLOCK = 128
KEY_TILE = 4 * Q_BLOCK
Q_GROUP = 2
LOG2E = math.log2(math.e)
Q_SCALE = DA_DQK ** -0.5 * LOG2E
REL_BUCKETS = 32
REL_MAX_DIST = 128
N_EXPERTS = 8
TOP_K = 2
D_FF_EXPERT = 3584
EPS = 1e-6
NEG = -1e30
LB_MAX = 0.999
HG_COLS = 4 * HG_WIDTH
DA_COLS = 3 * DA_HEADS * DA_DV
W_IN_COLS = HG_COLS + DA_COLS

LANES = 128
SUBLANES = 8
VMEM_LIMIT = 56 * 1024 * 1024

ROW_TILE = 256
MIX_TILE = 512
HG_TILE = 128
MOE_TILE = 256
FF_CHUNK = 1792

F32 = jnp.float32
BF16 = jnp.bfloat16


def _params(*sem):
    return pltpu.CompilerParams(dimension_semantics=sem, vmem_limit_bytes=VMEM_LIMIT)


def _dot(a, b):
    return jnp.dot(a, b, preferred_element_type=F32)


def _dot_nt(a, b):
    return lax.dot_general(a, b, (((1,), (1,)), ((), ())), preferred_element_type=F32)


def _dot_tn(a, b):
    return lax.dot_general(a, b, (((0,), (0,)), ((), ())), preferred_element_type=F32)


def _rms(x, gain):
    return x * lax.rsqrt(jnp.mean(x * x, axis=-1, keepdims=True) + EPS) * gain


TOKEN_ROWS = D_MODEL // LANES


def _store_token_major(ref, x):
    t = x.shape[0]
    for s in range(TOKEN_ROWS):
        ref[pl.ds(s, t, stride=TOKEN_ROWS), :] = x[:, s * LANES:(s + 1) * LANES]


def _load_token_major(ref, t):
    return jnp.concatenate(
        [ref[pl.ds(s, t, stride=TOKEN_ROWS), :] for s in range(TOKEN_ROWS)], axis=1)


def _silu(x):
    return x * (0.5 * jnp.tanh(0.5 * x) + 0.5)


BF16_SUBLANES = 16


def _cast_jobs(jobs, grid_steps):
    inputs, in_specs, out_specs, out_shapes = [], [], [], []
    for w, index in jobs:
        _, rows, cols = w.shape
        steps = max(s for s in range(1, grid_steps + 1)
                    if rows % s == 0 and (rows // s) % BF16_SUBLANES == 0)
        slab_rows = rows // steps
        in_specs.append(pl.BlockSpec((None, slab_rows, cols),
                                     lambda i, index=index, steps=steps: (index, jnp.minimum(i, steps - 1), 0)))
        out_specs.append(pl.BlockSpec((slab_rows, cols),
                                      lambda i, steps=steps: (jnp.minimum(i, steps - 1), 0)))
        out_shapes.append(jax.ShapeDtypeStruct((rows, cols), BF16))
        inputs.append(w)
    return inputs, in_specs, out_specs, out_shapes


def _run_cast_jobs(src_refs, dst_refs):
    for src_ref, dst_ref in zip(src_refs, dst_refs):
        dst_ref[...] = src_ref[...].astype(BF16)


def _mix_in_kernel(x_ref, g_ref, w_ref, *rest):
    n_cast = (len(rest) - 2) // 2
    hg_ref, da_ref = rest[n_cast:n_cast + 2]
    u = _rms(x_ref[...], g_ref[...]).astype(BF16)
    hg_ref[...] = _dot(u, w_ref[:, :HG_COLS])
    n_q = DA_HEADS * 2 * DA_DQK
    da_ref[:, :n_q] = (_dot(u, w_ref[:, HG_COLS:HG_COLS + n_q]) * Q_SCALE).astype(BF16)
    da_ref[:, n_q:] = _dot(u, w_ref[:, HG_COLS + n_q:]).astype(BF16)
    _run_cast_jobs(rest[:n_cast], rest[n_cast + 2:])


def _mix_in(h, gain, w, layer, to_cast=()):
    n = h.shape[0]
    steps = n // MIX_TILE
    cast_in, cast_in_specs, cast_out_specs, cast_out_shapes = _cast_jobs(to_cast, steps)
    out = pl.pallas_call(
        _mix_in_kernel,
        grid=(steps,),
        in_specs=[
            pl.BlockSpec((MIX_TILE, D_MODEL), lambda i: (i, 0)),
            pl.BlockSpec((1, D_MODEL), lambda i: (0, 0)),
            pl.BlockSpec((None, D_MODEL, W_IN_COLS), lambda i: (layer, 0, 0),
                         pipeline_mode=pl.Buffered(1)),
        ] + cast_in_specs,
        out_specs=[
            pl.BlockSpec((MIX_TILE, HG_COLS), lambda i: (i, 0)),
            pl.BlockSpec((MIX_TILE, DA_COLS), lambda i: (i, 0)),
        ] + cast_out_specs,
        out_shape=[
            jax.ShapeDtypeStruct((n, HG_COLS), F32),
            jax.ShapeDtypeStruct((n, DA_COLS), BF16),
        ] + cast_out_shapes,
        compiler_params=_params("arbitrary"),
        name="mix_in",
    )(h, gain, w, *cast_in)
    return out[0], out[1], list(out[2:])


def _mix_in_combine_kernel(dest_ref, hn_ref, gate_ref, y_ref, g_ref, w_ref,
                           h_ref, hg_ref, da_ref, buf_ref, sem):
    step = pl.program_id(0)
    last = pl.num_programs(0) - 1
    slot = step % 2
    tile_rows = MIX_TILE * TOKEN_ROWS
    n_tokens = dest_ref.shape[0] // TOP_K

    def start_gathers(s, to_slot, rows):
        base = s * MIX_TILE
        slots = [dest_ref[k * n_tokens + base + r] for r in rows for k in range(TOP_K)]
        for j, src in enumerate(slots):
            r, k = rows[j // TOP_K], j % TOP_K
            _token_copy(y_ref, src, buf_ref.at[to_slot, k], r, sem.at[to_slot]).start(priority=1)

    def wait_gathers(of_slot):
        for k in range(TOP_K):
            pltpu.make_async_copy(y_ref.at[pl.ds(0, tile_rows), :], buf_ref.at[of_slot, k],
                                  sem.at[of_slot]).wait()

    @pl.when(step == 0)
    def _():
        def first(g, carry):
            start_gathers(0, 0, [g * ISSUE_TOKENS + r for r in range(ISSUE_TOKENS)])
            return carry

        lax.fori_loop(0, MIX_TILE // ISSUE_TOKENS, first, 0)

    wait_gathers(slot)
    gate = gate_ref[...]
    h = (hn_ref[...] + gate[:, 0:1] * _load_token_major(buf_ref.at[slot, 0], MIX_TILE)
         + gate[:, 1:2] * _load_token_major(buf_ref.at[slot, 1], MIX_TILE))
    h_ref[...] = h
    u = _rms(h, g_ref[...]).astype(BF16)

    nxt = jnp.minimum(step + 1, last)
    n_q = DA_HEADS * 2 * DA_DQK
    groups = [(c, c + HG_WIDTH) for c in range(0, HG_COLS, HG_WIDTH)]
    groups += [(HG_COLS, HG_COLS + n_q), (HG_COLS + n_q, W_IN_COLS)]
    per_group = -(-MIX_TILE // len(groups))
    for gi, (c0, c1) in enumerate(groups):
        proj = _dot(u, w_ref[:, c0:c1])
        if c1 <= HG_COLS:
            hg_ref[:, c0:c1] = proj
        elif c0 == HG_COLS:
            da_ref[:, :n_q] = (proj * Q_SCALE).astype(BF16)
        else:
            da_ref[:, n_q:] = proj.astype(BF16)
        rows = list(range(gi * per_group, min((gi + 1) * per_group, MIX_TILE)))
        for r0 in range(0, len(rows), ISSUE_TOKENS):
            start_gathers(nxt, 1 - slot, rows[r0:r0 + ISSUE_TOKENS])

    @pl.when(step == last)
    def _():
        wait_gathers(1 - slot)


def _mix_in_combine(dest, hn, gate, y, gain, w, layer):
    n = hn.shape[0]
    row = lambda i, d: (i, 0)
    return pl.pallas_call(
        _mix_in_combine_kernel,
        grid_spec=pltpu.PrefetchScalarGridSpec(
            num_scalar_prefetch=1,
            grid=(n // MIX_TILE,),
            in_specs=[
                pl.BlockSpec((MIX_TILE, D_MODEL), row),
                pl.BlockSpec((MIX_TILE, N_EXPERTS), row),
                pl.BlockSpec(memory_space=pl.ANY),
                pl.BlockSpec((1, D_MODEL), lambda i, d: (0, 0)),
                pl.BlockSpec((None, D_MODEL, W_IN_COLS), lambda i, d: (layer, 0, 0),
                             pipeline_mode=pl.Buffered(1)),
            ],
            out_specs=[
                pl.BlockSpec((MIX_TILE, D_MODEL), row),
                pl.BlockSpec((MIX_TILE, HG_COLS), row),
                pl.BlockSpec((MIX_TILE, DA_COLS), row),
            ],
            scratch_shapes=[pltpu.VMEM((2, TOP_K, MIX_TILE * TOKEN_ROWS, LANES), F32),
                            pltpu.SemaphoreType.DMA((2,))],
        ),
        out_shape=[
            jax.ShapeDtypeStruct((n, D_MODEL), F32),
            jax.ShapeDtypeStruct((n, HG_COLS), F32),
            jax.ShapeDtypeStruct((n, DA_COLS), BF16),
        ],
        compiler_params=_params("arbitrary"),
        name="mix_in_combine",
    )(dest, hn, gate, y, gain, w)


HG_LEVELS = (32, 16, 8, 4, 2, 1)
N_SUMS = len(HG_LEVELS) + 2


def _hgrn_consts():
    c = HG_CHUNK
    t = np.arange(c)[:, None]
    j = np.arange(c)[None, :]
    sums = np.zeros((N_SUMS, c, c), np.float32)
    masks = np.zeros((len(HG_LEVELS) + 1, c, c), np.float32)
    sums[0] = j <= t
    masks[0] = np.eye(c)
    for li, w in enumerate(HG_LEVELS, start=1):
        ref = (t // (2 * w)) * (2 * w) + w
        sums[li] = np.where(t >= ref, (j > ref) & (j <= t), (j > t) & (j <= ref))
        masks[li] = (t // (2 * w) == j // (2 * w)) & (t % (2 * w) >= w) & (j % (2 * w) < w)
    sums[N_SUMS - 1] = j > t
    sums = sums.reshape(N_SUMS * c, c)
    return np.concatenate([sums, sums], axis=1), masks


_HG_SUMS, _HG_MASKS = _hgrn_consts()


def _hgrn_kernel(hg_ref, loga_ref, log1m_ref, nw_ref, sums_ref, masks_ref, o_ref, state_ref):
    c_idx = pl.program_id(1)

    @pl.when(c_idx == 0)
    def _():
        state_ref[...] = jnp.zeros_like(state_ref)

    C = HG_CHUNK
    W = HG_WIDTH
    n_chunks = HG_TILE // C
    sums = sums_ref[...]
    nw = nw_ref[...]
    row_idx = c_idx * HG_TILE + lax.broadcasted_iota(jnp.int32, (HG_TILE, 1), 0)
    valid = row_idx >= (LEAD - N_META)
    step = lax.broadcasted_iota(jnp.int32, (C, 1), 0)
    head_cols = [slice(hd * HG_D, (hd + 1) * HG_D) for hd in range(HG_HEADS)]

    f = hg_ref[:, W:2 * W]
    qf = _silu(hg_ref[:, 0:W])
    ls = jnp.minimum(f, 0.0) - jnp.log(1.0 + jnp.exp(-jnp.abs(f)))
    cc = log1m_ref[...] + ls
    loga = loga_ref[...]
    lf = jnp.maximum(loga, cc) + jnp.log(1.0 + jnp.exp(-jnp.abs(loga - cc)))
    kk = jnp.exp(cc - f)
    lf = jnp.where(valid, lf, 0.0)
    kk = jnp.where(valid, kk, 0.0)
    lf2 = lf * LOG2E
    lf_hi = lf2.astype(BF16)
    lf_lo = (lf2 - lf_hi.astype(F32)).astype(BF16)
    vb = hg_ref[:, 2 * W:3 * W].astype(BF16)
    gate = _silu(hg_ref[:, 3 * W:4 * W])

    def side_by_side(x):
        return jnp.concatenate([x[ch * C:(ch + 1) * C] for ch in range(n_chunks)], axis=1)

    qf_w, kk_w = side_by_side(qf), side_by_side(kk)
    qb_w, kb_w = qf_w.astype(BF16), kk_w.astype(BF16)
    e = jnp.exp2(_dot(sums, jnp.concatenate([side_by_side(lf_hi), side_by_side(lf_lo)], axis=0)))
    e_b = e[0:C]
    decay_end = e_b[C - 1:C, :]
    q_in = (qf_w * e_b).astype(BF16)
    k_out = (kk_w * e[(N_SUMS - 1) * C:N_SUMS * C]).astype(BF16)
    z = [(jnp.where((step & w) != 0, qf_w, kk_w) * e[li * C:(li + 1) * C]).astype(BF16)
         for li, w in enumerate(HG_LEVELS, start=1)]
    unit_cols = [[slice(ch * W + hd * HG_D, ch * W + (hd + 1) * HG_D) for hd in range(HG_HEADS)]
                 for ch in range(n_chunks)]
    scores = []
    for ch in range(n_chunks):
        scores.append([])
        for cols in unit_cols[ch]:
            s = masks_ref[0] * _dot_nt(qb_w[:, cols], kb_w[:, cols])
            for li in range(1, len(HG_LEVELS) + 1):
                zl = z[li - 1][:, cols]
                s += masks_ref[li] * _dot_nt(zl, zl)
            scores[ch].append(s.astype(BF16))

    for ch in range(n_chunks):
        rows = slice(ch * C, (ch + 1) * C)
        for hd, cols in enumerate(head_cols):
            wide = unit_cols[ch][hd]
            st = state_ref[hd]
            v_h = vb[rows, cols]
            o = _dot_nt(q_in[:, wide], st.astype(BF16)) + _dot(scores[ch][hd], v_h)
            state_ref[hd] = st * decay_end[:, wide] + _dot_tn(v_h, k_out[:, wide])
            o = _rms(o, nw) * gate[rows, cols]
            o_ref[rows, cols] = o.astype(o_ref.dtype)


def _hgrn(hg, loga, log1m, norm_w, batch, length):
    hg3 = hg.reshape(batch, length, HG_COLS)
    out = pl.pallas_call(
        _hgrn_kernel,
        grid=(batch, length // HG_TILE),
        in_specs=[
            pl.BlockSpec((None, HG_TILE, HG_COLS), lambda b, c: (b, c, 0)),
            pl.BlockSpec((1, HG_WIDTH), lambda b, c: (0, 0)),
            pl.BlockSpec((1, HG_WIDTH), lambda b, c: (0, 0)),
            pl.BlockSpec((1, HG_D), lambda b, c: (0, 0)),
            pl.BlockSpec(_HG_SUMS.shape, lambda b, c: (0, 0)),
            pl.BlockSpec(_HG_MASKS.shape, lambda b, c: (0, 0, 0)),
        ],
        out_specs=pl.BlockSpec((None, HG_TILE, HG_WIDTH), lambda b, c: (b, c, 0)),
        out_shape=jax.ShapeDtypeStruct((batch, length, HG_WIDTH), BF16),
        scratch_shapes=[pltpu.VMEM((HG_HEADS, HG_D, HG_D), F32)],
        compiler_params=_params("parallel", "arbitrary"),
        name="hgrn2",
    )(hg3, loga, log1m, norm_w, jnp.asarray(_HG_SUMS, BF16), jnp.asarray(_HG_MASKS, F32))
    return out.reshape(batch * length, HG_WIDTH)


def _attn_kernel(q_ref, k_ref, v_ref, toe_ref, cst_ref, w_ref, o_ref, s_ref, *, n_blocks):
    lam = cst_ref[0:1, 0:1]
    post = cst_ref[1:2, :]
    lane = lax.broadcasted_iota(jnp.int32, (Q_BLOCK, Q_BLOCK), 1)
    first_half = lane < DA_DQK
    key_ok0 = lane >= (LEAD - N_META)
    inert_bias = jnp.where(key_ok0, 0.0, NEG)

    def near_bias(kind, kb):
        bias = toe_ref[kind]
        if kb == 0:
            bias = jnp.where(key_ok0, bias, NEG)
        return bias

    def slabs(x):
        return [x[:, c:c + Q_BLOCK] for c in range(0, x.shape[1], Q_BLOCK)]

    map_rows = 2 * Q_BLOCK
    groups = [tuple(range(i, min(i + Q_GROUP, n_blocks))) for i in range(0, n_blocks, Q_GROUP)]

    def both_maps(bias):
        return jnp.concatenate([bias, bias], axis=0)

    def stacked_q(blocks):
        parts = []
        for i in blocks:
            qi = q_ref[i * Q_BLOCK:(i + 1) * Q_BLOCK, :]
            zero = jnp.zeros_like(qi)
            parts += [jnp.where(first_half, qi, zero), jnp.where(first_half, zero, qi)]
        return jnp.concatenate(parts, axis=0)

    def group_tiles(blocks):
        first, last = blocks[0], blocks[-1]
        tiles = []
        far_end = max(first - 1, 0)
        if far_end >= 1:
            tiles.append((0, Q_BLOCK, jnp.concatenate([inert_bias] * (2 * len(blocks)), axis=0), 0))
        kb = 1
        while kb < far_end:
            width = KEY_TILE
            while kb + width // Q_BLOCK > far_end:
                width //= 2
            tiles.append((kb * Q_BLOCK, width, None, 0))
            kb += width // Q_BLOCK
        for kb in range(far_end, last + 1):
            biases, first_row = [], None
            for r, qb in enumerate(blocks):
                if kb > qb:
                    continue
                if first_row is None:
                    first_row = r * map_rows
                if kb == qb:
                    bias = near_bias(0, kb)
                elif kb == qb - 1:
                    bias = near_bias(1, kb)
                else:
                    bias = inert_bias if kb == 0 else jnp.zeros_like(inert_bias)
                biases.append(both_maps(bias))
            tiles.append((kb * Q_BLOCK, Q_BLOCK, jnp.concatenate(biases, axis=0), first_row))
        return tiles

    def merge(acc, x, first_row, op):
        if acc is None:
            return x
        if first_row == 0:
            return op(acc, x)
        return jnp.concatenate([acc[:first_row], op(acc[first_row:], x)], axis=0)

    def sweep_scores(g):
        blocks = groups[g]
        rows = len(blocks) * map_rows
        q2 = stacked_q(blocks)
        m_acc = None
        for start, width, bias, first_row in group_tiles(blocks):
            s = _dot_nt(q2[first_row:], k_ref[start:start + width, :])
            if bias is not None:
                s = s + bias
            s_ref[g % 2, first_row:rows, start:start + width] = s
            for slab in slabs(s):
                m_acc = merge(m_acc, slab, first_row, jnp.maximum)
        return m_acc.max(axis=-1, keepdims=True)

    row_max = sweep_scores(0)
    for g, blocks in enumerate(groups):
        rows = len(blocks) * map_rows
        m = row_max
        if g + 1 < len(groups):
            row_max = sweep_scores(g + 1)
        l_acc = o_acc = None
        for start, width, _, first_row in group_tiles(blocks):
            p = jnp.exp2(s_ref[g % 2, first_row:rows, start:start + width] - m[first_row:])
            for slab in slabs(p):
                l_acc = merge(l_acc, slab, first_row, jnp.add)
            o_acc = merge(o_acc, _dot(p.astype(BF16), v_ref[start:start + width, :]), first_row, jnp.add)
        o2 = o_acc * (1.0 / l_acc.sum(axis=-1, keepdims=True))
        for r, i in enumerate(blocks):
            o = o2[r * map_rows:r * map_rows + Q_BLOCK] - lam * o2[r * map_rows + Q_BLOCK:(r + 1) * map_rows]
            o = _rms(o, w_ref[...]) * post
            o_ref[i * Q_BLOCK:(i + 1) * Q_BLOCK, :] = o.astype(o_ref.dtype)


def _attn(da, toe, cst, subln_w, batch, length):
    da3 = da.reshape(batch, length, DA_COLS)
    hw = DA_HEADS
    out = pl.pallas_call(
        functools.partial(_attn_kernel, n_blocks=length // Q_BLOCK),
        grid=(batch, DA_HEADS),
        in_specs=[
            pl.BlockSpec((None, length, DA_DV), lambda b, h: (b, 0, h)),
            pl.BlockSpec((None, length, DA_DV), lambda b, h: (b, 0, hw + h)),
            pl.BlockSpec((None, length, DA_DV), lambda b, h: (b, 0, 2 * hw + h)),
            pl.BlockSpec((None, 2, Q_BLOCK, Q_BLOCK), lambda b, h: (h, 0, 0, 0)),
            pl.BlockSpec((SUBLANES, LANES), lambda b, h: (0, 0)),
            pl.BlockSpec((1, DA_DV), lambda b, h: (0, 0)),
        ],
        out_specs=pl.BlockSpec((None, length, DA_DV), lambda b, h: (b, 0, h)),
        out_shape=jax.ShapeDtypeStruct((batch, length, DA_HEADS * DA_DV), BF16),
        scratch_shapes=[pltpu.VMEM((2, Q_GROUP * 2 * Q_BLOCK, length), F32)],
        compiler_params=_params("parallel", "parallel"),
        name="diff_attn",
    )(da3, da3, da3, toe, cst, subln_w)
    return out.reshape(batch * length, DA_HEADS * DA_DV)


def _t5_bucket(dist):
    n = jnp.maximum(dist, 0)
    max_exact = REL_BUCKETS // 2
    nf = jnp.maximum(n, max_exact).astype(F32)
    large = max_exact + (jnp.log(nf / max_exact) / math.log(REL_MAX_DIST / max_exact)
                         * (REL_BUCKETS - max_exact)).astype(jnp.int32)
    large = jnp.minimum(large, REL_BUCKETS - 1)
    return jnp.where(n < max_exact, n, large)


def _attn_bias_tables(rel_bias):
    tab = rel_bias.astype(F32)
    qi = jnp.arange(Q_BLOCK, dtype=jnp.int32)[:, None]
    ki = jnp.arange(Q_BLOCK, dtype=jnp.int32)[None, :]

    def lookup(bucket):
        onehot = bucket[None, :, :, None] == jnp.arange(REL_BUCKETS, dtype=jnp.int32)
        return jnp.sum(jnp.where(onehot, tab.T[:, None, None, :], 0.0), axis=-1)

    far = tab[REL_BUCKETS - 1][:, None, None]
    diag = jnp.where((ki <= qi)[None], (lookup(_t5_bucket(qi - ki)) - far) * LOG2E, NEG)
    prev = (lookup(_t5_bucket(qi - ki + Q_BLOCK)) - far) * LOG2E
    return jnp.stack([diag, prev], axis=1)


def _out_proj_router_kernel(h_ref, ohg_ref, oda_ref, wo_ref, g_ref, router_ref, before_ref,
                            hn_ref, u_ref, route_ref, gate_ref, count_ref, seen_ref):
    @pl.when(pl.program_id(0) == 0)
    def _():
        seen_ref[...] = jnp.zeros_like(seen_ref)

    hn = (h_ref[...] + _dot(ohg_ref[...], wo_ref[:HG_WIDTH, :])
          + _dot(oda_ref[...], wo_ref[HG_WIDTH:, :]))
    hn_ref[...] = hn
    u = _rms(hn, g_ref[...])
    _store_token_major(u_ref, u)
    u_hi = u.astype(BF16)
    u_lo = (u - u_hi.astype(F32)).astype(BF16)
    logits = _dot(jnp.concatenate([u_hi, u_lo, u_hi], axis=1), router_ref[...])
    logits = logits.T[:N_EXPERTS]
    expert = lax.broadcasted_iota(jnp.int32, logits.shape, 0)

    def top(x):
        best = x.max(axis=0, keepdims=True)
        return best, jnp.where(x == best, expert, N_EXPERTS).min(axis=0, keepdims=True)

    l1, e1 = top(logits)
    l2, e2 = top(jnp.where(expert == e1, -jnp.inf, logits))
    w2 = jnp.exp(l2 - l1)
    g1 = 1.0 / (1.0 + w2)
    g2 = w2 / (1.0 + w2)

    pick1 = (expert == e1).astype(F32)
    pick2 = (expert == e2).astype(F32)
    picked = pick1 + pick2
    ahead = seen_ref[:, 0:1] + _dot(picked.astype(BF16), before_ref[...])
    rank1 = jnp.sum(pick1 * ahead, axis=0, keepdims=True).astype(jnp.int32)
    rank2 = jnp.sum(pick2 * ahead, axis=0, keepdims=True).astype(jnp.int32)
    seen = seen_ref[...] + jnp.sum(picked, axis=1, keepdims=True)
    seen_ref[...] = seen

    route_ref[...] = jnp.where(expert == 0, e1, jnp.where(expert == 1, e2, jnp.where(
        expert == 2, rank1, jnp.where(expert == 3, rank2, 0))))
    gate_ref[...] = jnp.where(expert == 0, g1, jnp.where(expert == 1, g2, 0.0)).T
    count_ref[...] = seen.astype(jnp.int32)


def _out_proj_router(h, o_hg, o_da, wo, layer, gain, router):
    n = h.shape[0]
    row = lambda i: (i, 0)
    full = lambda i: (0, 0)
    assert N_EXPERTS >= 2 * TOP_K
    lanes_of = lambda i: (0, i)
    before = jnp.asarray(np.triu(np.ones((ROW_TILE, ROW_TILE), np.float32), 1), BF16)
    return pl.pallas_call(
        _out_proj_router_kernel,
        grid=(n // ROW_TILE,),
        in_specs=[
            pl.BlockSpec((ROW_TILE, D_MODEL), row),
            pl.BlockSpec((ROW_TILE, HG_WIDTH), row),
            pl.BlockSpec((ROW_TILE, DA_HEADS * DA_DV), row),
            pl.BlockSpec((None,) + wo.shape[1:], lambda i: (layer, 0, 0)),
            pl.BlockSpec((1, D_MODEL), full),
            pl.BlockSpec(router.shape, full),
            pl.BlockSpec(before.shape, full),
        ],
        out_specs=[
            pl.BlockSpec((ROW_TILE, D_MODEL), row),
            pl.BlockSpec((ROW_TILE * TOKEN_ROWS, LANES), row),
            pl.BlockSpec((N_EXPERTS, ROW_TILE), lanes_of),
            pl.BlockSpec((ROW_TILE, N_EXPERTS), row),
            pl.BlockSpec((N_EXPERTS, LANES), full),
        ],
        out_shape=[
            jax.ShapeDtypeStruct((n, D_MODEL), F32),
            jax.ShapeDtypeStruct((n * TOKEN_ROWS, LANES), F32),
            jax.ShapeDtypeStruct((N_EXPERTS, n), jnp.int32),
            jax.ShapeDtypeStruct((n, N_EXPERTS), F32),
            jax.ShapeDtypeStruct((N_EXPERTS, LANES), jnp.int32),
        ],
        scratch_shapes=[pltpu.VMEM((N_EXPERTS, LANES), F32)],
        compiler_params=_params("arbitrary"),
        name="out_proj_router",
    )(h, o_hg, o_da, wo, gain, router, before)


def _dense_layer_kernel(h_ref, ohg_ref, oda_ref, wo_ref, g_ref, w1_ref, w3_ref, w2_ref, *rest):
    n_cast = (len(rest) - 1) // 2
    o_ref = rest[n_cast]
    hn = (h_ref[...] + _dot(ohg_ref[...], wo_ref[:HG_WIDTH, :])
          + _dot(oda_ref[...], wo_ref[HG_WIDTH:, :]))
    u = _rms(hn, g_ref[...]).astype(BF16)
    a = _dot(u, w1_ref[...])
    act = (_silu(a) * _dot(u, w3_ref[...])).astype(BF16)
    o_ref[...] = hn + _dot(act, w2_ref[...])
    _run_cast_jobs(rest[:n_cast], rest[n_cast + 1:])


def _dense_layer(h, o_hg, o_da, wo, layer, gain, w1, w3, w2, idx, to_cast, cast_idx):
    n = h.shape[0]
    steps = n // ROW_TILE
    row = lambda i: (i, 0)

    def resident(w, index):
        return pl.BlockSpec((None,) + w.shape[1:], lambda i: (index, 0, 0),
                            pipeline_mode=pl.Buffered(1))

    jobs = [(w.reshape(w.shape[0], w.shape[1] * w.shape[2], w.shape[3]), cast_idx) for w in to_cast]
    cast_in, cast_in_specs, cast_out_specs, cast_out_shape = _cast_jobs(jobs, steps)

    out = pl.pallas_call(
        _dense_layer_kernel,
        grid=(steps,),
        in_specs=[
            pl.BlockSpec((ROW_TILE, D_MODEL), row),
            pl.BlockSpec((ROW_TILE, HG_WIDTH), row),
            pl.BlockSpec((ROW_TILE, DA_HEADS * DA_DV), row),
            resident(wo, layer),
            pl.BlockSpec((1, D_MODEL), lambda i: (0, 0)),
            resident(w1, idx),
            resident(w3, idx),
            resident(w2, idx),
        ] + cast_in_specs,
        out_specs=[pl.BlockSpec((ROW_TILE, D_MODEL), row)] + cast_out_specs,
        out_shape=[jax.ShapeDtypeStruct((n, D_MODEL), F32)] + cast_out_shape,
        compiler_params=_params("arbitrary"),
        name="dense_layer",
    )(h, o_hg, o_da, wo, gain, w1, w3, w2, *cast_in)
    return out[0], [b.reshape(w.shape[1:]) for b, w in zip(out[1:], to_cast)]


DISPATCH_TILE = 1024
ISSUE_TOKENS = 8


def _token_copy(src_ref, src_tok, dst_ref, dst_tok, sem):
    src = pl.multiple_of(src_tok * TOKEN_ROWS, TOKEN_ROWS)
    dst = pl.multiple_of(dst_tok * TOKEN_ROWS, TOKEN_ROWS)
    return pltpu.make_async_copy(src_ref.at[pl.ds(src, TOKEN_ROWS), :],
                                 dst_ref.at[pl.ds(dst, TOKEN_ROWS), :], sem)


def _dispatch_kernel(dest_ref, empty_ref, u_ref, xs_ref, zero_ref, sem):
    step = pl.program_id(0)
    base = step * DISPATCH_TILE
    tile_rows = DISPATCH_TILE * TOKEN_ROWS

    def retire_tile():
        pltpu.make_async_copy(u_ref, xs_ref.at[pl.ds(0, tile_rows), :], sem).wait()

    @pl.when(step == 0)
    def _():
        zero_ref[...] = jnp.zeros_like(zero_ref)

        def clear(g, carry):
            slots = [empty_ref[g * ISSUE_TOKENS * TOP_K + j] for j in range(ISSUE_TOKENS * TOP_K)]
            for j, slot in enumerate(slots):
                _token_copy(zero_ref, 0, xs_ref, slot, sem).start(priority=j % 2)
            return carry

        n_empty = empty_ref.shape[0]
        lax.fori_loop(0, n_empty // (ISSUE_TOKENS * TOP_K), clear, 0)
        for _ in range(n_empty // DISPATCH_TILE):
            retire_tile()

    n_tokens = dest_ref.shape[0] // TOP_K

    def start(g, carry):
        r0 = g * ISSUE_TOKENS
        slots = [dest_ref[(j % TOP_K) * n_tokens + base + r0 + j // TOP_K]
                 for j in range(ISSUE_TOKENS * TOP_K)]
        for j, slot in enumerate(slots):
            _token_copy(u_ref, r0 + j // TOP_K, xs_ref, slot, sem).start(priority=j % 2)
        return carry

    lax.fori_loop(0, DISPATCH_TILE // ISSUE_TOKENS, start, 0)
    for _ in range(TOP_K):
        retire_tile()


def _dispatch(dest, empty_slots, u_tm, n_slots):
    n = dest.shape[0] // TOP_K
    assert empty_slots.shape[0] % DISPATCH_TILE == 0
    return pl.pallas_call(
        _dispatch_kernel,
        grid_spec=pltpu.PrefetchScalarGridSpec(
            num_scalar_prefetch=2,
            grid=(n // DISPATCH_TILE,),
            in_specs=[pl.BlockSpec((DISPATCH_TILE * TOKEN_ROWS, LANES), lambda i, d, e: (i, 0))],
            out_specs=pl.BlockSpec(memory_space=pl.ANY),
            scratch_shapes=[pltpu.VMEM((TOKEN_ROWS, LANES), u_tm.dtype),
                            pltpu.SemaphoreType.DMA(())],
        ),
        out_shape=jax.ShapeDtypeStruct((n_slots * TOKEN_ROWS, LANES), u_tm.dtype),
        compiler_params=_params("arbitrary"),
        name="moe_dispatch",
    )(dest, empty_slots, u_tm)


def _expert_kernel(be_ref, nused_ref, xs_ref, w1_ref, w3_ref, w2_ref, y_ref):
    del be_ref
    blk = pl.program_id(0)

    @pl.when(blk < nused_ref[0])
    def _():
        x = _load_token_major(xs_ref, MOE_TILE).astype(BF16)
        acc = jnp.zeros((MOE_TILE, D_MODEL), F32)
        for c0 in range(0, D_FF_EXPERT, FF_CHUNK):
            a = _dot(x, w1_ref[:, c0:c0 + FF_CHUNK])
            act = (_silu(a) * _dot(x, w3_ref[:, c0:c0 + FF_CHUNK])).astype(BF16)
            acc = acc + _dot(act, w2_ref[c0:c0 + FF_CHUNK, :])
        _store_token_major(y_ref, acc)

    @pl.when(blk >= nused_ref[0])
    def _():
        y_ref[...] = jnp.zeros_like(y_ref)


def _experts(block_expert, n_used, xs, w1, w3, w2):
    n_blocks = xs.shape[0] // (MOE_TILE * TOKEN_ROWS)
    wmap = lambda i, be, nu: (be[i], 0, 0)
    slots = pl.BlockSpec((MOE_TILE * TOKEN_ROWS, LANES), lambda i, be, nu: (i, 0))
    return pl.pallas_call(
        _expert_kernel,
        grid_spec=pltpu.PrefetchScalarGridSpec(
            num_scalar_prefetch=2,
            grid=(n_blocks,),
            in_specs=[
                slots,
                pl.BlockSpec((None, D_MODEL, D_FF_EXPERT), wmap),
                pl.BlockSpec((None, D_MODEL, D_FF_EXPERT), wmap),
                pl.BlockSpec((None, D_FF_EXPERT, D_MODEL), wmap),
            ],
            out_specs=slots,
        ),
        out_shape=jax.ShapeDtypeStruct(xs.shape, F32),
        compiler_params=_params("arbitrary"),
        name="moe_experts",
    )(block_expert, n_used, xs, w1, w3, w2)


def _combine_final_kernel(dest_ref, h_ref, gate_ref, y_ref, g_ref, o_ref, buf_ref, sem):
    step = pl.program_id(0)
    n_tokens = dest_ref.shape[0] // TOP_K

    def issue(s, slot):
        base = s * Q_BLOCK

        def body(g, carry):
            r0 = g * ISSUE_TOKENS
            slots = [dest_ref[(j % TOP_K) * n_tokens + base + r0 + j // TOP_K]
                     for j in range(ISSUE_TOKENS * TOP_K)]
            for j, src in enumerate(slots):
                _token_copy(y_ref, src, buf_ref.at[slot, j % TOP_K], r0 + j // TOP_K,
                            sem.at[slot]).start(priority=j % 2)
            return carry

        lax.fori_loop(0, Q_BLOCK // ISSUE_TOKENS, body, 0)

    @pl.when(step == 0)
    def _():
        issue(0, 0)

    for slot in range(2):
        @pl.when(step % 2 == slot)
        def _():
            @pl.when(step + 1 < pl.num_programs(0))
            def _():
                issue(step + 1, 1 - slot)

            for k in range(TOP_K):
                pltpu.make_async_copy(y_ref.at[pl.ds(0, Q_BLOCK * TOKEN_ROWS), :],
                                      buf_ref.at[slot, k], sem.at[slot]).wait()
            gate = gate_ref[...]
            h = (h_ref[...]
                 + gate[:, 0:1] * _load_token_major(buf_ref.at[slot, 0], Q_BLOCK)
                 + gate[:, 1:2] * _load_token_major(buf_ref.at[slot, 1], Q_BLOCK))
            o_ref[...] = _rms(h, g_ref[...])


def _combine_final(dest, h, gate, y, gain, batch, length, seq):
    n = h.shape[0]
    per_seq = length // Q_BLOCK
    lead_blocks = LEAD // Q_BLOCK
    return pl.pallas_call(
        _combine_final_kernel,
        grid_spec=pltpu.PrefetchScalarGridSpec(
            num_scalar_prefetch=1,
            grid=(n // Q_BLOCK,),
            in_specs=[
                pl.BlockSpec((Q_BLOCK, D_MODEL), lambda i, d: (i, 0)),
                pl.BlockSpec((Q_BLOCK, N_EXPERTS), lambda i, d: (i, 0)),
                pl.BlockSpec(memory_space=pl.ANY),
                pl.BlockSpec((1, D_MODEL), lambda i, d: (0, 0)),
            ],
            out_specs=pl.BlockSpec(
                (None, Q_BLOCK, D_MODEL),
                lambda i, d: (i // per_seq, jnp.maximum(i % per_seq - lead_blocks, 0), 0)),
            scratch_shapes=[pltpu.VMEM((2, TOP_K, Q_BLOCK * TOKEN_ROWS, LANES), F32),
                            pltpu.SemaphoreType.DMA((2,))],
        ),
        out_shape=jax.ShapeDtypeStruct((batch, seq, D_MODEL), F32),
        compiler_params=_params("arbitrary"),
        name="moe_combine_final",
    )(dest, h, gate, y, gain)


def _prefix_sum(x):
    k = x.shape[0]
    keep = (np.arange(k)[None, :] <= np.arange(k)[:, None]).reshape((k, k) + (1,) * (x.ndim - 1))
    return jnp.sum(jnp.where(keep, x[None], jnp.zeros_like(x[None])), axis=1)


def _slot_tables(route, counts, n_slots):
    n = route.shape[1]
    flat_e = route[0:TOP_K].reshape(n * TOP_K)
    rank = route[TOP_K:2 * TOP_K].reshape(n * TOP_K)
    counts = counts[:, 0]
    padded = (counts + MOE_TILE - 1) // MOE_TILE * MOE_TILE
    pad_end = _prefix_sum(padded)
    pad_start = pad_end - padded
    onehot = flat_e[:, None] == jnp.arange(N_EXPERTS, dtype=jnp.int32)[None, :]
    dest = (rank + jnp.sum(jnp.where(onehot, pad_start[None, :], 0), axis=-1)).astype(jnp.int32)
    n_blocks = n_slots // MOE_TILE
    block_first = jnp.arange(n_blocks, dtype=jnp.int32) * MOE_TILE
    block_expert = jnp.minimum(
        jnp.sum((pad_end[None, :] <= block_first[:, None]).astype(jnp.int32), axis=-1),
        N_EXPERTS - 1).astype(jnp.int32)
    n_used = (pad_end[-1:] // MOE_TILE).astype(jnp.int32)
    n_empty = n_slots - n * TOP_K
    gap_end = _prefix_sum(padded - counts)
    j = jnp.arange(n_empty, dtype=jnp.int32)
    owner = jnp.sum((j[:, None] >= gap_end[None, :]).astype(jnp.int32), axis=-1)
    first_empty = jnp.concatenate([pad_start + counts, pad_end[-1:]])
    gap_start = jnp.concatenate([jnp.zeros((1,), gap_end.dtype), gap_end])
    pick = owner[:, None] == jnp.arange(N_EXPERTS + 1, dtype=jnp.int32)[None, :]
    empty_slots = (j + jnp.sum(jnp.where(pick, (first_empty - gap_start)[None, :], 0), axis=-1))
    return dest, block_expert, n_used, empty_slots.astype(jnp.int32)


def _moe_ffn(u, route, counts, w1, w3, w2):
    n = route.shape[1]
    n_slots = (n * TOP_K // MOE_TILE + N_EXPERTS) * MOE_TILE
    dest, block_expert, n_used, empty_slots = _slot_tables(route, counts, n_slots)
    xs = _dispatch(dest, empty_slots, u, n_slots)
    y = _experts(block_expert, n_used, xs, w1, w3, w2)
    return dest, y


def kernel(x, meta, rel_bias, norm_mix, w_in, hg_lb_logits, hg_norm_w, da_lambda, da_subln_w, w_out, norm_ffn, dense_w1, dense_w3, dense_w2, moe_router, moe_w1, moe_w3, moe_w2, final_norm):
    batch, seq, d = x.shape
    length = LEAD + seq
    h = jnp.concatenate([
        jnp.zeros((batch, LEAD - N_META, d), x.dtype),
        jnp.broadcast_to(meta[None].astype(x.dtype), (batch, N_META, d)),
        x], axis=1).reshape(batch * length, d)

    toe = _attn_bias_tables(rel_bias)
    lb_cum = _prefix_sum(jax.nn.softmax(hg_lb_logits.astype(F32), axis=0))
    lb_all = jnp.clip(lb_cum - lb_cum[0:1], 0.0, LB_MAX)
    log_lb = jnp.log(lb_all)
    log_1m_lb = jnp.log1p(-lb_all)

    assert DEPTH % 2 == 0
    dense_f32 = (dense_w1, dense_w3, dense_w2)
    stacked_rows = lambda w: w.reshape(1, w.shape[0] * w.shape[1], w.shape[2])

    unsummed = None
    for l in range(DEPTH):
        if l == 0:
            jobs = [(w, 0) for w in dense_f32] + [(stacked_rows(w_out), 0), (stacked_rows(w_in), 0)]
            hg, da, casts = _mix_in(h, norm_mix[l][None], w_in[0:1].astype(BF16), 0, jobs)
            dense_b = [w[None] for w in casts[:3]]
            w_out_b = casts[3].reshape(w_out.shape)
            w_in_b = casts[4].reshape(w_in.shape)
        elif unsummed is None:
            jobs = [(w, (l + 1) // 2) for w in dense_f32] if l + 1 < DEPTH else []
            hg, da, casts = _mix_in(h, norm_mix[l][None], w_in_b, l, jobs)
            if jobs:
                dense_b = [w[None] for w in casts]
        else:
            h, hg, da = _mix_in_combine(*unsummed, norm_mix[l][None], w_in_b, l)
            unsummed = None
        lam_init = 0.8 - 0.6 * math.exp(-0.3 * l)
        lv = da_lambda[l].astype(F32)
        lam = jnp.exp(jnp.sum(lv[0] * lv[1])) - jnp.exp(jnp.sum(lv[2] * lv[3])) + lam_init
        cst = jnp.zeros((SUBLANES, LANES), F32).at[0].set(lam).at[1].set(1.0 - lam_init)
        o_hg = _hgrn(hg, log_lb[l][None], log_1m_lb[l][None], hg_norm_w[l][None], batch, length)
        o_da = _attn(da, toe, cst, da_subln_w[l][None], batch, length)
        i = l // 2
        if l % 2 == 0:
            h, moe_b = _dense_layer(h, o_hg, o_da, w_out_b, l, norm_ffn[l][None], *dense_b, 0,
                                    (moe_w1, moe_w3, moe_w2), i)
        else:
            router = jnp.zeros((d, LANES), F32).at[:, :N_EXPERTS].set(moe_router[i].astype(F32))
            r_hi = router.astype(BF16)
            r_lo = (router - r_hi.astype(F32)).astype(BF16)
            router = jnp.concatenate([r_hi, r_hi, r_lo], axis=0)
            hn, u, route, gate, counts = _out_proj_router(h, o_hg, o_da, w_out_b, l,
                                                          norm_ffn[l][None], router)
            dest, y = _moe_ffn(u, route, counts, *moe_b)
            if l + 1 < DEPTH:
                unsummed = (dest, hn, gate, y)
            else:
                return _combine_final(dest, hn, gate, y, final_norm[None], batch, length, seq)
```

```python
import functools
import math

import jax
import jax.numpy as jnp
import numpy as np
from jax import lax
from jax.experimental import pallas as pl
from jax.experimental.pallas import tpu as pltpu

D_MODEL = 1024
DEPTH = 4
N_META = 16
LEAD = 128
HG_WIDTH = 512
HG_HEADS = 4
HG_D = 128
HG_CHUNK = 64
DA_HEADS = 4
DA_DQK = 64
DA_DV = 128
Q_BLOCK = 128
KEY_TILE = 4 * Q_BLOCK
Q_GROUP = 2
LOG2E = math.log2(math.e)
Q_SCALE = DA_DQK ** -0.5 * LOG2E
REL_BUCKETS = 32
REL_MAX_DIST = 128
N_EXPERTS = 8
TOP_K = 2
D_FF_EXPERT = 3584
EPS = 1e-6
NEG = -1e30
LB_MAX = 0.999
HG_COLS = 4 * HG_WIDTH
DA_COLS = 3 * DA_HEADS * DA_DV
W_IN_COLS = HG_COLS + DA_COLS

LANES = 128
SUBLANES = 8
VMEM_LIMIT = 56 * 1024 * 1024

ROW_TILE = 256
MIX_TILE = 512
HG_TILE = 128
MOE_TILE = 256
FF_CHUNK = 1792

F32 = jnp.float32
BF16 = jnp.bfloat16


def _params(*sem):
    return pltpu.CompilerParams(dimension_semantics=sem, vmem_limit_bytes=VMEM_LIMIT)


def _dot(a, b):
    return jnp.dot(a, b, preferred_element_type=F32)


def _dot_nt(a, b):
    return lax.dot_general(a, b, (((1,), (1,)), ((), ())), preferred_element_type=F32)


def _dot_tn(a, b):
    return lax.dot_general(a, b, (((0,), (0,)), ((), ())), preferred_element_type=F32)


def _rms(x, gain):
    return x * lax.rsqrt(jnp.mean(x * x, axis=-1, keepdims=True) + EPS) * gain


TOKEN_ROWS = D_MODEL // LANES


def _store_token_major(ref, x):
    t = x.shape[0]
    for s in range(TOKEN_ROWS):
        ref[pl.ds(s, t, stride=TOKEN_ROWS), :] = x[:, s * LANES:(s + 1) * LANES]


def _load_token_major(ref, t):
    return jnp.concatenate(
        [ref[pl.ds(s, t, stride=TOKEN_ROWS), :] for s in range(TOKEN_ROWS)], axis=1)


def _silu(x):
    return x * (0.5 * jnp.tanh(0.5 * x) + 0.5)


BF16_SUBLANES = 16


def _cast_jobs(jobs, grid_steps):
    inputs, in_specs, out_specs, out_shapes = [], [], [], []
    for w, index in jobs:
        _, rows, cols = w.shape
        steps = max(s for s in range(1, grid_steps + 1)
                    if rows % s == 0 and (rows // s) % BF16_SUBLANES == 0)
        slab_rows = rows // steps
        in_specs.append(pl.BlockSpec((None, slab_rows, cols),
                                     lambda i, index=index, steps=steps: (index, jnp.minimum(i, steps - 1), 0)))
        out_specs.append(pl.BlockSpec((slab_rows, cols),
                                      lambda i, steps=steps: (jnp.minimum(i, steps - 1), 0)))
        out_shapes.append(jax.ShapeDtypeStruct((rows, cols), BF16))
        inputs.append(w)
    return inputs, in_specs, out_specs, out_shapes


def _run_cast_jobs(src_refs, dst_refs):
    for src_ref, dst_ref in zip(src_refs, dst_refs):
        dst_ref[...] = src_ref[...].astype(BF16)


def _mix_in_kernel(x_ref, g_ref, w_ref, *rest):
    n_cast = (len(rest) - 2) // 2
    hg_ref, da_ref = rest[n_cast:n_cast + 2]
    u = _rms(x_ref[...], g_ref[...]).astype(BF16)
    hg_ref[...] = _dot(u, w_ref[:, :HG_COLS])
    n_q = DA_HEADS * 2 * DA_DQK
    da_ref[:, :n_q] = (_dot(u, w_ref[:, HG_COLS:HG_COLS + n_q]) * Q_SCALE).astype(BF16)
    da_ref[:, n_q:] = _dot(u, w_ref[:, HG_COLS + n_q:]).astype(BF16)
    _run_cast_jobs(rest[:n_cast], rest[n_cast + 2:])


def _mix_in(h, gain, w, layer, to_cast=()):
    n = h.shape[0]
    steps = n // MIX_TILE
    cast_in, cast_in_specs, cast_out_specs, cast_out_shapes = _cast_jobs(to_cast, steps)
    out = pl.pallas_call(
        _mix_in_kernel,
        grid=(steps,),
        in_specs=[
            pl.BlockSpec((MIX_TILE, D_MODEL), lambda i: (i, 0)),
            pl.BlockSpec((1, D_MODEL), lambda i: (0, 0)),
            pl.BlockSpec((None, D_MODEL, W_IN_COLS), lambda i: (layer, 0, 0),
                         pipeline_mode=pl.Buffered(1)),
        ] + cast_in_specs,
        out_specs=[
            pl.BlockSpec((MIX_TILE, HG_COLS), lambda i: (i, 0)),
            pl.BlockSpec((MIX_TILE, DA_COLS), lambda i: (i, 0)),
        ] + cast_out_specs,
        out_shape=[
            jax.ShapeDtypeStruct((n, HG_COLS), F32),
            jax.ShapeDtypeStruct((n, DA_COLS), BF16),
        ] + cast_out_shapes,
        compiler_params=_params("arbitrary"),
        name="mix_in",
    )(h, gain, w, *cast_in)
    return out[0], out[1], list(out[2:])


def _mix_in_combine_kernel(dest_ref, hn_ref, gate_ref, y_ref, g_ref, w_ref,
                           h_ref, hg_ref, da_ref, buf_ref, sem):
    step = pl.program_id(0)
    last = pl.num_programs(0) - 1
    slot = step % 2
    tile_rows = MIX_TILE * TOKEN_ROWS
    n_tokens = dest_ref.shape[0] // TOP_K

    def start_gathers(s, to_slot, rows):
        base = s * MIX_TILE
        slots = [dest_ref[k * n_tokens + base + r] for r in rows for k in range(TOP_K)]
        for j, src in enumerate(slots):
            r, k = rows[j // TOP_K], j % TOP_K
            _token_copy(y_ref, src, buf_ref.at[to_slot, k], r, sem.at[to_slot]).start(priority=j % 2)

    def wait_gathers(of_slot):
        for k in range(TOP_K):
            pltpu.make_async_copy(y_ref.at[pl.ds(0, tile_rows), :], buf_ref.at[of_slot, k],
                                  sem.at[of_slot]).wait()

    @pl.when(step == 0)
    def _():
        def first(g, carry):
            start_gathers(0, 0, [g * ISSUE_TOKENS + r for r in range(ISSUE_TOKENS)])
            return carry

        lax.fori_loop(0, MIX_TILE // ISSUE_TOKENS, first, 0)

    wait_gathers(slot)
    gate = gate_ref[...]
    h = (hn_ref[...] + gate[:, 0:1] * _load_token_major(buf_ref.at[slot, 0], MIX_TILE)
         + gate[:, 1:2] * _load_token_major(buf_ref.at[slot, 1], MIX_TILE))
    h_ref[...] = h
    u = _rms(h, g_ref[...]).astype(BF16)

    nxt = jnp.minimum(step + 1, last)
    n_q = DA_HEADS * 2 * DA_DQK
    groups = [(c, c + HG_WIDTH) for c in range(0, HG_COLS, HG_WIDTH)]
    groups += [(HG_COLS, HG_COLS + n_q), (HG_COLS + n_q, W_IN_COLS)]
    per_group = -(-MIX_TILE // len(groups))
    for gi, (c0, c1) in enumerate(groups):
        proj = _dot(u, w_ref[:, c0:c1])
        if c1 <= HG_COLS:
            hg_ref[:, c0:c1] = proj
        elif c0 == HG_COLS:
            da_ref[:, :n_q] = (proj * Q_SCALE).astype(BF16)
        else:
            da_ref[:, n_q:] = proj.astype(BF16)
        rows = list(range(gi * per_group, min((gi + 1) * per_group, MIX_TILE)))
        for r0 in range(0, len(rows), ISSUE_TOKENS):
            start_gathers(nxt, 1 - slot, rows[r0:r0 + ISSUE_TOKENS])

    @pl.when(step == last)
    def _():
        wait_gathers(1 - slot)


def _mix_in_combine(dest, hn, gate, y, gain, w, layer):
    n = hn.shape[0]
    row = lambda i, d: (i, 0)
    return pl.pallas_call(
        _mix_in_combine_kernel,
        grid_spec=pltpu.PrefetchScalarGridSpec(
            num_scalar_prefetch=1,
            grid=(n // MIX_TILE,),
            in_specs=[
                pl.BlockSpec((MIX_TILE, D_MODEL), row),
                pl.BlockSpec((MIX_TILE, N_EXPERTS), row),
                pl.BlockSpec(memory_space=pl.ANY),
                pl.BlockSpec((1, D_MODEL), lambda i, d: (0, 0)),
                pl.BlockSpec((None, D_MODEL, W_IN_COLS), lambda i, d: (layer, 0, 0),
                             pipeline_mode=pl.Buffered(1)),
            ],
            out_specs=[
                pl.BlockSpec((MIX_TILE, D_MODEL), row),
                pl.BlockSpec((MIX_TILE, HG_COLS), row),
                pl.BlockSpec((MIX_TILE, DA_COLS), row),
            ],
            scratch_shapes=[pltpu.VMEM((2, TOP_K, MIX_TILE * TOKEN_ROWS, LANES), F32),
                            pltpu.SemaphoreType.DMA((2,))],
        ),
        out_shape=[
            jax.ShapeDtypeStruct((n, D_MODEL), F32),
            jax.ShapeDtypeStruct((n, HG_COLS), F32),
            jax.ShapeDtypeStruct((n, DA_COLS), BF16),
        ],
        compiler_params=_params("arbitrary"),
        name="mix_in_combine",
    )(dest, hn, gate, y, gain, w)


HG_LEVELS = (32, 16, 8, 4, 2, 1)
N_SUMS = len(HG_LEVELS) + 2


def _hgrn_consts():
    c = HG_CHUNK
    t = np.arange(c)[:, None]
    j = np.arange(c)[None, :]
    sums = np.zeros((N_SUMS, c, c), np.float32)
    masks = np.zeros((len(HG_LEVELS) + 1, c, c), np.float32)
    sums[0] = j <= t
    masks[0] = np.eye(c)
    for li, w in enumerate(HG_LEVELS, start=1):
        ref = (t // (2 * w)) * (2 * w) + w
        sums[li] = np.where(t >= ref, (j > ref) & (j <= t), (j > t) & (j <= ref))
        masks[li] = (t // (2 * w) == j // (2 * w)) & (t % (2 * w) >= w) & (j % (2 * w) < w)
    sums[N_SUMS - 1] = j > t
    sums = sums.reshape(N_SUMS * c, c)
    return np.concatenate([sums, sums], axis=1), masks


_HG_SUMS, _HG_MASKS = _hgrn_consts()


def _hgrn_kernel(hg_ref, loga_ref, log1m_ref, nw_ref, sums_ref, masks_ref, o_ref, state_ref):
    c_idx = pl.program_id(1)

    @pl.when(c_idx == 0)
    def _():
        state_ref[...] = jnp.zeros_like(state_ref)

    C = HG_CHUNK
    W = HG_WIDTH
    n_chunks = HG_TILE // C
    sums = sums_ref[...]
    nw = nw_ref[...]
    row_idx = c_idx * HG_TILE + lax.broadcasted_iota(jnp.int32, (HG_TILE, 1), 0)
    valid = row_idx >= (LEAD - N_META)
    step = lax.broadcasted_iota(jnp.int32, (C, 1), 0)
    head_cols = [slice(hd * HG_D, (hd + 1) * HG_D) for hd in range(HG_HEADS)]

    f = hg_ref[:, W:2 * W]
    qf = _silu(hg_ref[:, 0:W])
    ls = jnp.minimum(f, 0.0) - jnp.log(1.0 + jnp.exp(-jnp.abs(f)))
    cc = log1m_ref[...] + ls
    loga = loga_ref[...]
    lf = jnp.maximum(loga, cc) + jnp.log(1.0 + jnp.exp(-jnp.abs(loga - cc)))
    kk = jnp.exp(cc - f)
    lf = jnp.where(valid, lf, 0.0)
    kk = jnp.where(valid, kk, 0.0)
    lf2 = lf * LOG2E
    lf_hi = lf2.astype(BF16)
    lf_lo = (lf2 - lf_hi.astype(F32)).astype(BF16)
    vb = hg_ref[:, 2 * W:3 * W].astype(BF16)
    gate = _silu(hg_ref[:, 3 * W:4 * W])

    def side_by_side(x):
        return jnp.concatenate([x[ch * C:(ch + 1) * C] for ch in range(n_chunks)], axis=1)

    qf_w, kk_w = side_by_side(qf), side_by_side(kk)
    qb_w, kb_w = qf_w.astype(BF16), kk_w.astype(BF16)
    e = jnp.exp2(_dot(sums, jnp.concatenate([side_by_side(lf_hi), side_by_side(lf_lo)], axis=0)))
    e_b = e[0:C]
    decay_end = e_b[C - 1:C, :]
    q_in = (qf_w * e_b).astype(BF16)
    k_out = (kk_w * e[(N_SUMS - 1) * C:N_SUMS * C]).astype(BF16)
    z = [(jnp.where((step & w) != 0, qf_w, kk_w) * e[li * C:(li + 1) * C]).astype(BF16)
         for li, w in enumerate(HG_LEVELS, start=1)]
    unit_cols = [[slice(ch * W + hd * HG_D, ch * W + (hd + 1) * HG_D) for hd in range(HG_HEADS)]
                 for ch in range(n_chunks)]
    scores = []
    for ch in range(n_chunks):
        scores.append([])
        for cols in unit_cols[ch]:
            s = masks_ref[0] * _dot_nt(qb_w[:, cols], kb_w[:, cols])
            for li in range(1, len(HG_LEVELS) + 1):
                zl = z[li - 1][:, cols]
                s += masks_ref[li] * _dot_nt(zl, zl)
            scores[ch].append(s.astype(BF16))

    for ch in range(n_chunks):
        rows = slice(ch * C, (ch + 1) * C)
        for hd, cols in enumerate(head_cols):
            wide = unit_cols[ch][hd]
            st = state_ref[hd]
            v_h = vb[rows, cols]
            o = _dot_nt(q_in[:, wide], st.astype(BF16)) + _dot(scores[ch][hd], v_h)
            state_ref[hd] = st * decay_end[:, wide] + _dot_tn(v_h, k_out[:, wide])
            o = _rms(o, nw) * gate[rows, cols]
            o_ref[rows, cols] = o.astype(o_ref.dtype)


def _hgrn(hg, loga, log1m, norm_w, batch, length):
    hg3 = hg.reshape(batch, length, HG_COLS)
    out = pl.pallas_call(
        _hgrn_kernel,
        grid=(batch, length // HG_TILE),
        in_specs=[
            pl.BlockSpec((None, HG_TILE, HG_COLS), lambda b, c: (b, c, 0)),
            pl.BlockSpec((1, HG_WIDTH), lambda b, c: (0, 0)),
            pl.BlockSpec((1, HG_WIDTH), lambda b, c: (0, 0)),
            pl.BlockSpec((1, HG_D), lambda b, c: (0, 0)),
            pl.BlockSpec(_HG_SUMS.shape, lambda b, c: (0, 0)),
            pl.BlockSpec(_HG_MASKS.shape, lambda b, c: (0, 0, 0)),
        ],
        out_specs=pl.BlockSpec((None, HG_TILE, HG_WIDTH), lambda b, c: (b, c, 0)),
        out_shape=jax.ShapeDtypeStruct((batch, length, HG_WIDTH), BF16),
        scratch_shapes=[pltpu.VMEM((HG_HEADS, HG_D, HG_D), F32)],
        compiler_params=_params("parallel", "arbitrary"),
        name="hgrn2",
    )(hg3, loga, log1m, norm_w, jnp.asarray(_HG_SUMS, BF16), jnp.asarray(_HG_MASKS, F32))
    return out.reshape(batch * length, HG_WIDTH)


def _attn_kernel(q_ref, k_ref, v_ref, toe_ref, cst_ref, w_ref, o_ref, s_ref, *, n_blocks):
    lam = cst_ref[0:1, 0:1]
    post = cst_ref[1:2, :]
    lane = lax.broadcasted_iota(jnp.int32, (Q_BLOCK, Q_BLOCK), 1)
    first_half = lane < DA_DQK
    key_ok0 = lane >= (LEAD - N_META)
    inert_bias = jnp.where(key_ok0, 0.0, NEG)

    def near_bias(kind, kb):
        bias = toe_ref[kind]
        if kb == 0:
            bias = jnp.where(key_ok0, bias, NEG)
        return bias

    def slabs(x):
        return [x[:, c:c + Q_BLOCK] for c in range(0, x.shape[1], Q_BLOCK)]

    map_rows = 2 * Q_BLOCK
    groups = [tuple(range(i, min(i + Q_GROUP, n_blocks))) for i in range(0, n_blocks, Q_GROUP)]

    def both_maps(bias):
        return jnp.concatenate([bias, bias], axis=0)

    def stacked_q(blocks):
        parts = []
        for i in blocks:
            qi = q_ref[i * Q_BLOCK:(i + 1) * Q_BLOCK, :]
            zero = jnp.zeros_like(qi)
            parts += [jnp.where(first_half, qi, zero), jnp.where(first_half, zero, qi)]
        return jnp.concatenate(parts, axis=0)

    def group_tiles(blocks):
        first, last = blocks[0], blocks[-1]
        tiles = []
        far_end = max(first - 1, 0)
        if far_end >= 1:
            tiles.append((0, Q_BLOCK, jnp.concatenate([inert_bias] * (2 * len(blocks)), axis=0), 0))
        kb = 1
        while kb < far_end:
            width = KEY_TILE
            while kb + width // Q_BLOCK > far_end:
                width //= 2
            tiles.append((kb * Q_BLOCK, width, None, 0))
            kb += width // Q_BLOCK
        for kb in range(far_end, last + 1):
            biases, first_row = [], None
            for r, qb in enumerate(blocks):
                if kb > qb:
                    continue
                if first_row is None:
                    first_row = r * map_rows
                if kb == qb:
                    bias = near_bias(0, kb)
                elif kb == qb - 1:
                    bias = near_bias(1, kb)
                else:
                    bias = inert_bias if kb == 0 else jnp.zeros_like(inert_bias)
                biases.append(both_maps(bias))
            tiles.append((kb * Q_BLOCK, Q_BLOCK, jnp.concatenate(biases, axis=0), first_row))
        return tiles

    def merge(acc, x, first_row, op):
        if acc is None:
            return x
        if first_row == 0:
            return op(acc, x)
        return jnp.concatenate([acc[:first_row], op(acc[first_row:], x)], axis=0)

    def sweep_scores(g):
        blocks = groups[g]
        rows = len(blocks) * map_rows
        q2 = stacked_q(blocks)
        m_acc = None
        for start, width, bias, first_row in group_tiles(blocks):
            s = _dot_nt(q2[first_row:], k_ref[start:start + width, :])
            if bias is not None:
                s = s + bias
            s_ref[g % 2, first_row:rows, start:start + width] = s
            for slab in slabs(s):
                m_acc = merge(m_acc, slab, first_row, jnp.maximum)
        return m_acc.max(axis=-1, keepdims=True)

    row_max = sweep_scores(0)
    for g, blocks in enumerate(groups):
        rows = len(blocks) * map_rows
        m = row_max
        if g + 1 < len(groups):
            row_max = sweep_scores(g + 1)
        l_acc = o_acc = None
        for start, width, _, first_row in group_tiles(blocks):
            p = jnp.exp2(s_ref[g % 2, first_row:rows, start:start + width] - m[first_row:])
            for slab in slabs(p):
                l_acc = merge(l_acc, slab, first_row, jnp.add)
            o_acc = merge(o_acc, _dot(p.astype(BF16), v_ref[start:start + width, :]), first_row, jnp.add)
        o2 = o_acc * (1.0 / l_acc.sum(axis=-1, keepdims=True))
        for r, i in enumerate(blocks):
            o = o2[r * map_rows:r * map_rows + Q_BLOCK] - lam * o2[r * map_rows + Q_BLOCK:(r + 1) * map_rows]
            o = _rms(o, w_ref[...]) * post
            o_ref[i * Q_BLOCK:(i + 1) * Q_BLOCK, :] = o.astype(o_ref.dtype)


def _attn(da, toe, cst, subln_w, batch, length):
    da3 = da.reshape(batch, length, DA_COLS)
    hw = DA_HEADS
    out = pl.pallas_call(
        functools.partial(_attn_kernel, n_blocks=length // Q_BLOCK),
        grid=(batch, DA_HEADS),
        in_specs=[
            pl.BlockSpec((None, length, DA_DV), lambda b, h: (b, 0, h)),
            pl.BlockSpec((None, length, DA_DV), lambda b, h: (b, 0, hw + h)),
            pl.BlockSpec((None, length, DA_DV), lambda b, h: (b, 0, 2 * hw + h)),
            pl.BlockSpec((None, 2, Q_BLOCK, Q_BLOCK), lambda b, h: (h, 0, 0, 0)),
            pl.BlockSpec((SUBLANES, LANES), lambda b, h: (0, 0)),
            pl.BlockSpec((1, DA_DV), lambda b, h: (0, 0)),
        ],
        out_specs=pl.BlockSpec((None, length, DA_DV), lambda b, h: (b, 0, h)),
        out_shape=jax.ShapeDtypeStruct((batch, length, DA_HEADS * DA_DV), BF16),
        scratch_shapes=[pltpu.VMEM((2, Q_GROUP * 2 * Q_BLOCK, length), F32)],
        compiler_params=_params("parallel", "parallel"),
        name="diff_attn",
    )(da3, da3, da3, toe, cst, subln_w)
    return out.reshape(batch * length, DA_HEADS * DA_DV)


def _t5_bucket(dist):
    n = jnp.maximum(dist, 0)
    max_exact = REL_BUCKETS // 2
    nf = jnp.maximum(n, max_exact).astype(F32)
    large = max_exact + (jnp.log(nf / max_exact) / math.log(REL_MAX_DIST / max_exact)
                         * (REL_BUCKETS - max_exact)).astype(jnp.int32)
    large = jnp.minimum(large, REL_BUCKETS - 1)
    return jnp.where(n < max_exact, n, large)


def _attn_bias_tables(rel_bias):
    tab = rel_bias.astype(F32)
    qi = jnp.arange(Q_BLOCK, dtype=jnp.int32)[:, None]
    ki = jnp.arange(Q_BLOCK, dtype=jnp.int32)[None, :]

    def lookup(bucket):
        onehot = bucket[None, :, :, None] == jnp.arange(REL_BUCKETS, dtype=jnp.int32)
        return jnp.sum(jnp.where(onehot, tab.T[:, None, None, :], 0.0), axis=-1)

    far = tab[REL_BUCKETS - 1][:, None, None]
    diag = jnp.where((ki <= qi)[None], (lookup(_t5_bucket(qi - ki)) - far) * LOG2E, NEG)
    prev = (lookup(_t5_bucket(qi - ki + Q_BLOCK)) - far) * LOG2E
    return jnp.stack([diag, prev], axis=1)


def _out_proj_router_kernel(h_ref, ohg_ref, oda_ref, wo_ref, g_ref, router_ref, before_ref,
                            hn_ref, u_ref, route_ref, gate_ref, count_ref, seen_ref):
    @pl.when(pl.program_id(0) == 0)
    def _():
        seen_ref[...] = jnp.zeros_like(seen_ref)

    hn = (h_ref[...] + _dot(ohg_ref[...], wo_ref[:HG_WIDTH, :])
          + _dot(oda_ref[...], wo_ref[HG_WIDTH:, :]))
    hn_ref[...] = hn
    u = _rms(hn, g_ref[...])
    _store_token_major(u_ref, u)
    u_hi = u.astype(BF16)
    u_lo = (u - u_hi.astype(F32)).astype(BF16)
    logits = _dot(jnp.concatenate([u_hi, u_lo, u_hi], axis=1), router_ref[...])
    logits = logits.T[:N_EXPERTS]
    expert = lax.broadcasted_iota(jnp.int32, logits.shape, 0)

    def top(x):
        best = x.max(axis=0, keepdims=True)
        return best, jnp.where(x == best, expert, N_EXPERTS).min(axis=0, keepdims=True)

    l1, e1 = top(logits)
    l2, e2 = top(jnp.where(expert == e1, -jnp.inf, logits))
    w2 = jnp.exp(l2 - l1)
    g1 = 1.0 / (1.0 + w2)
    g2 = w2 / (1.0 + w2)

    pick1 = (expert == e1).astype(F32)
    pick2 = (expert == e2).astype(F32)
    picked = pick1 + pick2
    ahead = seen_ref[:, 0:1] + _dot(picked.astype(BF16), before_ref[...])
    rank1 = jnp.sum(pick1 * ahead, axis=0, keepdims=True).astype(jnp.int32)
    rank2 = jnp.sum(pick2 * ahead, axis=0, keepdims=True).astype(jnp.int32)
    seen = seen_ref[...] + jnp.sum(picked, axis=1, keepdims=True)
    seen_ref[...] = seen

    route_ref[...] = jnp.where(expert == 0, e1, jnp.where(expert == 1, e2, jnp.where(
        expert == 2, rank1, jnp.where(expert == 3, rank2, 0))))
    gate_ref[...] = jnp.where(expert == 0, g1, jnp.where(expert == 1, g2, 0.0)).T
    count_ref[...] = seen.astype(jnp.int32)


def _out_proj_router(h, o_hg, o_da, wo, layer, gain, router):
    n = h.shape[0]
    row = lambda i: (i, 0)
    full = lambda i: (0, 0)
    assert N_EXPERTS >= 2 * TOP_K
    lanes_of = lambda i: (0, i)
    before = jnp.asarray(np.triu(np.ones((ROW_TILE, ROW_TILE), np.float32), 1), BF16)
    return pl.pallas_call(
        _out_proj_router_kernel,
        grid=(n // ROW_TILE,),
        in_specs=[
            pl.BlockSpec((ROW_TILE, D_MODEL), row),
            pl.BlockSpec((ROW_TILE, HG_WIDTH), row),
            pl.BlockSpec((ROW_TILE, DA_HEADS * DA_DV), row),
            pl.BlockSpec((None,) + wo.shape[1:], lambda i: (layer, 0, 0)),
            pl.BlockSpec((1, D_MODEL), full),
            pl.BlockSpec(router.shape, full),
            pl.BlockSpec(before.shape, full),
        ],
        out_specs=[
            pl.BlockSpec((ROW_TILE, D_MODEL), row),
            pl.BlockSpec((ROW_TILE * TOKEN_ROWS, LANES), row),
            pl.BlockSpec((N_EXPERTS, ROW_TILE), lanes_of),
            pl.BlockSpec((ROW_TILE, N_EXPERTS), row),
            pl.BlockSpec((N_EXPERTS, LANES), full),
        ],
        out_shape=[
            jax.ShapeDtypeStruct((n, D_MODEL), F32),
            jax.ShapeDtypeStruct((n * TOKEN_ROWS, LANES), F32),
            jax.ShapeDtypeStruct((N_EXPERTS, n), jnp.int32),
            jax.ShapeDtypeStruct((n, N_EXPERTS), F32),
            jax.ShapeDtypeStruct((N_EXPERTS, LANES), jnp.int32),
        ],
        scratch_shapes=[pltpu.VMEM((N_EXPERTS, LANES), F32)],
        compiler_params=_params("arbitrary"),
        name="out_proj_router",
    )(h, o_hg, o_da, wo, gain, router, before)


def _dense_layer_kernel(h_ref, ohg_ref, oda_ref, wo_ref, g_ref, w1_ref, w3_ref, w2_ref, *rest):
    n_cast = (len(rest) - 1) // 2
    o_ref = rest[n_cast]
    hn = (h_ref[...] + _dot(ohg_ref[...], wo_ref[:HG_WIDTH, :])
          + _dot(oda_ref[...], wo_ref[HG_WIDTH:, :]))
    u = _rms(hn, g_ref[...]).astype(BF16)
    a = _dot(u, w1_ref[...])
    act = (_silu(a) * _dot(u, w3_ref[...])).astype(BF16)
    o_ref[...] = hn + _dot(act, w2_ref[...])
    _run_cast_jobs(rest[:n_cast], rest[n_cast + 1:])


def _dense_layer(h, o_hg, o_da, wo, layer, gain, w1, w3, w2, idx, to_cast, cast_idx):
    n = h.shape[0]
    steps = n // ROW_TILE
    row = lambda i: (i, 0)

    def resident(w, index):
        return pl.BlockSpec((None,) + w.shape[1:], lambda i: (index, 0, 0),
                            pipeline_mode=pl.Buffered(1))

    jobs = [(w.reshape(w.shape[0], w.shape[1] * w.shape[2], w.shape[3]), cast_idx) for w in to_cast]
    cast_in, cast_in_specs, cast_out_specs, cast_out_shape = _cast_jobs(jobs, steps)

    out = pl.pallas_call(
        _dense_layer_kernel,
        grid=(steps,),
        in_specs=[
            pl.BlockSpec((ROW_TILE, D_MODEL), row),
            pl.BlockSpec((ROW_TILE, HG_WIDTH), row),
            pl.BlockSpec((ROW_TILE, DA_HEADS * DA_DV), row),
            resident(wo, layer),
            pl.BlockSpec((1, D_MODEL), lambda i: (0, 0)),
            resident(w1, idx),
            resident(w3, idx),
            resident(w2, idx),
        ] + cast_in_specs,
        out_specs=[pl.BlockSpec((ROW_TILE, D_MODEL), row)] + cast_out_specs,
        out_shape=[jax.ShapeDtypeStruct((n, D_MODEL), F32)] + cast_out_shape,
        compiler_params=_params("arbitrary"),
        name="dense_layer",
    )(h, o_hg, o_da, wo, gain, w1, w3, w2, *cast_in)
    return out[0], [b.reshape(w.shape[1:]) for b, w in zip(out[1:], to_cast)]


DISPATCH_TILE = 2176
ISSUE_TOKENS = 8


def _token_copy(src_ref, src_tok, dst_ref, dst_tok, sem):
    src = pl.multiple_of(src_tok * TOKEN_ROWS, TOKEN_ROWS)
    dst = pl.multiple_of(dst_tok * TOKEN_ROWS, TOKEN_ROWS)
    return pltpu.make_async_copy(src_ref.at[pl.ds(src, TOKEN_ROWS), :],
                                 dst_ref.at[pl.ds(dst, TOKEN_ROWS), :], sem)


def _dispatch_kernel(dest_ref, empty_ref, u_ref, xs_ref, zero_ref, sem):
    step = pl.program_id(0)
    base = step * DISPATCH_TILE
    tile_rows = DISPATCH_TILE * TOKEN_ROWS

    def retire_tile():
        pltpu.make_async_copy(u_ref, xs_ref.at[pl.ds(0, tile_rows), :], sem).wait()

    @pl.when(step == 0)
    def _():
        zero_ref[...] = jnp.zeros_like(zero_ref)

        def clear(g, carry):
            slots = [empty_ref[g * ISSUE_TOKENS * TOP_K + j] for j in range(ISSUE_TOKENS * TOP_K)]
            for j, slot in enumerate(slots):
                _token_copy(zero_ref, 0, xs_ref, slot, sem).start(priority=j % 2)
            return carry

        n_empty = empty_ref.shape[0]
        lax.fori_loop(0, n_empty // (ISSUE_TOKENS * TOP_K), clear, 0)
        empty_rows = n_empty * TOKEN_ROWS
        pltpu.make_async_copy(u_ref.at[pl.ds(0, empty_rows), :], xs_ref.at[pl.ds(0, empty_rows), :],
                              sem).wait()

    n_tokens = dest_ref.shape[0] // TOP_K

    def start(g, carry):
        r0 = g * ISSUE_TOKENS
        slots = [dest_ref[(j % TOP_K) * n_tokens + base + r0 + j // TOP_K]
                 for j in range(ISSUE_TOKENS * TOP_K)]
        for j, slot in enumerate(slots):
            _token_copy(u_ref, r0 + j // TOP_K, xs_ref, slot, sem).start(priority=j % 2)
        return carry

    lax.fori_loop(0, DISPATCH_TILE // ISSUE_TOKENS, start, 0)
    for _ in range(TOP_K):
        retire_tile()


def _dispatch(dest, empty_slots, u_tm, n_slots):
    n = dest.shape[0] // TOP_K
    assert empty_slots.shape[0] <= DISPATCH_TILE and n % DISPATCH_TILE == 0
    return pl.pallas_call(
        _dispatch_kernel,
        grid_spec=pltpu.PrefetchScalarGridSpec(
            num_scalar_prefetch=2,
            grid=(n // DISPATCH_TILE,),
            in_specs=[pl.BlockSpec((DISPATCH_TILE * TOKEN_ROWS, LANES), lambda i, d, e: (i, 0))],
            out_specs=pl.BlockSpec(memory_space=pl.ANY),
            scratch_shapes=[pltpu.VMEM((TOKEN_ROWS, LANES), u_tm.dtype),
                            pltpu.SemaphoreType.DMA(())],
        ),
        out_shape=jax.ShapeDtypeStruct((n_slots * TOKEN_ROWS, LANES), u_tm.dtype),
        compiler_params=_params("arbitrary"),
        name="moe_dispatch",
    )(dest, empty_slots, u_tm)


def _expert_kernel(be_ref, nused_ref, xs_ref, w1_ref, w3_ref, w2_ref, y_ref):
    del be_ref
    blk = pl.program_id(0)

    @pl.when(blk < nused_ref[0])
    def _():
        x = _load_token_major(xs_ref, MOE_TILE).astype(BF16)
        acc = jnp.zeros((MOE_TILE, D_MODEL), F32)
        for c0 in range(0, D_FF_EXPERT, FF_CHUNK):
            a = _dot(x, w1_ref[:, c0:c0 + FF_CHUNK])
            act = (_silu(a) * _dot(x, w3_ref[:, c0:c0 + FF_CHUNK])).astype(BF16)
            acc = acc + _dot(act, w2_ref[c0:c0 + FF_CHUNK, :])
        _store_token_major(y_ref, acc)

    @pl.when(blk >= nused_ref[0])
    def _():
        y_ref[...] = jnp.zeros_like(y_ref)


def _experts(block_expert, n_used, xs, w1, w3, w2):
    n_blocks = xs.shape[0] // (MOE_TILE * TOKEN_ROWS)
    wmap = lambda i, be, nu: (be[i], 0, 0)
    slots = pl.BlockSpec((MOE_TILE * TOKEN_ROWS, LANES), lambda i, be, nu: (i, 0))
    return pl.pallas_call(
        _expert_kernel,
        grid_spec=pltpu.PrefetchScalarGridSpec(
            num_scalar_prefetch=2,
            grid=(n_blocks,),
            in_specs=[
                slots,
                pl.BlockSpec((None, D_MODEL, D_FF_EXPERT), wmap),
                pl.BlockSpec((None, D_MODEL, D_FF_EXPERT), wmap),
                pl.BlockSpec((None, D_FF_EXPERT, D_MODEL), wmap),
            ],
            out_specs=slots,
        ),
        out_shape=jax.ShapeDtypeStruct(xs.shape, F32),
        compiler_params=_params("arbitrary"),
        name="moe_experts",
    )(block_expert, n_used, xs, w1, w3, w2)


def _combine_final_kernel(dest_ref, h_ref, gate_ref, y_ref, g_ref, o_ref, buf_ref, sem):
    step = pl.program_id(0)
    n_tokens = dest_ref.shape[0] // TOP_K

    def issue(s, slot):
        base = s * Q_BLOCK

        def body(g, carry):
            r0 = g * ISSUE_TOKENS
            slots = [dest_ref[(j % TOP_K) * n_tokens + base + r0 + j // TOP_K]
                     for j in range(ISSUE_TOKENS * TOP_K)]
            for j, src in enumerate(slots):
                _token_copy(y_ref, src, buf_ref.at[slot, j % TOP_K], r0 + j // TOP_K,
                            sem.at[slot]).start(priority=j % 2)
            return carry

        lax.fori_loop(0, Q_BLOCK // ISSUE_TOKENS, body, 0)

    @pl.when(step == 0)
    def _():
        issue(0, 0)

    for slot in range(2):
        @pl.when(step % 2 == slot)
        def _():
            @pl.when(step + 1 < pl.num_programs(0))
            def _():
                issue(step + 1, 1 - slot)

            for k in range(TOP_K):
                pltpu.make_async_copy(y_ref.at[pl.ds(0, Q_BLOCK * TOKEN_ROWS), :],
                                      buf_ref.at[slot, k], sem.at[slot]).wait()
            gate = gate_ref[...]
            h = (h_ref[...]
                 + gate[:, 0:1] * _load_token_major(buf_ref.at[slot, 0], Q_BLOCK)
                 + gate[:, 1:2] * _load_token_major(buf_ref.at[slot, 1], Q_BLOCK))
            o_ref[...] = _rms(h, g_ref[...])


def _combine_final(dest, h, gate, y, gain, batch, length, seq):
    n = h.shape[0]
    per_seq = length // Q_BLOCK
    lead_blocks = LEAD // Q_BLOCK
    return pl.pallas_call(
        _combine_final_kernel,
        grid_spec=pltpu.PrefetchScalarGridSpec(
            num_scalar_prefetch=1,
            grid=(n // Q_BLOCK,),
            in_specs=[
                pl.BlockSpec((Q_BLOCK, D_MODEL), lambda i, d: (i, 0)),
                pl.BlockSpec((Q_BLOCK, N_EXPERTS), lambda i, d: (i, 0)),
                pl.BlockSpec(memory_space=pl.ANY),
                pl.BlockSpec((1, D_MODEL), lambda i, d: (0, 0)),
            ],
            out_specs=pl.BlockSpec(
                (None, Q_BLOCK, D_MODEL),
                lambda i, d: (i // per_seq, jnp.maximum(i % per_seq - lead_blocks, 0), 0)),
            scratch_shapes=[pltpu.VMEM((2, TOP_K, Q_BLOCK * TOKEN_ROWS, LANES), F32),
                            pltpu.SemaphoreType.DMA((2,))],
        ),
        out_shape=jax.ShapeDtypeStruct((batch, seq, D_MODEL), F32),
        compiler_params=_params("arbitrary"),
        name="moe_combine_final",
    )(dest, h, gate, y, gain)


def _prefix_sum(x):
    k = x.shape[0]
    keep = (np.arange(k)[None, :] <= np.arange(k)[:, None]).reshape((k, k) + (1,) * (x.ndim - 1))
    return jnp.sum(jnp.where(keep, x[None], jnp.zeros_like(x[None])), axis=1)


def _slot_tables(route, counts, n_slots):
    n = route.shape[1]
    flat_e = route[0:TOP_K].reshape(n * TOP_K)
    rank = route[TOP_K:2 * TOP_K].reshape(n * TOP_K)
    counts = counts[:, 0]
    padded = (counts + MOE_TILE - 1) // MOE_TILE * MOE_TILE
    pad_end = _prefix_sum(padded)
    pad_start = pad_end - padded
    onehot = flat_e[:, None] == jnp.arange(N_EXPERTS, dtype=jnp.int32)[None, :]
    dest = (rank + jnp.sum(jnp.where(onehot, pad_start[None, :], 0), axis=-1)).astype(jnp.int32)
    n_blocks = n_slots // MOE_TILE
    block_first = jnp.arange(n_blocks, dtype=jnp.int32) * MOE_TILE
    block_expert = jnp.minimum(
        jnp.sum((pad_end[None, :] <= block_first[:, None]).astype(jnp.int32), axis=-1),
        N_EXPERTS - 1).astype(jnp.int32)
    n_used = (pad_end[-1:] // MOE_TILE).astype(jnp.int32)
    n_empty = n_slots - n * TOP_K
    gap_end = _prefix_sum(padded - counts)
    j = jnp.arange(n_empty, dtype=jnp.int32)
    owner = jnp.sum((j[:, None] >= gap_end[None, :]).astype(jnp.int32), axis=-1)
    first_empty = jnp.concatenate([pad_start + counts, pad_end[-1:]])
    gap_start = jnp.concatenate([jnp.zeros((1,), gap_end.dtype), gap_end])
    pick = owner[:, None] == jnp.arange(N_EXPERTS + 1, dtype=jnp.int32)[None, :]
    empty_slots = (j + jnp.sum(jnp.where(pick, (first_empty - gap_start)[None, :], 0), axis=-1))
    return dest, block_expert, n_used, empty_slots.astype(jnp.int32)


def _moe_ffn(u, route, counts, w1, w3, w2):
    n = route.shape[1]
    n_slots = (n * TOP_K // MOE_TILE + N_EXPERTS) * MOE_TILE
    dest, block_expert, n_used, empty_slots = _slot_tables(route, counts, n_slots)
    xs = _dispatch(dest, empty_slots, u, n_slots)
    y = _experts(block_expert, n_used, xs, w1, w3, w2)
    return dest, y


def kernel(x, meta, rel_bias, norm_mix, w_in, hg_lb_logits, hg_norm_w, da_lambda, da_subln_w, w_out, norm_ffn, dense_w1, dense_w3, dense_w2, moe_router, moe_w1, moe_w3, moe_w2, final_norm):
    batch, seq, d = x.shape
    length = LEAD + seq
    h = jnp.concatenate([
        jnp.zeros((batch, LEAD - N_META, d), x.dtype),
        jnp.broadcast_to(meta[None].astype(x.dtype), (batch, N_META, d)),
        x], axis=1).reshape(batch * length, d)

    toe = _attn_bias_tables(rel_bias)
    lb_cum = _prefix_sum(jax.nn.softmax(hg_lb_logits.astype(F32), axis=0))
    lb_all = jnp.clip(lb_cum - lb_cum[0:1], 0.0, LB_MAX)
    log_lb = jnp.log(lb_all)
    log_1m_lb = jnp.log1p(-lb_all)

    assert DEPTH % 2 == 0
    dense_f32 = (dense_w1, dense_w3, dense_w2)
    stacked_rows = lambda w: w.reshape(1, w.shape[0] * w.shape[1], w.shape[2])

    unsummed = None
    for l in range(DEPTH):
        if l == 0:
            jobs = [(w, 0) for w in dense_f32] + [(stacked_rows(w_out), 0), (stacked_rows(w_in), 0)]
            hg, da, casts = _mix_in(h, norm_mix[l][None], w_in[0:1].astype(BF16), 0, jobs)
            dense_b = [w[None] for w in casts[:3]]
            w_out_b = casts[3].reshape(w_out.shape)
            w_in_b = casts[4].reshape(w_in.shape)
        elif unsummed is None:
            jobs = [(w, (l + 1) // 2) for w in dense_f32] if l + 1 < DEPTH else []
            hg, da, casts = _mix_in(h, norm_mix[l][None], w_in_b, l, jobs)
            if jobs:
                dense_b = [w[None] for w in casts]
        else:
            h, hg, da = _mix_in_combine(*unsummed, norm_mix[l][None], w_in_b, l)
            unsummed = None
        lam_init = 0.8 - 0.6 * math.exp(-0.3 * l)
        lv = da_lambda[l].astype(F32)
        lam = jnp.exp(jnp.sum(lv[0] * lv[1])) - jnp.exp(jnp.sum(lv[2] * lv[3])) + lam_init
        cst = jnp.zeros((SUBLANES, LANES), F32).at[0].set(lam).at[1].set(1.0 - lam_init)
        o_hg = _hgrn(hg, log_lb[l][None], log_1m_lb[l][None], hg_norm_w[l][None], batch, length)
        o_da = _attn(da, toe, cst, da_subln_w[l][None], batch, length)
        i = l // 2
        if l % 2 == 0:
            h, moe_b = _dense_layer(h, o_hg, o_da, w_out_b, l, norm_ffn[l][None], *dense_b, 0,
                                    (moe_w1, moe_w3, moe_w2), i)
        else:
            router = jnp.zeros((d, LANES), F32).at[:, :N_EXPERTS].set(moe_router[i].astype(F32))
            r_hi = router.astype(BF16)
            r_lo = (router - r_hi.astype(F32)).astype(BF16)
            router = jnp.concatenate([r_hi, r_hi, r_lo], axis=0)
            hn, u, route, gate, counts = _out_proj_router(h, o_hg, o_da, w_out_b, l,
                                                          norm_ffn[l][None], router)
            dest, y = _moe_ffn(u, route, counts, *moe_b)
            if l + 1 < DEPTH:
                unsummed = (dest, hn, gate, y)
            else:
                return _combine_final(dest, hn, gate, y, final_norm[None], batch, length, seq)
```

```python
import functools
import math

import jax
import jax.numpy as jnp
import numpy as np
from jax import lax
from jax.experimental import pallas as pl
from jax.experimental.pallas import tpu as pltpu

D_MODEL = 1024
DEPTH = 4
N_META = 16
LEAD = 128
HG_WIDTH = 512
HG_HEADS = 4
HG_D = 128
HG_CHUNK = 64
DA_HEADS = 4
DA_DQK = 64
DA_DV = 128
Q_BLOCK = 128
KEY_TILE = 4 * Q_BLOCK
Q_GROUP = 2
LOG2E = math.log2(math.e)
Q_SCALE = DA_DQK ** -0.5 * LOG2E
REL_BUCKETS = 32
REL_MAX_DIST = 128
N_EXPERTS = 8
TOP_K = 2
D_FF_EXPERT = 3584
EPS = 1e-6
NEG = -1e30
LB_MAX = 0.999
HG_COLS = 4 * HG_WIDTH
DA_COLS = 3 * DA_HEADS * DA_DV
W_IN_COLS = HG_COLS + DA_COLS

LANES = 128
SUBLANES = 8
VMEM_LIMIT = 56 * 1024 * 1024

ROW_TILE = 256
MIX_TILE = 512
HG_TILE = 128
HG_BATCH = 2
MOE_TILE = 256
FF_CHUNK = 1792

F32 = jnp.float32
BF16 = jnp.bfloat16


def _params(*sem):
    return pltpu.CompilerParams(dimension_semantics=sem, vmem_limit_bytes=VMEM_LIMIT)


def _dot(a, b):
    return jnp.dot(a, b, preferred_element_type=F32)


def _dot_nt(a, b):
    return lax.dot_general(a, b, (((1,), (1,)), ((), ())), preferred_element_type=F32)


def _dot_tn(a, b):
    return lax.dot_general(a, b, (((0,), (0,)), ((), ())), preferred_element_type=F32)


def _rms(x, gain):
    return x * lax.rsqrt(jnp.mean(x * x, axis=-1, keepdims=True) + EPS) * gain


TOKEN_ROWS = D_MODEL // LANES


def _store_token_major(ref, x):
    t = x.shape[0]
    for s in range(TOKEN_ROWS):
        ref[pl.ds(s, t, stride=TOKEN_ROWS), :] = x[:, s * LANES:(s + 1) * LANES]


def _load_token_major(ref, t):
    return jnp.concatenate(
        [ref[pl.ds(s, t, stride=TOKEN_ROWS), :] for s in range(TOKEN_ROWS)], axis=1)


def _silu(x):
    return x * (0.5 * jnp.tanh(0.5 * x) + 0.5)


BF16_SUBLANES = 16


def _cast_jobs(jobs, grid_steps):
    inputs, in_specs, out_specs, out_shapes = [], [], [], []
    for w, index in jobs:
        _, rows, cols = w.shape
        steps = max(s for s in range(1, grid_steps + 1)
                    if rows % s == 0 and (rows // s) % BF16_SUBLANES == 0)
        slab_rows = rows // steps
        in_specs.append(pl.BlockSpec((None, slab_rows, cols),
                                     lambda i, index=index, steps=steps: (index, jnp.minimum(i, steps - 1), 0)))
        out_specs.append(pl.BlockSpec((slab_rows, cols),
                                      lambda i, steps=steps: (jnp.minimum(i, steps - 1), 0)))
        out_shapes.append(jax.ShapeDtypeStruct((rows, cols), BF16))
        inputs.append(w)
    return inputs, in_specs, out_specs, out_shapes


def _run_cast_jobs(src_refs, dst_refs):
    for src_ref, dst_ref in zip(src_refs, dst_refs):
        dst_ref[...] = src_ref[...].astype(BF16)


def _mix_in_kernel(x_ref, g_ref, w_ref, *rest):
    n_cast = (len(rest) - 2) // 2
    hg_ref, da_ref = rest[n_cast:n_cast + 2]
    u = _rms(x_ref[...], g_ref[...]).astype(BF16)
    hg_ref[...] = _dot(u, w_ref[:, :HG_COLS])
    n_q = DA_HEADS * 2 * DA_DQK
    da_ref[:, :n_q] = (_dot(u, w_ref[:, HG_COLS:HG_COLS + n_q]) * Q_SCALE).astype(BF16)
    da_ref[:, n_q:] = _dot(u, w_ref[:, HG_COLS + n_q:]).astype(BF16)
    _run_cast_jobs(rest[:n_cast], rest[n_cast + 2:])


def _mix_in(h, gain, w, layer, to_cast=()):
    n = h.shape[0]
    steps = n // MIX_TILE
    cast_in, cast_in_specs, cast_out_specs, cast_out_shapes = _cast_jobs(to_cast, steps)
    out = pl.pallas_call(
        _mix_in_kernel,
        grid=(steps,),
        in_specs=[
            pl.BlockSpec((MIX_TILE, D_MODEL), lambda i: (i, 0)),
            pl.BlockSpec((1, D_MODEL), lambda i: (0, 0)),
            pl.BlockSpec((None, D_MODEL, W_IN_COLS), lambda i: (layer, 0, 0),
                         pipeline_mode=pl.Buffered(1)),
        ] + cast_in_specs,
        out_specs=[
            pl.BlockSpec((MIX_TILE, HG_COLS), lambda i: (i, 0)),
            pl.BlockSpec((MIX_TILE, DA_COLS), lambda i: (i, 0)),
        ] + cast_out_specs,
        out_shape=[
            jax.ShapeDtypeStruct((n, HG_COLS), F32),
            jax.ShapeDtypeStruct((n, DA_COLS), BF16),
        ] + cast_out_shapes,
        compiler_params=_params("arbitrary"),
        name="mix_in",
    )(h, gain, w, *cast_in)
    return out[0], out[1], list(out[2:])


def _mix_in_combine_kernel(dest_ref, hn_ref, gate_ref, y_ref, g_ref, w_ref,
                           h_ref, hg_ref, da_ref, buf_ref, sem):
    step = pl.program_id(0)
    last = pl.num_programs(0) - 1
    slot = step % 2
    tile_rows = MIX_TILE * TOKEN_ROWS
    n_tokens = dest_ref.shape[0] // TOP_K

    def start_gathers(s, to_slot, rows):
        base = s * MIX_TILE
        slots = [dest_ref[k * n_tokens + base + r] for r in rows for k in range(TOP_K)]
        for j, src in enumerate(slots):
            r, k = rows[j // TOP_K], j % TOP_K
            _token_copy(y_ref, src, buf_ref.at[to_slot, k], r, sem.at[to_slot]).start(priority=j % 2)

    def wait_gathers(of_slot):
        for k in range(TOP_K):
            pltpu.make_async_copy(y_ref.at[pl.ds(0, tile_rows), :], buf_ref.at[of_slot, k],
                                  sem.at[of_slot]).wait()

    @pl.when(step == 0)
    def _():
        def first(g, carry):
            start_gathers(0, 0, [g * ISSUE_TOKENS + r for r in range(ISSUE_TOKENS)])
            return carry

        lax.fori_loop(0, MIX_TILE // ISSUE_TOKENS, first, 0)

    wait_gathers(slot)
    gate = gate_ref[...]
    h = (hn_ref[...] + gate[:, 0:1] * _load_token_major(buf_ref.at[slot, 0], MIX_TILE)
         + gate[:, 1:2] * _load_token_major(buf_ref.at[slot, 1], MIX_TILE))
    h_ref[...] = h
    u = _rms(h, g_ref[...]).astype(BF16)

    nxt = jnp.minimum(step + 1, last)
    n_q = DA_HEADS * 2 * DA_DQK
    groups = [(c, c + HG_WIDTH) for c in range(0, HG_COLS, HG_WIDTH)]
    groups += [(HG_COLS, HG_COLS + n_q), (HG_COLS + n_q, W_IN_COLS)]
    per_group = -(-MIX_TILE // len(groups))
    for gi, (c0, c1) in enumerate(groups):
        proj = _dot(u, w_ref[:, c0:c1])
        if c1 <= HG_COLS:
            hg_ref[:, c0:c1] = proj
        elif c0 == HG_COLS:
            da_ref[:, :n_q] = (proj * Q_SCALE).astype(BF16)
        else:
            da_ref[:, n_q:] = proj.astype(BF16)
        rows = list(range(gi * per_group, min((gi + 1) * per_group, MIX_TILE)))
        for r0 in range(0, len(rows), ISSUE_TOKENS):
            start_gathers(nxt, 1 - slot, rows[r0:r0 + ISSUE_TOKENS])

    @pl.when(step == last)
    def _():
        wait_gathers(1 - slot)


def _mix_in_combine(dest, hn, gate, y, gain, w, layer):
    n = hn.shape[0]
    row = lambda i, d: (i, 0)
    return pl.pallas_call(
        _mix_in_combine_kernel,
        grid_spec=pltpu.PrefetchScalarGridSpec(
            num_scalar_prefetch=1,
            grid=(n // MIX_TILE,),
            in_specs=[
                pl.BlockSpec((MIX_TILE, D_MODEL), row),
                pl.BlockSpec((MIX_TILE, N_EXPERTS), row),
                pl.BlockSpec(memory_space=pl.ANY),
                pl.BlockSpec((1, D_MODEL), lambda i, d: (0, 0)),
                pl.BlockSpec((None, D_MODEL, W_IN_COLS), lambda i, d: (layer, 0, 0),
                             pipeline_mode=pl.Buffered(1)),
            ],
            out_specs=[
                pl.BlockSpec((MIX_TILE, D_MODEL), row),
                pl.BlockSpec((MIX_TILE, HG_COLS), row),
                pl.BlockSpec((MIX_TILE, DA_COLS), row),
            ],
            scratch_shapes=[pltpu.VMEM((2, TOP_K, MIX_TILE * TOKEN_ROWS, LANES), F32),
                            pltpu.SemaphoreType.DMA((2,))],
        ),
        out_shape=[
            jax.ShapeDtypeStruct((n, D_MODEL), F32),
            jax.ShapeDtypeStruct((n, HG_COLS), F32),
            jax.ShapeDtypeStruct((n, DA_COLS), BF16),
        ],
        compiler_params=_params("arbitrary"),
        name="mix_in_combine",
    )(dest, hn, gate, y, gain, w)


HG_LEVELS = (32, 16, 8, 4, 2, 1)
N_SUMS = len(HG_LEVELS) + 2


def _hgrn_consts():
    c = HG_CHUNK
    t = np.arange(c)[:, None]
    j = np.arange(c)[None, :]
    sums = np.zeros((N_SUMS, c, c), np.float32)
    masks = np.zeros((len(HG_LEVELS) + 1, c, c), np.float32)
    sums[0] = j <= t
    masks[0] = np.eye(c)
    for li, w in enumerate(HG_LEVELS, start=1):
        ref = (t // (2 * w)) * (2 * w) + w
        sums[li] = np.where(t >= ref, (j > ref) & (j <= t), (j > t) & (j <= ref))
        masks[li] = (t // (2 * w) == j // (2 * w)) & (t % (2 * w) >= w) & (j % (2 * w) < w)
    sums[N_SUMS - 1] = j > t
    sums = sums.reshape(N_SUMS * c, c)
    return np.concatenate([sums, sums], axis=1), masks


_HG_SUMS, _HG_MASKS = _hgrn_consts()


def _hgrn_kernel(hg_ref, loga_ref, log1m_ref, nw_ref, sums_ref, masks_ref, o_ref, state_ref):
    c_idx = pl.program_id(1)

    @pl.when(c_idx == 0)
    def _():
        state_ref[...] = jnp.zeros_like(state_ref)

    C = HG_CHUNK
    W = HG_WIDTH
    per_seq = HG_TILE // C
    n_chunks = HG_BATCH * per_seq
    rows_all = HG_BATCH * HG_TILE
    sums = sums_ref[...]
    nw = nw_ref[...]
    row_idx = c_idx * HG_TILE + lax.broadcasted_iota(jnp.int32, (rows_all, 1), 0) % HG_TILE
    valid = row_idx >= (LEAD - N_META)
    step = lax.broadcasted_iota(jnp.int32, (C, 1), 0)
    head_cols = [slice(hd * HG_D, (hd + 1) * HG_D) for hd in range(HG_HEADS)]

    def columns(c0, c1):
        return hg_ref[:, :, c0:c1].reshape(rows_all, c1 - c0)

    f = columns(W, 2 * W)
    qf = _silu(columns(0, W))
    ls = jnp.minimum(f, 0.0) - jnp.log(1.0 + jnp.exp(-jnp.abs(f)))
    cc = log1m_ref[...] + ls
    loga = loga_ref[...]
    lf = jnp.maximum(loga, cc) + jnp.log(1.0 + jnp.exp(-jnp.abs(loga - cc)))
    kk = jnp.exp(cc - f)
    lf = jnp.where(valid, lf, 0.0)
    kk = jnp.where(valid, kk, 0.0)
    lf2 = lf * LOG2E
    lf_hi = lf2.astype(BF16)
    lf_lo = (lf2 - lf_hi.astype(F32)).astype(BF16)
    vb = columns(2 * W, 3 * W).astype(BF16)
    gate = _silu(columns(3 * W, 4 * W))

    def side_by_side(x):
        return jnp.concatenate([x[ch * C:(ch + 1) * C] for ch in range(n_chunks)], axis=1)

    qf_w, kk_w = side_by_side(qf), side_by_side(kk)
    qb_w, kb_w = qf_w.astype(BF16), kk_w.astype(BF16)
    e = jnp.exp2(_dot(sums, jnp.concatenate([side_by_side(lf_hi), side_by_side(lf_lo)], axis=0)))
    e_b = e[0:C]
    decay_end = e_b[C - 1:C, :]
    q_in = (qf_w * e_b).astype(BF16)
    k_out = (kk_w * e[(N_SUMS - 1) * C:N_SUMS * C]).astype(BF16)
    z = [(jnp.where((step & w) != 0, qf_w, kk_w) * e[li * C:(li + 1) * C]).astype(BF16)
         for li, w in enumerate(HG_LEVELS, start=1)]
    unit_cols = [[slice(ch * W + hd * HG_D, ch * W + (hd + 1) * HG_D) for hd in range(HG_HEADS)]
                 for ch in range(n_chunks)]
    scores = []
    for ch in range(n_chunks):
        scores.append([])
        for cols in unit_cols[ch]:
            s = masks_ref[0] * _dot_nt(qb_w[:, cols], kb_w[:, cols])
            for li in range(1, len(HG_LEVELS) + 1):
                zl = z[li - 1][:, cols]
                s += masks_ref[li] * _dot_nt(zl, zl)
            scores[ch].append(s.astype(BF16))

    for local in range(per_seq):
        for seq in range(HG_BATCH):
            ch = seq * per_seq + local
            rows = slice(ch * C, (ch + 1) * C)
            for hd, cols in enumerate(head_cols):
                wide = unit_cols[ch][hd]
                st = state_ref[seq * HG_HEADS + hd]
                v_h = vb[rows, cols]
                o = _dot_nt(q_in[:, wide], st.astype(BF16)) + _dot(scores[ch][hd], v_h)
                state_ref[seq * HG_HEADS + hd] = (st * decay_end[:, wide]
                                                  + _dot_tn(v_h, k_out[:, wide]))
                o = _rms(o, nw) * gate[rows, cols]
                o_ref[seq, local * C:(local + 1) * C, cols] = o.astype(o_ref.dtype)


def _hgrn(hg, loga, log1m, norm_w, batch, length):
    hg3 = hg.reshape(batch, length, HG_COLS)
    out = pl.pallas_call(
        _hgrn_kernel,
        grid=(batch // HG_BATCH, length // HG_TILE),
        in_specs=[
            pl.BlockSpec((HG_BATCH, HG_TILE, HG_COLS), lambda b, c: (b, c, 0)),
            pl.BlockSpec((1, HG_WIDTH), lambda b, c: (0, 0)),
            pl.BlockSpec((1, HG_WIDTH), lambda b, c: (0, 0)),
            pl.BlockSpec((1, HG_D), lambda b, c: (0, 0)),
            pl.BlockSpec(_HG_SUMS.shape, lambda b, c: (0, 0)),
            pl.BlockSpec(_HG_MASKS.shape, lambda b, c: (0, 0, 0)),
        ],
        out_specs=pl.BlockSpec((HG_BATCH, HG_TILE, HG_WIDTH), lambda b, c: (b, c, 0)),
        out_shape=jax.ShapeDtypeStruct((batch, length, HG_WIDTH), BF16),
        scratch_shapes=[pltpu.VMEM((HG_BATCH * HG_HEADS, HG_D, HG_D), F32)],
        compiler_params=_params("parallel", "arbitrary"),
        name="hgrn2",
    )(hg3, loga, log1m, norm_w, jnp.asarray(_HG_SUMS, BF16), jnp.asarray(_HG_MASKS, F32))
    return out.reshape(batch * length, HG_WIDTH)


def _attn_kernel(q_ref, k_ref, v_ref, toe_ref, cst_ref, w_ref, o_ref, s_ref, *, n_blocks):
    lam = cst_ref[0:1, 0:1]
    post = cst_ref[1:2, :]
    lane = lax.broadcasted_iota(jnp.int32, (Q_BLOCK, Q_BLOCK), 1)
    first_half = lane < DA_DQK
    key_ok0 = lane >= (LEAD - N_META)
    inert_bias = jnp.where(key_ok0, 0.0, NEG)

    def near_bias(kind, kb):
        bias = toe_ref[kind]
        if kb == 0:
            bias = jnp.where(key_ok0, bias, NEG)
        return bias

    def slabs(x):
        return [x[:, c:c + Q_BLOCK] for c in range(0, x.shape[1], Q_BLOCK)]

    map_rows = 2 * Q_BLOCK
    groups = [tuple(range(i, min(i + Q_GROUP, n_blocks))) for i in range(0, n_blocks, Q_GROUP)]

    def both_maps(bias):
        return jnp.concatenate([bias, bias], axis=0)

    def stacked_q(blocks):
        parts = []
        for i in blocks:
            qi = q_ref[i * Q_BLOCK:(i + 1) * Q_BLOCK, :]
            zero = jnp.zeros_like(qi)
            parts += [jnp.where(first_half, qi, zero), jnp.where(first_half, zero, qi)]
        return jnp.concatenate(parts, axis=0)

    def group_tiles(blocks):
        first, last = blocks[0], blocks[-1]
        tiles = []
        far_end = max(first - 1, 0)
        if far_end >= 1:
            tiles.append((0, Q_BLOCK, jnp.concatenate([inert_bias] * (2 * len(blocks)), axis=0), 0))
        kb = 1
        while kb < far_end:
            width = KEY_TILE
            while kb + width // Q_BLOCK > far_end:
                width //= 2
            tiles.append((kb * Q_BLOCK, width, None, 0))
            kb += width // Q_BLOCK
        for kb in range(far_end, last + 1):
            biases, first_row = [], None
            for r, qb in enumerate(blocks):
                if kb > qb:
                    continue
                if first_row is None:
                    first_row = r * map_rows
                if kb == qb:
                    bias = near_bias(0, kb)
                elif kb == qb - 1:
                    bias = near_bias(1, kb)
                else:
                    bias = inert_bias if kb == 0 else jnp.zeros_like(inert_bias)
                biases.append(both_maps(bias))
            tiles.append((kb * Q_BLOCK, Q_BLOCK, jnp.concatenate(biases, axis=0), first_row))
        return tiles

    def merge(acc, x, first_row, op):
        if acc is None:
            return x
        if first_row == 0:
            return op(acc, x)
        return jnp.concatenate([acc[:first_row], op(acc[first_row:], x)], axis=0)

    def sweep_scores(g):
        blocks = groups[g]
        rows = len(blocks) * map_rows
        q2 = stacked_q(blocks)
        m_acc = None
        for start, width, bias, first_row in group_tiles(blocks):
            s = _dot_nt(q2[first_row:], k_ref[start:start + width, :])
            if bias is not None:
                s = s + bias
            s_ref[g % 2, first_row:rows, start:start + width] = s
            for slab in slabs(s):
                m_acc = merge(m_acc, slab, first_row, jnp.maximum)
        return m_acc.max(axis=-1, keepdims=True)

    row_max = sweep_scores(0)
    for g, blocks in enumerate(groups):
        rows = len(blocks) * map_rows
        m = row_max
        if g + 1 < len(groups):
            row_max = sweep_scores(g + 1)
        l_acc = o_acc = None
        for start, width, _, first_row in group_tiles(blocks):
            p = jnp.exp2(s_ref[g % 2, first_row:rows, start:start + width] - m[first_row:])
            for slab in slabs(p):
                l_acc = merge(l_acc, slab, first_row, jnp.add)
            o_acc = merge(o_acc, _dot(p.astype(BF16), v_ref[start:start + width, :]), first_row, jnp.add)
        o2 = o_acc * (1.0 / l_acc.sum(axis=-1, keepdims=True))
        for r, i in enumerate(blocks):
            o = o2[r * map_rows:r * map_rows + Q_BLOCK] - lam * o2[r * map_rows + Q_BLOCK:(r + 1) * map_rows]
            o = _rms(o, w_ref[...]) * post
            o_ref[i * Q_BLOCK:(i + 1) * Q_BLOCK, :] = o.astype(o_ref.dtype)


def _attn(da, toe, cst, subln_w, batch, length):
    da3 = da.reshape(batch, length, DA_COLS)
    hw = DA_HEADS
    out = pl.pallas_call(
        functools.partial(_attn_kernel, n_blocks=length // Q_BLOCK),
        grid=(batch, DA_HEADS),
        in_specs=[
            pl.BlockSpec((None, length, DA_DV), lambda b, h: (b, 0, h)),
            pl.BlockSpec((None, length, DA_DV), lambda b, h: (b, 0, hw + h)),
            pl.BlockSpec((None, length, DA_DV), lambda b, h: (b, 0, 2 * hw + h)),
            pl.BlockSpec((None, 2, Q_BLOCK, Q_BLOCK), lambda b, h: (h, 0, 0, 0)),
            pl.BlockSpec((SUBLANES, LANES), lambda b, h: (0, 0)),
            pl.BlockSpec((1, DA_DV), lambda b, h: (0, 0)),
        ],
        out_specs=pl.BlockSpec((None, length, DA_DV), lambda b, h: (b, 0, h)),
        out_shape=jax.ShapeDtypeStruct((batch, length, DA_HEADS * DA_DV), BF16),
        scratch_shapes=[pltpu.VMEM((2, Q_GROUP * 2 * Q_BLOCK, length), F32)],
        compiler_params=_params("parallel", "parallel"),
        name="diff_attn",
    )(da3, da3, da3, toe, cst, subln_w)
    return out.reshape(batch * length, DA_HEADS * DA_DV)


def _t5_bucket(dist):
    n = jnp.maximum(dist, 0)
    max_exact = REL_BUCKETS // 2
    nf = jnp.maximum(n, max_exact).astype(F32)
    large = max_exact + (jnp.log(nf / max_exact) / math.log(REL_MAX_DIST / max_exact)
                         * (REL_BUCKETS - max_exact)).astype(jnp.int32)
    large = jnp.minimum(large, REL_BUCKETS - 1)
    return jnp.where(n < max_exact, n, large)


def _attn_bias_tables(rel_bias):
    tab = rel_bias.astype(F32)
    qi = jnp.arange(Q_BLOCK, dtype=jnp.int32)[:, None]
    ki = jnp.arange(Q_BLOCK, dtype=jnp.int32)[None, :]

    def lookup(bucket):
        onehot = bucket[None, :, :, None] == jnp.arange(REL_BUCKETS, dtype=jnp.int32)
        return jnp.sum(jnp.where(onehot, tab.T[:, None, None, :], 0.0), axis=-1)

    far = tab[REL_BUCKETS - 1][:, None, None]
    diag = jnp.where((ki <= qi)[None], (lookup(_t5_bucket(qi - ki)) - far) * LOG2E, NEG)
    prev = (lookup(_t5_bucket(qi - ki + Q_BLOCK)) - far) * LOG2E
    return jnp.stack([diag, prev], axis=1)


def _out_proj_router_kernel(h_ref, ohg_ref, oda_ref, wo_ref, g_ref, router_ref, before_ref,
                            hn_ref, u_ref, route_ref, gate_ref, count_ref, seen_ref):
    @pl.when(pl.program_id(0) == 0)
    def _():
        seen_ref[...] = jnp.zeros_like(seen_ref)

    hn = (h_ref[...] + _dot(ohg_ref[...], wo_ref[:HG_WIDTH, :])
          + _dot(oda_ref[...], wo_ref[HG_WIDTH:, :]))
    hn_ref[...] = hn
    u = _rms(hn, g_ref[...])
    _store_token_major(u_ref, u)
    u_hi = u.astype(BF16)
    u_lo = (u - u_hi.astype(F32)).astype(BF16)
    logits = _dot(jnp.concatenate([u_hi, u_lo, u_hi], axis=1), router_ref[...])
    logits = logits.T[:N_EXPERTS]
    expert = lax.broadcasted_iota(jnp.int32, logits.shape, 0)

    def top(x):
        best = x.max(axis=0, keepdims=True)
        return best, jnp.where(x == best, expert, N_EXPERTS).min(axis=0, keepdims=True)

    l1, e1 = top(logits)
    l2, e2 = top(jnp.where(expert == e1, -jnp.inf, logits))
    w2 = jnp.exp(l2 - l1)
    g1 = 1.0 / (1.0 + w2)
    g2 = w2 / (1.0 + w2)

    pick1 = (expert == e1).astype(F32)
    pick2 = (expert == e2).astype(F32)
    picked = pick1 + pick2
    ahead = seen_ref[:, 0:1] + _dot(picked.astype(BF16), before_ref[...])
    rank1 = jnp.sum(pick1 * ahead, axis=0, keepdims=True).astype(jnp.int32)
    rank2 = jnp.sum(pick2 * ahead, axis=0, keepdims=True).astype(jnp.int32)
    seen = seen_ref[...] + jnp.sum(picked, axis=1, keepdims=True)
    seen_ref[...] = seen

    route_ref[...] = jnp.where(expert == 0, e1, jnp.where(expert == 1, e2, jnp.where(
        expert == 2, rank1, jnp.where(expert == 3, rank2, 0))))
    gate_ref[...] = jnp.where(expert == 0, g1, jnp.where(expert == 1, g2, 0.0)).T
    count_ref[...] = seen.astype(jnp.int32)


def _out_proj_router(h, o_hg, o_da, wo, layer, gain, router):
    n = h.shape[0]
    row = lambda i: (i, 0)
    full = lambda i: (0, 0)
    assert N_EXPERTS >= 2 * TOP_K
    lanes_of = lambda i: (0, i)
    before = jnp.asarray(np.triu(np.ones((ROW_TILE, ROW_TILE), np.float32), 1), BF16)
    return pl.pallas_call(
        _out_proj_router_kernel,
        grid=(n // ROW_TILE,),
        in_specs=[
            pl.BlockSpec((ROW_TILE, D_MODEL), row),
            pl.BlockSpec((ROW_TILE, HG_WIDTH), row),
            pl.BlockSpec((ROW_TILE, DA_HEADS * DA_DV), row),
            pl.BlockSpec((None,) + wo.shape[1:], lambda i: (layer, 0, 0)),
            pl.BlockSpec((1, D_MODEL), full),
            pl.BlockSpec(router.shape, full),
            pl.BlockSpec(before.shape, full),
        ],
        out_specs=[
            pl.BlockSpec((ROW_TILE, D_MODEL), row),
            pl.BlockSpec((ROW_TILE * TOKEN_ROWS, LANES), row),
            pl.BlockSpec((N_EXPERTS, ROW_TILE), lanes_of),
            pl.BlockSpec((ROW_TILE, N_EXPERTS), row),
            pl.BlockSpec((N_EXPERTS, LANES), full),
        ],
        out_shape=[
            jax.ShapeDtypeStruct((n, D_MODEL), F32),
            jax.ShapeDtypeStruct((n * TOKEN_ROWS, LANES), F32),
            jax.ShapeDtypeStruct((N_EXPERTS, n), jnp.int32),
            jax.ShapeDtypeStruct((n, N_EXPERTS), F32),
            jax.ShapeDtypeStruct((N_EXPERTS, LANES), jnp.int32),
        ],
        scratch_shapes=[pltpu.VMEM((N_EXPERTS, LANES), F32)],
        compiler_params=_params("arbitrary"),
        name="out_proj_router",
    )(h, o_hg, o_da, wo, gain, router, before)


def _dense_layer_kernel(h_ref, ohg_ref, oda_ref, wo_ref, g_ref, w1_ref, w3_ref, w2_ref, *rest):
    n_cast = (len(rest) - 1) // 2
    o_ref = rest[n_cast]
    hn = (h_ref[...] + _dot(ohg_ref[...], wo_ref[:HG_WIDTH, :])
          + _dot(oda_ref[...], wo_ref[HG_WIDTH:, :]))
    u = _rms(hn, g_ref[...]).astype(BF16)
    a = _dot(u, w1_ref[...])
    act = (_silu(a) * _dot(u, w3_ref[...])).astype(BF16)
    o_ref[...] = hn + _dot(act, w2_ref[...])
    _run_cast_jobs(rest[:n_cast], rest[n_cast + 1:])


def _dense_layer(h, o_hg, o_da, wo, layer, gain, w1, w3, w2, idx, to_cast, cast_idx):
    n = h.shape[0]
    steps = n // ROW_TILE
    row = lambda i: (i, 0)

    def resident(w, index):
        return pl.BlockSpec((None,) + w.shape[1:], lambda i: (index, 0, 0),
                            pipeline_mode=pl.Buffered(1))

    jobs = [(w.reshape(w.shape[0], w.shape[1] * w.shape[2], w.shape[3]), cast_idx) for w in to_cast]
    cast_in, cast_in_specs, cast_out_specs, cast_out_shape = _cast_jobs(jobs, steps)

    out = pl.pallas_call(
        _dense_layer_kernel,
        grid=(steps,),
        in_specs=[
            pl.BlockSpec((ROW_TILE, D_MODEL), row),
            pl.BlockSpec((ROW_TILE, HG_WIDTH), row),
            pl.BlockSpec((ROW_TILE, DA_HEADS * DA_DV), row),
            resident(wo, layer),
            pl.BlockSpec((1, D_MODEL), lambda i: (0, 0)),
            resident(w1, idx),
            resident(w3, idx),
            resident(w2, idx),
        ] + cast_in_specs,
        out_specs=[pl.BlockSpec((ROW_TILE, D_MODEL), row)] + cast_out_specs,
        out_shape=[jax.ShapeDtypeStruct((n, D_MODEL), F32)] + cast_out_shape,
        compiler_params=_params("arbitrary"),
        name="dense_layer",
    )(h, o_hg, o_da, wo, gain, w1, w3, w2, *cast_in)
    return out[0], [b.reshape(w.shape[1:]) for b, w in zip(out[1:], to_cast)]


DISPATCH_TILE = 2176
ISSUE_TOKENS = 8


def _token_copy(src_ref, src_tok, dst_ref, dst_tok, sem):
    src = pl.multiple_of(src_tok * TOKEN_ROWS, TOKEN_ROWS)
    dst = pl.multiple_of(dst_tok * TOKEN_ROWS, TOKEN_ROWS)
    return pltpu.make_async_copy(src_ref.at[pl.ds(src, TOKEN_ROWS), :],
                                 dst_ref.at[pl.ds(dst, TOKEN_ROWS), :], sem)


def _dispatch_kernel(dest_ref, empty_ref, u_ref, xs_ref, zero_ref, sem):
    step = pl.program_id(0)
    base = step * DISPATCH_TILE
    tile_rows = DISPATCH_TILE * TOKEN_ROWS

    def retire_tile():
        pltpu.make_async_copy(u_ref, xs_ref.at[pl.ds(0, tile_rows), :], sem).wait()

    @pl.when(step == 0)
    def _():
        zero_ref[...] = jnp.zeros_like(zero_ref)

        def clear(g, carry):
            slots = [empty_ref[g * ISSUE_TOKENS * TOP_K + j] for j in range(ISSUE_TOKENS * TOP_K)]
            for j, slot in enumerate(slots):
                _token_copy(zero_ref, 0, xs_ref, slot, sem).start(priority=j % 2)
            return carry

        n_empty = empty_ref.shape[0]
        lax.fori_loop(0, n_empty // (ISSUE_TOKENS * TOP_K), clear, 0)
        empty_rows = n_empty * TOKEN_ROWS
        pltpu.make_async_copy(u_ref.at[pl.ds(0, empty_rows), :], xs_ref.at[pl.ds(0, empty_rows), :],
                              sem).wait()

    n_tokens = dest_ref.shape[0] // TOP_K

    def start(g, carry):
        r0 = g * ISSUE_TOKENS
        slots = [dest_ref[(j % TOP_K) * n_tokens + base + r0 + j // TOP_K]
                 for j in range(ISSUE_TOKENS * TOP_K)]
        for j, slot in enumerate(slots):
            _token_copy(u_ref, r0 + j // TOP_K, xs_ref, slot, sem).start(priority=j % 2)
        return carry

    lax.fori_loop(0, DISPATCH_TILE // ISSUE_TOKENS, start, 0)
    for _ in range(TOP_K):
        retire_tile()


def _dispatch(dest, empty_slots, u_tm, n_slots):
    n = dest.shape[0] // TOP_K
    assert empty_slots.shape[0] <= DISPATCH_TILE and n % DISPATCH_TILE == 0
    return pl.pallas_call(
        _dispatch_kernel,
        grid_spec=pltpu.PrefetchScalarGridSpec(
            num_scalar_prefetch=2,
            grid=(n // DISPATCH_TILE,),
            in_specs=[pl.BlockSpec((DISPATCH_TILE * TOKEN_ROWS, LANES), lambda i, d, e: (i, 0))],
            out_specs=pl.BlockSpec(memory_space=pl.ANY),
            scratch_shapes=[pltpu.VMEM((TOKEN_ROWS, LANES), u_tm.dtype),
                            pltpu.SemaphoreType.DMA(())],
        ),
        out_shape=jax.ShapeDtypeStruct((n_slots * TOKEN_ROWS, LANES), u_tm.dtype),
        compiler_params=_params("arbitrary"),
        name="moe_dispatch",
    )(dest, empty_slots, u_tm)


def _expert_kernel(be_ref, nused_ref, xs_ref, w1_ref, w3_ref, w2_ref, y_ref):
    del be_ref
    blk = pl.program_id(0)

    @pl.when(blk < nused_ref[0])
    def _():
        x = _load_token_major(xs_ref, MOE_TILE).astype(BF16)
        acc = jnp.zeros((MOE_TILE, D_MODEL), F32)
        for c0 in range(0, D_FF_EXPERT, FF_CHUNK):
            a = _dot(x, w1_ref[:, c0:c0 + FF_CHUNK])
            act = (_silu(a) * _dot(x, w3_ref[:, c0:c0 + FF_CHUNK])).astype(BF16)
            acc = acc + _dot(act, w2_ref[c0:c0 + FF_CHUNK, :])
        _store_token_major(y_ref, acc)

    @pl.when(blk >= nused_ref[0])
    def _():
        y_ref[...] = jnp.zeros_like(y_ref)


def _experts(block_expert, n_used, xs, w1, w3, w2):
    n_blocks = xs.shape[0] // (MOE_TILE * TOKEN_ROWS)
    wmap = lambda i, be, nu: (be[i], 0, 0)
    slots = pl.BlockSpec((MOE_TILE * TOKEN_ROWS, LANES), lambda i, be, nu: (i, 0))
    return pl.pallas_call(
        _expert_kernel,
        grid_spec=pltpu.PrefetchScalarGridSpec(
            num_scalar_prefetch=2,
            grid=(n_blocks,),
            in_specs=[
                slots,
                pl.BlockSpec((None, D_MODEL, D_FF_EXPERT), wmap),
                pl.BlockSpec((None, D_MODEL, D_FF_EXPERT), wmap),
                pl.BlockSpec((None, D_FF_EXPERT, D_MODEL), wmap),
            ],
            out_specs=slots,
        ),
        out_shape=jax.ShapeDtypeStruct(xs.shape, F32),
        compiler_params=_params("arbitrary"),
        name="moe_experts",
    )(block_expert, n_used, xs, w1, w3, w2)


def _combine_final_kernel(dest_ref, h_ref, gate_ref, y_ref, g_ref, o_ref, buf_ref, sem):
    step = pl.program_id(0)
    n_tokens = dest_ref.shape[0] // TOP_K

    def issue(s, slot):
        base = s * Q_BLOCK

        def body(g, carry):
            r0 = g * ISSUE_TOKENS
            slots = [dest_ref[(j % TOP_K) * n_tokens + base + r0 + j // TOP_K]
                     for j in range(ISSUE_TOKENS * TOP_K)]
            for j, src in enumerate(slots):
                _token_copy(y_ref, src, buf_ref.at[slot, j % TOP_K], r0 + j // TOP_K,
                            sem.at[slot]).start(priority=j % 2)
            return carry

        lax.fori_loop(0, Q_BLOCK // ISSUE_TOKENS, body, 0)

    @pl.when(step == 0)
    def _():
        issue(0, 0)

    for slot in range(2):
        @pl.when(step % 2 == slot)
        def _():
            @pl.when(step + 1 < pl.num_programs(0))
            def _():
                issue(step + 1, 1 - slot)

            for k in range(TOP_K):
                pltpu.make_async_copy(y_ref.at[pl.ds(0, Q_BLOCK * TOKEN_ROWS), :],
                                      buf_ref.at[slot, k], sem.at[slot]).wait()
            gate = gate_ref[...]
            h = (h_ref[...]
                 + gate[:, 0:1] * _load_token_major(buf_ref.at[slot, 0], Q_BLOCK)
                 + gate[:, 1:2] * _load_token_major(buf_ref.at[slot, 1], Q_BLOCK))
            o_ref[...] = _rms(h, g_ref[...])


def _combine_final(dest, h, gate, y, gain, batch, length, seq):
    n = h.shape[0]
    per_seq = length // Q_BLOCK
    lead_blocks = LEAD // Q_BLOCK
    return pl.pallas_call(
        _combine_final_kernel,
        grid_spec=pltpu.PrefetchScalarGridSpec(
            num_scalar_prefetch=1,
            grid=(n // Q_BLOCK,),
            in_specs=[
                pl.BlockSpec((Q_BLOCK, D_MODEL), lambda i, d: (i, 0)),
                pl.BlockSpec((Q_BLOCK, N_EXPERTS), lambda i, d: (i, 0)),
                pl.BlockSpec(memory_space=pl.ANY),
                pl.BlockSpec((1, D_MODEL), lambda i, d: (0, 0)),
            ],
            out_specs=pl.BlockSpec(
                (None, Q_BLOCK, D_MODEL),
                lambda i, d: (i // per_seq, jnp.maximum(i % per_seq - lead_blocks, 0), 0)),
            scratch_shapes=[pltpu.VMEM((2, TOP_K, Q_BLOCK * TOKEN_ROWS, LANES), F32),
                            pltpu.SemaphoreType.DMA((2,))],
        ),
        out_shape=jax.ShapeDtypeStruct((batch, seq, D_MODEL), F32),
        compiler_params=_params("arbitrary"),
        name="moe_combine_final",
    )(dest, h, gate, y, gain)


def _prefix_sum(x):
    k = x.shape[0]
    keep = (np.arange(k)[None, :] <= np.arange(k)[:, None]).reshape((k, k) + (1,) * (x.ndim - 1))
    return jnp.sum(jnp.where(keep, x[None], jnp.zeros_like(x[None])), axis=1)


def _slot_tables(route, counts, n_slots):
    n = route.shape[1]
    flat_e = route[0:TOP_K].reshape(n * TOP_K)
    rank = route[TOP_K:2 * TOP_K].reshape(n * TOP_K)
    counts = counts[:, 0]
    padded = (counts + MOE_TILE - 1) // MOE_TILE * MOE_TILE
    pad_end = _prefix_sum(padded)
    pad_start = pad_end - padded
    onehot = flat_e[:, None] == jnp.arange(N_EXPERTS, dtype=jnp.int32)[None, :]
    dest = (rank + jnp.sum(jnp.where(onehot, pad_start[None, :], 0), axis=-1)).astype(jnp.int32)
    n_blocks = n_slots // MOE_TILE
    block_first = jnp.arange(n_blocks, dtype=jnp.int32) * MOE_TILE
    block_expert = jnp.minimum(
        jnp.sum((pad_end[None, :] <= block_first[:, None]).astype(jnp.int32), axis=-1),
        N_EXPERTS - 1).astype(jnp.int32)
    n_used = (pad_end[-1:] // MOE_TILE).astype(jnp.int32)
    n_empty = n_slots - n * TOP_K
    gap_end = _prefix_sum(padded - counts)
    j = jnp.arange(n_empty, dtype=jnp.int32)
    owner = jnp.sum((j[:, None] >= gap_end[None, :]).astype(jnp.int32), axis=-1)
    first_empty = jnp.concatenate([pad_start + counts, pad_end[-1:]])
    gap_start = jnp.concatenate([jnp.zeros((1,), gap_end.dtype), gap_end])
    pick = owner[:, None] == jnp.arange(N_EXPERTS + 1, dtype=jnp.int32)[None, :]
    empty_slots = (j + jnp.sum(jnp.where(pick, (first_empty - gap_start)[None, :], 0), axis=-1))
    return dest, block_expert, n_used, empty_slots.astype(jnp.int32)


def _moe_ffn(u, route, counts, w1, w3, w2):
    n = route.shape[1]
    n_slots = (n * TOP_K // MOE_TILE + N_EXPERTS) * MOE_TILE
    dest, block_expert, n_used, empty_slots = _slot_tables(route, counts, n_slots)
    xs = _dispatch(dest, empty_slots, u, n_slots)
    y = _experts(block_expert, n_used, xs, w1, w3, w2)
    return dest, y


def kernel(x, meta, rel_bias, norm_mix, w_in, hg_lb_logits, hg_norm_w, da_lambda, da_subln_w, w_out, norm_ffn, dense_w1, dense_w3, dense_w2, moe_router, moe_w1, moe_w3, moe_w2, final_norm):
    batch, seq, d = x.shape
    length = LEAD + seq
    h = jnp.concatenate([
        jnp.zeros((batch, LEAD - N_META, d), x.dtype),
        jnp.broadcast_to(meta[None].astype(x.dtype), (batch, N_META, d)),
        x], axis=1).reshape(batch * length, d)

    toe = _attn_bias_tables(rel_bias)
    lb_cum = _prefix_sum(jax.nn.softmax(hg_lb_logits.astype(F32), axis=0))
    lb_all = jnp.clip(lb_cum - lb_cum[0:1], 0.0, LB_MAX)
    log_lb = jnp.log(lb_all)
    log_1m_lb = jnp.log1p(-lb_all)

    assert DEPTH % 2 == 0
    dense_f32 = (dense_w1, dense_w3, dense_w2)
    stacked_rows = lambda w: w.reshape(1, w.shape[0] * w.shape[1], w.shape[2])

    unsummed = None
    for l in range(DEPTH):
        if l == 0:
            jobs = [(w, 0) for w in dense_f32] + [(stacked_rows(w_out), 0), (stacked_rows(w_in), 0)]
            hg, da, casts = _mix_in(h, norm_mix[l][None], w_in[0:1].astype(BF16), 0, jobs)
            dense_b = [w[None] for w in casts[:3]]
            w_out_b = casts[3].reshape(w_out.shape)
            w_in_b = casts[4].reshape(w_in.shape)
        elif unsummed is None:
            jobs = [(w, (l + 1) // 2) for w in dense_f32] if l + 1 < DEPTH else []
            hg, da, casts = _mix_in(h, norm_mix[l][None], w_in_b, l, jobs)
            if jobs:
                dense_b = [w[None] for w in casts]
        else:
            h, hg, da = _mix_in_combine(*unsummed, norm_mix[l][None], w_in_b, l)
            unsummed = None
        lam_init = 0.8 - 0.6 * math.exp(-0.3 * l)
        lv = da_lambda[l].astype(F32)
        lam = jnp.exp(jnp.sum(lv[0] * lv[1])) - jnp.exp(jnp.sum(lv[2] * lv[3])) + lam_init
        cst = jnp.zeros((SUBLANES, LANES), F32).at[0].set(lam).at[1].set(1.0 - lam_init)
        o_hg = _hgrn(hg, log_lb[l][None], log_1m_lb[l][None], hg_norm_w[l][None], batch, length)
        o_da = _attn(da, toe, cst, da_subln_w[l][None], batch, length)
        i = l // 2
        if l % 2 == 0:
            h, moe_b = _dense_layer(h, o_hg, o_da, w_out_b, l, norm_ffn[l][None], *dense_b, 0,
                                    (moe_w1, moe_w3, moe_w2), i)
        else:
            router = jnp.zeros((d, LANES), F32).at[:, :N_EXPERTS].set(moe_router[i].astype(F32))
            r_hi = router.astype(BF16)
            r_lo = (router - r_hi.astype(F32)).astype(BF16)
            router = jnp.concatenate([r_hi, r_hi, r_lo], axis=0)
            hn, u, route, gate, counts = _out_proj_router(h, o_hg, o_da, w_out_b, l,
                                                          norm_ffn[l][None], router)
            dest, y = _moe_ffn(u, route, counts, *moe_b)
            if l + 1 < DEPTH:
                unsummed = (dest, hn, gate, y)
            else:
                return _combine_final(dest, hn, gate, y, final_norm[None], batch, length, seq)
```

```python
import functools
import math

import jax
import jax.numpy as jnp
import numpy as np
from jax import lax
from jax.experimental import pallas as pl
from jax.experimental.pallas import tpu as pltpu

D_MODEL = 1024
DEPTH = 4
N_META = 16
LEAD = 128
HG_WIDTH = 512
HG_HEADS = 4
HG_D = 128
HG_CHUNK = 64
DA_HEADS = 4
DA_DQK = 64
DA_DV = 128
Q_BLOCK = 128
KEY_TILE = 8 * Q_BLOCK
Q_GROUP = 2
LOG2E = math.log2(math.e)
Q_SCALE = DA_DQK ** -0.5 * LOG2E
REL_BUCKETS = 32
REL_MAX_DIST = 128
N_EXPERTS = 8
TOP_K = 2
D_FF_EXPERT = 3584
EPS = 1e-6
NEG = -1e30
LB_MAX = 0.999
HG_COLS = 4 * HG_WIDTH
DA_COLS = 3 * DA_HEADS * DA_DV
W_IN_COLS = HG_COLS + DA_COLS

LANES = 128
SUBLANES = 8
VMEM_LIMIT = 56 * 1024 * 1024

ROW_TILE = 256
MIX_TILE = 512
HG_TILE = 128
HG_BATCH = 2
MOE_TILE = 256
FF_CHUNK = 1792

F32 = jnp.float32
BF16 = jnp.bfloat16


def _params(*sem):
    return pltpu.CompilerParams(dimension_semantics=sem, vmem_limit_bytes=VMEM_LIMIT)


def _dot(a, b):
    return jnp.dot(a, b, preferred_element_type=F32)


def _dot_nt(a, b):
    return lax.dot_general(a, b, (((1,), (1,)), ((), ())), preferred_element_type=F32)


def _dot_tn(a, b):
    return lax.dot_general(a, b, (((0,), (0,)), ((), ())), preferred_element_type=F32)


def _rms(x, gain):
    return x * lax.rsqrt(jnp.mean(x * x, axis=-1, keepdims=True) + EPS) * gain


TOKEN_ROWS = D_MODEL // LANES


def _store_token_major(ref, x):
    t = x.shape[0]
    for s in range(TOKEN_ROWS):
        ref[pl.ds(s, t, stride=TOKEN_ROWS), :] = x[:, s * LANES:(s + 1) * LANES]


def _load_token_major(ref, t):
    return jnp.concatenate(
        [ref[pl.ds(s, t, stride=TOKEN_ROWS), :] for s in range(TOKEN_ROWS)], axis=1)


def _silu(x):
    return x * (0.5 * jnp.tanh(0.5 * x) + 0.5)


BF16_SUBLANES = 16


def _cast_jobs(jobs, grid_steps):
    inputs, in_specs, out_specs, out_shapes = [], [], [], []
    for w, index in jobs:
        _, rows, cols = w.shape
        steps = max(s for s in range(1, grid_steps + 1)
                    if rows % s == 0 and (rows // s) % BF16_SUBLANES == 0)
        slab_rows = rows // steps
        in_specs.append(pl.BlockSpec((None, slab_rows, cols),
                                     lambda i, index=index, steps=steps: (index, jnp.minimum(i, steps - 1), 0)))
        out_specs.append(pl.BlockSpec((slab_rows, cols),
                                      lambda i, steps=steps: (jnp.minimum(i, steps - 1), 0)))
        out_shapes.append(jax.ShapeDtypeStruct((rows, cols), BF16))
        inputs.append(w)
    return inputs, in_specs, out_specs, out_shapes


def _run_cast_jobs(src_refs, dst_refs):
    for src_ref, dst_ref in zip(src_refs, dst_refs):
        dst_ref[...] = src_ref[...].astype(BF16)


def _mix_in_kernel(x_ref, g_ref, w_ref, *rest):
    n_cast = (len(rest) - 2) // 2
    hg_ref, da_ref = rest[n_cast:n_cast + 2]
    u = _rms(x_ref[...], g_ref[...]).astype(BF16)
    hg_ref[...] = _dot(u, w_ref[:, :HG_COLS])
    n_q = DA_HEADS * 2 * DA_DQK
    da_ref[:, :n_q] = (_dot(u, w_ref[:, HG_COLS:HG_COLS + n_q]) * Q_SCALE).astype(BF16)
    da_ref[:, n_q:] = _dot(u, w_ref[:, HG_COLS + n_q:]).astype(BF16)
    _run_cast_jobs(rest[:n_cast], rest[n_cast + 2:])


def _mix_in(h, gain, w, layer, to_cast=()):
    n = h.shape[0]
    steps = n // MIX_TILE
    cast_in, cast_in_specs, cast_out_specs, cast_out_shapes = _cast_jobs(to_cast, steps)
    out = pl.pallas_call(
        _mix_in_kernel,
        grid=(steps,),
        in_specs=[
            pl.BlockSpec((MIX_TILE, D_MODEL), lambda i: (i, 0)),
            pl.BlockSpec((1, D_MODEL), lambda i: (0, 0)),
            pl.BlockSpec((None, D_MODEL, W_IN_COLS), lambda i: (layer, 0, 0),
                         pipeline_mode=pl.Buffered(1)),
        ] + cast_in_specs,
        out_specs=[
            pl.BlockSpec((MIX_TILE, HG_COLS), lambda i: (i, 0)),
            pl.BlockSpec((MIX_TILE, DA_COLS), lambda i: (i, 0)),
        ] + cast_out_specs,
        out_shape=[
            jax.ShapeDtypeStruct((n, HG_COLS), F32),
            jax.ShapeDtypeStruct((n, DA_COLS), BF16),
        ] + cast_out_shapes,
        compiler_params=_params("arbitrary"),
        name="mix_in",
    )(h, gain, w, *cast_in)
    return out[0], out[1], list(out[2:])


def _mix_in_combine_kernel(dest_ref, hn_ref, gate_ref, y_ref, g_ref, w_ref,
                           h_ref, hg_ref, da_ref, buf_ref, sem):
    step = pl.program_id(0)
    last = pl.num_programs(0) - 1
    slot = step % 2
    tile_rows = MIX_TILE * TOKEN_ROWS
    n_tokens = dest_ref.shape[0] // TOP_K

    def start_gathers(s, to_slot, rows):
        base = s * MIX_TILE
        slots = [dest_ref[k * n_tokens + base + r] for r in rows for k in range(TOP_K)]
        for j, src in enumerate(slots):
            r, k = rows[j // TOP_K], j % TOP_K
            _token_copy(y_ref, src, buf_ref.at[to_slot, k], r, sem.at[to_slot]).start(priority=j % 2)

    def wait_gathers(of_slot):
        for k in range(TOP_K):
            pltpu.make_async_copy(y_ref.at[pl.ds(0, tile_rows), :], buf_ref.at[of_slot, k],
                                  sem.at[of_slot]).wait()

    @pl.when(step == 0)
    def _():
        def first(g, carry):
            start_gathers(0, 0, [g * ISSUE_TOKENS + r for r in range(ISSUE_TOKENS)])
            return carry

        lax.fori_loop(0, MIX_TILE // ISSUE_TOKENS, first, 0)

    wait_gathers(slot)
    gate = gate_ref[...]
    h = (hn_ref[...] + gate[:, 0:1] * _load_token_major(buf_ref.at[slot, 0], MIX_TILE)
         + gate[:, 1:2] * _load_token_major(buf_ref.at[slot, 1], MIX_TILE))
    h_ref[...] = h
    u = _rms(h, g_ref[...]).astype(BF16)

    nxt = jnp.minimum(step + 1, last)
    n_q = DA_HEADS * 2 * DA_DQK
    groups = [(c, c + HG_WIDTH) for c in range(0, HG_COLS, HG_WIDTH)]
    groups += [(HG_COLS, HG_COLS + n_q), (HG_COLS + n_q, W_IN_COLS)]
    per_group = -(-MIX_TILE // len(groups))
    for gi, (c0, c1) in enumerate(groups):
        proj = _dot(u, w_ref[:, c0:c1])
        if c1 <= HG_COLS:
            hg_ref[:, c0:c1] = proj
        elif c0 == HG_COLS:
            da_ref[:, :n_q] = (proj * Q_SCALE).astype(BF16)
        else:
            da_ref[:, n_q:] = proj.astype(BF16)
        rows = list(range(gi * per_group, min((gi + 1) * per_group, MIX_TILE)))
        for r0 in range(0, len(rows), ISSUE_TOKENS):
            start_gathers(nxt, 1 - slot, rows[r0:r0 + ISSUE_TOKENS])

    @pl.when(step == last)
    def _():
        wait_gathers(1 - slot)


def _mix_in_combine(dest, hn, gate, y, gain, w, layer):
    n = hn.shape[0]
    row = lambda i, d: (i, 0)
    return pl.pallas_call(
        _mix_in_combine_kernel,
        grid_spec=pltpu.PrefetchScalarGridSpec(
            num_scalar_prefetch=1,
            grid=(n // MIX_TILE,),
            in_specs=[
                pl.BlockSpec((MIX_TILE, D_MODEL), row),
                pl.BlockSpec((MIX_TILE, N_EXPERTS), row),
                pl.BlockSpec(memory_space=pl.ANY),
                pl.BlockSpec((1, D_MODEL), lambda i, d: (0, 0)),
                pl.BlockSpec((None, D_MODEL, W_IN_COLS), lambda i, d: (layer, 0, 0),
                             pipeline_mode=pl.Buffered(1)),
            ],
            out_specs=[
                pl.BlockSpec((MIX_TILE, D_MODEL), row),
                pl.BlockSpec((MIX_TILE, HG_COLS), row),
                pl.BlockSpec((MIX_TILE, DA_COLS), row),
            ],
            scratch_shapes=[pltpu.VMEM((2, TOP_K, MIX_TILE * TOKEN_ROWS, LANES), F32),
                            pltpu.SemaphoreType.DMA((2,))],
        ),
        out_shape=[
            jax.ShapeDtypeStruct((n, D_MODEL), F32),
            jax.ShapeDtypeStruct((n, HG_COLS), F32),
            jax.ShapeDtypeStruct((n, DA_COLS), BF16),
        ],
        compiler_params=_params("arbitrary"),
        name="mix_in_combine",
    )(dest, hn, gate, y, gain, w)


HG_LEVELS = (32, 16, 8, 4, 2, 1)
N_SUMS = len(HG_LEVELS) + 2


def _hgrn_consts():
    c = HG_CHUNK
    t = np.arange(c)[:, None]
    j = np.arange(c)[None, :]
    sums = np.zeros((N_SUMS, c, c), np.float32)
    masks = np.zeros((len(HG_LEVELS) + 1, c, c), np.float32)
    sums[0] = j <= t
    masks[0] = np.eye(c)
    for li, w in enumerate(HG_LEVELS, start=1):
        ref = (t // (2 * w)) * (2 * w) + w
        sums[li] = np.where(t >= ref, (j > ref) & (j <= t), (j > t) & (j <= ref))
        masks[li] = (t // (2 * w) == j // (2 * w)) & (t % (2 * w) >= w) & (j % (2 * w) < w)
    sums[N_SUMS - 1] = j > t
    sums = sums.reshape(N_SUMS * c, c)
    return np.concatenate([sums, sums], axis=1), masks


_HG_SUMS, _HG_MASKS = _hgrn_consts()


def _hgrn_kernel(hg_ref, loga_ref, log1m_ref, nw_ref, sums_ref, masks_ref, o_ref, state_ref):
    c_idx = pl.program_id(1)

    @pl.when(c_idx == 0)
    def _():
        state_ref[...] = jnp.zeros_like(state_ref)

    C = HG_CHUNK
    W = HG_WIDTH
    per_seq = HG_TILE // C
    n_chunks = HG_BATCH * per_seq
    rows_all = HG_BATCH * HG_TILE
    sums = sums_ref[...]
    nw = nw_ref[...]
    row_idx = c_idx * HG_TILE + lax.broadcasted_iota(jnp.int32, (rows_all, 1), 0) % HG_TILE
    valid = row_idx >= (LEAD - N_META)
    step = lax.broadcasted_iota(jnp.int32, (C, 1), 0)
    head_cols = [slice(hd * HG_D, (hd + 1) * HG_D) for hd in range(HG_HEADS)]

    def columns(c0, c1):
        return hg_ref[:, :, c0:c1].reshape(rows_all, c1 - c0)

    f = columns(W, 2 * W)
    qf = _silu(columns(0, W))
    ls = jnp.minimum(f, 0.0) - jnp.log(1.0 + jnp.exp(-jnp.abs(f)))
    cc = log1m_ref[...] + ls
    loga = loga_ref[...]
    lf = jnp.maximum(loga, cc) + jnp.log(1.0 + jnp.exp(-jnp.abs(loga - cc)))
    kk = jnp.exp(cc - f)
    lf = jnp.where(valid, lf, 0.0)
    kk = jnp.where(valid, kk, 0.0)
    lf2 = lf * LOG2E
    lf_hi = lf2.astype(BF16)
    lf_lo = (lf2 - lf_hi.astype(F32)).astype(BF16)
    vb = columns(2 * W, 3 * W).astype(BF16)
    gate = _silu(columns(3 * W, 4 * W))

    def side_by_side(x):
        return jnp.concatenate([x[ch * C:(ch + 1) * C] for ch in range(n_chunks)], axis=1)

    qf_w, kk_w = side_by_side(qf), side_by_side(kk)
    qb_w, kb_w = qf_w.astype(BF16), kk_w.astype(BF16)
    e = jnp.exp2(_dot(sums, jnp.concatenate([side_by_side(lf_hi), side_by_side(lf_lo)], axis=0)))
    e_b = e[0:C]
    decay_end = e_b[C - 1:C, :]
    q_in = (qf_w * e_b).astype(BF16)
    k_out = (kk_w * e[(N_SUMS - 1) * C:N_SUMS * C]).astype(BF16)
    z = [(jnp.where((step & w) != 0, qf_w, kk_w) * e[li * C:(li + 1) * C]).astype(BF16)
         for li, w in enumerate(HG_LEVELS, start=1)]
    unit_cols = [[slice(ch * W + hd * HG_D, ch * W + (hd + 1) * HG_D) for hd in range(HG_HEADS)]
                 for ch in range(n_chunks)]
    scores = []
    for ch in range(n_chunks):
        scores.append([])
        for cols in unit_cols[ch]:
            s = masks_ref[0] * _dot_nt(qb_w[:, cols], kb_w[:, cols])
            for li in range(1, len(HG_LEVELS) + 1):
                zl = z[li - 1][:, cols]
                s += masks_ref[li] * _dot_nt(zl, zl)
            scores[ch].append(s.astype(BF16))

    for local in range(per_seq):
        for seq in range(HG_BATCH):
            ch = seq * per_seq + local
            rows = slice(ch * C, (ch + 1) * C)
            for hd, cols in enumerate(head_cols):
                wide = unit_cols[ch][hd]
                st = state_ref[seq * HG_HEADS + hd]
                v_h = vb[rows, cols]
                o = _dot_nt(q_in[:, wide], st.astype(BF16)) + _dot(scores[ch][hd], v_h)
                state_ref[seq * HG_HEADS + hd] = (st * decay_end[:, wide]
                                                  + _dot_tn(v_h, k_out[:, wide]))
                o = _rms(o, nw) * gate[rows, cols]
                o_ref[seq, local * C:(local + 1) * C, cols] = o.astype(o_ref.dtype)


def _hgrn(hg, loga, log1m, norm_w, batch, length):
    hg3 = hg.reshape(batch, length, HG_COLS)
    out = pl.pallas_call(
        _hgrn_kernel,
        grid=(batch // HG_BATCH, length // HG_TILE),
        in_specs=[
            pl.BlockSpec((HG_BATCH, HG_TILE, HG_COLS), lambda b, c: (b, c, 0)),
            pl.BlockSpec((1, HG_WIDTH), lambda b, c: (0, 0)),
            pl.BlockSpec((1, HG_WIDTH), lambda b, c: (0, 0)),
            pl.BlockSpec((1, HG_D), lambda b, c: (0, 0)),
            pl.BlockSpec(_HG_SUMS.shape, lambda b, c: (0, 0)),
            pl.BlockSpec(_HG_MASKS.shape, lambda b, c: (0, 0, 0)),
        ],
        out_specs=pl.BlockSpec((HG_BATCH, HG_TILE, HG_WIDTH), lambda b, c: (b, c, 0)),
        out_shape=jax.ShapeDtypeStruct((batch, length, HG_WIDTH), BF16),
        scratch_shapes=[pltpu.VMEM((HG_BATCH * HG_HEADS, HG_D, HG_D), F32)],
        compiler_params=_params("parallel", "arbitrary"),
        name="hgrn2",
    )(hg3, loga, log1m, norm_w, jnp.asarray(_HG_SUMS, BF16), jnp.asarray(_HG_MASKS, F32))
    return out.reshape(batch * length, HG_WIDTH)


def _attn_kernel(q_ref, k_ref, v_ref, toe_ref, cst_ref, w_ref, o_ref, s_ref, *, n_blocks):
    lam = cst_ref[0:1, 0:1]
    post = cst_ref[1:2, :]
    lane = lax.broadcasted_iota(jnp.int32, (Q_BLOCK, Q_BLOCK), 1)
    first_half = lane < DA_DQK
    key_ok0 = lane >= (LEAD - N_META)
    inert_bias = jnp.where(key_ok0, 0.0, NEG)

    def near_bias(kind, kb):
        bias = toe_ref[kind]
        if kb == 0:
            bias = jnp.where(key_ok0, bias, NEG)
        return bias

    def slabs(x):
        return [x[:, c:c + Q_BLOCK] for c in range(0, x.shape[1], Q_BLOCK)]

    map_rows = 2 * Q_BLOCK
    groups = [tuple(range(i, min(i + Q_GROUP, n_blocks))) for i in range(0, n_blocks, Q_GROUP)]

    def both_maps(bias):
        return jnp.concatenate([bias, bias], axis=0)

    def stacked_q(blocks):
        parts = []
        for i in blocks:
            qi = q_ref[i * Q_BLOCK:(i + 1) * Q_BLOCK, :]
            zero = jnp.zeros_like(qi)
            parts += [jnp.where(first_half, qi, zero), jnp.where(first_half, zero, qi)]
        return jnp.concatenate(parts, axis=0)

    def group_tiles(blocks):
        first, last = blocks[0], blocks[-1]
        tiles = []
        far_end = max(first - 1, 0)
        if far_end >= 1:
            tiles.append((0, Q_BLOCK, jnp.concatenate([inert_bias] * (2 * len(blocks)), axis=0), 0))
        kb = 1
        while kb < far_end:
            width = KEY_TILE
            while kb + width // Q_BLOCK > far_end:
                width //= 2
            tiles.append((kb * Q_BLOCK, width, None, 0))
            kb += width // Q_BLOCK
        for kb in range(far_end, last + 1):
            biases, first_row = [], None
            for r, qb in enumerate(blocks):
                if kb > qb:
                    continue
                if first_row is None:
                    first_row = r * map_rows
                if kb == qb:
                    bias = near_bias(0, kb)
                elif kb == qb - 1:
                    bias = near_bias(1, kb)
                else:
                    bias = inert_bias if kb == 0 else jnp.zeros_like(inert_bias)
                biases.append(both_maps(bias))
            tiles.append((kb * Q_BLOCK, Q_BLOCK, jnp.concatenate(biases, axis=0), first_row))
        return tiles

    def merge(acc, x, first_row, op):
        if acc is None:
            return x
        if first_row == 0:
            return op(acc, x)
        return jnp.concatenate([acc[:first_row], op(acc[first_row:], x)], axis=0)

    def sweep_scores(g):
        blocks = groups[g]
        rows = len(blocks) * map_rows
        q2 = stacked_q(blocks)
        m_acc = None
        for start, width, bias, first_row in group_tiles(blocks):
            s = _dot_nt(q2[first_row:], k_ref[start:start + width, :])
            if bias is not None:
                s = s + bias
            s_ref[g % 2, first_row:rows, start:start + width] = s
            for slab in slabs(s):
                m_acc = merge(m_acc, slab, first_row, jnp.maximum)
        return m_acc.max(axis=-1, keepdims=True)

    row_max = sweep_scores(0)
    for g, blocks in enumerate(groups):
        rows = len(blocks) * map_rows
        m = row_max
        if g + 1 < len(groups):
            row_max = sweep_scores(g + 1)
        l_acc = o_acc = None
        for start, width, _, first_row in group_tiles(blocks):
            p = jnp.exp2(s_ref[g % 2, first_row:rows, start:start + width] - m[first_row:])
            for slab in slabs(p):
                l_acc = merge(l_acc, slab, first_row, jnp.add)
            o_acc = merge(o_acc, _dot(p.astype(BF16), v_ref[start:start + width, :]), first_row, jnp.add)
        o2 = o_acc * (1.0 / l_acc.sum(axis=-1, keepdims=True))
        for r, i in enumerate(blocks):
            o = o2[r * map_rows:r * map_rows + Q_BLOCK] - lam * o2[r * map_rows + Q_BLOCK:(r + 1) * map_rows]
            o = _rms(o, w_ref[...]) * post
            o_ref[i * Q_BLOCK:(i + 1) * Q_BLOCK, :] = o.astype(o_ref.dtype)


def _attn(da, toe, cst, subln_w, batch, length):
    da3 = da.reshape(batch, length, DA_COLS)
    hw = DA_HEADS
    out = pl.pallas_call(
        functools.partial(_attn_kernel, n_blocks=length // Q_BLOCK),
        grid=(batch, DA_HEADS),
        in_specs=[
            pl.BlockSpec((None, length, DA_DV), lambda b, h: (b, 0, h)),
            pl.BlockSpec((None, length, DA_DV), lambda b, h: (b, 0, hw + h)),
            pl.BlockSpec((None, length, DA_DV), lambda b, h: (b, 0, 2 * hw + h)),
            pl.BlockSpec((None, 2, Q_BLOCK, Q_BLOCK), lambda b, h: (h, 0, 0, 0)),
            pl.BlockSpec((SUBLANES, LANES), lambda b, h: (0, 0)),
            pl.BlockSpec((1, DA_DV), lambda b, h: (0, 0)),
        ],
        out_specs=pl.BlockSpec((None, length, DA_DV), lambda b, h: (b, 0, h)),
        out_shape=jax.ShapeDtypeStruct((batch, length, DA_HEADS * DA_DV), BF16),
        scratch_shapes=[pltpu.VMEM((2, Q_GROUP * 2 * Q_BLOCK, length), F32)],
        compiler_params=_params("parallel", "parallel"),
        name="diff_attn",
    )(da3, da3, da3, toe, cst, subln_w)
    return out.reshape(batch * length, DA_HEADS * DA_DV)


def _t5_bucket(dist):
    n = jnp.maximum(dist, 0)
    max_exact = REL_BUCKETS // 2
    nf = jnp.maximum(n, max_exact).astype(F32)
    large = max_exact + (jnp.log(nf / max_exact) / math.log(REL_MAX_DIST / max_exact)
                         * (REL_BUCKETS - max_exact)).astype(jnp.int32)
    large = jnp.minimum(large, REL_BUCKETS - 1)
    return jnp.where(n < max_exact, n, large)


def _attn_bias_tables(rel_bias):
    tab = rel_bias.astype(F32)
    qi = jnp.arange(Q_BLOCK, dtype=jnp.int32)[:, None]
    ki = jnp.arange(Q_BLOCK, dtype=jnp.int32)[None, :]

    def lookup(bucket):
        onehot = bucket[None, :, :, None] == jnp.arange(REL_BUCKETS, dtype=jnp.int32)
        return jnp.sum(jnp.where(onehot, tab.T[:, None, None, :], 0.0), axis=-1)

    far = tab[REL_BUCKETS - 1][:, None, None]
    diag = jnp.where((ki <= qi)[None], (lookup(_t5_bucket(qi - ki)) - far) * LOG2E, NEG)
    prev = (lookup(_t5_bucket(qi - ki + Q_BLOCK)) - far) * LOG2E
    return jnp.stack([diag, prev], axis=1)


def _out_proj_router_kernel(h_ref, ohg_ref, oda_ref, wo_ref, g_ref, router_ref, before_ref,
                            hn_ref, u_ref, route_ref, gate_ref, count_ref, seen_ref):
    @pl.when(pl.program_id(0) == 0)
    def _():
        seen_ref[...] = jnp.zeros_like(seen_ref)

    hn = (h_ref[...] + _dot(ohg_ref[...], wo_ref[:HG_WIDTH, :])
          + _dot(oda_ref[...], wo_ref[HG_WIDTH:, :]))
    hn_ref[...] = hn
    u = _rms(hn, g_ref[...])
    _store_token_major(u_ref, u)
    u_hi = u.astype(BF16)
    u_lo = (u - u_hi.astype(F32)).astype(BF16)
    logits = _dot(jnp.concatenate([u_hi, u_lo, u_hi], axis=1), router_ref[...])
    logits = logits.T[:N_EXPERTS]
    expert = lax.broadcasted_iota(jnp.int32, logits.shape, 0)

    def top(x):
        best = x.max(axis=0, keepdims=True)
        return best, jnp.where(x == best, expert, N_EXPERTS).min(axis=0, keepdims=True)

    l1, e1 = top(logits)
    l2, e2 = top(jnp.where(expert == e1, -jnp.inf, logits))
    w2 = jnp.exp(l2 - l1)
    g1 = 1.0 / (1.0 + w2)
    g2 = w2 / (1.0 + w2)

    pick1 = (expert == e1).astype(F32)
    pick2 = (expert == e2).astype(F32)
    picked = pick1 + pick2
    ahead = seen_ref[:, 0:1] + _dot(picked.astype(BF16), before_ref[...])
    rank1 = jnp.sum(pick1 * ahead, axis=0, keepdims=True).astype(jnp.int32)
    rank2 = jnp.sum(pick2 * ahead, axis=0, keepdims=True).astype(jnp.int32)
    seen = seen_ref[...] + jnp.sum(picked, axis=1, keepdims=True)
    seen_ref[...] = seen

    route_ref[...] = jnp.where(expert == 0, e1, jnp.where(expert == 1, e2, jnp.where(
        expert == 2, rank1, jnp.where(expert == 3, rank2, 0))))
    gate_ref[...] = jnp.where(expert == 0, g1, jnp.where(expert == 1, g2, 0.0)).T
    count_ref[...] = seen.astype(jnp.int32)


def _out_proj_router(h, o_hg, o_da, wo, layer, gain, router):
    n = h.shape[0]
    row = lambda i: (i, 0)
    full = lambda i: (0, 0)
    assert N_EXPERTS >= 2 * TOP_K
    lanes_of = lambda i: (0, i)
    before = jnp.asarray(np.triu(np.ones((ROW_TILE, ROW_TILE), np.float32), 1), BF16)
    return pl.pallas_call(
        _out_proj_router_kernel,
        grid=(n // ROW_TILE,),
        in_specs=[
            pl.BlockSpec((ROW_TILE, D_MODEL), row),
            pl.BlockSpec((ROW_TILE, HG_WIDTH), row),
            pl.BlockSpec((ROW_TILE, DA_HEADS * DA_DV), row),
            pl.BlockSpec((None,) + wo.shape[1:], lambda i: (layer, 0, 0)),
            pl.BlockSpec((1, D_MODEL), full),
            pl.BlockSpec(router.shape, full),
            pl.BlockSpec(before.shape, full),
        ],
        out_specs=[
            pl.BlockSpec((ROW_TILE, D_MODEL), row),
            pl.BlockSpec((ROW_TILE * TOKEN_ROWS, LANES), row),
            pl.BlockSpec((N_EXPERTS, ROW_TILE), lanes_of),
            pl.BlockSpec((ROW_TILE, N_EXPERTS), row),
            pl.BlockSpec((N_EXPERTS, LANES), full),
        ],
        out_shape=[
            jax.ShapeDtypeStruct((n, D_MODEL), F32),
            jax.ShapeDtypeStruct((n * TOKEN_ROWS, LANES), F32),
            jax.ShapeDtypeStruct((N_EXPERTS, n), jnp.int32),
            jax.ShapeDtypeStruct((n, N_EXPERTS), F32),
            jax.ShapeDtypeStruct((N_EXPERTS, LANES), jnp.int32),
        ],
        scratch_shapes=[pltpu.VMEM((N_EXPERTS, LANES), F32)],
        compiler_params=_params("arbitrary"),
        name="out_proj_router",
    )(h, o_hg, o_da, wo, gain, router, before)


def _dense_layer_kernel(h_ref, ohg_ref, oda_ref, wo_ref, g_ref, w1_ref, w3_ref, w2_ref, *rest):
    n_cast = (len(rest) - 1) // 2
    o_ref = rest[n_cast]
    hn = (h_ref[...] + _dot(ohg_ref[...], wo_ref[:HG_WIDTH, :])
          + _dot(oda_ref[...], wo_ref[HG_WIDTH:, :]))
    u = _rms(hn, g_ref[...]).astype(BF16)
    a = _dot(u, w1_ref[...])
    act = (_silu(a) * _dot(u, w3_ref[...])).astype(BF16)
    o_ref[...] = hn + _dot(act, w2_ref[...])
    _run_cast_jobs(rest[:n_cast], rest[n_cast + 1:])


def _dense_layer(h, o_hg, o_da, wo, layer, gain, w1, w3, w2, idx, to_cast, cast_idx):
    n = h.shape[0]
    steps = n // ROW_TILE
    row = lambda i: (i, 0)

    def resident(w, index):
        return pl.BlockSpec((None,) + w.shape[1:], lambda i: (index, 0, 0),
                            pipeline_mode=pl.Buffered(1))

    jobs = [(w.reshape(w.shape[0], w.shape[1] * w.shape[2], w.shape[3]), cast_idx) for w in to_cast]
    cast_in, cast_in_specs, cast_out_specs, cast_out_shape = _cast_jobs(jobs, steps)

    out = pl.pallas_call(
        _dense_layer_kernel,
        grid=(steps,),
        in_specs=[
            pl.BlockSpec((ROW_TILE, D_MODEL), row),
            pl.BlockSpec((ROW_TILE, HG_WIDTH), row),
            pl.BlockSpec((ROW_TILE, DA_HEADS * DA_DV), row),
            resident(wo, layer),
            pl.BlockSpec((1, D_MODEL), lambda i: (0, 0)),
            resident(w1, idx),
            resident(w3, idx),
            resident(w2, idx),
        ] + cast_in_specs,
        out_specs=[pl.BlockSpec((ROW_TILE, D_MODEL), row)] + cast_out_specs,
        out_shape=[jax.ShapeDtypeStruct((n, D_MODEL), F32)] + cast_out_shape,
        compiler_params=_params("arbitrary"),
        name="dense_layer",
    )(h, o_hg, o_da, wo, gain, w1, w3, w2, *cast_in)
    return out[0], [b.reshape(w.shape[1:]) for b, w in zip(out[1:], to_cast)]


DISPATCH_TILE = 2176
ISSUE_TOKENS = 8


def _token_copy(src_ref, src_tok, dst_ref, dst_tok, sem):
    src = pl.multiple_of(src_tok * TOKEN_ROWS, TOKEN_ROWS)
    dst = pl.multiple_of(dst_tok * TOKEN_ROWS, TOKEN_ROWS)
    return pltpu.make_async_copy(src_ref.at[pl.ds(src, TOKEN_ROWS), :],
                                 dst_ref.at[pl.ds(dst, TOKEN_ROWS), :], sem)


def _dispatch_kernel(dest_ref, empty_ref, u_ref, xs_ref, zero_ref, sem):
    step = pl.program_id(0)
    base = step * DISPATCH_TILE
    tile_rows = DISPATCH_TILE * TOKEN_ROWS

    def retire_tile():
        pltpu.make_async_copy(u_ref, xs_ref.at[pl.ds(0, tile_rows), :], sem).wait()

    @pl.when(step == 0)
    def _():
        zero_ref[...] = jnp.zeros_like(zero_ref)

        def clear(g, carry):
            slots = [empty_ref[g * ISSUE_TOKENS * TOP_K + j] for j in range(ISSUE_TOKENS * TOP_K)]
            for j, slot in enumerate(slots):
                _token_copy(zero_ref, 0, xs_ref, slot, sem).start(priority=j % 2)
            return carry

        n_empty = empty_ref.shape[0]
        lax.fori_loop(0, n_empty // (ISSUE_TOKENS * TOP_K), clear, 0)
        empty_rows = n_empty * TOKEN_ROWS
        pltpu.make_async_copy(u_ref.at[pl.ds(0, empty_rows), :], xs_ref.at[pl.ds(0, empty_rows), :],
                              sem).wait()

    n_tokens = dest_ref.shape[0] // TOP_K

    def start(g, carry):
        r0 = g * ISSUE_TOKENS
        slots = [dest_ref[(j % TOP_K) * n_tokens + base + r0 + j // TOP_K]
                 for j in range(ISSUE_TOKENS * TOP_K)]
        for j, slot in enumerate(slots):
            _token_copy(u_ref, r0 + j // TOP_K, xs_ref, slot, sem).start(priority=j % 2)
        return carry

    lax.fori_loop(0, DISPATCH_TILE // ISSUE_TOKENS, start, 0)
    for _ in range(TOP_K):
        retire_tile()


def _dispatch(dest, empty_slots, u_tm, n_slots):
    n = dest.shape[0] // TOP_K
    assert empty_slots.shape[0] <= DISPATCH_TILE and n % DISPATCH_TILE == 0
    return pl.pallas_call(
        _dispatch_kernel,
        grid_spec=pltpu.PrefetchScalarGridSpec(
            num_scalar_prefetch=2,
            grid=(n // DISPATCH_TILE,),
            in_specs=[pl.BlockSpec((DISPATCH_TILE * TOKEN_ROWS, LANES), lambda i, d, e: (i, 0))],
            out_specs=pl.BlockSpec(memory_space=pl.ANY),
            scratch_shapes=[pltpu.VMEM((TOKEN_ROWS, LANES), u_tm.dtype),
                            pltpu.SemaphoreType.DMA(())],
        ),
        out_shape=jax.ShapeDtypeStruct((n_slots * TOKEN_ROWS, LANES), u_tm.dtype),
        compiler_params=_params("arbitrary"),
        name="moe_dispatch",
    )(dest, empty_slots, u_tm)


def _expert_kernel(be_ref, nused_ref, xs_ref, w1_ref, w3_ref, w2_ref, y_ref):
    del be_ref
    blk = pl.program_id(0)

    @pl.when(blk < nused_ref[0])
    def _():
        x = _load_token_major(xs_ref, MOE_TILE).astype(BF16)
        acc = jnp.zeros((MOE_TILE, D_MODEL), F32)
        for c0 in range(0, D_FF_EXPERT, FF_CHUNK):
            a = _dot(x, w1_ref[:, c0:c0 + FF_CHUNK])
            act = (_silu(a) * _dot(x, w3_ref[:, c0:c0 + FF_CHUNK])).astype(BF16)
            acc = acc + _dot(act, w2_ref[c0:c0 + FF_CHUNK, :])
        _store_token_major(y_ref, acc)

    @pl.when(blk >= nused_ref[0])
    def _():
        y_ref[...] = jnp.zeros_like(y_ref)


def _experts(block_expert, n_used, xs, w1, w3, w2):
    n_blocks = xs.shape[0] // (MOE_TILE * TOKEN_ROWS)
    wmap = lambda i, be, nu: (be[i], 0, 0)
    slots = pl.BlockSpec((MOE_TILE * TOKEN_ROWS, LANES), lambda i, be, nu: (i, 0))
    return pl.pallas_call(
        _expert_kernel,
        grid_spec=pltpu.PrefetchScalarGridSpec(
            num_scalar_prefetch=2,
            grid=(n_blocks,),
            in_specs=[
                slots,
                pl.BlockSpec((None, D_MODEL, D_FF_EXPERT), wmap),
                pl.BlockSpec((None, D_MODEL, D_FF_EXPERT), wmap),
                pl.BlockSpec((None, D_FF_EXPERT, D_MODEL), wmap),
            ],
            out_specs=slots,
        ),
        out_shape=jax.ShapeDtypeStruct(xs.shape, F32),
        compiler_params=_params("arbitrary"),
        name="moe_experts",
    )(block_expert, n_used, xs, w1, w3, w2)


def _combine_final_kernel(dest_ref, h_ref, gate_ref, y_ref, g_ref, o_ref, buf_ref, sem):
    step = pl.program_id(0)
    n_tokens = dest_ref.shape[0] // TOP_K

    def issue(s, slot):
        base = s * Q_BLOCK

        def body(g, carry):
            r0 = g * ISSUE_TOKENS
            slots = [dest_ref[(j % TOP_K) * n_tokens + base + r0 + j // TOP_K]
                     for j in range(ISSUE_TOKENS * TOP_K)]
            for j, src in enumerate(slots):
                _token_copy(y_ref, src, buf_ref.at[slot, j % TOP_K], r0 + j // TOP_K,
                            sem.at[slot]).start(priority=j % 2)
            return carry

        lax.fori_loop(0, Q_BLOCK // ISSUE_TOKENS, body, 0)

    @pl.when(step == 0)
    def _():
        issue(0, 0)

    for slot in range(2):
        @pl.when(step % 2 == slot)
        def _():
            @pl.when(step + 1 < pl.num_programs(0))
            def _():
                issue(step + 1, 1 - slot)

            for k in range(TOP_K):
                pltpu.make_async_copy(y_ref.at[pl.ds(0, Q_BLOCK * TOKEN_ROWS), :],
                                      buf_ref.at[slot, k], sem.at[slot]).wait()
            gate = gate_ref[...]
            h = (h_ref[...]
                 + gate[:, 0:1] * _load_token_major(buf_ref.at[slot, 0], Q_BLOCK)
                 + gate[:, 1:2] * _load_token_major(buf_ref.at[slot, 1], Q_BLOCK))
            o_ref[...] = _rms(h, g_ref[...])


def _combine_final(dest, h, gate, y, gain, batch, length, seq):
    n = h.shape[0]
    per_seq = length // Q_BLOCK
    lead_blocks = LEAD // Q_BLOCK
    return pl.pallas_call(
        _combine_final_kernel,
        grid_spec=pltpu.PrefetchScalarGridSpec(
            num_scalar_prefetch=1,
            grid=(n // Q_BLOCK,),
            in_specs=[
                pl.BlockSpec((Q_BLOCK, D_MODEL), lambda i, d: (i, 0)),
                pl.BlockSpec((Q_BLOCK, N_EXPERTS), lambda i, d: (i, 0)),
                pl.BlockSpec(memory_space=pl.ANY),
                pl.BlockSpec((1, D_MODEL), lambda i, d: (0, 0)),
            ],
            out_specs=pl.BlockSpec(
                (None, Q_BLOCK, D_MODEL),
                lambda i, d: (i // per_seq, jnp.maximum(i % per_seq - lead_blocks, 0), 0)),
            scratch_shapes=[pltpu.VMEM((2, TOP_K, Q_BLOCK * TOKEN_ROWS, LANES), F32),
                            pltpu.SemaphoreType.DMA((2,))],
        ),
        out_shape=jax.ShapeDtypeStruct((batch, seq, D_MODEL), F32),
        compiler_params=_params("arbitrary"),
        name="moe_combine_final",
    )(dest, h, gate, y, gain)


def _prefix_sum(x):
    k = x.shape[0]
    keep = (np.arange(k)[None, :] <= np.arange(k)[:, None]).reshape((k, k) + (1,) * (x.ndim - 1))
    return jnp.sum(jnp.where(keep, x[None], jnp.zeros_like(x[None])), axis=1)


def _slot_tables(route, counts, n_slots):
    n = route.shape[1]
    flat_e = route[0:TOP_K].reshape(n * TOP_K)
    rank = route[TOP_K:2 * TOP_K].reshape(n * TOP_K)
    counts = counts[:, 0]
    padded = (counts + MOE_TILE - 1) // MOE_TILE * MOE_TILE
    pad_end = _prefix_sum(padded)
    pad_start = pad_end - padded
    onehot = flat_e[:, None] == jnp.arange(N_EXPERTS, dtype=jnp.int32)[None, :]
    dest = (rank + jnp.sum(jnp.where(onehot, pad_start[None, :], 0), axis=-1)).astype(jnp.int32)
    n_blocks = n_slots // MOE_TILE
    block_first = jnp.arange(n_blocks, dtype=jnp.int32) * MOE_TILE
    block_expert = jnp.minimum(
        jnp.sum((pad_end[None, :] <= block_first[:, None]).astype(jnp.int32), axis=-1),
        N_EXPERTS - 1).astype(jnp.int32)
    n_used = (pad_end[-1:] // MOE_TILE).astype(jnp.int32)
    n_empty = n_slots - n * TOP_K
    gap_end = _prefix_sum(padded - counts)
    j = jnp.arange(n_empty, dtype=jnp.int32)
    owner = jnp.sum((j[:, None] >= gap_end[None, :]).astype(jnp.int32), axis=-1)
    first_empty = jnp.concatenate([pad_start + counts, pad_end[-1:]])
    gap_start = jnp.concatenate([jnp.zeros((1,), gap_end.dtype), gap_end])
    pick = owner[:, None] == jnp.arange(N_EXPERTS + 1, dtype=jnp.int32)[None, :]
    empty_slots = (j + jnp.sum(jnp.where(pick, (first_empty - gap_start)[None, :], 0), axis=-1))
    return dest, block_expert, n_used, empty_slots.astype(jnp.int32)


def _moe_ffn(u, route, counts, w1, w3, w2):
    n = route.shape[1]
    n_slots = (n * TOP_K // MOE_TILE + N_EXPERTS) * MOE_TILE
    dest, block_expert, n_used, empty_slots = _slot_tables(route, counts, n_slots)
    xs = _dispatch(dest, empty_slots, u, n_slots)
    y = _experts(block_expert, n_used, xs, w1, w3, w2)
    return dest, y


def kernel(x, meta, rel_bias, norm_mix, w_in, hg_lb_logits, hg_norm_w, da_lambda, da_subln_w, w_out, norm_ffn, dense_w1, dense_w3, dense_w2, moe_router, moe_w1, moe_w3, moe_w2, final_norm):
    batch, seq, d = x.shape
    length = LEAD + seq
    h = jnp.concatenate([
        jnp.zeros((batch, LEAD - N_META, d), x.dtype),
        jnp.broadcast_to(meta[None].astype(x.dtype), (batch, N_META, d)),
        x], axis=1).reshape(batch * length, d)

    toe = _attn_bias_tables(rel_bias)
    lb_cum = _prefix_sum(jax.nn.softmax(hg_lb_logits.astype(F32), axis=0))
    lb_all = jnp.clip(lb_cum - lb_cum[0:1], 0.0, LB_MAX)
    log_lb = jnp.log(lb_all)
    log_1m_lb = jnp.log1p(-lb_all)

    assert DEPTH % 2 == 0
    dense_f32 = (dense_w1, dense_w3, dense_w2)
    stacked_rows = lambda w: w.reshape(1, w.shape[0] * w.shape[1], w.shape[2])

    unsummed = None
    for l in range(DEPTH):
        if l == 0:
            jobs = [(w, 0) for w in dense_f32] + [(stacked_rows(w_out), 0), (stacked_rows(w_in), 0)]
            hg, da, casts = _mix_in(h, norm_mix[l][None], w_in[0:1].astype(BF16), 0, jobs)
            dense_b = [w[None] for w in casts[:3]]
            w_out_b = casts[3].reshape(w_out.shape)
            w_in_b = casts[4].reshape(w_in.shape)
        elif unsummed is None:
            jobs = [(w, (l + 1) // 2) for w in dense_f32] if l + 1 < DEPTH else []
            hg, da, casts = _mix_in(h, norm_mix[l][None], w_in_b, l, jobs)
            if jobs:
                dense_b = [w[None] for w in casts]
        else:
            h, hg, da = _mix_in_combine(*unsummed, norm_mix[l][None], w_in_b, l)
            unsummed = None
        lam_init = 0.8 - 0.6 * math.exp(-0.3 * l)
        lv = da_lambda[l].astype(F32)
        lam = jnp.exp(jnp.sum(lv[0] * lv[1])) - jnp.exp(jnp.sum(lv[2] * lv[3])) + lam_init
        cst = jnp.zeros((SUBLANES, LANES), F32).at[0].set(lam).at[1].set(1.0 - lam_init)
        o_hg = _hgrn(hg, log_lb[l][None], log_1m_lb[l][None], hg_norm_w[l][None], batch, length)
        o_da = _attn(da, toe, cst, da_subln_w[l][None], batch, length)
        i = l // 2
        if l % 2 == 0:
            h, moe_b = _dense_layer(h, o_hg, o_da, w_out_b, l, norm_ffn[l][None], *dense_b, 0,
                                    (moe_w1, moe_w3, moe_w2), i)
        else:
            router = jnp.zeros((d, LANES), F32).at[:, :N_EXPERTS].set(moe_router[i].astype(F32))
            r_hi = router.astype(BF16)
            r_lo = (router - r_hi.astype(F32)).astype(BF16)
            router = jnp.concatenate([r_hi, r_hi, r_lo], axis=0)
            hn, u, route, gate, counts = _out_proj_router(h, o_hg, o_da, w_out_b, l,
                                                          norm_ffn[l][None], router)
            dest, y = _moe_ffn(u, route, counts, *moe_b)
            if l + 1 < DEPTH:
                unsummed = (dest, hn, gate, y)
            else:
                return _combine_final(dest, hn, gate, y, final_norm[None], batch, length, seq)
```
